```python
import jax, jax.numpy as jnp
from jax import lax
import numpy as np

D_MODEL = 1024
BATCH = 16
SEQ = 256
DEPTH = 1
DEC_BATCH = 2
DEC_SEQ = 1024
PAST_LEN = 256

GRID_W = 64
HEAD_DIM = 64
RW_HEADS = 8
RW_WIDTH = RW_HEADS * HEAD_DIM
NA_HEADS = 8
NA_WIDTH = NA_HEADS * HEAD_DIM
LORA_DECAY = 64
LORA_ICLR = 64
LORA_GATE = 128
NA_ROWS = 8
NA_COLS = 16
NA_QCOLS = 16
NA_KCOLS = 32
Q_BLOCK = 128
FF_HIDDEN = -(-8 * D_MODEL // (3 * 256)) * 256
RW_COLS = 3 * RW_WIDTH + 2 * LORA_DECAY + 2 * LORA_ICLR + LORA_GATE
NA_IN_COLS = 3 * NA_WIDTH
IN_COLS = RW_COLS + NA_IN_COLS + 2 * D_MODEL
RW_SPLITS = (RW_WIDTH, 2 * RW_WIDTH, 3 * RW_WIDTH,
             3 * RW_WIDTH + 2 * LORA_DECAY,
             3 * RW_WIDTH + 2 * LORA_DECAY + 2 * LORA_ICLR)
IN_SPLITS = (RW_COLS, RW_COLS + NA_IN_COLS, RW_COLS + NA_IN_COLS + D_MODEL)
RMS_EPS = 1e-6
GN_EPS = 64e-5
L2_EPS = 1e-12
NEG_INF = -1e30

kernel_name = "hybrid_rwkv7_natten_prefix_step"


def _rmsnorm(x, g):
    xf = x.astype(jnp.float32)
    y = xf * lax.rsqrt(jnp.mean(xf * xf, axis=-1, keepdims=True) + RMS_EPS)
    return (y * g.astype(jnp.float32)).astype(x.dtype)


def _heads(t):
    return t.reshape(t.shape[:-1] + (t.shape[-1] // HEAD_DIM, HEAD_DIM))


def _centred_shift(u, mu):
    prev = jnp.pad(u[:, :-1], ((0, 0), (1, 0), (0, 0)))
    nxt = jnp.pad(u[:, 1:], ((0, 0), (0, 1), (0, 0)))
    return u + mu[0] * (prev - u) + mu[1] * (nxt - u)


def _wkv_step(s, inp):
    r, w, k, v, a, b = inp
    sa = jnp.einsum("bhvk,bhk->bhv", s, a)
    s = s * w[:, :, None, :] + sa[..., None] * b[:, :, None, :] + v[..., None] * k[:, :, None, :]
    return s, jnp.einsum("bhvk,bhk->bhv", s, r)


def _wkv_scan(s0, r, w, k, v, a, b, reverse):
    xs = tuple(jnp.moveaxis(t.astype(jnp.float32), 1, 0) for t in (r, w, k, v, a, b))
    s, ys = lax.scan(_wkv_step, s0.astype(jnp.float32), xs, reverse=reverse)
    return s, jnp.moveaxis(ys, 0, 1)


def _rwkv_branch(u, s0, mu, w0, w_up, a0, a_up, g_up, k_k, k_a, r_k, ln_g, ln_b):
    u = _centred_shift(u, mu)
    r, k, v, wd, ad, gd = jnp.split(u, RW_SPLITS, axis=-1)
    bsz, t_len = r.shape[:2]
    wd = wd.reshape(bsz, t_len, 2, LORA_DECAY)
    ad = ad.reshape(bsz, t_len, 2, LORA_ICLR)
    w_soft = -jax.nn.softplus(-(w0 + jnp.einsum("btel,elc->btec", jnp.tanh(wd), w_up))) - 0.5
    decay = jnp.exp(-jnp.exp(w_soft.astype(jnp.float32)))
    iclr = jax.nn.sigmoid(a0 + jnp.einsum("btel,elc->btec", ad, a_up))
    gate = jax.nn.sigmoid(gd) @ g_up
    kk = _heads(k * k_k).astype(jnp.float32)
    kk = kk * lax.rsqrt(jnp.sum(kk * kk, axis=-1, keepdims=True) + L2_EPS)
    k_dir = k[:, :, None, :] * (1.0 + (iclr - 1.0) * k_a)
    rh, vh = _heads(r), _heads(v)
    s_f, y_f = _wkv_scan(s0[:, 0], rh, _heads(decay[:, :, 0]), _heads(k_dir[:, :, 0]), vh,
                         -kk, kk * _heads(iclr[:, :, 0]), False)
    s_b, y_b = _wkv_scan(s0[:, 1], rh, _heads(decay[:, :, 1]), _heads(k_dir[:, :, 1]), vh,
                         -kk, kk * _heads(iclr[:, :, 1]), True)
    y = y_f + y_b
    mean = jnp.mean(y, axis=-1, keepdims=True)
    var = jnp.mean(jnp.square(y - mean), axis=-1, keepdims=True)
    y = (y - mean) * lax.rsqrt(var + GN_EPS) * _heads(ln_g).astype(jnp.float32) + _heads(ln_b).astype(jnp.float32)
    k_bonus = _heads(0.5 * (k_dir[:, :, 0] + k_dir[:, :, 1]))
    bonus = jnp.sum(rh * k_bonus * r_k, axis=-1, keepdims=True) * vh
    out = (y.astype(u.dtype) + bonus).reshape(bsz, t_len, RW_WIDTH) * gate
    return out, jnp.stack([s_f, s_b], axis=1).astype(u.dtype)


def _qk_norm(t, g):
    tf = t.astype(jnp.float32)
    y = tf * lax.rsqrt(jnp.mean(tf * tf, axis=-1, keepdims=True) + RMS_EPS)
    return (y * g.astype(jnp.float32)).astype(t.dtype)


def _na_context(q, k, v):
    bsz, c_len, h, d = q.shape
    qb = jnp.moveaxis(q.reshape(bsz, c_len // Q_BLOCK, Q_BLOCK, h, d), 1, 0)

    def block(q_b):
        logits = jnp.einsum("bqhd,bkhd->bhqk", q_b, k).astype(jnp.float32) * (HEAD_DIM ** -0.5)
        p = jax.nn.softmax(logits, axis=-1).astype(v.dtype)
        return jnp.einsum("bhqk,bkhd->bqhd", p, v)

    out = lax.map(block, qb)
    return jnp.moveaxis(out, 0, 1).reshape(bsz, c_len, h * d)


def _na_latent(q, k, v, k_ctx, v_ctx, rpb):
    bsz, t_len, h, d = q.shape
    rows = t_len // GRID_W
    kr = min(NA_ROWS, rows)
    n_cb = GRID_W // NA_QCOLS
    row_start = np.clip(np.arange(rows) - kr // 2, 0, rows - kr)
    dr_idx = row_start[:, None] + np.arange(kr)[None, :] - np.arange(rows)[:, None] + NA_ROWS - 1
    q_col = np.arange(GRID_W).reshape(n_cb, NA_QCOLS)
    band_start = np.clip(q_col[:, 0] - NA_COLS // 2, 0, GRID_W - NA_KCOLS)
    k_col = band_start[:, None] + np.arange(NA_KCOLS)[None, :]
    win_start = np.clip(q_col - NA_COLS // 2, 0, GRID_W - NA_COLS)
    valid = (k_col[:, None, :] >= win_start[..., None]) & (k_col[:, None, :] < win_start[..., None] + NA_COLS)
    dc_idx = np.clip(k_col[:, None, :] - q_col[..., None], -(NA_COLS - 1), NA_COLS - 1) + NA_COLS - 1
    col_bias = rpb[:, :, dc_idx].astype(jnp.float32)
    qg = q.reshape(bsz, rows, n_cb, NA_QCOLS, h, d)
    kg = k.reshape(bsz, rows, GRID_W, h, d)
    vg = v.reshape(bsz, rows, GRID_W, h, d)
    scale = HEAD_DIM ** -0.5
    n_win = kr * NA_KCOLS

    def row_block(args):
        q_r, r0, dr = args
        k_r = lax.dynamic_slice_in_dim(kg, r0, kr, axis=1)[:, :, k_col]
        v_r = lax.dynamic_slice_in_dim(vg, r0, kr, axis=1)[:, :, k_col]
        bias = jnp.transpose(jnp.take(col_bias, dr, axis=1), (0, 2, 3, 1, 4))
        lw = jnp.einsum("bjqhd,bkjchd->bhjqkc", q_r, k_r).astype(jnp.float32) * scale + bias[None]
        lw = jnp.where(valid[:, :, None, :], lw, NEG_INF)
        lc = jnp.einsum("bjqhd,blhd->bhjql", q_r, k_ctx).astype(jnp.float32) * scale
        logits = jnp.concatenate([lw.reshape(bsz, h, n_cb, NA_QCOLS, n_win), lc], axis=-1)
        p = jax.nn.softmax(logits, axis=-1).astype(v.dtype)
        pw = p[..., :n_win].reshape(bsz, h, n_cb, NA_QCOLS, kr, NA_KCOLS)
        pc = p[..., n_win:]
        return (jnp.einsum("bhjqkc,bkjchd->bjqhd", pw, v_r)
                + jnp.einsum("bhjql,blhd->bjqhd", pc, v_ctx))

    out = lax.map(row_block, (jnp.moveaxis(qg, 1, 0), jnp.asarray(row_start, jnp.int32),
                              jnp.asarray(dr_idx, jnp.int32)))
    return jnp.moveaxis(out, 0, 1).reshape(bsz, t_len, h * d)


def _modulation(cvec, w_ada, b_ada):
    m = jax.nn.silu(cvec) @ w_ada + b_ada
    return jnp.split(m[..., None, :], 6, axis=-1)


def _trunk_layer(x, cvec, s0, ctx_k, ctx_v, p):
    sh1, sc1, g1, sh2, sc2, g2 = _modulation(cvec, p["w_ada"], p["b_ada"])
    h = _rmsnorm(x, p["norm1_g"]) * (1.0 + sc1) + sh1
    u_rw, u_na, gate_rw, gate_na = jnp.split(h @ p["w_in"], IN_SPLITS, axis=-1)
    o_rw, s_new = _rwkv_branch(u_rw, s0, p["shift_mu"], p["rw_w0"], p["rw_w_up"], p["rw_a0"],
                               p["rw_a_up"], p["rw_g_up"], p["rw_k_k"], p["rw_k_a"], p["rw_r_k"],
                               p["rw_ln_g"], p["rw_ln_b"])
    q, k, v = jnp.split(u_na, 3, axis=-1)
    q = _qk_norm(_heads(q), p["na_q_g"])
    k = _qk_norm(_heads(k), p["na_k_g"])
    v = _heads(v)
    if ctx_k is None:
        o_na = _na_context(q, k, v)
    else:
        o_na = _na_latent(q, k, v, ctx_k, ctx_v, p["na_rpb"])
    merged = (jax.nn.sigmoid(gate_rw) * (o_rw @ p["w_o_rwkv"])
              + jax.nn.sigmoid(gate_na) * (o_na @ p["w_o_na"]))
    x = x + g1 * (merged @ p["w_out"])
    h2 = _rmsnorm(x, p["norm2_g"]) * (1.0 + sc2) + sh2
    x = x + g2 * ((jax.nn.silu(h2 @ p["ffn_w1"]) * (h2 @ p["ffn_w3"])) @ p["ffn_w2"])
    return x, s_new, k, v


def setup_inputs(seed: int = 0) -> dict:
    key = jax.random.key(seed)
    ks = jax.random.split(key, 40)
    f32 = jnp.float32
    n = lambda i, shape, s: jax.random.normal(ks[i], shape, f32) * s
    return {
        "x_prompt": n(0, (BATCH, SEQ, D_MODEL), 1.0),
        "x_sample": n(1, (DEC_BATCH, DEC_SEQ, D_MODEL), 1.0),
        "state_rwkv": n(2, (DEC_BATCH, DEPTH, 2, RW_HEADS, HEAD_DIM, HEAD_DIM), 0.5),
        "cache_na_k": n(3, (DEC_BATCH, DEPTH, PAST_LEN, NA_HEADS, HEAD_DIM), 1.0),
        "cache_na_v": n(4, (DEC_BATCH, DEPTH, PAST_LEN, NA_HEADS, HEAD_DIM), 1.0),
        "c": n(5, (DEC_BATCH, D_MODEL), 1.0),
        "c_ctx": n(6, (D_MODEL,), 1.0),
        "norm1_g": 1.0 + n(7, (DEPTH, D_MODEL), 0.02),
        "norm2_g": 1.0 + n(8, (DEPTH, D_MODEL), 0.02),
        "w_ada": n(9, (DEPTH, D_MODEL, 6 * D_MODEL), 0.5 * D_MODEL ** -0.5),
        "b_ada": n(10, (DEPTH, 6 * D_MODEL), 0.02),
        "w_in": n(11, (DEPTH, D_MODEL, IN_COLS), D_MODEL ** -0.5),
        "shift_mu": jax.random.uniform(ks[12], (DEPTH, 2, RW_COLS), f32, 0.0, 0.5),
        "rw_w0": -1.0 + n(13, (DEPTH, 2, RW_WIDTH), 0.3),
        "rw_w_up": n(14, (DEPTH, 2, LORA_DECAY, RW_WIDTH), 0.5 * LORA_DECAY ** -0.5),
        "rw_a0": n(15, (DEPTH, 2, RW_WIDTH), 0.1),
        "rw_a_up": n(16, (DEPTH, 2, LORA_ICLR, RW_WIDTH), LORA_ICLR ** -0.5),
        "rw_g_up": n(17, (DEPTH, LORA_GATE, RW_WIDTH), LORA_GATE ** -0.5),
        "rw_k_k": 0.85 + n(18, (DEPTH, RW_WIDTH), 0.02),
        "rw_k_a": 1.0 + n(19, (DEPTH, RW_WIDTH), 0.02),
        "rw_r_k": n(20, (DEPTH, RW_HEADS, HEAD_DIM), 0.1),
        "rw_ln_g": 1.0 + n(21, (DEPTH, RW_WIDTH), 0.02),
        "rw_ln_b": n(22, (DEPTH, RW_WIDTH), 0.02),
        "na_q_g": 1.0 + n(23, (DEPTH, HEAD_DIM), 0.02),
        "na_k_g": 1.0 + n(24, (DEPTH, HEAD_DIM), 0.02),
        "na_rpb": n(25, (DEPTH, NA_HEADS, 2 * NA_ROWS - 1, 2 * NA_COLS - 1), 0.1),
        "w_o_rwkv": n(26, (DEPTH, RW_WIDTH, D_MODEL), RW_WIDTH ** -0.5),
        "w_o_na": n(27, (DEPTH, NA_WIDTH, D_MODEL), NA_WIDTH ** -0.5),
        "w_out": n(28, (DEPTH, D_MODEL, D_MODEL), D_MODEL ** -0.5),
        "ffn_w1": n(29, (DEPTH, D_MODEL, FF_HIDDEN), D_MODEL ** -0.5),
        "ffn_w3": n(30, (DEPTH, D_MODEL, FF_HIDDEN), D_MODEL ** -0.5),
        "ffn_w2": n(31, (DEPTH, FF_HIDDEN, D_MODEL), FF_HIDDEN ** -0.5),
    }


def reference(x_prompt, x_sample, state_rwkv, cache_na_k, cache_na_v, c, c_ctx,
              norm1_g, norm2_g, w_ada, b_ada, w_in, shift_mu, rw_w0, rw_w_up, rw_a0, rw_a_up,
              rw_g_up, rw_k_k, rw_k_a, rw_r_k, rw_ln_g, rw_ln_b, na_q_g, na_k_g, na_rpb,
              w_o_rwkv, w_o_na, w_out, ffn_w1, ffn_w3, ffn_w2):
    y_p, y_s = x_prompt, x_sample
    new_s, new_k, new_v = [], [], []
    for l in range(DEPTH):
        p = dict(norm1_g=norm1_g[l], norm2_g=norm2_g[l], w_ada=w_ada[l], b_ada=b_ada[l],
                 w_in=w_in[l], shift_mu=shift_mu[l], rw_w0=rw_w0[l], rw_w_up=rw_w_up[l],
                 rw_a0=rw_a0[l], rw_a_up=rw_a_up[l], rw_g_up=rw_g_up[l], rw_k_k=rw_k_k[l],
                 rw_k_a=rw_k_a[l], rw_r_k=rw_r_k[l], rw_ln_g=rw_ln_g[l], rw_ln_b=rw_ln_b[l],
                 na_q_g=na_q_g[l], na_k_g=na_k_g[l], na_rpb=na_rpb[l], w_o_rwkv=w_o_rwkv[l],
                 w_o_na=w_o_na[l], w_out=w_out[l], ffn_w1=ffn_w1[l], ffn_w3=ffn_w3[l],
                 ffn_w2=ffn_w2[l])
        s0 = jnp.zeros((y_p.shape[0], 2, RW_HEADS, HEAD_DIM, HEAD_DIM), y_p.dtype)
        y_p, s_l, k_l, v_l = _trunk_layer(y_p, c_ctx, s0, None, None, p)
        new_s.append(s_l)
        new_k.append(k_l)
        new_v.append(v_l)
        y_s, _, _, _ = _trunk_layer(y_s, c, state_rwkv[:, l], cache_na_k[:, l], cache_na_v[:, l], p)
    state_rwkv_new = jnp.stack(new_s, axis=1)
    cache_na_k_new = jnp.stack(new_k, axis=1)
    cache_na_v_new = jnp.stack(new_v, axis=1)
    return (y_prompt_out := y_p, y_s, state_rwkv_new, cache_na_k_new, cache_na_v_new)
```

```python
import functools

import numpy as np
import jax
import jax.numpy as jnp
from jax import lax
from jax.experimental import pallas as pl
from jax.experimental.pallas import tpu as pltpu

D_MODEL = 1024
GRID_W = 64
HEAD_DIM = 64
RW_HEADS = 8
RW_WIDTH = RW_HEADS * HEAD_DIM
NA_HEADS = 8
NA_WIDTH = NA_HEADS * HEAD_DIM
LORA_DECAY = 64
LORA_ICLR = 64
LORA_GATE = 128
NA_ROWS = 8
NA_COLS = 16
FF_HIDDEN = 2816
RW_COLS = 3 * RW_WIDTH + 2 * LORA_DECAY + 2 * LORA_ICLR + LORA_GATE
NA_IN_COLS = 3 * NA_WIDTH
GATE_COLS = 2 * D_MODEL
RMS_EPS = 1e-6
GN_EPS = 64e-5
L2_EPS = 1e-12
NEG_INF = -1e30

LANES = 128
PAIRS = RW_HEADS // 2
CHUNK = 64
STACK = 2 * CHUNK
TOKEN_TILE = 256
FF_CHUNK = 256
VMEM_LIMIT = 56 * 1024 * 1024

F32 = jnp.float32
BF16 = jnp.bfloat16


def _dot(a, b):
    return jnp.dot(a.astype(BF16), b.astype(BF16), preferred_element_type=F32)


def _dot_nt(a, b):
    return lax.dot_general(a.astype(BF16), b.astype(BF16), (((1,), (1,)), ((), ())),
                           preferred_element_type=F32)


def _split2(x):
    hi = x.astype(BF16)
    lo = (x - hi.astype(F32)).astype(BF16)
    return hi, lo


def _split3(x):
    hi = x.astype(BF16)
    r1 = x - hi.astype(F32)
    mid = r1.astype(BF16)
    lo = (r1 - mid.astype(F32)).astype(BF16)
    return hi, mid, lo


def _dot_exact_rhs(a, b_exact):
    h, m, l = _split3(a)
    d = lambda x: jnp.dot(x, b_exact, preferred_element_type=F32)
    return d(h) + d(m) + d(l)


def _dot_exact_lhs(a_exact, b):
    h, m, l = _split3(b)
    d = lambda x: jnp.dot(a_exact, x, preferred_element_type=F32)
    return d(h) + d(m) + d(l)


def _dot3(a, b):
    ah, al = _split2(a)
    bh, bl = _split2(b)
    d = lambda x, y: jnp.dot(x, y, preferred_element_type=F32)
    return d(ah, bh) + d(al, bh) + d(ah, bl)


def _head_ones():
    r = lax.broadcasted_iota(jnp.int32, (LANES, LANES), 0) // HEAD_DIM
    c = lax.broadcasted_iota(jnp.int32, (LANES, LANES), 1) // HEAD_DIM
    return jnp.where(r == c, 1.0, 0.0).astype(BF16)


def _head_sum(x, ones):
    return _dot_exact_rhs(x, ones)


def _softplus(z):
    return jnp.maximum(z, 0.0) + jnp.log(1.0 + jnp.exp(-jnp.abs(z)))


def _rms_rows(x):
    return x * lax.rsqrt(jnp.mean(x * x, axis=-1, keepdims=True) + RMS_EPS)


def _const_spec(shape):
    nd = len(shape)
    return pl.BlockSpec(shape, lambda *_: (0,) * nd)


def _params(n_axes):
    return pltpu.CompilerParams(dimension_semantics=("arbitrary",) * n_axes,
                                vmem_limit_bytes=VMEM_LIMIT)


def _mod_kernel(c_ref, w_ref, b_ref, o_ref):
    s = c_ref[...]
    s = s * jax.nn.sigmoid(s)
    o_ref[...] = _dot3(s, w_ref[...]) + b_ref[...]


def _modulation(cvecs, w_ada, b_ada):
    n = cvecs.shape[0]
    tn = 1536
    return pl.pallas_call(
        _mod_kernel,
        grid=(6 * D_MODEL // tn,),
        in_specs=[pl.BlockSpec((n, D_MODEL), lambda j: (0, 0)),
                  pl.BlockSpec((D_MODEL, tn), lambda j: (0, j)),
                  pl.BlockSpec((1, tn), lambda j: (0, j))],
        out_specs=pl.BlockSpec((n, tn), lambda j: (0, j)),
        out_shape=jax.ShapeDtypeStruct((n, 6 * D_MODEL), F32),
        compiler_params=_params(1),
        name="modulation",
    )(cvecs, w_ada, b_ada.reshape(1, -1))


def _inproj_kernel(x_ref, mod_ref, g_ref, w_ref, urw_ref, una_ref, gt_ref):
    x = x_ref[0]
    h = _rms_rows(x) * g_ref[...]
    h = (h * (1.0 + mod_ref[0, 1:2, :]) + mod_ref[0, 0:1, :]).astype(BF16)
    d = lambda lo, hi: jnp.dot(h, w_ref[:, lo:hi], preferred_element_type=F32)
    urw_ref[0] = d(0, RW_COLS)
    una_ref[0] = d(RW_COLS, RW_COLS + NA_IN_COLS)
    gt_ref[0] = d(RW_COLS + NA_IN_COLS, RW_COLS + NA_IN_COLS + GATE_COLS)


def _in_proj(x, mod_all, mod_off, mod_stride, norm_g, w_in_bf):
    bsz, t_len, _ = x.shape
    tm = TOKEN_TILE
    row = lambda b, i: (b, i, 0)
    return pl.pallas_call(
        _inproj_kernel,
        grid=(bsz, t_len // tm),
        in_specs=[pl.BlockSpec((1, tm, D_MODEL), row),
                  pl.BlockSpec((1, 6, D_MODEL), lambda b, i: (mod_off + mod_stride * b, 0, 0)),
                  _const_spec((1, D_MODEL)),
                  _const_spec(w_in_bf.shape)],
        out_specs=[pl.BlockSpec((1, tm, RW_COLS), row),
                   pl.BlockSpec((1, tm, NA_IN_COLS), row),
                   pl.BlockSpec((1, tm, GATE_COLS), row)],
        out_shape=[jax.ShapeDtypeStruct((bsz, t_len, RW_COLS), F32),
                   jax.ShapeDtypeStruct((bsz, t_len, NA_IN_COLS), F32),
                   jax.ShapeDtypeStruct((bsz, t_len, GATE_COLS), F32)],
        compiler_params=_params(2),
        name="in_proj",
    )(x, mod_all, norm_g.reshape(1, -1), w_in_bf)


def _shift(x, mu):
    t_len = x.shape[0]
    row = lax.broadcasted_iota(jnp.int32, x.shape, 0)
    prev = jnp.where(row == 0, 0.0, pltpu.roll(x, 1, 0))
    nxt = jnp.where(row == t_len - 1, 0.0, pltpu.roll(x, t_len - 1, 0))
    return x + mu[0:1, :] * (prev - x) + mu[1:2, :] * (nxt - x)


def _stack_heads(x, lane_lo):
    return jnp.concatenate([x * lane_lo, x * (1.0 - lane_lo)], axis=0)


def _wkv_unit(r, lw, kd, v, kk, b, st, reverse, consts):
    tri_f, tri_b, strict_f, strict_b, incl_f, incl_b, eye, lane_lo = consts
    tri = tri_b if reverse else tri_f
    strict = strict_b if reverse else strict_f
    incl = incl_b if reverse else incl_f
    mid_row = CHUNK // 2 if reverse else CHUNK // 2 - 1
    tot_row = 0 if reverse else CHUNK - 1

    a = -kk
    cum = _dot_exact_lhs(tri, lw)
    ex = cum - lw
    mid = cum[mid_row:mid_row + 1, :]
    tot = cum[tot_row:tot_row + 1, :]
    up = jnp.exp(cum - mid)
    dn = jnp.exp(mid - cum)
    tail = jnp.exp(tot - cum)
    stack = lambda z: _stack_heads(z, lane_lo)
    at_m = stack(a * jnp.exp(ex - mid))
    rt_m = stack(r * up)
    bt_m = stack(b * dn)
    kt_m = stack(kd * dn)
    a_e = stack(a * jnp.exp(ex))
    r_e = stack(r * jnp.exp(cum))
    b_h = stack(b * tail)
    k_h = stack(kd * tail)
    vv = stack(v)

    a_ab = strict * _dot_nt(at_m, bt_m)
    a_ak = strict * _dot_nt(at_m, kt_m)
    a_rb = incl * _dot_nt(rt_m, bt_m)
    a_rk = incl * _dot_nt(rt_m, kt_m)

    x = jnp.concatenate([a_e, _dot(a_ak, vv)], axis=1)
    npow = a_ab
    steps = CHUNK.bit_length() - 1
    for j in range(steps):
        x = x + _dot(npow, x)
        if j < steps - 1:
            npow = _dot(npow, npow)

    gh = _dot(b_h.T, x)
    g = gh[:, :LANES] + jnp.where(eye, jnp.exp(tot), 0.0)
    h = gh[:, LANES:] + _dot(k_h.T, vv)
    qy = _dot(a_rb, x)
    q = r_e + qy[:, :LANES]
    y0 = qy[:, LANES:] + _dot(a_rk, vv)

    res = _dot(jnp.concatenate([q, g], axis=0), st)
    y_big = res[:STACK] + y0
    st_new = res[STACK:] + h
    y = y_big[:CHUNK, :] + y_big[CHUNK:, :]
    return y, st_new


def _wkv_constants():
    lane_head = np.arange(LANES) // HEAD_DIM
    ones = (lane_head[:, None] == lane_head[None, :]).astype(np.float32)
    t = np.arange(CHUNK)
    tri = np.stack([t[None, :] <= t[:, None], t[None, :] >= t[:, None]]).astype(np.float32)
    s = np.arange(STACK)
    same = (s[:, None] // CHUNK) == (s[None, :] // CHUNK)
    rs, cs = s[:, None], s[None, :]
    masks = np.stack([same & (cs < rs), same & (cs > rs), same & (cs <= rs), same & (cs >= rs)]).astype(np.float32)
    return jnp.asarray(ones, BF16), jnp.asarray(tri, BF16), jnp.asarray(masks, F32)


def _rwkv_kernel(*refs, t_len, has_s0):
    (r_ref, k_ref, v_ref, lo_ref, mur_ref, muk_ref, muv_ref, mul_ref, w0_ref, a0_ref, wup_ref, aup_ref,
     gup_ref, kk_ref, ka_ref, rk_ref, lng_ref, lnb_ref, ones_ref, tri_ref, mask_ref) = refs[:21]
    pos = 21
    s0_ref = None
    if has_s0:
        s0_ref = refs[pos]
        pos += 1
    o_ref, sn_ref = refs[pos], refs[pos + 1]
    (r_s, v_s, kk_s, b0_s, b1_s, lw0_s, lw1_s, kd0_s, kd1_s, gate_s, bonus_s, yf_s, yb_s) = refs[pos + 2:]

    ones = ones_ref[...]
    lane = lax.broadcasted_iota(jnp.int32, (1, LANES), 1)
    lo_half = lane < HEAD_DIM

    r = _shift(r_ref[0], mur_ref[...])
    k = _shift(k_ref[0], muk_ref[...])
    v = _shift(v_ref[0], muv_ref[...])
    lo = _shift(lo_ref[0], mul_ref[...])
    wd = jnp.tanh(lo[:, 0:LANES])
    ad = lo[:, LANES:2 * LANES]
    gd = lo[:, 2 * LANES:3 * LANES]
    kk = k * kk_ref[...]
    kk = kk * lax.rsqrt(_head_sum(kk * kk, ones) + L2_EPS)
    kdirs = []
    for e, (lw_s, kd_s, b_s) in enumerate(((lw0_s, kd0_s, b0_s), (lw1_s, kd1_s, b1_s))):
        sel = lo_half if e == 0 else jnp.logical_not(lo_half)
        w_lin = w0_ref[e:e + 1, :] + _dot3(jnp.where(sel, wd, 0.0), wup_ref[...])
        w_soft = -_softplus(-w_lin) - 0.5
        lw_s[...] = -jnp.exp(w_soft)
        iclr = jax.nn.sigmoid(a0_ref[e:e + 1, :] + _dot(jnp.where(sel, ad, 0.0), aup_ref[...]))
        kd = k * (1.0 + (iclr - 1.0) * ka_ref[...])
        kd_s[...] = kd
        b_s[...] = kk * iclr
        kdirs.append(kd)
    gate_s[...] = _dot(jax.nn.sigmoid(gd), gup_ref[...])
    bonus_s[...] = _head_sum(r * (0.5 * (kdirs[0] + kdirs[1])) * rk_ref[...], ones) * v
    r_s[...] = r
    v_s[...] = v
    kk_s[...] = kk

    n_chunks = t_len // CHUNK
    rs = lax.broadcasted_iota(jnp.int32, (STACK, STACK), 0)
    cs = lax.broadcasted_iota(jnp.int32, (STACK, STACK), 1)
    consts = (tri_ref[0], tri_ref[1], mask_ref[0], mask_ref[1], mask_ref[2], mask_ref[3],
              rs == cs, jnp.where(lo_half, 1.0, 0.0))

    def body(it, carry):
        st_f, st_b = carry
        sf = pl.multiple_of(it * CHUNK, CHUNK)
        sb = pl.multiple_of((n_chunks - 1 - it) * CHUNK, CHUNK)
        rows_f = pl.ds(sf, CHUNK)
        rows_b = pl.ds(sb, CHUNK)
        y_f, st_f = _wkv_unit(r_s[rows_f, :], lw0_s[rows_f, :], kd0_s[rows_f, :], v_s[rows_f, :],
                              kk_s[rows_f, :], b0_s[rows_f, :], st_f, False, consts)
        y_b, st_b = _wkv_unit(r_s[rows_b, :], lw1_s[rows_b, :], kd1_s[rows_b, :], v_s[rows_b, :],
                              kk_s[rows_b, :], b1_s[rows_b, :], st_b, True, consts)
        yf_s[rows_f, :] = y_f
        yb_s[rows_b, :] = y_b
        return st_f, st_b

    if has_s0:
        init = (s0_ref[0, 0, 0].T, s0_ref[0, 1, 0].T)
    else:
        init = (jnp.zeros((LANES, LANES), F32), jnp.zeros((LANES, LANES), F32))
    st_f, st_b = lax.fori_loop(0, n_chunks, body, init)
    sn_ref[0, 0, 0] = st_f.T
    sn_ref[0, 1, 0] = st_b.T

    y = yf_s[...] + yb_s[...]
    inv_d = 1.0 / HEAD_DIM
    mean = _head_sum(y, ones) * inv_d
    dlt = y - mean
    var = _head_sum(dlt * dlt, ones) * inv_d
    yn = dlt * lax.rsqrt(var + GN_EPS) * lng_ref[...] + lnb_ref[...]
    o_ref[0] = (yn + bonus_s[...]) * gate_s[...]


def _rwkv_branch(u_rw, s0_big, p):
    bsz, t_len, _ = u_rw.shape
    has_s0 = s0_big is not None
    seg = RW_WIDTH // LANES
    tok = lambda off: pl.BlockSpec((1, t_len, LANES), lambda b, j: (b, 0, off + j))
    mu = lambda off: pl.BlockSpec((2, LANES), lambda b, j: (0, off + j))
    vec2 = pl.BlockSpec((2, LANES), lambda b, j: (0, j))
    vec1 = pl.BlockSpec((1, LANES), lambda b, j: (0, j))
    mat = pl.BlockSpec((LANES, LANES), lambda b, j: (0, j))
    lora_w = 3 * LANES
    lora_blk = 3 * RW_WIDTH // lora_w
    in_specs = [tok(0), tok(seg), tok(2 * seg),
                pl.BlockSpec((1, t_len, lora_w), lambda b, j: (b, 0, lora_blk)),
                mu(0), mu(seg), mu(2 * seg),
                pl.BlockSpec((2, lora_w), lambda b, j: (0, lora_blk)),
                vec2, vec2, mat, mat, mat, vec1, vec1, vec1, vec1, vec1,
                _const_spec((LANES, LANES)), _const_spec((2, CHUNK, CHUNK)), _const_spec((4, STACK, STACK))]
    args = [u_rw, u_rw, u_rw, u_rw, p["shift_mu"], p["shift_mu"], p["shift_mu"], p["shift_mu"],
            p["rw_w0"], p["rw_a0"],
            p["rw_w_up"].reshape(2 * LORA_DECAY, RW_WIDTH), p["rw_a_up"].reshape(2 * LORA_ICLR, RW_WIDTH),
            p["rw_g_up"], p["rw_k_k"].reshape(1, -1), p["rw_k_a"].reshape(1, -1),
            p["rw_r_k"].reshape(1, -1), p["rw_ln_g"].reshape(1, -1), p["rw_ln_b"].reshape(1, -1),
            *_wkv_constants()]
    st_spec = pl.BlockSpec((1, 2, 1, LANES, LANES), lambda b, j: (b, 0, j, 0, 0))
    if has_s0:
        in_specs.append(st_spec)
        args.append(s0_big)
    scratch = [pltpu.VMEM((t_len, LANES), F32) for _ in range(13)]
    o_rw, s_new = pl.pallas_call(
        functools.partial(_rwkv_kernel, t_len=t_len, has_s0=has_s0),
        grid=(bsz, PAIRS),
        in_specs=in_specs,
        out_specs=[pl.BlockSpec((1, t_len, LANES), lambda b, j: (b, 0, j)), st_spec],
        out_shape=[jax.ShapeDtypeStruct((bsz, t_len, RW_WIDTH), F32),
                   jax.ShapeDtypeStruct((bsz, 2, PAIRS, LANES, LANES), F32)],
        scratch_shapes=scratch,
        compiler_params=_params(2),
        name="rwkv_branch",
    )(*args)
    return o_rw, s_new


def _state_to_big(s0):
    bsz = s0.shape[0]
    x = s0.reshape(bsz, 2, PAIRS, 2, HEAD_DIM, HEAD_DIM)
    z = jnp.zeros_like(x[:, :, :, 0])
    top = jnp.concatenate([x[:, :, :, 0], z], axis=-1)
    bot = jnp.concatenate([z, x[:, :, :, 1]], axis=-1)
    return jnp.concatenate([top, bot], axis=-2)


def _state_from_big(s_big):
    bsz = s_big.shape[0]
    h0 = s_big[:, :, :, :HEAD_DIM, :HEAD_DIM]
    h1 = s_big[:, :, :, HEAD_DIM:, HEAD_DIM:]
    return jnp.stack([h0, h1], axis=3).reshape(bsz, 2, RW_HEADS, HEAD_DIM, HEAD_DIM)


def _qk_norm(t, g, ones):
    ms = _head_sum(t * t, ones) * (1.0 / HEAD_DIM)
    return t * lax.rsqrt(ms + RMS_EPS) * g


def _na_ctx_kernel(q_ref, k_ref, v_ref, qg_ref, kg_ref, o_ref, kn_ref, vc_ref):
    ones = _head_ones()
    lo_half = lax.broadcasted_iota(jnp.int32, (1, LANES), 1) < HEAD_DIM
    qn = _qk_norm(q_ref[0], qg_ref[...], ones)
    kn = _qk_norm(k_ref[0], kg_ref[...], ones)
    v = v_ref[0]
    kn_ref[0] = kn
    vc_ref[0] = v
    scale = HEAD_DIM ** -0.5
    outs = []
    for h in range(2):
        sel = lo_half if h == 0 else jnp.logical_not(lo_half)
        s = _dot_nt(jnp.where(sel, qn, 0.0), kn) * scale
        m = jnp.max(s, axis=-1, keepdims=True)
        p = jnp.exp(s - m)
        l = jnp.sum(p, axis=-1, keepdims=True)
        outs.append(_dot(p, v) / l)
    o_ref[0] = jnp.where(lo_half, outs[0], outs[1])


def _na_context(u_na, q_g, k_g):
    bsz, t_len, _ = u_na.shape
    seg = NA_WIDTH // LANES
    tok = lambda off: pl.BlockSpec((1, t_len, LANES), lambda b, j: (b, 0, off + j))
    out_blk = pl.BlockSpec((1, t_len, LANES), lambda b, j: (b, 0, j))
    g2 = lambda g: jnp.tile(g.reshape(1, HEAD_DIM), (1, 2))
    shp = jax.ShapeDtypeStruct((bsz, t_len, NA_WIDTH), F32)
    return pl.pallas_call(
        _na_ctx_kernel,
        grid=(bsz, seg),
        in_specs=[tok(0), tok(seg), tok(2 * seg), _const_spec((1, LANES)), _const_spec((1, LANES))],
        out_specs=[out_blk, out_blk, out_blk],
        out_shape=[shp, shp, shp],
        compiler_params=_params(2),
        name="na_context",
    )(u_na, u_na, u_na, g2(q_g), g2(k_g))


def _na_lat_kernel(q_ref, k_ref, v_ref, kc_ref, vc_ref, qg_ref, kg_ref, eb_ref, o_ref, qn_s, kn_s, *, rows, kr):
    ones = _head_ones()
    lo_half = lax.broadcasted_iota(jnp.int32, (1, LANES), 1) < HEAD_DIM
    qn_s[...] = _qk_norm(q_ref[0], qg_ref[...], ones)
    kn_s[...] = _qk_norm(k_ref[0], kg_ref[...], ones)
    kc = kc_ref[0]
    vc = vc_ref[0]
    scale = HEAD_DIM ** -0.5
    win = kr * GRID_W

    def body(i, carry):
        r0 = jnp.clip(i - kr // 2, 0, rows - kr)
        var = r0 - i + (NA_ROWS - 1)
        q_rows = pl.ds(pl.multiple_of(i * GRID_W, GRID_W), GRID_W)
        k_rows = pl.ds(pl.multiple_of(r0 * GRID_W, GRID_W), win)
        qi = qn_s[q_rows, :]
        kw = kn_s[k_rows, :]
        vw = v_ref[0, k_rows, :]
        outs = []
        for h in range(2):
            sel = lo_half if h == 0 else jnp.logical_not(lo_half)
            qh = jnp.where(sel, qi, 0.0)
            lw = _dot_nt(qh, kw) * scale + eb_ref[h, var]
            lc = _dot_nt(qh, kc) * scale
            m = jnp.maximum(jnp.max(lw, axis=-1, keepdims=True), jnp.max(lc, axis=-1, keepdims=True))
            pw = jnp.exp(lw - m)
            pc = jnp.exp(lc - m)
            l = jnp.sum(pw, axis=-1, keepdims=True) + jnp.sum(pc, axis=-1, keepdims=True)
            outs.append((_dot(pw, vw) + _dot(pc, vc)) / l)
        o_ref[0, q_rows, :] = jnp.where(lo_half, outs[0], outs[1])
        return carry

    lax.fori_loop(0, rows, body, 0)


def _latent_bias_table(rpb, rows, kr):
    n_var = rows - kr + 1 if rows - kr + 1 < NA_ROWS else NA_ROWS
    qc = np.arange(GRID_W)[:, None]
    kc = np.arange(GRID_W)[None, :]
    ws = np.clip(qc - NA_COLS // 2, 0, GRID_W - NA_COLS)
    valid = (kc >= ws) & (kc < ws + NA_COLS)
    dc = np.clip(kc - qc, -(NA_COLS - 1), NA_COLS - 1) + NA_COLS - 1
    dr = np.arange(NA_ROWS)[:, None] + np.arange(kr)[None, :]
    tab = rpb[:, dr][:, :, :, dc]
    tab = jnp.where(valid[None, None, None], tab, NEG_INF)
    tab = jnp.transpose(tab, (0, 1, 3, 2, 4))
    return tab.reshape(NA_HEADS, NA_ROWS, GRID_W, kr * GRID_W)


def _na_latent(u_na, k_ctx, v_ctx, q_g, k_g, rpb):
    bsz, t_len, _ = u_na.shape
    rows = t_len // GRID_W
    kr = min(NA_ROWS, rows)
    ctx_len = k_ctx.shape[1]
    seg = NA_WIDTH // LANES
    tok = lambda off: pl.BlockSpec((1, t_len, LANES), lambda b, j: (b, 0, off + j))
    ctx = pl.BlockSpec((1, ctx_len, LANES), lambda b, j: (b, 0, j))
    g2 = lambda g: jnp.tile(g.reshape(1, HEAD_DIM), (1, 2))
    eb = _latent_bias_table(rpb, rows, kr)
    return pl.pallas_call(
        functools.partial(_na_lat_kernel, rows=rows, kr=kr),
        grid=(bsz, seg),
        in_specs=[tok(0), tok(seg), tok(2 * seg), ctx, ctx,
                  _const_spec((1, LANES)), _const_spec((1, LANES)),
                  pl.BlockSpec((2, NA_ROWS, GRID_W, kr * GRID_W), lambda b, j: (j, 0, 0, 0))],
        out_specs=pl.BlockSpec((1, t_len, LANES), lambda b, j: (b, 0, j)),
        out_shape=jax.ShapeDtypeStruct((bsz, t_len, NA_WIDTH), F32),
        scratch_shapes=[pltpu.VMEM((t_len, LANES), F32), pltpu.VMEM((t_len, LANES), F32)],
        compiler_params=_params(2),
        name="na_latent",
    )(u_na, u_na, u_na, k_ctx, v_ctx, g2(q_g), g2(k_g), eb)


def _outproj_kernel(x_ref, orw_ref, ona_ref, gt_ref, mod_ref, g_ref, wor_ref, won_ref, wout_ref, x1_ref, h2_ref):
    g_rw = jax.nn.sigmoid(gt_ref[0, :, :D_MODEL])
    g_na = jax.nn.sigmoid(gt_ref[0, :, D_MODEL:])
    merged = g_rw * _dot(orw_ref[0], wor_ref[...]) + g_na * _dot(ona_ref[0], won_ref[...])
    x1 = x_ref[0] + mod_ref[0, 2:3, :] * _dot(merged, wout_ref[...])
    x1_ref[0] = x1
    h2 = _rms_rows(x1) * g_ref[...]
    h2_ref[0] = (h2 * (1.0 + mod_ref[0, 4:5, :]) + mod_ref[0, 3:4, :]).astype(BF16)


def _out_proj(x, o_rw, o_na, gates, mod_all, mod_off, mod_stride, norm_g, w_or, w_on, w_out):
    bsz, t_len, _ = x.shape
    tm = TOKEN_TILE
    row = lambda b, i: (b, i, 0)
    return pl.pallas_call(
        _outproj_kernel,
        grid=(bsz, t_len // tm),
        in_specs=[pl.BlockSpec((1, tm, D_MODEL), row),
                  pl.BlockSpec((1, tm, RW_WIDTH), row),
                  pl.BlockSpec((1, tm, NA_WIDTH), row),
                  pl.BlockSpec((1, tm, GATE_COLS), row),
                  pl.BlockSpec((1, 6, D_MODEL), lambda b, i: (mod_off + mod_stride * b, 0, 0)),
                  _const_spec((1, D_MODEL)),
                  _const_spec(w_or.shape), _const_spec(w_on.shape), _const_spec(w_out.shape)],
        out_specs=[pl.BlockSpec((1, tm, D_MODEL), row), pl.BlockSpec((1, tm, D_MODEL), row)],
        out_shape=[jax.ShapeDtypeStruct((bsz, t_len, D_MODEL), F32),
                   jax.ShapeDtypeStruct((bsz, t_len, D_MODEL), BF16)],
        compiler_params=_params(2),
        name="out_proj",
    )(x, o_rw, o_na, gates, mod_all, norm_g.reshape(1, -1), w_or, w_on, w_out)


def _ffn_kernel(h2_ref, x1_ref, mod_ref, w1_ref, w3_ref, w2_ref, y_ref):
    h2 = h2_ref[0]
    acc = jnp.zeros((h2.shape[0], D_MODEL), F32)
    for c in range(FF_HIDDEN // FF_CHUNK):
        cols = slice(c * FF_CHUNK, (c + 1) * FF_CHUNK)
        a = jnp.dot(h2, w1_ref[:, cols], preferred_element_type=F32)
        b = jnp.dot(h2, w3_ref[:, cols], preferred_element_type=F32)
        hh = (a * jax.nn.sigmoid(a) * b).astype(BF16)
        acc = acc + jnp.dot(hh, w2_ref[cols, :], preferred_element_type=F32)
    y_ref[0] = x1_ref[0] + mod_ref[0, 5:6, :] * acc


def _ffn(h2, x1, mod_all, mod_off, mod_stride, w1, w3, w2):
    bsz, t_len, _ = x1.shape
    tm = TOKEN_TILE
    row = lambda b, i: (b, i, 0)
    return pl.pallas_call(
        _ffn_kernel,
        grid=(bsz, t_len // tm),
        in_specs=[pl.BlockSpec((1, tm, D_MODEL), row),
                  pl.BlockSpec((1, tm, D_MODEL), row),
                  pl.BlockSpec((1, 6, D_MODEL), lambda b, i: (mod_off + mod_stride * b, 0, 0)),
                  _const_spec(w1.shape), _const_spec(w3.shape), _const_spec(w2.shape)],
        out_specs=pl.BlockSpec((1, tm, D_MODEL), row),
        out_shape=jax.ShapeDtypeStruct((bsz, t_len, D_MODEL), F32),
        compiler_params=_params(2),
        name="ffn",
    )(h2, x1, mod_all, w1, w3, w2)


def _trunk(x, mod_all, mod_off, mod_stride, s0_big, ctx_kv, p, wb):
    u_rw, u_na, gates = _in_proj(x, mod_all, mod_off, mod_stride, p["norm1_g"], wb["w_in"])
    o_rw, s_new = _rwkv_branch(u_rw, s0_big, p)
    if ctx_kv is None:
        o_na, k_new, v_new = _na_context(u_na, p["na_q_g"], p["na_k_g"])
    else:
        o_na = _na_latent(u_na, ctx_kv[0], ctx_kv[1], p["na_q_g"], p["na_k_g"], p["na_rpb"])
        k_new = v_new = None
    x1, h2 = _out_proj(x, o_rw, o_na, gates, mod_all, mod_off, mod_stride, p["norm2_g"],
                       wb["w_o_rwkv"], wb["w_o_na"], wb["w_out"])
    y = _ffn(h2, x1, mod_all, mod_off, mod_stride, wb["ffn_w1"], wb["ffn_w3"], wb["ffn_w2"])
    return y, s_new, k_new, v_new


def kernel(x_prompt, x_sample, state_rwkv, cache_na_k, cache_na_v, c, c_ctx, norm1_g, norm2_g, w_ada, b_ada,
           w_in, shift_mu, rw_w0, rw_w_up, rw_a0, rw_a_up, rw_g_up, rw_k_k, rw_k_a, rw_r_k, rw_ln_g, rw_ln_b,
           na_q_g, na_k_g, na_rpb, w_o_rwkv, w_o_na, w_out, ffn_w1, ffn_w3, ffn_w2):
    depth = w_in.shape[0]
    bsz, seq = x_prompt.shape[:2]
    dec = x_sample.shape[0]
    n_vec = 8
    cvecs = jnp.concatenate([c_ctx[None, :], c, jnp.zeros((n_vec - 1 - dec, D_MODEL), F32)], axis=0)
    y_p, y_s = x_prompt, x_sample
    new_s, new_k, new_v = [], [], []
    for l in range(depth):
        p = dict(norm1_g=norm1_g[l], norm2_g=norm2_g[l], shift_mu=shift_mu[l], rw_w0=rw_w0[l],
                 rw_w_up=rw_w_up[l], rw_a0=rw_a0[l], rw_a_up=rw_a_up[l], rw_g_up=rw_g_up[l],
                 rw_k_k=rw_k_k[l], rw_k_a=rw_k_a[l], rw_r_k=rw_r_k[l], rw_ln_g=rw_ln_g[l],
                 rw_ln_b=rw_ln_b[l], na_q_g=na_q_g[l], na_k_g=na_k_g[l], na_rpb=na_rpb[l])
        wb = dict(w_in=w_in[l].astype(BF16), w_o_rwkv=w_o_rwkv[l].astype(BF16), w_o_na=w_o_na[l].astype(BF16),
                  w_out=w_out[l].astype(BF16), ffn_w1=ffn_w1[l].astype(BF16), ffn_w3=ffn_w3[l].astype(BF16),
                  ffn_w2=ffn_w2[l].astype(BF16))
        mod_all = _modulation(cvecs, w_ada[l], b_ada[l])[:1 + dec].reshape(1 + dec, 6, D_MODEL)
        y_p, s_big, k_l, v_l = _trunk(y_p, mod_all, 0, 0, None, None, p, wb)
        new_s.append(_state_from_big(s_big))
        new_k.append(k_l.reshape(bsz, seq, NA_HEADS, HEAD_DIM))
        new_v.append(v_l.reshape(bsz, seq, NA_HEADS, HEAD_DIM))
        ctx_k = cache_na_k[:, l].reshape(dec, -1, NA_WIDTH)
        ctx_v = cache_na_v[:, l].reshape(dec, -1, NA_WIDTH)
        y_s, _, _, _ = _trunk(y_s, mod_all, 1, 1, _state_to_big(state_rwkv[:, l]), (ctx_k, ctx_v), p, wb)
    return (y_p, y_s, jnp.stack(new_s, axis=1), jnp.stack(new_k, axis=1), jnp.stack(new_v, axis=1))
```

```python
import functools

import numpy as np
import jax
import jax.numpy as jnp
from jax import lax
from jax.experimental import pallas as pl
from jax.experimental.pallas import tpu as pltpu

D_MODEL = 1024
GRID_W = 64
HEAD_DIM = 64
RW_HEADS = 8
RW_WIDTH = RW_HEADS * HEAD_DIM
NA_HEADS = 8
NA_WIDTH = NA_HEADS * HEAD_DIM
LORA_DECAY = 64
LORA_ICLR = 64
LORA_GATE = 128
NA_ROWS = 8
NA_COLS = 16
FF_HIDDEN = 2816
RW_COLS = 3 * RW_WIDTH + 2 * LORA_DECAY + 2 * LORA_ICLR + LORA_GATE
NA_IN_COLS = 3 * NA_WIDTH
GATE_COLS = 2 * D_MODEL
RMS_EPS = 1e-6
GN_EPS = 64e-5
L2_EPS = 1e-12
NEG_INF = -1e30

LANES = 128
PAIRS = RW_HEADS // 2
CHUNK = 64
STACK = 2 * CHUNK
UNIT_CHUNKS = 2
TOKEN_TILE = 256
FF_CHUNK = 256
VMEM_LIMIT = 56 * 1024 * 1024

F32 = jnp.float32
BF16 = jnp.bfloat16


def _dot(a, b):
    return jnp.dot(a.astype(BF16), b.astype(BF16), preferred_element_type=F32)


def _dot_nt(a, b):
    return lax.dot_general(a.astype(BF16), b.astype(BF16), (((1,), (1,)), ((), ())),
                           preferred_element_type=F32)


def _split2(x):
    hi = x.astype(BF16)
    lo = (x - hi.astype(F32)).astype(BF16)
    return hi, lo


def _split3(x):
    hi = x.astype(BF16)
    r1 = x - hi.astype(F32)
    mid = r1.astype(BF16)
    lo = (r1 - mid.astype(F32)).astype(BF16)
    return hi, mid, lo


def _dot_exact_rhs(a, b_exact):
    h, m, l = _split3(a)
    d = lambda x: jnp.dot(x, b_exact, preferred_element_type=F32)
    return d(h) + d(m) + d(l)


def _dot_exact_lhs(a_exact, b):
    h, m, l = _split3(b)
    d = lambda x: jnp.dot(a_exact, x, preferred_element_type=F32)
    return d(h) + d(m) + d(l)


def _dot3(a, b):
    ah, al = _split2(a)
    bh, bl = _split2(b)
    d = lambda x, y: jnp.dot(x, y, preferred_element_type=F32)
    return d(ah, bh) + d(al, bh) + d(ah, bl)


def _head_ones():
    r = lax.broadcasted_iota(jnp.int32, (LANES, LANES), 0) // HEAD_DIM
    c = lax.broadcasted_iota(jnp.int32, (LANES, LANES), 1) // HEAD_DIM
    return jnp.where(r == c, 1.0, 0.0).astype(BF16)


def _head_sum(x, ones):
    return _dot_exact_rhs(x, ones)


def _softplus(z):
    return jnp.maximum(z, 0.0) + jnp.log(1.0 + jnp.exp(-jnp.abs(z)))


def _rms_rows(x):
    return x * lax.rsqrt(jnp.mean(x * x, axis=-1, keepdims=True) + RMS_EPS)


def _const_spec(shape):
    nd = len(shape)
    return pl.BlockSpec(shape, lambda *_: (0,) * nd)


def _params(n_axes):
    return pltpu.CompilerParams(dimension_semantics=("arbitrary",) * n_axes,
                                vmem_limit_bytes=VMEM_LIMIT)


def _mod_kernel(c_ref, w_ref, b_ref, o_ref):
    s = c_ref[...]
    s = s * jax.nn.sigmoid(s)
    o_ref[...] = _dot3(s, w_ref[...]) + b_ref[...]


def _modulation(cvecs, w_ada, b_ada):
    n = cvecs.shape[0]
    tn = 1536
    return pl.pallas_call(
        _mod_kernel,
        grid=(6 * D_MODEL // tn,),
        in_specs=[pl.BlockSpec((n, D_MODEL), lambda j: (0, 0)),
                  pl.BlockSpec((D_MODEL, tn), lambda j: (0, j)),
                  pl.BlockSpec((1, tn), lambda j: (0, j))],
        out_specs=pl.BlockSpec((n, tn), lambda j: (0, j)),
        out_shape=jax.ShapeDtypeStruct((n, 6 * D_MODEL), F32),
        compiler_params=_params(1),
        name="modulation",
    )(cvecs, w_ada, b_ada.reshape(1, -1))


def _inproj_kernel(x_ref, mod_ref, g_ref, w_ref, urw_ref, una_ref, gt_ref):
    x = x_ref[0]
    h = _rms_rows(x) * g_ref[...]
    h = (h * (1.0 + mod_ref[0, 1:2, :]) + mod_ref[0, 0:1, :]).astype(BF16)
    d = lambda lo, hi: jnp.dot(h, w_ref[:, lo:hi], preferred_element_type=F32)
    urw_ref[0] = d(0, RW_COLS)
    una_ref[0] = d(RW_COLS, RW_COLS + NA_IN_COLS)
    gt_ref[0] = d(RW_COLS + NA_IN_COLS, RW_COLS + NA_IN_COLS + GATE_COLS)


def _in_proj(x, mod_all, mod_off, mod_stride, norm_g, w_in_bf):
    bsz, t_len, _ = x.shape
    tm = TOKEN_TILE
    row = lambda b, i: (b, i, 0)
    return pl.pallas_call(
        _inproj_kernel,
        grid=(bsz, t_len // tm),
        in_specs=[pl.BlockSpec((1, tm, D_MODEL), row),
                  pl.BlockSpec((1, 6, D_MODEL), lambda b, i: (mod_off + mod_stride * b, 0, 0)),
                  _const_spec((1, D_MODEL)),
                  _const_spec(w_in_bf.shape)],
        out_specs=[pl.BlockSpec((1, tm, RW_COLS), row),
                   pl.BlockSpec((1, tm, NA_IN_COLS), row),
                   pl.BlockSpec((1, tm, GATE_COLS), row)],
        out_shape=[jax.ShapeDtypeStruct((bsz, t_len, RW_COLS), F32),
                   jax.ShapeDtypeStruct((bsz, t_len, NA_IN_COLS), F32),
                   jax.ShapeDtypeStruct((bsz, t_len, GATE_COLS), F32)],
        compiler_params=_params(2),
        name="in_proj",
    )(x, mod_all, norm_g.reshape(1, -1), w_in_bf)


def _shift(x, mu):
    t_len = x.shape[0]
    row = lax.broadcasted_iota(jnp.int32, x.shape, 0)
    prev = jnp.where(row == 0, 0.0, pltpu.roll(x, 1, 0))
    nxt = jnp.where(row == t_len - 1, 0.0, pltpu.roll(x, t_len - 1, 0))
    return x + mu[0:1, :] * (prev - x) + mu[1:2, :] * (nxt - x)


def _stack_heads(x, lane_lo):
    return jnp.concatenate([x * lane_lo, x * (1.0 - lane_lo)], axis=0)


def _wkv_intra(units, consts):
    tri, mask_s, mask_i, eye, lane_lo = consts
    stack = lambda z: _stack_heads(z, lane_lo)

    cums = [_dot_exact_lhs(tri[int(u[6])], u[1]) for u in units]

    prep = []
    for (r, lw, kd, v, kk, b, reverse), cum in zip(units, cums):
        mid_row = CHUNK // 2 if reverse else CHUNK // 2 - 1
        tot_row = 0 if reverse else CHUNK - 1
        a = -kk
        ex = cum - lw
        mid = cum[mid_row:mid_row + 1, :]
        tot = cum[tot_row:tot_row + 1, :]
        up = jnp.exp(cum - mid)
        dn = jnp.exp(mid - cum)
        tail = jnp.exp(tot - cum)
        prep.append(dict(
            at_m=stack(a * jnp.exp(ex - mid)).astype(BF16),
            rt_m=stack(r * up).astype(BF16),
            btkt=jnp.concatenate([stack(b * dn), stack(kd * dn)], axis=0).astype(BF16),
            a_e=stack(a * jnp.exp(ex)),
            r_e=stack(r * jnp.exp(cum)),
            bk_t=jnp.concatenate([stack(b * tail), stack(kd * tail)], axis=0).T.astype(BF16),
            vv=stack(v).astype(BF16),
            diag=jnp.where(eye, jnp.exp(tot), 0.0),
            rev=int(reverse)))

    ntd = lambda x, y: lax.dot_general(x, y, (((1,), (1,)), ((), ())), preferred_element_type=F32)
    mm = lambda x, y: jnp.dot(x, y, preferred_element_type=F32)
    top = [mask_s[p["rev"]] * ntd(p["at_m"], p["btkt"]) for p in prep]
    bot = [(mask_i[p["rev"]] * ntd(p["rt_m"], p["btkt"])).astype(BF16) for p in prep]

    npow = [t[:, :LANES].astype(BF16) for t in top]
    xs = [jnp.concatenate([p["a_e"], mm(t[:, LANES:].astype(BF16), p["vv"])], axis=1) for p, t in zip(prep, top)]
    steps = CHUNK.bit_length() - 1
    for j in range(steps):
        xs = [x + mm(nw, x.astype(BF16)) for x, nw in zip(xs, npow)]
        if j < steps - 1:
            npow = [mm(nw, nw).astype(BF16) for nw in npow]

    out = []
    zeros = jnp.zeros((STACK, LANES), BF16)
    for p, x, bt in zip(prep, xs, bot):
        rhs = jnp.concatenate([x.astype(BF16), jnp.concatenate([zeros, p["vv"]], axis=1)], axis=0)
        lhs = jnp.concatenate([bt, p["bk_t"]], axis=0)
        res = mm(lhs, rhs)
        lhs2 = res[:, :LANES] + jnp.concatenate([p["r_e"], p["diag"]], axis=0)
        out.append((lhs2.astype(BF16), res[:, LANES:]))
    return out


def _wkv_constants():
    lane_head = np.arange(LANES) // HEAD_DIM
    ones = (lane_head[:, None] == lane_head[None, :]).astype(np.float32)
    t = np.arange(CHUNK)
    tri = np.stack([t[None, :] <= t[:, None], t[None, :] >= t[:, None]]).astype(np.float32)
    s = np.arange(STACK)
    same = (s[:, None] // CHUNK) == (s[None, :] // CHUNK)
    rs, cs = s[:, None], s[None, :]
    masks = np.stack([same & (cs < rs), same & (cs > rs), same & (cs <= rs), same & (cs >= rs)]).astype(np.float32)
    masks = np.concatenate([masks, masks], axis=-1)
    return jnp.asarray(ones, BF16), jnp.asarray(tri, BF16), jnp.asarray(masks, F32)


def _rwkv_kernel(*refs, t_len, has_s0):
    (r_ref, k_ref, v_ref, lo_ref, mur_ref, muk_ref, muv_ref, mul_ref, w0_ref, a0_ref, wup_ref, aup_ref,
     gup_ref, kk_ref, ka_ref, rk_ref, lng_ref, lnb_ref, ones_ref, tri_ref, mask_ref) = refs[:21]
    pos = 21
    s0_ref = None
    if has_s0:
        s0_ref = refs[pos]
        pos += 1
    o_ref, sn_ref = refs[pos], refs[pos + 1]
    (r_s, v_s, kk_s, b0_s, b1_s, lw0_s, lw1_s, kd0_s, kd1_s, gate_s, bonus_s, yf_s, yb_s,
     lhs_s, add_s, st_s) = refs[pos + 2:]

    ones = ones_ref[...]
    lane = lax.broadcasted_iota(jnp.int32, (1, LANES), 1)
    lo_half = lane < HEAD_DIM

    r = _shift(r_ref[0], mur_ref[...])
    k = _shift(k_ref[0], muk_ref[...])
    v = _shift(v_ref[0], muv_ref[...])
    lo = _shift(lo_ref[0], mul_ref[...])
    wd = jnp.tanh(lo[:, 0:LANES])
    ad = lo[:, LANES:2 * LANES]
    gd = lo[:, 2 * LANES:3 * LANES]
    kk = k * kk_ref[...]
    kk = kk * lax.rsqrt(_head_sum(kk * kk, ones) + L2_EPS)
    kdirs = []
    for e, (lw_s, kd_s, b_s) in enumerate(((lw0_s, kd0_s, b0_s), (lw1_s, kd1_s, b1_s))):
        sel = lo_half if e == 0 else jnp.logical_not(lo_half)
        w_lin = w0_ref[e:e + 1, :] + _dot3(jnp.where(sel, wd, 0.0), wup_ref[...])
        w_soft = -_softplus(-w_lin) - 0.5
        lw_s[...] = -jnp.exp(w_soft)
        iclr = jax.nn.sigmoid(a0_ref[e:e + 1, :] + _dot(jnp.where(sel, ad, 0.0), aup_ref[...]))
        kd = k * (1.0 + (iclr - 1.0) * ka_ref[...])
        kd_s[...] = kd
        b_s[...] = kk * iclr
        kdirs.append(kd)
    gate_s[...] = _dot(jax.nn.sigmoid(gd), gup_ref[...])
    bonus_s[...] = _head_sum(r * (0.5 * (kdirs[0] + kdirs[1])) * rk_ref[...], ones) * v
    r_s[...] = r
    v_s[...] = v
    kk_s[...] = kk

    n_chunks = t_len // CHUNK
    rs = lax.broadcasted_iota(jnp.int32, (STACK, STACK), 0)
    cs = lax.broadcasted_iota(jnp.int32, (STACK, STACK), 1)
    consts = ((tri_ref[0], tri_ref[1]), (mask_ref[0], mask_ref[1]), (mask_ref[2], mask_ref[3]),
              rs == cs, jnp.where(lo_half, 1.0, 0.0))
    dirs = ((lw0_s, kd0_s, b0_s), (lw1_s, kd1_s, b1_s))

    def intra_body(it, carry):
        units, ids = [], []
        for cc in range(UNIT_CHUNKS):
            c = it * UNIT_CHUNKS + cc
            rows = pl.ds(pl.multiple_of(c * CHUNK, CHUNK), CHUNK)
            for e, (lw_s, kd_s, b_s) in enumerate(dirs):
                units.append((r_s[rows, :], lw_s[rows, :], kd_s[rows, :], v_s[rows, :], kk_s[rows, :],
                              b_s[rows, :], e == 1))
                ids.append(e * n_chunks + c)
        for uid, (lhs, add) in zip(ids, _wkv_intra(units, consts)):
            lhs_s[uid] = lhs
            add_s[uid] = add
        return carry

    lax.fori_loop(0, n_chunks // UNIT_CHUNKS, intra_body, 0)

    if has_s0:
        st_s[0] = s0_ref[0, 0, 0].T
        st_s[1] = s0_ref[0, 1, 0].T
    else:
        st_s[...] = jnp.zeros((2, LANES, LANES), F32)

    def state_body(it, carry):
        cf = it
        cb = n_chunks - 1 - it
        res_f = jnp.dot(lhs_s[cf], st_s[0].astype(BF16), preferred_element_type=F32) + add_s[cf]
        res_b = jnp.dot(lhs_s[n_chunks + cb], st_s[1].astype(BF16), preferred_element_type=F32) + add_s[n_chunks + cb]
        yf_s[pl.ds(pl.multiple_of(cf * CHUNK, CHUNK), CHUNK), :] = res_f[:CHUNK] + res_f[CHUNK:STACK]
        yb_s[pl.ds(pl.multiple_of(cb * CHUNK, CHUNK), CHUNK), :] = res_b[:CHUNK] + res_b[CHUNK:STACK]
        st_s[0] = res_f[STACK:]
        st_s[1] = res_b[STACK:]
        return carry

    lax.fori_loop(0, n_chunks, state_body, 0)
    sn_ref[0, 0, 0] = st_s[0].T
    sn_ref[0, 1, 0] = st_s[1].T

    y = yf_s[...] + yb_s[...]
    inv_d = 1.0 / HEAD_DIM
    mean = _head_sum(y, ones) * inv_d
    dlt = y - mean
    var = _head_sum(dlt * dlt, ones) * inv_d
    yn = dlt * lax.rsqrt(var + GN_EPS) * lng_ref[...] + lnb_ref[...]
    o_ref[0] = (yn + bonus_s[...]) * gate_s[...]


def _rwkv_branch(u_rw, s0_big, p):
    bsz, t_len, _ = u_rw.shape
    has_s0 = s0_big is not None
    seg = RW_WIDTH // LANES
    tok = lambda off: pl.BlockSpec((1, t_len, LANES), lambda b, j: (b, 0, off + j))
    mu = lambda off: pl.BlockSpec((2, LANES), lambda b, j: (0, off + j))
    vec2 = pl.BlockSpec((2, LANES), lambda b, j: (0, j))
    vec1 = pl.BlockSpec((1, LANES), lambda b, j: (0, j))
    mat = pl.BlockSpec((LANES, LANES), lambda b, j: (0, j))
    lora_w = 3 * LANES
    lora_blk = 3 * RW_WIDTH // lora_w
    in_specs = [tok(0), tok(seg), tok(2 * seg),
                pl.BlockSpec((1, t_len, lora_w), lambda b, j: (b, 0, lora_blk)),
                mu(0), mu(seg), mu(2 * seg),
                pl.BlockSpec((2, lora_w), lambda b, j: (0, lora_blk)),
                vec2, vec2, mat, mat, mat, vec1, vec1, vec1, vec1, vec1,
                _const_spec((LANES, LANES)), _const_spec((2, CHUNK, CHUNK)), _const_spec((4, STACK, 2 * STACK))]
    args = [u_rw, u_rw, u_rw, u_rw, p["shift_mu"], p["shift_mu"], p["shift_mu"], p["shift_mu"],
            p["rw_w0"], p["rw_a0"],
            p["rw_w_up"].reshape(2 * LORA_DECAY, RW_WIDTH), p["rw_a_up"].reshape(2 * LORA_ICLR, RW_WIDTH),
            p["rw_g_up"], p["rw_k_k"].reshape(1, -1), p["rw_k_a"].reshape(1, -1),
            p["rw_r_k"].reshape(1, -1), p["rw_ln_g"].reshape(1, -1), p["rw_ln_b"].reshape(1, -1),
            *_wkv_constants()]
    st_spec = pl.BlockSpec((1, 2, 1, LANES, LANES), lambda b, j: (b, 0, j, 0, 0))
    if has_s0:
        in_specs.append(st_spec)
        args.append(s0_big)
    n_units = 2 * (t_len // CHUNK)
    scratch = [pltpu.VMEM((t_len, LANES), F32) for _ in range(13)]
    scratch += [pltpu.VMEM((n_units, 2 * STACK, LANES), BF16), pltpu.VMEM((n_units, 2 * STACK, LANES), F32),
                pltpu.VMEM((2, LANES, LANES), F32)]
    o_rw, s_new = pl.pallas_call(
        functools.partial(_rwkv_kernel, t_len=t_len, has_s0=has_s0),
        grid=(bsz, PAIRS),
        in_specs=in_specs,
        out_specs=[pl.BlockSpec((1, t_len, LANES), lambda b, j: (b, 0, j)), st_spec],
        out_shape=[jax.ShapeDtypeStruct((bsz, t_len, RW_WIDTH), F32),
                   jax.ShapeDtypeStruct((bsz, 2, PAIRS, LANES, LANES), F32)],
        scratch_shapes=scratch,
        compiler_params=_params(2),
        name="rwkv_branch",
    )(*args)
    return o_rw, s_new


def _state_to_big(s0):
    bsz = s0.shape[0]
    x = s0.reshape(bsz, 2, PAIRS, 2, HEAD_DIM, HEAD_DIM)
    z = jnp.zeros_like(x[:, :, :, 0])
    top = jnp.concatenate([x[:, :, :, 0], z], axis=-1)
    bot = jnp.concatenate([z, x[:, :, :, 1]], axis=-1)
    return jnp.concatenate([top, bot], axis=-2)


def _state_from_big(s_big):
    bsz = s_big.shape[0]
    h0 = s_big[:, :, :, :HEAD_DIM, :HEAD_DIM]
    h1 = s_big[:, :, :, HEAD_DIM:, HEAD_DIM:]
    return jnp.stack([h0, h1], axis=3).reshape(bsz, 2, RW_HEADS, HEAD_DIM, HEAD_DIM)


def _qk_norm(t, g, ones):
    ms = _head_sum(t * t, ones) * (1.0 / HEAD_DIM)
    return t * lax.rsqrt(ms + RMS_EPS) * g


def _na_ctx_kernel(q_ref, k_ref, v_ref, qg_ref, kg_ref, o_ref, kn_ref, vc_ref):
    ones = _head_ones()
    lo_half = lax.broadcasted_iota(jnp.int32, (1, LANES), 1) < HEAD_DIM
    qn = _qk_norm(q_ref[0], qg_ref[...], ones)
    kn = _qk_norm(k_ref[0], kg_ref[...], ones)
    v = v_ref[0]
    kn_ref[0] = kn
    vc_ref[0] = v
    scale = HEAD_DIM ** -0.5
    outs = []
    for h in range(2):
        sel = lo_half if h == 0 else jnp.logical_not(lo_half)
        s = _dot_nt(jnp.where(sel, qn, 0.0), kn) * scale
        m = jnp.max(s, axis=-1, keepdims=True)
        p = jnp.exp(s - m)
        l = jnp.sum(p, axis=-1, keepdims=True)
        outs.append(_dot(p, v) / l)
    o_ref[0] = jnp.where(lo_half, outs[0], outs[1])


def _na_context(u_na, q_g, k_g):
    bsz, t_len, _ = u_na.shape
    seg = NA_WIDTH // LANES
    tok = lambda off: pl.BlockSpec((1, t_len, LANES), lambda b, j: (b, 0, off + j))
    out_blk = pl.BlockSpec((1, t_len, LANES), lambda b, j: (b, 0, j))
    g2 = lambda g: jnp.tile(g.reshape(1, HEAD_DIM), (1, 2))
    shp = jax.ShapeDtypeStruct((bsz, t_len, NA_WIDTH), F32)
    return pl.pallas_call(
        _na_ctx_kernel,
        grid=(bsz, seg),
        in_specs=[tok(0), tok(seg), tok(2 * seg), _const_spec((1, LANES)), _const_spec((1, LANES))],
        out_specs=[out_blk, out_blk, out_blk],
        out_shape=[shp, shp, shp],
        compiler_params=_params(2),
        name="na_context",
    )(u_na, u_na, u_na, g2(q_g), g2(k_g))


def _na_lat_kernel(q_ref, k_ref, v_ref, kc_ref, vc_ref, qg_ref, kg_ref, eb_ref, o_ref, qn_s, kn_s, *, rows, kr):
    ones = _head_ones()
    lo_half = lax.broadcasted_iota(jnp.int32, (1, LANES), 1) < HEAD_DIM
    qn_s[...] = _qk_norm(q_ref[0], qg_ref[...], ones)
    kn_s[...] = _qk_norm(k_ref[0], kg_ref[...], ones)
    kc = kc_ref[0]
    vc = vc_ref[0]
    scale = HEAD_DIM ** -0.5
    win = kr * GRID_W

    def body(i, carry):
        r0 = jnp.clip(i - kr // 2, 0, rows - kr)
        var = r0 - i + (NA_ROWS - 1)
        q_rows = pl.ds(pl.multiple_of(i * GRID_W, GRID_W), GRID_W)
        k_rows = pl.ds(pl.multiple_of(r0 * GRID_W, GRID_W), win)
        qi = qn_s[q_rows, :]
        kw = kn_s[k_rows, :]
        vw = v_ref[0, k_rows, :]
        outs = []
        for h in range(2):
            sel = lo_half if h == 0 else jnp.logical_not(lo_half)
            qh = jnp.where(sel, qi, 0.0)
            lw = _dot_nt(qh, kw) * scale + eb_ref[h, var]
            lc = _dot_nt(qh, kc) * scale
            m = jnp.maximum(jnp.max(lw, axis=-1, keepdims=True), jnp.max(lc, axis=-1, keepdims=True))
            pw = jnp.exp(lw - m)
            pc = jnp.exp(lc - m)
            l = jnp.sum(pw, axis=-1, keepdims=True) + jnp.sum(pc, axis=-1, keepdims=True)
            outs.append((_dot(pw, vw) + _dot(pc, vc)) / l)
        o_ref[0, q_rows, :] = jnp.where(lo_half, outs[0], outs[1])
        return carry

    lax.fori_loop(0, rows, body, 0)


def _latent_bias_table(rpb, rows, kr):
    n_var = rows - kr + 1 if rows - kr + 1 < NA_ROWS else NA_ROWS
    qc = np.arange(GRID_W)[:, None]
    kc = np.arange(GRID_W)[None, :]
    ws = np.clip(qc - NA_COLS // 2, 0, GRID_W - NA_COLS)
    valid = (kc >= ws) & (kc < ws + NA_COLS)
    dc = np.clip(kc - qc, -(NA_COLS - 1), NA_COLS - 1) + NA_COLS - 1
    dr = np.arange(NA_ROWS)[:, None] + np.arange(kr)[None, :]
    tab = rpb[:, dr][:, :, :, dc]
    tab = jnp.where(valid[None, None, None], tab, NEG_INF)
    tab = jnp.transpose(tab, (0, 1, 3, 2, 4))
    return tab.reshape(NA_HEADS, NA_ROWS, GRID_W, kr * GRID_W)


def _na_latent(u_na, k_ctx, v_ctx, q_g, k_g, rpb):
    bsz, t_len, _ = u_na.shape
    rows = t_len // GRID_W
    kr = min(NA_ROWS, rows)
    ctx_len = k_ctx.shape[1]
    seg = NA_WIDTH // LANES
    tok = lambda off: pl.BlockSpec((1, t_len, LANES), lambda b, j: (b, 0, off + j))
    ctx = pl.BlockSpec((1, ctx_len, LANES), lambda b, j: (b, 0, j))
    g2 = lambda g: jnp.tile(g.reshape(1, HEAD_DIM), (1, 2))
    eb = _latent_bias_table(rpb, rows, kr)
    return pl.pallas_call(
        functools.partial(_na_lat_kernel, rows=rows, kr=kr),
        grid=(bsz, seg),
        in_specs=[tok(0), tok(seg), tok(2 * seg), ctx, ctx,
                  _const_spec((1, LANES)), _const_spec((1, LANES)),
                  pl.BlockSpec((2, NA_ROWS, GRID_W, kr * GRID_W), lambda b, j: (j, 0, 0, 0))],
        out_specs=pl.BlockSpec((1, t_len, LANES), lambda b, j: (b, 0, j)),
        out_shape=jax.ShapeDtypeStruct((bsz, t_len, NA_WIDTH), F32),
        scratch_shapes=[pltpu.VMEM((t_len, LANES), F32), pltpu.VMEM((t_len, LANES), F32)],
        compiler_params=_params(2),
        name="na_latent",
    )(u_na, u_na, u_na, k_ctx, v_ctx, g2(q_g), g2(k_g), eb)


def _outproj_kernel(x_ref, orw_ref, ona_ref, gt_ref, mod_ref, g_ref, wor_ref, won_ref, wout_ref, x1_ref, h2_ref):
    g_rw = jax.nn.sigmoid(gt_ref[0, :, :D_MODEL])
    g_na = jax.nn.sigmoid(gt_ref[0, :, D_MODEL:])
    merged = g_rw * _dot(orw_ref[0], wor_ref[...]) + g_na * _dot(ona_ref[0], won_ref[...])
    x1 = x_ref[0] + mod_ref[0, 2:3, :] * _dot(merged, wout_ref[...])
    x1_ref[0] = x1
    h2 = _rms_rows(x1) * g_ref[...]
    h2_ref[0] = (h2 * (1.0 + mod_ref[0, 4:5, :]) + mod_ref[0, 3:4, :]).astype(BF16)


def _out_proj(x, o_rw, o_na, gates, mod_all, mod_off, mod_stride, norm_g, w_or, w_on, w_out):
    bsz, t_len, _ = x.shape
    tm = TOKEN_TILE
    row = lambda b, i: (b, i, 0)
    return pl.pallas_call(
        _outproj_kernel,
        grid=(bsz, t_len // tm),
        in_specs=[pl.BlockSpec((1, tm, D_MODEL), row),
                  pl.BlockSpec((1, tm, RW_WIDTH), row),
                  pl.BlockSpec((1, tm, NA_WIDTH), row),
                  pl.BlockSpec((1, tm, GATE_COLS), row),
                  pl.BlockSpec((1, 6, D_MODEL), lambda b, i: (mod_off + mod_stride * b, 0, 0)),
                  _const_spec((1, D_MODEL)),
                  _const_spec(w_or.shape), _const_spec(w_on.shape), _const_spec(w_out.shape)],
        out_specs=[pl.BlockSpec((1, tm, D_MODEL), row), pl.BlockSpec((1, tm, D_MODEL), row)],
        out_shape=[jax.ShapeDtypeStruct((bsz, t_len, D_MODEL), F32),
                   jax.ShapeDtypeStruct((bsz, t_len, D_MODEL), BF16)],
        compiler_params=_params(2),
        name="out_proj",
    )(x, o_rw, o_na, gates, mod_all, norm_g.reshape(1, -1), w_or, w_on, w_out)


def _ffn_kernel(h2_ref, x1_ref, mod_ref, w1_ref, w3_ref, w2_ref, y_ref):
    h2 = h2_ref[0]
    acc = jnp.zeros((h2.shape[0], D_MODEL), F32)
    for c in range(FF_HIDDEN // FF_CHUNK):
        cols = slice(c * FF_CHUNK, (c + 1) * FF_CHUNK)
        a = jnp.dot(h2, w1_ref[:, cols], preferred_element_type=F32)
        b = jnp.dot(h2, w3_ref[:, cols], preferred_element_type=F32)
        hh = (a * jax.nn.sigmoid(a) * b).astype(BF16)
        acc = acc + jnp.dot(hh, w2_ref[cols, :], preferred_element_type=F32)
    y_ref[0] = x1_ref[0] + mod_ref[0, 5:6, :] * acc


def _ffn(h2, x1, mod_all, mod_off, mod_stride, w1, w3, w2):
    bsz, t_len, _ = x1.shape
    tm = TOKEN_TILE
    row = lambda b, i: (b, i, 0)
    return pl.pallas_call(
        _ffn_kernel,
        grid=(bsz, t_len // tm),
        in_specs=[pl.BlockSpec((1, tm, D_MODEL), row),
                  pl.BlockSpec((1, tm, D_MODEL), row),
                  pl.BlockSpec((1, 6, D_MODEL), lambda b, i: (mod_off + mod_stride * b, 0, 0)),
                  _const_spec(w1.shape), _const_spec(w3.shape), _const_spec(w2.shape)],
        out_specs=pl.BlockSpec((1, tm, D_MODEL), row),
        out_shape=jax.ShapeDtypeStruct((bsz, t_len, D_MODEL), F32),
        compiler_params=_params(2),
        name="ffn",
    )(h2, x1, mod_all, w1, w3, w2)


def _trunk(x, mod_all, mod_off, mod_stride, s0_big, ctx_kv, p, wb):
    u_rw, u_na, gates = _in_proj(x, mod_all, mod_off, mod_stride, p["norm1_g"], wb["w_in"])
    o_rw, s_new = _rwkv_branch(u_rw, s0_big, p)
    if ctx_kv is None:
        o_na, k_new, v_new = _na_context(u_na, p["na_q_g"], p["na_k_g"])
    else:
        o_na = _na_latent(u_na, ctx_kv[0], ctx_kv[1], p["na_q_g"], p["na_k_g"], p["na_rpb"])
        k_new = v_new = None
    x1, h2 = _out_proj(x, o_rw, o_na, gates, mod_all, mod_off, mod_stride, p["norm2_g"],
                       wb["w_o_rwkv"], wb["w_o_na"], wb["w_out"])
    y = _ffn(h2, x1, mod_all, mod_off, mod_stride, wb["ffn_w1"], wb["ffn_w3"], wb["ffn_w2"])
    return y, s_new, k_new, v_new


def kernel(x_prompt, x_sample, state_rwkv, cache_na_k, cache_na_v, c, c_ctx, norm1_g, norm2_g, w_ada, b_ada,
           w_in, shift_mu, rw_w0, rw_w_up, rw_a0, rw_a_up, rw_g_up, rw_k_k, rw_k_a, rw_r_k, rw_ln_g, rw_ln_b,
           na_q_g, na_k_g, na_rpb, w_o_rwkv, w_o_na, w_out, ffn_w1, ffn_w3, ffn_w2):
    depth = w_in.shape[0]
    bsz, seq = x_prompt.shape[:2]
    dec = x_sample.shape[0]
    n_vec = 8
    cvecs = jnp.concatenate([c_ctx[None, :], c, jnp.zeros((n_vec - 1 - dec, D_MODEL), F32)], axis=0)
    y_p, y_s = x_prompt, x_sample
    new_s, new_k, new_v = [], [], []
    for l in range(depth):
        p = dict(norm1_g=norm1_g[l], norm2_g=norm2_g[l], shift_mu=shift_mu[l], rw_w0=rw_w0[l],
                 rw_w_up=rw_w_up[l], rw_a0=rw_a0[l], rw_a_up=rw_a_up[l], rw_g_up=rw_g_up[l],
                 rw_k_k=rw_k_k[l], rw_k_a=rw_k_a[l], rw_r_k=rw_r_k[l], rw_ln_g=rw_ln_g[l],
                 rw_ln_b=rw_ln_b[l], na_q_g=na_q_g[l], na_k_g=na_k_g[l], na_rpb=na_rpb[l])
        wb = dict(w_in=w_in[l].astype(BF16), w_o_rwkv=w_o_rwkv[l].astype(BF16), w_o_na=w_o_na[l].astype(BF16),
                  w_out=w_out[l].astype(BF16), ffn_w1=ffn_w1[l].astype(BF16), ffn_w3=ffn_w3[l].astype(BF16),
                  ffn_w2=ffn_w2[l].astype(BF16))
        mod_all = _modulation(cvecs, w_ada[l], b_ada[l])[:1 + dec].reshape(1 + dec, 6, D_MODEL)
        y_p, s_big, k_l, v_l = _trunk(y_p, mod_all, 0, 0, None, None, p, wb)
        new_s.append(_state_from_big(s_big))
        new_k.append(k_l.reshape(bsz, seq, NA_HEADS, HEAD_DIM))
        new_v.append(v_l.reshape(bsz, seq, NA_HEADS, HEAD_DIM))
        ctx_k = cache_na_k[:, l].reshape(dec, -1, NA_WIDTH)
        ctx_v = cache_na_v[:, l].reshape(dec, -1, NA_WIDTH)
        y_s, _, _, _ = _trunk(y_s, mod_all, 1, 1, _state_to_big(state_rwkv[:, l]), (ctx_k, ctx_v), p, wb)
    return (y_p, y_s, jnp.stack(new_s, axis=1), jnp.stack(new_k, axis=1), jnp.stack(new_v, axis=1))
```

```python
import functools

import numpy as np
import jax
import jax.numpy as jnp
from jax import lax
from jax.experimental import pallas as pl
from jax.experimental.pallas import tpu as pltpu

D_MODEL = 1024
GRID_W = 64
HEAD_DIM = 64
RW_HEADS = 8
RW_WIDTH = RW_HEADS * HEAD_DIM
NA_HEADS = 8
NA_WIDTH = NA_HEADS * HEAD_DIM
LORA_DECAY = 64
LORA_ICLR = 64
LORA_GATE = 128
NA_ROWS = 8
NA_COLS = 16
FF_HIDDEN = 2816
RW_COLS = 3 * RW_WIDTH + 2 * LORA_DECAY + 2 * LORA_ICLR + LORA_GATE
NA_IN_COLS = 3 * NA_WIDTH
GATE_COLS = 2 * D_MODEL
RMS_EPS = 1e-6
GN_EPS = 64e-5
L2_EPS = 1e-12
NEG_INF = -1e30

LANES = 128
PAIRS = RW_HEADS // 2
CHUNK = 64
STACK = 2 * CHUNK
UNIT_CHUNKS = 4
TOKEN_TILE = 512
FF_CHUNK = 256
VMEM_LIMIT = 56 * 1024 * 1024

F32 = jnp.float32
BF16 = jnp.bfloat16


def _dot(a, b):
    return jnp.dot(a.astype(BF16), b.astype(BF16), preferred_element_type=F32)


def _dot_nt(a, b):
    return lax.dot_general(a.astype(BF16), b.astype(BF16), (((1,), (1,)), ((), ())),
                           preferred_element_type=F32)


def _split2(x):
    hi = x.astype(BF16)
    lo = (x - hi.astype(F32)).astype(BF16)
    return hi, lo


def _split3(x):
    hi = x.astype(BF16)
    r1 = x - hi.astype(F32)
    mid = r1.astype(BF16)
    lo = (r1 - mid.astype(F32)).astype(BF16)
    return hi, mid, lo


def _dot_exact_rhs(a, b_exact):
    h, m, l = _split3(a)
    d = lambda x: jnp.dot(x, b_exact, preferred_element_type=F32)
    return d(h) + d(m) + d(l)


def _dot_exact_lhs(a_exact, b):
    h, m, l = _split3(b)
    d = lambda x: jnp.dot(a_exact, x, preferred_element_type=F32)
    return d(h) + d(m) + d(l)


def _dot3(a, b):
    ah, al = _split2(a)
    bh, bl = _split2(b)
    d = lambda x, y: jnp.dot(x, y, preferred_element_type=F32)
    return d(ah, bh) + d(al, bh) + d(ah, bl)


def _head_ones():
    r = lax.broadcasted_iota(jnp.int32, (LANES, LANES), 0) // HEAD_DIM
    c = lax.broadcasted_iota(jnp.int32, (LANES, LANES), 1) // HEAD_DIM
    return jnp.where(r == c, 1.0, 0.0).astype(BF16)


def _head_sum(x, ones):
    return _dot_exact_rhs(x, ones)


def _softplus(z):
    return jnp.maximum(z, 0.0) + jnp.log(1.0 + jnp.exp(-jnp.abs(z)))


def _rms_rows(x):
    return x * lax.rsqrt(jnp.mean(x * x, axis=-1, keepdims=True) + RMS_EPS)


def _const_spec(shape):
    nd = len(shape)
    return pl.BlockSpec(shape, lambda *_: (0,) * nd, pipeline_mode=pl.Buffered(1))


def _params(n_axes):
    return pltpu.CompilerParams(dimension_semantics=("arbitrary",) * n_axes,
                                vmem_limit_bytes=VMEM_LIMIT)


def _mod_kernel(c_ref, w_ref, b_ref, o_ref):
    s = c_ref[...]
    s = s * jax.nn.sigmoid(s)
    o_ref[...] = _dot3(s, w_ref[...]) + b_ref[...]


def _modulation(cvecs, w_ada, b_ada):
    n = cvecs.shape[0]
    tn = 1536
    return pl.pallas_call(
        _mod_kernel,
        grid=(6 * D_MODEL // tn,),
        in_specs=[pl.BlockSpec((n, D_MODEL), lambda j: (0, 0)),
                  pl.BlockSpec((D_MODEL, tn), lambda j: (0, j)),
                  pl.BlockSpec((1, tn), lambda j: (0, j))],
        out_specs=pl.BlockSpec((n, tn), lambda j: (0, j)),
        out_shape=jax.ShapeDtypeStruct((n, 6 * D_MODEL), F32),
        compiler_params=_params(1),
        name="modulation",
    )(cvecs, w_ada, b_ada.reshape(1, -1))


def _inproj_kernel(x_ref, mod_ref, g_ref, w_ref, urw_ref, una_ref, gt_ref):
    x = x_ref[0]
    h = _rms_rows(x) * g_ref[...]
    h = (h * (1.0 + mod_ref[0, 1:2, :]) + mod_ref[0, 0:1, :]).astype(BF16)
    d = lambda lo, hi: jnp.dot(h, w_ref[:, lo:hi], preferred_element_type=F32)
    urw_ref[0] = d(0, RW_COLS)
    una_ref[0] = d(RW_COLS, RW_COLS + NA_IN_COLS)
    gt_ref[0] = d(RW_COLS + NA_IN_COLS, RW_COLS + NA_IN_COLS + GATE_COLS)


def _in_proj(x, mod_all, mod_off, mod_stride, norm_g, w_in_bf):
    bsz, t_len, _ = x.shape
    tm = TOKEN_TILE
    row = lambda b, i: (b, i, 0)
    return pl.pallas_call(
        _inproj_kernel,
        grid=(bsz, t_len // tm),
        in_specs=[pl.BlockSpec((1, tm, D_MODEL), row),
                  pl.BlockSpec((1, 6, D_MODEL), lambda b, i: (mod_off + mod_stride * b, 0, 0)),
                  _const_spec((1, D_MODEL)),
                  _const_spec(w_in_bf.shape)],
        out_specs=[pl.BlockSpec((1, tm, RW_COLS), row),
                   pl.BlockSpec((1, tm, NA_IN_COLS), row),
                   pl.BlockSpec((1, tm, GATE_COLS), row)],
        out_shape=[jax.ShapeDtypeStruct((bsz, t_len, RW_COLS), F32),
                   jax.ShapeDtypeStruct((bsz, t_len, NA_IN_COLS), F32),
                   jax.ShapeDtypeStruct((bsz, t_len, GATE_COLS), F32)],
        compiler_params=_params(2),
        name="in_proj",
    )(x, mod_all, norm_g.reshape(1, -1), w_in_bf)


def _shift(x, mu):
    t_len = x.shape[0]
    row = lax.broadcasted_iota(jnp.int32, x.shape, 0)
    prev = jnp.where(row == 0, 0.0, pltpu.roll(x, 1, 0))
    nxt = jnp.where(row == t_len - 1, 0.0, pltpu.roll(x, t_len - 1, 0))
    return x + mu[0:1, :] * (prev - x) + mu[1:2, :] * (nxt - x)


def _stack_heads(x, lane_lo):
    return jnp.concatenate([x * lane_lo, x * (1.0 - lane_lo)], axis=0)


def _wkv_intra(units, consts):
    tri, mask_s, mask_i, eye, lane_lo = consts
    stack = lambda z: _stack_heads(z, lane_lo)

    cums = [_dot_exact_lhs(tri[int(u[6])], u[1]) for u in units]

    prep = []
    for (r, lw, kd, v, kk, b, reverse), cum in zip(units, cums):
        mid_row = CHUNK // 2 if reverse else CHUNK // 2 - 1
        tot_row = 0 if reverse else CHUNK - 1
        a = -kk
        ex = cum - lw
        mid = cum[mid_row:mid_row + 1, :]
        tot = cum[tot_row:tot_row + 1, :]
        up = jnp.exp(cum - mid)
        dn = jnp.exp(mid - cum)
        tail = jnp.exp(tot - cum)
        prep.append(dict(
            at_m=stack(a * jnp.exp(ex - mid)).astype(BF16),
            rt_m=stack(r * up).astype(BF16),
            btkt=jnp.concatenate([stack(b * dn), stack(kd * dn)], axis=0).astype(BF16),
            a_e=stack(a * jnp.exp(ex)),
            r_e=stack(r * jnp.exp(cum)),
            bk_t=jnp.concatenate([stack(b * tail), stack(kd * tail)], axis=0).T.astype(BF16),
            vv=stack(v).astype(BF16),
            diag=jnp.where(eye, jnp.exp(tot), 0.0),
            rev=int(reverse)))

    ntd = lambda x, y: lax.dot_general(x, y, (((1,), (1,)), ((), ())), preferred_element_type=F32)
    mm = lambda x, y: jnp.dot(x, y, preferred_element_type=F32)
    top = [mask_s[p["rev"]] * ntd(p["at_m"], p["btkt"]) for p in prep]
    bot = [(mask_i[p["rev"]] * ntd(p["rt_m"], p["btkt"])).astype(BF16) for p in prep]

    ms = [t[:, :LANES] for t in top]
    pws = [m.astype(BF16) for m in ms]
    pw2 = [mm(pw, pw) for pw in pws]
    steps = CHUNK.bit_length() - 1
    for j in range(1, steps):
        pws = [q.astype(BF16) for q in pw2]
        if j < steps - 1:
            both = [mm(pw, jnp.concatenate([m.astype(BF16), pw], axis=1)) for pw, m in zip(pws, ms)]
            ms = [m + q + b[:, :LANES] for m, q, b in zip(ms, pw2, both)]
            pw2 = [b[:, LANES:] for b in both]
        else:
            ms = [m + q + mm(pw, m.astype(BF16)) for m, q, pw in zip(ms, pw2, pws)]
    x0 = [jnp.concatenate([p["a_e"], mm(t[:, LANES:].astype(BF16), p["vv"])], axis=1) for p, t in zip(prep, top)]
    xs = [x + mm(m.astype(BF16), x.astype(BF16)) for x, m in zip(x0, ms)]

    out = []
    zeros = jnp.zeros((STACK, LANES), BF16)
    for p, x, bt in zip(prep, xs, bot):
        rhs = jnp.concatenate([x.astype(BF16), jnp.concatenate([zeros, p["vv"]], axis=1)], axis=0)
        lhs = jnp.concatenate([bt, p["bk_t"]], axis=0)
        res = mm(lhs, rhs)
        lhs2 = res[:, :LANES] + jnp.concatenate([p["r_e"], p["diag"]], axis=0)
        out.append((lhs2.astype(BF16), res[:, LANES:]))
    return out


def _wkv_constants():
    lane_head = np.arange(LANES) // HEAD_DIM
    ones = (lane_head[:, None] == lane_head[None, :]).astype(np.float32)
    t = np.arange(CHUNK)
    tri = np.stack([t[None, :] <= t[:, None], t[None, :] >= t[:, None]]).astype(np.float32)
    s = np.arange(STACK)
    same = (s[:, None] // CHUNK) == (s[None, :] // CHUNK)
    rs, cs = s[:, None], s[None, :]
    masks = np.stack([same & (cs < rs), same & (cs > rs), same & (cs <= rs), same & (cs >= rs)]).astype(np.float32)
    masks = np.concatenate([masks, masks], axis=-1)
    return jnp.asarray(ones, BF16), jnp.asarray(tri, BF16), jnp.asarray(masks, F32)


def _rwkv_kernel(*refs, t_len, has_s0):
    (r_ref, k_ref, v_ref, lo_ref, mur_ref, muk_ref, muv_ref, mul_ref, w0_ref, a0_ref, wup_ref, aup_ref,
     gup_ref, kk_ref, ka_ref, rk_ref, lng_ref, lnb_ref, ones_ref, tri_ref, mask_ref) = refs[:21]
    pos = 21
    s0_ref = None
    if has_s0:
        s0_ref = refs[pos]
        pos += 1
    o_ref, sn_ref = refs[pos], refs[pos + 1]
    (r_s, v_s, kk_s, b0_s, b1_s, lw0_s, lw1_s, kd0_s, kd1_s, gate_s, bonus_s, yf_s, yb_s,
     lhs_s, add_s, st_s) = refs[pos + 2:]

    ones = ones_ref[...]
    lane = lax.broadcasted_iota(jnp.int32, (1, LANES), 1)
    lo_half = lane < HEAD_DIM

    r = _shift(r_ref[0], mur_ref[...])
    k = _shift(k_ref[0], muk_ref[...])
    v = _shift(v_ref[0], muv_ref[...])
    lo = _shift(lo_ref[0], mul_ref[...])
    wd = jnp.tanh(lo[:, 0:LANES])
    ad = lo[:, LANES:2 * LANES]
    gd = lo[:, 2 * LANES:3 * LANES]
    kk = k * kk_ref[...]
    kk = kk * lax.rsqrt(_head_sum(kk * kk, ones) + L2_EPS)
    kdirs = []
    for e, (lw_s, kd_s, b_s) in enumerate(((lw0_s, kd0_s, b0_s), (lw1_s, kd1_s, b1_s))):
        sel = lo_half if e == 0 else jnp.logical_not(lo_half)
        w_lin = w0_ref[e:e + 1, :] + _dot3(jnp.where(sel, wd, 0.0), wup_ref[...])
        w_soft = -_softplus(-w_lin) - 0.5
        lw_s[...] = -jnp.exp(w_soft)
        iclr = jax.nn.sigmoid(a0_ref[e:e + 1, :] + _dot(jnp.where(sel, ad, 0.0), aup_ref[...]))
        kd = k * (1.0 + (iclr - 1.0) * ka_ref[...])
        kd_s[...] = kd
        b_s[...] = kk * iclr
        kdirs.append(kd)
    gate_s[...] = _dot(jax.nn.sigmoid(gd), gup_ref[...])
    bonus_s[...] = _head_sum(r * (0.5 * (kdirs[0] + kdirs[1])) * rk_ref[...], ones) * v
    r_s[...] = r
    v_s[...] = v
    kk_s[...] = kk

    n_chunks = t_len // CHUNK
    rs = lax.broadcasted_iota(jnp.int32, (STACK, STACK), 0)
    cs = lax.broadcasted_iota(jnp.int32, (STACK, STACK), 1)
    consts = ((tri_ref[0], tri_ref[1]), (mask_ref[0], mask_ref[1]), (mask_ref[2], mask_ref[3]),
              rs == cs, jnp.where(lo_half, 1.0, 0.0))
    dirs = ((lw0_s, kd0_s, b0_s), (lw1_s, kd1_s, b1_s))

    def intra_body(it, carry):
        units, ids = [], []
        for cc in range(UNIT_CHUNKS):
            c = it * UNIT_CHUNKS + cc
            rows = pl.ds(pl.multiple_of(c * CHUNK, CHUNK), CHUNK)
            for e, (lw_s, kd_s, b_s) in enumerate(dirs):
                units.append((r_s[rows, :], lw_s[rows, :], kd_s[rows, :], v_s[rows, :], kk_s[rows, :],
                              b_s[rows, :], e == 1))
                ids.append(e * n_chunks + c)
        for uid, (lhs, add) in zip(ids, _wkv_intra(units, consts)):
            lhs_s[uid] = lhs
            add_s[uid] = add
        return carry

    lax.fori_loop(0, n_chunks // UNIT_CHUNKS, intra_body, 0)

    if has_s0:
        st_s[0] = s0_ref[0, 0, 0].T
        st_s[1] = s0_ref[0, 1, 0].T
    else:
        st_s[...] = jnp.zeros((2, LANES, LANES), F32)

    def state_body(it, carry):
        cf = it
        cb = n_chunks - 1 - it
        res_f = jnp.dot(lhs_s[cf], st_s[0].astype(BF16), preferred_element_type=F32) + add_s[cf]
        res_b = jnp.dot(lhs_s[n_chunks + cb], st_s[1].astype(BF16), preferred_element_type=F32) + add_s[n_chunks + cb]
        yf_s[pl.ds(pl.multiple_of(cf * CHUNK, CHUNK), CHUNK), :] = res_f[:CHUNK] + res_f[CHUNK:STACK]
        yb_s[pl.ds(pl.multiple_of(cb * CHUNK, CHUNK), CHUNK), :] = res_b[:CHUNK] + res_b[CHUNK:STACK]
        st_s[0] = res_f[STACK:]
        st_s[1] = res_b[STACK:]
        return carry

    lax.fori_loop(0, n_chunks, state_body, 0)
    sn_ref[0, 0, 0] = st_s[0].T
    sn_ref[0, 1, 0] = st_s[1].T

    y = yf_s[...] + yb_s[...]
    inv_d = 1.0 / HEAD_DIM
    mean = _head_sum(y, ones) * inv_d
    dlt = y - mean
    var = _head_sum(dlt * dlt, ones) * inv_d
    yn = dlt * lax.rsqrt(var + GN_EPS) * lng_ref[...] + lnb_ref[...]
    o_ref[0] = (yn + bonus_s[...]) * gate_s[...]


def _rwkv_branch(u_rw, s0_big, p):
    bsz, t_len, _ = u_rw.shape
    has_s0 = s0_big is not None
    seg = RW_WIDTH // LANES
    tok = lambda off: pl.BlockSpec((1, t_len, LANES), lambda b, j: (b, 0, off + j))
    mu = lambda off: pl.BlockSpec((2, LANES), lambda b, j: (0, off + j))
    vec2 = pl.BlockSpec((2, LANES), lambda b, j: (0, j))
    vec1 = pl.BlockSpec((1, LANES), lambda b, j: (0, j))
    mat = pl.BlockSpec((LANES, LANES), lambda b, j: (0, j))
    lora_w = 3 * LANES
    lora_blk = 3 * RW_WIDTH // lora_w
    in_specs = [tok(0), tok(seg), tok(2 * seg),
                pl.BlockSpec((1, t_len, lora_w), lambda b, j: (b, 0, lora_blk)),
                mu(0), mu(seg), mu(2 * seg),
                pl.BlockSpec((2, lora_w), lambda b, j: (0, lora_blk)),
                vec2, vec2, mat, mat, mat, vec1, vec1, vec1, vec1, vec1,
                _const_spec((LANES, LANES)), _const_spec((2, CHUNK, CHUNK)), _const_spec((4, STACK, 2 * STACK))]
    args = [u_rw, u_rw, u_rw, u_rw, p["shift_mu"], p["shift_mu"], p["shift_mu"], p["shift_mu"],
            p["rw_w0"], p["rw_a0"],
            p["rw_w_up"].reshape(2 * LORA_DECAY, RW_WIDTH), p["rw_a_up"].reshape(2 * LORA_ICLR, RW_WIDTH),
            p["rw_g_up"], p["rw_k_k"].reshape(1, -1), p["rw_k_a"].reshape(1, -1),
            p["rw_r_k"].reshape(1, -1), p["rw_ln_g"].reshape(1, -1), p["rw_ln_b"].reshape(1, -1),
            *_wkv_constants()]
    st_spec = pl.BlockSpec((1, 2, 1, LANES, LANES), lambda b, j: (b, 0, j, 0, 0))
    if has_s0:
        in_specs.append(st_spec)
        args.append(s0_big)
    n_units = 2 * (t_len // CHUNK)
    scratch = [pltpu.VMEM((t_len, LANES), F32) for _ in range(13)]
    scratch += [pltpu.VMEM((n_units, 2 * STACK, LANES), BF16), pltpu.VMEM((n_units, 2 * STACK, LANES), F32),
                pltpu.VMEM((2, LANES, LANES), F32)]
    o_rw, s_new = pl.pallas_call(
        functools.partial(_rwkv_kernel, t_len=t_len, has_s0=has_s0),
        grid=(bsz, PAIRS),
        in_specs=in_specs,
        out_specs=[pl.BlockSpec((1, t_len, LANES), lambda b, j: (b, 0, j)), st_spec],
        out_shape=[jax.ShapeDtypeStruct((bsz, t_len, RW_WIDTH), F32),
                   jax.ShapeDtypeStruct((bsz, 2, PAIRS, LANES, LANES), F32)],
        scratch_shapes=scratch,
        compiler_params=_params(2),
        name="rwkv_branch",
    )(*args)
    return o_rw, s_new


def _state_to_big(s0):
    bsz = s0.shape[0]
    x = s0.reshape(bsz, 2, PAIRS, 2, HEAD_DIM, HEAD_DIM)
    z = jnp.zeros_like(x[:, :, :, 0])
    top = jnp.concatenate([x[:, :, :, 0], z], axis=-1)
    bot = jnp.concatenate([z, x[:, :, :, 1]], axis=-1)
    return jnp.concatenate([top, bot], axis=-2)


def _state_from_big(s_big):
    bsz = s_big.shape[0]
    h0 = s_big[:, :, :, :HEAD_DIM, :HEAD_DIM]
    h1 = s_big[:, :, :, HEAD_DIM:, HEAD_DIM:]
    return jnp.stack([h0, h1], axis=3).reshape(bsz, 2, RW_HEADS, HEAD_DIM, HEAD_DIM)


def _qk_norm(t, g, ones):
    ms = _head_sum(t * t, ones) * (1.0 / HEAD_DIM)
    return t * lax.rsqrt(ms + RMS_EPS) * g


def _nt(x, y):
    return lax.dot_general(x, y, (((1,), (1,)), ((), ())), preferred_element_type=F32)


def _na_ctx_kernel(q_ref, k_ref, v_ref, qg_ref, kg_ref, o_ref, kn_ref, vc_ref):
    ones = _head_ones()
    lo_half = lax.broadcasted_iota(jnp.int32, (1, LANES), 1) < HEAD_DIM
    lo = jnp.where(lo_half, 1.0, 0.0)
    t_len = q_ref.shape[1]
    scale = HEAD_DIM ** -0.5
    qs, ks, vs = [], [], []
    for j in range(NA_WIDTH // LANES):
        cols = slice(j * LANES, (j + 1) * LANES)
        qn = _qk_norm(q_ref[0, :, cols], qg_ref[...], ones)
        kn = _qk_norm(k_ref[0, :, cols], kg_ref[...], ones)
        v = v_ref[0, :, cols]
        kn_ref[0, :, cols] = kn
        vc_ref[0, :, cols] = v
        qs.append(jnp.concatenate([qn * lo, qn * (1.0 - lo)], axis=0).astype(BF16))
        ks.append(kn.astype(BF16))
        vs.append(v.astype(BF16))
    logits = [_nt(q, k) * scale for q, k in zip(qs, ks)]
    ms = [jnp.max(s, axis=-1, keepdims=True) for s in logits]
    ps = [jnp.exp(s - m) for s, m in zip(logits, ms)]
    ls = [jnp.sum(p, axis=-1, keepdims=True) for p in ps]
    outs = [jnp.dot(p.astype(BF16), v, preferred_element_type=F32) / l for p, v, l in zip(ps, vs, ls)]
    for j, o in enumerate(outs):
        o_ref[0, :, j * LANES:(j + 1) * LANES] = jnp.where(lo_half, o[:t_len], o[t_len:])


def _na_context(u_na, q_g, k_g):
    bsz, t_len, _ = u_na.shape
    tok = lambda seg: pl.BlockSpec((1, t_len, NA_WIDTH), lambda b: (b, 0, seg))
    out_blk = pl.BlockSpec((1, t_len, NA_WIDTH), lambda b: (b, 0, 0))
    g2 = lambda g: jnp.tile(g.reshape(1, HEAD_DIM), (1, 2))
    shp = jax.ShapeDtypeStruct((bsz, t_len, NA_WIDTH), F32)
    return pl.pallas_call(
        _na_ctx_kernel,
        grid=(bsz,),
        in_specs=[tok(0), tok(1), tok(2), _const_spec((1, LANES)), _const_spec((1, LANES))],
        out_specs=[out_blk, out_blk, out_blk],
        out_shape=[shp, shp, shp],
        compiler_params=_params(1),
        name="na_context",
    )(u_na, u_na, u_na, g2(q_g), g2(k_g))


NA_ROW_ILP = 2


def _na_lat_kernel(q_ref, k_ref, v_ref, kc_ref, vc_ref, qg_ref, kg_ref, tab_ref, o_ref,
                   q0_s, q1_s, kn_s, v_s, kc_s, vc_s, *, rows, kr):
    ones = _head_ones()
    lo_half = lax.broadcasted_iota(jnp.int32, (1, LANES), 1) < HEAD_DIM
    lo = jnp.where(lo_half, 1.0, 0.0)
    qn = _qk_norm(q_ref[0], qg_ref[...], ones)
    q0_s[...] = (qn * lo).astype(BF16)
    q1_s[...] = (qn * (1.0 - lo)).astype(BF16)
    kn_s[...] = _qk_norm(k_ref[0], kg_ref[...], ones).astype(BF16)
    v_s[...] = v_ref[0].astype(BF16)
    kc_s[...] = kc_ref[0].astype(BF16)
    vc_s[...] = vc_ref[0].astype(BF16)
    scale = HEAD_DIM ** -0.5
    win = kr * GRID_W

    def body(it, carry):
        qs, k_rows, q_rows, biases = [], [], [], []
        for s in range(NA_ROW_ILP):
            i = it * NA_ROW_ILP + s
            r0 = jnp.clip(i - kr // 2, 0, rows - kr)
            d0 = r0 - i + (NA_ROWS - 1)
            qr = pl.ds(pl.multiple_of(i * GRID_W, GRID_W), GRID_W)
            q_rows.append(qr)
            k_rows.append(pl.ds(pl.multiple_of(r0 * GRID_W, GRID_W), win))
            qs.append(jnp.concatenate([q0_s[qr, :], q1_s[qr, :]], axis=0))
            biases.append(jnp.concatenate(
                [jnp.concatenate([tab_ref[h, d0 + 2 * m] for m in range(kr // 2)], axis=1) for h in range(2)],
                axis=0))
        lw = [_nt(q, kn_s[kr_, :]) * scale + b for q, kr_, b in zip(qs, k_rows, biases)]
        lc = [_nt(q, kc_s[...]) * scale for q in qs]
        ms = [jnp.maximum(jnp.max(a, axis=-1, keepdims=True), jnp.max(c, axis=-1, keepdims=True))
              for a, c in zip(lw, lc)]
        pw = [jnp.exp(a - m) for a, m in zip(lw, ms)]
        pc = [jnp.exp(c - m) for c, m in zip(lc, ms)]
        ls = [jnp.sum(a, axis=-1, keepdims=True) + jnp.sum(c, axis=-1, keepdims=True) for a, c in zip(pw, pc)]
        outs = [(jnp.dot(a.astype(BF16), v_s[kr_, :], preferred_element_type=F32)
                 + jnp.dot(c.astype(BF16), vc_s[...], preferred_element_type=F32)) / l
                for a, c, kr_, l in zip(pw, pc, k_rows, ls)]
        for qr, o in zip(q_rows, outs):
            o_ref[0, qr, :] = jnp.where(lo_half, o[:GRID_W], o[GRID_W:])
        return carry

    lax.fori_loop(0, rows // NA_ROW_ILP, body, 0)


def _latent_bias_table(rpb):
    qc = np.arange(GRID_W)[:, None]
    kc = np.arange(GRID_W)[None, :]
    ws = np.clip(qc - NA_COLS // 2, 0, GRID_W - NA_COLS)
    valid = (kc >= ws) & (kc < ws + NA_COLS)
    dc = np.clip(kc - qc, -(NA_COLS - 1), NA_COLS - 1) + NA_COLS - 1
    onehot = (dc[None] == np.arange(2 * NA_COLS - 1)[:, None, None]).astype(np.float32)
    cb = jnp.einsum("hdc,cqk->hdqk", rpb, jnp.asarray(onehot), precision=lax.Precision.HIGHEST)
    cb = jnp.where(valid[None, None], cb, NEG_INF)
    return jnp.concatenate([cb[:, :-1], cb[:, 1:]], axis=-1)


def _na_latent(u_na, k_ctx, v_ctx, q_g, k_g, rpb):
    bsz, t_len, _ = u_na.shape
    rows = t_len // GRID_W
    kr = min(NA_ROWS, rows)
    assert kr % 2 == 0 and rows % NA_ROW_ILP == 0
    ctx_len = k_ctx.shape[1]
    seg = NA_WIDTH // LANES
    tok = lambda off: pl.BlockSpec((1, t_len, LANES), lambda b, j: (b, 0, off + j))
    ctx = pl.BlockSpec((1, ctx_len, LANES), lambda b, j: (b, 0, j))
    g2 = lambda g: jnp.tile(g.reshape(1, HEAD_DIM), (1, 2))
    tab = _latent_bias_table(rpb)
    tok_s = pltpu.VMEM((t_len, LANES), BF16)
    ctx_s = pltpu.VMEM((ctx_len, LANES), BF16)
    return pl.pallas_call(
        functools.partial(_na_lat_kernel, rows=rows, kr=kr),
        grid=(bsz, seg),
        in_specs=[tok(0), tok(seg), tok(2 * seg), ctx, ctx,
                  _const_spec((1, LANES)), _const_spec((1, LANES)),
                  pl.BlockSpec((2, 2 * NA_ROWS - 2, GRID_W, 2 * GRID_W), lambda b, j: (j, 0, 0, 0))],
        out_specs=pl.BlockSpec((1, t_len, LANES), lambda b, j: (b, 0, j)),
        out_shape=jax.ShapeDtypeStruct((bsz, t_len, NA_WIDTH), F32),
        scratch_shapes=[tok_s, tok_s, tok_s, tok_s, ctx_s, ctx_s],
        compiler_params=_params(2),
        name="na_latent",
    )(u_na, u_na, u_na, k_ctx, v_ctx, g2(q_g), g2(k_g), tab)


def _outproj_kernel(x_ref, orw_ref, ona_ref, gt_ref, mod_ref, g_ref, wor_ref, won_ref, wout_ref, x1_ref, h2_ref):
    g_rw = jax.nn.sigmoid(gt_ref[0, :, :D_MODEL])
    g_na = jax.nn.sigmoid(gt_ref[0, :, D_MODEL:])
    merged = g_rw * _dot(orw_ref[0], wor_ref[...]) + g_na * _dot(ona_ref[0], won_ref[...])
    x1 = x_ref[0] + mod_ref[0, 2:3, :] * _dot(merged, wout_ref[...])
    x1_ref[0] = x1
    h2 = _rms_rows(x1) * g_ref[...]
    h2_ref[0] = (h2 * (1.0 + mod_ref[0, 4:5, :]) + mod_ref[0, 3:4, :]).astype(BF16)


def _out_proj(x, o_rw, o_na, gates, mod_all, mod_off, mod_stride, norm_g, w_or, w_on, w_out):
    bsz, t_len, _ = x.shape
    tm = TOKEN_TILE
    row = lambda b, i: (b, i, 0)
    return pl.pallas_call(
        _outproj_kernel,
        grid=(bsz, t_len // tm),
        in_specs=[pl.BlockSpec((1, tm, D_MODEL), row),
                  pl.BlockSpec((1, tm, RW_WIDTH), row),
                  pl.BlockSpec((1, tm, NA_WIDTH), row),
                  pl.BlockSpec((1, tm, GATE_COLS), row),
                  pl.BlockSpec((1, 6, D_MODEL), lambda b, i: (mod_off + mod_stride * b, 0, 0)),
                  _const_spec((1, D_MODEL)),
                  _const_spec(w_or.shape), _const_spec(w_on.shape), _const_spec(w_out.shape)],
        out_specs=[pl.BlockSpec((1, tm, D_MODEL), row), pl.BlockSpec((1, tm, D_MODEL), row)],
        out_shape=[jax.ShapeDtypeStruct((bsz, t_len, D_MODEL), F32),
                   jax.ShapeDtypeStruct((bsz, t_len, D_MODEL), BF16)],
        compiler_params=_params(2),
        name="out_proj",
    )(x, o_rw, o_na, gates, mod_all, norm_g.reshape(1, -1), w_or, w_on, w_out)


def _ffn_kernel(h2_ref, x1_ref, mod_ref, w1_ref, w3_ref, w2_ref, y_ref):
    h2 = h2_ref[0]
    acc = jnp.zeros((h2.shape[0], D_MODEL), F32)
    for c in range(FF_HIDDEN // FF_CHUNK):
        cols = slice(c * FF_CHUNK, (c + 1) * FF_CHUNK)
        a = jnp.dot(h2, w1_ref[:, cols], preferred_element_type=F32)
        b = jnp.dot(h2, w3_ref[:, cols], preferred_element_type=F32)
        hh = (a * jax.nn.sigmoid(a) * b).astype(BF16)
        acc = acc + jnp.dot(hh, w2_ref[cols, :], preferred_element_type=F32)
    y_ref[0] = x1_ref[0] + mod_ref[0, 5:6, :] * acc


def _ffn(h2, x1, mod_all, mod_off, mod_stride, w1, w3, w2):
    bsz, t_len, _ = x1.shape
    tm = TOKEN_TILE
    row = lambda b, i: (b, i, 0)
    return pl.pallas_call(
        _ffn_kernel,
        grid=(bsz, t_len // tm),
        in_specs=[pl.BlockSpec((1, tm, D_MODEL), row),
                  pl.BlockSpec((1, tm, D_MODEL), row),
                  pl.BlockSpec((1, 6, D_MODEL), lambda b, i: (mod_off + mod_stride * b, 0, 0)),
                  _const_spec(w1.shape), _const_spec(w3.shape), _const_spec(w2.shape)],
        out_specs=pl.BlockSpec((1, tm, D_MODEL), row),
        out_shape=jax.ShapeDtypeStruct((bsz, t_len, D_MODEL), F32),
        compiler_params=_params(2),
        name="ffn",
    )(h2, x1, mod_all, w1, w3, w2)


def _trunk(x, mod_all, mod_off, mod_stride, s0_big, ctx_kv, p, wb):
    shape = x.shape
    flat = (lambda t: t.reshape(1, -1, t.shape[-1])) if mod_stride == 0 else (lambda t: t)
    unflat = lambda t: t.reshape(shape[0], shape[1], t.shape[-1])
    u_rw, u_na, gates = _in_proj(flat(x), mod_all, mod_off, mod_stride, p["norm1_g"], wb["w_in"])
    u_rw, u_na = unflat(u_rw), unflat(u_na)
    o_rw, s_new = _rwkv_branch(u_rw, s0_big, p)
    if ctx_kv is None:
        o_na, k_new, v_new = _na_context(u_na, p["na_q_g"], p["na_k_g"])
    else:
        o_na = _na_latent(u_na, ctx_kv[0], ctx_kv[1], p["na_q_g"], p["na_k_g"], p["na_rpb"])
        k_new = v_new = None
    x1, h2 = _out_proj(flat(x), flat(o_rw), flat(o_na), gates, mod_all, mod_off, mod_stride, p["norm2_g"],
                       wb["w_o_rwkv"], wb["w_o_na"], wb["w_out"])
    y = _ffn(h2, x1, mod_all, mod_off, mod_stride, wb["ffn_w1"], wb["ffn_w3"], wb["ffn_w2"])
    return unflat(y), s_new, k_new, v_new


def kernel(x_prompt, x_sample, state_rwkv, cache_na_k, cache_na_v, c, c_ctx, norm1_g, norm2_g, w_ada, b_ada,
           w_in, shift_mu, rw_w0, rw_w_up, rw_a0, rw_a_up, rw_g_up, rw_k_k, rw_k_a, rw_r_k, rw_ln_g, rw_ln_b,
           na_q_g, na_k_g, na_rpb, w_o_rwkv, w_o_na, w_out, ffn_w1, ffn_w3, ffn_w2):
    depth = w_in.shape[0]
    bsz, seq = x_prompt.shape[:2]
    dec = x_sample.shape[0]
    n_vec = 8
    cvecs = jnp.concatenate([c_ctx[None, :], c, jnp.zeros((n_vec - 1 - dec, D_MODEL), F32)], axis=0)
    y_p, y_s = x_prompt, x_sample
    new_s, new_k, new_v = [], [], []
    for l in range(depth):
        p = dict(norm1_g=norm1_g[l], norm2_g=norm2_g[l], shift_mu=shift_mu[l], rw_w0=rw_w0[l],
                 rw_w_up=rw_w_up[l], rw_a0=rw_a0[l], rw_a_up=rw_a_up[l], rw_g_up=rw_g_up[l],
                 rw_k_k=rw_k_k[l], rw_k_a=rw_k_a[l], rw_r_k=rw_r_k[l], rw_ln_g=rw_ln_g[l],
                 rw_ln_b=rw_ln_b[l], na_q_g=na_q_g[l], na_k_g=na_k_g[l], na_rpb=na_rpb[l])
        wb = dict(w_in=w_in[l].astype(BF16), w_o_rwkv=w_o_rwkv[l].astype(BF16), w_o_na=w_o_na[l].astype(BF16),
                  w_out=w_out[l].astype(BF16), ffn_w1=ffn_w1[l].astype(BF16), ffn_w3=ffn_w3[l].astype(BF16),
                  ffn_w2=ffn_w2[l].astype(BF16))
        mod_all = _modulation(cvecs, w_ada[l], b_ada[l])[:1 + dec].reshape(1 + dec, 6, D_MODEL)
        y_p, s_big, k_l, v_l = _trunk(y_p, mod_all, 0, 0, None, None, p, wb)
        new_s.append(_state_from_big(s_big))
        new_k.append(k_l.reshape(bsz, seq, NA_HEADS, HEAD_DIM))
        new_v.append(v_l.reshape(bsz, seq, NA_HEADS, HEAD_DIM))
        ctx_k = cache_na_k[:, l].reshape(dec, -1, NA_WIDTH)
        ctx_v = cache_na_v[:, l].reshape(dec, -1, NA_WIDTH)
        y_s, _, _, _ = _trunk(y_s, mod_all, 1, 1, _state_to_big(state_rwkv[:, l]), (ctx_k, ctx_v), p, wb)
    return (y_p, y_s, jnp.stack(new_s, axis=1), jnp.stack(new_k, axis=1), jnp.stack(new_v, axis=1))
```

```python
import functools

import numpy as np
import jax
import jax.numpy as jnp
from jax import lax
from jax.experimental import pallas as pl
from jax.experimental.pallas import tpu as pltpu

D_MODEL = 1024
GRID_W = 64
HEAD_DIM = 64
RW_HEADS = 8
RW_WIDTH = RW_HEADS * HEAD_DIM
NA_HEADS = 8
NA_WIDTH = NA_HEADS * HEAD_DIM
LORA_DECAY = 64
LORA_ICLR = 64
LORA_GATE = 128
NA_ROWS = 8
NA_COLS = 16
FF_HIDDEN = 2816
RW_COLS = 3 * RW_WIDTH + 2 * LORA_DECAY + 2 * LORA_ICLR + LORA_GATE
NA_IN_COLS = 3 * NA_WIDTH
GATE_COLS = 2 * D_MODEL
RMS_EPS = 1e-6
GN_EPS = 64e-5
L2_EPS = 1e-12
NEG_INF = -1e30

LANES = 128
PAIRS = RW_HEADS // 2
CHUNK = 64
STACK = 2 * CHUNK
RWKV_PAIRS_CTX = 4
RWKV_PAIRS_LAT = 2
UNIT_CHUNKS = 4
TOKEN_TILE = 512
FF_CHUNK = 256
VMEM_LIMIT = 56 * 1024 * 1024

F32 = jnp.float32
BF16 = jnp.bfloat16


def _dot(a, b):
    return jnp.dot(a.astype(BF16), b.astype(BF16), preferred_element_type=F32)


def _dot_nt(a, b):
    return lax.dot_general(a.astype(BF16), b.astype(BF16), (((1,), (1,)), ((), ())),
                           preferred_element_type=F32)


def _split2(x):
    hi = x.astype(BF16)
    lo = (x - hi.astype(F32)).astype(BF16)
    return hi, lo


def _split3(x):
    hi = x.astype(BF16)
    r1 = x - hi.astype(F32)
    mid = r1.astype(BF16)
    lo = (r1 - mid.astype(F32)).astype(BF16)
    return hi, mid, lo


def _dot_exact_rhs(a, b_exact):
    h, m, l = _split3(a)
    d = lambda x: jnp.dot(x, b_exact, preferred_element_type=F32)
    return d(h) + d(m) + d(l)


def _dot_exact_lhs(a_exact, b):
    h, m, l = _split3(b)
    d = lambda x: jnp.dot(a_exact, x, preferred_element_type=F32)
    return d(h) + d(m) + d(l)


def _dot3(a, b):
    ah, al = _split2(a)
    bh, bl = _split2(b)
    d = lambda x, y: jnp.dot(x, y, preferred_element_type=F32)
    return d(ah, bh) + d(al, bh) + d(ah, bl)


def _head_ones():
    r = lax.broadcasted_iota(jnp.int32, (LANES, LANES), 0) // HEAD_DIM
    c = lax.broadcasted_iota(jnp.int32, (LANES, LANES), 1) // HEAD_DIM
    return jnp.where(r == c, 1.0, 0.0).astype(BF16)


def _head_sum(x, ones):
    return _dot_exact_rhs(x, ones)


def _softplus(z):
    return jnp.maximum(z, 0.0) + jnp.log(1.0 + jnp.exp(-jnp.abs(z)))


def _rms_rows(x):
    return x * lax.rsqrt(jnp.mean(x * x, axis=-1, keepdims=True) + RMS_EPS)


def _const_spec(shape):
    nd = len(shape)
    return pl.BlockSpec(shape, lambda *_: (0,) * nd, pipeline_mode=pl.Buffered(1))


def _params(n_axes):
    return pltpu.CompilerParams(dimension_semantics=("arbitrary",) * n_axes,
                                vmem_limit_bytes=VMEM_LIMIT)


def _mod_kernel(c_ref, w_ref, b_ref, o_ref):
    s = c_ref[...]
    s = s * jax.nn.sigmoid(s)
    o_ref[...] = _dot3(s, w_ref[...]) + b_ref[...]


def _modulation(cvecs, w_ada, b_ada):
    n = cvecs.shape[0]
    tn = 1536
    return pl.pallas_call(
        _mod_kernel,
        grid=(6 * D_MODEL // tn,),
        in_specs=[pl.BlockSpec((n, D_MODEL), lambda j: (0, 0)),
                  pl.BlockSpec((D_MODEL, tn), lambda j: (0, j)),
                  pl.BlockSpec((1, tn), lambda j: (0, j))],
        out_specs=pl.BlockSpec((n, tn), lambda j: (0, j)),
        out_shape=jax.ShapeDtypeStruct((n, 6 * D_MODEL), F32),
        compiler_params=_params(1),
        name="modulation",
    )(cvecs, w_ada, b_ada.reshape(1, -1))


def _inproj_kernel(x_ref, mod_ref, g_ref, w_ref, urw_ref, una_ref, gt_ref):
    x = x_ref[0]
    h = _rms_rows(x) * g_ref[...]
    h = (h * (1.0 + mod_ref[0, 1:2, :]) + mod_ref[0, 0:1, :]).astype(BF16)
    d = lambda lo, hi: jnp.dot(h, w_ref[:, lo:hi], preferred_element_type=F32)
    urw_ref[0] = d(0, RW_COLS)
    una_ref[0] = d(RW_COLS, RW_COLS + NA_IN_COLS)
    gt_ref[0] = d(RW_COLS + NA_IN_COLS, RW_COLS + NA_IN_COLS + GATE_COLS)


def _in_proj(x, mod_all, mod_off, mod_stride, norm_g, w_in_bf):
    bsz, t_len, _ = x.shape
    tm = TOKEN_TILE
    row = lambda b, i: (b, i, 0)
    return pl.pallas_call(
        _inproj_kernel,
        grid=(bsz, t_len // tm),
        in_specs=[pl.BlockSpec((1, tm, D_MODEL), row),
                  pl.BlockSpec((1, 6, D_MODEL), lambda b, i: (mod_off + mod_stride * b, 0, 0)),
                  _const_spec((1, D_MODEL)),
                  _const_spec(w_in_bf.shape)],
        out_specs=[pl.BlockSpec((1, tm, RW_COLS), row),
                   pl.BlockSpec((1, tm, NA_IN_COLS), row),
                   pl.BlockSpec((1, tm, GATE_COLS), row)],
        out_shape=[jax.ShapeDtypeStruct((bsz, t_len, RW_COLS), F32),
                   jax.ShapeDtypeStruct((bsz, t_len, NA_IN_COLS), F32),
                   jax.ShapeDtypeStruct((bsz, t_len, GATE_COLS), F32)],
        compiler_params=_params(2),
        name="in_proj",
    )(x, mod_all, norm_g.reshape(1, -1), w_in_bf)


def _shift(x, mu):
    t_len = x.shape[0]
    row = lax.broadcasted_iota(jnp.int32, x.shape, 0)
    prev = jnp.where(row == 0, 0.0, pltpu.roll(x, 1, 0))
    nxt = jnp.where(row == t_len - 1, 0.0, pltpu.roll(x, t_len - 1, 0))
    return x + mu[0:1, :] * (prev - x) + mu[1:2, :] * (nxt - x)


def _stack_heads(x, lane_lo):
    return jnp.concatenate([x * lane_lo, x * (1.0 - lane_lo)], axis=0)


def _wkv_intra(units, consts):
    tri, mask_s, mask_i, eye, lane_lo = consts
    stack = lambda z: _stack_heads(z, lane_lo)

    cums = [_dot_exact_lhs(tri[int(u[6])], u[1]) for u in units]

    prep = []
    for (r, lw, kd, v, kk, b, reverse), cum in zip(units, cums):
        mid_row = CHUNK // 2 if reverse else CHUNK // 2 - 1
        tot_row = 0 if reverse else CHUNK - 1
        a = -kk
        ex = cum - lw
        mid = cum[mid_row:mid_row + 1, :]
        tot = cum[tot_row:tot_row + 1, :]
        up = jnp.exp(cum - mid)
        dn = jnp.exp(mid - cum)
        tail = jnp.exp(tot - cum)
        prep.append(dict(
            at_m=stack(a * jnp.exp(ex - mid)).astype(BF16),
            rt_m=stack(r * up).astype(BF16),
            btkt=jnp.concatenate([stack(b * dn), stack(kd * dn)], axis=0).astype(BF16),
            a_e=stack(a * jnp.exp(ex)),
            r_e=stack(r * jnp.exp(cum)),
            bk_t=jnp.concatenate([stack(b * tail), stack(kd * tail)], axis=0).T.astype(BF16),
            vv=stack(v).astype(BF16),
            diag=jnp.where(eye, jnp.exp(tot), 0.0),
            rev=int(reverse)))

    ntd = lambda x, y: lax.dot_general(x, y, (((1,), (1,)), ((), ())), preferred_element_type=F32)
    mm = lambda x, y: jnp.dot(x, y, preferred_element_type=F32)
    top = [mask_s[p["rev"]] * ntd(p["at_m"], p["btkt"]) for p in prep]
    bot = [(mask_i[p["rev"]] * ntd(p["rt_m"], p["btkt"])).astype(BF16) for p in prep]

    ms = [t[:, :LANES] for t in top]
    pws = [m.astype(BF16) for m in ms]
    pw2 = [mm(pw, pw) for pw in pws]
    steps = CHUNK.bit_length() - 1
    for j in range(1, steps):
        pws = [q.astype(BF16) for q in pw2]
        if j < steps - 1:
            both = [mm(pw, jnp.concatenate([m.astype(BF16), pw], axis=1)) for pw, m in zip(pws, ms)]
            ms = [m + q + b[:, :LANES] for m, q, b in zip(ms, pw2, both)]
            pw2 = [b[:, LANES:] for b in both]
        else:
            ms = [m + q + mm(pw, m.astype(BF16)) for m, q, pw in zip(ms, pw2, pws)]
    x0 = [jnp.concatenate([p["a_e"], mm(t[:, LANES:].astype(BF16), p["vv"])], axis=1) for p, t in zip(prep, top)]
    xs = [x + mm(m.astype(BF16), x.astype(BF16)) for x, m in zip(x0, ms)]

    out = []
    zeros = jnp.zeros((STACK, LANES), BF16)
    for p, x, bt in zip(prep, xs, bot):
        rhs = jnp.concatenate([x.astype(BF16), jnp.concatenate([zeros, p["vv"]], axis=1)], axis=0)
        lhs = jnp.concatenate([bt, p["bk_t"]], axis=0)
        res = mm(lhs, rhs)
        lhs2 = res[:, :LANES] + jnp.concatenate([p["r_e"], p["diag"]], axis=0)
        out.append((lhs2.astype(BF16), res[:, LANES:]))
    return out


def _wkv_constants():
    lane_head = np.arange(LANES) // HEAD_DIM
    ones = (lane_head[:, None] == lane_head[None, :]).astype(np.float32)
    t = np.arange(CHUNK)
    tri = np.stack([t[None, :] <= t[:, None], t[None, :] >= t[:, None]]).astype(np.float32)
    s = np.arange(STACK)
    same = (s[:, None] // CHUNK) == (s[None, :] // CHUNK)
    rs, cs = s[:, None], s[None, :]
    masks = np.stack([same & (cs < rs), same & (cs > rs), same & (cs <= rs), same & (cs >= rs)]).astype(np.float32)
    masks = np.concatenate([masks, masks], axis=-1)
    return jnp.asarray(ones, BF16), jnp.asarray(tri, BF16), jnp.asarray(masks, F32)


def _rwkv_kernel(*refs, t_len, has_s0, pairs):
    (r_ref, k_ref, v_ref, lo_ref, mur_ref, muk_ref, muv_ref, mul_ref, w0_ref, a0_ref, wup_ref, aup_ref,
     gup_ref, kk_ref, ka_ref, rk_ref, lng_ref, lnb_ref, ones_ref, tri_ref, mask_ref) = refs[:21]
    pos = 21
    s0_ref = None
    if has_s0:
        s0_ref = refs[pos]
        pos += 1
    o_ref, sn_ref = refs[pos], refs[pos + 1]
    (r_s, v_s, kk_s, b0_s, b1_s, lw0_s, lw1_s, kd0_s, kd1_s, gate_s, bonus_s, yf_s, yb_s,
     lhs_s, add_s, st_s) = refs[pos + 2:]

    ones = ones_ref[...]
    lane = lax.broadcasted_iota(jnp.int32, (1, LANES), 1)
    lo_half = lane < HEAD_DIM
    lane_lo = jnp.where(lo_half, 1.0, 0.0)
    mm = lambda x, y: jnp.dot(x, y, preferred_element_type=F32)

    lo = _shift(lo_ref[0], mul_ref[...])
    wd = jnp.tanh(lo[:, 0:LANES])
    ad = lo[:, LANES:2 * LANES]
    sig_gd = jax.nn.sigmoid(lo[:, 2 * LANES:3 * LANES]).astype(BF16)
    wd_split = [_split2(wd * m) for m in (lane_lo, 1.0 - lane_lo)]
    ad_bf = [(ad * m).astype(BF16) for m in (lane_lo, 1.0 - lane_lo)]
    for j in range(pairs):
        cols = slice(j * LANES, (j + 1) * LANES)
        r = _shift(r_ref[0, :, cols], mur_ref[:, cols])
        k = _shift(k_ref[0, :, cols], muk_ref[:, cols])
        v = _shift(v_ref[0, :, cols], muv_ref[:, cols])
        kk = k * kk_ref[:, cols]
        kk = kk * lax.rsqrt(_head_sum(kk * kk, ones) + L2_EPS)
        wup_h, wup_l = _split2(wup_ref[:, cols])
        aup = aup_ref[:, cols].astype(BF16)
        kdirs = []
        for e, (lw_s, kd_s, b_s) in enumerate(((lw0_s, kd0_s, b0_s), (lw1_s, kd1_s, b1_s))):
            wd_h, wd_l = wd_split[e]
            w_lin = w0_ref[e:e + 1, cols] + (mm(wd_h, wup_h) + mm(wd_l, wup_h) + mm(wd_h, wup_l))
            w_soft = -_softplus(-w_lin) - 0.5
            lw_s[j] = -jnp.exp(w_soft)
            iclr = jax.nn.sigmoid(a0_ref[e:e + 1, cols] + mm(ad_bf[e], aup))
            kd = k * (1.0 + (iclr - 1.0) * ka_ref[:, cols])
            kd_s[j] = kd
            b_s[j] = kk * iclr
            kdirs.append(kd)
        gate_s[:, cols] = mm(sig_gd, gup_ref[:, cols].astype(BF16))
        bonus_s[:, cols] = _head_sum(r * (0.5 * (kdirs[0] + kdirs[1])) * rk_ref[:, cols], ones) * v
        r_s[j] = r
        v_s[j] = v
        kk_s[j] = kk

    n_chunks = t_len // CHUNK
    groups = n_chunks // UNIT_CHUNKS
    rs = lax.broadcasted_iota(jnp.int32, (STACK, STACK), 0)
    cs = lax.broadcasted_iota(jnp.int32, (STACK, STACK), 1)
    consts = ((tri_ref[0], tri_ref[1]), (mask_ref[0], mask_ref[1]), (mask_ref[2], mask_ref[3]),
              rs == cs, lane_lo)
    dirs = ((lw0_s, kd0_s, b0_s), (lw1_s, kd1_s, b1_s))

    def intra_body(it, carry):
        j = it // groups
        g = it % groups
        units, ids = [], []
        for cc in range(UNIT_CHUNKS):
            c = g * UNIT_CHUNKS + cc
            rows = pl.ds(pl.multiple_of(c * CHUNK, CHUNK), CHUNK)
            for e, (lw_s, kd_s, b_s) in enumerate(dirs):
                units.append((r_s[j, rows, :], lw_s[j, rows, :], kd_s[j, rows, :], v_s[j, rows, :],
                              kk_s[j, rows, :], b_s[j, rows, :], e == 1))
                ids.append((j * 2 + e) * n_chunks + c)
        for uid, (lhs, add) in zip(ids, _wkv_intra(units, consts)):
            lhs_s[uid] = lhs
            add_s[uid] = add
        return carry

    lax.fori_loop(0, pairs * groups, intra_body, 0)

    for j in range(pairs):
        for e in range(2):
            if has_s0:
                st_s[2 * j + e] = s0_ref[0, e, j].T
            else:
                st_s[2 * j + e] = jnp.zeros((LANES, LANES), F32)

    def state_body(it, carry):
        chunk = (it, n_chunks - 1 - it)
        uids = [(j * 2 + e) * n_chunks + chunk[e] for j in range(pairs) for e in range(2)]
        sts = [st_s[ch].astype(BF16) for ch in range(2 * pairs)]
        res = [mm(lhs_s[uid], st) + add_s[uid] for uid, st in zip(uids, sts)]
        for ch, rr in enumerate(res):
            j, e = divmod(ch, 2)
            y_s = yb_s if e else yf_s
            y_s[j, pl.ds(pl.multiple_of(chunk[e] * CHUNK, CHUNK), CHUNK), :] = rr[:CHUNK] + rr[CHUNK:STACK]
            st_s[ch] = rr[STACK:]
        return carry

    lax.fori_loop(0, n_chunks, state_body, 0)
    for j in range(pairs):
        for e in range(2):
            sn_ref[0, e, j] = st_s[2 * j + e].T

    inv_d = 1.0 / HEAD_DIM
    for j in range(pairs):
        cols = slice(j * LANES, (j + 1) * LANES)
        y = yf_s[j] + yb_s[j]
        mean = _head_sum(y, ones) * inv_d
        dlt = y - mean
        var = _head_sum(dlt * dlt, ones) * inv_d
        yn = dlt * lax.rsqrt(var + GN_EPS) * lng_ref[:, cols] + lnb_ref[:, cols]
        o_ref[0, :, cols] = (yn + bonus_s[:, cols]) * gate_s[:, cols]


def _rwkv_branch(u_rw, s0_big, p, pairs):
    bsz, t_len, _ = u_rw.shape
    has_s0 = s0_big is not None
    width = pairs * LANES
    seg = RW_WIDTH // width
    tok = lambda off: pl.BlockSpec((1, t_len, width), lambda b, j: (b, 0, off + j))
    mu = lambda off: pl.BlockSpec((2, width), lambda b, j: (0, off + j))
    vec2 = pl.BlockSpec((2, width), lambda b, j: (0, j))
    vec1 = pl.BlockSpec((1, width), lambda b, j: (0, j))
    mat = pl.BlockSpec((LANES, width), lambda b, j: (0, j))
    lora_w = 3 * LANES
    lora_blk = 3 * RW_WIDTH // lora_w
    in_specs = [tok(0), tok(seg), tok(2 * seg),
                pl.BlockSpec((1, t_len, lora_w), lambda b, j: (b, 0, lora_blk)),
                mu(0), mu(seg), mu(2 * seg),
                pl.BlockSpec((2, lora_w), lambda b, j: (0, lora_blk)),
                vec2, vec2, mat, mat, mat, vec1, vec1, vec1, vec1, vec1,
                _const_spec((LANES, LANES)), _const_spec((2, CHUNK, CHUNK)), _const_spec((4, STACK, 2 * STACK))]
    args = [u_rw, u_rw, u_rw, u_rw, p["shift_mu"], p["shift_mu"], p["shift_mu"], p["shift_mu"],
            p["rw_w0"], p["rw_a0"],
            p["rw_w_up"].reshape(2 * LORA_DECAY, RW_WIDTH), p["rw_a_up"].reshape(2 * LORA_ICLR, RW_WIDTH),
            p["rw_g_up"], p["rw_k_k"].reshape(1, -1), p["rw_k_a"].reshape(1, -1),
            p["rw_r_k"].reshape(1, -1), p["rw_ln_g"].reshape(1, -1), p["rw_ln_b"].reshape(1, -1),
            *_wkv_constants()]
    st_spec = pl.BlockSpec((1, 2, pairs, LANES, LANES), lambda b, j: (b, 0, j, 0, 0))
    if has_s0:
        in_specs.append(st_spec)
        args.append(s0_big)
    n_units = 2 * pairs * (t_len // CHUNK)
    per_pair = pltpu.VMEM((pairs, t_len, LANES), F32)
    full = pltpu.VMEM((t_len, width), F32)
    scratch = [per_pair] * 9 + [full, full, per_pair, per_pair,
                                pltpu.VMEM((n_units, 2 * STACK, LANES), BF16),
                                pltpu.VMEM((n_units, 2 * STACK, LANES), F32),
                                pltpu.VMEM((2 * pairs, LANES, LANES), F32)]
    o_rw, s_new = pl.pallas_call(
        functools.partial(_rwkv_kernel, t_len=t_len, has_s0=has_s0, pairs=pairs),
        grid=(bsz, PAIRS // pairs),
        in_specs=in_specs,
        out_specs=[pl.BlockSpec((1, t_len, width), lambda b, j: (b, 0, j)), st_spec],
        out_shape=[jax.ShapeDtypeStruct((bsz, t_len, RW_WIDTH), F32),
                   jax.ShapeDtypeStruct((bsz, 2, PAIRS, LANES, LANES), F32)],
        scratch_shapes=scratch,
        compiler_params=_params(2),
        name="rwkv_branch",
    )(*args)
    return o_rw, s_new


def _state_to_big(s0):
    bsz = s0.shape[0]
    x = s0.reshape(bsz, 2, PAIRS, 2, HEAD_DIM, HEAD_DIM)
    z = jnp.zeros_like(x[:, :, :, 0])
    top = jnp.concatenate([x[:, :, :, 0], z], axis=-1)
    bot = jnp.concatenate([z, x[:, :, :, 1]], axis=-1)
    return jnp.concatenate([top, bot], axis=-2)


def _state_from_big(s_big):
    bsz = s_big.shape[0]
    h0 = s_big[:, :, :, :HEAD_DIM, :HEAD_DIM]
    h1 = s_big[:, :, :, HEAD_DIM:, HEAD_DIM:]
    return jnp.stack([h0, h1], axis=3).reshape(bsz, 2, RW_HEADS, HEAD_DIM, HEAD_DIM)


def _qk_norm(t, g, ones):
    ms = _head_sum(t * t, ones) * (1.0 / HEAD_DIM)
    return t * lax.rsqrt(ms + RMS_EPS) * g


def _nt(x, y):
    return lax.dot_general(x, y, (((1,), (1,)), ((), ())), preferred_element_type=F32)


def _na_ctx_kernel(q_ref, k_ref, v_ref, qg_ref, kg_ref, o_ref, kn_ref, vc_ref):
    ones = _head_ones()
    lo_half = lax.broadcasted_iota(jnp.int32, (1, LANES), 1) < HEAD_DIM
    lo = jnp.where(lo_half, 1.0, 0.0)
    t_len = q_ref.shape[1]
    scale = HEAD_DIM ** -0.5
    qs, ks, vs = [], [], []
    for j in range(NA_WIDTH // LANES):
        cols = slice(j * LANES, (j + 1) * LANES)
        qn = _qk_norm(q_ref[0, :, cols], qg_ref[...], ones)
        kn = _qk_norm(k_ref[0, :, cols], kg_ref[...], ones)
        v = v_ref[0, :, cols]
        kn_ref[0, :, cols] = kn
        vc_ref[0, :, cols] = v
        qs.append(jnp.concatenate([qn * lo, qn * (1.0 - lo)], axis=0).astype(BF16))
        ks.append(kn.astype(BF16))
        vs.append(v.astype(BF16))
    logits = [_nt(q, k) * scale for q, k in zip(qs, ks)]
    ms = [jnp.max(s, axis=-1, keepdims=True) for s in logits]
    ps = [jnp.exp(s - m) for s, m in zip(logits, ms)]
    ls = [jnp.sum(p, axis=-1, keepdims=True) for p in ps]
    outs = [jnp.dot(p.astype(BF16), v, preferred_element_type=F32) / l for p, v, l in zip(ps, vs, ls)]
    for j, o in enumerate(outs):
        o_ref[0, :, j * LANES:(j + 1) * LANES] = jnp.where(lo_half, o[:t_len], o[t_len:])


def _na_context(u_na, q_g, k_g):
    bsz, t_len, _ = u_na.shape
    tok = lambda seg: pl.BlockSpec((1, t_len, NA_WIDTH), lambda b: (b, 0, seg))
    out_blk = pl.BlockSpec((1, t_len, NA_WIDTH), lambda b: (b, 0, 0))
    g2 = lambda g: jnp.tile(g.reshape(1, HEAD_DIM), (1, 2))
    shp = jax.ShapeDtypeStruct((bsz, t_len, NA_WIDTH), F32)
    return pl.pallas_call(
        _na_ctx_kernel,
        grid=(bsz,),
        in_specs=[tok(0), tok(1), tok(2), _const_spec((1, LANES)), _const_spec((1, LANES))],
        out_specs=[out_blk, out_blk, out_blk],
        out_shape=[shp, shp, shp],
        compiler_params=_params(1),
        name="na_context",
    )(u_na, u_na, u_na, g2(q_g), g2(k_g))


NA_ROW_ILP = 2


def _na_lat_kernel(q_ref, k_ref, v_ref, kc_ref, vc_ref, qg_ref, kg_ref, tab_ref, o_ref,
                   q0_s, q1_s, kn_s, v_s, kc_s, vc_s, *, rows, kr):
    ones = _head_ones()
    lo_half = lax.broadcasted_iota(jnp.int32, (1, LANES), 1) < HEAD_DIM
    lo = jnp.where(lo_half, 1.0, 0.0)
    qn = _qk_norm(q_ref[0], qg_ref[...], ones)
    q0_s[...] = (qn * lo).astype(BF16)
    q1_s[...] = (qn * (1.0 - lo)).astype(BF16)
    kn_s[...] = _qk_norm(k_ref[0], kg_ref[...], ones).astype(BF16)
    v_s[...] = v_ref[0].astype(BF16)
    kc_s[...] = kc_ref[0].astype(BF16)
    vc_s[...] = vc_ref[0].astype(BF16)
    scale = HEAD_DIM ** -0.5
    win = kr * GRID_W

    def body(it, carry):
        qs, k_rows, q_rows, biases = [], [], [], []
        for s in range(NA_ROW_ILP):
            i = it * NA_ROW_ILP + s
            r0 = jnp.clip(i - kr // 2, 0, rows - kr)
            d0 = r0 - i + (NA_ROWS - 1)
            qr = pl.ds(pl.multiple_of(i * GRID_W, GRID_W), GRID_W)
            q_rows.append(qr)
            k_rows.append(pl.ds(pl.multiple_of(r0 * GRID_W, GRID_W), win))
            qs.append(jnp.concatenate([q0_s[qr, :], q1_s[qr, :]], axis=0))
            biases.append(jnp.concatenate(
                [jnp.concatenate([tab_ref[h, d0 + 2 * m] for m in range(kr // 2)], axis=1) for h in range(2)],
                axis=0))
        lw = [_nt(q, kn_s[kr_, :]) * scale + b for q, kr_, b in zip(qs, k_rows, biases)]
        lc = [_nt(q, kc_s[...]) * scale for q in qs]
        ms = [jnp.maximum(jnp.max(a, axis=-1, keepdims=True), jnp.max(c, axis=-1, keepdims=True))
              for a, c in zip(lw, lc)]
        pw = [jnp.exp(a - m) for a, m in zip(lw, ms)]
        pc = [jnp.exp(c - m) for c, m in zip(lc, ms)]
        ls = [jnp.sum(a, axis=-1, keepdims=True) + jnp.sum(c, axis=-1, keepdims=True) for a, c in zip(pw, pc)]
        outs = [(jnp.dot(a.astype(BF16), v_s[kr_, :], preferred_element_type=F32)
                 + jnp.dot(c.astype(BF16), vc_s[...], preferred_element_type=F32)) / l
                for a, c, kr_, l in zip(pw, pc, k_rows, ls)]
        for qr, o in zip(q_rows, outs):
            o_ref[0, qr, :] = jnp.where(lo_half, o[:GRID_W], o[GRID_W:])
        return carry

    lax.fori_loop(0, rows // NA_ROW_ILP, body, 0)


def _latent_bias_table(rpb):
    qc = np.arange(GRID_W)[:, None]
    kc = np.arange(GRID_W)[None, :]
    ws = np.clip(qc - NA_COLS // 2, 0, GRID_W - NA_COLS)
    valid = (kc >= ws) & (kc < ws + NA_COLS)
    dc = np.clip(kc - qc, -(NA_COLS - 1), NA_COLS - 1) + NA_COLS - 1
    onehot = (dc[None] == np.arange(2 * NA_COLS - 1)[:, None, None]).astype(np.float32)
    cb = jnp.einsum("hdc,cqk->hdqk", rpb, jnp.asarray(onehot), precision=lax.Precision.HIGHEST)
    cb = jnp.where(valid[None, None], cb, NEG_INF)
    return jnp.concatenate([cb[:, :-1], cb[:, 1:]], axis=-1)


def _na_latent(u_na, k_ctx, v_ctx, q_g, k_g, rpb):
    bsz, t_len, _ = u_na.shape
    rows = t_len // GRID_W
    kr = min(NA_ROWS, rows)
    assert kr % 2 == 0 and rows % NA_ROW_ILP == 0
    ctx_len = k_ctx.shape[1]
    seg = NA_WIDTH // LANES
    tok = lambda off: pl.BlockSpec((1, t_len, LANES), lambda b, j: (b, 0, off + j))
    ctx = pl.BlockSpec((1, ctx_len, LANES), lambda b, j: (b, 0, j))
    g2 = lambda g: jnp.tile(g.reshape(1, HEAD_DIM), (1, 2))
    tab = _latent_bias_table(rpb)
    tok_s = pltpu.VMEM((t_len, LANES), BF16)
    ctx_s = pltpu.VMEM((ctx_len, LANES), BF16)
    return pl.pallas_call(
        functools.partial(_na_lat_kernel, rows=rows, kr=kr),
        grid=(bsz, seg),
        in_specs=[tok(0), tok(seg), tok(2 * seg), ctx, ctx,
                  _const_spec((1, LANES)), _const_spec((1, LANES)),
                  pl.BlockSpec((2, 2 * NA_ROWS - 2, GRID_W, 2 * GRID_W), lambda b, j: (j, 0, 0, 0))],
        out_specs=pl.BlockSpec((1, t_len, LANES), lambda b, j: (b, 0, j)),
        out_shape=jax.ShapeDtypeStruct((bsz, t_len, NA_WIDTH), F32),
        scratch_shapes=[tok_s, tok_s, tok_s, tok_s, ctx_s, ctx_s],
        compiler_params=_params(2),
        name="na_latent",
    )(u_na, u_na, u_na, k_ctx, v_ctx, g2(q_g), g2(k_g), tab)


def _outproj_kernel(x_ref, orw_ref, ona_ref, gt_ref, mod_ref, g_ref, wor_ref, won_ref, wout_ref, x1_ref, h2_ref):
    g_rw = jax.nn.sigmoid(gt_ref[0, :, :D_MODEL])
    g_na = jax.nn.sigmoid(gt_ref[0, :, D_MODEL:])
    merged = g_rw * _dot(orw_ref[0], wor_ref[...]) + g_na * _dot(ona_ref[0], won_ref[...])
    x1 = x_ref[0] + mod_ref[0, 2:3, :] * _dot(merged, wout_ref[...])
    x1_ref[0] = x1
    h2 = _rms_rows(x1) * g_ref[...]
    h2_ref[0] = (h2 * (1.0 + mod_ref[0, 4:5, :]) + mod_ref[0, 3:4, :]).astype(BF16)


def _out_proj(x, o_rw, o_na, gates, mod_all, mod_off, mod_stride, norm_g, w_or, w_on, w_out):
    bsz, t_len, _ = x.shape
    tm = TOKEN_TILE
    row = lambda b, i: (b, i, 0)
    return pl.pallas_call(
        _outproj_kernel,
        grid=(bsz, t_len // tm),
        in_specs=[pl.BlockSpec((1, tm, D_MODEL), row),
                  pl.BlockSpec((1, tm, RW_WIDTH), row),
                  pl.BlockSpec((1, tm, NA_WIDTH), row),
                  pl.BlockSpec((1, tm, GATE_COLS), row),
                  pl.BlockSpec((1, 6, D_MODEL), lambda b, i: (mod_off + mod_stride * b, 0, 0)),
                  _const_spec((1, D_MODEL)),
                  _const_spec(w_or.shape), _const_spec(w_on.shape), _const_spec(w_out.shape)],
        out_specs=[pl.BlockSpec((1, tm, D_MODEL), row), pl.BlockSpec((1, tm, D_MODEL), row)],
        out_shape=[jax.ShapeDtypeStruct((bsz, t_len, D_MODEL), F32),
                   jax.ShapeDtypeStruct((bsz, t_len, D_MODEL), BF16)],
        compiler_params=_params(2),
        name="out_proj",
    )(x, o_rw, o_na, gates, mod_all, norm_g.reshape(1, -1), w_or, w_on, w_out)


def _ffn_kernel(h2_ref, x1_ref, mod_ref, w1_ref, w3_ref, w2_ref, y_ref):
    h2 = h2_ref[0]
    acc = jnp.zeros((h2.shape[0], D_MODEL), F32)
    for c in range(FF_HIDDEN // FF_CHUNK):
        cols = slice(c * FF_CHUNK, (c + 1) * FF_CHUNK)
        a = jnp.dot(h2, w1_ref[:, cols], preferred_element_type=F32)
        b = jnp.dot(h2, w3_ref[:, cols], preferred_element_type=F32)
        hh = (a * jax.nn.sigmoid(a) * b).astype(BF16)
        acc = acc + jnp.dot(hh, w2_ref[cols, :], preferred_element_type=F32)
    y_ref[0] = x1_ref[0] + mod_ref[0, 5:6, :] * acc


def _ffn(h2, x1, mod_all, mod_off, mod_stride, w1, w3, w2):
    bsz, t_len, _ = x1.shape
    tm = TOKEN_TILE
    row = lambda b, i: (b, i, 0)
    return pl.pallas_call(
        _ffn_kernel,
        grid=(bsz, t_len // tm),
        in_specs=[pl.BlockSpec((1, tm, D_MODEL), row),
                  pl.BlockSpec((1, tm, D_MODEL), row),
                  pl.BlockSpec((1, 6, D_MODEL), lambda b, i: (mod_off + mod_stride * b, 0, 0)),
                  _const_spec(w1.shape), _const_spec(w3.shape), _const_spec(w2.shape)],
        out_specs=pl.BlockSpec((1, tm, D_MODEL), row),
        out_shape=jax.ShapeDtypeStruct((bsz, t_len, D_MODEL), F32),
        compiler_params=_params(2),
        name="ffn",
    )(h2, x1, mod_all, w1, w3, w2)


def _trunk(x, mod_all, mod_off, mod_stride, s0_big, ctx_kv, p, wb):
    shape = x.shape
    flat = (lambda t: t.reshape(1, -1, t.shape[-1])) if mod_stride == 0 else (lambda t: t)
    unflat = lambda t: t.reshape(shape[0], shape[1], t.shape[-1])
    u_rw, u_na, gates = _in_proj(flat(x), mod_all, mod_off, mod_stride, p["norm1_g"], wb["w_in"])
    u_rw, u_na = unflat(u_rw), unflat(u_na)
    o_rw, s_new = _rwkv_branch(u_rw, s0_big, p, RWKV_PAIRS_CTX if ctx_kv is None else RWKV_PAIRS_LAT)
    if ctx_kv is None:
        o_na, k_new, v_new = _na_context(u_na, p["na_q_g"], p["na_k_g"])
    else:
        o_na = _na_latent(u_na, ctx_kv[0], ctx_kv[1], p["na_q_g"], p["na_k_g"], p["na_rpb"])
        k_new = v_new = None
    x1, h2 = _out_proj(flat(x), flat(o_rw), flat(o_na), gates, mod_all, mod_off, mod_stride, p["norm2_g"],
                       wb["w_o_rwkv"], wb["w_o_na"], wb["w_out"])
    y = _ffn(h2, x1, mod_all, mod_off, mod_stride, wb["ffn_w1"], wb["ffn_w3"], wb["ffn_w2"])
    return unflat(y), s_new, k_new, v_new


def kernel(x_prompt, x_sample, state_rwkv, cache_na_k, cache_na_v, c, c_ctx, norm1_g, norm2_g, w_ada, b_ada,
           w_in, shift_mu, rw_w0, rw_w_up, rw_a0, rw_a_up, rw_g_up, rw_k_k, rw_k_a, rw_r_k, rw_ln_g, rw_ln_b,
           na_q_g, na_k_g, na_rpb, w_o_rwkv, w_o_na, w_out, ffn_w1, ffn_w3, ffn_w2):
    depth = w_in.shape[0]
    bsz, seq = x_prompt.shape[:2]
    dec = x_sample.shape[0]
    n_vec = 8
    cvecs = jnp.concatenate([c_ctx[None, :], c, jnp.zeros((n_vec - 1 - dec, D_MODEL), F32)], axis=0)
    y_p, y_s = x_prompt, x_sample
    new_s, new_k, new_v = [], [], []
    for l in range(depth):
        p = dict(norm1_g=norm1_g[l], norm2_g=norm2_g[l], shift_mu=shift_mu[l], rw_w0=rw_w0[l],
                 rw_w_up=rw_w_up[l], rw_a0=rw_a0[l], rw_a_up=rw_a_up[l], rw_g_up=rw_g_up[l],
                 rw_k_k=rw_k_k[l], rw_k_a=rw_k_a[l], rw_r_k=rw_r_k[l], rw_ln_g=rw_ln_g[l],
                 rw_ln_b=rw_ln_b[l], na_q_g=na_q_g[l], na_k_g=na_k_g[l], na_rpb=na_rpb[l])
        wb = dict(w_in=w_in[l].astype(BF16), w_o_rwkv=w_o_rwkv[l].astype(BF16), w_o_na=w_o_na[l].astype(BF16),
                  w_out=w_out[l].astype(BF16), ffn_w1=ffn_w1[l].astype(BF16), ffn_w3=ffn_w3[l].astype(BF16),
                  ffn_w2=ffn_w2[l].astype(BF16))
        mod_all = _modulation(cvecs, w_ada[l], b_ada[l])[:1 + dec].reshape(1 + dec, 6, D_MODEL)
        y_p, s_big, k_l, v_l = _trunk(y_p, mod_all, 0, 0, None, None, p, wb)
        new_s.append(_state_from_big(s_big))
        new_k.append(k_l.reshape(bsz, seq, NA_HEADS, HEAD_DIM))
        new_v.append(v_l.reshape(bsz, seq, NA_HEADS, HEAD_DIM))
        ctx_k = cache_na_k[:, l].reshape(dec, -1, NA_WIDTH)
        ctx_v = cache_na_v[:, l].reshape(dec, -1, NA_WIDTH)
        y_s, _, _, _ = _trunk(y_s, mod_all, 1, 1, _state_to_big(state_rwkv[:, l]), (ctx_k, ctx_v), p, wb)
    return (y_p, y_s, jnp.stack(new_s, axis=1), jnp.stack(new_k, axis=1), jnp.stack(new_v, axis=1))
```

```python
import functools

import numpy as np
import jax
import jax.numpy as jnp
from jax import lax
from jax.experimental import pallas as pl
from jax.experimental.pallas import tpu as pltpu

D_MODEL = 1024
GRID_W = 64
HEAD_DIM = 64
RW_HEADS = 8
RW_WIDTH = RW_HEADS * HEAD_DIM
NA_HEADS = 8
NA_WIDTH = NA_HEADS * HEAD_DIM
LORA_DECAY = 64
LORA_ICLR = 64
LORA_GATE = 128
NA_ROWS = 8
NA_COLS = 16
FF_HIDDEN = 2816
RW_COLS = 3 * RW_WIDTH + 2 * LORA_DECAY + 2 * LORA_ICLR + LORA_GATE
NA_IN_COLS = 3 * NA_WIDTH
GATE_COLS = 2 * D_MODEL
RMS_EPS = 1e-6
GN_EPS = 64e-5
L2_EPS = 1e-12
NEG_INF = -1e30
DECAY_SCALE = float(np.exp(-0.5))

LANES = 128
PAIRS = RW_HEADS // 2
CHUNK = 64
STACK = 2 * CHUNK
RWKV_PAIRS_CTX = 4
RWKV_PAIRS_LAT = 2
UNIT_CHUNKS = 4
TOKEN_TILE = 512
FF_CHUNK = 256
VMEM_LIMIT = 56 * 1024 * 1024

F32 = jnp.float32
BF16 = jnp.bfloat16


def _dot(a, b):
    return jnp.dot(a.astype(BF16), b.astype(BF16), preferred_element_type=F32)


def _dot_nt(a, b):
    return lax.dot_general(a.astype(BF16), b.astype(BF16), (((1,), (1,)), ((), ())),
                           preferred_element_type=F32)


def _split2(x):
    hi = x.astype(BF16)
    lo = (x - hi.astype(F32)).astype(BF16)
    return hi, lo


def _dot_exact_lhs(a_exact, b):
    h, l = _split2(b)
    d = lambda x: jnp.dot(a_exact, x, preferred_element_type=F32)
    return d(h) + d(l)


def _dot3(a, b):
    ah, al = _split2(a)
    bh, bl = _split2(b)
    d = lambda x, y: jnp.dot(x, y, preferred_element_type=F32)
    return d(ah, bh) + d(al, bh) + d(ah, bl)


def _head_ones():
    r = lax.broadcasted_iota(jnp.int32, (LANES, LANES), 0) // HEAD_DIM
    c = lax.broadcasted_iota(jnp.int32, (LANES, LANES), 1) // HEAD_DIM
    return jnp.where(r == c, 1.0, 0.0).astype(BF16)


def _head_sum(x, ones):
    return jnp.dot(x.astype(BF16), ones, preferred_element_type=F32)


def _rms_rows(x):
    return x * lax.rsqrt(jnp.mean(x * x, axis=-1, keepdims=True) + RMS_EPS)


def _const_spec(shape):
    nd = len(shape)
    return pl.BlockSpec(shape, lambda *_: (0,) * nd, pipeline_mode=pl.Buffered(1))


def _params(n_axes):
    return pltpu.CompilerParams(dimension_semantics=("arbitrary",) * n_axes,
                                vmem_limit_bytes=VMEM_LIMIT)


def _mod_kernel(c_ref, w_ref, b_ref, o_ref):
    s = c_ref[...]
    s = s * jax.nn.sigmoid(s)
    o_ref[...] = _dot3(s, w_ref[...]) + b_ref[...]


def _modulation(cvecs, w_ada, b_ada):
    n = cvecs.shape[0]
    tn = 1536
    return pl.pallas_call(
        _mod_kernel,
        grid=(6 * D_MODEL // tn,),
        in_specs=[pl.BlockSpec((n, D_MODEL), lambda j: (0, 0)),
                  pl.BlockSpec((D_MODEL, tn), lambda j: (0, j)),
                  pl.BlockSpec((1, tn), lambda j: (0, j))],
        out_specs=pl.BlockSpec((n, tn), lambda j: (0, j)),
        out_shape=jax.ShapeDtypeStruct((n, 6 * D_MODEL), F32),
        compiler_params=_params(1),
        name="modulation",
    )(cvecs, w_ada, b_ada.reshape(1, -1))


def _inproj_kernel(x_ref, mod_ref, g_ref, w_ref, urw_ref, una_ref, gt_ref):
    x = x_ref[0]
    h = _rms_rows(x) * g_ref[...]
    h = (h * (1.0 + mod_ref[0, 1:2, :]) + mod_ref[0, 0:1, :]).astype(BF16)
    d = lambda lo, hi: jnp.dot(h, w_ref[:, lo:hi], preferred_element_type=F32)
    urw_ref[0] = d(0, RW_COLS)
    una_ref[0] = d(RW_COLS, RW_COLS + NA_IN_COLS)
    gt_ref[0] = d(RW_COLS + NA_IN_COLS, RW_COLS + NA_IN_COLS + GATE_COLS)


def _in_proj(x, mod_all, mod_off, mod_stride, norm_g, w_in_bf):
    bsz, t_len, _ = x.shape
    tm = TOKEN_TILE
    row = lambda b, i: (b, i, 0)
    return pl.pallas_call(
        _inproj_kernel,
        grid=(bsz, t_len // tm),
        in_specs=[pl.BlockSpec((1, tm, D_MODEL), row),
                  pl.BlockSpec((1, 6, D_MODEL), lambda b, i: (mod_off + mod_stride * b, 0, 0)),
                  _const_spec((1, D_MODEL)),
                  _const_spec(w_in_bf.shape)],
        out_specs=[pl.BlockSpec((1, tm, RW_COLS), row),
                   pl.BlockSpec((1, tm, NA_IN_COLS), row),
                   pl.BlockSpec((1, tm, GATE_COLS), row)],
        out_shape=[jax.ShapeDtypeStruct((bsz, t_len, RW_COLS), F32),
                   jax.ShapeDtypeStruct((bsz, t_len, NA_IN_COLS), F32),
                   jax.ShapeDtypeStruct((bsz, t_len, GATE_COLS), F32)],
        compiler_params=_params(2),
        name="in_proj",
    )(x, mod_all, norm_g.reshape(1, -1), w_in_bf)


def _shift(x, mu):
    t_len = x.shape[0]
    row = lax.broadcasted_iota(jnp.int32, x.shape, 0)
    prev = jnp.where(row == 0, 0.0, pltpu.roll(x, 1, 0))
    nxt = jnp.where(row == t_len - 1, 0.0, pltpu.roll(x, t_len - 1, 0))
    return x + mu[0:1, :] * (prev - x) + mu[1:2, :] * (nxt - x)


def _stack_heads(x, lane_lo):
    return jnp.concatenate([x * lane_lo, x * (1.0 - lane_lo)], axis=0)


def _wkv_intra(units, consts):
    tri, mask_s, mask_i, eye, lane_lo = consts
    stack = lambda z: _stack_heads(z, lane_lo)

    cums = [_dot_exact_lhs(tri[int(u[6])], u[1]) for u in units]

    prep = []
    for (r, lw, kd, v, kk, b, reverse), cum in zip(units, cums):
        mid_row = CHUNK // 2 if reverse else CHUNK // 2 - 1
        tot_row = 0 if reverse else CHUNK - 1
        a = -kk
        ex = cum - lw
        mid = cum[mid_row:mid_row + 1, :]
        tot = cum[tot_row:tot_row + 1, :]
        up = jnp.exp(cum - mid)
        dn = jnp.exp(mid - cum)
        tail = jnp.exp(tot - cum)
        prep.append(dict(
            at_m=stack(a * jnp.exp(ex - mid)).astype(BF16),
            rt_m=stack(r * up).astype(BF16),
            btkt=jnp.concatenate([stack(b * dn), stack(kd * dn)], axis=0).astype(BF16),
            a_e=stack(a * jnp.exp(ex)),
            r_e=stack(r * jnp.exp(cum)),
            bk_t=jnp.concatenate([stack(b * tail), stack(kd * tail)], axis=0).T.astype(BF16),
            vv=stack(v).astype(BF16),
            diag=jnp.where(eye, jnp.exp(tot), 0.0),
            rev=int(reverse)))

    ntd = lambda x, y: lax.dot_general(x, y, (((1,), (1,)), ((), ())), preferred_element_type=F32)
    mm = lambda x, y: jnp.dot(x, y, preferred_element_type=F32)
    top = [mask_s[p["rev"]] * ntd(p["at_m"], p["btkt"]) for p in prep]
    bot = [(mask_i[p["rev"]] * ntd(p["rt_m"], p["btkt"])).astype(BF16) for p in prep]

    ms = [t[:, :LANES] for t in top]
    pws = [m.astype(BF16) for m in ms]
    pw2 = [mm(pw, pw) for pw in pws]
    steps = CHUNK.bit_length() - 1
    for j in range(1, steps):
        pws = [q.astype(BF16) for q in pw2]
        if j < steps - 1:
            both = [mm(pw, jnp.concatenate([m.astype(BF16), pw], axis=1)) for pw, m in zip(pws, ms)]
            ms = [m + q + b[:, :LANES] for m, q, b in zip(ms, pw2, both)]
            pw2 = [b[:, LANES:] for b in both]
        else:
            ms = [m + q + mm(pw, m.astype(BF16)) for m, q, pw in zip(ms, pw2, pws)]
    x0 = [jnp.concatenate([p["a_e"], mm(t[:, LANES:].astype(BF16), p["vv"])], axis=1) for p, t in zip(prep, top)]
    xs = [x + mm(m.astype(BF16), x.astype(BF16)) for x, m in zip(x0, ms)]

    out = []
    zeros = jnp.zeros((STACK, LANES), BF16)
    for p, x, bt in zip(prep, xs, bot):
        rhs = jnp.concatenate([x.astype(BF16), jnp.concatenate([zeros, p["vv"]], axis=1)], axis=0)
        lhs = jnp.concatenate([bt, p["bk_t"]], axis=0)
        res = mm(lhs, rhs)
        lhs2 = res[:, :LANES] + jnp.concatenate([p["r_e"], p["diag"]], axis=0)
        out.append((lhs2.astype(BF16), res[:, LANES:]))
    return out


def _wkv_constants():
    lane_head = np.arange(LANES) // HEAD_DIM
    ones = (lane_head[:, None] == lane_head[None, :]).astype(np.float32)
    t = np.arange(CHUNK)
    tri = np.stack([t[None, :] <= t[:, None], t[None, :] >= t[:, None]]).astype(np.float32)
    s = np.arange(STACK)
    same = (s[:, None] // CHUNK) == (s[None, :] // CHUNK)
    rs, cs = s[:, None], s[None, :]
    masks = np.stack([same & (cs < rs), same & (cs > rs), same & (cs <= rs), same & (cs >= rs)]).astype(np.float32)
    masks = np.concatenate([masks, masks], axis=-1)
    return jnp.asarray(ones, BF16), jnp.asarray(tri, BF16), jnp.asarray(masks, F32)


def _rwkv_kernel(*refs, t_len, has_s0, pairs):
    (r_ref, k_ref, v_ref, lo_ref, mur_ref, muk_ref, muv_ref, mul_ref, w0_ref, a0_ref, wup_ref, aup_ref,
     gup_ref, kk_ref, ka_ref, rk_ref, lng_ref, lnb_ref, ones_ref, tri_ref, mask_ref) = refs[:21]
    pos = 21
    s0_ref = None
    if has_s0:
        s0_ref = refs[pos]
        pos += 1
    o_ref, sn_ref = refs[pos], refs[pos + 1]
    (r_s, v_s, kk_s, b0_s, b1_s, lw0_s, lw1_s, kd0_s, kd1_s, gate_s, bonus_s, yf_s, yb_s,
     lhs_s, add_s, st_s) = refs[pos + 2:]

    ones = ones_ref[...]
    lane = lax.broadcasted_iota(jnp.int32, (1, LANES), 1)
    lo_half = lane < HEAD_DIM
    lane_lo = jnp.where(lo_half, 1.0, 0.0)
    mm = lambda x, y: jnp.dot(x, y, preferred_element_type=F32)

    lo = _shift(lo_ref[0], mul_ref[...])
    wd = jnp.tanh(lo[:, 0:LANES])
    ad = lo[:, LANES:2 * LANES]
    sig_gd = jax.nn.sigmoid(lo[:, 2 * LANES:3 * LANES]).astype(BF16)
    wd_split = [_split2(wd * m) for m in (lane_lo, 1.0 - lane_lo)]
    ad_bf = [(ad * m).astype(BF16) for m in (lane_lo, 1.0 - lane_lo)]
    for j in range(pairs):
        cols = slice(j * LANES, (j + 1) * LANES)
        r = _shift(r_ref[0, :, cols], mur_ref[:, cols])
        k = _shift(k_ref[0, :, cols], muk_ref[:, cols])
        v = _shift(v_ref[0, :, cols], muv_ref[:, cols])
        kk = k * kk_ref[:, cols]
        kk = kk * lax.rsqrt(_head_sum(kk * kk, ones) + L2_EPS)
        wup_h, wup_l = _split2(wup_ref[:, cols])
        aup = aup_ref[:, cols].astype(BF16)
        kdirs = []
        for e, (lw_s, kd_s, b_s) in enumerate(((lw0_s, kd0_s, b0_s), (lw1_s, kd1_s, b1_s))):
            wd_h, wd_l = wd_split[e]
            w_lin = w0_ref[e:e + 1, cols] + (mm(wd_h, wup_h) + mm(wd_l, wup_h) + mm(wd_h, wup_l))
            lw_s[j] = -DECAY_SCALE * jax.nn.sigmoid(w_lin)
            iclr = jax.nn.sigmoid(a0_ref[e:e + 1, cols] + mm(ad_bf[e], aup))
            kd = k * (1.0 + (iclr - 1.0) * ka_ref[:, cols])
            kd_s[j] = kd
            b_s[j] = kk * iclr
            kdirs.append(kd)
        gate_s[:, cols] = mm(sig_gd, gup_ref[:, cols].astype(BF16))
        bonus_s[:, cols] = _head_sum(r * (0.5 * (kdirs[0] + kdirs[1])) * rk_ref[:, cols], ones) * v
        r_s[j] = r
        v_s[j] = v
        kk_s[j] = kk

    n_chunks = t_len // CHUNK
    groups = n_chunks // UNIT_CHUNKS
    rs = lax.broadcasted_iota(jnp.int32, (STACK, STACK), 0)
    cs = lax.broadcasted_iota(jnp.int32, (STACK, STACK), 1)
    consts = ((tri_ref[0], tri_ref[1]), (mask_ref[0], mask_ref[1]), (mask_ref[2], mask_ref[3]),
              rs == cs, lane_lo)
    dirs = ((lw0_s, kd0_s, b0_s), (lw1_s, kd1_s, b1_s))

    def intra_body(it, carry):
        j = it // groups
        g = it % groups
        units, ids = [], []
        for cc in range(UNIT_CHUNKS):
            c = g * UNIT_CHUNKS + cc
            rows = pl.ds(pl.multiple_of(c * CHUNK, CHUNK), CHUNK)
            for e, (lw_s, kd_s, b_s) in enumerate(dirs):
                units.append((r_s[j, rows, :], lw_s[j, rows, :], kd_s[j, rows, :], v_s[j, rows, :],
                              kk_s[j, rows, :], b_s[j, rows, :], e == 1))
                ids.append((j * 2 + e) * n_chunks + c)
        for uid, (lhs, add) in zip(ids, _wkv_intra(units, consts)):
            lhs_s[uid] = lhs
            add_s[uid] = add
        return carry

    lax.fori_loop(0, pairs * groups, intra_body, 0)

    for j in range(pairs):
        for e in range(2):
            if has_s0:
                st_s[2 * j + e] = s0_ref[0, e, j].T
            else:
                st_s[2 * j + e] = jnp.zeros((LANES, LANES), F32)

    def state_body(it, carry):
        chunk = (it, n_chunks - 1 - it)
        uids = [(j * 2 + e) * n_chunks + chunk[e] for j in range(pairs) for e in range(2)]
        sts = [st_s[ch].astype(BF16) for ch in range(2 * pairs)]
        res = [mm(lhs_s[uid], st) + add_s[uid] for uid, st in zip(uids, sts)]
        for ch, rr in enumerate(res):
            j, e = divmod(ch, 2)
            y_s = yb_s if e else yf_s
            y_s[j, pl.ds(pl.multiple_of(chunk[e] * CHUNK, CHUNK), CHUNK), :] = rr[:CHUNK] + rr[CHUNK:STACK]
            st_s[ch] = rr[STACK:]
        return carry

    lax.fori_loop(0, n_chunks, state_body, 0)
    for j in range(pairs):
        for e in range(2):
            sn_ref[0, e, j] = st_s[2 * j + e].T

    inv_d = 1.0 / HEAD_DIM
    for j in range(pairs):
        cols = slice(j * LANES, (j + 1) * LANES)
        y = yf_s[j] + yb_s[j]
        mean = _head_sum(y, ones) * inv_d
        dlt = y - mean
        var = _head_sum(dlt * dlt, ones) * inv_d
        yn = dlt * lax.rsqrt(var + GN_EPS) * lng_ref[:, cols] + lnb_ref[:, cols]
        o_ref[0, :, cols] = (yn + bonus_s[:, cols]) * gate_s[:, cols]


def _rwkv_branch(u_rw, s0_big, p, pairs):
    bsz, t_len, _ = u_rw.shape
    has_s0 = s0_big is not None
    width = pairs * LANES
    seg = RW_WIDTH // width
    tok = lambda off: pl.BlockSpec((1, t_len, width), lambda b, j: (b, 0, off + j))
    mu = lambda off: pl.BlockSpec((2, width), lambda b, j: (0, off + j))
    vec2 = pl.BlockSpec((2, width), lambda b, j: (0, j))
    vec1 = pl.BlockSpec((1, width), lambda b, j: (0, j))
    mat = pl.BlockSpec((LANES, width), lambda b, j: (0, j))
    lora_w = 3 * LANES
    lora_blk = 3 * RW_WIDTH // lora_w
    in_specs = [tok(0), tok(seg), tok(2 * seg),
                pl.BlockSpec((1, t_len, lora_w), lambda b, j: (b, 0, lora_blk)),
                mu(0), mu(seg), mu(2 * seg),
                pl.BlockSpec((2, lora_w), lambda b, j: (0, lora_blk)),
                vec2, vec2, mat, mat, mat, vec1, vec1, vec1, vec1, vec1,
                _const_spec((LANES, LANES)), _const_spec((2, CHUNK, CHUNK)), _const_spec((4, STACK, 2 * STACK))]
    args = [u_rw, u_rw, u_rw, u_rw, p["shift_mu"], p["shift_mu"], p["shift_mu"], p["shift_mu"],
            p["rw_w0"], p["rw_a0"],
            p["rw_w_up"].reshape(2 * LORA_DECAY, RW_WIDTH), p["rw_a_up"].reshape(2 * LORA_ICLR, RW_WIDTH),
            p["rw_g_up"], p["rw_k_k"].reshape(1, -1), p["rw_k_a"].reshape(1, -1),
            p["rw_r_k"].reshape(1, -1), p["rw_ln_g"].reshape(1, -1), p["rw_ln_b"].reshape(1, -1),
            *_wkv_constants()]
    st_spec = pl.BlockSpec((1, 2, pairs, LANES, LANES), lambda b, j: (b, 0, j, 0, 0))
    if has_s0:
        in_specs.append(st_spec)
        args.append(s0_big)
    n_units = 2 * pairs * (t_len // CHUNK)
    per_pair = pltpu.VMEM((pairs, t_len, LANES), F32)
    full = pltpu.VMEM((t_len, width), F32)
    scratch = [per_pair] * 9 + [full, full, per_pair, per_pair,
                                pltpu.VMEM((n_units, 2 * STACK, LANES), BF16),
                                pltpu.VMEM((n_units, 2 * STACK, LANES), F32),
                                pltpu.VMEM((2 * pairs, LANES, LANES), F32)]
    o_rw, s_new = pl.pallas_call(
        functools.partial(_rwkv_kernel, t_len=t_len, has_s0=has_s0, pairs=pairs),
        grid=(bsz, PAIRS // pairs),
        in_specs=in_specs,
        out_specs=[pl.BlockSpec((1, t_len, width), lambda b, j: (b, 0, j)), st_spec],
        out_shape=[jax.ShapeDtypeStruct((bsz, t_len, RW_WIDTH), F32),
                   jax.ShapeDtypeStruct((bsz, 2, PAIRS, LANES, LANES), F32)],
        scratch_shapes=scratch,
        compiler_params=_params(2),
        name="rwkv_branch",
    )(*args)
    return o_rw, s_new


def _state_to_big(s0):
    bsz = s0.shape[0]
    x = s0.reshape(bsz, 2, PAIRS, 2, HEAD_DIM, HEAD_DIM)
    z = jnp.zeros_like(x[:, :, :, 0])
    top = jnp.concatenate([x[:, :, :, 0], z], axis=-1)
    bot = jnp.concatenate([z, x[:, :, :, 1]], axis=-1)
    return jnp.concatenate([top, bot], axis=-2)


def _state_from_big(s_big):
    bsz = s_big.shape[0]
    h0 = s_big[:, :, :, :HEAD_DIM, :HEAD_DIM]
    h1 = s_big[:, :, :, HEAD_DIM:, HEAD_DIM:]
    return jnp.stack([h0, h1], axis=3).reshape(bsz, 2, RW_HEADS, HEAD_DIM, HEAD_DIM)


def _qk_norm(t, g, ones):
    ms = _head_sum(t * t, ones) * (1.0 / HEAD_DIM)
    return t * lax.rsqrt(ms + RMS_EPS) * g


def _nt(x, y):
    return lax.dot_general(x, y, (((1,), (1,)), ((), ())), preferred_element_type=F32)


def _na_ctx_kernel(q_ref, k_ref, v_ref, qg_ref, kg_ref, o_ref, kn_ref, vc_ref):
    ones = _head_ones()
    lo_half = lax.broadcasted_iota(jnp.int32, (1, LANES), 1) < HEAD_DIM
    lo = jnp.where(lo_half, 1.0, 0.0)
    t_len = q_ref.shape[1]
    scale = HEAD_DIM ** -0.5
    qs, ks, vs = [], [], []
    for j in range(NA_WIDTH // LANES):
        cols = slice(j * LANES, (j + 1) * LANES)
        qn = _qk_norm(q_ref[0, :, cols], qg_ref[...], ones)
        kn = _qk_norm(k_ref[0, :, cols], kg_ref[...], ones)
        v = v_ref[0, :, cols]
        kn_ref[0, :, cols] = kn
        vc_ref[0, :, cols] = v
        qs.append(jnp.concatenate([qn * lo, qn * (1.0 - lo)], axis=0).astype(BF16))
        ks.append(kn.astype(BF16))
        vs.append(v.astype(BF16))
    logits = [_nt(q, k) * scale for q, k in zip(qs, ks)]
    ms = [jnp.max(s, axis=-1, keepdims=True) for s in logits]
    ps = [jnp.exp(s - m) for s, m in zip(logits, ms)]
    ls = [jnp.sum(p, axis=-1, keepdims=True) for p in ps]
    outs = [jnp.dot(p.astype(BF16), v, preferred_element_type=F32) / l for p, v, l in zip(ps, vs, ls)]
    for j, o in enumerate(outs):
        o_ref[0, :, j * LANES:(j + 1) * LANES] = jnp.where(lo_half, o[:t_len], o[t_len:])


def _na_context(u_na, q_g, k_g):
    bsz, t_len, _ = u_na.shape
    tok = lambda seg: pl.BlockSpec((1, t_len, NA_WIDTH), lambda b: (b, 0, seg))
    out_blk = pl.BlockSpec((1, t_len, NA_WIDTH), lambda b: (b, 0, 0))
    g2 = lambda g: jnp.tile(g.reshape(1, HEAD_DIM), (1, 2))
    shp = jax.ShapeDtypeStruct((bsz, t_len, NA_WIDTH), F32)
    return pl.pallas_call(
        _na_ctx_kernel,
        grid=(bsz,),
        in_specs=[tok(0), tok(1), tok(2), _const_spec((1, LANES)), _const_spec((1, LANES))],
        out_specs=[out_blk, out_blk, out_blk],
        out_shape=[shp, shp, shp],
        compiler_params=_params(1),
        name="na_context",
    )(u_na, u_na, u_na, g2(q_g), g2(k_g))


NA_ROW_ILP = 2


def _na_lat_kernel(q_ref, k_ref, v_ref, kc_ref, vc_ref, qg_ref, kg_ref, tab_ref, o_ref,
                   q0_s, q1_s, kn_s, v_s, kc_s, vc_s, *, rows, kr):
    ones = _head_ones()
    lo_half = lax.broadcasted_iota(jnp.int32, (1, LANES), 1) < HEAD_DIM
    lo = jnp.where(lo_half, 1.0, 0.0)
    qn = _qk_norm(q_ref[0], qg_ref[...], ones)
    q0_s[...] = (qn * lo).astype(BF16)
    q1_s[...] = (qn * (1.0 - lo)).astype(BF16)
    kn_s[...] = _qk_norm(k_ref[0], kg_ref[...], ones).astype(BF16)
    v_s[...] = v_ref[0].astype(BF16)
    kc_s[...] = kc_ref[0].astype(BF16)
    vc_s[...] = vc_ref[0].astype(BF16)
    scale = HEAD_DIM ** -0.5
    win = kr * GRID_W

    def body(it, carry):
        qs, k_rows, q_rows, biases = [], [], [], []
        for s in range(NA_ROW_ILP):
            i = it * NA_ROW_ILP + s
            r0 = jnp.clip(i - kr // 2, 0, rows - kr)
            d0 = r0 - i + (NA_ROWS - 1)
            qr = pl.ds(pl.multiple_of(i * GRID_W, GRID_W), GRID_W)
            q_rows.append(qr)
            k_rows.append(pl.ds(pl.multiple_of(r0 * GRID_W, GRID_W), win))
            qs.append(jnp.concatenate([q0_s[qr, :], q1_s[qr, :]], axis=0))
            biases.append(jnp.concatenate(
                [jnp.concatenate([tab_ref[h, d0 + 2 * m] for m in range(kr // 2)], axis=1) for h in range(2)],
                axis=0))
        lw = [_nt(q, kn_s[kr_, :]) * scale + b for q, kr_, b in zip(qs, k_rows, biases)]
        lc = [_nt(q, kc_s[...]) * scale for q in qs]
        ms = [jnp.maximum(jnp.max(a, axis=-1, keepdims=True), jnp.max(c, axis=-1, keepdims=True))
              for a, c in zip(lw, lc)]
        pw = [jnp.exp(a - m) for a, m in zip(lw, ms)]
        pc = [jnp.exp(c - m) for c, m in zip(lc, ms)]
        ls = [jnp.sum(a, axis=-1, keepdims=True) + jnp.sum(c, axis=-1, keepdims=True) for a, c in zip(pw, pc)]
        outs = [(jnp.dot(a.astype(BF16), v_s[kr_, :], preferred_element_type=F32)
                 + jnp.dot(c.astype(BF16), vc_s[...], preferred_element_type=F32)) / l
                for a, c, kr_, l in zip(pw, pc, k_rows, ls)]
        for qr, o in zip(q_rows, outs):
            o_ref[0, qr, :] = jnp.where(lo_half, o[:GRID_W], o[GRID_W:])
        return carry

    lax.fori_loop(0, rows // NA_ROW_ILP, body, 0)


def _latent_bias_table(rpb):
    qc = np.arange(GRID_W)[:, None]
    kc = np.arange(GRID_W)[None, :]
    ws = np.clip(qc - NA_COLS // 2, 0, GRID_W - NA_COLS)
    valid = (kc >= ws) & (kc < ws + NA_COLS)
    dc = np.clip(kc - qc, -(NA_COLS - 1), NA_COLS - 1) + NA_COLS - 1
    onehot = (dc[None] == np.arange(2 * NA_COLS - 1)[:, None, None]).astype(np.float32)
    cb = jnp.einsum("hdc,cqk->hdqk", rpb, jnp.asarray(onehot), precision=lax.Precision.HIGHEST)
    cb = jnp.where(valid[None, None], cb, NEG_INF)
    return jnp.concatenate([cb[:, :-1], cb[:, 1:]], axis=-1)


def _na_latent(u_na, k_ctx, v_ctx, q_g, k_g, rpb):
    bsz, t_len, _ = u_na.shape
    rows = t_len // GRID_W
    kr = min(NA_ROWS, rows)
    assert kr % 2 == 0 and rows % NA_ROW_ILP == 0
    ctx_len = k_ctx.shape[1]
    seg = NA_WIDTH // LANES
    tok = lambda off: pl.BlockSpec((1, t_len, LANES), lambda b, j: (b, 0, off + j))
    ctx = pl.BlockSpec((1, ctx_len, LANES), lambda b, j: (b, 0, j))
    g2 = lambda g: jnp.tile(g.reshape(1, HEAD_DIM), (1, 2))
    tab = _latent_bias_table(rpb)
    tok_s = pltpu.VMEM((t_len, LANES), BF16)
    ctx_s = pltpu.VMEM((ctx_len, LANES), BF16)
    return pl.pallas_call(
        functools.partial(_na_lat_kernel, rows=rows, kr=kr),
        grid=(bsz, seg),
        in_specs=[tok(0), tok(seg), tok(2 * seg), ctx, ctx,
                  _const_spec((1, LANES)), _const_spec((1, LANES)),
                  pl.BlockSpec((2, 2 * NA_ROWS - 2, GRID_W, 2 * GRID_W), lambda b, j: (j, 0, 0, 0))],
        out_specs=pl.BlockSpec((1, t_len, LANES), lambda b, j: (b, 0, j)),
        out_shape=jax.ShapeDtypeStruct((bsz, t_len, NA_WIDTH), F32),
        scratch_shapes=[tok_s, tok_s, tok_s, tok_s, ctx_s, ctx_s],
        compiler_params=_params(2),
        name="na_latent",
    )(u_na, u_na, u_na, k_ctx, v_ctx, g2(q_g), g2(k_g), tab)


def _outproj_kernel(x_ref, orw_ref, ona_ref, gt_ref, mod_ref, g_ref, wor_ref, won_ref, wout_ref, x1_ref, h2_ref):
    g_rw = jax.nn.sigmoid(gt_ref[0, :, :D_MODEL])
    g_na = jax.nn.sigmoid(gt_ref[0, :, D_MODEL:])
    merged = g_rw * _dot(orw_ref[0], wor_ref[...]) + g_na * _dot(ona_ref[0], won_ref[...])
    x1 = x_ref[0] + mod_ref[0, 2:3, :] * _dot(merged, wout_ref[...])
    x1_ref[0] = x1
    h2 = _rms_rows(x1) * g_ref[...]
    h2_ref[0] = (h2 * (1.0 + mod_ref[0, 4:5, :]) + mod_ref[0, 3:4, :]).astype(BF16)


def _out_proj(x, o_rw, o_na, gates, mod_all, mod_off, mod_stride, norm_g, w_or, w_on, w_out):
    bsz, t_len, _ = x.shape
    tm = TOKEN_TILE
    row = lambda b, i: (b, i, 0)
    return pl.pallas_call(
        _outproj_kernel,
        grid=(bsz, t_len // tm),
        in_specs=[pl.BlockSpec((1, tm, D_MODEL), row),
                  pl.BlockSpec((1, tm, RW_WIDTH), row),
                  pl.BlockSpec((1, tm, NA_WIDTH), row),
                  pl.BlockSpec((1, tm, GATE_COLS), row),
                  pl.BlockSpec((1, 6, D_MODEL), lambda b, i: (mod_off + mod_stride * b, 0, 0)),
                  _const_spec((1, D_MODEL)),
                  _const_spec(w_or.shape), _const_spec(w_on.shape), _const_spec(w_out.shape)],
        out_specs=[pl.BlockSpec((1, tm, D_MODEL), row), pl.BlockSpec((1, tm, D_MODEL), row)],
        out_shape=[jax.ShapeDtypeStruct((bsz, t_len, D_MODEL), F32),
                   jax.ShapeDtypeStruct((bsz, t_len, D_MODEL), BF16)],
        compiler_params=_params(2),
        name="out_proj",
    )(x, o_rw, o_na, gates, mod_all, norm_g.reshape(1, -1), w_or, w_on, w_out)


def _ffn_kernel(h2_ref, x1_ref, mod_ref, w1_ref, w3_ref, w2_ref, y_ref):
    h2 = h2_ref[0]
    acc = jnp.zeros((h2.shape[0], D_MODEL), F32)
    for c in range(FF_HIDDEN // FF_CHUNK):
        cols = slice(c * FF_CHUNK, (c + 1) * FF_CHUNK)
        a = jnp.dot(h2, w1_ref[:, cols], preferred_element_type=F32)
        b = jnp.dot(h2, w3_ref[:, cols], preferred_element_type=F32)
        hh = (a * jax.nn.sigmoid(a) * b).astype(BF16)
        acc = acc + jnp.dot(hh, w2_ref[cols, :], preferred_element_type=F32)
    y_ref[0] = x1_ref[0] + mod_ref[0, 5:6, :] * acc


def _ffn(h2, x1, mod_all, mod_off, mod_stride, w1, w3, w2):
    bsz, t_len, _ = x1.shape
    tm = TOKEN_TILE
    row = lambda b, i: (b, i, 0)
    return pl.pallas_call(
        _ffn_kernel,
        grid=(bsz, t_len // tm),
        in_specs=[pl.BlockSpec((1, tm, D_MODEL), row),
                  pl.BlockSpec((1, tm, D_MODEL), row),
                  pl.BlockSpec((1, 6, D_MODEL), lambda b, i: (mod_off + mod_stride * b, 0, 0)),
                  _const_spec(w1.shape), _const_spec(w3.shape), _const_spec(w2.shape)],
        out_specs=pl.BlockSpec((1, tm, D_MODEL), row),
        out_shape=jax.ShapeDtypeStruct((bsz, t_len, D_MODEL), F32),
        compiler_params=_params(2),
        name="ffn",
    )(h2, x1, mod_all, w1, w3, w2)


def _trunk(x, mod_all, mod_off, mod_stride, s0_big, ctx_kv, p, wb):
    shape = x.shape
    flat = (lambda t: t.reshape(1, -1, t.shape[-1])) if mod_stride == 0 else (lambda t: t)
    unflat = lambda t: t.reshape(shape[0], shape[1], t.shape[-1])
    u_rw, u_na, gates = _in_proj(flat(x), mod_all, mod_off, mod_stride, p["norm1_g"], wb["w_in"])
    u_rw, u_na = unflat(u_rw), unflat(u_na)
    o_rw, s_new = _rwkv_branch(u_rw, s0_big, p, RWKV_PAIRS_CTX if ctx_kv is None else RWKV_PAIRS_LAT)
    if ctx_kv is None:
        o_na, k_new, v_new = _na_context(u_na, p["na_q_g"], p["na_k_g"])
    else:
        o_na = _na_latent(u_na, ctx_kv[0], ctx_kv[1], p["na_q_g"], p["na_k_g"], p["na_rpb"])
        k_new = v_new = None
    x1, h2 = _out_proj(flat(x), flat(o_rw), flat(o_na), gates, mod_all, mod_off, mod_stride, p["norm2_g"],
                       wb["w_o_rwkv"], wb["w_o_na"], wb["w_out"])
    y = _ffn(h2, x1, mod_all, mod_off, mod_stride, wb["ffn_w1"], wb["ffn_w3"], wb["ffn_w2"])
    return unflat(y), s_new, k_new, v_new


def kernel(x_prompt, x_sample, state_rwkv, cache_na_k, cache_na_v, c, c_ctx, norm1_g, norm2_g, w_ada, b_ada,
           w_in, shift_mu, rw_w0, rw_w_up, rw_a0, rw_a_up, rw_g_up, rw_k_k, rw_k_a, rw_r_k, rw_ln_g, rw_ln_b,
           na_q_g, na_k_g, na_rpb, w_o_rwkv, w_o_na, w_out, ffn_w1, ffn_w3, ffn_w2):
    depth = w_in.shape[0]
    bsz, seq = x_prompt.shape[:2]
    dec = x_sample.shape[0]
    n_vec = 8
    cvecs = jnp.concatenate([c_ctx[None, :], c, jnp.zeros((n_vec - 1 - dec, D_MODEL), F32)], axis=0)
    y_p, y_s = x_prompt, x_sample
    new_s, new_k, new_v = [], [], []
    for l in range(depth):
        p = dict(norm1_g=norm1_g[l], norm2_g=norm2_g[l], shift_mu=shift_mu[l], rw_w0=rw_w0[l],
                 rw_w_up=rw_w_up[l], rw_a0=rw_a0[l], rw_a_up=rw_a_up[l], rw_g_up=rw_g_up[l],
                 rw_k_k=rw_k_k[l], rw_k_a=rw_k_a[l], rw_r_k=rw_r_k[l], rw_ln_g=rw_ln_g[l],
                 rw_ln_b=rw_ln_b[l], na_q_g=na_q_g[l], na_k_g=na_k_g[l], na_rpb=na_rpb[l])
        wb = dict(w_in=w_in[l].astype(BF16), w_o_rwkv=w_o_rwkv[l].astype(BF16), w_o_na=w_o_na[l].astype(BF16),
                  w_out=w_out[l].astype(BF16), ffn_w1=ffn_w1[l].astype(BF16), ffn_w3=ffn_w3[l].astype(BF16),
                  ffn_w2=ffn_w2[l].astype(BF16))
        mod_all = _modulation(cvecs, w_ada[l], b_ada[l])[:1 + dec].reshape(1 + dec, 6, D_MODEL)
        y_p, s_big, k_l, v_l = _trunk(y_p, mod_all, 0, 0, None, None, p, wb)
        new_s.append(_state_from_big(s_big))
        new_k.append(k_l.reshape(bsz, seq, NA_HEADS, HEAD_DIM))
        new_v.append(v_l.reshape(bsz, seq, NA_HEADS, HEAD_DIM))
        ctx_k = cache_na_k[:, l].reshape(dec, -1, NA_WIDTH)
        ctx_v = cache_na_v[:, l].reshape(dec, -1, NA_WIDTH)
        y_s, _, _, _ = _trunk(y_s, mod_all, 1, 1, _state_to_big(state_rwkv[:, l]), (ctx_k, ctx_v), p, wb)
    return (y_p, y_s, jnp.stack(new_s, axis=1), jnp.stack(new_k, axis=1), jnp.stack(new_v, axis=1))
```

```python
import functools

import numpy as np
import jax
import jax.numpy as jnp
from jax import lax
from jax.experimental import pallas as pl
from jax.experimental.pallas import tpu as pltpu

D_MODEL = 1024
GRID_W = 64
HEAD_DIM = 64
RW_HEADS = 8
RW_WIDTH = RW_HEADS * HEAD_DIM
NA_HEADS = 8
NA_WIDTH = NA_HEADS * HEAD_DIM
LORA_DECAY = 64
LORA_ICLR = 64
LORA_GATE = 128
NA_ROWS = 8
NA_COLS = 16
FF_HIDDEN = 2816
RW_COLS = 3 * RW_WIDTH + 2 * LORA_DECAY + 2 * LORA_ICLR + LORA_GATE
NA_IN_COLS = 3 * NA_WIDTH
GATE_COLS = 2 * D_MODEL
RMS_EPS = 1e-6
GN_EPS = 64e-5
L2_EPS = 1e-12
NEG_INF = -1e30
DECAY_SCALE = float(np.exp(-0.5))
QK_SCALE = HEAD_DIM ** -0.5
assert QK_SCALE == 0.125

LANES = 128
PAIRS = RW_HEADS // 2
CHUNK = 64
STACK = 2 * CHUNK
RWKV_PAIRS_CTX = 4
RWKV_PAIRS_LAT = 2
UNIT_CHUNKS = 4
TOKEN_TILE = 512
FF_CHUNK = 256
VMEM_LIMIT = 56 * 1024 * 1024

F32 = jnp.float32
BF16 = jnp.bfloat16


def _dot(a, b):
    return jnp.dot(a.astype(BF16), b.astype(BF16), preferred_element_type=F32)


def _dot_nt(a, b):
    return lax.dot_general(a.astype(BF16), b.astype(BF16), (((1,), (1,)), ((), ())),
                           preferred_element_type=F32)


def _split2(x):
    hi = x.astype(BF16)
    lo = (x - hi.astype(F32)).astype(BF16)
    return hi, lo


def _dot_exact_lhs(a_exact, b):
    h, l = _split2(b)
    d = lambda x: jnp.dot(a_exact, x, preferred_element_type=F32)
    return d(h) + d(l)


def _dot3(a, b):
    ah, al = _split2(a)
    bh, bl = _split2(b)
    d = lambda x, y: jnp.dot(x, y, preferred_element_type=F32)
    return d(ah, bh) + d(al, bh) + d(ah, bl)


def _head_ones():
    r = lax.broadcasted_iota(jnp.int32, (LANES, LANES), 0) // HEAD_DIM
    c = lax.broadcasted_iota(jnp.int32, (LANES, LANES), 1) // HEAD_DIM
    return jnp.where(r == c, 1.0, 0.0).astype(BF16)


def _head_sum(x, ones):
    return jnp.dot(x.astype(BF16), ones, preferred_element_type=F32)


def _rms_rows(x):
    return x * lax.rsqrt(jnp.mean(x * x, axis=-1, keepdims=True) + RMS_EPS)


def _const_spec(shape):
    nd = len(shape)
    return pl.BlockSpec(shape, lambda *_: (0,) * nd, pipeline_mode=pl.Buffered(1))


def _params(n_axes):
    return pltpu.CompilerParams(dimension_semantics=("arbitrary",) * n_axes,
                                vmem_limit_bytes=VMEM_LIMIT)


def _mod_kernel(c_ref, w_ref, b_ref, o_ref):
    s = c_ref[...]
    s = s * jax.nn.sigmoid(s)
    o_ref[...] = _dot3(s, w_ref[...]) + b_ref[...]


def _modulation(cvecs, w_ada, b_ada):
    n = cvecs.shape[0]
    tn = 1536
    return pl.pallas_call(
        _mod_kernel,
        grid=(6 * D_MODEL // tn,),
        in_specs=[pl.BlockSpec((n, D_MODEL), lambda j: (0, 0)),
                  pl.BlockSpec((D_MODEL, tn), lambda j: (0, j)),
                  pl.BlockSpec((1, tn), lambda j: (0, j))],
        out_specs=pl.BlockSpec((n, tn), lambda j: (0, j)),
        out_shape=jax.ShapeDtypeStruct((n, 6 * D_MODEL), F32),
        compiler_params=_params(1),
        name="modulation",
    )(cvecs, w_ada, b_ada.reshape(1, -1))


def _inproj_kernel(x_ref, mod_ref, g_ref, w_ref, urw_ref, una_ref, gt_ref):
    x = x_ref[0]
    h = _rms_rows(x) * g_ref[...]
    h = (h * (1.0 + mod_ref[0, 1:2, :]) + mod_ref[0, 0:1, :]).astype(BF16)
    d = lambda lo, hi: jnp.dot(h, w_ref[:, lo:hi], preferred_element_type=F32)
    urw_ref[0] = d(0, RW_COLS)
    una_ref[0] = d(RW_COLS, RW_COLS + NA_IN_COLS)
    gt_ref[0] = d(RW_COLS + NA_IN_COLS, RW_COLS + NA_IN_COLS + GATE_COLS)


def _in_proj(x, mod_all, mod_off, mod_stride, norm_g, w_in_bf):
    bsz, t_len, _ = x.shape
    tm = TOKEN_TILE
    row = lambda b, i: (b, i, 0)
    return pl.pallas_call(
        _inproj_kernel,
        grid=(bsz, t_len // tm),
        in_specs=[pl.BlockSpec((1, tm, D_MODEL), row),
                  pl.BlockSpec((1, 6, D_MODEL), lambda b, i: (mod_off + mod_stride * b, 0, 0)),
                  _const_spec((1, D_MODEL)),
                  _const_spec(w_in_bf.shape)],
        out_specs=[pl.BlockSpec((1, tm, RW_COLS), row),
                   pl.BlockSpec((1, tm, NA_IN_COLS), row),
                   pl.BlockSpec((1, tm, GATE_COLS), row)],
        out_shape=[jax.ShapeDtypeStruct((bsz, t_len, RW_COLS), F32),
                   jax.ShapeDtypeStruct((bsz, t_len, NA_IN_COLS), F32),
                   jax.ShapeDtypeStruct((bsz, t_len, GATE_COLS), F32)],
        compiler_params=_params(2),
        name="in_proj",
    )(x, mod_all, norm_g.reshape(1, -1), w_in_bf)


def _shift(x, mu):
    t_len = x.shape[0]
    row = lax.broadcasted_iota(jnp.int32, x.shape, 0)
    prev = jnp.where(row == 0, 0.0, pltpu.roll(x, 1, 0))
    nxt = jnp.where(row == t_len - 1, 0.0, pltpu.roll(x, t_len - 1, 0))
    return x + mu[0:1, :] * (prev - x) + mu[1:2, :] * (nxt - x)


def _stack_heads(x, lane_lo):
    return jnp.concatenate([x * lane_lo, x * (1.0 - lane_lo)], axis=0)


def _wkv_intra(units, consts):
    tri, mask_s, mask_i, eye, lane_lo, blk = consts
    eye_f = jnp.where(eye, 1.0, 0.0)
    stack = lambda z: _stack_heads(z, lane_lo)

    cums = [_dot_exact_lhs(tri[int(u[6])], u[1]) for u in units]

    prep = []
    for (r, lw, kd, v, kk, b, reverse), cum in zip(units, cums):
        mid_row = CHUNK // 2 if reverse else CHUNK // 2 - 1
        tot_row = 0 if reverse else CHUNK - 1
        a = -kk
        ex = cum - lw
        mid = cum[mid_row:mid_row + 1, :]
        tot = cum[tot_row:tot_row + 1, :]
        up = jnp.exp(cum - mid)
        dn = jnp.exp(mid - cum)
        tail = jnp.exp(tot - cum)
        prep.append(dict(
            at_m=stack(a * jnp.exp(ex - mid)).astype(BF16),
            rt_m=stack(r * up).astype(BF16),
            btkt=jnp.concatenate([stack(b * dn), stack(kd * dn)], axis=0).astype(BF16),
            a_e=stack(a * jnp.exp(ex)),
            r_e=stack(r * jnp.exp(cum)),
            bk_t=jnp.concatenate([stack(b * tail), stack(kd * tail)], axis=0).T.astype(BF16),
            vv=stack(v).astype(BF16),
            diag=jnp.where(eye, jnp.exp(tot), 0.0),
            rev=int(reverse)))

    ntd = lambda x, y: lax.dot_general(x, y, (((1,), (1,)), ((), ())), preferred_element_type=F32)
    mm = lambda x, y: jnp.dot(x, y, preferred_element_type=F32)
    top = [mask_s[p["rev"]] * ntd(p["at_m"], p["btkt"]) for p in prep]
    bot = [(mask_i[p["rev"]] * ntd(p["rt_m"], p["btkt"])).astype(BF16) for p in prep]

    diag_blk, swap_eye = blk
    off_blk = 1.0 - diag_blk
    both = [t[:, :LANES] + swap_eye for t in top]
    steps = CHUNK.bit_length() - 1
    diag_bf = diag_blk.astype(BF16)
    for j in range(steps):
        packed = [q.astype(BF16) for q in both]
        res = [mm(qb * diag_bf, qb) for qb in packed]
        both = [r + off_blk * q for r, q in zip(res, both)]
    ms = [pltpu.roll(q * off_blk, HEAD_DIM, 1) - eye_f for q in both]
    x0 =[jnp.concatenate([p["a_e"], mm(t[:, LANES:].astype(BF16), p["vv"])], axis=1) for p, t in zip(prep, top)]
    xs = [x + mm(m.astype(BF16), x.astype(BF16)) for x, m in zip(x0, ms)]

    out = []
    zeros = jnp.zeros((STACK, LANES), BF16)
    for p, x, bt in zip(prep, xs, bot):
        rhs = jnp.concatenate([x.astype(BF16), jnp.concatenate([zeros, p["vv"]], axis=1)], axis=0)
        lhs = jnp.concatenate([bt, p["bk_t"]], axis=0)
        res = mm(lhs, rhs)
        lhs2 = res[:, :LANES] + jnp.concatenate([p["r_e"], p["diag"]], axis=0)
        out.append((lhs2.astype(BF16), res[:, LANES:]))
    return out


def _wkv_constants():
    lane_head = np.arange(LANES) // HEAD_DIM
    ones = (lane_head[:, None] == lane_head[None, :]).astype(np.float32)
    t = np.arange(CHUNK)
    tri = np.stack([t[None, :] <= t[:, None], t[None, :] >= t[:, None]]).astype(np.float32)
    s = np.arange(STACK)
    same = (s[:, None] // CHUNK) == (s[None, :] // CHUNK)
    rs, cs = s[:, None], s[None, :]
    masks = np.stack([same & (cs < rs), same & (cs > rs), same & (cs <= rs), same & (cs >= rs)]).astype(np.float32)
    masks = np.concatenate([masks, masks], axis=-1)
    return jnp.asarray(ones, BF16), jnp.asarray(tri, BF16), jnp.asarray(masks, F32)


def _rwkv_kernel(*refs, t_len, has_s0, pairs):
    (r_ref, k_ref, v_ref, lo_ref, mur_ref, muk_ref, muv_ref, mul_ref, w0_ref, a0_ref, wup_ref, aup_ref,
     gup_ref, kk_ref, ka_ref, rk_ref, lng_ref, lnb_ref, ones_ref, tri_ref, mask_ref) = refs[:21]
    pos = 21
    s0_ref = None
    if has_s0:
        s0_ref = refs[pos]
        pos += 1
    o_ref, sn_ref = refs[pos], refs[pos + 1]
    (r_s, v_s, kk_s, b0_s, b1_s, lw0_s, lw1_s, kd0_s, kd1_s, gate_s, bonus_s, yf_s, yb_s,
     lhs_s, add_s, st_s) = refs[pos + 2:]

    ones = ones_ref[...]
    lane = lax.broadcasted_iota(jnp.int32, (1, LANES), 1)
    lo_half = lane < HEAD_DIM
    lane_lo = jnp.where(lo_half, 1.0, 0.0)
    mm = lambda x, y: jnp.dot(x, y, preferred_element_type=F32)

    lo = _shift(lo_ref[0], mul_ref[...])
    wd = jnp.tanh(lo[:, 0:LANES])
    ad = lo[:, LANES:2 * LANES]
    sig_gd = jax.nn.sigmoid(lo[:, 2 * LANES:3 * LANES]).astype(BF16)
    wd_split = [_split2(wd * m) for m in (lane_lo, 1.0 - lane_lo)]
    ad_bf = [(ad * m).astype(BF16) for m in (lane_lo, 1.0 - lane_lo)]
    for j in range(pairs):
        cols = slice(j * LANES, (j + 1) * LANES)
        r = _shift(r_ref[0, :, cols], mur_ref[:, cols])
        k = _shift(k_ref[0, :, cols], muk_ref[:, cols])
        v = _shift(v_ref[0, :, cols], muv_ref[:, cols])
        kk = k * kk_ref[:, cols]
        kk = kk * lax.rsqrt(_head_sum(kk * kk, ones) + L2_EPS)
        wup_h, wup_l = _split2(wup_ref[:, cols])
        aup = aup_ref[:, cols].astype(BF16)
        kdirs = []
        for e, (lw_s, kd_s, b_s) in enumerate(((lw0_s, kd0_s, b0_s), (lw1_s, kd1_s, b1_s))):
            wd_h, wd_l = wd_split[e]
            w_lin = w0_ref[e:e + 1, cols] + (mm(wd_h, wup_h) + mm(wd_l, wup_h) + mm(wd_h, wup_l))
            lw_s[j] = -DECAY_SCALE * jax.nn.sigmoid(w_lin)
            iclr = jax.nn.sigmoid(a0_ref[e:e + 1, cols] + mm(ad_bf[e], aup))
            kd = k * (1.0 + (iclr - 1.0) * ka_ref[:, cols])
            kd_s[j] = kd
            b_s[j] = kk * iclr
            kdirs.append(kd)
        gate_s[:, cols] = mm(sig_gd, gup_ref[:, cols].astype(BF16))
        bonus_s[:, cols] = _head_sum(r * (0.5 * (kdirs[0] + kdirs[1])) * rk_ref[:, cols], ones) * v
        r_s[j] = r
        v_s[j] = v
        kk_s[j] = kk

    n_chunks = t_len // CHUNK
    groups = n_chunks // UNIT_CHUNKS
    rs = lax.broadcasted_iota(jnp.int32, (STACK, STACK), 0)
    cs = lax.broadcasted_iota(jnp.int32, (STACK, STACK), 1)
    as_f32 = lambda m: jnp.where(m, 1.0, 0.0)
    blk = (as_f32(rs // CHUNK == cs // CHUNK), as_f32(cs == (rs + CHUNK) % STACK))
    consts = ((tri_ref[0], tri_ref[1]), (mask_ref[0], mask_ref[1]), (mask_ref[2], mask_ref[3]),
              rs == cs, lane_lo, blk)
    dirs = ((lw0_s, kd0_s, b0_s), (lw1_s, kd1_s, b1_s))

    def intra_body(it, carry):
        j = it // groups
        g = it % groups
        units, ids = [], []
        for cc in range(UNIT_CHUNKS):
            c = g * UNIT_CHUNKS + cc
            rows = pl.ds(pl.multiple_of(c * CHUNK, CHUNK), CHUNK)
            for e, (lw_s, kd_s, b_s) in enumerate(dirs):
                units.append((r_s[j, rows, :], lw_s[j, rows, :], kd_s[j, rows, :], v_s[j, rows, :],
                              kk_s[j, rows, :], b_s[j, rows, :], e == 1))
                ids.append((j * 2 + e) * n_chunks + c)
        for uid, (lhs, add) in zip(ids, _wkv_intra(units, consts)):
            lhs_s[uid] = lhs
            add_s[uid] = add
        return carry

    lax.fori_loop(0, pairs * groups, intra_body, 0)

    for j in range(pairs):
        for e in range(2):
            if has_s0:
                st_s[2 * j + e] = s0_ref[0, e, j].T
            else:
                st_s[2 * j + e] = jnp.zeros((LANES, LANES), F32)

    def state_body(it, carry):
        chunk = (it, n_chunks - 1 - it)
        uids = [(j * 2 + e) * n_chunks + chunk[e] for j in range(pairs) for e in range(2)]
        sts = [st_s[ch].astype(BF16) for ch in range(2 * pairs)]
        res = [mm(lhs_s[uid], st) + add_s[uid] for uid, st in zip(uids, sts)]
        for ch, rr in enumerate(res):
            j, e = divmod(ch, 2)
            y_s = yb_s if e else yf_s
            y_s[j, pl.ds(pl.multiple_of(chunk[e] * CHUNK, CHUNK), CHUNK), :] = rr[:CHUNK] + rr[CHUNK:STACK]
            st_s[ch] = rr[STACK:]
        return carry

    lax.fori_loop(0, n_chunks, state_body, 0)
    for j in range(pairs):
        for e in range(2):
            sn_ref[0, e, j] = st_s[2 * j + e].T

    inv_d = 1.0 / HEAD_DIM
    for j in range(pairs):
        cols = slice(j * LANES, (j + 1) * LANES)
        y = yf_s[j] + yb_s[j]
        mean = _head_sum(y, ones) * inv_d
        dlt = y - mean
        var = _head_sum(dlt * dlt, ones) * inv_d
        yn = dlt * lax.rsqrt(var + GN_EPS) * lng_ref[:, cols] + lnb_ref[:, cols]
        o_ref[0, :, cols] = (yn + bonus_s[:, cols]) * gate_s[:, cols]


def _rwkv_branch(u_rw, s0_big, p, pairs):
    bsz, t_len, _ = u_rw.shape
    has_s0 = s0_big is not None
    width = pairs * LANES
    seg = RW_WIDTH // width
    tok = lambda off: pl.BlockSpec((1, t_len, width), lambda b, j: (b, 0, off + j))
    mu = lambda off: pl.BlockSpec((2, width), lambda b, j: (0, off + j))
    vec2 = pl.BlockSpec((2, width), lambda b, j: (0, j))
    vec1 = pl.BlockSpec((1, width), lambda b, j: (0, j))
    mat = pl.BlockSpec((LANES, width), lambda b, j: (0, j))
    lora_w = 3 * LANES
    lora_blk = 3 * RW_WIDTH // lora_w
    in_specs = [tok(0), tok(seg), tok(2 * seg),
                pl.BlockSpec((1, t_len, lora_w), lambda b, j: (b, 0, lora_blk)),
                mu(0), mu(seg), mu(2 * seg),
                pl.BlockSpec((2, lora_w), lambda b, j: (0, lora_blk)),
                vec2, vec2, mat, mat, mat, vec1, vec1, vec1, vec1, vec1,
                _const_spec((LANES, LANES)), _const_spec((2, CHUNK, CHUNK)), _const_spec((4, STACK, 2 * STACK))]
    args = [u_rw, u_rw, u_rw, u_rw, p["shift_mu"], p["shift_mu"], p["shift_mu"], p["shift_mu"],
            p["rw_w0"], p["rw_a0"],
            p["rw_w_up"].reshape(2 * LORA_DECAY, RW_WIDTH), p["rw_a_up"].reshape(2 * LORA_ICLR, RW_WIDTH),
            p["rw_g_up"], p["rw_k_k"].reshape(1, -1), p["rw_k_a"].reshape(1, -1),
            p["rw_r_k"].reshape(1, -1), p["rw_ln_g"].reshape(1, -1), p["rw_ln_b"].reshape(1, -1),
            *_wkv_constants()]
    st_spec = pl.BlockSpec((1, 2, pairs, LANES, LANES), lambda b, j: (b, 0, j, 0, 0))
    if has_s0:
        in_specs.append(st_spec)
        args.append(s0_big)
    n_units = 2 * pairs * (t_len // CHUNK)
    per_pair = pltpu.VMEM((pairs, t_len, LANES), F32)
    full = pltpu.VMEM((t_len, width), F32)
    scratch = [per_pair] * 9 + [full, full, per_pair, per_pair,
                                pltpu.VMEM((n_units, 2 * STACK, LANES), BF16),
                                pltpu.VMEM((n_units, 2 * STACK, LANES), F32),
                                pltpu.VMEM((2 * pairs, LANES, LANES), F32)]
    o_rw, s_new = pl.pallas_call(
        functools.partial(_rwkv_kernel, t_len=t_len, has_s0=has_s0, pairs=pairs),
        grid=(bsz, PAIRS // pairs),
        in_specs=in_specs,
        out_specs=[pl.BlockSpec((1, t_len, width), lambda b, j: (b, 0, j)), st_spec],
        out_shape=[jax.ShapeDtypeStruct((bsz, t_len, RW_WIDTH), F32),
                   jax.ShapeDtypeStruct((bsz, 2, PAIRS, LANES, LANES), F32)],
        scratch_shapes=scratch,
        compiler_params=_params(2),
        name="rwkv_branch",
    )(*args)
    return o_rw, s_new


def _state_to_big(s0):
    bsz = s0.shape[0]
    x = s0.reshape(bsz, 2, PAIRS, 2, HEAD_DIM, HEAD_DIM)
    z = jnp.zeros_like(x[:, :, :, 0])
    top = jnp.concatenate([x[:, :, :, 0], z], axis=-1)
    bot = jnp.concatenate([z, x[:, :, :, 1]], axis=-1)
    return jnp.concatenate([top, bot], axis=-2)


def _state_from_big(s_big):
    bsz = s_big.shape[0]
    h0 = s_big[:, :, :, :HEAD_DIM, :HEAD_DIM]
    h1 = s_big[:, :, :, HEAD_DIM:, HEAD_DIM:]
    return jnp.stack([h0, h1], axis=3).reshape(bsz, 2, RW_HEADS, HEAD_DIM, HEAD_DIM)


def _qk_norm(t, g, ones):
    ms = _head_sum(t * t, ones) * (1.0 / HEAD_DIM)
    return t * lax.rsqrt(ms + RMS_EPS) * g


def _nt(x, y):
    return lax.dot_general(x, y, (((1,), (1,)), ((), ())), preferred_element_type=F32)


def _na_ctx_kernel(q_ref, k_ref, v_ref, qg_ref, kg_ref, o_ref, kn_ref, vc_ref):
    ones = _head_ones()
    lo_half = lax.broadcasted_iota(jnp.int32, (1, LANES), 1) < HEAD_DIM
    lo = jnp.where(lo_half, 1.0, 0.0)
    t_len = q_ref.shape[1]
    qs, ks, vs = [], [], []
    for j in range(NA_WIDTH // LANES):
        cols = slice(j * LANES, (j + 1) * LANES)
        qn = _qk_norm(q_ref[0, :, cols], qg_ref[...], ones)
        kn = _qk_norm(k_ref[0, :, cols], kg_ref[...], ones)
        v = v_ref[0, :, cols]
        kn_ref[0, :, cols] = kn
        vc_ref[0, :, cols] = v
        qn = qn * QK_SCALE
        qs.append(jnp.concatenate([qn * lo, qn * (1.0 - lo)], axis=0).astype(BF16))
        ks.append(kn.astype(BF16))
        vs.append(v.astype(BF16))
    logits = [_nt(q, k) for q, k in zip(qs, ks)]
    ms = [jnp.max(s, axis=-1, keepdims=True) for s in logits]
    ps = [jnp.exp(s - m) for s, m in zip(logits, ms)]
    ls = [jnp.sum(p, axis=-1, keepdims=True) for p in ps]
    outs = [jnp.dot(p.astype(BF16), v, preferred_element_type=F32) / l for p, v, l in zip(ps, vs, ls)]
    for j, o in enumerate(outs):
        o_ref[0, :, j * LANES:(j + 1) * LANES] = jnp.where(lo_half, o[:t_len], o[t_len:])


def _na_context(u_na, q_g, k_g):
    bsz, t_len, _ = u_na.shape
    tok = lambda seg: pl.BlockSpec((1, t_len, NA_WIDTH), lambda b: (b, 0, seg))
    out_blk = pl.BlockSpec((1, t_len, NA_WIDTH), lambda b: (b, 0, 0))
    g2 = lambda g: jnp.tile(g.reshape(1, HEAD_DIM), (1, 2))
    shp = jax.ShapeDtypeStruct((bsz, t_len, NA_WIDTH), F32)
    return pl.pallas_call(
        _na_ctx_kernel,
        grid=(bsz,),
        in_specs=[tok(0), tok(1), tok(2), _const_spec((1, LANES)), _const_spec((1, LANES))],
        out_specs=[out_blk, out_blk, out_blk],
        out_shape=[shp, shp, shp],
        compiler_params=_params(1),
        name="na_context",
    )(u_na, u_na, u_na, g2(q_g), g2(k_g))


NA_ROW_ILP = 8


def _na_lat_kernel(q_ref, k_ref, v_ref, kc_ref, vc_ref, qg_ref, kg_ref, tab_ref, o_ref,
                   q0_s, q1_s, kn_s, v_s, kc_s, vc_s, *, rows, kr):
    ones = _head_ones()
    lo_half = lax.broadcasted_iota(jnp.int32, (1, LANES), 1) < HEAD_DIM
    lo = jnp.where(lo_half, 1.0, 0.0)
    qn = _qk_norm(q_ref[0], qg_ref[...], ones) * QK_SCALE
    q0_s[...] = (qn * lo).astype(BF16)
    q1_s[...] = (qn * (1.0 - lo)).astype(BF16)
    kn_s[...] = _qk_norm(k_ref[0], kg_ref[...], ones).astype(BF16)
    v_s[...] = v_ref[0].astype(BF16)
    kc_s[...] = kc_ref[0].astype(BF16)
    vc_s[...] = vc_ref[0].astype(BF16)
    win = kr * GRID_W

    def body(it, carry):
        qs, k_rows, q_rows, biases = [], [], [], []
        for s in range(NA_ROW_ILP):
            i = it * NA_ROW_ILP + s
            r0 = jnp.clip(i - kr // 2, 0, rows - kr)
            d0 = r0 - i + (NA_ROWS - 1)
            qr = pl.ds(pl.multiple_of(i * GRID_W, GRID_W), GRID_W)
            q_rows.append(qr)
            k_rows.append(pl.ds(pl.multiple_of(r0 * GRID_W, GRID_W), win))
            qs.append(jnp.concatenate([q0_s[qr, :], q1_s[qr, :]], axis=0))
            biases.append(jnp.concatenate(
                [jnp.concatenate([tab_ref[h, d0 + 2 * m] for m in range(kr // 2)], axis=1) for h in range(2)],
                axis=0))
        lw = [_nt(q, kn_s[kr_, :]) + b for q, kr_, b in zip(qs, k_rows, biases)]
        lc = [_nt(q, kc_s[...]) for q in qs]
        ms = [jnp.maximum(jnp.max(a, axis=-1, keepdims=True), jnp.max(c, axis=-1, keepdims=True))
              for a, c in zip(lw, lc)]
        pw = [jnp.exp(a - m) for a, m in zip(lw, ms)]
        pc = [jnp.exp(c - m) for c, m in zip(lc, ms)]
        ls = [jnp.sum(a, axis=-1, keepdims=True) + jnp.sum(c, axis=-1, keepdims=True) for a, c in zip(pw, pc)]
        outs = [(jnp.dot(a.astype(BF16), v_s[kr_, :], preferred_element_type=F32)
                 + jnp.dot(c.astype(BF16), vc_s[...], preferred_element_type=F32)) / l
                for a, c, kr_, l in zip(pw, pc, k_rows, ls)]
        for qr, o in zip(q_rows, outs):
            o_ref[0, qr, :] = jnp.where(lo_half, o[:GRID_W], o[GRID_W:])
        return carry

    lax.fori_loop(0, rows // NA_ROW_ILP, body, 0)


def _latent_bias_table(rpb):
    qc = np.arange(GRID_W)[:, None]
    kc = np.arange(GRID_W)[None, :]
    ws = np.clip(qc - NA_COLS // 2, 0, GRID_W - NA_COLS)
    valid = (kc >= ws) & (kc < ws + NA_COLS)
    dc = np.clip(kc - qc, -(NA_COLS - 1), NA_COLS - 1) + NA_COLS - 1
    onehot = (dc[None] == np.arange(2 * NA_COLS - 1)[:, None, None]).astype(np.float32)
    cb = jnp.einsum("hdc,cqk->hdqk", rpb, jnp.asarray(onehot), precision=lax.Precision.HIGHEST)
    cb = jnp.where(valid[None, None], cb, NEG_INF)
    return jnp.concatenate([cb[:, :-1], cb[:, 1:]], axis=-1)


def _na_latent(u_na, k_ctx, v_ctx, q_g, k_g, rpb):
    bsz, t_len, _ = u_na.shape
    rows = t_len // GRID_W
    kr = min(NA_ROWS, rows)
    assert kr % 2 == 0 and rows % NA_ROW_ILP == 0
    ctx_len = k_ctx.shape[1]
    seg = NA_WIDTH // LANES
    tok = lambda off: pl.BlockSpec((1, t_len, LANES), lambda b, j: (b, 0, off + j))
    ctx = pl.BlockSpec((1, ctx_len, LANES), lambda b, j: (b, 0, j))
    g2 = lambda g: jnp.tile(g.reshape(1, HEAD_DIM), (1, 2))
    tab = _latent_bias_table(rpb)
    tok_s = pltpu.VMEM((t_len, LANES), BF16)
    ctx_s = pltpu.VMEM((ctx_len, LANES), BF16)
    return pl.pallas_call(
        functools.partial(_na_lat_kernel, rows=rows, kr=kr),
        grid=(bsz, seg),
        in_specs=[tok(0), tok(seg), tok(2 * seg), ctx, ctx,
                  _const_spec((1, LANES)), _const_spec((1, LANES)),
                  pl.BlockSpec((2, 2 * NA_ROWS - 2, GRID_W, 2 * GRID_W), lambda b, j: (j, 0, 0, 0))],
        out_specs=pl.BlockSpec((1, t_len, LANES), lambda b, j: (b, 0, j)),
        out_shape=jax.ShapeDtypeStruct((bsz, t_len, NA_WIDTH), F32),
        scratch_shapes=[tok_s, tok_s, tok_s, tok_s, ctx_s, ctx_s],
        compiler_params=_params(2),
        name="na_latent",
    )(u_na, u_na, u_na, k_ctx, v_ctx, g2(q_g), g2(k_g), tab)


def _outproj_kernel(x_ref, orw_ref, ona_ref, gt_ref, mod_ref, g_ref, wor_ref, won_ref, wout_ref, x1_ref, h2_ref):
    g_rw = jax.nn.sigmoid(gt_ref[0, :, :D_MODEL])
    g_na = jax.nn.sigmoid(gt_ref[0, :, D_MODEL:])
    merged = g_rw * _dot(orw_ref[0], wor_ref[...]) + g_na * _dot(ona_ref[0], won_ref[...])
    x1 = x_ref[0] + mod_ref[0, 2:3, :] * _dot(merged, wout_ref[...])
    x1_ref[0] = x1
    h2 = _rms_rows(x1) * g_ref[...]
    h2_ref[0] = (h2 * (1.0 + mod_ref[0, 4:5, :]) + mod_ref[0, 3:4, :]).astype(BF16)


def _out_proj(x, o_rw, o_na, gates, mod_all, mod_off, mod_stride, norm_g, w_or, w_on, w_out):
    bsz, t_len, _ = x.shape
    tm = TOKEN_TILE
    row = lambda b, i: (b, i, 0)
    return pl.pallas_call(
        _outproj_kernel,
        grid=(bsz, t_len // tm),
        in_specs=[pl.BlockSpec((1, tm, D_MODEL), row),
                  pl.BlockSpec((1, tm, RW_WIDTH), row),
                  pl.BlockSpec((1, tm, NA_WIDTH), row),
                  pl.BlockSpec((1, tm, GATE_COLS), row),
                  pl.BlockSpec((1, 6, D_MODEL), lambda b, i: (mod_off + mod_stride * b, 0, 0)),
                  _const_spec((1, D_MODEL)),
                  _const_spec(w_or.shape), _const_spec(w_on.shape), _const_spec(w_out.shape)],
        out_specs=[pl.BlockSpec((1, tm, D_MODEL), row), pl.BlockSpec((1, tm, D_MODEL), row)],
        out_shape=[jax.ShapeDtypeStruct((bsz, t_len, D_MODEL), F32),
                   jax.ShapeDtypeStruct((bsz, t_len, D_MODEL), BF16)],
        compiler_params=_params(2),
        name="out_proj",
    )(x, o_rw, o_na, gates, mod_all, norm_g.reshape(1, -1), w_or, w_on, w_out)


def _ffn_kernel(h2_ref, x1_ref, mod_ref, w1_ref, w3_ref, w2_ref, y_ref):
    h2 = h2_ref[0]
    acc = jnp.zeros((h2.shape[0], D_MODEL), F32)
    for c in range(FF_HIDDEN // FF_CHUNK):
        cols = slice(c * FF_CHUNK, (c + 1) * FF_CHUNK)
        a = jnp.dot(h2, w1_ref[:, cols], preferred_element_type=F32)
        b = jnp.dot(h2, w3_ref[:, cols], preferred_element_type=F32)
        hh = (a * jax.nn.sigmoid(a) * b).astype(BF16)
        acc = acc + jnp.dot(hh, w2_ref[cols, :], preferred_element_type=F32)
    y_ref[0] = x1_ref[0] + mod_ref[0, 5:6, :] * acc


def _ffn(h2, x1, mod_all, mod_off, mod_stride, w1, w3, w2):
    bsz, t_len, _ = x1.shape
    tm = TOKEN_TILE
    row = lambda b, i: (b, i, 0)
    return pl.pallas_call(
        _ffn_kernel,
        grid=(bsz, t_len // tm),
        in_specs=[pl.BlockSpec((1, tm, D_MODEL), row),
                  pl.BlockSpec((1, tm, D_MODEL), row),
                  pl.BlockSpec((1, 6, D_MODEL), lambda b, i: (mod_off + mod_stride * b, 0, 0)),
                  _const_spec(w1.shape), _const_spec(w3.shape), _const_spec(w2.shape)],
        out_specs=pl.BlockSpec((1, tm, D_MODEL), row),
        out_shape=jax.ShapeDtypeStruct((bsz, t_len, D_MODEL), F32),
        compiler_params=_params(2),
        name="ffn",
    )(h2, x1, mod_all, w1, w3, w2)


def _trunk(x, mod_all, mod_off, mod_stride, s0_big, ctx_kv, p, wb):
    shape = x.shape
    flat = (lambda t: t.reshape(1, -1, t.shape[-1])) if mod_stride == 0 else (lambda t: t)
    unflat = lambda t: t.reshape(shape[0], shape[1], t.shape[-1])
    u_rw, u_na, gates = _in_proj(flat(x), mod_all, mod_off, mod_stride, p["norm1_g"], wb["w_in"])
    u_rw, u_na = unflat(u_rw), unflat(u_na)
    o_rw, s_new = _rwkv_branch(u_rw, s0_big, p, RWKV_PAIRS_CTX if ctx_kv is None else RWKV_PAIRS_LAT)
    if ctx_kv is None:
        o_na, k_new, v_new = _na_context(u_na, p["na_q_g"], p["na_k_g"])
    else:
        o_na = _na_latent(u_na, ctx_kv[0], ctx_kv[1], p["na_q_g"], p["na_k_g"], p["na_rpb"])
        k_new = v_new = None
    x1, h2 = _out_proj(flat(x), flat(o_rw), flat(o_na), gates, mod_all, mod_off, mod_stride, p["norm2_g"],
                       wb["w_o_rwkv"], wb["w_o_na"], wb["w_out"])
    y = _ffn(h2, x1, mod_all, mod_off, mod_stride, wb["ffn_w1"], wb["ffn_w3"], wb["ffn_w2"])
    return unflat(y), s_new, k_new, v_new


def kernel(x_prompt, x_sample, state_rwkv, cache_na_k, cache_na_v, c, c_ctx, norm1_g, norm2_g, w_ada, b_ada,
           w_in, shift_mu, rw_w0, rw_w_up, rw_a0, rw_a_up, rw_g_up, rw_k_k, rw_k_a, rw_r_k, rw_ln_g, rw_ln_b,
           na_q_g, na_k_g, na_rpb, w_o_rwkv, w_o_na, w_out, ffn_w1, ffn_w3, ffn_w2):
    depth = w_in.shape[0]
    bsz, seq = x_prompt.shape[:2]
    dec = x_sample.shape[0]
    n_vec = 8
    cvecs = jnp.concatenate([c_ctx[None, :], c, jnp.zeros((n_vec - 1 - dec, D_MODEL), F32)], axis=0)
    y_p, y_s = x_prompt, x_sample
    new_s, new_k, new_v = [], [], []
    for l in range(depth):
        p = dict(norm1_g=norm1_g[l], norm2_g=norm2_g[l], shift_mu=shift_mu[l], rw_w0=rw_w0[l],
                 rw_w_up=rw_w_up[l], rw_a0=rw_a0[l], rw_a_up=rw_a_up[l], rw_g_up=rw_g_up[l],
                 rw_k_k=rw_k_k[l], rw_k_a=rw_k_a[l], rw_r_k=rw_r_k[l], rw_ln_g=rw_ln_g[l],
                 rw_ln_b=rw_ln_b[l], na_q_g=na_q_g[l], na_k_g=na_k_g[l], na_rpb=na_rpb[l])
        wb = dict(w_in=w_in[l].astype(BF16), w_o_rwkv=w_o_rwkv[l].astype(BF16), w_o_na=w_o_na[l].astype(BF16),
                  w_out=w_out[l].astype(BF16), ffn_w1=ffn_w1[l].astype(BF16), ffn_w3=ffn_w3[l].astype(BF16),
                  ffn_w2=ffn_w2[l].astype(BF16))
        mod_all = _modulation(cvecs, w_ada[l], b_ada[l])[:1 + dec].reshape(1 + dec, 6, D_MODEL)
        y_p, s_big, k_l, v_l = _trunk(y_p, mod_all, 0, 0, None, None, p, wb)
        new_s.append(_state_from_big(s_big))
        new_k.append(k_l.reshape(bsz, seq, NA_HEADS, HEAD_DIM))
        new_v.append(v_l.reshape(bsz, seq, NA_HEADS, HEAD_DIM))
        ctx_k = cache_na_k[:, l].reshape(dec, -1, NA_WIDTH)
        ctx_v = cache_na_v[:, l].reshape(dec, -1, NA_WIDTH)
        y_s, _, _, _ = _trunk(y_s, mod_all, 1, 1, _state_to_big(state_rwkv[:, l]), (ctx_k, ctx_v), p, wb)
    return (y_p, y_s, jnp.stack(new_s, axis=1), jnp.stack(new_k, axis=1), jnp.stack(new_v, axis=1))
```

```python
import functools

import numpy as np
import jax
import jax.numpy as jnp
from jax import lax
from jax.experimental import pallas as pl
from jax.experimental.pallas import tpu as pltpu

D_MODEL = 1024
GRID_W = 64
HEAD_DIM = 64
RW_HEADS = 8
RW_WIDTH = RW_HEADS * HEAD_DIM
NA_HEADS = 8
NA_WIDTH = NA_HEADS * HEAD_DIM
LORA_DECAY = 64
LORA_ICLR = 64
LORA_GATE = 128
NA_ROWS = 8
NA_COLS = 16
FF_HIDDEN = 2816
RW_COLS = 3 * RW_WIDTH + 2 * LORA_DECAY + 2 * LORA_ICLR + LORA_GATE
NA_IN_COLS = 3 * NA_WIDTH
GATE_COLS = 2 * D_MODEL
RMS_EPS = 1e-6
GN_EPS = 64e-5
L2_EPS = 1e-12
NEG_INF = -1e30
DECAY_SCALE = float(np.exp(-0.5))
QK_SCALE = HEAD_DIM ** -0.5
assert QK_SCALE == 0.125

LANES = 128
PAIRS = RW_HEADS // 2
CHUNK = 64
STACK = 2 * CHUNK
RWKV_PAIRS_CTX = 4
RWKV_PAIRS_LAT = 2
UNITS_PER_STEP = 16
TOKEN_TILE = 512
FF_CHUNK = 256
VMEM_LIMIT = 56 * 1024 * 1024

F32 = jnp.float32
BF16 = jnp.bfloat16


def _dot(a, b):
    return jnp.dot(a.astype(BF16), b.astype(BF16), preferred_element_type=F32)


def _dot_nt(a, b):
    return lax.dot_general(a.astype(BF16), b.astype(BF16), (((1,), (1,)), ((), ())),
                           preferred_element_type=F32)


def _split2(x):
    hi = x.astype(BF16)
    lo = (x - hi.astype(F32)).astype(BF16)
    return hi, lo


def _dot_exact_lhs(a_exact, b):
    h, l = _split2(b)
    d = lambda x: jnp.dot(a_exact, x, preferred_element_type=F32)
    return d(h) + d(l)


def _dot3(a, b):
    ah, al = _split2(a)
    bh, bl = _split2(b)
    d = lambda x, y: jnp.dot(x, y, preferred_element_type=F32)
    return d(ah, bh) + d(al, bh) + d(ah, bl)


def _head_ones():
    r = lax.broadcasted_iota(jnp.int32, (LANES, LANES), 0) // HEAD_DIM
    c = lax.broadcasted_iota(jnp.int32, (LANES, LANES), 1) // HEAD_DIM
    return jnp.where(r == c, 1.0, 0.0).astype(BF16)


def _head_sum(x, ones):
    return jnp.dot(x.astype(BF16), ones, preferred_element_type=F32)


def _sigmoid(x):
    return 0.5 * jnp.tanh(0.5 * x) + 0.5


def _rms_rows(x):
    return x * lax.rsqrt(jnp.mean(x * x, axis=-1, keepdims=True) + RMS_EPS)


def _const_spec(shape):
    nd = len(shape)
    return pl.BlockSpec(shape, lambda *_: (0,) * nd, pipeline_mode=pl.Buffered(1))


def _params(n_axes):
    return pltpu.CompilerParams(dimension_semantics=("arbitrary",) * n_axes,
                                vmem_limit_bytes=VMEM_LIMIT)


def _mod_kernel(c_ref, w_ref, b_ref, o_ref):
    s = c_ref[...]
    s = s * _sigmoid(s)
    o_ref[...] = _dot3(s, w_ref[...]) + b_ref[...]


def _modulation(cvecs, w_ada, b_ada):
    n = cvecs.shape[0]
    tn = 1536
    return pl.pallas_call(
        _mod_kernel,
        grid=(6 * D_MODEL // tn,),
        in_specs=[pl.BlockSpec((n, D_MODEL), lambda j: (0, 0)),
                  pl.BlockSpec((D_MODEL, tn), lambda j: (0, j)),
                  pl.BlockSpec((1, tn), lambda j: (0, j))],
        out_specs=pl.BlockSpec((n, tn), lambda j: (0, j)),
        out_shape=jax.ShapeDtypeStruct((n, 6 * D_MODEL), F32),
        compiler_params=_params(1),
        name="modulation",
    )(cvecs, w_ada, b_ada.reshape(1, -1))


def _inproj_kernel(x_ref, mod_ref, g_ref, w_ref, urw_ref, una_ref, gt_ref):
    x = x_ref[0]
    h = _rms_rows(x) * g_ref[...]
    h = (h * (1.0 + mod_ref[0, 1:2, :]) + mod_ref[0, 0:1, :]).astype(BF16)
    d = lambda lo, hi: jnp.dot(h, w_ref[:, lo:hi], preferred_element_type=F32)
    urw_ref[0] = d(0, RW_COLS)
    una_ref[0] = d(RW_COLS, RW_COLS + NA_IN_COLS)
    gt_ref[0] = d(RW_COLS + NA_IN_COLS, RW_COLS + NA_IN_COLS + GATE_COLS).astype(BF16)


def _in_proj(x, mod_all, mod_off, mod_stride, norm_g, w_in_bf):
    bsz, t_len, _ = x.shape
    tm = TOKEN_TILE
    row = lambda b, i: (b, i, 0)
    return pl.pallas_call(
        _inproj_kernel,
        grid=(bsz, t_len // tm),
        in_specs=[pl.BlockSpec((1, tm, D_MODEL), row),
                  pl.BlockSpec((1, 6, D_MODEL), lambda b, i: (mod_off + mod_stride * b, 0, 0)),
                  _const_spec((1, D_MODEL)),
                  _const_spec(w_in_bf.shape)],
        out_specs=[pl.BlockSpec((1, tm, RW_COLS), row),
                   pl.BlockSpec((1, tm, NA_IN_COLS), row),
                   pl.BlockSpec((1, tm, GATE_COLS), row)],
        out_shape=[jax.ShapeDtypeStruct((bsz, t_len, RW_COLS), F32),
                   jax.ShapeDtypeStruct((bsz, t_len, NA_IN_COLS), F32),
                   jax.ShapeDtypeStruct((bsz, t_len, GATE_COLS), BF16)],
        compiler_params=_params(2),
        name="in_proj",
    )(x, mod_all, norm_g.reshape(1, -1), w_in_bf)


def _shift(x, mu):
    t_len = x.shape[0]
    row = lax.broadcasted_iota(jnp.int32, x.shape, 0)
    prev = jnp.where(row == 0, 0.0, pltpu.roll(x, 1, 0))
    nxt = jnp.where(row == t_len - 1, 0.0, pltpu.roll(x, t_len - 1, 0))
    return x + mu[0:1, :] * (prev - x) + mu[1:2, :] * (nxt - x)


def _stack_heads(x, lane_lo):
    return jnp.concatenate([x * lane_lo, x * (1.0 - lane_lo)], axis=0)


def _wkv_intra(units, consts):
    tri, mask_s, mask_i, eye, lane_lo, blk = consts
    eye_f = jnp.where(eye, 1.0, 0.0)
    stack = lambda z: _stack_heads(z, lane_lo)

    cums = [_dot_exact_lhs(tri[int(u[6])], u[1]) for u in units]

    prep = []
    for (r, lw, kd, v, kk, b, reverse), cum in zip(units, cums):
        mid_row = CHUNK // 2 if reverse else CHUNK // 2 - 1
        tot_row = 0 if reverse else CHUNK - 1
        a = -kk
        ex = cum - lw
        mid = cum[mid_row:mid_row + 1, :]
        tot = cum[tot_row:tot_row + 1, :]
        up = jnp.exp(cum - mid)
        dn = jnp.exp(mid - cum)
        tail = jnp.exp(tot - cum)
        prep.append(dict(
            at_m=stack(a * jnp.exp(ex - mid)).astype(BF16),
            rt_m=stack(r * up).astype(BF16),
            btkt=jnp.concatenate([stack(b * dn), stack(kd * dn)], axis=0).astype(BF16),
            a_e=stack(a * jnp.exp(ex)),
            r_e=stack(r * jnp.exp(cum)),
            bk_t=jnp.concatenate([stack(b * tail), stack(kd * tail)], axis=0).T.astype(BF16),
            vv=stack(v).astype(BF16),
            diag=jnp.where(eye, jnp.exp(tot), 0.0),
            rev=int(reverse)))

    ntd = lambda x, y: lax.dot_general(x, y, (((1,), (1,)), ((), ())), preferred_element_type=F32)
    mm = lambda x, y: jnp.dot(x, y, preferred_element_type=F32)
    top = [mask_s[p["rev"]] * ntd(p["at_m"], p["btkt"]) for p in prep]
    bot = [(mask_i[p["rev"]] * ntd(p["rt_m"], p["btkt"])).astype(BF16) for p in prep]

    diag_blk, swap_eye = blk
    off_blk = 1.0 - diag_blk
    both = [t[:, :LANES] + swap_eye for t in top]
    steps = CHUNK.bit_length() - 1
    diag_bf = diag_blk.astype(BF16)
    for j in range(steps):
        packed = [q.astype(BF16) for q in both]
        res = [mm(qb * diag_bf, qb) for qb in packed]
        both = [r + off_blk * q for r, q in zip(res, both)]
    ms = [pltpu.roll(q * off_blk, HEAD_DIM, 1) - eye_f for q in both]
    x0 =[jnp.concatenate([p["a_e"], mm(t[:, LANES:].astype(BF16), p["vv"])], axis=1) for p, t in zip(prep, top)]
    xs = [x + mm(m.astype(BF16), x.astype(BF16)) for x, m in zip(x0, ms)]

    out = []
    zeros = jnp.zeros((STACK, LANES), BF16)
    for p, x, bt in zip(prep, xs, bot):
        rhs = jnp.concatenate([x.astype(BF16), jnp.concatenate([zeros, p["vv"]], axis=1)], axis=0)
        lhs = jnp.concatenate([bt, p["bk_t"]], axis=0)
        res = mm(lhs, rhs)
        lhs2 = res[:, :LANES] + jnp.concatenate([p["r_e"], p["diag"]], axis=0)
        out.append((lhs2.astype(BF16), res[:, LANES:]))
    return out


def _wkv_constants():
    lane_head = np.arange(LANES) // HEAD_DIM
    ones = (lane_head[:, None] == lane_head[None, :]).astype(np.float32)
    t = np.arange(CHUNK)
    tri = np.stack([t[None, :] <= t[:, None], t[None, :] >= t[:, None]]).astype(np.float32)
    s = np.arange(STACK)
    same = (s[:, None] // CHUNK) == (s[None, :] // CHUNK)
    rs, cs = s[:, None], s[None, :]
    masks = np.stack([same & (cs < rs), same & (cs > rs), same & (cs <= rs), same & (cs >= rs)]).astype(np.float32)
    masks = np.concatenate([masks, masks], axis=-1)
    return jnp.asarray(ones, BF16), jnp.asarray(tri, BF16), jnp.asarray(masks, F32)


def _rwkv_kernel(*refs, t_len, has_s0, pairs):
    (r_ref, k_ref, v_ref, lo_ref, mur_ref, muk_ref, muv_ref, mul_ref, w0_ref, a0_ref, wup_ref, aup_ref,
     gup_ref, kk_ref, ka_ref, rk_ref, lng_ref, lnb_ref, ones_ref, tri_ref, mask_ref) = refs[:21]
    pos = 21
    s0_ref = None
    if has_s0:
        s0_ref = refs[pos]
        pos += 1
    o_ref, sn_ref = refs[pos], refs[pos + 1]
    (r_s, v_s, kk_s, b0_s, b1_s, lw0_s, lw1_s, kd0_s, kd1_s, gate_s, bonus_s, yf_s, yb_s,
     lhs_s, add_s, st_s) = refs[pos + 2:]

    ones = ones_ref[...]
    lane = lax.broadcasted_iota(jnp.int32, (1, LANES), 1)
    lo_half = lane < HEAD_DIM
    lane_lo = jnp.where(lo_half, 1.0, 0.0)
    mm = lambda x, y: jnp.dot(x, y, preferred_element_type=F32)

    lo = _shift(lo_ref[0], mul_ref[...])
    wd = jnp.tanh(lo[:, 0:LANES])
    ad = lo[:, LANES:2 * LANES]
    sig_gd = _sigmoid(lo[:, 2 * LANES:3 * LANES]).astype(BF16)
    wd_split = [_split2(wd * m) for m in (lane_lo, 1.0 - lane_lo)]
    ad_bf = [(ad * m).astype(BF16) for m in (lane_lo, 1.0 - lane_lo)]
    for j in range(pairs):
        cols = slice(j * LANES, (j + 1) * LANES)
        r = _shift(r_ref[0, :, cols], mur_ref[:, cols])
        k = _shift(k_ref[0, :, cols], muk_ref[:, cols])
        v = _shift(v_ref[0, :, cols], muv_ref[:, cols])
        kk = k * kk_ref[:, cols]
        kk = kk * lax.rsqrt(_head_sum(kk * kk, ones) + L2_EPS)
        wup_h, wup_l = _split2(wup_ref[:, cols])
        aup = aup_ref[:, cols].astype(BF16)
        kdirs = []
        for e, (lw_s, kd_s, b_s) in enumerate(((lw0_s, kd0_s, b0_s), (lw1_s, kd1_s, b1_s))):
            wd_h, wd_l = wd_split[e]
            w_lin = w0_ref[e:e + 1, cols] + (mm(wd_h, wup_h) + mm(wd_l, wup_h) + mm(wd_h, wup_l))
            lw_s[j] = -DECAY_SCALE * _sigmoid(w_lin)
            iclr = _sigmoid(a0_ref[e:e + 1, cols] + mm(ad_bf[e], aup))
            kd = k * (1.0 + (iclr - 1.0) * ka_ref[:, cols])
            kd_s[j] = kd
            b_s[j] = kk * iclr
            kdirs.append(kd)
        gate_s[:, cols] = mm(sig_gd, gup_ref[:, cols].astype(BF16))
        bonus_s[:, cols] = _head_sum(r * (0.5 * (kdirs[0] + kdirs[1])) * rk_ref[:, cols], ones) * v
        r_s[j] = r
        v_s[j] = v
        kk_s[j] = kk

    n_chunks = t_len // CHUNK
    chunks_per = min(n_chunks, UNITS_PER_STEP // 2)
    pairs_per = min(pairs, UNITS_PER_STEP // (2 * chunks_per))
    groups = n_chunks // chunks_per
    rs = lax.broadcasted_iota(jnp.int32, (STACK, STACK), 0)
    cs = lax.broadcasted_iota(jnp.int32, (STACK, STACK), 1)
    as_f32 = lambda m: jnp.where(m, 1.0, 0.0)
    blk = (as_f32(rs // CHUNK == cs // CHUNK), as_f32(cs == (rs + CHUNK) % STACK))
    consts = ((tri_ref[0], tri_ref[1]), (mask_ref[0], mask_ref[1]), (mask_ref[2], mask_ref[3]),
              rs == cs, lane_lo, blk)
    dirs = ((lw0_s, kd0_s, b0_s), (lw1_s, kd1_s, b1_s))

    def intra_body(it, carry):
        pg = it // groups
        g = it % groups
        units, ids = [], []
        for jj in range(pairs_per):
            j = pg * pairs_per + jj
            for cc in range(chunks_per):
                c = g * chunks_per + cc
                rows = pl.ds(pl.multiple_of(c * CHUNK, CHUNK), CHUNK)
                for e, (lw_s, kd_s, b_s) in enumerate(dirs):
                    units.append((r_s[j, rows, :], lw_s[j, rows, :], kd_s[j, rows, :], v_s[j, rows, :],
                                  kk_s[j, rows, :], b_s[j, rows, :], e == 1))
                    ids.append((j * 2 + e) * n_chunks + c)
        for uid, (lhs, add) in zip(ids, _wkv_intra(units, consts)):
            lhs_s[uid] = lhs
            add_s[uid] = add
        return carry

    lax.fori_loop(0, (pairs // pairs_per) * groups, intra_body, 0)

    for j in range(pairs):
        for e in range(2):
            if has_s0:
                st_s[2 * j + e] = s0_ref[0, e, j].T
            else:
                st_s[2 * j + e] = jnp.zeros((LANES, LANES), F32)

    def state_body(it, carry):
        chunk = (it, n_chunks - 1 - it)
        uids = [(j * 2 + e) * n_chunks + chunk[e] for j in range(pairs) for e in range(2)]
        sts = [st_s[ch].astype(BF16) for ch in range(2 * pairs)]
        res = [mm(lhs_s[uid], st) + add_s[uid] for uid, st in zip(uids, sts)]
        for ch, rr in enumerate(res):
            j, e = divmod(ch, 2)
            y_s = yb_s if e else yf_s
            y_s[j, pl.ds(pl.multiple_of(chunk[e] * CHUNK, CHUNK), CHUNK), :] = rr[:CHUNK] + rr[CHUNK:STACK]
            st_s[ch] = rr[STACK:]
        return carry

    lax.fori_loop(0, n_chunks, state_body, 0)
    for j in range(pairs):
        for e in range(2):
            sn_ref[0, e, j] = st_s[2 * j + e].T

    inv_d = 1.0 / HEAD_DIM
    for j in range(pairs):
        cols = slice(j * LANES, (j + 1) * LANES)
        y = yf_s[j] + yb_s[j]
        mean = _head_sum(y, ones) * inv_d
        dlt = y - mean
        var = _head_sum(dlt * dlt, ones) * inv_d
        yn = dlt * lax.rsqrt(var + GN_EPS) * lng_ref[:, cols] + lnb_ref[:, cols]
        o_ref[0, :, cols] = (yn + bonus_s[:, cols]) * gate_s[:, cols]


def _rwkv_branch(u_rw, s0_big, p, pairs):
    bsz, t_len, _ = u_rw.shape
    has_s0 = s0_big is not None
    width = pairs * LANES
    seg = RW_WIDTH // width
    tok = lambda off: pl.BlockSpec((1, t_len, width), lambda b, j: (b, 0, off + j))
    mu = lambda off: pl.BlockSpec((2, width), lambda b, j: (0, off + j))
    vec2 = pl.BlockSpec((2, width), lambda b, j: (0, j))
    vec1 = pl.BlockSpec((1, width), lambda b, j: (0, j))
    mat = pl.BlockSpec((LANES, width), lambda b, j: (0, j))
    lora_w = 3 * LANES
    lora_blk = 3 * RW_WIDTH // lora_w
    in_specs = [tok(0), tok(seg), tok(2 * seg),
                pl.BlockSpec((1, t_len, lora_w), lambda b, j: (b, 0, lora_blk)),
                mu(0), mu(seg), mu(2 * seg),
                pl.BlockSpec((2, lora_w), lambda b, j: (0, lora_blk)),
                vec2, vec2, mat, mat, mat, vec1, vec1, vec1, vec1, vec1,
                _const_spec((LANES, LANES)), _const_spec((2, CHUNK, CHUNK)), _const_spec((4, STACK, 2 * STACK))]
    args = [u_rw, u_rw, u_rw, u_rw, p["shift_mu"], p["shift_mu"], p["shift_mu"], p["shift_mu"],
            p["rw_w0"], p["rw_a0"],
            p["rw_w_up"].reshape(2 * LORA_DECAY, RW_WIDTH), p["rw_a_up"].reshape(2 * LORA_ICLR, RW_WIDTH),
            p["rw_g_up"], p["rw_k_k"].reshape(1, -1), p["rw_k_a"].reshape(1, -1),
            p["rw_r_k"].reshape(1, -1), p["rw_ln_g"].reshape(1, -1), p["rw_ln_b"].reshape(1, -1),
            *_wkv_constants()]
    st_spec = pl.BlockSpec((1, 2, pairs, LANES, LANES), lambda b, j: (b, 0, j, 0, 0))
    if has_s0:
        in_specs.append(st_spec)
        args.append(s0_big)
    n_units = 2 * pairs * (t_len // CHUNK)
    per_pair = pltpu.VMEM((pairs, t_len, LANES), F32)
    full = pltpu.VMEM((t_len, width), F32)
    scratch = [per_pair] * 9 + [full, full, per_pair, per_pair,
                                pltpu.VMEM((n_units, 2 * STACK, LANES), BF16),
                                pltpu.VMEM((n_units, 2 * STACK, LANES), F32),
                                pltpu.VMEM((2 * pairs, LANES, LANES), F32)]
    o_rw, s_new = pl.pallas_call(
        functools.partial(_rwkv_kernel, t_len=t_len, has_s0=has_s0, pairs=pairs),
        grid=(bsz, PAIRS // pairs),
        in_specs=in_specs,
        out_specs=[pl.BlockSpec((1, t_len, width), lambda b, j: (b, 0, j)), st_spec],
        out_shape=[jax.ShapeDtypeStruct((bsz, t_len, RW_WIDTH), F32),
                   jax.ShapeDtypeStruct((bsz, 2, PAIRS, LANES, LANES), F32)],
        scratch_shapes=scratch,
        compiler_params=_params(2),
        name="rwkv_branch",
    )(*args)
    return o_rw, s_new


def _state_to_big(s0):
    bsz = s0.shape[0]
    x = s0.reshape(bsz, 2, PAIRS, 2, HEAD_DIM, HEAD_DIM)
    z = jnp.zeros_like(x[:, :, :, 0])
    top = jnp.concatenate([x[:, :, :, 0], z], axis=-1)
    bot = jnp.concatenate([z, x[:, :, :, 1]], axis=-1)
    return jnp.concatenate([top, bot], axis=-2)


def _state_from_big(s_big):
    bsz = s_big.shape[0]
    h0 = s_big[:, :, :, :HEAD_DIM, :HEAD_DIM]
    h1 = s_big[:, :, :, HEAD_DIM:, HEAD_DIM:]
    return jnp.stack([h0, h1], axis=3).reshape(bsz, 2, RW_HEADS, HEAD_DIM, HEAD_DIM)


def _qk_norm(t, g, ones):
    ms = _head_sum(t * t, ones) * (1.0 / HEAD_DIM)
    return t * lax.rsqrt(ms + RMS_EPS) * g


def _nt(x, y):
    return lax.dot_general(x, y, (((1,), (1,)), ((), ())), preferred_element_type=F32)


def _na_ctx_kernel(q_ref, k_ref, v_ref, qg_ref, kg_ref, o_ref, kn_ref, vc_ref):
    ones = _head_ones()
    lo_half = lax.broadcasted_iota(jnp.int32, (1, LANES), 1) < HEAD_DIM
    lo = jnp.where(lo_half, 1.0, 0.0)
    t_len = q_ref.shape[1]
    qs, ks, vs = [], [], []
    for j in range(NA_WIDTH // LANES):
        cols = slice(j * LANES, (j + 1) * LANES)
        qn = _qk_norm(q_ref[0, :, cols], qg_ref[...], ones)
        kn = _qk_norm(k_ref[0, :, cols], kg_ref[...], ones)
        v = v_ref[0, :, cols]
        kn_ref[0, :, cols] = kn
        vc_ref[0, :, cols] = v
        qn = qn * QK_SCALE
        qs.append(jnp.concatenate([qn * lo, qn * (1.0 - lo)], axis=0).astype(BF16))
        ks.append(kn.astype(BF16))
        vs.append(v.astype(BF16))
    logits = [_nt(q, k) for q, k in zip(qs, ks)]
    ms = [jnp.max(s, axis=-1, keepdims=True) for s in logits]
    ps = [jnp.exp(s - m) for s, m in zip(logits, ms)]
    ls = [jnp.sum(p, axis=-1, keepdims=True) for p in ps]
    outs = [jnp.dot(p.astype(BF16), v, preferred_element_type=F32) / l for p, v, l in zip(ps, vs, ls)]
    for j, o in enumerate(outs):
        o_ref[0, :, j * LANES:(j + 1) * LANES] = jnp.where(lo_half, o[:t_len], o[t_len:])


def _na_context(u_na, q_g, k_g):
    bsz, t_len, _ = u_na.shape
    tok = lambda seg: pl.BlockSpec((1, t_len, NA_WIDTH), lambda b: (b, 0, seg))
    out_blk = pl.BlockSpec((1, t_len, NA_WIDTH), lambda b: (b, 0, 0))
    g2 = lambda g: jnp.tile(g.reshape(1, HEAD_DIM), (1, 2))
    shp = jax.ShapeDtypeStruct((bsz, t_len, NA_WIDTH), F32)
    return pl.pallas_call(
        _na_ctx_kernel,
        grid=(bsz,),
        in_specs=[tok(0), tok(1), tok(2), _const_spec((1, LANES)), _const_spec((1, LANES))],
        out_specs=[out_blk, out_blk, out_blk],
        out_shape=[shp, shp, shp],
        compiler_params=_params(1),
        name="na_context",
    )(u_na, u_na, u_na, g2(q_g), g2(k_g))


NA_ROW_ILP = 8


def _na_lat_kernel(q_ref, k_ref, v_ref, kc_ref, vc_ref, qg_ref, kg_ref, tab_ref, o_ref,
                   q0_s, q1_s, kn_s, v_s, kc_s, vc_s, *, rows, kr):
    ones = _head_ones()
    lo_half = lax.broadcasted_iota(jnp.int32, (1, LANES), 1) < HEAD_DIM
    lo = jnp.where(lo_half, 1.0, 0.0)
    qn = _qk_norm(q_ref[0], qg_ref[...], ones) * QK_SCALE
    q0_s[...] = (qn * lo).astype(BF16)
    q1_s[...] = (qn * (1.0 - lo)).astype(BF16)
    kn_s[...] = _qk_norm(k_ref[0], kg_ref[...], ones).astype(BF16)
    v_s[...] = v_ref[0].astype(BF16)
    kc_s[...] = kc_ref[0].astype(BF16)
    vc_s[...] = vc_ref[0].astype(BF16)
    win = kr * GRID_W

    def body(it, carry):
        qs, k_rows, q_rows, biases = [], [], [], []
        for s in range(NA_ROW_ILP):
            i = it * NA_ROW_ILP + s
            r0 = jnp.clip(i - kr // 2, 0, rows - kr)
            d0 = r0 - i + (NA_ROWS - 1)
            qr = pl.ds(pl.multiple_of(i * GRID_W, GRID_W), GRID_W)
            q_rows.append(qr)
            k_rows.append(pl.ds(pl.multiple_of(r0 * GRID_W, GRID_W), win))
            qs.append(jnp.concatenate([q0_s[qr, :], q1_s[qr, :]], axis=0))
            biases.append(jnp.concatenate(
                [jnp.concatenate([tab_ref[h, d0 + 2 * m] for m in range(kr // 2)], axis=1) for h in range(2)],
                axis=0))
        lw = [_nt(q, kn_s[kr_, :]) + b for q, kr_, b in zip(qs, k_rows, biases)]
        lc = [_nt(q, kc_s[...]) for q in qs]
        ms = [jnp.maximum(jnp.max(a, axis=-1, keepdims=True), jnp.max(c, axis=-1, keepdims=True))
              for a, c in zip(lw, lc)]
        pw = [jnp.exp(a - m) for a, m in zip(lw, ms)]
        pc = [jnp.exp(c - m) for c, m in zip(lc, ms)]
        ls = [jnp.sum(a, axis=-1, keepdims=True) + jnp.sum(c, axis=-1, keepdims=True) for a, c in zip(pw, pc)]
        outs = [(jnp.dot(a.astype(BF16), v_s[kr_, :], preferred_element_type=F32)
                 + jnp.dot(c.astype(BF16), vc_s[...], preferred_element_type=F32)) / l
                for a, c, kr_, l in zip(pw, pc, k_rows, ls)]
        for qr, o in zip(q_rows, outs):
            o_ref[0, qr, :] = jnp.where(lo_half, o[:GRID_W], o[GRID_W:])
        return carry

    lax.fori_loop(0, rows // NA_ROW_ILP, body, 0)


def _latent_bias_table(rpb):
    qc = np.arange(GRID_W)[:, None]
    kc = np.arange(GRID_W)[None, :]
    ws = np.clip(qc - NA_COLS // 2, 0, GRID_W - NA_COLS)
    valid = (kc >= ws) & (kc < ws + NA_COLS)
    dc = np.clip(kc - qc, -(NA_COLS - 1), NA_COLS - 1) + NA_COLS - 1
    onehot = (dc[None] == np.arange(2 * NA_COLS - 1)[:, None, None]).astype(np.float32)
    cb = jnp.einsum("hdc,cqk->hdqk", rpb, jnp.asarray(onehot), precision=lax.Precision.HIGHEST)
    cb = jnp.where(valid[None, None], cb, NEG_INF)
    return jnp.concatenate([cb[:, :-1], cb[:, 1:]], axis=-1)


def _na_latent(u_na, k_ctx, v_ctx, q_g, k_g, rpb):
    bsz, t_len, _ = u_na.shape
    rows = t_len // GRID_W
    kr = min(NA_ROWS, rows)
    assert kr % 2 == 0 and rows % NA_ROW_ILP == 0
    ctx_len = k_ctx.shape[1]
    seg = NA_WIDTH // LANES
    tok = lambda off: pl.BlockSpec((1, t_len, LANES), lambda b, j: (b, 0, off + j))
    ctx = pl.BlockSpec((1, ctx_len, LANES), lambda b, j: (b, 0, j))
    g2 = lambda g: jnp.tile(g.reshape(1, HEAD_DIM), (1, 2))
    tab = _latent_bias_table(rpb)
    tok_s = pltpu.VMEM((t_len, LANES), BF16)
    ctx_s = pltpu.VMEM((ctx_len, LANES), BF16)
    return pl.pallas_call(
        functools.partial(_na_lat_kernel, rows=rows, kr=kr),
        grid=(bsz, seg),
        in_specs=[tok(0), tok(seg), tok(2 * seg), ctx, ctx,
                  _const_spec((1, LANES)), _const_spec((1, LANES)),
                  pl.BlockSpec((2, 2 * NA_ROWS - 2, GRID_W, 2 * GRID_W), lambda b, j: (j, 0, 0, 0))],
        out_specs=pl.BlockSpec((1, t_len, LANES), lambda b, j: (b, 0, j)),
        out_shape=jax.ShapeDtypeStruct((bsz, t_len, NA_WIDTH), F32),
        scratch_shapes=[tok_s, tok_s, tok_s, tok_s, ctx_s, ctx_s],
        compiler_params=_params(2),
        name="na_latent",
    )(u_na, u_na, u_na, k_ctx, v_ctx, g2(q_g), g2(k_g), tab)


def _out_ffn_kernel(x_ref, orw_ref, ona_ref, gt_ref, mod_ref, g_ref, wor_ref, won_ref, wout_ref,
                    w1_ref, w3_ref, w2_ref, y_ref):
    g_rw = _sigmoid(gt_ref[0, :, :D_MODEL].astype(F32))
    g_na = _sigmoid(gt_ref[0, :, D_MODEL:].astype(F32))
    merged = g_rw * _dot(orw_ref[0], wor_ref[...]) + g_na * _dot(ona_ref[0], won_ref[...])
    x1 = x_ref[0] + mod_ref[0, 2:3, :] * _dot(merged, wout_ref[...])
    h2 = _rms_rows(x1) * g_ref[...]
    h2 = (h2 * (1.0 + mod_ref[0, 4:5, :]) + mod_ref[0, 3:4, :]).astype(BF16)
    acc = jnp.zeros(x1.shape, F32)
    for c in range(FF_HIDDEN // FF_CHUNK):
        cols = slice(c * FF_CHUNK, (c + 1) * FF_CHUNK)
        a = jnp.dot(h2, w1_ref[:, cols], preferred_element_type=F32)
        b = jnp.dot(h2, w3_ref[:, cols], preferred_element_type=F32)
        hh = (a * _sigmoid(a) * b).astype(BF16)
        acc = acc + jnp.dot(hh, w2_ref[cols, :], preferred_element_type=F32)
    y_ref[0] = x1 + mod_ref[0, 5:6, :] * acc


def _out_ffn(x, o_rw, o_na, gates, mod_all, mod_off, mod_stride, norm_g, wb):
    bsz, t_len, _ = x.shape
    tm = TOKEN_TILE
    row = lambda b, i: (b, i, 0)
    weights = [wb[n] for n in ("w_o_rwkv", "w_o_na", "w_out", "ffn_w1", "ffn_w3", "ffn_w2")]
    return pl.pallas_call(
        _out_ffn_kernel,
        grid=(bsz, t_len // tm),
        in_specs=[pl.BlockSpec((1, tm, D_MODEL), row),
                  pl.BlockSpec((1, tm, RW_WIDTH), row),
                  pl.BlockSpec((1, tm, NA_WIDTH), row),
                  pl.BlockSpec((1, tm, GATE_COLS), row),
                  pl.BlockSpec((1, 6, D_MODEL), lambda b, i: (mod_off + mod_stride * b, 0, 0)),
                  _const_spec((1, D_MODEL))] + [_const_spec(w.shape) for w in weights],
        out_specs=pl.BlockSpec((1, tm, D_MODEL), row),
        out_shape=jax.ShapeDtypeStruct((bsz, t_len, D_MODEL), F32),
        compiler_params=_params(2),
        name="out_ffn",
    )(x, o_rw, o_na, gates, mod_all, norm_g.reshape(1, -1), *weights)


def _trunk(x, mod_all, mod_off, mod_stride, s0_big, ctx_kv, p, wb):
    shape = x.shape
    flat = (lambda t: t.reshape(1, -1, t.shape[-1])) if mod_stride == 0 else (lambda t: t)
    unflat = lambda t: t.reshape(shape[0], shape[1], t.shape[-1])
    u_rw, u_na, gates = _in_proj(flat(x), mod_all, mod_off, mod_stride, p["norm1_g"], wb["w_in"])
    u_rw, u_na = unflat(u_rw), unflat(u_na)
    o_rw, s_new = _rwkv_branch(u_rw, s0_big, p, RWKV_PAIRS_CTX if ctx_kv is None else RWKV_PAIRS_LAT)
    if ctx_kv is None:
        o_na, k_new, v_new = _na_context(u_na, p["na_q_g"], p["na_k_g"])
    else:
        o_na = _na_latent(u_na, ctx_kv[0], ctx_kv[1], p["na_q_g"], p["na_k_g"], p["na_rpb"])
        k_new = v_new = None
    y = _out_ffn(flat(x), flat(o_rw), flat(o_na), gates, mod_all, mod_off, mod_stride, p["norm2_g"], wb)
    return unflat(y), s_new, k_new, v_new


def kernel(x_prompt, x_sample, state_rwkv, cache_na_k, cache_na_v, c, c_ctx, norm1_g, norm2_g, w_ada, b_ada,
           w_in, shift_mu, rw_w0, rw_w_up, rw_a0, rw_a_up, rw_g_up, rw_k_k, rw_k_a, rw_r_k, rw_ln_g, rw_ln_b,
           na_q_g, na_k_g, na_rpb, w_o_rwkv, w_o_na, w_out, ffn_w1, ffn_w3, ffn_w2):
    depth = w_in.shape[0]
    bsz, seq = x_prompt.shape[:2]
    dec = x_sample.shape[0]
    n_vec = 8
    cvecs = jnp.concatenate([c_ctx[None, :], c, jnp.zeros((n_vec - 1 - dec, D_MODEL), F32)], axis=0)
    y_p, y_s = x_prompt, x_sample
    new_s, new_k, new_v = [], [], []
    for l in range(depth):
        p = dict(norm1_g=norm1_g[l], norm2_g=norm2_g[l], shift_mu=shift_mu[l], rw_w0=rw_w0[l],
                 rw_w_up=rw_w_up[l], rw_a0=rw_a0[l], rw_a_up=rw_a_up[l], rw_g_up=rw_g_up[l],
                 rw_k_k=rw_k_k[l], rw_k_a=rw_k_a[l], rw_r_k=rw_r_k[l], rw_ln_g=rw_ln_g[l],
                 rw_ln_b=rw_ln_b[l], na_q_g=na_q_g[l], na_k_g=na_k_g[l], na_rpb=na_rpb[l])
        wb = dict(w_in=w_in[l].astype(BF16), w_o_rwkv=w_o_rwkv[l].astype(BF16), w_o_na=w_o_na[l].astype(BF16),
                  w_out=w_out[l].astype(BF16), ffn_w1=ffn_w1[l].astype(BF16), ffn_w3=ffn_w3[l].astype(BF16),
                  ffn_w2=ffn_w2[l].astype(BF16))
        mod_all = _modulation(cvecs, w_ada[l], b_ada[l])[:1 + dec].reshape(1 + dec, 6, D_MODEL)
        y_p, s_big, k_l, v_l = _trunk(y_p, mod_all, 0, 0, None, None, p, wb)
        new_s.append(_state_from_big(s_big))
        new_k.append(k_l.reshape(bsz, seq, NA_HEADS, HEAD_DIM))
        new_v.append(v_l.reshape(bsz, seq, NA_HEADS, HEAD_DIM))
        ctx_k = cache_na_k[:, l].reshape(dec, -1, NA_WIDTH)
        ctx_v = cache_na_v[:, l].reshape(dec, -1, NA_WIDTH)
        y_s, _, _, _ = _trunk(y_s, mod_all, 1, 1, _state_to_big(state_rwkv[:, l]), (ctx_k, ctx_v), p, wb)
    return (y_p, y_s, jnp.stack(new_s, axis=1), jnp.stack(new_k, axis=1), jnp.stack(new_v, axis=1))
```

```python
import functools

import numpy as np
import jax
import jax.numpy as jnp
from jax import lax
from jax.experimental import pallas as pl
from jax.experimental.pallas import tpu as pltpu

D_MODEL = 1024
GRID_W = 64
HEAD_DIM = 64
RW_HEADS = 8
RW_WIDTH = RW_HEADS * HEAD_DIM
NA_HEADS = 8
NA_WIDTH = NA_HEADS * HEAD_DIM
LORA_DECAY = 64
LORA_ICLR = 64
LORA_GATE = 128
NA_ROWS = 8
NA_COLS = 16
FF_HIDDEN = 2816
RW_COLS = 3 * RW_WIDTH + 2 * LORA_DECAY + 2 * LORA_ICLR + LORA_GATE
NA_IN_COLS = 3 * NA_WIDTH
GATE_COLS = 2 * D_MODEL
RMS_EPS = 1e-6
GN_EPS = 64e-5
L2_EPS = 1e-12
NEG_INF = -1e30
DECAY_SCALE = float(np.exp(-0.5))
QK_SCALE = HEAD_DIM ** -0.5
assert QK_SCALE == 0.125

LANES = 128
PAIRS = RW_HEADS // 2
CHUNK = 64
STACK = 2 * CHUNK
RWKV_PAIRS_CTX = 4
RWKV_PAIRS_LAT = 2
UNITS_PER_STEP = 16
TOKEN_TILE = 512
FF_CHUNK = 256
VMEM_LIMIT = 56 * 1024 * 1024

F32 = jnp.float32
BF16 = jnp.bfloat16


def _dot(a, b):
    return jnp.dot(a.astype(BF16), b.astype(BF16), preferred_element_type=F32)


def _dot_nt(a, b):
    return lax.dot_general(a.astype(BF16), b.astype(BF16), (((1,), (1,)), ((), ())),
                           preferred_element_type=F32)


def _split2(x):
    hi = x.astype(BF16)
    lo = (x - hi.astype(F32)).astype(BF16)
    return hi, lo


def _dot_exact_lhs(a_exact, b):
    h, l = _split2(b)
    d = lambda x: jnp.dot(a_exact, x, preferred_element_type=F32)
    return d(h) + d(l)


def _dot3(a, b):
    ah, al = _split2(a)
    bh, bl = _split2(b)
    d = lambda x, y: jnp.dot(x, y, preferred_element_type=F32)
    return d(ah, bh) + d(al, bh) + d(ah, bl)


def _head_ones():
    r = lax.broadcasted_iota(jnp.int32, (LANES, LANES), 0) // HEAD_DIM
    c = lax.broadcasted_iota(jnp.int32, (LANES, LANES), 1) // HEAD_DIM
    return jnp.where(r == c, 1.0, 0.0).astype(BF16)


def _head_sum(x, ones):
    return jnp.dot(x.astype(BF16), ones, preferred_element_type=F32)


def _sigmoid(x):
    return 0.5 * jnp.tanh(0.5 * x) + 0.5


def _rms_rows(x):
    return x * lax.rsqrt(jnp.mean(x * x, axis=-1, keepdims=True) + RMS_EPS)


def _const_spec(shape):
    nd = len(shape)
    return pl.BlockSpec(shape, lambda *_: (0,) * nd, pipeline_mode=pl.Buffered(1))


def _params(n_axes):
    return pltpu.CompilerParams(dimension_semantics=("arbitrary",) * n_axes,
                                vmem_limit_bytes=VMEM_LIMIT)


def _mod_kernel(c_ref, w_ref, b_ref, o_ref):
    s = c_ref[...]
    s = s * _sigmoid(s)
    o_ref[...] = _dot3(s, w_ref[...]) + b_ref[...]


def _modulation(cvecs, w_ada, b_ada):
    n = cvecs.shape[0]
    tn = 1536
    return pl.pallas_call(
        _mod_kernel,
        grid=(6 * D_MODEL // tn,),
        in_specs=[pl.BlockSpec((n, D_MODEL), lambda j: (0, 0)),
                  pl.BlockSpec((D_MODEL, tn), lambda j: (0, j)),
                  pl.BlockSpec((1, tn), lambda j: (0, j))],
        out_specs=pl.BlockSpec((n, tn), lambda j: (0, j)),
        out_shape=jax.ShapeDtypeStruct((n, 6 * D_MODEL), F32),
        compiler_params=_params(1),
        name="modulation",
    )(cvecs, w_ada, b_ada.reshape(1, -1))


def _inproj_kernel(x_ref, mod_ref, g_ref, w_ref, urw_ref, una_ref, gt_ref):
    x = x_ref[0]
    h = _rms_rows(x) * g_ref[...]
    h = (h * (1.0 + mod_ref[0, 1:2, :]) + mod_ref[0, 0:1, :]).astype(BF16)
    d = lambda lo, hi: jnp.dot(h, w_ref[:, lo:hi], preferred_element_type=F32)
    urw_ref[0] = d(0, RW_COLS)
    una_ref[0] = d(RW_COLS, RW_COLS + NA_IN_COLS)
    gt_ref[0] = d(RW_COLS + NA_IN_COLS, RW_COLS + NA_IN_COLS + GATE_COLS).astype(BF16)


def _in_proj(x, mod_all, mod_off, mod_stride, norm_g, w_in_bf):
    bsz, t_len, _ = x.shape
    tm = TOKEN_TILE
    row = lambda b, i: (b, i, 0)
    return pl.pallas_call(
        _inproj_kernel,
        grid=(bsz, t_len // tm),
        in_specs=[pl.BlockSpec((1, tm, D_MODEL), row),
                  pl.BlockSpec((1, 6, D_MODEL), lambda b, i: (mod_off + mod_stride * b, 0, 0)),
                  _const_spec((1, D_MODEL)),
                  _const_spec(w_in_bf.shape)],
        out_specs=[pl.BlockSpec((1, tm, RW_COLS), row),
                   pl.BlockSpec((1, tm, NA_IN_COLS), row),
                   pl.BlockSpec((1, tm, GATE_COLS), row)],
        out_shape=[jax.ShapeDtypeStruct((bsz, t_len, RW_COLS), F32),
                   jax.ShapeDtypeStruct((bsz, t_len, NA_IN_COLS), F32),
                   jax.ShapeDtypeStruct((bsz, t_len, GATE_COLS), BF16)],
        compiler_params=_params(2),
        name="in_proj",
    )(x, mod_all, norm_g.reshape(1, -1), w_in_bf)


def _shift(x, mu):
    t_len = x.shape[0]
    row = lax.broadcasted_iota(jnp.int32, x.shape, 0)
    prev = jnp.where(row == 0, 0.0, pltpu.roll(x, 1, 0))
    nxt = jnp.where(row == t_len - 1, 0.0, pltpu.roll(x, t_len - 1, 0))
    return x + mu[0:1, :] * (prev - x) + mu[1:2, :] * (nxt - x)


def _stack_heads(x, lane_lo):
    return jnp.concatenate([x * lane_lo, x * (1.0 - lane_lo)], axis=0)


def _wkv_intra(units, consts):
    tri, mask_s, mask_i, eye, lane_lo, blk = consts
    eye_f = jnp.where(eye, 1.0, 0.0)
    stack = lambda z: _stack_heads(z, lane_lo)

    cums = [_dot_exact_lhs(tri[int(u[6])], u[1]) for u in units]

    prep = []
    for (r, lw, kd, v, kk, b, reverse), cum in zip(units, cums):
        mid_row = CHUNK // 2 if reverse else CHUNK // 2 - 1
        tot_row = 0 if reverse else CHUNK - 1
        a = -kk
        ex = cum - lw
        mid = cum[mid_row:mid_row + 1, :]
        tot = cum[tot_row:tot_row + 1, :]
        up = jnp.exp(cum - mid)
        dn = jnp.exp(mid - cum)
        tail = jnp.exp(tot - cum)
        prep.append(dict(
            at_m=stack(a * jnp.exp(ex - mid)).astype(BF16),
            rt_m=stack(r * up).astype(BF16),
            btkt=jnp.concatenate([stack(b * dn), stack(kd * dn)], axis=0).astype(BF16),
            a_e=stack(a * jnp.exp(ex)),
            r_e=stack(r * jnp.exp(cum)),
            bk_t=jnp.concatenate([stack(b * tail), stack(kd * tail)], axis=0).T.astype(BF16),
            vv=stack(v).astype(BF16),
            diag=jnp.where(eye, jnp.exp(tot), 0.0),
            rev=int(reverse)))

    ntd = lambda x, y: lax.dot_general(x, y, (((1,), (1,)), ((), ())), preferred_element_type=F32)
    mm = lambda x, y: jnp.dot(x, y, preferred_element_type=F32)
    top = [mask_s[p["rev"]] * ntd(p["at_m"], p["btkt"]) for p in prep]
    bot = [(mask_i[p["rev"]] * ntd(p["rt_m"], p["btkt"])).astype(BF16) for p in prep]

    diag_blk, swap_eye = blk
    off_blk = 1.0 - diag_blk
    both = [t[:, :LANES] + swap_eye for t in top]
    steps = CHUNK.bit_length() - 1
    diag_bf = diag_blk.astype(BF16)
    for j in range(steps):
        packed = [q.astype(BF16) for q in both]
        res = [mm(qb * diag_bf, qb) for qb in packed]
        both = [r + off_blk * q for r, q in zip(res, both)]
    ms = [pltpu.roll(q * off_blk, HEAD_DIM, 1) - eye_f for q in both]
    x0 =[jnp.concatenate([p["a_e"], mm(t[:, LANES:].astype(BF16), p["vv"])], axis=1) for p, t in zip(prep, top)]
    xs = [x + mm(m.astype(BF16), x.astype(BF16)) for x, m in zip(x0, ms)]

    out = []
    zeros = jnp.zeros((STACK, LANES), BF16)
    for p, x, bt in zip(prep, xs, bot):
        rhs = jnp.concatenate([x.astype(BF16), jnp.concatenate([zeros, p["vv"]], axis=1)], axis=0)
        lhs = jnp.concatenate([bt, p["bk_t"]], axis=0)
        res = mm(lhs, rhs)
        lhs2 = res[:, :LANES] + jnp.concatenate([p["r_e"], p["diag"]], axis=0)
        out.append((lhs2.astype(BF16), res[:, LANES:]))
    return out


def _wkv_constants():
    lane_head = np.arange(LANES) // HEAD_DIM
    ones = (lane_head[:, None] == lane_head[None, :]).astype(np.float32)
    t = np.arange(CHUNK)
    tri = np.stack([t[None, :] <= t[:, None], t[None, :] >= t[:, None]]).astype(np.float32)
    s = np.arange(STACK)
    same = (s[:, None] // CHUNK) == (s[None, :] // CHUNK)
    rs, cs = s[:, None], s[None, :]
    masks = np.stack([same & (cs < rs), same & (cs > rs), same & (cs <= rs), same & (cs >= rs)]).astype(np.float32)
    masks = np.concatenate([masks, masks], axis=-1)
    return jnp.asarray(ones, BF16), jnp.asarray(tri, BF16), jnp.asarray(masks, F32)


def _rwkv_kernel(*refs, t_len, has_s0, pairs, n_cast):
    (r_ref, k_ref, v_ref, lo_ref, mur_ref, muk_ref, muv_ref, mul_ref, w0_ref, a0_ref, wup_ref, aup_ref,
     gup_ref, kk_ref, ka_ref, rk_ref, lng_ref, lnb_ref, ones_ref, tri_ref, mask_ref) = refs[:21]
    pos = 21
    s0_ref = None
    if has_s0:
        s0_ref = refs[pos]
        pos += 1
    cast_in = refs[pos:pos + n_cast]
    pos += n_cast
    o_ref, sn_ref = refs[pos], refs[pos + 1]
    cast_out = refs[pos + 2:pos + 2 + n_cast]
    (r_s, v_s, kk_s, b0_s, b1_s, lw0_s, lw1_s, kd0_s, kd1_s, gate_s, bonus_s, yf_s, yb_s,
     lhs_s, add_s, st_s) = refs[pos + 2 + n_cast:]
    for w_in_ref, w_out_ref in zip(cast_in, cast_out):
        w_out_ref[...] = w_in_ref[...].astype(BF16)

    ones = ones_ref[...]
    lane = lax.broadcasted_iota(jnp.int32, (1, LANES), 1)
    lo_half = lane < HEAD_DIM
    lane_lo = jnp.where(lo_half, 1.0, 0.0)
    mm = lambda x, y: jnp.dot(x, y, preferred_element_type=F32)

    lo = _shift(lo_ref[0], mul_ref[...])
    wd = jnp.tanh(lo[:, 0:LANES])
    ad = lo[:, LANES:2 * LANES]
    sig_gd = _sigmoid(lo[:, 2 * LANES:3 * LANES]).astype(BF16)
    wd_split = [_split2(wd * m) for m in (lane_lo, 1.0 - lane_lo)]
    ad_bf = [(ad * m).astype(BF16) for m in (lane_lo, 1.0 - lane_lo)]
    for j in range(pairs):
        cols = slice(j * LANES, (j + 1) * LANES)
        r = _shift(r_ref[0, :, cols], mur_ref[:, cols])
        k = _shift(k_ref[0, :, cols], muk_ref[:, cols])
        v = _shift(v_ref[0, :, cols], muv_ref[:, cols])
        kk = k * kk_ref[:, cols]
        kk = kk * lax.rsqrt(_head_sum(kk * kk, ones) + L2_EPS)
        wup_h, wup_l = _split2(wup_ref[:, cols])
        aup = aup_ref[:, cols].astype(BF16)
        kdirs = []
        for e, (lw_s, kd_s, b_s) in enumerate(((lw0_s, kd0_s, b0_s), (lw1_s, kd1_s, b1_s))):
            wd_h, wd_l = wd_split[e]
            w_lin = w0_ref[e:e + 1, cols] + (mm(wd_h, wup_h) + mm(wd_l, wup_h) + mm(wd_h, wup_l))
            lw_s[j] = -DECAY_SCALE * _sigmoid(w_lin)
            iclr = _sigmoid(a0_ref[e:e + 1, cols] + mm(ad_bf[e], aup))
            kd = k * (1.0 + (iclr - 1.0) * ka_ref[:, cols])
            kd_s[j] = kd
            b_s[j] = kk * iclr
            kdirs.append(kd)
        gate_s[:, cols] = mm(sig_gd, gup_ref[:, cols].astype(BF16))
        bonus_s[:, cols] = _head_sum(r * (0.5 * (kdirs[0] + kdirs[1])) * rk_ref[:, cols], ones) * v
        r_s[j] = r
        v_s[j] = v
        kk_s[j] = kk

    n_chunks = t_len // CHUNK
    chunks_per = min(n_chunks, UNITS_PER_STEP // 2)
    pairs_per = min(pairs, UNITS_PER_STEP // (2 * chunks_per))
    groups = n_chunks // chunks_per
    rs = lax.broadcasted_iota(jnp.int32, (STACK, STACK), 0)
    cs = lax.broadcasted_iota(jnp.int32, (STACK, STACK), 1)
    as_f32 = lambda m: jnp.where(m, 1.0, 0.0)
    blk = (as_f32(rs // CHUNK == cs // CHUNK), as_f32(cs == (rs + CHUNK) % STACK))
    consts = ((tri_ref[0], tri_ref[1]), (mask_ref[0], mask_ref[1]), (mask_ref[2], mask_ref[3]),
              rs == cs, lane_lo, blk)
    dirs = ((lw0_s, kd0_s, b0_s), (lw1_s, kd1_s, b1_s))

    def intra_body(it, carry):
        pg = it // groups
        g = it % groups
        units, ids = [], []
        for jj in range(pairs_per):
            j = pg * pairs_per + jj
            for cc in range(chunks_per):
                c = g * chunks_per + cc
                rows = pl.ds(pl.multiple_of(c * CHUNK, CHUNK), CHUNK)
                for e, (lw_s, kd_s, b_s) in enumerate(dirs):
                    units.append((r_s[j, rows, :], lw_s[j, rows, :], kd_s[j, rows, :], v_s[j, rows, :],
                                  kk_s[j, rows, :], b_s[j, rows, :], e == 1))
                    ids.append((j * 2 + e) * n_chunks + c)
        for uid, (lhs, add) in zip(ids, _wkv_intra(units, consts)):
            lhs_s[uid] = lhs
            add_s[uid] = add
        return carry

    lax.fori_loop(0, (pairs // pairs_per) * groups, intra_body, 0)

    for j in range(pairs):
        for e in range(2):
            if has_s0:
                st_s[2 * j + e] = s0_ref[0, e, j].T
            else:
                st_s[2 * j + e] = jnp.zeros((LANES, LANES), F32)

    def state_body(it, carry):
        chunk = (it, n_chunks - 1 - it)
        uids = [(j * 2 + e) * n_chunks + chunk[e] for j in range(pairs) for e in range(2)]
        sts = [st_s[ch].astype(BF16) for ch in range(2 * pairs)]
        res = [mm(lhs_s[uid], st) + add_s[uid] for uid, st in zip(uids, sts)]
        for ch, rr in enumerate(res):
            j, e = divmod(ch, 2)
            y_s = yb_s if e else yf_s
            y_s[j, pl.ds(pl.multiple_of(chunk[e] * CHUNK, CHUNK), CHUNK), :] = rr[:CHUNK] + rr[CHUNK:STACK]
            st_s[ch] = rr[STACK:]
        return carry

    lax.fori_loop(0, n_chunks, state_body, 0)
    for j in range(pairs):
        for e in range(2):
            st_t = st_s[2 * j + e].T
            sn_ref[0, e, 2 * j] = st_t[:HEAD_DIM, :HEAD_DIM]
            sn_ref[0, e, 2 * j + 1] = pltpu.roll(st_t, HEAD_DIM, 1)[HEAD_DIM:, :HEAD_DIM]

    inv_d = 1.0 / HEAD_DIM
    for j in range(pairs):
        cols = slice(j * LANES, (j + 1) * LANES)
        y = yf_s[j] + yb_s[j]
        mean = _head_sum(y, ones) * inv_d
        dlt = y - mean
        var = _head_sum(dlt * dlt, ones) * inv_d
        yn = dlt * lax.rsqrt(var + GN_EPS) * lng_ref[:, cols] + lnb_ref[:, cols]
        o_ref[0, :, cols] = (yn + bonus_s[:, cols]) * gate_s[:, cols]


def _rwkv_branch(u_rw, s0_big, p, pairs, cast=()):
    bsz, t_len, _ = u_rw.shape
    has_s0 = s0_big is not None
    width = pairs * LANES
    seg = RW_WIDTH // width
    tok = lambda off: pl.BlockSpec((1, t_len, width), lambda b, j: (b, 0, off + j))
    mu = lambda off: pl.BlockSpec((2, width), lambda b, j: (0, off + j))
    vec2 = pl.BlockSpec((2, width), lambda b, j: (0, j))
    vec1 = pl.BlockSpec((1, width), lambda b, j: (0, j))
    mat = pl.BlockSpec((LANES, width), lambda b, j: (0, j))
    lora_w = 3 * LANES
    lora_blk = 3 * RW_WIDTH // lora_w
    in_specs = [tok(0), tok(seg), tok(2 * seg),
                pl.BlockSpec((1, t_len, lora_w), lambda b, j: (b, 0, lora_blk)),
                mu(0), mu(seg), mu(2 * seg),
                pl.BlockSpec((2, lora_w), lambda b, j: (0, lora_blk)),
                vec2, vec2, mat, mat, mat, vec1, vec1, vec1, vec1, vec1,
                _const_spec((LANES, LANES)), _const_spec((2, CHUNK, CHUNK)), _const_spec((4, STACK, 2 * STACK))]
    args = [u_rw, u_rw, u_rw, u_rw, p["shift_mu"], p["shift_mu"], p["shift_mu"], p["shift_mu"],
            p["rw_w0"], p["rw_a0"],
            p["rw_w_up"].reshape(2 * LORA_DECAY, RW_WIDTH), p["rw_a_up"].reshape(2 * LORA_ICLR, RW_WIDTH),
            p["rw_g_up"], p["rw_k_k"].reshape(1, -1), p["rw_k_a"].reshape(1, -1),
            p["rw_r_k"].reshape(1, -1), p["rw_ln_g"].reshape(1, -1), p["rw_ln_b"].reshape(1, -1),
            *_wkv_constants()]
    st_spec = pl.BlockSpec((1, 2, pairs, LANES, LANES), lambda b, j: (b, 0, j, 0, 0))
    if has_s0:
        in_specs.append(st_spec)
        args.append(s0_big)
    out_specs = [pl.BlockSpec((1, t_len, width), lambda b, j: (b, 0, j)),
                 pl.BlockSpec((1, 2, 2 * pairs, HEAD_DIM, HEAD_DIM), lambda b, j: (b, 0, j, 0, 0))]
    out_shape = [jax.ShapeDtypeStruct((bsz, t_len, RW_WIDTH), F32),
                 jax.ShapeDtypeStruct((bsz, 2, RW_HEADS, HEAD_DIM, HEAD_DIM), F32)]
    for w in cast:
        assert PAIRS == pairs and w.shape[0] % (16 * bsz) == 0
        blk = pl.BlockSpec((w.shape[0] // bsz, w.shape[1]), lambda b, j: (b, 0))
        in_specs.append(blk)
        args.append(w)
        out_specs.append(blk)
        out_shape.append(jax.ShapeDtypeStruct(w.shape, BF16))
    n_units = 2 * pairs * (t_len // CHUNK)
    per_pair = pltpu.VMEM((pairs, t_len, LANES), F32)
    full = pltpu.VMEM((t_len, width), F32)
    scratch = [per_pair] * 9 + [full, full, per_pair, per_pair,
                                pltpu.VMEM((n_units, 2 * STACK, LANES), BF16),
                                pltpu.VMEM((n_units, 2 * STACK, LANES), F32),
                                pltpu.VMEM((2 * pairs, LANES, LANES), F32)]
    o_rw, s_new, *casted = pl.pallas_call(
        functools.partial(_rwkv_kernel, t_len=t_len, has_s0=has_s0, pairs=pairs, n_cast=len(cast)),
        grid=(bsz, PAIRS // pairs),
        in_specs=in_specs,
        out_specs=out_specs,
        out_shape=out_shape,
        scratch_shapes=scratch,
        compiler_params=_params(2),
        name="rwkv_branch",
    )(*args)
    return o_rw, s_new, casted


def _state_to_big(s0):
    bsz = s0.shape[0]
    x = s0.reshape(bsz, 2, PAIRS, 2, HEAD_DIM, HEAD_DIM)
    z = jnp.zeros_like(x[:, :, :, 0])
    top = jnp.concatenate([x[:, :, :, 0], z], axis=-1)
    bot = jnp.concatenate([z, x[:, :, :, 1]], axis=-1)
    return jnp.concatenate([top, bot], axis=-2)


def _qk_norm(t, g, ones):
    ms = _head_sum(t * t, ones) * (1.0 / HEAD_DIM)
    return t * lax.rsqrt(ms + RMS_EPS) * g


def _nt(x, y):
    return lax.dot_general(x, y, (((1,), (1,)), ((), ())), preferred_element_type=F32)


def _na_ctx_kernel(q_ref, k_ref, v_ref, qg_ref, kg_ref, o_ref, kn_ref, vc_ref):
    ones = _head_ones()
    lo_half = lax.broadcasted_iota(jnp.int32, (1, LANES), 1) < HEAD_DIM
    lo = jnp.where(lo_half, 1.0, 0.0)
    t_len = q_ref.shape[1]
    qs, ks, vs = [], [], []
    for j in range(NA_WIDTH // LANES):
        cols = slice(j * LANES, (j + 1) * LANES)
        qn = _qk_norm(q_ref[0, :, cols], qg_ref[...], ones)
        kn = _qk_norm(k_ref[0, :, cols], kg_ref[...], ones)
        v = v_ref[0, :, cols]
        kn_ref[0, :, cols] = kn
        vc_ref[0, :, cols] = v
        qn = qn * QK_SCALE
        qs.append(jnp.concatenate([qn * lo, qn * (1.0 - lo)], axis=0).astype(BF16))
        ks.append(kn.astype(BF16))
        vs.append(v.astype(BF16))
    logits = [_nt(q, k) for q, k in zip(qs, ks)]
    ms = [jnp.max(s, axis=-1, keepdims=True) for s in logits]
    ps = [jnp.exp(s - m) for s, m in zip(logits, ms)]
    ls = [jnp.sum(p, axis=-1, keepdims=True) for p in ps]
    outs = [jnp.dot(p.astype(BF16), v, preferred_element_type=F32) / l for p, v, l in zip(ps, vs, ls)]
    for j, o in enumerate(outs):
        o_ref[0, :, j * LANES:(j + 1) * LANES] = jnp.where(lo_half, o[:t_len], o[t_len:])


def _na_context(u_na, q_g, k_g):
    bsz, t_len, _ = u_na.shape
    tok = lambda seg: pl.BlockSpec((1, t_len, NA_WIDTH), lambda b: (b, 0, seg))
    out_blk = pl.BlockSpec((1, t_len, NA_WIDTH), lambda b: (b, 0, 0))
    g2 = lambda g: jnp.tile(g.reshape(1, HEAD_DIM), (1, 2))
    shp = jax.ShapeDtypeStruct((bsz, t_len, NA_WIDTH), F32)
    return pl.pallas_call(
        _na_ctx_kernel,
        grid=(bsz,),
        in_specs=[tok(0), tok(1), tok(2), _const_spec((1, LANES)), _const_spec((1, LANES))],
        out_specs=[out_blk, out_blk, out_blk],
        out_shape=[shp, shp, shp],
        compiler_params=_params(1),
        name="na_context",
    )(u_na, u_na, u_na, g2(q_g), g2(k_g))


NA_ROW_ILP = 8


def _na_lat_kernel(q_ref, k_ref, v_ref, kc_ref, vc_ref, qg_ref, kg_ref, tab_ref, o_ref,
                   q0_s, q1_s, kn_s, v_s, kc_s, vc_s, *, rows, kr):
    ones = _head_ones()
    lo_half = lax.broadcasted_iota(jnp.int32, (1, LANES), 1) < HEAD_DIM
    lo = jnp.where(lo_half, 1.0, 0.0)
    qn = _qk_norm(q_ref[0], qg_ref[...], ones) * QK_SCALE
    q0_s[...] = (qn * lo).astype(BF16)
    q1_s[...] = (qn * (1.0 - lo)).astype(BF16)
    kn_s[...] = _qk_norm(k_ref[0], kg_ref[...], ones).astype(BF16)
    v_s[...] = v_ref[0].astype(BF16)
    kc_s[...] = kc_ref[0].astype(BF16)
    vc_s[...] = vc_ref[0].astype(BF16)
    win = kr * GRID_W

    def body(it, carry):
        qs, k_rows, q_rows, biases = [], [], [], []
        for s in range(NA_ROW_ILP):
            i = it * NA_ROW_ILP + s
            r0 = jnp.clip(i - kr // 2, 0, rows - kr)
            d0 = r0 - i + (NA_ROWS - 1)
            qr = pl.ds(pl.multiple_of(i * GRID_W, GRID_W), GRID_W)
            q_rows.append(qr)
            k_rows.append(pl.ds(pl.multiple_of(r0 * GRID_W, GRID_W), win))
            qs.append(jnp.concatenate([q0_s[qr, :], q1_s[qr, :]], axis=0))
            biases.append(jnp.concatenate(
                [jnp.concatenate([tab_ref[h, d0 + 2 * m] for m in range(kr // 2)], axis=1) for h in range(2)],
                axis=0))
        lw = [_nt(q, kn_s[kr_, :]) + b for q, kr_, b in zip(qs, k_rows, biases)]
        lc = [_nt(q, kc_s[...]) for q in qs]
        ms = [jnp.maximum(jnp.max(a, axis=-1, keepdims=True), jnp.max(c, axis=-1, keepdims=True))
              for a, c in zip(lw, lc)]
        pw = [jnp.exp(a - m) for a, m in zip(lw, ms)]
        pc = [jnp.exp(c - m) for c, m in zip(lc, ms)]
        ls = [jnp.sum(a, axis=-1, keepdims=True) + jnp.sum(c, axis=-1, keepdims=True) for a, c in zip(pw, pc)]
        outs = [(jnp.dot(a.astype(BF16), v_s[kr_, :], preferred_element_type=F32)
                 + jnp.dot(c.astype(BF16), vc_s[...], preferred_element_type=F32)) / l
                for a, c, kr_, l in zip(pw, pc, k_rows, ls)]
        for qr, o in zip(q_rows, outs):
            o_ref[0, qr, :] = jnp.where(lo_half, o[:GRID_W], o[GRID_W:])
        return carry

    lax.fori_loop(0, rows // NA_ROW_ILP, body, 0)


def _latent_bias_table(rpb):
    qc = np.arange(GRID_W)[:, None]
    kc = np.arange(GRID_W)[None, :]
    ws = np.clip(qc - NA_COLS // 2, 0, GRID_W - NA_COLS)
    valid = (kc >= ws) & (kc < ws + NA_COLS)
    dc = np.clip(kc - qc, -(NA_COLS - 1), NA_COLS - 1) + NA_COLS - 1
    onehot = (dc[None] == np.arange(2 * NA_COLS - 1)[:, None, None]).astype(np.float32)
    cb = jnp.einsum("hdc,cqk->hdqk", rpb, jnp.asarray(onehot), precision=lax.Precision.HIGHEST)
    cb = jnp.where(valid[None, None], cb, NEG_INF)
    return jnp.concatenate([cb[:, :-1], cb[:, 1:]], axis=-1)


def _na_latent(u_na, k_ctx, v_ctx, q_g, k_g, rpb):
    bsz, t_len, _ = u_na.shape
    rows = t_len // GRID_W
    kr = min(NA_ROWS, rows)
    assert kr % 2 == 0 and rows % NA_ROW_ILP == 0
    ctx_len = k_ctx.shape[1]
    seg = NA_WIDTH // LANES
    tok = lambda off: pl.BlockSpec((1, t_len, LANES), lambda b, j: (b, 0, off + j))
    ctx = pl.BlockSpec((1, ctx_len, LANES), lambda b, j: (b, 0, j))
    g2 = lambda g: jnp.tile(g.reshape(1, HEAD_DIM), (1, 2))
    tab = _latent_bias_table(rpb)
    tok_s = pltpu.VMEM((t_len, LANES), BF16)
    ctx_s = pltpu.VMEM((ctx_len, LANES), BF16)
    return pl.pallas_call(
        functools.partial(_na_lat_kernel, rows=rows, kr=kr),
        grid=(bsz, seg),
        in_specs=[tok(0), tok(seg), tok(2 * seg), ctx, ctx,
                  _const_spec((1, LANES)), _const_spec((1, LANES)),
                  pl.BlockSpec((2, 2 * NA_ROWS - 2, GRID_W, 2 * GRID_W), lambda b, j: (j, 0, 0, 0))],
        out_specs=pl.BlockSpec((1, t_len, LANES), lambda b, j: (b, 0, j)),
        out_shape=jax.ShapeDtypeStruct((bsz, t_len, NA_WIDTH), F32),
        scratch_shapes=[tok_s, tok_s, tok_s, tok_s, ctx_s, ctx_s],
        compiler_params=_params(2),
        name="na_latent",
    )(u_na, u_na, u_na, k_ctx, v_ctx, g2(q_g), g2(k_g), tab)


def _out_ffn_kernel(x_ref, orw_ref, ona_ref, gt_ref, mod_ref, g_ref, wor_ref, won_ref, wout_ref,
                    w1_ref, w3_ref, w2_ref, y_ref):
    g_rw = _sigmoid(gt_ref[0, :, :D_MODEL].astype(F32))
    g_na = _sigmoid(gt_ref[0, :, D_MODEL:].astype(F32))
    merged = g_rw * _dot(orw_ref[0], wor_ref[...]) + g_na * _dot(ona_ref[0], won_ref[...])
    x1 = x_ref[0] + mod_ref[0, 2:3, :] * _dot(merged, wout_ref[...])
    h2 = _rms_rows(x1) * g_ref[...]
    h2 = (h2 * (1.0 + mod_ref[0, 4:5, :]) + mod_ref[0, 3:4, :]).astype(BF16)
    acc = jnp.zeros(x1.shape, F32)
    for c in range(FF_HIDDEN // FF_CHUNK):
        cols = slice(c * FF_CHUNK, (c + 1) * FF_CHUNK)
        a = jnp.dot(h2, w1_ref[:, cols], preferred_element_type=F32)
        b = jnp.dot(h2, w3_ref[:, cols], preferred_element_type=F32)
        hh = (a * _sigmoid(a) * b).astype(BF16)
        acc = acc + jnp.dot(hh, w2_ref[cols, :], preferred_element_type=F32)
    y_ref[0] = x1 + mod_ref[0, 5:6, :] * acc


def _out_ffn(x, o_rw, o_na, gates, mod_all, mod_off, mod_stride, norm_g, wb):
    bsz, t_len, _ = x.shape
    tm = TOKEN_TILE
    row = lambda b, i: (b, i, 0)
    weights = [wb[n] for n in LATE_WEIGHTS]
    return pl.pallas_call(
        _out_ffn_kernel,
        grid=(bsz, t_len // tm),
        in_specs=[pl.BlockSpec((1, tm, D_MODEL), row),
                  pl.BlockSpec((1, tm, RW_WIDTH), row),
                  pl.BlockSpec((1, tm, NA_WIDTH), row),
                  pl.BlockSpec((1, tm, GATE_COLS), row),
                  pl.BlockSpec((1, 6, D_MODEL), lambda b, i: (mod_off + mod_stride * b, 0, 0)),
                  _const_spec((1, D_MODEL))] + [_const_spec(w.shape) for w in weights],
        out_specs=pl.BlockSpec((1, tm, D_MODEL), row),
        out_shape=jax.ShapeDtypeStruct((bsz, t_len, D_MODEL), F32),
        compiler_params=_params(2),
        name="out_ffn",
    )(x, o_rw, o_na, gates, mod_all, norm_g.reshape(1, -1), *weights)


LATE_WEIGHTS = ("w_o_rwkv", "w_o_na", "w_out", "ffn_w1", "ffn_w3", "ffn_w2")


def _trunk(x, mod_all, mod_off, mod_stride, s0_big, ctx_kv, p, w_in_bf, late):
    shape = x.shape
    flat = (lambda t: t.reshape(1, -1, t.shape[-1])) if mod_stride == 0 else (lambda t: t)
    unflat = lambda t: t.reshape(shape[0], shape[1], t.shape[-1])
    u_rw, u_na, gates = _in_proj(flat(x), mod_all, mod_off, mod_stride, p["norm1_g"], w_in_bf)
    u_rw, u_na = unflat(u_rw), unflat(u_na)
    if ctx_kv is None:
        o_rw, s_new, casted = _rwkv_branch(u_rw, s0_big, p, RWKV_PAIRS_CTX, [late[n] for n in LATE_WEIGHTS])
        late = dict(zip(LATE_WEIGHTS, casted))
    else:
        o_rw, s_new, _ = _rwkv_branch(u_rw, s0_big, p, RWKV_PAIRS_LAT)
    if ctx_kv is None:
        o_na, k_new, v_new = _na_context(u_na, p["na_q_g"], p["na_k_g"])
    else:
        o_na = _na_latent(u_na, ctx_kv[0], ctx_kv[1], p["na_q_g"], p["na_k_g"], p["na_rpb"])
        k_new = v_new = None
    y = _out_ffn(flat(x), flat(o_rw), flat(o_na), gates, mod_all, mod_off, mod_stride, p["norm2_g"], late)
    return unflat(y), s_new, k_new, v_new, late


def kernel(x_prompt, x_sample, state_rwkv, cache_na_k, cache_na_v, c, c_ctx, norm1_g, norm2_g, w_ada, b_ada,
           w_in, shift_mu, rw_w0, rw_w_up, rw_a0, rw_a_up, rw_g_up, rw_k_k, rw_k_a, rw_r_k, rw_ln_g, rw_ln_b,
           na_q_g, na_k_g, na_rpb, w_o_rwkv, w_o_na, w_out, ffn_w1, ffn_w3, ffn_w2):
    depth = w_in.shape[0]
    bsz, seq = x_prompt.shape[:2]
    dec = x_sample.shape[0]
    n_vec = 8
    cvecs = jnp.concatenate([c_ctx[None, :], c, jnp.zeros((n_vec - 1 - dec, D_MODEL), F32)], axis=0)
    y_p, y_s = x_prompt, x_sample
    new_s, new_k, new_v = [], [], []
    for l in range(depth):
        p = dict(norm1_g=norm1_g[l], norm2_g=norm2_g[l], shift_mu=shift_mu[l], rw_w0=rw_w0[l],
                 rw_w_up=rw_w_up[l], rw_a0=rw_a0[l], rw_a_up=rw_a_up[l], rw_g_up=rw_g_up[l],
                 rw_k_k=rw_k_k[l], rw_k_a=rw_k_a[l], rw_r_k=rw_r_k[l], rw_ln_g=rw_ln_g[l],
                 rw_ln_b=rw_ln_b[l], na_q_g=na_q_g[l], na_k_g=na_k_g[l], na_rpb=na_rpb[l])
        w_in_bf = w_in[l].astype(BF16)
        late = dict(w_o_rwkv=w_o_rwkv[l], w_o_na=w_o_na[l], w_out=w_out[l], ffn_w1=ffn_w1[l], ffn_w3=ffn_w3[l],
                    ffn_w2=ffn_w2[l])
        mod_all = _modulation(cvecs, w_ada[l], b_ada[l]).reshape(n_vec, 6, D_MODEL)
        y_p, s_big, k_l, v_l, late = _trunk(y_p, mod_all, 0, 0, None, None, p, w_in_bf, late)
        new_s.append(s_big)
        new_k.append(k_l.reshape(bsz, seq, NA_HEADS, HEAD_DIM))
        new_v.append(v_l.reshape(bsz, seq, NA_HEADS, HEAD_DIM))
        ctx_k = cache_na_k[:, l].reshape(dec, -1, NA_WIDTH)
        ctx_v = cache_na_v[:, l].reshape(dec, -1, NA_WIDTH)
        y_s = _trunk(y_s, mod_all, 1, 1, _state_to_big(state_rwkv[:, l]), (ctx_k, ctx_v), p, w_in_bf, late)[0]
    return (y_p, y_s, jnp.stack(new_s, axis=1), jnp.stack(new_k, axis=1), jnp.stack(new_v, axis=1))
```

```python
import functools

import numpy as np
import jax
import jax.numpy as jnp
from jax import lax
from jax.experimental import pallas as pl
from jax.experimental.pallas import tpu as pltpu

D_MODEL = 1024
GRID_W = 64
HEAD_DIM = 64
RW_HEADS = 8
RW_WIDTH = RW_HEADS * HEAD_DIM
NA_HEADS = 8
NA_WIDTH = NA_HEADS * HEAD_DIM
LORA_DECAY = 64
LORA_ICLR = 64
LORA_GATE = 128
NA_ROWS = 8
NA_COLS = 16
FF_HIDDEN = 2816
RW_COLS = 3 * RW_WIDTH + 2 * LORA_DECAY + 2 * LORA_ICLR + LORA_GATE
NA_IN_COLS = 3 * NA_WIDTH
GATE_COLS = 2 * D_MODEL
RMS_EPS = 1e-6
GN_EPS = 64e-5
L2_EPS = 1e-12
NEG_INF = -1e30
DECAY_SCALE = float(np.exp(-0.5))
QK_SCALE = HEAD_DIM ** -0.5
assert QK_SCALE == 0.125

LANES = 128
PAIRS = RW_HEADS // 2
CHUNK = 64
STACK = 2 * CHUNK
RWKV_PAIRS_CTX = 4
RWKV_PAIRS_LAT = 2
UNITS_PER_STEP = 16
TOKEN_TILE = 512
FF_CHUNK = 256
VMEM_LIMIT = 56 * 1024 * 1024

F32 = jnp.float32
BF16 = jnp.bfloat16


def _dot(a, b):
    return jnp.dot(a.astype(BF16), b.astype(BF16), preferred_element_type=F32)


def _dot_nt(a, b):
    return lax.dot_general(a.astype(BF16), b.astype(BF16), (((1,), (1,)), ((), ())),
                           preferred_element_type=F32)


def _split2(x):
    hi = x.astype(BF16)
    lo = (x - hi.astype(F32)).astype(BF16)
    return hi, lo


def _dot_exact_lhs(a_exact, b):
    h, l = _split2(b)
    d = lambda x: jnp.dot(a_exact, x, preferred_element_type=F32)
    return d(h) + d(l)


def _dot3(a, b):
    ah, al = _split2(a)
    bh, bl = _split2(b)
    d = lambda x, y: jnp.dot(x, y, preferred_element_type=F32)
    return d(ah, bh) + d(al, bh) + d(ah, bl)


def _head_ones():
    r = lax.broadcasted_iota(jnp.int32, (LANES, LANES), 0) // HEAD_DIM
    c = lax.broadcasted_iota(jnp.int32, (LANES, LANES), 1) // HEAD_DIM
    return jnp.where(r == c, 1.0, 0.0).astype(BF16)


def _head_sum(x, ones):
    return jnp.dot(x.astype(BF16), ones, preferred_element_type=F32)


def _sigmoid(x):
    return 0.5 * jnp.tanh(0.5 * x) + 0.5


def _rms_rows(x):
    return x * lax.rsqrt(jnp.mean(x * x, axis=-1, keepdims=True) + RMS_EPS)


def _const_spec(shape):
    nd = len(shape)
    return pl.BlockSpec(shape, lambda *_: (0,) * nd, pipeline_mode=pl.Buffered(1))


def _params(n_axes):
    return pltpu.CompilerParams(dimension_semantics=("arbitrary",) * n_axes,
                                vmem_limit_bytes=VMEM_LIMIT)


def _mod_kernel(c_ref, w_ref, b_ref, win_ref, o_ref, winb_ref):
    s = c_ref[...]
    s = s * _sigmoid(s)
    o_ref[...] = _dot3(s, w_ref[...]) + b_ref[...]
    winb_ref[...] = win_ref[...].astype(BF16)


def _modulation(cvecs, w_ada, b_ada, w_in):
    n = cvecs.shape[0]
    tn = 1536
    steps = 6 * D_MODEL // tn
    rows = w_in.shape[0] // steps
    assert rows % 16 == 0
    return pl.pallas_call(
        _mod_kernel,
        grid=(steps,),
        in_specs=[pl.BlockSpec((n, D_MODEL), lambda j: (0, 0)),
                  pl.BlockSpec((D_MODEL, tn), lambda j: (0, j)),
                  pl.BlockSpec((1, tn), lambda j: (0, j)),
                  pl.BlockSpec((rows, w_in.shape[1]), lambda j: (j, 0))],
        out_specs=[pl.BlockSpec((n, tn), lambda j: (0, j)),
                   pl.BlockSpec((rows, w_in.shape[1]), lambda j: (j, 0))],
        out_shape=[jax.ShapeDtypeStruct((n, 6 * D_MODEL), F32),
                   jax.ShapeDtypeStruct(w_in.shape, BF16)],
        compiler_params=_params(1),
        name="modulation",
    )(cvecs, w_ada, b_ada.reshape(1, -1), w_in)


def _inproj_kernel(x_ref, mod_ref, g_ref, w_ref, urw_ref, una_ref, gt_ref):
    x = x_ref[0]
    h = _rms_rows(x) * g_ref[...]
    h = (h * (1.0 + mod_ref[0, 1:2, :]) + mod_ref[0, 0:1, :]).astype(BF16)
    d = lambda lo, hi: jnp.dot(h, w_ref[:, lo:hi], preferred_element_type=F32)
    urw_ref[0] = d(0, RW_COLS)
    una_ref[0] = d(RW_COLS, RW_COLS + NA_IN_COLS)
    gt_ref[0] = d(RW_COLS + NA_IN_COLS, RW_COLS + NA_IN_COLS + GATE_COLS).astype(BF16)


def _in_proj(x, mod_all, mod_off, mod_stride, norm_g, w_in_bf):
    bsz, t_len, _ = x.shape
    tm = TOKEN_TILE
    row = lambda b, i: (b, i, 0)
    return pl.pallas_call(
        _inproj_kernel,
        grid=(bsz, t_len // tm),
        in_specs=[pl.BlockSpec((1, tm, D_MODEL), row),
                  pl.BlockSpec((1, 6, D_MODEL), lambda b, i: (mod_off + mod_stride * b, 0, 0)),
                  _const_spec((1, D_MODEL)),
                  _const_spec(w_in_bf.shape)],
        out_specs=[pl.BlockSpec((1, tm, RW_COLS), row),
                   pl.BlockSpec((1, tm, NA_IN_COLS), row),
                   pl.BlockSpec((1, tm, GATE_COLS), row)],
        out_shape=[jax.ShapeDtypeStruct((bsz, t_len, RW_COLS), F32),
                   jax.ShapeDtypeStruct((bsz, t_len, NA_IN_COLS), F32),
                   jax.ShapeDtypeStruct((bsz, t_len, GATE_COLS), BF16)],
        compiler_params=_params(2),
        name="in_proj",
    )(x, mod_all, norm_g.reshape(1, -1), w_in_bf)


def _shift(x, mu):
    t_len = x.shape[0]
    row = lax.broadcasted_iota(jnp.int32, x.shape, 0)
    prev = jnp.where(row == 0, 0.0, pltpu.roll(x, 1, 0))
    nxt = jnp.where(row == t_len - 1, 0.0, pltpu.roll(x, t_len - 1, 0))
    return x + mu[0:1, :] * (prev - x) + mu[1:2, :] * (nxt - x)


def _stack_heads(x, lane_lo):
    return jnp.concatenate([x * lane_lo, x * (1.0 - lane_lo)], axis=0)


def _wkv_intra(units, consts):
    tri, mask_s, mask_i, eye, lane_lo, blk = consts
    stack = lambda z: _stack_heads(z, lane_lo)

    cums = [_dot_exact_lhs(tri[int(u[6])], u[1]) for u in units]

    prep = []
    for (r, lw, kd, v, kk, b, reverse), cum in zip(units, cums):
        mid_row = CHUNK // 2 if reverse else CHUNK // 2 - 1
        tot_row = 0 if reverse else CHUNK - 1
        a = -kk
        ex = cum - lw
        mid = cum[mid_row:mid_row + 1, :]
        tot = cum[tot_row:tot_row + 1, :]
        up = jnp.exp(cum - mid)
        dn = jnp.exp(mid - cum)
        tail = jnp.exp(tot - cum)
        prep.append(dict(
            at_m=stack(a * jnp.exp(ex - mid)).astype(BF16),
            rt_m=stack(r * up).astype(BF16),
            btkt=jnp.concatenate([stack(b * dn), stack(kd * dn)], axis=0).astype(BF16),
            a_e=stack(a * jnp.exp(ex)),
            r_e=stack(r * jnp.exp(cum)),
            bk_t=jnp.concatenate([stack(b * tail), stack(kd * tail)], axis=0).T.astype(BF16),
            vv=stack(v).astype(BF16),
            diag=jnp.where(eye, jnp.exp(tot), 0.0),
            rev=int(reverse)))

    ntd = lambda x, y: lax.dot_general(x, y, (((1,), (1,)), ((), ())), preferred_element_type=F32)
    mm = lambda x, y: jnp.dot(x, y, preferred_element_type=F32)
    top = [mask_s[p["rev"]] * ntd(p["at_m"], p["btkt"]) for p in prep]
    bot = [(mask_i[p["rev"]] * ntd(p["rt_m"], p["btkt"])).astype(BF16) for p in prep]

    diag_blk, swap_eye = blk
    off_blk = 1.0 - diag_blk
    both = [t[:, :LANES] + swap_eye for t in top]
    steps = CHUNK.bit_length() - 1
    diag_bf = diag_blk.astype(BF16)
    for j in range(steps):
        packed = [q.astype(BF16) for q in both]
        res = [mm(qb * diag_bf, qb) for qb in packed]
        both = [r + off_blk * q for r, q in zip(res, both)]
    ts = [pltpu.roll(q, HEAD_DIM, 1).astype(BF16) for q in both]
    x0 = [jnp.concatenate([p["a_e"], mm(t[:, LANES:].astype(BF16), p["vv"])], axis=1) for p, t in zip(prep, top)]
    xs = [mm(t, x.astype(BF16)) for x, t in zip(x0, ts)]

    out = []
    zeros = jnp.zeros((STACK, LANES), BF16)
    for p, x, bt in zip(prep, xs, bot):
        rhs = jnp.concatenate([x.astype(BF16), jnp.concatenate([zeros, p["vv"]], axis=1)], axis=0)
        lhs = jnp.concatenate([bt, p["bk_t"]], axis=0)
        res = mm(lhs, rhs)
        lhs2 = res[:, :LANES] + jnp.concatenate([p["r_e"], p["diag"]], axis=0)
        out.append((lhs2.astype(BF16), res[:, LANES:]))
    return out


def _wkv_constants():
    lane_head = np.arange(LANES) // HEAD_DIM
    ones = (lane_head[:, None] == lane_head[None, :]).astype(np.float32)
    t = np.arange(CHUNK)
    tri = np.stack([t[None, :] <= t[:, None], t[None, :] >= t[:, None]]).astype(np.float32)
    s = np.arange(STACK)
    same = (s[:, None] // CHUNK) == (s[None, :] // CHUNK)
    rs, cs = s[:, None], s[None, :]
    masks = np.stack([same & (cs < rs), same & (cs > rs), same & (cs <= rs), same & (cs >= rs)]).astype(np.float32)
    masks = np.concatenate([masks, masks], axis=-1)
    return jnp.asarray(ones, BF16), jnp.asarray(tri, BF16), jnp.asarray(masks, F32)


def _rwkv_kernel(*refs, t_len, has_s0, pairs, n_cast):
    (r_ref, k_ref, v_ref, lo_ref, mur_ref, muk_ref, muv_ref, mul_ref, w0_ref, a0_ref, wup_ref, aup_ref,
     gup_ref, kk_ref, ka_ref, rk_ref, lng_ref, lnb_ref, ones_ref, tri_ref, mask_ref) = refs[:21]
    pos = 21
    s0_ref = None
    if has_s0:
        s0_ref = refs[pos]
        pos += 1
    cast_in = refs[pos:pos + n_cast]
    pos += n_cast
    o_ref, sn_ref = refs[pos], refs[pos + 1]
    cast_out = refs[pos + 2:pos + 2 + n_cast]
    (r_s, v_s, kk_s, b0_s, b1_s, lw0_s, lw1_s, kd0_s, kd1_s, gate_s, bonus_s, yf_s, yb_s,
     lhs_s, add_s, st_s) = refs[pos + 2 + n_cast:]
    for w_in_ref, w_out_ref in zip(cast_in, cast_out):
        w_out_ref[...] = w_in_ref[...].astype(BF16)

    ones = ones_ref[...]
    lane = lax.broadcasted_iota(jnp.int32, (1, LANES), 1)
    lo_half = lane < HEAD_DIM
    lane_lo = jnp.where(lo_half, 1.0, 0.0)
    mm = lambda x, y: jnp.dot(x, y, preferred_element_type=F32)

    lo = _shift(lo_ref[0], mul_ref[...])
    wd = jnp.tanh(lo[:, 0:LANES])
    ad = lo[:, LANES:2 * LANES]
    sig_gd = _sigmoid(lo[:, 2 * LANES:3 * LANES]).astype(BF16)
    wd_split = [_split2(wd * m) for m in (lane_lo, 1.0 - lane_lo)]
    ad_bf = [(ad * m).astype(BF16) for m in (lane_lo, 1.0 - lane_lo)]
    for j in range(pairs):
        cols = slice(j * LANES, (j + 1) * LANES)
        r = _shift(r_ref[0, :, cols], mur_ref[:, cols])
        k = _shift(k_ref[0, :, cols], muk_ref[:, cols])
        v = _shift(v_ref[0, :, cols], muv_ref[:, cols])
        kk = k * kk_ref[:, cols]
        kk = kk * lax.rsqrt(_head_sum(kk * kk, ones) + L2_EPS)
        wup_h, wup_l = _split2(wup_ref[:, cols])
        aup = aup_ref[:, cols].astype(BF16)
        kdirs = []
        for e, (lw_s, kd_s, b_s) in enumerate(((lw0_s, kd0_s, b0_s), (lw1_s, kd1_s, b1_s))):
            wd_h, wd_l = wd_split[e]
            w_lin = w0_ref[e:e + 1, cols] + (mm(wd_h, wup_h) + mm(wd_l, wup_h) + mm(wd_h, wup_l))
            lw_s[j] = -DECAY_SCALE * _sigmoid(w_lin)
            iclr = _sigmoid(a0_ref[e:e + 1, cols] + mm(ad_bf[e], aup))
            kd = k * (1.0 + (iclr - 1.0) * ka_ref[:, cols])
            kd_s[j] = kd
            b_s[j] = kk * iclr
            kdirs.append(kd)
        gate_s[:, cols] = mm(sig_gd, gup_ref[:, cols].astype(BF16))
        bonus_s[:, cols] = _head_sum(r * (0.5 * (kdirs[0] + kdirs[1])) * rk_ref[:, cols], ones) * v
        r_s[j] = r
        v_s[j] = v
        kk_s[j] = kk

    n_chunks = t_len // CHUNK
    chunks_per = min(n_chunks, UNITS_PER_STEP // 2)
    pairs_per = min(pairs, UNITS_PER_STEP // (2 * chunks_per))
    groups = n_chunks // chunks_per
    rs = lax.broadcasted_iota(jnp.int32, (STACK, STACK), 0)
    cs = lax.broadcasted_iota(jnp.int32, (STACK, STACK), 1)
    as_f32 = lambda m: jnp.where(m, 1.0, 0.0)
    blk = (as_f32(rs // CHUNK == cs // CHUNK), as_f32(cs == (rs + CHUNK) % STACK))
    consts = ((tri_ref[0], tri_ref[1]), (mask_ref[0], mask_ref[1]), (mask_ref[2], mask_ref[3]),
              rs == cs, lane_lo, blk)
    dirs = ((lw0_s, kd0_s, b0_s), (lw1_s, kd1_s, b1_s))

    def intra_body(it, carry):
        pg = it // groups
        g = it % groups
        units, ids = [], []
        for jj in range(pairs_per):
            j = pg * pairs_per + jj
            for cc in range(chunks_per):
                c = g * chunks_per + cc
                rows = pl.ds(pl.multiple_of(c * CHUNK, CHUNK), CHUNK)
                for e, (lw_s, kd_s, b_s) in enumerate(dirs):
                    units.append((r_s[j, rows, :], lw_s[j, rows, :], kd_s[j, rows, :], v_s[j, rows, :],
                                  kk_s[j, rows, :], b_s[j, rows, :], e == 1))
                    ids.append((j * 2 + e) * n_chunks + c)
        for uid, (lhs, add) in zip(ids, _wkv_intra(units, consts)):
            lhs_s[uid] = lhs
            add_s[uid] = add
        return carry

    lax.fori_loop(0, (pairs // pairs_per) * groups, intra_body, 0)

    for j in range(pairs):
        for e in range(2):
            if has_s0:
                st_s[2 * j + e] = s0_ref[0, e, j].T
            else:
                st_s[2 * j + e] = jnp.zeros((LANES, LANES), F32)

    def state_body(it, carry):
        chunk = (it, n_chunks - 1 - it)
        uids = [(j * 2 + e) * n_chunks + chunk[e] for j in range(pairs) for e in range(2)]
        sts = [st_s[ch].astype(BF16) for ch in range(2 * pairs)]
        res = [mm(lhs_s[uid], st) + add_s[uid] for uid, st in zip(uids, sts)]
        for ch, rr in enumerate(res):
            j, e = divmod(ch, 2)
            y_s = yb_s if e else yf_s
            y_s[j, pl.ds(pl.multiple_of(chunk[e] * CHUNK, CHUNK), CHUNK), :] = rr[:CHUNK] + rr[CHUNK:STACK]
            st_s[ch] = rr[STACK:]
        return carry

    lax.fori_loop(0, n_chunks, state_body, 0)
    for j in range(pairs):
        for e in range(2):
            st_t = st_s[2 * j + e].T
            sn_ref[0, e, 2 * j] = st_t[:HEAD_DIM, :HEAD_DIM]
            sn_ref[0, e, 2 * j + 1] = pltpu.roll(st_t, HEAD_DIM, 1)[HEAD_DIM:, :HEAD_DIM]

    inv_d = 1.0 / HEAD_DIM
    for j in range(pairs):
        cols = slice(j * LANES, (j + 1) * LANES)
        y = yf_s[j] + yb_s[j]
        mean = _head_sum(y, ones) * inv_d
        dlt = y - mean
        var = _head_sum(dlt * dlt, ones) * inv_d
        yn = dlt * lax.rsqrt(var + GN_EPS) * lng_ref[:, cols] + lnb_ref[:, cols]
        o_ref[0, :, cols] = (yn + bonus_s[:, cols]) * gate_s[:, cols]


def _rwkv_branch(u_rw, s0_big, p, pairs, cast=()):
    bsz, t_len, _ = u_rw.shape
    has_s0 = s0_big is not None
    width = pairs * LANES
    seg = RW_WIDTH // width
    tok = lambda off: pl.BlockSpec((1, t_len, width), lambda b, j: (b, 0, off + j))
    mu = lambda off: pl.BlockSpec((2, width), lambda b, j: (0, off + j))
    vec2 = pl.BlockSpec((2, width), lambda b, j: (0, j))
    vec1 = pl.BlockSpec((1, width), lambda b, j: (0, j))
    mat = pl.BlockSpec((LANES, width), lambda b, j: (0, j))
    lora_w = 3 * LANES
    lora_blk = 3 * RW_WIDTH // lora_w
    in_specs = [tok(0), tok(seg), tok(2 * seg),
                pl.BlockSpec((1, t_len, lora_w), lambda b, j: (b, 0, lora_blk)),
                mu(0), mu(seg), mu(2 * seg),
                pl.BlockSpec((2, lora_w), lambda b, j: (0, lora_blk)),
                vec2, vec2, mat, mat, mat, vec1, vec1, vec1, vec1, vec1,
                _const_spec((LANES, LANES)), _const_spec((2, CHUNK, CHUNK)), _const_spec((4, STACK, 2 * STACK))]
    args = [u_rw, u_rw, u_rw, u_rw, p["shift_mu"], p["shift_mu"], p["shift_mu"], p["shift_mu"],
            p["rw_w0"], p["rw_a0"],
            p["rw_w_up"].reshape(2 * LORA_DECAY, RW_WIDTH), p["rw_a_up"].reshape(2 * LORA_ICLR, RW_WIDTH),
            p["rw_g_up"], p["rw_k_k"].reshape(1, -1), p["rw_k_a"].reshape(1, -1),
            p["rw_r_k"].reshape(1, -1), p["rw_ln_g"].reshape(1, -1), p["rw_ln_b"].reshape(1, -1),
            *_wkv_constants()]
    st_spec = pl.BlockSpec((1, 2, pairs, LANES, LANES), lambda b, j: (b, 0, j, 0, 0))
    if has_s0:
        in_specs.append(st_spec)
        args.append(s0_big)
    out_specs = [pl.BlockSpec((1, t_len, width), lambda b, j: (b, 0, j)),
                 pl.BlockSpec((1, 2, 2 * pairs, HEAD_DIM, HEAD_DIM), lambda b, j: (b, 0, j, 0, 0))]
    out_shape = [jax.ShapeDtypeStruct((bsz, t_len, RW_WIDTH), F32),
                 jax.ShapeDtypeStruct((bsz, 2, RW_HEADS, HEAD_DIM, HEAD_DIM), F32)]
    for w in cast:
        assert PAIRS == pairs and w.shape[0] % (16 * bsz) == 0
        blk = pl.BlockSpec((w.shape[0] // bsz, w.shape[1]), lambda b, j: (b, 0))
        in_specs.append(blk)
        args.append(w)
        out_specs.append(blk)
        out_shape.append(jax.ShapeDtypeStruct(w.shape, BF16))
    n_units = 2 * pairs * (t_len // CHUNK)
    per_pair = pltpu.VMEM((pairs, t_len, LANES), F32)
    full = pltpu.VMEM((t_len, width), F32)
    scratch = [per_pair] * 9 + [full, full, per_pair, per_pair,
                                pltpu.VMEM((n_units, 2 * STACK, LANES), BF16),
                                pltpu.VMEM((n_units, 2 * STACK, LANES), F32),
                                pltpu.VMEM((2 * pairs, LANES, LANES), F32)]
    o_rw, s_new, *casted = pl.pallas_call(
        functools.partial(_rwkv_kernel, t_len=t_len, has_s0=has_s0, pairs=pairs, n_cast=len(cast)),
        grid=(bsz, PAIRS // pairs),
        in_specs=in_specs,
        out_specs=out_specs,
        out_shape=out_shape,
        scratch_shapes=scratch,
        compiler_params=_params(2),
        name="rwkv_branch",
    )(*args)
    return o_rw, s_new, casted


def _state_to_big(s0):
    bsz = s0.shape[0]
    x = s0.reshape(bsz, 2, PAIRS, 2, HEAD_DIM, HEAD_DIM)
    z = jnp.zeros_like(x[:, :, :, 0])
    top = jnp.concatenate([x[:, :, :, 0], z], axis=-1)
    bot = jnp.concatenate([z, x[:, :, :, 1]], axis=-1)
    return jnp.concatenate([top, bot], axis=-2)


def _qk_norm(t, g, ones):
    ms = _head_sum(t * t, ones) * (1.0 / HEAD_DIM)
    return t * lax.rsqrt(ms + RMS_EPS) * g


def _nt(x, y):
    return lax.dot_general(x, y, (((1,), (1,)), ((), ())), preferred_element_type=F32)


def _na_ctx_kernel(q_ref, k_ref, v_ref, qg_ref, kg_ref, o_ref, kn_ref, vc_ref):
    ones = _head_ones()
    lo_half = lax.broadcasted_iota(jnp.int32, (1, LANES), 1) < HEAD_DIM
    lo = jnp.where(lo_half, 1.0, 0.0)
    t_len = q_ref.shape[1]
    qs, ks, vs = [], [], []
    for j in range(NA_WIDTH // LANES):
        cols = slice(j * LANES, (j + 1) * LANES)
        qn = _qk_norm(q_ref[0, :, cols], qg_ref[...], ones)
        kn = _qk_norm(k_ref[0, :, cols], kg_ref[...], ones)
        v = v_ref[0, :, cols]
        kn_ref[0, :, cols] = kn
        vc_ref[0, :, cols] = v
        qn = qn * QK_SCALE
        qs.append(jnp.concatenate([qn * lo, qn * (1.0 - lo)], axis=0).astype(BF16))
        ks.append(kn.astype(BF16))
        vs.append(v.astype(BF16))
    logits = [_nt(q, k) for q, k in zip(qs, ks)]
    ms = [jnp.max(s, axis=-1, keepdims=True) for s in logits]
    ps = [jnp.exp(s - m) for s, m in zip(logits, ms)]
    ls = [jnp.sum(p, axis=-1, keepdims=True) for p in ps]
    outs = [jnp.dot(p.astype(BF16), v, preferred_element_type=F32) / l for p, v, l in zip(ps, vs, ls)]
    for j, o in enumerate(outs):
        o_ref[0, :, j * LANES:(j + 1) * LANES] = jnp.where(lo_half, o[:t_len], o[t_len:])


def _na_context(u_na, q_g, k_g):
    bsz, t_len, _ = u_na.shape
    tok = lambda seg: pl.BlockSpec((1, t_len, NA_WIDTH), lambda b: (b, 0, seg))
    out_blk = pl.BlockSpec((1, t_len, NA_WIDTH), lambda b: (b, 0, 0))
    g2 = lambda g: jnp.tile(g.reshape(1, HEAD_DIM), (1, 2))
    shp = jax.ShapeDtypeStruct((bsz, t_len, NA_WIDTH), F32)
    return pl.pallas_call(
        _na_ctx_kernel,
        grid=(bsz,),
        in_specs=[tok(0), tok(1), tok(2), _const_spec((1, LANES)), _const_spec((1, LANES))],
        out_specs=[out_blk, out_blk, out_blk],
        out_shape=[shp, shp, shp],
        compiler_params=_params(1),
        name="na_context",
    )(u_na, u_na, u_na, g2(q_g), g2(k_g))


NA_ROW_ILP = 8


def _na_lat_kernel(q_ref, k_ref, v_ref, kc_ref, vc_ref, qg_ref, kg_ref, tab_ref, o_ref,
                   q0_s, q1_s, kn_s, v_s, kc_s, vc_s, *, rows, kr):
    ones = _head_ones()
    lo_half = lax.broadcasted_iota(jnp.int32, (1, LANES), 1) < HEAD_DIM
    lo = jnp.where(lo_half, 1.0, 0.0)
    qn = _qk_norm(q_ref[0], qg_ref[...], ones) * QK_SCALE
    q0_s[...] = (qn * lo).astype(BF16)
    q1_s[...] = (qn * (1.0 - lo)).astype(BF16)
    kn_s[...] = _qk_norm(k_ref[0], kg_ref[...], ones).astype(BF16)
    v_s[...] = v_ref[0].astype(BF16)
    kc_s[...] = kc_ref[0].astype(BF16)
    vc_s[...] = vc_ref[0].astype(BF16)
    win = kr * GRID_W

    def body(it, carry):
        qs, k_rows, q_rows, biases = [], [], [], []
        for s in range(NA_ROW_ILP):
            i = it * NA_ROW_ILP + s
            r0 = jnp.clip(i - kr // 2, 0, rows - kr)
            d0 = r0 - i + (NA_ROWS - 1)
            qr = pl.ds(pl.multiple_of(i * GRID_W, GRID_W), GRID_W)
            q_rows.append(qr)
            k_rows.append(pl.ds(pl.multiple_of(r0 * GRID_W, GRID_W), win))
            qs.append(jnp.concatenate([q0_s[qr, :], q1_s[qr, :]], axis=0))
            biases.append(jnp.concatenate(
                [jnp.concatenate([tab_ref[h, d0 + 2 * m] for m in range(kr // 2)], axis=1) for h in range(2)],
                axis=0))
        lw = [_nt(q, kn_s[kr_, :]) + b for q, kr_, b in zip(qs, k_rows, biases)]
        lc = [_nt(q, kc_s[...]) for q in qs]
        ms = [jnp.maximum(jnp.max(a, axis=-1, keepdims=True), jnp.max(c, axis=-1, keepdims=True))
              for a, c in zip(lw, lc)]
        pw = [jnp.exp(a - m) for a, m in zip(lw, ms)]
        pc = [jnp.exp(c - m) for c, m in zip(lc, ms)]
        ls = [jnp.sum(a, axis=-1, keepdims=True) + jnp.sum(c, axis=-1, keepdims=True) for a, c in zip(pw, pc)]
        outs = [(jnp.dot(a.astype(BF16), v_s[kr_, :], preferred_element_type=F32)
                 + jnp.dot(c.astype(BF16), vc_s[...], preferred_element_type=F32)) / l
                for a, c, kr_, l in zip(pw, pc, k_rows, ls)]
        for qr, o in zip(q_rows, outs):
            o_ref[0, qr, :] = jnp.where(lo_half, o[:GRID_W], o[GRID_W:])
        return carry

    lax.fori_loop(0, rows // NA_ROW_ILP, body, 0)


def _latent_bias_table(rpb):
    qc = np.arange(GRID_W)[:, None]
    kc = np.arange(GRID_W)[None, :]
    ws = np.clip(qc - NA_COLS // 2, 0, GRID_W - NA_COLS)
    valid = (kc >= ws) & (kc < ws + NA_COLS)
    dc = np.clip(kc - qc, -(NA_COLS - 1), NA_COLS - 1) + NA_COLS - 1
    onehot = (dc[None] == np.arange(2 * NA_COLS - 1)[:, None, None]).astype(np.float32)
    cb = jnp.einsum("hdc,cqk->hdqk", rpb, jnp.asarray(onehot), precision=lax.Precision.HIGHEST)
    cb = jnp.where(valid[None, None], cb, NEG_INF)
    return jnp.concatenate([cb[:, :-1], cb[:, 1:]], axis=-1)


def _na_latent(u_na, k_ctx, v_ctx, q_g, k_g, rpb):
    bsz, t_len, _ = u_na.shape
    rows = t_len // GRID_W
    kr = min(NA_ROWS, rows)
    assert kr % 2 == 0 and rows % NA_ROW_ILP == 0
    ctx_len = k_ctx.shape[1]
    seg = NA_WIDTH // LANES
    tok = lambda off: pl.BlockSpec((1, t_len, LANES), lambda b, j: (b, 0, off + j))
    ctx = pl.BlockSpec((1, ctx_len, LANES), lambda b, j: (b, 0, j))
    g2 = lambda g: jnp.tile(g.reshape(1, HEAD_DIM), (1, 2))
    tab = _latent_bias_table(rpb)
    tok_s = pltpu.VMEM((t_len, LANES), BF16)
    ctx_s = pltpu.VMEM((ctx_len, LANES), BF16)
    return pl.pallas_call(
        functools.partial(_na_lat_kernel, rows=rows, kr=kr),
        grid=(bsz, seg),
        in_specs=[tok(0), tok(seg), tok(2 * seg), ctx, ctx,
                  _const_spec((1, LANES)), _const_spec((1, LANES)),
                  pl.BlockSpec((2, 2 * NA_ROWS - 2, GRID_W, 2 * GRID_W), lambda b, j: (j, 0, 0, 0))],
        out_specs=pl.BlockSpec((1, t_len, LANES), lambda b, j: (b, 0, j)),
        out_shape=jax.ShapeDtypeStruct((bsz, t_len, NA_WIDTH), F32),
        scratch_shapes=[tok_s, tok_s, tok_s, tok_s, ctx_s, ctx_s],
        compiler_params=_params(2),
        name="na_latent",
    )(u_na, u_na, u_na, k_ctx, v_ctx, g2(q_g), g2(k_g), tab)


def _out_ffn_kernel(x_ref, orw_ref, ona_ref, gt_ref, mod_ref, g_ref, wor_ref, won_ref, wout_ref,
                    w1_ref, w3_ref, w2_ref, y_ref):
    g_rw = _sigmoid(gt_ref[0, :, :D_MODEL].astype(F32))
    g_na = _sigmoid(gt_ref[0, :, D_MODEL:].astype(F32))
    merged = g_rw * _dot(orw_ref[0], wor_ref[...]) + g_na * _dot(ona_ref[0], won_ref[...])
    x1 = x_ref[0] + mod_ref[0, 2:3, :] * _dot(merged, wout_ref[...])
    h2 = _rms_rows(x1) * g_ref[...]
    h2 = (h2 * (1.0 + mod_ref[0, 4:5, :]) + mod_ref[0, 3:4, :]).astype(BF16)
    acc = jnp.zeros(x1.shape, F32)
    for c in range(FF_HIDDEN // FF_CHUNK):
        cols = slice(c * FF_CHUNK, (c + 1) * FF_CHUNK)
        a = jnp.dot(h2, w1_ref[:, cols], preferred_element_type=F32)
        b = jnp.dot(h2, w3_ref[:, cols], preferred_element_type=F32)
        hh = (a * _sigmoid(a) * b).astype(BF16)
        acc = acc + jnp.dot(hh, w2_ref[cols, :], preferred_element_type=F32)
    y_ref[0] = x1 + mod_ref[0, 5:6, :] * acc


def _out_ffn(x, o_rw, o_na, gates, mod_all, mod_off, mod_stride, norm_g, wb):
    bsz, t_len, _ = x.shape
    tm = TOKEN_TILE
    row = lambda b, i: (b, i, 0)
    weights = [wb[n] for n in LATE_WEIGHTS]
    return pl.pallas_call(
        _out_ffn_kernel,
        grid=(bsz, t_len // tm),
        in_specs=[pl.BlockSpec((1, tm, D_MODEL), row),
                  pl.BlockSpec((1, tm, RW_WIDTH), row),
                  pl.BlockSpec((1, tm, NA_WIDTH), row),
                  pl.BlockSpec((1, tm, GATE_COLS), row),
                  pl.BlockSpec((1, 6, D_MODEL), lambda b, i: (mod_off + mod_stride * b, 0, 0)),
                  _const_spec((1, D_MODEL))] + [_const_spec(w.shape) for w in weights],
        out_specs=pl.BlockSpec((1, tm, D_MODEL), row),
        out_shape=jax.ShapeDtypeStruct((bsz, t_len, D_MODEL), F32),
        compiler_params=_params(2),
        name="out_ffn",
    )(x, o_rw, o_na, gates, mod_all, norm_g.reshape(1, -1), *weights)


LATE_WEIGHTS = ("w_o_rwkv", "w_o_na", "w_out", "ffn_w1", "ffn_w3", "ffn_w2")


def _trunk(x, mod_all, mod_off, mod_stride, s0_big, ctx_kv, p, w_in_bf, late):
    shape = x.shape
    flat = (lambda t: t.reshape(1, -1, t.shape[-1])) if mod_stride == 0 else (lambda t: t)
    unflat = lambda t: t.reshape(shape[0], shape[1], t.shape[-1])
    u_rw, u_na, gates = _in_proj(flat(x), mod_all, mod_off, mod_stride, p["norm1_g"], w_in_bf)
    u_rw, u_na = unflat(u_rw), unflat(u_na)
    if ctx_kv is None:
        o_rw, s_new, casted = _rwkv_branch(u_rw, s0_big, p, RWKV_PAIRS_CTX, [late[n] for n in LATE_WEIGHTS])
        late = dict(zip(LATE_WEIGHTS, casted))
    else:
        o_rw, s_new, _ = _rwkv_branch(u_rw, s0_big, p, RWKV_PAIRS_LAT)
    if ctx_kv is None:
        o_na, k_new, v_new = _na_context(u_na, p["na_q_g"], p["na_k_g"])
    else:
        o_na = _na_latent(u_na, ctx_kv[0], ctx_kv[1], p["na_q_g"], p["na_k_g"], p["na_rpb"])
        k_new = v_new = None
    y = _out_ffn(flat(x), flat(o_rw), flat(o_na), gates, mod_all, mod_off, mod_stride, p["norm2_g"], late)
    return unflat(y), s_new, k_new, v_new, late


def kernel(x_prompt, x_sample, state_rwkv, cache_na_k, cache_na_v, c, c_ctx, norm1_g, norm2_g, w_ada, b_ada,
           w_in, shift_mu, rw_w0, rw_w_up, rw_a0, rw_a_up, rw_g_up, rw_k_k, rw_k_a, rw_r_k, rw_ln_g, rw_ln_b,
           na_q_g, na_k_g, na_rpb, w_o_rwkv, w_o_na, w_out, ffn_w1, ffn_w3, ffn_w2):
    depth = w_in.shape[0]
    bsz, seq = x_prompt.shape[:2]
    dec = x_sample.shape[0]
    n_vec = 8
    cvecs = jnp.concatenate([c_ctx[None, :], c, jnp.zeros((n_vec - 1 - dec, D_MODEL), F32)], axis=0)
    y_p, y_s = x_prompt, x_sample
    new_s, new_k, new_v = [], [], []
    for l in range(depth):
        p = dict(norm1_g=norm1_g[l], norm2_g=norm2_g[l], shift_mu=shift_mu[l], rw_w0=rw_w0[l],
                 rw_w_up=rw_w_up[l], rw_a0=rw_a0[l], rw_a_up=rw_a_up[l], rw_g_up=rw_g_up[l],
                 rw_k_k=rw_k_k[l], rw_k_a=rw_k_a[l], rw_r_k=rw_r_k[l], rw_ln_g=rw_ln_g[l],
                 rw_ln_b=rw_ln_b[l], na_q_g=na_q_g[l], na_k_g=na_k_g[l], na_rpb=na_rpb[l])
        late = dict(w_o_rwkv=w_o_rwkv[l], w_o_na=w_o_na[l], w_out=w_out[l], ffn_w1=ffn_w1[l], ffn_w3=ffn_w3[l],
                    ffn_w2=ffn_w2[l])
        mod_all, w_in_bf = _modulation(cvecs, w_ada[l], b_ada[l], w_in[l])
        mod_all = mod_all.reshape(n_vec, 6, D_MODEL)
        y_p, s_big, k_l, v_l, late = _trunk(y_p, mod_all, 0, 0, None, None, p, w_in_bf, late)
        new_s.append(s_big)
        new_k.append(k_l.reshape(bsz, seq, NA_HEADS, HEAD_DIM))
        new_v.append(v_l.reshape(bsz, seq, NA_HEADS, HEAD_DIM))
        ctx_k = cache_na_k[:, l].reshape(dec, -1, NA_WIDTH)
        ctx_v = cache_na_v[:, l].reshape(dec, -1, NA_WIDTH)
        y_s = _trunk(y_s, mod_all, 1, 1, _state_to_big(state_rwkv[:, l]), (ctx_k, ctx_v), p, w_in_bf, late)[0]
    return (y_p, y_s, jnp.stack(new_s, axis=1), jnp.stack(new_k, axis=1), jnp.stack(new_v, axis=1))
```

```python
import functools

import numpy as np
import jax
import jax.numpy as jnp
from jax import lax
from jax.experimental import pallas as pl
from jax.experimental.pallas import tpu as pltpu

D_MODEL = 1024
GRID_W = 64
HEAD_DIM = 64
RW_HEADS = 8
RW_WIDTH = RW_HEADS * HEAD_DIM
NA_HEADS = 8
NA_WIDTH = NA_HEADS * HEAD_DIM
LORA_DECAY = 64
LORA_ICLR = 64
LORA_GATE = 128
NA_ROWS = 8
NA_COLS = 16
FF_HIDDEN = 2816
RW_COLS = 3 * RW_WIDTH + 2 * LORA_DECAY + 2 * LORA_ICLR + LORA_GATE
NA_IN_COLS = 3 * NA_WIDTH
GATE_COLS = 2 * D_MODEL
RMS_EPS = 1e-6
GN_EPS = 64e-5
L2_EPS = 1e-12
NEG_INF = -1e30
DECAY_SCALE = float(np.exp(-0.5))
QK_SCALE = HEAD_DIM ** -0.5
assert QK_SCALE == 0.125

LANES = 128
PAIRS = RW_HEADS // 2
CHUNK = 64
STACK = 2 * CHUNK
RWKV_PAIRS_CTX = 4
RWKV_PAIRS_LAT = 2
UNITS_PER_STEP = 16
TOKEN_TILE = 512
FF_CHUNK = 256
VMEM_LIMIT = 56 * 1024 * 1024

F32 = jnp.float32
BF16 = jnp.bfloat16


def _dot(a, b):
    return jnp.dot(a.astype(BF16), b.astype(BF16), preferred_element_type=F32)


def _dot_nt(a, b):
    return lax.dot_general(a.astype(BF16), b.astype(BF16), (((1,), (1,)), ((), ())),
                           preferred_element_type=F32)


def _split2(x):
    hi = x.astype(BF16)
    lo = (x - hi.astype(F32)).astype(BF16)
    return hi, lo


def _dot_exact_lhs(a_exact, b):
    h, l = _split2(b)
    d = lambda x: jnp.dot(a_exact, x, preferred_element_type=F32)
    return d(h) + d(l)


def _head_ones():
    r = lax.broadcasted_iota(jnp.int32, (LANES, LANES), 0) // HEAD_DIM
    c = lax.broadcasted_iota(jnp.int32, (LANES, LANES), 1) // HEAD_DIM
    return jnp.where(r == c, 1.0, 0.0).astype(BF16)


def _head_sum(x, ones):
    return jnp.dot(x.astype(BF16), ones, preferred_element_type=F32)


def _sigmoid(x):
    return 0.5 * jnp.tanh(0.5 * x) + 0.5


def _rms_rows(x):
    return x * lax.rsqrt(jnp.mean(x * x, axis=-1, keepdims=True) + RMS_EPS)


def _const_spec(shape):
    nd = len(shape)
    return pl.BlockSpec(shape, lambda *_: (0,) * nd, pipeline_mode=pl.Buffered(1))


def _params(n_axes):
    return pltpu.CompilerParams(dimension_semantics=("arbitrary",) * n_axes,
                                vmem_limit_bytes=VMEM_LIMIT)


def _mod_kernel(c_ref, w_ref, b_ref, win_ref, o_ref, winb_ref):
    s = c_ref[...]
    s = s * _sigmoid(s)
    w = w_ref[...]
    for r in range(s.shape[1]):
        o_ref[r:r + 1, :] = jnp.sum(w * s[:, r:r + 1], axis=0, keepdims=True) + b_ref[...]
    winb_ref[...] = win_ref[...].astype(BF16)


def _modulation(cvecs, w_ada, b_ada, w_in):
    n = cvecs.shape[1]
    tn = 1536
    steps = 6 * D_MODEL // tn
    rows = w_in.shape[0] // steps
    assert rows % 16 == 0
    return pl.pallas_call(
        _mod_kernel,
        grid=(steps,),
        in_specs=[pl.BlockSpec((D_MODEL, n), lambda j: (0, 0)),
                  pl.BlockSpec((D_MODEL, tn), lambda j: (0, j)),
                  pl.BlockSpec((1, tn), lambda j: (0, j)),
                  pl.BlockSpec((rows, w_in.shape[1]), lambda j: (j, 0))],
        out_specs=[pl.BlockSpec((n, tn), lambda j: (0, j)),
                   pl.BlockSpec((rows, w_in.shape[1]), lambda j: (j, 0))],
        out_shape=[jax.ShapeDtypeStruct((n, 6 * D_MODEL), F32),
                   jax.ShapeDtypeStruct(w_in.shape, BF16)],
        compiler_params=_params(1),
        name="modulation",
    )(cvecs, w_ada, b_ada.reshape(1, -1), w_in)


def _pick_group(i, tiles_a, a_ref, b_ref):
    return jnp.where(i < tiles_a, a_ref[0], b_ref[0])


def _group_specs(tiles_a, tm, width):
    return [pl.BlockSpec((1, tm, width), lambda i: (jnp.minimum(i, tiles_a - 1), 0, 0)),
            pl.BlockSpec((1, tm, width), lambda i: (jnp.maximum(i - tiles_a, 0), 0, 0))]


def _mod_spec(tiles_a, tiles_per_b):
    return pl.BlockSpec((1, 6, D_MODEL),
                        lambda i: (jnp.where(i < tiles_a, 0, 1 + (i - tiles_a) // tiles_per_b), 0, 0))


def _inproj_kernel(xa_ref, xb_ref, mod_ref, g_ref, w_ref, urw_ref, una_ref, gt_ref, *, tiles_a):
    x = _pick_group(pl.program_id(0), tiles_a, xa_ref, xb_ref)
    h = _rms_rows(x) * g_ref[...]
    h = (h * (1.0 + mod_ref[0, 1:2, :]) + mod_ref[0, 0:1, :]).astype(BF16)
    d = lambda lo, hi: jnp.dot(h, w_ref[:, lo:hi], preferred_element_type=F32)
    urw_ref[...] = d(0, RW_COLS)
    una_ref[...] = d(RW_COLS, RW_COLS + NA_IN_COLS)
    gt_ref[...] = d(RW_COLS + NA_IN_COLS, RW_COLS + NA_IN_COLS + GATE_COLS).astype(BF16)


def _in_proj(xa, xb, tiles_per_b, mod_all, norm_g, w_in_bf):
    tm = TOKEN_TILE
    tiles_a, tiles = xa.shape[0], xa.shape[0] + xb.shape[0]
    row = lambda i: (i, 0)
    return pl.pallas_call(
        functools.partial(_inproj_kernel, tiles_a=tiles_a),
        grid=(tiles,),
        in_specs=_group_specs(tiles_a, tm, D_MODEL) + [_mod_spec(tiles_a, tiles_per_b),
                                                       _const_spec((1, D_MODEL)), _const_spec(w_in_bf.shape)],
        out_specs=[pl.BlockSpec((tm, RW_COLS), row),
                   pl.BlockSpec((tm, NA_IN_COLS), row),
                   pl.BlockSpec((tm, GATE_COLS), row)],
        out_shape=[jax.ShapeDtypeStruct((tiles * tm, RW_COLS), F32),
                   jax.ShapeDtypeStruct((tiles * tm, NA_IN_COLS), F32),
                   jax.ShapeDtypeStruct((tiles * tm, GATE_COLS), BF16)],
        compiler_params=_params(1),
        name="in_proj",
    )(xa, xb, mod_all, norm_g.reshape(1, -1), w_in_bf)


def _shift(x, mu):
    t_len = x.shape[0]
    row = lax.broadcasted_iota(jnp.int32, x.shape, 0)
    prev = jnp.where(row == 0, 0.0, pltpu.roll(x, 1, 0))
    nxt = jnp.where(row == t_len - 1, 0.0, pltpu.roll(x, t_len - 1, 0))
    return x + mu[0:1, :] * (prev - x) + mu[1:2, :] * (nxt - x)


def _stack_heads(x, lane_lo):
    return jnp.concatenate([x * lane_lo, x * (1.0 - lane_lo)], axis=0)


def _wkv_intra(units, consts):
    tri, mask_s, mask_i, eye, lane_lo, blk = consts
    stack = lambda z: _stack_heads(z, lane_lo)

    cums = [_dot_exact_lhs(tri[int(u[6])], u[1]) for u in units]

    prep = []
    for (r, lw, kd, v, kk, b, reverse), cum in zip(units, cums):
        mid_row = CHUNK // 2 if reverse else CHUNK // 2 - 1
        tot_row = 0 if reverse else CHUNK - 1
        a = -kk
        ex = cum - lw
        mid = cum[mid_row:mid_row + 1, :]
        tot = cum[tot_row:tot_row + 1, :]
        up = jnp.exp(cum - mid)
        dn = jnp.exp(mid - cum)
        tail = jnp.exp(tot - cum)
        prep.append(dict(
            at_m=stack(a * jnp.exp(ex - mid)).astype(BF16),
            rt_m=stack(r * up).astype(BF16),
            btkt=jnp.concatenate([stack(b * dn), stack(kd * dn)], axis=0).astype(BF16),
            a_e=stack(a * jnp.exp(ex)),
            r_e=stack(r * jnp.exp(cum)),
            bk_t=jnp.concatenate([stack(b * tail), stack(kd * tail)], axis=0).T.astype(BF16),
            vv=stack(v).astype(BF16),
            diag=jnp.where(eye, jnp.exp(tot), 0.0),
            rev=int(reverse)))

    ntd = lambda x, y: lax.dot_general(x, y, (((1,), (1,)), ((), ())), preferred_element_type=F32)
    mm = lambda x, y: jnp.dot(x, y, preferred_element_type=F32)
    top = [mask_s[p["rev"]] * ntd(p["at_m"], p["btkt"]) for p in prep]
    bot = [(mask_i[p["rev"]] * ntd(p["rt_m"], p["btkt"])).astype(BF16) for p in prep]

    diag_blk, swap_eye = blk
    off_blk = 1.0 - diag_blk
    both = [t[:, :LANES] + swap_eye for t in top]
    steps = CHUNK.bit_length() - 1
    diag_bf = diag_blk.astype(BF16)
    for j in range(steps):
        packed = [q.astype(BF16) for q in both]
        res = [mm(qb * diag_bf, qb) for qb in packed]
        both = [r + off_blk * q for r, q in zip(res, both)]
    ts = [pltpu.roll(q, HEAD_DIM, 1).astype(BF16) for q in both]
    x0 = [jnp.concatenate([p["a_e"], mm(t[:, LANES:].astype(BF16), p["vv"])], axis=1) for p, t in zip(prep, top)]
    xs = [mm(t, x.astype(BF16)) for x, t in zip(x0, ts)]

    out = []
    zeros = jnp.zeros((STACK, LANES), BF16)
    for p, x, bt in zip(prep, xs, bot):
        rhs = jnp.concatenate([x.astype(BF16), jnp.concatenate([zeros, p["vv"]], axis=1)], axis=0)
        lhs = jnp.concatenate([bt, p["bk_t"]], axis=0)
        res = mm(lhs, rhs)
        lhs2 = res[:, :LANES] + jnp.concatenate([p["r_e"], p["diag"]], axis=0)
        out.append((lhs2.astype(BF16), res[:, LANES:]))
    return out


def _wkv_constants():
    lane_head = np.arange(LANES) // HEAD_DIM
    ones = (lane_head[:, None] == lane_head[None, :]).astype(np.float32)
    t = np.arange(CHUNK)
    tri = np.stack([t[None, :] <= t[:, None], t[None, :] >= t[:, None]]).astype(np.float32)
    s = np.arange(STACK)
    same = (s[:, None] // CHUNK) == (s[None, :] // CHUNK)
    rs, cs = s[:, None], s[None, :]
    masks = np.stack([same & (cs < rs), same & (cs > rs), same & (cs <= rs), same & (cs >= rs)]).astype(np.float32)
    masks = np.concatenate([masks, masks], axis=-1)
    return jnp.asarray(ones, BF16), jnp.asarray(tri, BF16), jnp.asarray(masks, F32)


def _rwkv_kernel(*refs, t_len, has_s0, pairs, n_cast):
    (r_ref, k_ref, v_ref, lo_ref, mur_ref, muk_ref, muv_ref, mul_ref, w0_ref, a0_ref, wup_ref, aup_ref,
     gup_ref, kk_ref, ka_ref, rk_ref, lng_ref, lnb_ref, ones_ref, tri_ref, mask_ref) = refs[:21]
    pos = 21
    s0_ref = None
    if has_s0:
        s0_ref = refs[pos]
        pos += 1
    cast_in = refs[pos:pos + n_cast]
    pos += n_cast
    o_ref, sn_ref = refs[pos], refs[pos + 1]
    cast_out = refs[pos + 2:pos + 2 + n_cast]
    (r_s, v_s, kk_s, b0_s, b1_s, lw0_s, lw1_s, kd0_s, kd1_s, gate_s, bonus_s, yf_s, yb_s,
     lhs_s, add_s, st_s) = refs[pos + 2 + n_cast:]
    for w_in_ref, w_out_ref in zip(cast_in, cast_out):
        w_out_ref[...] = w_in_ref[...].astype(BF16)

    ones = ones_ref[...]
    lane = lax.broadcasted_iota(jnp.int32, (1, LANES), 1)
    lo_half = lane < HEAD_DIM
    lane_lo = jnp.where(lo_half, 1.0, 0.0)
    mm = lambda x, y: jnp.dot(x, y, preferred_element_type=F32)

    lo = _shift(lo_ref[0], mul_ref[...])
    wd = jnp.tanh(lo[:, 0:LANES])
    ad = lo[:, LANES:2 * LANES]
    sig_gd = _sigmoid(lo[:, 2 * LANES:3 * LANES]).astype(BF16)
    wd_split = [_split2(wd * m) for m in (lane_lo, 1.0 - lane_lo)]
    ad_bf = [(ad * m).astype(BF16) for m in (lane_lo, 1.0 - lane_lo)]
    for j in range(pairs):
        cols = slice(j * LANES, (j + 1) * LANES)
        r = _shift(r_ref[0, :, cols], mur_ref[:, cols])
        k = _shift(k_ref[0, :, cols], muk_ref[:, cols])
        v = _shift(v_ref[0, :, cols], muv_ref[:, cols])
        kk = k * kk_ref[:, cols]
        kk = kk * lax.rsqrt(_head_sum(kk * kk, ones) + L2_EPS)
        wup_h, wup_l = _split2(wup_ref[:, cols])
        aup = aup_ref[:, cols].astype(BF16)
        kdirs = []
        for e, (lw_s, kd_s, b_s) in enumerate(((lw0_s, kd0_s, b0_s), (lw1_s, kd1_s, b1_s))):
            wd_h, wd_l = wd_split[e]
            w_lin = w0_ref[e:e + 1, cols] + (mm(wd_h, wup_h) + mm(wd_l, wup_h) + mm(wd_h, wup_l))
            lw_s[j] = -DECAY_SCALE * _sigmoid(w_lin)
            iclr = _sigmoid(a0_ref[e:e + 1, cols] + mm(ad_bf[e], aup))
            kd = k * (1.0 + (iclr - 1.0) * ka_ref[:, cols])
            kd_s[j] = kd
            b_s[j] = kk * iclr
            kdirs.append(kd)
        gate_s[:, cols] = mm(sig_gd, gup_ref[:, cols].astype(BF16))
        bonus_s[:, cols] = _head_sum(r * (0.5 * (kdirs[0] + kdirs[1])) * rk_ref[:, cols], ones) * v
        r_s[j] = r
        v_s[j] = v
        kk_s[j] = kk

    n_chunks = t_len // CHUNK
    chunks_per = min(n_chunks, UNITS_PER_STEP // 2)
    pairs_per = min(pairs, UNITS_PER_STEP // (2 * chunks_per))
    groups = n_chunks // chunks_per
    rs = lax.broadcasted_iota(jnp.int32, (STACK, STACK), 0)
    cs = lax.broadcasted_iota(jnp.int32, (STACK, STACK), 1)
    as_f32 = lambda m: jnp.where(m, 1.0, 0.0)
    blk = (as_f32(rs // CHUNK == cs // CHUNK), as_f32(cs == (rs + CHUNK) % STACK))
    consts = ((tri_ref[0], tri_ref[1]), (mask_ref[0], mask_ref[1]), (mask_ref[2], mask_ref[3]),
              rs == cs, lane_lo, blk)
    dirs = ((lw0_s, kd0_s, b0_s), (lw1_s, kd1_s, b1_s))

    def intra_body(it, carry):
        pg = it // groups
        g = it % groups
        units, ids = [], []
        for jj in range(pairs_per):
            j = pg * pairs_per + jj
            for cc in range(chunks_per):
                c = g * chunks_per + cc
                rows = pl.ds(pl.multiple_of(c * CHUNK, CHUNK), CHUNK)
                for e, (lw_s, kd_s, b_s) in enumerate(dirs):
                    units.append((r_s[j, rows, :], lw_s[j, rows, :], kd_s[j, rows, :], v_s[j, rows, :],
                                  kk_s[j, rows, :], b_s[j, rows, :], e == 1))
                    ids.append((j * 2 + e) * n_chunks + c)
        for uid, (lhs, add) in zip(ids, _wkv_intra(units, consts)):
            lhs_s[uid] = lhs
            add_s[uid] = add
        return carry

    lax.fori_loop(0, (pairs // pairs_per) * groups, intra_body, 0)

    for j in range(pairs):
        for e in range(2):
            if has_s0:
                st_s[2 * j + e] = s0_ref[0, e, j].T
            else:
                st_s[2 * j + e] = jnp.zeros((LANES, LANES), F32)

    def state_body(it, carry):
        chunk = (it, n_chunks - 1 - it)
        uids = [(j * 2 + e) * n_chunks + chunk[e] for j in range(pairs) for e in range(2)]
        sts = [st_s[ch].astype(BF16) for ch in range(2 * pairs)]
        res = [mm(lhs_s[uid], st) + add_s[uid] for uid, st in zip(uids, sts)]
        for ch, rr in enumerate(res):
            j, e = divmod(ch, 2)
            y_s = yb_s if e else yf_s
            y_s[j, pl.ds(pl.multiple_of(chunk[e] * CHUNK, CHUNK), CHUNK), :] = rr[:CHUNK] + rr[CHUNK:STACK]
            st_s[ch] = rr[STACK:]
        return carry

    lax.fori_loop(0, n_chunks, state_body, 0)
    for j in range(pairs):
        for e in range(2):
            st_t = st_s[2 * j + e].T
            sn_ref[0, e, 2 * j] = st_t[:HEAD_DIM, :HEAD_DIM]
            sn_ref[0, e, 2 * j + 1] = pltpu.roll(st_t, HEAD_DIM, 1)[HEAD_DIM:, :HEAD_DIM]

    inv_d = 1.0 / HEAD_DIM
    for j in range(pairs):
        cols = slice(j * LANES, (j + 1) * LANES)
        y = yf_s[j] + yb_s[j]
        mean = _head_sum(y, ones) * inv_d
        dlt = y - mean
        var = _head_sum(dlt * dlt, ones) * inv_d
        yn = dlt * lax.rsqrt(var + GN_EPS) * lng_ref[:, cols] + lnb_ref[:, cols]
        o_ref[0, :, cols] = (yn + bonus_s[:, cols]) * gate_s[:, cols]


def _rwkv_branch(u_rw, bsz, b_off, s0_big, p, pairs, cast=()):
    t_len = u_rw.shape[1]
    has_s0 = s0_big is not None
    width = pairs * LANES
    seg = RW_WIDTH // width
    tok = lambda off: pl.BlockSpec((1, t_len, width), lambda b, j: (b_off + b, 0, off + j))
    mu = lambda off: pl.BlockSpec((2, width), lambda b, j: (0, off + j))
    vec2 = pl.BlockSpec((2, width), lambda b, j: (0, j))
    vec1 = pl.BlockSpec((1, width), lambda b, j: (0, j))
    mat = pl.BlockSpec((LANES, width), lambda b, j: (0, j))
    lora_w = 3 * LANES
    lora_blk = 3 * RW_WIDTH // lora_w
    in_specs = [tok(0), tok(seg), tok(2 * seg),
                pl.BlockSpec((1, t_len, lora_w), lambda b, j: (b_off + b, 0, lora_blk)),
                mu(0), mu(seg), mu(2 * seg),
                pl.BlockSpec((2, lora_w), lambda b, j: (0, lora_blk)),
                vec2, vec2, mat, mat, mat, vec1, vec1, vec1, vec1, vec1,
                _const_spec((LANES, LANES)), _const_spec((2, CHUNK, CHUNK)), _const_spec((4, STACK, 2 * STACK))]
    args = [u_rw, u_rw, u_rw, u_rw, p["shift_mu"], p["shift_mu"], p["shift_mu"], p["shift_mu"],
            p["rw_w0"], p["rw_a0"],
            p["rw_w_up"].reshape(2 * LORA_DECAY, RW_WIDTH), p["rw_a_up"].reshape(2 * LORA_ICLR, RW_WIDTH),
            p["rw_g_up"], p["rw_k_k"].reshape(1, -1), p["rw_k_a"].reshape(1, -1),
            p["rw_r_k"].reshape(1, -1), p["rw_ln_g"].reshape(1, -1), p["rw_ln_b"].reshape(1, -1),
            *_wkv_constants()]
    st_spec = pl.BlockSpec((1, 2, pairs, LANES, LANES), lambda b, j: (b, 0, j, 0, 0))
    if has_s0:
        in_specs.append(st_spec)
        args.append(s0_big)
    out_specs = [pl.BlockSpec((1, t_len, width), lambda b, j: (b, 0, j)),
                 pl.BlockSpec((1, 2, 2 * pairs, HEAD_DIM, HEAD_DIM), lambda b, j: (b, 0, j, 0, 0))]
    out_shape = [jax.ShapeDtypeStruct((bsz, t_len, RW_WIDTH), F32),
                 jax.ShapeDtypeStruct((bsz, 2, RW_HEADS, HEAD_DIM, HEAD_DIM), F32)]
    for w in cast:
        assert PAIRS == pairs and w.shape[0] % (16 * bsz) == 0
        blk = pl.BlockSpec((w.shape[0] // bsz, w.shape[1]), lambda b, j: (b, 0))
        in_specs.append(blk)
        args.append(w)
        out_specs.append(blk)
        out_shape.append(jax.ShapeDtypeStruct(w.shape, BF16))
    n_units = 2 * pairs * (t_len // CHUNK)
    per_pair = pltpu.VMEM((pairs, t_len, LANES), F32)
    full = pltpu.VMEM((t_len, width), F32)
    scratch = [per_pair] * 9 + [full, full, per_pair, per_pair,
                                pltpu.VMEM((n_units, 2 * STACK, LANES), BF16),
                                pltpu.VMEM((n_units, 2 * STACK, LANES), F32),
                                pltpu.VMEM((2 * pairs, LANES, LANES), F32)]
    o_rw, s_new, *casted = pl.pallas_call(
        functools.partial(_rwkv_kernel, t_len=t_len, has_s0=has_s0, pairs=pairs, n_cast=len(cast)),
        grid=(bsz, PAIRS // pairs),
        in_specs=in_specs,
        out_specs=out_specs,
        out_shape=out_shape,
        scratch_shapes=scratch,
        compiler_params=_params(2),
        name="rwkv_branch",
    )(*args)
    return o_rw, s_new, casted


def _state_to_big(s0):
    bsz = s0.shape[0]
    x = s0.reshape(bsz, 2, PAIRS, 2, HEAD_DIM, HEAD_DIM)
    z = jnp.zeros_like(x[:, :, :, 0])
    top = jnp.concatenate([x[:, :, :, 0], z], axis=-1)
    bot = jnp.concatenate([z, x[:, :, :, 1]], axis=-1)
    return jnp.concatenate([top, bot], axis=-2)


def _qk_norm(t, g, ones):
    ms = _head_sum(t * t, ones) * (1.0 / HEAD_DIM)
    return t * lax.rsqrt(ms + RMS_EPS) * g


def _nt(x, y):
    return lax.dot_general(x, y, (((1,), (1,)), ((), ())), preferred_element_type=F32)


def _na_ctx_kernel(q_ref, k_ref, v_ref, qg_ref, kg_ref, o_ref, kn_ref, vc_ref):
    ones = _head_ones()
    lo_half = lax.broadcasted_iota(jnp.int32, (1, LANES), 1) < HEAD_DIM
    lo = jnp.where(lo_half, 1.0, 0.0)
    t_len = q_ref.shape[1]
    qs, ks, vs = [], [], []
    for j in range(NA_WIDTH // LANES):
        cols = slice(j * LANES, (j + 1) * LANES)
        qn = _qk_norm(q_ref[0, :, cols], qg_ref[...], ones)
        kn = _qk_norm(k_ref[0, :, cols], kg_ref[...], ones)
        v = v_ref[0, :, cols]
        kn_ref[0, :, cols] = kn
        vc_ref[0, :, cols] = v
        qn = qn * QK_SCALE
        qs.append(jnp.concatenate([qn * lo, qn * (1.0 - lo)], axis=0).astype(BF16))
        ks.append(kn.astype(BF16))
        vs.append(v.astype(BF16))
    logits = [_nt(q, k) for q, k in zip(qs, ks)]
    ms = [jnp.max(s, axis=-1, keepdims=True) for s in logits]
    ps = [jnp.exp(s - m) for s, m in zip(logits, ms)]
    ls = [jnp.sum(p, axis=-1, keepdims=True) for p in ps]
    outs = [jnp.dot(p.astype(BF16), v, preferred_element_type=F32) / l for p, v, l in zip(ps, vs, ls)]
    for j, o in enumerate(outs):
        o_ref[0, :, j * LANES:(j + 1) * LANES] = jnp.where(lo_half, o[:t_len], o[t_len:])


def _na_context(u_na, bsz, q_g, k_g):
    t_len = u_na.shape[1]
    tok = lambda seg: pl.BlockSpec((1, t_len, NA_WIDTH), lambda b: (b, 0, seg))
    out_blk = pl.BlockSpec((1, t_len, NA_WIDTH), lambda b: (b, 0, 0))
    g2 = lambda g: jnp.tile(g.reshape(1, HEAD_DIM), (1, 2))
    shp = jax.ShapeDtypeStruct((bsz, t_len, NA_WIDTH), F32)
    return pl.pallas_call(
        _na_ctx_kernel,
        grid=(bsz,),
        in_specs=[tok(0), tok(1), tok(2), _const_spec((1, LANES)), _const_spec((1, LANES))],
        out_specs=[out_blk, out_blk, out_blk],
        out_shape=[shp, shp, shp],
        compiler_params=_params(1),
        name="na_context",
    )(u_na, u_na, u_na, g2(q_g), g2(k_g))


NA_ROW_ILP = 8


def _na_lat_kernel(q_ref, k_ref, v_ref, kc_ref, vc_ref, qg_ref, kg_ref, tab_ref, o_ref,
                   q0_s, q1_s, kn_s, v_s, kc_s, vc_s, *, rows, kr):
    ones = _head_ones()
    lo_half = lax.broadcasted_iota(jnp.int32, (1, LANES), 1) < HEAD_DIM
    lo = jnp.where(lo_half, 1.0, 0.0)
    qn = _qk_norm(q_ref[0], qg_ref[...], ones) * QK_SCALE
    q0_s[...] = (qn * lo).astype(BF16)
    q1_s[...] = (qn * (1.0 - lo)).astype(BF16)
    kn_s[...] = _qk_norm(k_ref[0], kg_ref[...], ones).astype(BF16)
    v_s[...] = v_ref[0].astype(BF16)
    kc_s[...] = kc_ref[0].astype(BF16)
    vc_s[...] = vc_ref[0].astype(BF16)
    win = kr * GRID_W

    def body(it, carry):
        qs, k_rows, q_rows, biases = [], [], [], []
        for s in range(NA_ROW_ILP):
            i = it * NA_ROW_ILP + s
            r0 = jnp.clip(i - kr // 2, 0, rows - kr)
            d0 = r0 - i + (NA_ROWS - 1)
            qr = pl.ds(pl.multiple_of(i * GRID_W, GRID_W), GRID_W)
            q_rows.append(qr)
            k_rows.append(pl.ds(pl.multiple_of(r0 * GRID_W, GRID_W), win))
            qs.append(jnp.concatenate([q0_s[qr, :], q1_s[qr, :]], axis=0))
            biases.append(jnp.concatenate(
                [jnp.concatenate([tab_ref[h, d0 + 2 * m] for m in range(kr // 2)], axis=1) for h in range(2)],
                axis=0))
        lw = [_nt(q, kn_s[kr_, :]) + b for q, kr_, b in zip(qs, k_rows, biases)]
        lc = [_nt(q, kc_s[...]) for q in qs]
        ms = [jnp.maximum(jnp.max(a, axis=-1, keepdims=True), jnp.max(c, axis=-1, keepdims=True))
              for a, c in zip(lw, lc)]
        pw = [jnp.exp(a - m) for a, m in zip(lw, ms)]
        pc = [jnp.exp(c - m) for c, m in zip(lc, ms)]
        ls = [jnp.sum(a, axis=-1, keepdims=True) + jnp.sum(c, axis=-1, keepdims=True) for a, c in zip(pw, pc)]
        outs = [(jnp.dot(a.astype(BF16), v_s[kr_, :], preferred_element_type=F32)
                 + jnp.dot(c.astype(BF16), vc_s[...], preferred_element_type=F32)) / l
                for a, c, kr_, l in zip(pw, pc, k_rows, ls)]
        for qr, o in zip(q_rows, outs):
            o_ref[0, qr, :] = jnp.where(lo_half, o[:GRID_W], o[GRID_W:])
        return carry

    lax.fori_loop(0, rows // NA_ROW_ILP, body, 0)


def _latent_bias_table(rpb):
    qc = np.arange(GRID_W)[:, None]
    kc = np.arange(GRID_W)[None, :]
    ws = np.clip(qc - NA_COLS // 2, 0, GRID_W - NA_COLS)
    valid = (kc >= ws) & (kc < ws + NA_COLS)
    dc = np.clip(kc - qc, -(NA_COLS - 1), NA_COLS - 1) + NA_COLS - 1
    onehot = (dc[None] == np.arange(2 * NA_COLS - 1)[:, None, None]).astype(np.float32)
    cb = jnp.einsum("hdc,cqk->hdqk", rpb, jnp.asarray(onehot), precision=lax.Precision.HIGHEST)
    cb = jnp.where(valid[None, None], cb, NEG_INF)
    return jnp.concatenate([cb[:, :-1], cb[:, 1:]], axis=-1)


def _na_latent(u_na, bsz, b_off, k_ctx, v_ctx, q_g, k_g, rpb):
    t_len = u_na.shape[1]
    rows = t_len // GRID_W
    kr = min(NA_ROWS, rows)
    assert kr % 2 == 0 and rows % NA_ROW_ILP == 0
    ctx_len = k_ctx.shape[1]
    seg = NA_WIDTH // LANES
    tok = lambda off: pl.BlockSpec((1, t_len, LANES), lambda b, j: (b_off + b, 0, off + j))
    ctx = pl.BlockSpec((1, ctx_len, LANES), lambda b, j: (b, 0, j))
    g2 = lambda g: jnp.tile(g.reshape(1, HEAD_DIM), (1, 2))
    tab = _latent_bias_table(rpb)
    tok_s = pltpu.VMEM((t_len, LANES), BF16)
    ctx_s = pltpu.VMEM((ctx_len, LANES), BF16)
    return pl.pallas_call(
        functools.partial(_na_lat_kernel, rows=rows, kr=kr),
        grid=(bsz, seg),
        in_specs=[tok(0), tok(seg), tok(2 * seg), ctx, ctx,
                  _const_spec((1, LANES)), _const_spec((1, LANES)),
                  pl.BlockSpec((2, 2 * NA_ROWS - 2, GRID_W, 2 * GRID_W), lambda b, j: (j, 0, 0, 0))],
        out_specs=pl.BlockSpec((1, t_len, LANES), lambda b, j: (b, 0, j)),
        out_shape=jax.ShapeDtypeStruct((bsz, t_len, NA_WIDTH), F32),
        scratch_shapes=[tok_s, tok_s, tok_s, tok_s, ctx_s, ctx_s],
        compiler_params=_params(2),
        name="na_latent",
    )(u_na, u_na, u_na, k_ctx, v_ctx, g2(q_g), g2(k_g), tab)


def _out_ffn_kernel(xa_ref, xb_ref, orwa_ref, orwb_ref, onaa_ref, onab_ref, gt_ref, mod_ref, g_ref,
                    wor_ref, won_ref, wout_ref, w1_ref, w3_ref, w2_ref, ya_ref, yb_ref, *, tiles_a):
    i = pl.program_id(0)
    x = _pick_group(i, tiles_a, xa_ref, xb_ref)
    o_rw = _pick_group(i, tiles_a, orwa_ref, orwb_ref)
    o_na = _pick_group(i, tiles_a, onaa_ref, onab_ref)
    g_rw = _sigmoid(gt_ref[:, :D_MODEL].astype(F32))
    g_na = _sigmoid(gt_ref[:, D_MODEL:].astype(F32))
    merged = g_rw * _dot(o_rw, wor_ref[...]) + g_na * _dot(o_na, won_ref[...])
    x1 = x + mod_ref[0, 2:3, :] * _dot(merged, wout_ref[...])
    h2 = _rms_rows(x1) * g_ref[...]
    h2 = (h2 * (1.0 + mod_ref[0, 4:5, :]) + mod_ref[0, 3:4, :]).astype(BF16)
    acc = jnp.zeros(x1.shape, F32)
    for c in range(FF_HIDDEN // FF_CHUNK):
        cols = slice(c * FF_CHUNK, (c + 1) * FF_CHUNK)
        a = jnp.dot(h2, w1_ref[:, cols], preferred_element_type=F32)
        b = jnp.dot(h2, w3_ref[:, cols], preferred_element_type=F32)
        hh = (a * _sigmoid(a) * b).astype(BF16)
        acc = acc + jnp.dot(hh, w2_ref[cols, :], preferred_element_type=F32)
    y = x1 + mod_ref[0, 5:6, :] * acc

    @pl.when(i < tiles_a)
    def _():
        ya_ref[0] = y

    @pl.when(i >= tiles_a)
    def _():
        yb_ref[0] = y


def _out_ffn(xa, xb, orw_a, orw_b, ona_a, ona_b, gates, tiles_per_b, mod_all, norm_g, wb):
    tm = TOKEN_TILE
    tiles_a, tiles = xa.shape[0], xa.shape[0] + xb.shape[0]
    weights = [wb[n] for n in LATE_WEIGHTS]
    groups = lambda width: _group_specs(tiles_a, tm, width)
    return pl.pallas_call(
        functools.partial(_out_ffn_kernel, tiles_a=tiles_a),
        grid=(tiles,),
        in_specs=groups(D_MODEL) + groups(RW_WIDTH) + groups(NA_WIDTH)
        + [pl.BlockSpec((tm, GATE_COLS), lambda i: (i, 0)), _mod_spec(tiles_a, tiles_per_b),
           _const_spec((1, D_MODEL))] + [_const_spec(w.shape) for w in weights],
        out_specs=groups(D_MODEL),
        out_shape=[jax.ShapeDtypeStruct(xa.shape, F32), jax.ShapeDtypeStruct(xb.shape, F32)],
        compiler_params=_params(1),
        name="out_ffn",
    )(xa, xb, orw_a, orw_b, ona_a, ona_b, gates, mod_all, norm_g.reshape(1, -1), *weights)


LATE_WEIGHTS = ("w_o_rwkv", "w_o_na", "w_out", "ffn_w1", "ffn_w3", "ffn_w2")


def kernel(x_prompt, x_sample, state_rwkv, cache_na_k, cache_na_v, c, c_ctx, norm1_g, norm2_g, w_ada, b_ada,
           w_in, shift_mu, rw_w0, rw_w_up, rw_a0, rw_a_up, rw_g_up, rw_k_k, rw_k_a, rw_r_k, rw_ln_g, rw_ln_b,
           na_q_g, na_k_g, na_rpb, w_o_rwkv, w_o_na, w_out, ffn_w1, ffn_w3, ffn_w2):
    depth = w_in.shape[0]
    bsz, seq = x_prompt.shape[:2]
    dec, dec_seq = x_sample.shape[:2]
    tm = TOKEN_TILE
    n_ctx = bsz * seq
    assert n_ctx % tm == 0 and dec_seq % tm == 0 and n_ctx % dec_seq == 0
    tiles_per_b = dec_seq // tm
    tiled = lambda t: t.reshape(-1, tm, t.shape[-1])
    cvecs = jnp.concatenate([c_ctx[None, :], c], axis=0).T
    y_p, y_s = x_prompt, x_sample
    new_s, new_k, new_v = [], [], []
    for l in range(depth):
        p = dict(norm1_g=norm1_g[l], norm2_g=norm2_g[l], shift_mu=shift_mu[l], rw_w0=rw_w0[l],
                 rw_w_up=rw_w_up[l], rw_a0=rw_a0[l], rw_a_up=rw_a_up[l], rw_g_up=rw_g_up[l],
                 rw_k_k=rw_k_k[l], rw_k_a=rw_k_a[l], rw_r_k=rw_r_k[l], rw_ln_g=rw_ln_g[l],
                 rw_ln_b=rw_ln_b[l], na_q_g=na_q_g[l], na_k_g=na_k_g[l], na_rpb=na_rpb[l])
        late_f32 = [w[l] for w in (w_o_rwkv, w_o_na, w_out, ffn_w1, ffn_w3, ffn_w2)]
        mod_all, w_in_bf = _modulation(cvecs, w_ada[l], b_ada[l], w_in[l])
        mod_all = mod_all.reshape(1 + dec, 6, D_MODEL)
        u_rw, u_na, gates = _in_proj(tiled(y_p), tiled(y_s), tiles_per_b, mod_all, p["norm1_g"], w_in_bf)
        ctx_view = lambda t: t.reshape(-1, seq, t.shape[-1])
        lat_view = lambda t: t.reshape(-1, dec_seq, t.shape[-1])
        lat_off = n_ctx // dec_seq
        o_rw_p, s_l, casted = _rwkv_branch(ctx_view(u_rw), bsz, 0, None, p, RWKV_PAIRS_CTX, late_f32)
        late = dict(zip(LATE_WEIGHTS, casted))
        o_na_p, k_l, v_l = _na_context(ctx_view(u_na), bsz, p["na_q_g"], p["na_k_g"])
        new_s.append(s_l)
        new_k.append(k_l.reshape(bsz, seq, NA_HEADS, HEAD_DIM))
        new_v.append(v_l.reshape(bsz, seq, NA_HEADS, HEAD_DIM))
        ctx_k = cache_na_k[:, l].reshape(dec, -1, NA_WIDTH)
        ctx_v = cache_na_v[:, l].reshape(dec, -1, NA_WIDTH)
        o_rw_s, _, _ = _rwkv_branch(lat_view(u_rw), dec, lat_off, _state_to_big(state_rwkv[:, l]), p, RWKV_PAIRS_LAT)
        o_na_s = _na_latent(lat_view(u_na), dec, lat_off, ctx_k, ctx_v, p["na_q_g"], p["na_k_g"], p["na_rpb"])
        y_p_t, y_s_t = _out_ffn(tiled(y_p), tiled(y_s), tiled(o_rw_p), tiled(o_rw_s), tiled(o_na_p), tiled(o_na_s),
                                gates, tiles_per_b, mod_all, p["norm2_g"], late)
        y_p, y_s = y_p_t.reshape(x_prompt.shape), y_s_t.reshape(x_sample.shape)
    return (y_p, y_s, jnp.stack(new_s, axis=1), jnp.stack(new_k, axis=1), jnp.stack(new_v, axis=1))
```

```python
import functools

import numpy as np
import jax
import jax.numpy as jnp
from jax import lax
from jax.experimental import pallas as pl
from jax.experimental.pallas import tpu as pltpu

D_MODEL = 1024
GRID_W = 64
HEAD_DIM = 64
RW_HEADS = 8
RW_WIDTH = RW_HEADS * HEAD_DIM
NA_HEADS = 8
NA_WIDTH = NA_HEADS * HEAD_DIM
LORA_DECAY = 64
LORA_ICLR = 64
LORA_GATE = 128
NA_ROWS = 8
NA_COLS = 16
FF_HIDDEN = 2816
RW_COLS = 3 * RW_WIDTH + 2 * LORA_DECAY + 2 * LORA_ICLR + LORA_GATE
NA_IN_COLS = 3 * NA_WIDTH
GATE_COLS = 2 * D_MODEL
RMS_EPS = 1e-6
GN_EPS = 64e-5
L2_EPS = 1e-12
NEG_INF = -1e30
DECAY_SCALE = float(np.exp(-0.5))
QK_SCALE = HEAD_DIM ** -0.5
assert QK_SCALE == 0.125

LANES = 128
PAIRS = RW_HEADS // 2
CHUNK = 64
STACK = 2 * CHUNK
RWKV_PAIRS_CTX = 4
RWKV_PAIRS_LAT = 2
RWKV_UNITS_CTX = 16
RWKV_UNITS_LAT = 16
TOKEN_TILE = 512
FF_CHUNK = 256
VMEM_LIMIT = 56 * 1024 * 1024

F32 = jnp.float32
BF16 = jnp.bfloat16


def _dot(a, b):
    return jnp.dot(a.astype(BF16), b.astype(BF16), preferred_element_type=F32)


def _dot_nt(a, b):
    return lax.dot_general(a.astype(BF16), b.astype(BF16), (((1,), (1,)), ((), ())),
                           preferred_element_type=F32)


def _split2(x):
    hi = x.astype(BF16)
    lo = (x - hi.astype(F32)).astype(BF16)
    return hi, lo


def _dot_exact_lhs(a_exact, b):
    h, l = _split2(b)
    d = lambda x: jnp.dot(a_exact, x, preferred_element_type=F32)
    return d(h) + d(l)


def _head_ones():
    r = lax.broadcasted_iota(jnp.int32, (LANES, LANES), 0) // HEAD_DIM
    c = lax.broadcasted_iota(jnp.int32, (LANES, LANES), 1) // HEAD_DIM
    return jnp.where(r == c, 1.0, 0.0).astype(BF16)


def _head_sum(x, ones):
    return jnp.dot(x.astype(BF16), ones, preferred_element_type=F32)


def _sigmoid(x):
    return 0.5 * jnp.tanh(0.5 * x) + 0.5


def _rms_rows(x):
    return x * lax.rsqrt(jnp.mean(x * x, axis=-1, keepdims=True) + RMS_EPS)


def _const_spec(shape):
    nd = len(shape)
    return pl.BlockSpec(shape, lambda *_: (0,) * nd, pipeline_mode=pl.Buffered(1))


def _params(n_axes):
    return pltpu.CompilerParams(dimension_semantics=("arbitrary",) * n_axes,
                                vmem_limit_bytes=VMEM_LIMIT)


def _mod_kernel(c_ref, w_ref, b_ref, win_ref, o_ref, winb_ref):
    s = c_ref[...]
    s = s * _sigmoid(s)
    w = w_ref[...]
    for r in range(s.shape[1]):
        o_ref[r:r + 1, :] = jnp.sum(w * s[:, r:r + 1], axis=0, keepdims=True) + b_ref[...]
    winb_ref[...] = win_ref[...].astype(BF16)


def _modulation(cvecs, w_ada, b_ada, w_in):
    n = cvecs.shape[1]
    tn = 768
    steps = 6 * D_MODEL // tn
    rows = w_in.shape[0] // steps
    assert rows % 16 == 0
    return pl.pallas_call(
        _mod_kernel,
        grid=(steps,),
        in_specs=[pl.BlockSpec((D_MODEL, n), lambda j: (0, 0)),
                  pl.BlockSpec((D_MODEL, tn), lambda j: (0, j)),
                  pl.BlockSpec((1, tn), lambda j: (0, j)),
                  pl.BlockSpec((rows, w_in.shape[1]), lambda j: (j, 0))],
        out_specs=[pl.BlockSpec((n, tn), lambda j: (0, j)),
                   pl.BlockSpec((rows, w_in.shape[1]), lambda j: (j, 0))],
        out_shape=[jax.ShapeDtypeStruct((n, 6 * D_MODEL), F32),
                   jax.ShapeDtypeStruct(w_in.shape, BF16)],
        compiler_params=_params(1),
        name="modulation",
    )(cvecs, w_ada, b_ada.reshape(1, -1), w_in)


def _pick_group(i, tiles_a, a_ref, b_ref):
    return jnp.where(i < tiles_a, a_ref[0], b_ref[0])


def _group_specs(tiles_a, tm, width):
    return [pl.BlockSpec((1, tm, width), lambda i: (jnp.minimum(i, tiles_a - 1), 0, 0)),
            pl.BlockSpec((1, tm, width), lambda i: (jnp.maximum(i - tiles_a, 0), 0, 0))]


def _mod_spec(tiles_a, tiles_per_b):
    return pl.BlockSpec((1, 6, D_MODEL),
                        lambda i: (jnp.where(i < tiles_a, 0, 1 + (i - tiles_a) // tiles_per_b), 0, 0))


def _inproj_kernel(xa_ref, xb_ref, mod_ref, g_ref, w_ref, urw_ref, una_ref, gt_ref, *, tiles_a):
    x = _pick_group(pl.program_id(0), tiles_a, xa_ref, xb_ref)
    h = _rms_rows(x) * g_ref[...]
    h = (h * (1.0 + mod_ref[0, 1:2, :]) + mod_ref[0, 0:1, :]).astype(BF16)
    d = lambda lo, hi: jnp.dot(h, w_ref[:, lo:hi], preferred_element_type=F32)
    urw_ref[...] = d(0, RW_COLS)
    una_ref[...] = d(RW_COLS, RW_COLS + NA_IN_COLS)
    gt_ref[...] = d(RW_COLS + NA_IN_COLS, RW_COLS + NA_IN_COLS + GATE_COLS).astype(BF16)


def _in_proj(xa, xb, tiles_per_b, mod_all, norm_g, w_in_bf):
    tm = TOKEN_TILE
    tiles_a, tiles = xa.shape[0], xa.shape[0] + xb.shape[0]
    row = lambda i: (i, 0)
    return pl.pallas_call(
        functools.partial(_inproj_kernel, tiles_a=tiles_a),
        grid=(tiles,),
        in_specs=_group_specs(tiles_a, tm, D_MODEL) + [_mod_spec(tiles_a, tiles_per_b),
                                                       _const_spec((1, D_MODEL)), _const_spec(w_in_bf.shape)],
        out_specs=[pl.BlockSpec((tm, RW_COLS), row),
                   pl.BlockSpec((tm, NA_IN_COLS), row),
                   pl.BlockSpec((tm, GATE_COLS), row)],
        out_shape=[jax.ShapeDtypeStruct((tiles * tm, RW_COLS), F32),
                   jax.ShapeDtypeStruct((tiles * tm, NA_IN_COLS), F32),
                   jax.ShapeDtypeStruct((tiles * tm, GATE_COLS), BF16)],
        compiler_params=_params(1),
        name="in_proj",
    )(xa, xb, mod_all, norm_g.reshape(1, -1), w_in_bf)


def _shift(x, mu):
    t_len = x.shape[0]
    row = lax.broadcasted_iota(jnp.int32, x.shape, 0)
    prev = jnp.where(row == 0, 0.0, pltpu.roll(x, 1, 0))
    nxt = jnp.where(row == t_len - 1, 0.0, pltpu.roll(x, t_len - 1, 0))
    return x + mu[0:1, :] * (prev - x) + mu[1:2, :] * (nxt - x)


def _stack_heads(x, lane_lo):
    return jnp.concatenate([x * lane_lo, x * (1.0 - lane_lo)], axis=0)


def _wkv_intra(units, consts):
    tri, mask_s, mask_i, eye, lane_lo, blk = consts
    stack = lambda z: _stack_heads(z, lane_lo)

    cums = [_dot_exact_lhs(tri[int(u[6])], u[1]) for u in units]

    prep = []
    for (r, lw, kd, v, kk, b, reverse), cum in zip(units, cums):
        mid_row = CHUNK // 2 if reverse else CHUNK // 2 - 1
        tot_row = 0 if reverse else CHUNK - 1
        a = -kk
        ex = cum - lw
        mid = cum[mid_row:mid_row + 1, :]
        tot = cum[tot_row:tot_row + 1, :]
        up = jnp.exp(cum - mid)
        dn = jnp.exp(mid - cum)
        tail = jnp.exp(tot - cum)
        prep.append(dict(
            at_m=stack(a * jnp.exp(ex - mid)).astype(BF16),
            rt_m=stack(r * up).astype(BF16),
            btkt=jnp.concatenate([stack(b * dn), stack(kd * dn)], axis=0).astype(BF16),
            a_e=stack(a * jnp.exp(ex)),
            r_e=stack(r * jnp.exp(cum)),
            bk_t=jnp.concatenate([stack(b * tail), stack(kd * tail)], axis=0).T.astype(BF16),
            vv=stack(v).astype(BF16),
            diag=jnp.where(eye, jnp.exp(tot), 0.0),
            rev=int(reverse)))

    ntd = lambda x, y: lax.dot_general(x, y, (((1,), (1,)), ((), ())), preferred_element_type=F32)
    mm = lambda x, y: jnp.dot(x, y, preferred_element_type=F32)
    top = [mask_s[p["rev"]] * ntd(p["at_m"], p["btkt"]) for p in prep]
    bot = [(mask_i[p["rev"]] * ntd(p["rt_m"], p["btkt"])).astype(BF16) for p in prep]

    diag_blk, swap_eye = blk
    off_blk = 1.0 - diag_blk
    both = [t[:, :LANES] + swap_eye for t in top]
    steps = CHUNK.bit_length() - 1
    diag_bf = diag_blk.astype(BF16)
    for j in range(steps):
        packed = [q.astype(BF16) for q in both]
        res = [mm(qb * diag_bf, qb) for qb in packed]
        both = [r + off_blk * q for r, q in zip(res, both)]
    ts = [pltpu.roll(q, HEAD_DIM, 1).astype(BF16) for q in both]
    x0 = [jnp.concatenate([p["a_e"], mm(t[:, LANES:].astype(BF16), p["vv"])], axis=1) for p, t in zip(prep, top)]
    xs = [mm(t, x.astype(BF16)) for x, t in zip(x0, ts)]

    out = []
    zeros = jnp.zeros((STACK, LANES), BF16)
    for p, x, bt in zip(prep, xs, bot):
        rhs = jnp.concatenate([x.astype(BF16), jnp.concatenate([zeros, p["vv"]], axis=1)], axis=0)
        lhs = jnp.concatenate([bt, p["bk_t"]], axis=0)
        res = mm(lhs, rhs)
        lhs2 = res[:, :LANES] + jnp.concatenate([p["r_e"], p["diag"]], axis=0)
        out.append((lhs2.astype(BF16), res[:, LANES:]))
    return out


def _wkv_constants():
    lane_head = np.arange(LANES) // HEAD_DIM
    ones = (lane_head[:, None] == lane_head[None, :]).astype(np.float32)
    t = np.arange(CHUNK)
    tri = np.stack([t[None, :] <= t[:, None], t[None, :] >= t[:, None]]).astype(np.float32)
    s = np.arange(STACK)
    same = (s[:, None] // CHUNK) == (s[None, :] // CHUNK)
    rs, cs = s[:, None], s[None, :]
    masks = np.stack([same & (cs < rs), same & (cs > rs), same & (cs <= rs), same & (cs >= rs)]).astype(np.float32)
    masks = np.concatenate([masks, masks], axis=-1)
    return jnp.asarray(ones, BF16), jnp.asarray(tri, BF16), jnp.asarray(masks, F32)


def _rwkv_kernel(*refs, t_len, has_s0, pairs, units, n_cast):
    (r_ref, k_ref, v_ref, lo_ref, mur_ref, muk_ref, muv_ref, mul_ref, w0_ref, a0_ref, wup_ref, aup_ref,
     gup_ref, kk_ref, ka_ref, rk_ref, lng_ref, lnb_ref, ones_ref, tri_ref, mask_ref) = refs[:21]
    pos = 21
    s0_ref = None
    if has_s0:
        s0_ref = refs[pos]
        pos += 1
    cast_in = refs[pos:pos + n_cast]
    pos += n_cast
    o_ref, sn_ref = refs[pos], refs[pos + 1]
    cast_out = refs[pos + 2:pos + 2 + n_cast]
    (r_s, v_s, kk_s, b0_s, b1_s, lw0_s, lw1_s, kd0_s, kd1_s, gate_s, bonus_s, yf_s, yb_s,
     lhs_s, add_s, st_s) = refs[pos + 2 + n_cast:]
    for w_in_ref, w_out_ref in zip(cast_in, cast_out):
        w_out_ref[...] = w_in_ref[...].astype(BF16)

    ones = ones_ref[...]
    lane = lax.broadcasted_iota(jnp.int32, (1, LANES), 1)
    lo_half = lane < HEAD_DIM
    lane_lo = jnp.where(lo_half, 1.0, 0.0)
    mm = lambda x, y: jnp.dot(x, y, preferred_element_type=F32)

    lo = _shift(lo_ref[0], mul_ref[...])
    wd = jnp.tanh(lo[:, 0:LANES])
    ad = lo[:, LANES:2 * LANES]
    sig_gd = _sigmoid(lo[:, 2 * LANES:3 * LANES]).astype(BF16)
    wd_split = [_split2(wd * m) for m in (lane_lo, 1.0 - lane_lo)]
    ad_bf = [(ad * m).astype(BF16) for m in (lane_lo, 1.0 - lane_lo)]
    for j in range(pairs):
        cols = slice(j * LANES, (j + 1) * LANES)
        r = _shift(r_ref[0, :, cols], mur_ref[:, cols])
        k = _shift(k_ref[0, :, cols], muk_ref[:, cols])
        v = _shift(v_ref[0, :, cols], muv_ref[:, cols])
        kk = k * kk_ref[:, cols]
        kk = kk * lax.rsqrt(_head_sum(kk * kk, ones) + L2_EPS)
        wup_h, wup_l = _split2(wup_ref[:, cols])
        aup = aup_ref[:, cols].astype(BF16)
        kdirs = []
        for e, (lw_s, kd_s, b_s) in enumerate(((lw0_s, kd0_s, b0_s), (lw1_s, kd1_s, b1_s))):
            wd_h, wd_l = wd_split[e]
            w_lin = w0_ref[e:e + 1, cols] + (mm(wd_h, wup_h) + mm(wd_l, wup_h) + mm(wd_h, wup_l))
            lw_s[j] = -DECAY_SCALE * _sigmoid(w_lin)
            iclr = _sigmoid(a0_ref[e:e + 1, cols] + mm(ad_bf[e], aup))
            kd = k * (1.0 + (iclr - 1.0) * ka_ref[:, cols])
            kd_s[j] = kd
            b_s[j] = kk * iclr
            kdirs.append(kd)
        gate_s[:, cols] = mm(sig_gd, gup_ref[:, cols].astype(BF16))
        bonus_s[:, cols] = _head_sum(r * (0.5 * (kdirs[0] + kdirs[1])) * rk_ref[:, cols], ones) * v
        r_s[j] = r
        v_s[j] = v
        kk_s[j] = kk

    n_chunks = t_len // CHUNK
    chunks_per = min(n_chunks, units // 2)
    pairs_per = min(pairs, units // (2 * chunks_per))
    groups = n_chunks // chunks_per
    rs = lax.broadcasted_iota(jnp.int32, (STACK, STACK), 0)
    cs = lax.broadcasted_iota(jnp.int32, (STACK, STACK), 1)
    as_f32 = lambda m: jnp.where(m, 1.0, 0.0)
    blk = (as_f32(rs // CHUNK == cs // CHUNK), as_f32(cs == (rs + CHUNK) % STACK))
    consts = ((tri_ref[0], tri_ref[1]), (mask_ref[0], mask_ref[1]), (mask_ref[2], mask_ref[3]),
              rs == cs, lane_lo, blk)
    dirs = ((lw0_s, kd0_s, b0_s), (lw1_s, kd1_s, b1_s))

    def intra_body(it, carry):
        pg = it // groups
        g = it % groups
        units, ids = [], []
        for jj in range(pairs_per):
            j = pg * pairs_per + jj
            for cc in range(chunks_per):
                c = g * chunks_per + cc
                rows = pl.ds(pl.multiple_of(c * CHUNK, CHUNK), CHUNK)
                for e, (lw_s, kd_s, b_s) in enumerate(dirs):
                    units.append((r_s[j, rows, :], lw_s[j, rows, :], kd_s[j, rows, :], v_s[j, rows, :],
                                  kk_s[j, rows, :], b_s[j, rows, :], e == 1))
                    ids.append((j * 2 + e) * n_chunks + c)
        for uid, (lhs, add) in zip(ids, _wkv_intra(units, consts)):
            lhs_s[uid] = lhs
            add_s[uid] = add
        return carry

    lax.fori_loop(0, (pairs // pairs_per) * groups, intra_body, 0)

    for j in range(pairs):
        for e in range(2):
            if has_s0:
                st_s[2 * j + e] = s0_ref[0, e, j].T
            else:
                st_s[2 * j + e] = jnp.zeros((LANES, LANES), F32)

    def state_body(it, carry):
        chunk = (it, n_chunks - 1 - it)
        uids = [(j * 2 + e) * n_chunks + chunk[e] for j in range(pairs) for e in range(2)]
        sts = [st_s[ch].astype(BF16) for ch in range(2 * pairs)]
        res = [mm(lhs_s[uid], st) + add_s[uid] for uid, st in zip(uids, sts)]
        for ch, rr in enumerate(res):
            j, e = divmod(ch, 2)
            y_s = yb_s if e else yf_s
            y_s[j, pl.ds(pl.multiple_of(chunk[e] * CHUNK, CHUNK), CHUNK), :] = rr[:CHUNK] + rr[CHUNK:STACK]
            st_s[ch] = rr[STACK:]
        return carry

    lax.fori_loop(0, n_chunks, state_body, 0)
    for j in range(pairs):
        for e in range(2):
            st_t = st_s[2 * j + e].T
            sn_ref[0, e, 2 * j] = st_t[:HEAD_DIM, :HEAD_DIM]
            sn_ref[0, e, 2 * j + 1] = pltpu.roll(st_t, HEAD_DIM, 1)[HEAD_DIM:, :HEAD_DIM]

    inv_d = 1.0 / HEAD_DIM
    for j in range(pairs):
        cols = slice(j * LANES, (j + 1) * LANES)
        y = yf_s[j] + yb_s[j]
        mean = _head_sum(y, ones) * inv_d
        dlt = y - mean
        var = _head_sum(dlt * dlt, ones) * inv_d
        yn = dlt * lax.rsqrt(var + GN_EPS) * lng_ref[:, cols] + lnb_ref[:, cols]
        o_ref[0, :, cols] = (yn + bonus_s[:, cols]) * gate_s[:, cols]


def _rwkv_branch(u_rw, bsz, b_off, s0_big, p, pairs, units, cast=()):
    t_len = u_rw.shape[1]
    has_s0 = s0_big is not None
    width = pairs * LANES
    seg = RW_WIDTH // width
    tok = lambda off: pl.BlockSpec((1, t_len, width), lambda b, j: (b_off + b, 0, off + j))
    mu = lambda off: pl.BlockSpec((2, width), lambda b, j: (0, off + j))
    vec2 = pl.BlockSpec((2, width), lambda b, j: (0, j))
    vec1 = pl.BlockSpec((1, width), lambda b, j: (0, j))
    mat = pl.BlockSpec((LANES, width), lambda b, j: (0, j))
    lora_w = 3 * LANES
    lora_blk = 3 * RW_WIDTH // lora_w
    in_specs = [tok(0), tok(seg), tok(2 * seg),
                pl.BlockSpec((1, t_len, lora_w), lambda b, j: (b_off + b, 0, lora_blk)),
                mu(0), mu(seg), mu(2 * seg),
                pl.BlockSpec((2, lora_w), lambda b, j: (0, lora_blk)),
                vec2, vec2, mat, mat, mat, vec1, vec1, vec1, vec1, vec1,
                _const_spec((LANES, LANES)), _const_spec((2, CHUNK, CHUNK)), _const_spec((4, STACK, 2 * STACK))]
    args = [u_rw, u_rw, u_rw, u_rw, p["shift_mu"], p["shift_mu"], p["shift_mu"], p["shift_mu"],
            p["rw_w0"], p["rw_a0"],
            p["rw_w_up"].reshape(2 * LORA_DECAY, RW_WIDTH), p["rw_a_up"].reshape(2 * LORA_ICLR, RW_WIDTH),
            p["rw_g_up"], p["rw_k_k"].reshape(1, -1), p["rw_k_a"].reshape(1, -1),
            p["rw_r_k"].reshape(1, -1), p["rw_ln_g"].reshape(1, -1), p["rw_ln_b"].reshape(1, -1),
            *_wkv_constants()]
    st_spec = pl.BlockSpec((1, 2, pairs, LANES, LANES), lambda b, j: (b, 0, j, 0, 0))
    if has_s0:
        in_specs.append(st_spec)
        args.append(s0_big)
    out_specs = [pl.BlockSpec((1, t_len, width), lambda b, j: (b, 0, j)),
                 pl.BlockSpec((1, 2, 2 * pairs, HEAD_DIM, HEAD_DIM), lambda b, j: (b, 0, j, 0, 0))]
    out_shape = [jax.ShapeDtypeStruct((bsz, t_len, RW_WIDTH), F32),
                 jax.ShapeDtypeStruct((bsz, 2, RW_HEADS, HEAD_DIM, HEAD_DIM), F32)]
    for w in cast:
        assert PAIRS == pairs and w.shape[0] % (16 * bsz) == 0
        blk = pl.BlockSpec((w.shape[0] // bsz, w.shape[1]), lambda b, j: (b, 0))
        in_specs.append(blk)
        args.append(w)
        out_specs.append(blk)
        out_shape.append(jax.ShapeDtypeStruct(w.shape, BF16))
    n_units = 2 * pairs * (t_len // CHUNK)
    per_pair = pltpu.VMEM((pairs, t_len, LANES), F32)
    full = pltpu.VMEM((t_len, width), F32)
    scratch = [per_pair] * 9 + [full, full, per_pair, per_pair,
                                pltpu.VMEM((n_units, 2 * STACK, LANES), BF16),
                                pltpu.VMEM((n_units, 2 * STACK, LANES), F32),
                                pltpu.VMEM((2 * pairs, LANES, LANES), F32)]
    o_rw, s_new, *casted = pl.pallas_call(
        functools.partial(_rwkv_kernel, t_len=t_len, has_s0=has_s0, pairs=pairs, units=units, n_cast=len(cast)),
        grid=(bsz, PAIRS // pairs),
        in_specs=in_specs,
        out_specs=out_specs,
        out_shape=out_shape,
        scratch_shapes=scratch,
        compiler_params=_params(2),
        name="rwkv_branch",
    )(*args)
    return o_rw, s_new, casted


def _state_to_big(s0):
    bsz = s0.shape[0]
    x = s0.reshape(bsz, 2, PAIRS, 2, HEAD_DIM, 1, HEAD_DIM)
    same_head = np.eye(2, dtype=bool).reshape(2, 1, 2, 1)
    return jnp.where(same_head, x, 0.0).reshape(bsz, 2, PAIRS, LANES, LANES)


def _qk_norm(t, g, ones):
    ms = _head_sum(t * t, ones) * (1.0 / HEAD_DIM)
    return t * lax.rsqrt(ms + RMS_EPS) * g


def _nt(x, y):
    return lax.dot_general(x, y, (((1,), (1,)), ((), ())), preferred_element_type=F32)


def _na_ctx_kernel(q_ref, k_ref, v_ref, qg_ref, kg_ref, o_ref, kn_ref, vc_ref):
    ones = _head_ones()
    lo_half = lax.broadcasted_iota(jnp.int32, (1, LANES), 1) < HEAD_DIM
    lo = jnp.where(lo_half, 1.0, 0.0)
    t_len = q_ref.shape[1]
    qs, ks, vs = [], [], []
    for j in range(NA_WIDTH // LANES):
        cols = slice(j * LANES, (j + 1) * LANES)
        qn = _qk_norm(q_ref[0, :, cols], qg_ref[...], ones)
        kn = _qk_norm(k_ref[0, :, cols], kg_ref[...], ones)
        v = v_ref[0, :, cols]
        kn_ref[0, :, cols] = kn
        vc_ref[0, :, cols] = v
        qn = qn * QK_SCALE
        qs.append(jnp.concatenate([qn * lo, qn * (1.0 - lo)], axis=0).astype(BF16))
        ks.append(kn.astype(BF16))
        vs.append(v.astype(BF16))
    logits = [_nt(q, k) for q, k in zip(qs, ks)]
    ms = [jnp.max(s, axis=-1, keepdims=True) for s in logits]
    ps = [jnp.exp(s - m) for s, m in zip(logits, ms)]
    ls = [jnp.sum(p, axis=-1, keepdims=True) for p in ps]
    outs = [jnp.dot(p.astype(BF16), v, preferred_element_type=F32) / l for p, v, l in zip(ps, vs, ls)]
    for j, o in enumerate(outs):
        o_ref[0, :, j * LANES:(j + 1) * LANES] = jnp.where(lo_half, o[:t_len], o[t_len:])


def _na_context(u_na, bsz, q_g, k_g):
    t_len = u_na.shape[1]
    tok = lambda seg: pl.BlockSpec((1, t_len, NA_WIDTH), lambda b: (b, 0, seg))
    out_blk = pl.BlockSpec((1, t_len, NA_WIDTH), lambda b: (b, 0, 0))
    g2 = lambda g: jnp.tile(g.reshape(1, HEAD_DIM), (1, 2))
    shp = jax.ShapeDtypeStruct((bsz, t_len, NA_WIDTH), F32)
    return pl.pallas_call(
        _na_ctx_kernel,
        grid=(bsz,),
        in_specs=[tok(0), tok(1), tok(2), _const_spec((1, LANES)), _const_spec((1, LANES))],
        out_specs=[out_blk, out_blk, out_blk],
        out_shape=[shp, shp, shp],
        compiler_params=_params(1),
        name="na_context",
    )(u_na, u_na, u_na, g2(q_g), g2(k_g))


NA_ROW_ILP = 8


def _na_lat_kernel(q_ref, k_ref, v_ref, kc_ref, vc_ref, qg_ref, kg_ref, tab_ref, o_ref,
                   q0_s, q1_s, kn_s, v_s, kc_s, vc_s, *, rows, kr):
    ones = _head_ones()
    lo_half = lax.broadcasted_iota(jnp.int32, (1, LANES), 1) < HEAD_DIM
    lo = jnp.where(lo_half, 1.0, 0.0)
    qn = _qk_norm(q_ref[0], qg_ref[...], ones) * QK_SCALE
    q0_s[...] = (qn * lo).astype(BF16)
    q1_s[...] = (qn * (1.0 - lo)).astype(BF16)
    kn_s[...] = _qk_norm(k_ref[0], kg_ref[...], ones).astype(BF16)
    v_s[...] = v_ref[0].astype(BF16)
    kc_s[...] = kc_ref[0].astype(BF16)
    vc_s[...] = vc_ref[0].astype(BF16)
    win = kr * GRID_W

    def body(it, carry):
        qs, k_rows, q_rows, biases = [], [], [], []
        for s in range(NA_ROW_ILP):
            i = it * NA_ROW_ILP + s
            r0 = jnp.clip(i - kr // 2, 0, rows - kr)
            d0 = r0 - i + (NA_ROWS - 1)
            qr = pl.ds(pl.multiple_of(i * GRID_W, GRID_W), GRID_W)
            q_rows.append(qr)
            k_rows.append(pl.ds(pl.multiple_of(r0 * GRID_W, GRID_W), win))
            qs.append(jnp.concatenate([q0_s[qr, :], q1_s[qr, :]], axis=0))
            biases.append(jnp.concatenate(
                [jnp.concatenate([tab_ref[h, d0 + 2 * m] for m in range(kr // 2)], axis=1) for h in range(2)],
                axis=0))
        lw = [_nt(q, kn_s[kr_, :]) + b for q, kr_, b in zip(qs, k_rows, biases)]
        lc = [_nt(q, kc_s[...]) for q in qs]
        ms = [jnp.maximum(jnp.max(a, axis=-1, keepdims=True), jnp.max(c, axis=-1, keepdims=True))
              for a, c in zip(lw, lc)]
        pw = [jnp.exp(a - m) for a, m in zip(lw, ms)]
        pc = [jnp.exp(c - m) for c, m in zip(lc, ms)]
        ls = [jnp.sum(a, axis=-1, keepdims=True) + jnp.sum(c, axis=-1, keepdims=True) for a, c in zip(pw, pc)]
        outs = [(jnp.dot(a.astype(BF16), v_s[kr_, :], preferred_element_type=F32)
                 + jnp.dot(c.astype(BF16), vc_s[...], preferred_element_type=F32)) / l
                for a, c, kr_, l in zip(pw, pc, k_rows, ls)]
        for qr, o in zip(q_rows, outs):
            o_ref[0, qr, :] = jnp.where(lo_half, o[:GRID_W], o[GRID_W:])
        return carry

    lax.fori_loop(0, rows // NA_ROW_ILP, body, 0)


def _latent_bias_table(rpb):
    qc = np.arange(GRID_W)[:, None]
    kc = np.arange(GRID_W)[None, :]
    ws = np.clip(qc - NA_COLS // 2, 0, GRID_W - NA_COLS)
    valid = (kc >= ws) & (kc < ws + NA_COLS)
    dc = np.clip(kc - qc, -(NA_COLS - 1), NA_COLS - 1) + NA_COLS - 1
    onehot = (dc[None] == np.arange(2 * NA_COLS - 1)[:, None, None]).astype(np.float32)
    cb = jnp.einsum("hdc,cqk->hdqk", rpb, jnp.asarray(onehot), precision=lax.Precision.HIGHEST)
    cb = jnp.where(valid[None, None], cb, NEG_INF)
    return jnp.concatenate([cb[:, :-1], cb[:, 1:]], axis=-1)


def _na_latent(u_na, bsz, b_off, k_ctx, v_ctx, q_g, k_g, rpb):
    t_len = u_na.shape[1]
    rows = t_len // GRID_W
    kr = min(NA_ROWS, rows)
    assert kr % 2 == 0 and rows % NA_ROW_ILP == 0
    ctx_len = k_ctx.shape[1]
    seg = NA_WIDTH // LANES
    tok = lambda off: pl.BlockSpec((1, t_len, LANES), lambda b, j: (b_off + b, 0, off + j))
    ctx = pl.BlockSpec((1, ctx_len, LANES), lambda b, j: (b, 0, j))
    g2 = lambda g: jnp.tile(g.reshape(1, HEAD_DIM), (1, 2))
    tab = _latent_bias_table(rpb)
    tok_s = pltpu.VMEM((t_len, LANES), BF16)
    ctx_s = pltpu.VMEM((ctx_len, LANES), BF16)
    return pl.pallas_call(
        functools.partial(_na_lat_kernel, rows=rows, kr=kr),
        grid=(bsz, seg),
        in_specs=[tok(0), tok(seg), tok(2 * seg), ctx, ctx,
                  _const_spec((1, LANES)), _const_spec((1, LANES)),
                  pl.BlockSpec((2, 2 * NA_ROWS - 2, GRID_W, 2 * GRID_W), lambda b, j: (j, 0, 0, 0))],
        out_specs=pl.BlockSpec((1, t_len, LANES), lambda b, j: (b, 0, j)),
        out_shape=jax.ShapeDtypeStruct((bsz, t_len, NA_WIDTH), F32),
        scratch_shapes=[tok_s, tok_s, tok_s, tok_s, ctx_s, ctx_s],
        compiler_params=_params(2),
        name="na_latent",
    )(u_na, u_na, u_na, k_ctx, v_ctx, g2(q_g), g2(k_g), tab)


def _out_ffn_kernel(xa_ref, xb_ref, orwa_ref, orwb_ref, onaa_ref, onab_ref, gt_ref, mod_ref, g_ref,
                    wor_ref, won_ref, wout_ref, w1_ref, w3_ref, w2_ref, ya_ref, yb_ref, *, tiles_a):
    i = pl.program_id(0)
    x = _pick_group(i, tiles_a, xa_ref, xb_ref)
    o_rw = _pick_group(i, tiles_a, orwa_ref, orwb_ref)
    o_na = _pick_group(i, tiles_a, onaa_ref, onab_ref)
    g_rw = _sigmoid(gt_ref[:, :D_MODEL].astype(F32))
    g_na = _sigmoid(gt_ref[:, D_MODEL:].astype(F32))
    merged = g_rw * _dot(o_rw, wor_ref[...]) + g_na * _dot(o_na, won_ref[...])
    x1 = x + mod_ref[0, 2:3, :] * _dot(merged, wout_ref[...])
    h2 = _rms_rows(x1) * g_ref[...]
    h2 = (h2 * (1.0 + mod_ref[0, 4:5, :]) + mod_ref[0, 3:4, :]).astype(BF16)
    acc = jnp.zeros(x1.shape, F32)
    for c in range(FF_HIDDEN // FF_CHUNK):
        cols = slice(c * FF_CHUNK, (c + 1) * FF_CHUNK)
        a = jnp.dot(h2, w1_ref[:, cols], preferred_element_type=F32)
        b = jnp.dot(h2, w3_ref[:, cols], preferred_element_type=F32)
        hh = (a * _sigmoid(a) * b).astype(BF16)
        acc = acc + jnp.dot(hh, w2_ref[cols, :], preferred_element_type=F32)
    y = x1 + mod_ref[0, 5:6, :] * acc

    @pl.when(i < tiles_a)
    def _():
        ya_ref[0] = y

    @pl.when(i >= tiles_a)
    def _():
        yb_ref[0] = y


def _out_ffn(xa, xb, orw_a, orw_b, ona_a, ona_b, gates, tiles_per_b, mod_all, norm_g, wb):
    tm = TOKEN_TILE
    tiles_a, tiles = xa.shape[0], xa.shape[0] + xb.shape[0]
    weights = [wb[n] for n in LATE_WEIGHTS]
    groups = lambda width: _group_specs(tiles_a, tm, width)
    return pl.pallas_call(
        functools.partial(_out_ffn_kernel, tiles_a=tiles_a),
        grid=(tiles,),
        in_specs=groups(D_MODEL) + groups(RW_WIDTH) + groups(NA_WIDTH)
        + [pl.BlockSpec((tm, GATE_COLS), lambda i: (i, 0)), _mod_spec(tiles_a, tiles_per_b),
           _const_spec((1, D_MODEL))] + [_const_spec(w.shape) for w in weights],
        out_specs=groups(D_MODEL),
        out_shape=[jax.ShapeDtypeStruct(xa.shape, F32), jax.ShapeDtypeStruct(xb.shape, F32)],
        compiler_params=_params(1),
        name="out_ffn",
    )(xa, xb, orw_a, orw_b, ona_a, ona_b, gates, mod_all, norm_g.reshape(1, -1), *weights)


LATE_WEIGHTS = ("w_o_rwkv", "w_o_na", "w_out", "ffn_w1", "ffn_w3", "ffn_w2")


def kernel(x_prompt, x_sample, state_rwkv, cache_na_k, cache_na_v, c, c_ctx, norm1_g, norm2_g, w_ada, b_ada,
           w_in, shift_mu, rw_w0, rw_w_up, rw_a0, rw_a_up, rw_g_up, rw_k_k, rw_k_a, rw_r_k, rw_ln_g, rw_ln_b,
           na_q_g, na_k_g, na_rpb, w_o_rwkv, w_o_na, w_out, ffn_w1, ffn_w3, ffn_w2):
    depth = w_in.shape[0]
    bsz, seq = x_prompt.shape[:2]
    dec, dec_seq = x_sample.shape[:2]
    tm = TOKEN_TILE
    n_ctx = bsz * seq
    assert n_ctx % tm == 0 and dec_seq % tm == 0 and n_ctx % dec_seq == 0
    tiles_per_b = dec_seq // tm
    tiled = lambda t: t.reshape(-1, tm, t.shape[-1])
    cvecs = jnp.concatenate([c_ctx[None, :], c], axis=0).T
    y_p, y_s = x_prompt, x_sample
    new_s, new_k, new_v = [], [], []
    for l in range(depth):
        p = dict(norm1_g=norm1_g[l], norm2_g=norm2_g[l], shift_mu=shift_mu[l], rw_w0=rw_w0[l],
                 rw_w_up=rw_w_up[l], rw_a0=rw_a0[l], rw_a_up=rw_a_up[l], rw_g_up=rw_g_up[l],
                 rw_k_k=rw_k_k[l], rw_k_a=rw_k_a[l], rw_r_k=rw_r_k[l], rw_ln_g=rw_ln_g[l],
                 rw_ln_b=rw_ln_b[l], na_q_g=na_q_g[l], na_k_g=na_k_g[l], na_rpb=na_rpb[l])
        late_f32 = [w[l] for w in (w_o_rwkv, w_o_na, w_out, ffn_w1, ffn_w3, ffn_w2)]
        mod_all, w_in_bf = _modulation(cvecs, w_ada[l], b_ada[l], w_in[l])
        mod_all = mod_all.reshape(1 + dec, 6, D_MODEL)
        u_rw, u_na, gates = _in_proj(tiled(y_p), tiled(y_s), tiles_per_b, mod_all, p["norm1_g"], w_in_bf)
        ctx_view = lambda t: t.reshape(-1, seq, t.shape[-1])
        lat_view = lambda t: t.reshape(-1, dec_seq, t.shape[-1])
        lat_off = n_ctx // dec_seq
        o_rw_p, s_l, casted = _rwkv_branch(ctx_view(u_rw), bsz, 0, None, p, RWKV_PAIRS_CTX, RWKV_UNITS_CTX, late_f32)
        late = dict(zip(LATE_WEIGHTS, casted))
        o_na_p, k_l, v_l = _na_context(ctx_view(u_na), bsz, p["na_q_g"], p["na_k_g"])
        new_s.append(s_l)
        new_k.append(k_l.reshape(bsz, seq, NA_HEADS, HEAD_DIM))
        new_v.append(v_l.reshape(bsz, seq, NA_HEADS, HEAD_DIM))
        ctx_k = cache_na_k[:, l].reshape(dec, -1, NA_WIDTH)
        ctx_v = cache_na_v[:, l].reshape(dec, -1, NA_WIDTH)
        o_rw_s, _, _ = _rwkv_branch(lat_view(u_rw), dec, lat_off, _state_to_big(state_rwkv[:, l]), p, RWKV_PAIRS_LAT,
                                   RWKV_UNITS_LAT)
        o_na_s = _na_latent(lat_view(u_na), dec, lat_off, ctx_k, ctx_v, p["na_q_g"], p["na_k_g"], p["na_rpb"])
        y_p_t, y_s_t = _out_ffn(tiled(y_p), tiled(y_s), tiled(o_rw_p), tiled(o_rw_s), tiled(o_na_p), tiled(o_na_s),
                                gates, tiles_per_b, mod_all, p["norm2_g"], late)
        y_p, y_s = y_p_t.reshape(x_prompt.shape), y_s_t.reshape(x_sample.shape)
    return (y_p, y_s, jnp.stack(new_s, axis=1), jnp.stack(new_k, axis=1), jnp.stack(new_v, axis=1))
```

```python
import functools

import numpy as np
import jax
import jax.numpy as jnp
from jax import lax
from jax.experimental import pallas as pl
from jax.experimental.pallas import tpu as pltpu

D_MODEL = 1024
GRID_W = 64
HEAD_DIM = 64
RW_HEADS = 8
RW_WIDTH = RW_HEADS * HEAD_DIM
NA_HEADS = 8
NA_WIDTH = NA_HEADS * HEAD_DIM
LORA_DECAY = 64
LORA_ICLR = 64
LORA_GATE = 128
NA_ROWS = 8
NA_COLS = 16
FF_HIDDEN = 2816
RW_COLS = 3 * RW_WIDTH + 2 * LORA_DECAY + 2 * LORA_ICLR + LORA_GATE
NA_IN_COLS = 3 * NA_WIDTH
GATE_COLS = 2 * D_MODEL
RMS_EPS = 1e-6
GN_EPS = 64e-5
L2_EPS = 1e-12
NEG_INF = -1e30
DECAY_SCALE = float(np.exp(-0.5))
QK_SCALE = HEAD_DIM ** -0.5
assert QK_SCALE == 0.125

LANES = 128
PAIRS = RW_HEADS // 2
CHUNK = 64
STACK = 2 * CHUNK
RWKV_PAIRS_CTX = 4
RWKV_PAIRS_LAT = 2
RWKV_UNITS_CTX = 16
RWKV_UNITS_LAT = 16
TOKEN_TILE = 512
FF_CHUNK = 256
VMEM_LIMIT = 56 * 1024 * 1024

F32 = jnp.float32
BF16 = jnp.bfloat16


def _dot(a, b):
    return jnp.dot(a.astype(BF16), b.astype(BF16), preferred_element_type=F32)


def _dot_nt(a, b):
    return lax.dot_general(a.astype(BF16), b.astype(BF16), (((1,), (1,)), ((), ())),
                           preferred_element_type=F32)


def _split2(x):
    hi = x.astype(BF16)
    lo = (x - hi.astype(F32)).astype(BF16)
    return hi, lo


def _dot_exact_lhs(a_exact, b):
    h, l = _split2(b)
    d = lambda x: jnp.dot(a_exact, x, preferred_element_type=F32)
    return d(h) + d(l)


def _head_ones():
    r = lax.broadcasted_iota(jnp.int32, (LANES, LANES), 0) // HEAD_DIM
    c = lax.broadcasted_iota(jnp.int32, (LANES, LANES), 1) // HEAD_DIM
    return jnp.where(r == c, 1.0, 0.0).astype(BF16)


def _head_sum(x, ones):
    return jnp.dot(x.astype(BF16), ones, preferred_element_type=F32)


def _sigmoid(x):
    return 0.5 * jnp.tanh(0.5 * x) + 0.5


def _rms_rows(x):
    return x * lax.rsqrt(jnp.mean(x * x, axis=-1, keepdims=True) + RMS_EPS)


def _const_spec(shape):
    nd = len(shape)
    return pl.BlockSpec(shape, lambda *_: (0,) * nd, pipeline_mode=pl.Buffered(1))


def _params(n_axes):
    return pltpu.CompilerParams(dimension_semantics=("arbitrary",) * n_axes,
                                vmem_limit_bytes=VMEM_LIMIT)


def _mod_kernel(c_ref, w_ref, b_ref, win_ref, o_ref, winb_ref):
    s = c_ref[...]
    s = s * _sigmoid(s)
    w = w_ref[...]
    for r in range(s.shape[1]):
        o_ref[r:r + 1, :] = jnp.sum(w * s[:, r:r + 1], axis=0, keepdims=True) + b_ref[...]
    winb_ref[...] = win_ref[...].astype(BF16)


def _modulation(cvecs, w_ada, b_ada, w_in):
    n = cvecs.shape[1]
    tn = 768
    steps = 6 * D_MODEL // tn
    rows = w_in.shape[0] // steps
    assert rows % 16 == 0
    return pl.pallas_call(
        _mod_kernel,
        grid=(steps,),
        in_specs=[pl.BlockSpec((D_MODEL, n), lambda j: (0, 0)),
                  pl.BlockSpec((D_MODEL, tn), lambda j: (0, j)),
                  pl.BlockSpec((1, tn), lambda j: (0, j)),
                  pl.BlockSpec((rows, w_in.shape[1]), lambda j: (j, 0))],
        out_specs=[pl.BlockSpec((n, tn), lambda j: (0, j)),
                   pl.BlockSpec((rows, w_in.shape[1]), lambda j: (j, 0))],
        out_shape=[jax.ShapeDtypeStruct((n, 6 * D_MODEL), F32),
                   jax.ShapeDtypeStruct(w_in.shape, BF16)],
        compiler_params=_params(1),
        name="modulation",
    )(cvecs, w_ada, b_ada.reshape(1, -1), w_in)


def _pick_group(i, tiles_a, a_ref, b_ref):
    return jnp.where(i < tiles_a, a_ref[0], b_ref[0])


def _group_specs(tiles_a, tm, width):
    return [pl.BlockSpec((1, tm, width), lambda i: (jnp.minimum(i, tiles_a - 1), 0, 0)),
            pl.BlockSpec((1, tm, width), lambda i: (jnp.maximum(i - tiles_a, 0), 0, 0))]


def _mod_spec(tiles_a, tiles_per_b):
    return pl.BlockSpec((1, 6, D_MODEL),
                        lambda i: (jnp.where(i < tiles_a, 0, 1 + (i - tiles_a) // tiles_per_b), 0, 0))


def _inproj_kernel(xa_ref, xb_ref, mod_ref, g_ref, w_ref, urw_ref, una_ref, gt_ref, *, tiles_a):
    x = _pick_group(pl.program_id(0), tiles_a, xa_ref, xb_ref)
    h = _rms_rows(x) * g_ref[...]
    h = (h * (1.0 + mod_ref[0, 1:2, :]) + mod_ref[0, 0:1, :]).astype(BF16)
    d = lambda lo, hi: jnp.dot(h, w_ref[:, lo:hi], preferred_element_type=F32)
    urw_ref[...] = d(0, RW_COLS)
    una_ref[...] = d(RW_COLS, RW_COLS + NA_IN_COLS)
    gt_ref[...] = d(RW_COLS + NA_IN_COLS, RW_COLS + NA_IN_COLS + GATE_COLS).astype(BF16)


def _in_proj(xa, xb, tiles_per_b, mod_all, norm_g, w_in_bf):
    tm = TOKEN_TILE
    tiles_a, tiles = xa.shape[0], xa.shape[0] + xb.shape[0]
    row = lambda i: (i, 0)
    return pl.pallas_call(
        functools.partial(_inproj_kernel, tiles_a=tiles_a),
        grid=(tiles,),
        in_specs=_group_specs(tiles_a, tm, D_MODEL) + [_mod_spec(tiles_a, tiles_per_b),
                                                       _const_spec((1, D_MODEL)), _const_spec(w_in_bf.shape)],
        out_specs=[pl.BlockSpec((tm, RW_COLS), row),
                   pl.BlockSpec((tm, NA_IN_COLS), row),
                   pl.BlockSpec((tm, GATE_COLS), row)],
        out_shape=[jax.ShapeDtypeStruct((tiles * tm, RW_COLS), F32),
                   jax.ShapeDtypeStruct((tiles * tm, NA_IN_COLS), F32),
                   jax.ShapeDtypeStruct((tiles * tm, GATE_COLS), BF16)],
        compiler_params=_params(1),
        name="in_proj",
    )(xa, xb, mod_all, norm_g.reshape(1, -1), w_in_bf)


def _shift(x, mu):
    t_len = x.shape[0]
    row = lax.broadcasted_iota(jnp.int32, x.shape, 0)
    prev = jnp.where(row == 0, 0.0, pltpu.roll(x, 1, 0))
    nxt = jnp.where(row == t_len - 1, 0.0, pltpu.roll(x, t_len - 1, 0))
    return x + mu[0:1, :] * (prev - x) + mu[1:2, :] * (nxt - x)


def _stack_heads(x, lane_lo):
    return jnp.concatenate([x * lane_lo, x * (1.0 - lane_lo)], axis=0)


def _wkv_intra(units, consts):
    tri, mask_s, mask_i, eye, lane_lo, blk = consts
    stack = lambda z: _stack_heads(z, lane_lo)

    cums = [_dot_exact_lhs(tri[int(u[6])], u[1]) for u in units]

    prep = []
    for (r, lw, kd, v, kk, b, reverse), cum in zip(units, cums):
        mid_row = CHUNK // 2 if reverse else CHUNK // 2 - 1
        tot_row = 0 if reverse else CHUNK - 1
        a = -kk
        ex = cum - lw
        mid = cum[mid_row:mid_row + 1, :]
        tot = cum[tot_row:tot_row + 1, :]
        up = jnp.exp(cum - mid)
        dn = jnp.exp(mid - cum)
        tail = jnp.exp(tot - cum)
        prep.append(dict(
            at_m=stack(a * jnp.exp(ex - mid)).astype(BF16),
            rt_m=stack(r * up).astype(BF16),
            btkt=jnp.concatenate([stack(b * dn), stack(kd * dn)], axis=0).astype(BF16),
            a_e=stack(a * jnp.exp(ex)).astype(BF16),
            r_e=stack(r * jnp.exp(cum)),
            bk_t=jnp.concatenate([stack(b * tail), stack(kd * tail)], axis=0).T.astype(BF16),
            vv=stack(v).astype(BF16),
            decay=jnp.exp(tot),
            rev=int(reverse)))

    ntd = lambda x, y: lax.dot_general(x, y, (((1,), (1,)), ((), ())), preferred_element_type=F32)
    mm = lambda x, y: jnp.dot(x, y, preferred_element_type=F32)
    top = [mask_s[p["rev"]] * ntd(p["at_m"], p["btkt"]) for p in prep]
    bot = [(mask_i[p["rev"]] * ntd(p["rt_m"], p["btkt"])).astype(BF16) for p in prep]

    diag_blk, swap_eye = blk
    off_blk = 1.0 - diag_blk
    both = [t[:, :LANES] + swap_eye for t in top]
    a_ak = [t[:, LANES:].astype(BF16) for t in top]
    steps = CHUNK.bit_length() - 1
    diag_bf = diag_blk.astype(BF16)
    for j in range(steps):
        packed = [q.astype(BF16) for q in both]
        res = [mm(qb * diag_bf, qb) for qb in packed]
        both = [r + off_blk * q for r, q in zip(res, both)]
    ts = [pltpu.roll(q, HEAD_DIM, 1).astype(BF16) for q in both]
    x0 = [jnp.concatenate([p["a_e"], mm(ak, p["vv"]).astype(BF16)], axis=1) for p, ak in zip(prep, a_ak)]
    xs = [mm(t, x) for x, t in zip(x0, ts)]

    out = []
    zeros = jnp.zeros((STACK, LANES), BF16)
    for p, x, bt in zip(prep, xs, bot):
        rhs = jnp.concatenate([x.astype(BF16), jnp.concatenate([zeros, p["vv"]], axis=1)], axis=0)
        lhs = jnp.concatenate([bt, p["bk_t"]], axis=0)
        res = mm(lhs, rhs)
        lhs2 = res[:, :LANES] + jnp.concatenate([p["r_e"], jnp.where(eye, p["decay"], 0.0)], axis=0)
        out.append((lhs2.astype(BF16), res[:, LANES:]))
    return out


def _wkv_constants():
    lane_head = np.arange(LANES) // HEAD_DIM
    ones = (lane_head[:, None] == lane_head[None, :]).astype(np.float32)
    t = np.arange(CHUNK)
    tri = np.stack([t[None, :] <= t[:, None], t[None, :] >= t[:, None]]).astype(np.float32)
    s = np.arange(STACK)
    same = (s[:, None] // CHUNK) == (s[None, :] // CHUNK)
    rs, cs = s[:, None], s[None, :]
    masks = np.stack([same & (cs < rs), same & (cs > rs), same & (cs <= rs), same & (cs >= rs)]).astype(np.float32)
    masks = np.concatenate([masks, masks], axis=-1)
    return jnp.asarray(ones, BF16), jnp.asarray(tri, BF16), jnp.asarray(masks, F32)


def _rwkv_kernel(*refs, t_len, has_s0, pairs, units, n_cast):
    (r_ref, k_ref, v_ref, lo_ref, mur_ref, muk_ref, muv_ref, mul_ref, w0_ref, a0_ref, wup_ref, aup_ref,
     gup_ref, kk_ref, ka_ref, rk_ref, lng_ref, lnb_ref, ones_ref, tri_ref, mask_ref) = refs[:21]
    pos = 21
    s0_ref = None
    if has_s0:
        s0_ref = refs[pos]
        pos += 1
    cast_in = refs[pos:pos + n_cast]
    pos += n_cast
    o_ref, sn_ref = refs[pos], refs[pos + 1]
    cast_out = refs[pos + 2:pos + 2 + n_cast]
    (r_s, v_s, kk_s, b0_s, b1_s, lw0_s, lw1_s, kd0_s, kd1_s, gate_s, bonus_s, yf_s, yb_s,
     lhs_s, add_s, st_s) = refs[pos + 2 + n_cast:]
    for w_in_ref, w_out_ref in zip(cast_in, cast_out):
        w_out_ref[...] = w_in_ref[...].astype(BF16)

    ones = ones_ref[...]
    lane = lax.broadcasted_iota(jnp.int32, (1, LANES), 1)
    lo_half = lane < HEAD_DIM
    lane_lo = jnp.where(lo_half, 1.0, 0.0)
    mm = lambda x, y: jnp.dot(x, y, preferred_element_type=F32)

    lo = _shift(lo_ref[0], mul_ref[...])
    wd = jnp.tanh(lo[:, 0:LANES])
    ad = lo[:, LANES:2 * LANES]
    sig_gd = _sigmoid(lo[:, 2 * LANES:3 * LANES]).astype(BF16)
    wd_split = [_split2(wd * m) for m in (lane_lo, 1.0 - lane_lo)]
    ad_bf = [(ad * m).astype(BF16) for m in (lane_lo, 1.0 - lane_lo)]
    for j in range(pairs):
        cols = slice(j * LANES, (j + 1) * LANES)
        r = _shift(r_ref[0, :, cols], mur_ref[:, cols])
        k = _shift(k_ref[0, :, cols], muk_ref[:, cols])
        v = _shift(v_ref[0, :, cols], muv_ref[:, cols])
        kk = k * kk_ref[:, cols]
        kk = kk * lax.rsqrt(_head_sum(kk * kk, ones) + L2_EPS)
        wup_h, wup_l = _split2(wup_ref[:, cols])
        aup = aup_ref[:, cols].astype(BF16)
        kdirs = []
        for e, (lw_s, kd_s, b_s) in enumerate(((lw0_s, kd0_s, b0_s), (lw1_s, kd1_s, b1_s))):
            wd_h, wd_l = wd_split[e]
            w_lin = w0_ref[e:e + 1, cols] + (mm(wd_h, wup_h) + mm(wd_l, wup_h) + mm(wd_h, wup_l))
            lw_s[j] = -DECAY_SCALE * _sigmoid(w_lin)
            iclr = _sigmoid(a0_ref[e:e + 1, cols] + mm(ad_bf[e], aup))
            kd = k * (1.0 + (iclr - 1.0) * ka_ref[:, cols])
            kd_s[j] = kd
            b_s[j] = kk * iclr
            kdirs.append(kd)
        gate_s[:, cols] = mm(sig_gd, gup_ref[:, cols].astype(BF16))
        bonus_s[:, cols] = _head_sum(r * (0.5 * (kdirs[0] + kdirs[1])) * rk_ref[:, cols], ones) * v
        r_s[j] = r
        v_s[j] = v
        kk_s[j] = kk

    n_chunks = t_len // CHUNK
    chunks_per = min(n_chunks, units // 2)
    pairs_per = min(pairs, units // (2 * chunks_per))
    groups = n_chunks // chunks_per
    rs = lax.broadcasted_iota(jnp.int32, (STACK, STACK), 0)
    cs = lax.broadcasted_iota(jnp.int32, (STACK, STACK), 1)
    as_f32 = lambda m: jnp.where(m, 1.0, 0.0)
    blk = (as_f32(rs // CHUNK == cs // CHUNK), as_f32(cs == (rs + CHUNK) % STACK))
    consts = ((tri_ref[0], tri_ref[1]), (mask_ref[0], mask_ref[1]), (mask_ref[2], mask_ref[3]),
              rs == cs, lane_lo, blk)
    dirs = ((lw0_s, kd0_s, b0_s), (lw1_s, kd1_s, b1_s))

    def intra_body(it, carry):
        pg = it // groups
        g = it % groups
        units, ids = [], []
        for jj in range(pairs_per):
            j = pg * pairs_per + jj
            for cc in range(chunks_per):
                c = g * chunks_per + cc
                rows = pl.ds(pl.multiple_of(c * CHUNK, CHUNK), CHUNK)
                for e, (lw_s, kd_s, b_s) in enumerate(dirs):
                    units.append((r_s[j, rows, :], lw_s[j, rows, :], kd_s[j, rows, :], v_s[j, rows, :],
                                  kk_s[j, rows, :], b_s[j, rows, :], e == 1))
                    ids.append((j * 2 + e) * n_chunks + c)
        for uid, (lhs, add) in zip(ids, _wkv_intra(units, consts)):
            lhs_s[uid] = lhs
            add_s[uid] = add
        return carry

    lax.fori_loop(0, (pairs // pairs_per) * groups, intra_body, 0)

    for j in range(pairs):
        for e in range(2):
            if has_s0:
                st_s[2 * j + e] = s0_ref[0, e, j].T
            else:
                st_s[2 * j + e] = jnp.zeros((LANES, LANES), F32)

    def state_body(it, carry):
        chunk = (it, n_chunks - 1 - it)
        uids = [(j * 2 + e) * n_chunks + chunk[e] for j in range(pairs) for e in range(2)]
        sts = [st_s[ch].astype(BF16) for ch in range(2 * pairs)]
        res = [mm(lhs_s[uid], st) + add_s[uid] for uid, st in zip(uids, sts)]
        for ch, rr in enumerate(res):
            j, e = divmod(ch, 2)
            y_s = yb_s if e else yf_s
            y_s[j, pl.ds(pl.multiple_of(chunk[e] * CHUNK, CHUNK), CHUNK), :] = rr[:CHUNK] + rr[CHUNK:STACK]
            st_s[ch] = rr[STACK:]
        return carry

    lax.fori_loop(0, n_chunks, state_body, 0)
    for j in range(pairs):
        for e in range(2):
            st_t = st_s[2 * j + e].T
            sn_ref[0, e, 2 * j] = st_t[:HEAD_DIM, :HEAD_DIM]
            sn_ref[0, e, 2 * j + 1] = pltpu.roll(st_t, HEAD_DIM, 1)[HEAD_DIM:, :HEAD_DIM]

    inv_d = 1.0 / HEAD_DIM
    for j in range(pairs):
        cols = slice(j * LANES, (j + 1) * LANES)
        y = yf_s[j] + yb_s[j]
        mean = _head_sum(y, ones) * inv_d
        dlt = y - mean
        var = _head_sum(dlt * dlt, ones) * inv_d
        yn = dlt * lax.rsqrt(var + GN_EPS) * lng_ref[:, cols] + lnb_ref[:, cols]
        o_ref[0, :, cols] = (yn + bonus_s[:, cols]) * gate_s[:, cols]


def _rwkv_branch(u_rw, bsz, b_off, s0_big, p, pairs, units, cast=()):
    t_len = u_rw.shape[1]
    has_s0 = s0_big is not None
    width = pairs * LANES
    seg = RW_WIDTH // width
    tok = lambda off: pl.BlockSpec((1, t_len, width), lambda b, j: (b_off + b, 0, off + j))
    mu = lambda off: pl.BlockSpec((2, width), lambda b, j: (0, off + j))
    vec2 = pl.BlockSpec((2, width), lambda b, j: (0, j))
    vec1 = pl.BlockSpec((1, width), lambda b, j: (0, j))
    mat = pl.BlockSpec((LANES, width), lambda b, j: (0, j))
    lora_w = 3 * LANES
    lora_blk = 3 * RW_WIDTH // lora_w
    in_specs = [tok(0), tok(seg), tok(2 * seg),
                pl.BlockSpec((1, t_len, lora_w), lambda b, j: (b_off + b, 0, lora_blk)),
                mu(0), mu(seg), mu(2 * seg),
                pl.BlockSpec((2, lora_w), lambda b, j: (0, lora_blk)),
                vec2, vec2, mat, mat, mat, vec1, vec1, vec1, vec1, vec1,
                _const_spec((LANES, LANES)), _const_spec((2, CHUNK, CHUNK)), _const_spec((4, STACK, 2 * STACK))]
    args = [u_rw, u_rw, u_rw, u_rw, p["shift_mu"], p["shift_mu"], p["shift_mu"], p["shift_mu"],
            p["rw_w0"], p["rw_a0"],
            p["rw_w_up"].reshape(2 * LORA_DECAY, RW_WIDTH), p["rw_a_up"].reshape(2 * LORA_ICLR, RW_WIDTH),
            p["rw_g_up"], p["rw_k_k"].reshape(1, -1), p["rw_k_a"].reshape(1, -1),
            p["rw_r_k"].reshape(1, -1), p["rw_ln_g"].reshape(1, -1), p["rw_ln_b"].reshape(1, -1),
            *_wkv_constants()]
    st_spec = pl.BlockSpec((1, 2, pairs, LANES, LANES), lambda b, j: (b, 0, j, 0, 0))
    if has_s0:
        in_specs.append(st_spec)
        args.append(s0_big)
    out_specs = [pl.BlockSpec((1, t_len, width), lambda b, j: (b, 0, j)),
                 pl.BlockSpec((1, 2, 2 * pairs, HEAD_DIM, HEAD_DIM), lambda b, j: (b, 0, j, 0, 0))]
    out_shape = [jax.ShapeDtypeStruct((bsz, t_len, RW_WIDTH), F32),
                 jax.ShapeDtypeStruct((bsz, 2, RW_HEADS, HEAD_DIM, HEAD_DIM), F32)]
    for w in cast:
        assert PAIRS == pairs and w.shape[0] % (16 * bsz) == 0
        blk = pl.BlockSpec((w.shape[0] // bsz, w.shape[1]), lambda b, j: (b, 0))
        in_specs.append(blk)
        args.append(w)
        out_specs.append(blk)
        out_shape.append(jax.ShapeDtypeStruct(w.shape, BF16))
    n_units = 2 * pairs * (t_len // CHUNK)
    per_pair = pltpu.VMEM((pairs, t_len, LANES), F32)
    full = pltpu.VMEM((t_len, width), F32)
    scratch = [per_pair] * 9 + [full, full, per_pair, per_pair,
                                pltpu.VMEM((n_units, 2 * STACK, LANES), BF16),
                                pltpu.VMEM((n_units, 2 * STACK, LANES), F32),
                                pltpu.VMEM((2 * pairs, LANES, LANES), F32)]
    o_rw, s_new, *casted = pl.pallas_call(
        functools.partial(_rwkv_kernel, t_len=t_len, has_s0=has_s0, pairs=pairs, units=units, n_cast=len(cast)),
        grid=(bsz, PAIRS // pairs),
        in_specs=in_specs,
        out_specs=out_specs,
        out_shape=out_shape,
        scratch_shapes=scratch,
        compiler_params=_params(2),
        name="rwkv_branch",
    )(*args)
    return o_rw, s_new, casted


def _state_to_big(s0):
    bsz = s0.shape[0]
    x = s0.reshape(bsz, 2, PAIRS, 2, HEAD_DIM, 1, HEAD_DIM)
    same_head = np.eye(2, dtype=bool).reshape(2, 1, 2, 1)
    return jnp.where(same_head, x, 0.0).reshape(bsz, 2, PAIRS, LANES, LANES)


def _qk_norm(t, g, ones):
    ms = _head_sum(t * t, ones) * (1.0 / HEAD_DIM)
    return t * lax.rsqrt(ms + RMS_EPS) * g


def _nt(x, y):
    return lax.dot_general(x, y, (((1,), (1,)), ((), ())), preferred_element_type=F32)


def _na_ctx_kernel(q_ref, k_ref, v_ref, qg_ref, kg_ref, o_ref, kn_ref, vc_ref):
    ones = _head_ones()
    lo_half = lax.broadcasted_iota(jnp.int32, (1, LANES), 1) < HEAD_DIM
    lo = jnp.where(lo_half, 1.0, 0.0)
    t_len = q_ref.shape[1]
    qs, ks, vs = [], [], []
    for j in range(NA_WIDTH // LANES):
        cols = slice(j * LANES, (j + 1) * LANES)
        qn = _qk_norm(q_ref[0, :, cols], qg_ref[...], ones)
        kn = _qk_norm(k_ref[0, :, cols], kg_ref[...], ones)
        v = v_ref[0, :, cols]
        kn_ref[0, :, cols] = kn
        vc_ref[0, :, cols] = v
        qn = qn * QK_SCALE
        qs.append(jnp.concatenate([qn * lo, qn * (1.0 - lo)], axis=0).astype(BF16))
        ks.append(kn.astype(BF16))
        vs.append(v.astype(BF16))
    logits = [_nt(q, k) for q, k in zip(qs, ks)]
    ms = [jnp.max(s, axis=-1, keepdims=True) for s in logits]
    ps = [jnp.exp(s - m) for s, m in zip(logits, ms)]
    ls = [jnp.sum(p, axis=-1, keepdims=True) for p in ps]
    outs = [jnp.dot(p.astype(BF16), v, preferred_element_type=F32) / l for p, v, l in zip(ps, vs, ls)]
    for j, o in enumerate(outs):
        o_ref[0, :, j * LANES:(j + 1) * LANES] = jnp.where(lo_half, o[:t_len], o[t_len:])


def _na_context(u_na, bsz, q_g, k_g):
    t_len = u_na.shape[1]
    tok = lambda seg: pl.BlockSpec((1, t_len, NA_WIDTH), lambda b: (b, 0, seg))
    out_blk = pl.BlockSpec((1, t_len, NA_WIDTH), lambda b: (b, 0, 0))
    g2 = lambda g: jnp.tile(g.reshape(1, HEAD_DIM), (1, 2))
    shp = jax.ShapeDtypeStruct((bsz, t_len, NA_WIDTH), F32)
    return pl.pallas_call(
        _na_ctx_kernel,
        grid=(bsz,),
        in_specs=[tok(0), tok(1), tok(2), _const_spec((1, LANES)), _const_spec((1, LANES))],
        out_specs=[out_blk, out_blk, out_blk],
        out_shape=[shp, shp, shp],
        compiler_params=_params(1),
        name="na_context",
    )(u_na, u_na, u_na, g2(q_g), g2(k_g))


NA_ROW_ILP = 8


def _na_lat_kernel(q_ref, k_ref, v_ref, kc_ref, vc_ref, qg_ref, kg_ref, tab_ref, o_ref,
                   q0_s, q1_s, kn_s, v_s, kc_s, vc_s, *, rows, kr):
    ones = _head_ones()
    lo_half = lax.broadcasted_iota(jnp.int32, (1, LANES), 1) < HEAD_DIM
    lo = jnp.where(lo_half, 1.0, 0.0)
    qn = _qk_norm(q_ref[0], qg_ref[...], ones) * QK_SCALE
    q0_s[...] = (qn * lo).astype(BF16)
    q1_s[...] = (qn * (1.0 - lo)).astype(BF16)
    kn_s[...] = _qk_norm(k_ref[0], kg_ref[...], ones).astype(BF16)
    v_s[...] = v_ref[0].astype(BF16)
    kc_s[...] = kc_ref[0].astype(BF16)
    vc_s[...] = vc_ref[0].astype(BF16)
    win = kr * GRID_W

    def body(it, carry):
        qs, k_rows, q_rows, biases = [], [], [], []
        for s in range(NA_ROW_ILP):
            i = it * NA_ROW_ILP + s
            r0 = jnp.clip(i - kr // 2, 0, rows - kr)
            d0 = r0 - i + (NA_ROWS - 1)
            qr = pl.ds(pl.multiple_of(i * GRID_W, GRID_W), GRID_W)
            q_rows.append(qr)
            k_rows.append(pl.ds(pl.multiple_of(r0 * GRID_W, GRID_W), win))
            qs.append(jnp.concatenate([q0_s[qr, :], q1_s[qr, :]], axis=0))
            biases.append(jnp.concatenate(
                [jnp.concatenate([tab_ref[h, d0 + 2 * m] for m in range(kr // 2)], axis=1) for h in range(2)],
                axis=0))
        lw = [_nt(q, kn_s[kr_, :]) + b for q, kr_, b in zip(qs, k_rows, biases)]
        lc = [_nt(q, kc_s[...]) for q in qs]
        ms = [jnp.maximum(jnp.max(a, axis=-1, keepdims=True), jnp.max(c, axis=-1, keepdims=True))
              for a, c in zip(lw, lc)]
        pw = [jnp.exp(a - m) for a, m in zip(lw, ms)]
        pc = [jnp.exp(c - m) for c, m in zip(lc, ms)]
        ls = [jnp.sum(a, axis=-1, keepdims=True) + jnp.sum(c, axis=-1, keepdims=True) for a, c in zip(pw, pc)]
        outs = [(jnp.dot(a.astype(BF16), v_s[kr_, :], preferred_element_type=F32)
                 + jnp.dot(c.astype(BF16), vc_s[...], preferred_element_type=F32)) / l
                for a, c, kr_, l in zip(pw, pc, k_rows, ls)]
        for qr, o in zip(q_rows, outs):
            o_ref[0, qr, :] = jnp.where(lo_half, o[:GRID_W], o[GRID_W:])
        return carry

    lax.fori_loop(0, rows // NA_ROW_ILP, body, 0)


def _latent_bias_table(rpb):
    qc = np.arange(GRID_W)[:, None]
    kc = np.arange(GRID_W)[None, :]
    ws = np.clip(qc - NA_COLS // 2, 0, GRID_W - NA_COLS)
    valid = (kc >= ws) & (kc < ws + NA_COLS)
    n = 2 * GRID_W
    lead = GRID_W - NA_COLS
    z = jnp.pad(rpb, ((0, 0), (0, 0), (lead, n - lead - rpb.shape[-1])))
    z = jnp.roll(z, -(GRID_W - 1), axis=-1)
    skew = jnp.tile(z, (1, 1, GRID_W))[..., :GRID_W * (n - 1)].reshape(rpb.shape[:2] + (GRID_W, n - 1))
    cb = jnp.where(valid[None, None], skew[..., :GRID_W], NEG_INF)
    return jnp.concatenate([cb[:, :-1], cb[:, 1:]], axis=-1)


def _na_latent(u_na, bsz, b_off, k_ctx, v_ctx, q_g, k_g, rpb):
    t_len = u_na.shape[1]
    rows = t_len // GRID_W
    kr = min(NA_ROWS, rows)
    assert kr % 2 == 0 and rows % NA_ROW_ILP == 0
    ctx_len = k_ctx.shape[1]
    seg = NA_WIDTH // LANES
    tok = lambda off: pl.BlockSpec((1, t_len, LANES), lambda b, j: (b_off + b, 0, off + j))
    ctx = pl.BlockSpec((1, ctx_len, LANES), lambda b, j: (b, 0, j))
    g2 = lambda g: jnp.tile(g.reshape(1, HEAD_DIM), (1, 2))
    tab = _latent_bias_table(rpb)
    tok_s = pltpu.VMEM((t_len, LANES), BF16)
    ctx_s = pltpu.VMEM((ctx_len, LANES), BF16)
    return pl.pallas_call(
        functools.partial(_na_lat_kernel, rows=rows, kr=kr),
        grid=(bsz, seg),
        in_specs=[tok(0), tok(seg), tok(2 * seg), ctx, ctx,
                  _const_spec((1, LANES)), _const_spec((1, LANES)),
                  pl.BlockSpec((2, 2 * NA_ROWS - 2, GRID_W, 2 * GRID_W), lambda b, j: (j, 0, 0, 0))],
        out_specs=pl.BlockSpec((1, t_len, LANES), lambda b, j: (b, 0, j)),
        out_shape=jax.ShapeDtypeStruct((bsz, t_len, NA_WIDTH), F32),
        scratch_shapes=[tok_s, tok_s, tok_s, tok_s, ctx_s, ctx_s],
        compiler_params=_params(2),
        name="na_latent",
    )(u_na, u_na, u_na, k_ctx, v_ctx, g2(q_g), g2(k_g), tab)


def _out_ffn_kernel(xa_ref, xb_ref, orwa_ref, orwb_ref, onaa_ref, onab_ref, gt_ref, mod_ref, g_ref,
                    wor_ref, won_ref, wout_ref, w1_ref, w3_ref, w2_ref, ya_ref, yb_ref, *, tiles_a):
    i = pl.program_id(0)
    x = _pick_group(i, tiles_a, xa_ref, xb_ref)
    o_rw = _pick_group(i, tiles_a, orwa_ref, orwb_ref)
    o_na = _pick_group(i, tiles_a, onaa_ref, onab_ref)
    g_rw = _sigmoid(gt_ref[:, :D_MODEL].astype(F32))
    g_na = _sigmoid(gt_ref[:, D_MODEL:].astype(F32))
    merged = g_rw * _dot(o_rw, wor_ref[...]) + g_na * _dot(o_na, won_ref[...])
    x1 = x + mod_ref[0, 2:3, :] * _dot(merged, wout_ref[...])
    h2 = _rms_rows(x1) * g_ref[...]
    h2 = (h2 * (1.0 + mod_ref[0, 4:5, :]) + mod_ref[0, 3:4, :]).astype(BF16)
    acc = jnp.zeros(x1.shape, F32)
    for c in range(FF_HIDDEN // FF_CHUNK):
        cols = slice(c * FF_CHUNK, (c + 1) * FF_CHUNK)
        a = jnp.dot(h2, w1_ref[:, cols], preferred_element_type=F32)
        b = jnp.dot(h2, w3_ref[:, cols], preferred_element_type=F32)
        hh = (a * _sigmoid(a) * b).astype(BF16)
        acc = acc + jnp.dot(hh, w2_ref[cols, :], preferred_element_type=F32)
    y = x1 + mod_ref[0, 5:6, :] * acc

    @pl.when(i < tiles_a)
    def _():
        ya_ref[0] = y

    @pl.when(i >= tiles_a)
    def _():
        yb_ref[0] = y


def _out_ffn(xa, xb, orw_a, orw_b, ona_a, ona_b, gates, tiles_per_b, mod_all, norm_g, wb):
    tm = TOKEN_TILE
    tiles_a, tiles = xa.shape[0], xa.shape[0] + xb.shape[0]
    weights = [wb[n] for n in LATE_WEIGHTS]
    groups = lambda width: _group_specs(tiles_a, tm, width)
    return pl.pallas_call(
        functools.partial(_out_ffn_kernel, tiles_a=tiles_a),
        grid=(tiles,),
        in_specs=groups(D_MODEL) + groups(RW_WIDTH) + groups(NA_WIDTH)
        + [pl.BlockSpec((tm, GATE_COLS), lambda i: (i, 0)), _mod_spec(tiles_a, tiles_per_b),
           _const_spec((1, D_MODEL))] + [_const_spec(w.shape) for w in weights],
        out_specs=groups(D_MODEL),
        out_shape=[jax.ShapeDtypeStruct(xa.shape, F32), jax.ShapeDtypeStruct(xb.shape, F32)],
        compiler_params=_params(1),
        name="out_ffn",
    )(xa, xb, orw_a, orw_b, ona_a, ona_b, gates, mod_all, norm_g.reshape(1, -1), *weights)


LATE_WEIGHTS = ("w_o_rwkv", "w_o_na", "w_out", "ffn_w1", "ffn_w3", "ffn_w2")


def kernel(x_prompt, x_sample, state_rwkv, cache_na_k, cache_na_v, c, c_ctx, norm1_g, norm2_g, w_ada, b_ada,
           w_in, shift_mu, rw_w0, rw_w_up, rw_a0, rw_a_up, rw_g_up, rw_k_k, rw_k_a, rw_r_k, rw_ln_g, rw_ln_b,
           na_q_g, na_k_g, na_rpb, w_o_rwkv, w_o_na, w_out, ffn_w1, ffn_w3, ffn_w2):
    depth = w_in.shape[0]
    bsz, seq = x_prompt.shape[:2]
    dec, dec_seq = x_sample.shape[:2]
    tm = TOKEN_TILE
    n_ctx = bsz * seq
    assert n_ctx % tm == 0 and dec_seq % tm == 0 and n_ctx % dec_seq == 0
    tiles_per_b = dec_seq // tm
    tiled = lambda t: t.reshape(-1, tm, t.shape[-1])
    cvecs = jnp.concatenate([c_ctx[None, :], c], axis=0).T
    y_p, y_s = x_prompt, x_sample
    new_s, new_k, new_v = [], [], []
    for l in range(depth):
        p = dict(norm1_g=norm1_g[l], norm2_g=norm2_g[l], shift_mu=shift_mu[l], rw_w0=rw_w0[l],
                 rw_w_up=rw_w_up[l], rw_a0=rw_a0[l], rw_a_up=rw_a_up[l], rw_g_up=rw_g_up[l],
                 rw_k_k=rw_k_k[l], rw_k_a=rw_k_a[l], rw_r_k=rw_r_k[l], rw_ln_g=rw_ln_g[l],
                 rw_ln_b=rw_ln_b[l], na_q_g=na_q_g[l], na_k_g=na_k_g[l], na_rpb=na_rpb[l])
        late_f32 = [w[l] for w in (w_o_rwkv, w_o_na, w_out, ffn_w1, ffn_w3, ffn_w2)]
        mod_all, w_in_bf = _modulation(cvecs, w_ada[l], b_ada[l], w_in[l])
        mod_all = mod_all.reshape(1 + dec, 6, D_MODEL)
        u_rw, u_na, gates = _in_proj(tiled(y_p), tiled(y_s), tiles_per_b, mod_all, p["norm1_g"], w_in_bf)
        ctx_view = lambda t: t.reshape(-1, seq, t.shape[-1])
        lat_view = lambda t: t.reshape(-1, dec_seq, t.shape[-1])
        lat_off = n_ctx // dec_seq
        o_rw_p, s_l, casted = _rwkv_branch(ctx_view(u_rw), bsz, 0, None, p, RWKV_PAIRS_CTX, RWKV_UNITS_CTX, late_f32)
        late = dict(zip(LATE_WEIGHTS, casted))
        o_na_p, k_l, v_l = _na_context(ctx_view(u_na), bsz, p["na_q_g"], p["na_k_g"])
        new_s.append(s_l)
        new_k.append(k_l.reshape(bsz, seq, NA_HEADS, HEAD_DIM))
        new_v.append(v_l.reshape(bsz, seq, NA_HEADS, HEAD_DIM))
        ctx_k = cache_na_k[:, l].reshape(dec, -1, NA_WIDTH)
        ctx_v = cache_na_v[:, l].reshape(dec, -1, NA_WIDTH)
        o_rw_s, _, _ = _rwkv_branch(lat_view(u_rw), dec, lat_off, _state_to_big(state_rwkv[:, l]), p, RWKV_PAIRS_LAT,
                                   RWKV_UNITS_LAT)
        o_na_s = _na_latent(lat_view(u_na), dec, lat_off, ctx_k, ctx_v, p["na_q_g"], p["na_k_g"], p["na_rpb"])
        y_p_t, y_s_t = _out_ffn(tiled(y_p), tiled(y_s), tiled(o_rw_p), tiled(o_rw_s), tiled(o_na_p), tiled(o_na_s),
                                gates, tiles_per_b, mod_all, p["norm2_g"], late)
        y_p, y_s = y_p_t.reshape(x_prompt.shape), y_s_t.reshape(x_sample.shape)
    return (y_p, y_s, jnp.stack(new_s, axis=1), jnp.stack(new_k, axis=1), jnp.stack(new_v, axis=1))
```

```python
import functools

import numpy as np
import jax
import jax.numpy as jnp
from jax import lax
from jax.experimental import pallas as pl
from jax.experimental.pallas import tpu as pltpu

D_MODEL = 1024
GRID_W = 64
HEAD_DIM = 64
RW_HEADS = 8
RW_WIDTH = RW_HEADS * HEAD_DIM
NA_HEADS = 8
NA_WIDTH = NA_HEADS * HEAD_DIM
LORA_DECAY = 64
LORA_ICLR = 64
LORA_GATE = 128
NA_ROWS = 8
NA_COLS = 16
FF_HIDDEN = 2816
RW_COLS = 3 * RW_WIDTH + 2 * LORA_DECAY + 2 * LORA_ICLR + LORA_GATE
NA_IN_COLS = 3 * NA_WIDTH
GATE_COLS = 2 * D_MODEL
RMS_EPS = 1e-6
GN_EPS = 64e-5
L2_EPS = 1e-12
NEG_INF = -1e30
DECAY_SCALE = float(np.exp(-0.5))
QK_SCALE = HEAD_DIM ** -0.5
assert QK_SCALE == 0.125

LANES = 128
PAIRS = RW_HEADS // 2
CHUNK = 64
STACK = 2 * CHUNK
RWKV_PAIRS_CTX = 4
RWKV_PAIRS_LAT = 2
RWKV_UNITS_CTX = 16
RWKV_UNITS_LAT = 16
TOKEN_TILE = 512
FF_CHUNK = 256
VMEM_LIMIT = 56 * 1024 * 1024

F32 = jnp.float32
BF16 = jnp.bfloat16


def _dot(a, b):
    return jnp.dot(a.astype(BF16), b.astype(BF16), preferred_element_type=F32)


def _split2(x):
    hi = x.astype(BF16)
    lo = (x - hi.astype(F32)).astype(BF16)
    return hi, lo


def _dot_exact_lhs(a_exact, b):
    h, l = _split2(b)
    d = lambda x: jnp.dot(a_exact, x, preferred_element_type=F32)
    return d(h) + d(l)


def _head_ones():
    r = lax.broadcasted_iota(jnp.int32, (LANES, LANES), 0) // HEAD_DIM
    c = lax.broadcasted_iota(jnp.int32, (LANES, LANES), 1) // HEAD_DIM
    return jnp.where(r == c, 1.0, 0.0).astype(BF16)


def _head_sum(x, ones):
    return jnp.dot(x.astype(BF16), ones, preferred_element_type=F32)


def _sigmoid(x):
    return 0.5 * jnp.tanh(0.5 * x) + 0.5


def _rms_rows(x):
    return x * lax.rsqrt(jnp.mean(x * x, axis=-1, keepdims=True) + RMS_EPS)


def _const_spec(shape):
    nd = len(shape)
    return pl.BlockSpec(shape, lambda *_: (0,) * nd, pipeline_mode=pl.Buffered(1))


def _params(n_axes):
    return pltpu.CompilerParams(dimension_semantics=("arbitrary",) * n_axes,
                                vmem_limit_bytes=VMEM_LIMIT)


def _mod_kernel(c_ref, w_ref, b_ref, win_ref, o_ref, winb_ref):
    s = c_ref[...]
    s = s * _sigmoid(s)
    w = w_ref[...]
    for r in range(s.shape[1]):
        o_ref[r:r + 1, :] = jnp.sum(w * s[:, r:r + 1], axis=0, keepdims=True) + b_ref[...]
    winb_ref[...] = win_ref[...].astype(BF16)


def _modulation(cvecs, w_ada, b_ada, w_in):
    n = cvecs.shape[1]
    tn = 768
    steps = 6 * D_MODEL // tn
    rows = w_in.shape[0] // steps
    assert rows % 16 == 0
    return pl.pallas_call(
        _mod_kernel,
        grid=(steps,),
        in_specs=[pl.BlockSpec((D_MODEL, n), lambda j: (0, 0)),
                  pl.BlockSpec((D_MODEL, tn), lambda j: (0, j)),
                  pl.BlockSpec((1, tn), lambda j: (0, j)),
                  pl.BlockSpec((rows, w_in.shape[1]), lambda j: (j, 0))],
        out_specs=[pl.BlockSpec((n, tn), lambda j: (0, j)),
                   pl.BlockSpec((rows, w_in.shape[1]), lambda j: (j, 0))],
        out_shape=[jax.ShapeDtypeStruct((n, 6 * D_MODEL), F32),
                   jax.ShapeDtypeStruct(w_in.shape, BF16)],
        compiler_params=_params(1),
        name="modulation",
    )(cvecs, w_ada, b_ada.reshape(1, -1), w_in)


def _pick_group(i, tiles_a, a_ref, b_ref):
    return jnp.where(i < tiles_a, a_ref[0], b_ref[0])


def _group_specs(tiles_a, tm, width):
    return [pl.BlockSpec((1, tm, width), lambda i: (jnp.minimum(i, tiles_a - 1), 0, 0)),
            pl.BlockSpec((1, tm, width), lambda i: (jnp.maximum(i - tiles_a, 0), 0, 0))]


def _mod_spec(tiles_a, tiles_per_b):
    return pl.BlockSpec((1, 6, D_MODEL),
                        lambda i: (jnp.where(i < tiles_a, 0, 1 + (i - tiles_a) // tiles_per_b), 0, 0))


def _inproj_kernel(xa_ref, xb_ref, mod_ref, g_ref, w_ref, urw_ref, una_ref, gt_ref, *, tiles_a):
    x = _pick_group(pl.program_id(0), tiles_a, xa_ref, xb_ref)
    h = _rms_rows(x) * g_ref[...]
    h = (h * (1.0 + mod_ref[0, 1:2, :]) + mod_ref[0, 0:1, :]).astype(BF16)
    d = lambda lo, hi: jnp.dot(h, w_ref[:, lo:hi], preferred_element_type=F32)
    urw_ref[...] = d(0, RW_COLS)
    una_ref[...] = d(RW_COLS, RW_COLS + NA_IN_COLS)
    gt_ref[...] = d(RW_COLS + NA_IN_COLS, RW_COLS + NA_IN_COLS + GATE_COLS).astype(BF16)


def _in_proj(xa, xb, tiles_per_b, mod_all, norm_g, w_in_bf):
    tm = TOKEN_TILE
    tiles_a, tiles = xa.shape[0], xa.shape[0] + xb.shape[0]
    row = lambda i: (i, 0)
    return pl.pallas_call(
        functools.partial(_inproj_kernel, tiles_a=tiles_a),
        grid=(tiles,),
        in_specs=_group_specs(tiles_a, tm, D_MODEL) + [_mod_spec(tiles_a, tiles_per_b),
                                                       _const_spec((1, D_MODEL)), _const_spec(w_in_bf.shape)],
        out_specs=[pl.BlockSpec((tm, RW_COLS), row),
                   pl.BlockSpec((tm, NA_IN_COLS), row),
                   pl.BlockSpec((tm, GATE_COLS), row)],
        out_shape=[jax.ShapeDtypeStruct((tiles * tm, RW_COLS), F32),
                   jax.ShapeDtypeStruct((tiles * tm, NA_IN_COLS), F32),
                   jax.ShapeDtypeStruct((tiles * tm, GATE_COLS), BF16)],
        compiler_params=_params(1),
        name="in_proj",
    )(xa, xb, mod_all, norm_g.reshape(1, -1), w_in_bf)


def _shift(x, mu):
    t_len = x.shape[0]
    row = lax.broadcasted_iota(jnp.int32, x.shape, 0)
    prev = jnp.where(row == 0, 0.0, pltpu.roll(x, 1, 0))
    nxt = jnp.where(row == t_len - 1, 0.0, pltpu.roll(x, t_len - 1, 0))
    return x + mu[0:1, :] * (prev - x) + mu[1:2, :] * (nxt - x)


def _stack_heads(x, lane_lo):
    return jnp.concatenate([x * lane_lo, x * (1.0 - lane_lo)], axis=0)


def _wkv_intra(units, consts):
    tri, mask_s, mask_i, eye, lane_lo, blk = consts
    stack = lambda z: _stack_heads(z, lane_lo)

    cums = [_dot_exact_lhs(tri[int(u[6])], u[1]) for u in units]

    prep = []
    for (r, lw, kd, v, kk, b, reverse), cum in zip(units, cums):
        mid_row = CHUNK // 2 if reverse else CHUNK // 2 - 1
        tot_row = 0 if reverse else CHUNK - 1
        a = -kk
        ex = cum - lw
        mid = cum[mid_row:mid_row + 1, :]
        tot = cum[tot_row:tot_row + 1, :]
        up = jnp.exp(cum - mid)
        dn = jnp.exp(mid - cum)
        tail = jnp.exp(tot - cum)
        prep.append(dict(
            at_m=stack(a * jnp.exp(ex - mid)).astype(BF16),
            rt_m=stack(r * up).astype(BF16),
            btkt=jnp.concatenate([stack(b * dn), stack(kd * dn)], axis=0).astype(BF16),
            a_e=stack(a * jnp.exp(ex)),
            r_e=stack(r * jnp.exp(cum)),
            bk_t=jnp.concatenate([stack(b * tail), stack(kd * tail)], axis=0).T.astype(BF16),
            vv=stack(v).astype(BF16),
            diag=jnp.where(eye, jnp.exp(tot), 0.0),
            rev=int(reverse)))

    ntd = lambda x, y: lax.dot_general(x, y, (((1,), (1,)), ((), ())), preferred_element_type=F32)
    mm = lambda x, y: jnp.dot(x, y, preferred_element_type=F32)
    top = [mask_s[p["rev"]] * ntd(p["at_m"], p["btkt"]) for p in prep]
    bot = [(mask_i[p["rev"]] * ntd(p["rt_m"], p["btkt"])).astype(BF16) for p in prep]

    diag_blk, swap_eye = blk
    off_blk = 1.0 - diag_blk
    both = [t[:, :LANES] + swap_eye for t in top]
    steps = CHUNK.bit_length() - 1
    diag_bf = diag_blk.astype(BF16)
    for j in range(steps):
        packed = [q.astype(BF16) for q in both]
        res = [mm(qb * diag_bf, qb) for qb in packed]
        both = [r + off_blk * q for r, q in zip(res, both)]
    ts = [pltpu.roll(q, HEAD_DIM, 1).astype(BF16) for q in both]
    x0 = [jnp.concatenate([p["a_e"], mm(t[:, LANES:].astype(BF16), p["vv"])], axis=1) for p, t in zip(prep, top)]
    xs = [mm(t, x.astype(BF16)) for x, t in zip(x0, ts)]

    out = []
    zeros = jnp.zeros((STACK, LANES), BF16)
    for p, x, bt in zip(prep, xs, bot):
        rhs = jnp.concatenate([x.astype(BF16), jnp.concatenate([zeros, p["vv"]], axis=1)], axis=0)
        lhs = jnp.concatenate([bt, p["bk_t"]], axis=0)
        res = mm(lhs, rhs)
        lhs2 = res[:, :LANES] + jnp.concatenate([p["r_e"], p["diag"]], axis=0)
        out.append((lhs2.astype(BF16), res[:, LANES:]))
    return out


def _wkv_constants():
    lane_head = np.arange(LANES) // HEAD_DIM
    ones = (lane_head[:, None] == lane_head[None, :]).astype(np.float32)
    t = np.arange(CHUNK)
    tri = np.stack([t[None, :] <= t[:, None], t[None, :] >= t[:, None]]).astype(np.float32)
    s = np.arange(STACK)
    same = (s[:, None] // CHUNK) == (s[None, :] // CHUNK)
    rs, cs = s[:, None], s[None, :]
    masks = np.stack([same & (cs < rs), same & (cs > rs), same & (cs <= rs), same & (cs >= rs)]).astype(np.float32)
    masks = np.concatenate([masks, masks], axis=-1)
    return jnp.asarray(ones, BF16), jnp.asarray(tri, BF16), jnp.asarray(masks, F32)


def _rwkv_kernel(*refs, t_len, has_s0, pairs, units, n_cast):
    (r_ref, k_ref, v_ref, lo_ref, mur_ref, muk_ref, muv_ref, mul_ref, w0_ref, a0_ref, wup_ref, aup_ref,
     gup_ref, kk_ref, ka_ref, rk_ref, lng_ref, lnb_ref, ones_ref, tri_ref, mask_ref) = refs[:21]
    pos = 21
    s0_ref = None
    if has_s0:
        s0_ref = refs[pos]
        pos += 1
    cast_in = refs[pos:pos + n_cast]
    pos += n_cast
    o_ref, sn_ref = refs[pos], refs[pos + 1]
    cast_out = refs[pos + 2:pos + 2 + n_cast]
    (r_s, v_s, kk_s, b0_s, b1_s, lw0_s, lw1_s, kd0_s, kd1_s, gate_s, bonus_s, yf_s, yb_s,
     lhs_s, add_s, st_s) = refs[pos + 2 + n_cast:]
    for w_in_ref, w_out_ref in zip(cast_in, cast_out):
        w_out_ref[...] = w_in_ref[...].astype(BF16)

    ones = ones_ref[...]
    lane = lax.broadcasted_iota(jnp.int32, (1, LANES), 1)
    lo_half = lane < HEAD_DIM
    lane_lo = jnp.where(lo_half, 1.0, 0.0)
    mm = lambda x, y: jnp.dot(x, y, preferred_element_type=F32)

    lo = _shift(lo_ref[0], mul_ref[...])
    wd = jnp.tanh(lo[:, 0:LANES])
    ad = lo[:, LANES:2 * LANES]
    sig_gd = _sigmoid(lo[:, 2 * LANES:3 * LANES]).astype(BF16)
    wd_split = [_split2(wd * m) for m in (lane_lo, 1.0 - lane_lo)]
    ad_bf = [(ad * m).astype(BF16) for m in (lane_lo, 1.0 - lane_lo)]
    for j in range(pairs):
        cols = slice(j * LANES, (j + 1) * LANES)
        r = _shift(r_ref[0, :, cols], mur_ref[:, cols])
        k = _shift(k_ref[0, :, cols], muk_ref[:, cols])
        v = _shift(v_ref[0, :, cols], muv_ref[:, cols])
        kk = k * kk_ref[:, cols]
        kk = kk * lax.rsqrt(_head_sum(kk * kk, ones) + L2_EPS)
        wup_h, wup_l = _split2(wup_ref[:, cols])
        aup = aup_ref[:, cols].astype(BF16)
        kdirs = []
        for e, (lw_s, kd_s, b_s) in enumerate(((lw0_s, kd0_s, b0_s), (lw1_s, kd1_s, b1_s))):
            wd_h, wd_l = wd_split[e]
            w_lin = w0_ref[e:e + 1, cols] + (mm(wd_h, wup_h) + mm(wd_l, wup_h) + mm(wd_h, wup_l))
            lw_s[j] = -DECAY_SCALE * _sigmoid(w_lin)
            iclr = _sigmoid(a0_ref[e:e + 1, cols] + mm(ad_bf[e], aup))
            kd = k * (1.0 + (iclr - 1.0) * ka_ref[:, cols])
            kd_s[j] = kd
            b_s[j] = kk * iclr
            kdirs.append(kd)
        gate_s[:, cols] = mm(sig_gd, gup_ref[:, cols].astype(BF16))
        bonus_s[:, cols] = _head_sum(r * (0.5 * (kdirs[0] + kdirs[1])) * rk_ref[:, cols], ones) * v
        r_s[j] = r
        v_s[j] = v
        kk_s[j] = kk

    n_chunks = t_len // CHUNK
    chunks_per = min(n_chunks, units // 2)
    pairs_per = min(pairs, units // (2 * chunks_per))
    groups = n_chunks // chunks_per
    rs = lax.broadcasted_iota(jnp.int32, (STACK, STACK), 0)
    cs = lax.broadcasted_iota(jnp.int32, (STACK, STACK), 1)
    as_f32 = lambda m: jnp.where(m, 1.0, 0.0)
    blk = (as_f32(rs // CHUNK == cs // CHUNK), as_f32(cs == (rs + CHUNK) % STACK))
    consts = ((tri_ref[0], tri_ref[1]), (mask_ref[0], mask_ref[1]), (mask_ref[2], mask_ref[3]),
              rs == cs, lane_lo, blk)
    dirs = ((lw0_s, kd0_s, b0_s), (lw1_s, kd1_s, b1_s))

    def intra_body(it, carry):
        pg = it // groups
        g = it % groups
        units_, ids = [], []
        for jj in range(pairs_per):
            j = pg * pairs_per + jj
            for cc in range(chunks_per):
                c = g * chunks_per + cc
                rows = pl.ds(pl.multiple_of(c * CHUNK, CHUNK), CHUNK)
                for e, (lw_s, kd_s, b_s) in enumerate(dirs):
                    units_.append((r_s[j, rows, :], lw_s[j, rows, :], kd_s[j, rows, :], v_s[j, rows, :],
                                   kk_s[j, rows, :], b_s[j, rows, :], e == 1))
                    ids.append((j * 2 + e) * n_chunks + c)
        for uid, (lhs, add) in zip(ids, _wkv_intra(units_, consts)):
            lhs_s[uid] = lhs
            add_s[uid] = add
        return carry

    lax.fori_loop(0, (pairs // pairs_per) * groups, intra_body, 0)

    for j in range(pairs):
        for e in range(2):
            if has_s0:
                st_s[2 * j + e] = s0_ref[0, e, j].T
            else:
                st_s[2 * j + e] = jnp.zeros((LANES, LANES), F32)

    def state_body(it, carry):
        chunk = (it, n_chunks - 1 - it)
        uids = [(j * 2 + e) * n_chunks + chunk[e] for j in range(pairs) for e in range(2)]
        sts = [st_s[ch].astype(BF16) for ch in range(2 * pairs)]
        res = [mm(lhs_s[uid], st) + add_s[uid] for uid, st in zip(uids, sts)]
        for ch, rr in enumerate(res):
            j, e = divmod(ch, 2)
            y_s = yb_s if e else yf_s
            y_s[j, pl.ds(pl.multiple_of(chunk[e] * CHUNK, CHUNK), CHUNK), :] = rr[:CHUNK] + rr[CHUNK:STACK]
            st_s[ch] = rr[STACK:]
        return carry

    lax.fori_loop(0, n_chunks, state_body, 0)
    for j in range(pairs):
        for e in range(2):
            st_t = st_s[2 * j + e].T
            sn_ref[0, e, 2 * j] = st_t[:HEAD_DIM, :HEAD_DIM]
            sn_ref[0, e, 2 * j + 1] = pltpu.roll(st_t, HEAD_DIM, 1)[HEAD_DIM:, :HEAD_DIM]

    inv_d = 1.0 / HEAD_DIM
    for j in range(pairs):
        cols = slice(j * LANES, (j + 1) * LANES)
        y = yf_s[j] + yb_s[j]
        mean = _head_sum(y, ones) * inv_d
        dlt = y - mean
        var = _head_sum(dlt * dlt, ones) * inv_d
        yn = dlt * lax.rsqrt(var + GN_EPS) * lng_ref[:, cols] + lnb_ref[:, cols]
        o_ref[0, :, cols] = ((yn + bonus_s[:, cols]) * gate_s[:, cols]).astype(o_ref.dtype)


def _rwkv_branch(u_rw, bsz, b_off, s0_big, p, pairs, units, cast=()):
    t_len = u_rw.shape[1]
    has_s0 = s0_big is not None
    width = pairs * LANES
    seg = RW_WIDTH // width
    tok = lambda off: pl.BlockSpec((1, t_len, width), lambda b, j: (b_off + b, 0, off + j))
    mu = lambda off: pl.BlockSpec((2, width), lambda b, j: (0, off + j))
    vec2 = pl.BlockSpec((2, width), lambda b, j: (0, j))
    vec1 = pl.BlockSpec((1, width), lambda b, j: (0, j))
    mat = pl.BlockSpec((LANES, width), lambda b, j: (0, j))
    lora_w = 3 * LANES
    lora_blk = 3 * RW_WIDTH // lora_w
    in_specs = [tok(0), tok(seg), tok(2 * seg),
                pl.BlockSpec((1, t_len, lora_w), lambda b, j: (b_off + b, 0, lora_blk)),
                mu(0), mu(seg), mu(2 * seg),
                pl.BlockSpec((2, lora_w), lambda b, j: (0, lora_blk)),
                vec2, vec2, mat, mat, mat, vec1, vec1, vec1, vec1, vec1,
                _const_spec((LANES, LANES)), _const_spec((2, CHUNK, CHUNK)), _const_spec((4, STACK, 2 * STACK))]
    args = [u_rw, u_rw, u_rw, u_rw, p["shift_mu"], p["shift_mu"], p["shift_mu"], p["shift_mu"],
            p["rw_w0"], p["rw_a0"],
            p["rw_w_up"].reshape(2 * LORA_DECAY, RW_WIDTH), p["rw_a_up"].reshape(2 * LORA_ICLR, RW_WIDTH),
            p["rw_g_up"], p["rw_k_k"].reshape(1, -1), p["rw_k_a"].reshape(1, -1),
            p["rw_r_k"].reshape(1, -1), p["rw_ln_g"].reshape(1, -1), p["rw_ln_b"].reshape(1, -1),
            *_wkv_constants()]
    st_spec = pl.BlockSpec((1, 2, pairs, LANES, LANES), lambda b, j: (b, 0, j, 0, 0))
    if has_s0:
        in_specs.append(st_spec)
        args.append(s0_big)
    out_specs = [pl.BlockSpec((1, t_len, width), lambda b, j: (b, 0, j)),
                 pl.BlockSpec((1, 2, 2 * pairs, HEAD_DIM, HEAD_DIM), lambda b, j: (b, 0, j, 0, 0))]
    out_shape = [jax.ShapeDtypeStruct((bsz, t_len, RW_WIDTH), BF16),
                 jax.ShapeDtypeStruct((bsz, 2, RW_HEADS, HEAD_DIM, HEAD_DIM), F32)]
    for w in cast:
        assert PAIRS == pairs and w.shape[0] % (16 * bsz) == 0
        blk = pl.BlockSpec((w.shape[0] // bsz, w.shape[1]), lambda b, j: (b, 0))
        in_specs.append(blk)
        args.append(w)
        out_specs.append(blk)
        out_shape.append(jax.ShapeDtypeStruct(w.shape, BF16))
    n_units = 2 * pairs * (t_len // CHUNK)
    per_pair = pltpu.VMEM((pairs, t_len, LANES), F32)
    full = pltpu.VMEM((t_len, width), F32)
    scratch = [per_pair] * 9 + [full, full, per_pair, per_pair,
                                pltpu.VMEM((n_units, 2 * STACK, LANES), BF16),
                                pltpu.VMEM((n_units, 2 * STACK, LANES), F32),
                                pltpu.VMEM((2 * pairs, LANES, LANES), F32)]
    o_rw, s_new, *casted = pl.pallas_call(
        functools.partial(_rwkv_kernel, t_len=t_len, has_s0=has_s0, pairs=pairs, units=units, n_cast=len(cast)),
        grid=(bsz, PAIRS // pairs),
        in_specs=in_specs,
        out_specs=out_specs,
        out_shape=out_shape,
        scratch_shapes=scratch,
        compiler_params=_params(2),
        name="rwkv_branch",
    )(*args)
    return o_rw, s_new, casted


def _state_to_big(s0):
    bsz = s0.shape[0]
    x = s0.reshape(bsz, 2, PAIRS, 2, HEAD_DIM, 1, HEAD_DIM)
    same_head = np.eye(2, dtype=bool).reshape(2, 1, 2, 1)
    return jnp.where(same_head, x, 0.0).reshape(bsz, 2, PAIRS, LANES, LANES)


def _qk_norm(t, g, ones):
    ms = _head_sum(t * t, ones) * (1.0 / HEAD_DIM)
    return t * lax.rsqrt(ms + RMS_EPS) * g


def _nt(x, y):
    return lax.dot_general(x, y, (((1,), (1,)), ((), ())), preferred_element_type=F32)


def _na_ctx_kernel(q_ref, k_ref, v_ref, qg_ref, kg_ref, o_ref, kn_ref, vc_ref):
    ones = _head_ones()
    lo_half = lax.broadcasted_iota(jnp.int32, (1, LANES), 1) < HEAD_DIM
    lo = jnp.where(lo_half, 1.0, 0.0)
    t_len = q_ref.shape[1]
    qs, ks, vs = [], [], []
    for j in range(NA_WIDTH // LANES):
        cols = slice(j * LANES, (j + 1) * LANES)
        qn = _qk_norm(q_ref[0, :, cols], qg_ref[...], ones)
        kn = _qk_norm(k_ref[0, :, cols], kg_ref[...], ones)
        v = v_ref[0, :, cols]
        kn_ref[0, :, cols] = kn
        vc_ref[0, :, cols] = v
        qn = qn * QK_SCALE
        qs.append(jnp.concatenate([qn * lo, qn * (1.0 - lo)], axis=0).astype(BF16))
        ks.append(kn.astype(BF16))
        vs.append(v.astype(BF16))
    logits = [_nt(q, k) for q, k in zip(qs, ks)]
    ms = [jnp.max(s, axis=-1, keepdims=True) for s in logits]
    ps = [jnp.exp(s - m) for s, m in zip(logits, ms)]
    ls = [jnp.sum(p, axis=-1, keepdims=True) for p in ps]
    outs = [jnp.dot(p.astype(BF16), v, preferred_element_type=F32) / l for p, v, l in zip(ps, vs, ls)]
    for j, o in enumerate(outs):
        o_ref[0, :, j * LANES:(j + 1) * LANES] = jnp.where(lo_half, o[:t_len], o[t_len:]).astype(o_ref.dtype)


def _na_context(u_na, bsz, q_g, k_g):
    t_len = u_na.shape[1]
    tok = lambda seg: pl.BlockSpec((1, t_len, NA_WIDTH), lambda b: (b, 0, seg))
    out_blk = pl.BlockSpec((1, t_len, NA_WIDTH), lambda b: (b, 0, 0))
    g2 = lambda g: jnp.tile(g.reshape(1, HEAD_DIM), (1, 2))
    shp = jax.ShapeDtypeStruct((bsz, t_len, NA_WIDTH), F32)
    return pl.pallas_call(
        _na_ctx_kernel,
        grid=(bsz,),
        in_specs=[tok(0), tok(1), tok(2), _const_spec((1, LANES)), _const_spec((1, LANES))],
        out_specs=[out_blk, out_blk, out_blk],
        out_shape=[jax.ShapeDtypeStruct(shp.shape, BF16), shp, shp],
        compiler_params=_params(1),
        name="na_context",
    )(u_na, u_na, u_na, g2(q_g), g2(k_g))


NA_ROW_ILP = 8


def _na_lat_kernel(q_ref, k_ref, v_ref, kc_ref, vc_ref, qg_ref, kg_ref, tab_ref, o_ref,
                   q0_s, q1_s, kn_s, v_s, kc_s, vc_s, *, rows, kr):
    ones = _head_ones()
    lo_half = lax.broadcasted_iota(jnp.int32, (1, LANES), 1) < HEAD_DIM
    lo = jnp.where(lo_half, 1.0, 0.0)
    qn = _qk_norm(q_ref[0], qg_ref[...], ones) * QK_SCALE
    q0_s[...] = (qn * lo).astype(BF16)
    q1_s[...] = (qn * (1.0 - lo)).astype(BF16)
    kn_s[...] = _qk_norm(k_ref[0], kg_ref[...], ones).astype(BF16)
    v_s[...] = v_ref[0].astype(BF16)
    kc_s[...] = kc_ref[0].astype(BF16)
    vc_s[...] = vc_ref[0].astype(BF16)
    win = kr * GRID_W

    def body(it, carry):
        qs, k_rows, q_rows, biases = [], [], [], []
        for s in range(NA_ROW_ILP):
            i = it * NA_ROW_ILP + s
            r0 = jnp.clip(i - kr // 2, 0, rows - kr)
            d0 = r0 - i + (NA_ROWS - 1)
            qr = pl.ds(pl.multiple_of(i * GRID_W, GRID_W), GRID_W)
            q_rows.append(qr)
            k_rows.append(pl.ds(pl.multiple_of(r0 * GRID_W, GRID_W), win))
            qs.append(jnp.concatenate([q0_s[qr, :], q1_s[qr, :]], axis=0))
            biases.append(jnp.concatenate(
                [jnp.concatenate([tab_ref[h, d0 + 2 * m] for m in range(kr // 2)], axis=1) for h in range(2)],
                axis=0))
        lw = [_nt(q, kn_s[kr_, :]) + b for q, kr_, b in zip(qs, k_rows, biases)]
        lc = [_nt(q, kc_s[...]) for q in qs]
        ms = [jnp.maximum(jnp.max(a, axis=-1, keepdims=True), jnp.max(c, axis=-1, keepdims=True))
              for a, c in zip(lw, lc)]
        pw = [jnp.exp(a - m) for a, m in zip(lw, ms)]
        pc = [jnp.exp(c - m) for c, m in zip(lc, ms)]
        ls = [jnp.sum(a, axis=-1, keepdims=True) + jnp.sum(c, axis=-1, keepdims=True) for a, c in zip(pw, pc)]
        outs = [(jnp.dot(a.astype(BF16), v_s[kr_, :], preferred_element_type=F32)
                 + jnp.dot(c.astype(BF16), vc_s[...], preferred_element_type=F32)) / l
                for a, c, kr_, l in zip(pw, pc, k_rows, ls)]
        for qr, o in zip(q_rows, outs):
            o_ref[0, qr, :] = jnp.where(lo_half, o[:GRID_W], o[GRID_W:]).astype(o_ref.dtype)
        return carry

    lax.fori_loop(0, rows // NA_ROW_ILP, body, 0)


def _latent_bias_table(rpb):
    qc = np.arange(GRID_W)[:, None]
    kc = np.arange(GRID_W)[None, :]
    ws = np.clip(qc - NA_COLS // 2, 0, GRID_W - NA_COLS)
    valid = (kc >= ws) & (kc < ws + NA_COLS)
    dc = np.clip(kc - qc, -(NA_COLS - 1), NA_COLS - 1) + NA_COLS - 1
    onehot = (dc[None] == np.arange(2 * NA_COLS - 1)[:, None, None]).astype(np.float32)
    cb = jnp.einsum("hdc,cqk->hdqk", rpb, jnp.asarray(onehot), precision=lax.Precision.HIGHEST)
    cb = jnp.where(valid[None, None], cb, NEG_INF)
    return jnp.concatenate([cb[:, :-1], cb[:, 1:]], axis=-1)


def _na_latent(u_na, bsz, b_off, k_ctx, v_ctx, q_g, k_g, rpb):
    t_len = u_na.shape[1]
    rows = t_len // GRID_W
    kr = min(NA_ROWS, rows)
    assert kr % 2 == 0 and rows % NA_ROW_ILP == 0
    ctx_len = k_ctx.shape[1]
    seg = NA_WIDTH // LANES
    tok = lambda off: pl.BlockSpec((1, t_len, LANES), lambda b, j: (b_off + b, 0, off + j))
    ctx = pl.BlockSpec((1, ctx_len, LANES), lambda b, j: (b, 0, j))
    g2 = lambda g: jnp.tile(g.reshape(1, HEAD_DIM), (1, 2))
    tab = _latent_bias_table(rpb)
    tok_s = pltpu.VMEM((t_len, LANES), BF16)
    ctx_s = pltpu.VMEM((ctx_len, LANES), BF16)
    return pl.pallas_call(
        functools.partial(_na_lat_kernel, rows=rows, kr=kr),
        grid=(bsz, seg),
        in_specs=[tok(0), tok(seg), tok(2 * seg), ctx, ctx,
                  _const_spec((1, LANES)), _const_spec((1, LANES)),
                  pl.BlockSpec((2, 2 * NA_ROWS - 2, GRID_W, 2 * GRID_W), lambda b, j: (j, 0, 0, 0))],
        out_specs=pl.BlockSpec((1, t_len, LANES), lambda b, j: (b, 0, j)),
        out_shape=jax.ShapeDtypeStruct((bsz, t_len, NA_WIDTH), BF16),
        scratch_shapes=[tok_s, tok_s, tok_s, tok_s, ctx_s, ctx_s],
        compiler_params=_params(2),
        name="na_latent",
    )(u_na, u_na, u_na, k_ctx, v_ctx, g2(q_g), g2(k_g), tab)


LATE_WEIGHTS = ("w_o_rwkv", "w_o_na", "w_out", "ffn_w1", "ffn_w3", "ffn_w2")


def _out_ffn_kernel(xa_ref, xb_ref, orwa_ref, orwb_ref, onaa_ref, onab_ref, gt_ref, mod_ref, g_ref,
                    wor_ref, won_ref, wout_ref, w1_ref, w3_ref, w2_ref, ya_ref, yb_ref, *, tiles_a):
    i = pl.program_id(0)
    x = _pick_group(i, tiles_a, xa_ref, xb_ref)
    o_rw = _pick_group(i, tiles_a, orwa_ref, orwb_ref)
    o_na = _pick_group(i, tiles_a, onaa_ref, onab_ref)
    g_rw = _sigmoid(gt_ref[:, :D_MODEL].astype(F32))
    g_na = _sigmoid(gt_ref[:, D_MODEL:].astype(F32))
    merged = g_rw * _dot(o_rw, wor_ref[...]) + g_na * _dot(o_na, won_ref[...])
    x1 = x + mod_ref[0, 2:3, :] * _dot(merged, wout_ref[...])
    h2 = _rms_rows(x1) * g_ref[...]
    h2 = (h2 * (1.0 + mod_ref[0, 4:5, :]) + mod_ref[0, 3:4, :]).astype(BF16)
    acc = jnp.zeros(x1.shape, F32)
    for c in range(FF_HIDDEN // FF_CHUNK):
        cols = slice(c * FF_CHUNK, (c + 1) * FF_CHUNK)
        a = jnp.dot(h2, w1_ref[:, cols], preferred_element_type=F32)
        b = jnp.dot(h2, w3_ref[:, cols], preferred_element_type=F32)
        hh = (a * _sigmoid(a) * b).astype(BF16)
        acc = acc + jnp.dot(hh, w2_ref[cols, :], preferred_element_type=F32)
    y = x1 + mod_ref[0, 5:6, :] * acc

    @pl.when(i < tiles_a)
    def _():
        ya_ref[0] = y

    @pl.when(i >= tiles_a)
    def _():
        yb_ref[0] = y


def _out_ffn(xa, xb, orw_a, orw_b, ona_a, ona_b, gates, tiles_per_b, mod_all, norm_g, wb):
    tm = TOKEN_TILE
    tiles_a, tiles = xa.shape[0], xa.shape[0] + xb.shape[0]
    weights = [wb[n] for n in LATE_WEIGHTS]
    groups = lambda width: _group_specs(tiles_a, tm, width)
    return pl.pallas_call(
        functools.partial(_out_ffn_kernel, tiles_a=tiles_a),
        grid=(tiles,),
        in_specs=groups(D_MODEL) + groups(RW_WIDTH) + groups(NA_WIDTH)
        + [pl.BlockSpec((tm, GATE_COLS), lambda i: (i, 0)), _mod_spec(tiles_a, tiles_per_b),
           _const_spec((1, D_MODEL))] + [_const_spec(w.shape) for w in weights],
        out_specs=groups(D_MODEL),
        out_shape=[jax.ShapeDtypeStruct(xa.shape, F32), jax.ShapeDtypeStruct(xb.shape, F32)],
        compiler_params=_params(1),
        name="out_ffn",
    )(xa, xb, orw_a, orw_b, ona_a, ona_b, gates, mod_all, norm_g.reshape(1, -1), *weights)


def kernel(x_prompt, x_sample, state_rwkv, cache_na_k, cache_na_v, c, c_ctx, norm1_g, norm2_g, w_ada, b_ada,
           w_in, shift_mu, rw_w0, rw_w_up, rw_a0, rw_a_up, rw_g_up, rw_k_k, rw_k_a, rw_r_k, rw_ln_g, rw_ln_b,
           na_q_g, na_k_g, na_rpb, w_o_rwkv, w_o_na, w_out, ffn_w1, ffn_w3, ffn_w2):
    depth = w_in.shape[0]
    bsz, seq = x_prompt.shape[:2]
    dec, dec_seq = x_sample.shape[:2]
    tm = TOKEN_TILE
    n_ctx = bsz * seq
    assert n_ctx % tm == 0 and dec_seq % tm == 0 and n_ctx % dec_seq == 0
    tiles_per_b = dec_seq // tm
    tiled = lambda t: t.reshape(-1, tm, t.shape[-1])
    cvecs = jnp.concatenate([c_ctx[None, :], c], axis=0).T
    y_p, y_s = x_prompt, x_sample
    new_s, new_k, new_v = [], [], []
    for l in range(depth):
        p = dict(norm1_g=norm1_g[l], norm2_g=norm2_g[l], shift_mu=shift_mu[l], rw_w0=rw_w0[l],
                 rw_w_up=rw_w_up[l], rw_a0=rw_a0[l], rw_a_up=rw_a_up[l], rw_g_up=rw_g_up[l],
                 rw_k_k=rw_k_k[l], rw_k_a=rw_k_a[l], rw_r_k=rw_r_k[l], rw_ln_g=rw_ln_g[l],
                 rw_ln_b=rw_ln_b[l], na_q_g=na_q_g[l], na_k_g=na_k_g[l], na_rpb=na_rpb[l])
        late_f32 = [w[l] for w in (w_o_rwkv, w_o_na, w_out, ffn_w1, ffn_w3, ffn_w2)]
        mod_all, w_in_bf = _modulation(cvecs, w_ada[l], b_ada[l], w_in[l])
        mod_all = mod_all.reshape(1 + dec, 6, D_MODEL)
        u_rw, u_na, gates = _in_proj(tiled(y_p), tiled(y_s), tiles_per_b, mod_all, p["norm1_g"], w_in_bf)
        ctx_view = lambda t: t.reshape(-1, seq, t.shape[-1])
        lat_view = lambda t: t.reshape(-1, dec_seq, t.shape[-1])
        lat_off = n_ctx // dec_seq
        o_rw_p, s_l, casted = _rwkv_branch(ctx_view(u_rw), bsz, 0, None, p, RWKV_PAIRS_CTX, RWKV_UNITS_CTX, late_f32)
        late = dict(zip(LATE_WEIGHTS, casted))
        o_na_p, k_l, v_l = _na_context(ctx_view(u_na), bsz, p["na_q_g"], p["na_k_g"])
        new_s.append(s_l)
        new_k.append(k_l.reshape(bsz, seq, NA_HEADS, HEAD_DIM))
        new_v.append(v_l.reshape(bsz, seq, NA_HEADS, HEAD_DIM))
        ctx_k = cache_na_k[:, l].reshape(dec, -1, NA_WIDTH)
        ctx_v = cache_na_v[:, l].reshape(dec, -1, NA_WIDTH)
        o_rw_s, _, _ = _rwkv_branch(lat_view(u_rw), dec, lat_off, _state_to_big(state_rwkv[:, l]), p, RWKV_PAIRS_LAT,
                                   RWKV_UNITS_LAT)
        o_na_s = _na_latent(lat_view(u_na), dec, lat_off, ctx_k, ctx_v, p["na_q_g"], p["na_k_g"], p["na_rpb"])
        y_p_t, y_s_t = _out_ffn(tiled(y_p), tiled(y_s), tiled(o_rw_p), tiled(o_rw_s), tiled(o_na_p), tiled(o_na_s),
                                gates, tiles_per_b, mod_all, p["norm2_g"], late)
        y_p, y_s = y_p_t.reshape(x_prompt.shape), y_s_t.reshape(x_sample.shape)
    return (y_p, y_s, jnp.stack(new_s, axis=1), jnp.stack(new_k, axis=1), jnp.stack(new_v, axis=1))
```

```python
import functools

import numpy as np
import jax
import jax.numpy as jnp
from jax import lax
from jax.experimental import pallas as pl
from jax.experimental.pallas import tpu as pltpu

D_MODEL = 1024
GRID_W = 64
HEAD_DIM = 64
RW_HEADS = 8
RW_WIDTH = RW_HEADS * HEAD_DIM
NA_HEADS = 8
NA_WIDTH = NA_HEADS * HEAD_DIM
LORA_DECAY = 64
LORA_ICLR = 64
LORA_GATE = 128
NA_ROWS = 8
NA_COLS = 16
FF_HIDDEN = 2816
RW_COLS = 3 * RW_WIDTH + 2 * LORA_DECAY + 2 * LORA_ICLR + LORA_GATE
NA_IN_COLS = 3 * NA_WIDTH
GATE_COLS = 2 * D_MODEL
RMS_EPS = 1e-6
GN_EPS = 64e-5
L2_EPS = 1e-12
NEG_INF = -1e30
DECAY_SCALE = float(np.exp(-0.5))
QK_SCALE = HEAD_DIM ** -0.5
assert QK_SCALE == 0.125

LANES = 128
PAIRS = RW_HEADS // 2
CHUNK = 64
STACK = 2 * CHUNK
RWKV_PAIRS_CTX = 4
RWKV_PAIRS_LAT = 2
RWKV_UNITS_CTX = 16
RWKV_UNITS_LAT = 16
TOKEN_TILE = 512
FF_CHUNK = 256
VMEM_LIMIT = 56 * 1024 * 1024

F32 = jnp.float32
BF16 = jnp.bfloat16


def _dot(a, b):
    return jnp.dot(a.astype(BF16), b.astype(BF16), preferred_element_type=F32)


def _split2(x):
    hi = x.astype(BF16)
    lo = (x - hi.astype(F32)).astype(BF16)
    return hi, lo


def _dot_exact_lhs(a_exact, b):
    h, l = _split2(b)
    d = lambda x: jnp.dot(a_exact, x, preferred_element_type=F32)
    return d(h) + d(l)


def _head_ones():
    r = lax.broadcasted_iota(jnp.int32, (LANES, LANES), 0) // HEAD_DIM
    c = lax.broadcasted_iota(jnp.int32, (LANES, LANES), 1) // HEAD_DIM
    return jnp.where(r == c, 1.0, 0.0).astype(BF16)


def _head_sum(x, ones):
    return jnp.dot(x.astype(BF16), ones, preferred_element_type=F32)


def _sigmoid(x):
    return 0.5 * jnp.tanh(0.5 * x) + 0.5


def _rms_rows(x):
    return x * lax.rsqrt(jnp.mean(x * x, axis=-1, keepdims=True) + RMS_EPS)


def _const_spec(shape):
    nd = len(shape)
    return pl.BlockSpec(shape, lambda *_: (0,) * nd, pipeline_mode=pl.Buffered(1))


def _params(n_axes):
    return pltpu.CompilerParams(dimension_semantics=("arbitrary",) * n_axes,
                                vmem_limit_bytes=VMEM_LIMIT)


def _mod_kernel(c_ref, w_ref, b_ref, win_ref, o_ref, winb_ref):
    s = c_ref[...]
    s = s * _sigmoid(s)
    w = w_ref[...]
    for r in range(s.shape[1]):
        o_ref[r:r + 1, :] = jnp.sum(w * s[:, r:r + 1], axis=0, keepdims=True) + b_ref[...]
    winb_ref[...] = win_ref[...].astype(BF16)


def _modulation(cvecs, w_ada, b_ada, w_in):
    n = cvecs.shape[1]
    tn = 768
    steps = 6 * D_MODEL // tn
    rows = w_in.shape[0] // steps
    assert rows % 16 == 0
    return pl.pallas_call(
        _mod_kernel,
        grid=(steps,),
        in_specs=[pl.BlockSpec((D_MODEL, n), lambda j: (0, 0)),
                  pl.BlockSpec((D_MODEL, tn), lambda j: (0, j)),
                  pl.BlockSpec((1, tn), lambda j: (0, j)),
                  pl.BlockSpec((rows, w_in.shape[1]), lambda j: (j, 0))],
        out_specs=[pl.BlockSpec((n, tn), lambda j: (0, j)),
                   pl.BlockSpec((rows, w_in.shape[1]), lambda j: (j, 0))],
        out_shape=[jax.ShapeDtypeStruct((n, 6 * D_MODEL), F32),
                   jax.ShapeDtypeStruct(w_in.shape, BF16)],
        compiler_params=_params(1),
        name="modulation",
    )(cvecs, w_ada, b_ada.reshape(1, -1), w_in)


def _pick_group(i, tiles_a, a_ref, b_ref):
    return jnp.where(i < tiles_a, a_ref[0], b_ref[0])


def _group_specs(tiles_a, tm, width):
    return [pl.BlockSpec((1, tm, width), lambda i: (jnp.minimum(i, tiles_a - 1), 0, 0)),
            pl.BlockSpec((1, tm, width), lambda i: (jnp.maximum(i - tiles_a, 0), 0, 0))]


def _mod_spec(tiles_a, tiles_per_b):
    return pl.BlockSpec((1, 6, D_MODEL),
                        lambda i: (jnp.where(i < tiles_a, 0, 1 + (i - tiles_a) // tiles_per_b), 0, 0))


def _inproj_kernel(xa_ref, xb_ref, mod_ref, g_ref, w_ref, urw_ref, una_ref, gt_ref, *, tiles_a):
    x = _pick_group(pl.program_id(0), tiles_a, xa_ref, xb_ref)
    h = _rms_rows(x) * g_ref[...]
    h = (h * (1.0 + mod_ref[0, 1:2, :]) + mod_ref[0, 0:1, :]).astype(BF16)
    d = lambda lo, hi: jnp.dot(h, w_ref[:, lo:hi], preferred_element_type=F32)
    urw_ref[...] = d(0, RW_COLS)
    una_ref[...] = d(RW_COLS, RW_COLS + NA_IN_COLS)
    gt_ref[...] = d(RW_COLS + NA_IN_COLS, RW_COLS + NA_IN_COLS + GATE_COLS).astype(BF16)


def _in_proj(xa, xb, tiles_per_b, mod_all, norm_g, w_in_bf):
    tm = TOKEN_TILE
    tiles_a, tiles = xa.shape[0], xa.shape[0] + xb.shape[0]
    row = lambda i: (i, 0)
    return pl.pallas_call(
        functools.partial(_inproj_kernel, tiles_a=tiles_a),
        grid=(tiles,),
        in_specs=_group_specs(tiles_a, tm, D_MODEL) + [_mod_spec(tiles_a, tiles_per_b),
                                                       _const_spec((1, D_MODEL)), _const_spec(w_in_bf.shape)],
        out_specs=[pl.BlockSpec((tm, RW_COLS), row),
                   pl.BlockSpec((tm, NA_IN_COLS), row),
                   pl.BlockSpec((tm, GATE_COLS), row)],
        out_shape=[jax.ShapeDtypeStruct((tiles * tm, RW_COLS), F32),
                   jax.ShapeDtypeStruct((tiles * tm, NA_IN_COLS), F32),
                   jax.ShapeDtypeStruct((tiles * tm, GATE_COLS), BF16)],
        compiler_params=_params(1),
        name="in_proj",
    )(xa, xb, mod_all, norm_g.reshape(1, -1), w_in_bf)


def _shift(x, mu):
    t_len = x.shape[0]
    row = lax.broadcasted_iota(jnp.int32, x.shape, 0)
    prev = jnp.where(row == 0, 0.0, pltpu.roll(x, 1, 0))
    nxt = jnp.where(row == t_len - 1, 0.0, pltpu.roll(x, t_len - 1, 0))
    return x + mu[0:1, :] * (prev - x) + mu[1:2, :] * (nxt - x)


def _stack_heads(x, lane_lo):
    return jnp.concatenate([x * lane_lo, x * (1.0 - lane_lo)], axis=0)


def _wkv_intra(units, consts):
    tri, mask_s, mask_i, eye, lane_lo, blk = consts
    stack = lambda z: _stack_heads(z, lane_lo)

    cums = [_dot_exact_lhs(tri[int(u[6])], u[1]) for u in units]

    prep = []
    for (r, lw, kd, v, kk, b, reverse), cum in zip(units, cums):
        mid_row = CHUNK // 2 if reverse else CHUNK // 2 - 1
        tot_row = 0 if reverse else CHUNK - 1
        a = -kk
        ex = cum - lw
        mid = cum[mid_row:mid_row + 1, :]
        tot = cum[tot_row:tot_row + 1, :]
        up = jnp.exp(cum - mid)
        dn = jnp.exp(mid - cum)
        tail = jnp.exp(tot - cum)
        prep.append(dict(
            at_m=stack(a * jnp.exp(ex - mid)).astype(BF16),
            rt_m=stack(r * up).astype(BF16),
            bk_m=jnp.concatenate([b * dn, kd * dn], axis=0).astype(BF16),
            a_e=stack(a * jnp.exp(ex)),
            r_e=stack(r * jnp.exp(cum)),
            bk_t=jnp.concatenate([stack(b * tail), stack(kd * tail)], axis=0).T.astype(BF16),
            vv=stack(v).astype(BF16),
            diag=jnp.where(eye, jnp.exp(tot), 0.0),
            rev=int(reverse)))

    ntd = lambda x, y: lax.dot_general(x, y, (((1,), (1,)), ((), ())), preferred_element_type=F32)
    mm = lambda x, y: jnp.dot(x, y, preferred_element_type=F32)
    diag_blk, swap_eye = blk
    roll_head = lambda z: pltpu.roll(z, HEAD_DIM, 1)
    by_block = lambda z: jnp.concatenate([z[:CHUNK], roll_head(z[CHUNK:])], axis=0)
    top = [by_block(ntd(p["at_m"], p["bk_m"])) for p in prep]
    low = [by_block(ntd(p["rt_m"], p["bk_m"])) for p in prep]
    a_ak = [(roll_head(t) * mask_s[p["rev"]]).astype(BF16) for t, p in zip(top, prep)]
    bot = [jnp.concatenate([t * mask_i[p["rev"]], roll_head(t) * mask_i[p["rev"]]], axis=1).astype(BF16)
           for t, p in zip(low, prep)]

    off_blk = 1.0 - diag_blk
    both = [t * mask_s[p["rev"]] + swap_eye for t, p in zip(top, prep)]
    steps = CHUNK.bit_length() - 1
    diag_bf = diag_blk.astype(BF16)
    for j in range(steps):
        packed = [q.astype(BF16) for q in both]
        res = [mm(qb * diag_bf, qb) for qb in packed]
        both = [r + off_blk * q for r, q in zip(res, both)]
    ts = [pltpu.roll(q, HEAD_DIM, 1).astype(BF16) for q in both]
    x0 = [jnp.concatenate([p["a_e"], mm(ak, p["vv"])], axis=1) for p, ak in zip(prep, a_ak)]
    xs = [mm(t, x.astype(BF16)) for x, t in zip(x0, ts)]

    out = []
    zeros = jnp.zeros((STACK, LANES), BF16)
    for p, x, bt in zip(prep, xs, bot):
        rhs = jnp.concatenate([x.astype(BF16), jnp.concatenate([zeros, p["vv"]], axis=1)], axis=0)
        lhs = jnp.concatenate([bt, p["bk_t"]], axis=0)
        res = mm(lhs, rhs)
        lhs2 = res[:, :LANES] + jnp.concatenate([p["r_e"], p["diag"]], axis=0)
        out.append((lhs2.astype(BF16), res[:, LANES:]))
    return out


def _wkv_constants():
    lane_head = np.arange(LANES) // HEAD_DIM
    ones = (lane_head[:, None] == lane_head[None, :]).astype(np.float32)
    t = np.arange(CHUNK)
    tri = np.stack([t[None, :] <= t[:, None], t[None, :] >= t[:, None]]).astype(np.float32)
    rs, cs = t[:, None], t[None, :]
    tri_masks = np.stack([cs < rs, cs > rs, cs <= rs, cs >= rs]).astype(np.float32)
    masks = np.kron(np.eye(2, dtype=np.float32), tri_masks)
    return jnp.asarray(ones, BF16), jnp.asarray(tri, BF16), jnp.asarray(masks, F32)


def _rwkv_kernel(*refs, t_len, has_s0, pairs, units, n_cast):
    (r_ref, k_ref, v_ref, lo_ref, mur_ref, muk_ref, muv_ref, mul_ref, w0_ref, a0_ref, wup_ref, aup_ref,
     gup_ref, kk_ref, ka_ref, rk_ref, lng_ref, lnb_ref, ones_ref, tri_ref, mask_ref) = refs[:21]
    pos = 21
    s0_ref = None
    if has_s0:
        s0_ref = refs[pos]
        pos += 1
    cast_in = refs[pos:pos + n_cast]
    pos += n_cast
    o_ref, sn_ref = refs[pos], refs[pos + 1]
    cast_out = refs[pos + 2:pos + 2 + n_cast]
    (r_s, v_s, kk_s, b0_s, b1_s, lw0_s, lw1_s, kd0_s, kd1_s, gate_s, bonus_s, yf_s, yb_s,
     lhs_s, add_s, st_s) = refs[pos + 2 + n_cast:]
    for w_in_ref, w_out_ref in zip(cast_in, cast_out):
        w_out_ref[...] = w_in_ref[...].astype(BF16)

    ones = ones_ref[...]
    lane = lax.broadcasted_iota(jnp.int32, (1, LANES), 1)
    lo_half = lane < HEAD_DIM
    lane_lo = jnp.where(lo_half, 1.0, 0.0)
    mm = lambda x, y: jnp.dot(x, y, preferred_element_type=F32)

    lo = _shift(lo_ref[0], mul_ref[...])
    wd = jnp.tanh(lo[:, 0:LANES])
    ad = lo[:, LANES:2 * LANES]
    sig_gd = _sigmoid(lo[:, 2 * LANES:3 * LANES]).astype(BF16)
    wd_split = [_split2(wd * m) for m in (lane_lo, 1.0 - lane_lo)]
    ad_bf = [(ad * m).astype(BF16) for m in (lane_lo, 1.0 - lane_lo)]
    for j in range(pairs):
        cols = slice(j * LANES, (j + 1) * LANES)
        r = _shift(r_ref[0, :, cols], mur_ref[:, cols])
        k = _shift(k_ref[0, :, cols], muk_ref[:, cols])
        v = _shift(v_ref[0, :, cols], muv_ref[:, cols])
        kk = k * kk_ref[:, cols]
        kk = kk * lax.rsqrt(_head_sum(kk * kk, ones) + L2_EPS)
        wup_h, wup_l = _split2(wup_ref[:, cols])
        aup = aup_ref[:, cols].astype(BF16)
        kdirs = []
        for e, (lw_s, kd_s, b_s) in enumerate(((lw0_s, kd0_s, b0_s), (lw1_s, kd1_s, b1_s))):
            wd_h, wd_l = wd_split[e]
            w_lin = w0_ref[e:e + 1, cols] + (mm(wd_h, wup_h) + mm(wd_l, wup_h) + mm(wd_h, wup_l))
            lw_s[j] = -DECAY_SCALE * _sigmoid(w_lin)
            iclr = _sigmoid(a0_ref[e:e + 1, cols] + mm(ad_bf[e], aup))
            kd = k * (1.0 + (iclr - 1.0) * ka_ref[:, cols])
            kd_s[j] = kd
            b_s[j] = kk * iclr
            kdirs.append(kd)
        gate_s[:, cols] = mm(sig_gd, gup_ref[:, cols].astype(BF16))
        bonus_s[:, cols] = _head_sum(r * (0.5 * (kdirs[0] + kdirs[1])) * rk_ref[:, cols], ones) * v
        r_s[j] = r
        v_s[j] = v
        kk_s[j] = kk

    n_chunks = t_len // CHUNK
    chunks_per = min(n_chunks, units // 2)
    pairs_per = min(pairs, units // (2 * chunks_per))
    groups = n_chunks // chunks_per
    rs = lax.broadcasted_iota(jnp.int32, (STACK, STACK), 0)
    cs = lax.broadcasted_iota(jnp.int32, (STACK, STACK), 1)
    as_f32 = lambda m: jnp.where(m, 1.0, 0.0)
    blk = (as_f32(rs // CHUNK == cs // CHUNK), as_f32(cs == (rs + CHUNK) % STACK))
    consts = ((tri_ref[0], tri_ref[1]), (mask_ref[0], mask_ref[1]), (mask_ref[2], mask_ref[3]),
              rs == cs, lane_lo, blk)
    dirs = ((lw0_s, kd0_s, b0_s), (lw1_s, kd1_s, b1_s))

    def intra_body(it, carry):
        pg = it // groups
        g = it % groups
        units_, ids = [], []
        for jj in range(pairs_per):
            j = pg * pairs_per + jj
            for cc in range(chunks_per):
                c = g * chunks_per + cc
                rows = pl.ds(pl.multiple_of(c * CHUNK, CHUNK), CHUNK)
                for e, (lw_s, kd_s, b_s) in enumerate(dirs):
                    units_.append((r_s[j, rows, :], lw_s[j, rows, :], kd_s[j, rows, :], v_s[j, rows, :],
                                   kk_s[j, rows, :], b_s[j, rows, :], e == 1))
                    ids.append((j * 2 + e) * n_chunks + c)
        for uid, (lhs, add) in zip(ids, _wkv_intra(units_, consts)):
            lhs_s[uid] = lhs
            add_s[uid] = add
        return carry

    lax.fori_loop(0, (pairs // pairs_per) * groups, intra_body, 0)

    for j in range(pairs):
        for e in range(2):
            if has_s0:
                st_s[2 * j + e] = s0_ref[0, e, j].T
            else:
                st_s[2 * j + e] = jnp.zeros((LANES, LANES), F32)

    def state_body(it, carry):
        chunk = (it, n_chunks - 1 - it)
        uids = [(j * 2 + e) * n_chunks + chunk[e] for j in range(pairs) for e in range(2)]
        sts = [st_s[ch].astype(BF16) for ch in range(2 * pairs)]
        res = [mm(lhs_s[uid], st) + add_s[uid] for uid, st in zip(uids, sts)]
        for ch, rr in enumerate(res):
            j, e = divmod(ch, 2)
            y_s = yb_s if e else yf_s
            y_s[j, pl.ds(pl.multiple_of(chunk[e] * CHUNK, CHUNK), CHUNK), :] = rr[:CHUNK] + rr[CHUNK:STACK]
            st_s[ch] = rr[STACK:]
        return carry

    lax.fori_loop(0, n_chunks, state_body, 0)
    for j in range(pairs):
        for e in range(2):
            st_t = st_s[2 * j + e].T
            sn_ref[0, e, 2 * j] = st_t[:HEAD_DIM, :HEAD_DIM]
            sn_ref[0, e, 2 * j + 1] = pltpu.roll(st_t, HEAD_DIM, 1)[HEAD_DIM:, :HEAD_DIM]

    inv_d = 1.0 / HEAD_DIM
    for j in range(pairs):
        cols = slice(j * LANES, (j + 1) * LANES)
        y = yf_s[j] + yb_s[j]
        mean = _head_sum(y, ones) * inv_d
        dlt = y - mean
        var = _head_sum(dlt * dlt, ones) * inv_d
        yn = dlt * lax.rsqrt(var + GN_EPS) * lng_ref[:, cols] + lnb_ref[:, cols]
        o_ref[0, :, cols] = ((yn + bonus_s[:, cols]) * gate_s[:, cols]).astype(o_ref.dtype)


def _rwkv_branch(u_rw, bsz, b_off, s0_big, p, pairs, units, cast=()):
    t_len = u_rw.shape[1]
    has_s0 = s0_big is not None
    width = pairs * LANES
    seg = RW_WIDTH // width
    tok = lambda off: pl.BlockSpec((1, t_len, width), lambda b, j: (b_off + b, 0, off + j))
    mu = lambda off: pl.BlockSpec((2, width), lambda b, j: (0, off + j))
    vec2 = pl.BlockSpec((2, width), lambda b, j: (0, j))
    vec1 = pl.BlockSpec((1, width), lambda b, j: (0, j))
    mat = pl.BlockSpec((LANES, width), lambda b, j: (0, j))
    lora_w = 3 * LANES
    lora_blk = 3 * RW_WIDTH // lora_w
    in_specs = [tok(0), tok(seg), tok(2 * seg),
                pl.BlockSpec((1, t_len, lora_w), lambda b, j: (b_off + b, 0, lora_blk)),
                mu(0), mu(seg), mu(2 * seg),
                pl.BlockSpec((2, lora_w), lambda b, j: (0, lora_blk)),
                vec2, vec2, mat, mat, mat, vec1, vec1, vec1, vec1, vec1,
                _const_spec((LANES, LANES)), _const_spec((2, CHUNK, CHUNK)), _const_spec((4, STACK, STACK))]
    args = [u_rw, u_rw, u_rw, u_rw, p["shift_mu"], p["shift_mu"], p["shift_mu"], p["shift_mu"],
            p["rw_w0"], p["rw_a0"],
            p["rw_w_up"].reshape(2 * LORA_DECAY, RW_WIDTH), p["rw_a_up"].reshape(2 * LORA_ICLR, RW_WIDTH),
            p["rw_g_up"], p["rw_k_k"].reshape(1, -1), p["rw_k_a"].reshape(1, -1),
            p["rw_r_k"].reshape(1, -1), p["rw_ln_g"].reshape(1, -1), p["rw_ln_b"].reshape(1, -1),
            *_wkv_constants()]
    st_spec = pl.BlockSpec((1, 2, pairs, LANES, LANES), lambda b, j: (b, 0, j, 0, 0))
    if has_s0:
        in_specs.append(st_spec)
        args.append(s0_big)
    out_specs = [pl.BlockSpec((1, t_len, width), lambda b, j: (b, 0, j)),
                 pl.BlockSpec((1, 2, 2 * pairs, HEAD_DIM, HEAD_DIM), lambda b, j: (b, 0, j, 0, 0))]
    out_shape = [jax.ShapeDtypeStruct((bsz, t_len, RW_WIDTH), BF16),
                 jax.ShapeDtypeStruct((bsz, 2, RW_HEADS, HEAD_DIM, HEAD_DIM), F32)]
    for w in cast:
        assert PAIRS == pairs and w.shape[0] % (16 * bsz) == 0
        blk = pl.BlockSpec((w.shape[0] // bsz, w.shape[1]), lambda b, j: (b, 0))
        in_specs.append(blk)
        args.append(w)
        out_specs.append(blk)
        out_shape.append(jax.ShapeDtypeStruct(w.shape, BF16))
    n_units = 2 * pairs * (t_len // CHUNK)
    per_pair = pltpu.VMEM((pairs, t_len, LANES), F32)
    full = pltpu.VMEM((t_len, width), F32)
    scratch = [per_pair] * 9 + [full, full, per_pair, per_pair,
                                pltpu.VMEM((n_units, 2 * STACK, LANES), BF16),
                                pltpu.VMEM((n_units, 2 * STACK, LANES), F32),
                                pltpu.VMEM((2 * pairs, LANES, LANES), F32)]
    o_rw, s_new, *casted = pl.pallas_call(
        functools.partial(_rwkv_kernel, t_len=t_len, has_s0=has_s0, pairs=pairs, units=units, n_cast=len(cast)),
        grid=(bsz, PAIRS // pairs),
        in_specs=in_specs,
        out_specs=out_specs,
        out_shape=out_shape,
        scratch_shapes=scratch,
        compiler_params=_params(2),
        name="rwkv_branch",
    )(*args)
    return o_rw, s_new, casted


def _state_to_big(s0):
    bsz = s0.shape[0]
    x = s0.reshape(bsz, 2, PAIRS, 2, HEAD_DIM, 1, HEAD_DIM)
    same_head = np.eye(2, dtype=bool).reshape(2, 1, 2, 1)
    return jnp.where(same_head, x, 0.0).reshape(bsz, 2, PAIRS, LANES, LANES)


def _qk_norm(t, g, ones):
    ms = _head_sum(t * t, ones) * (1.0 / HEAD_DIM)
    return t * lax.rsqrt(ms + RMS_EPS) * g


def _nt(x, y):
    return lax.dot_general(x, y, (((1,), (1,)), ((), ())), preferred_element_type=F32)


def _na_ctx_kernel(q_ref, k_ref, v_ref, qg_ref, kg_ref, o_ref, kn_ref, vc_ref):
    ones = _head_ones()
    lo_half = lax.broadcasted_iota(jnp.int32, (1, LANES), 1) < HEAD_DIM
    lo = jnp.where(lo_half, 1.0, 0.0)
    t_len = q_ref.shape[1]
    qs, ks, vs = [], [], []
    for j in range(NA_WIDTH // LANES):
        cols = slice(j * LANES, (j + 1) * LANES)
        qn = _qk_norm(q_ref[0, :, cols], qg_ref[...], ones)
        kn = _qk_norm(k_ref[0, :, cols], kg_ref[...], ones)
        v = v_ref[0, :, cols]
        kn_ref[0, :, cols] = kn
        vc_ref[0, :, cols] = v
        qn = qn * QK_SCALE
        qs.append(jnp.concatenate([qn * lo, qn * (1.0 - lo)], axis=0).astype(BF16))
        ks.append(kn.astype(BF16))
        vs.append(v.astype(BF16))
    logits = [_nt(q, k) for q, k in zip(qs, ks)]
    ms = [jnp.max(s, axis=-1, keepdims=True) for s in logits]
    ps = [jnp.exp(s - m) for s, m in zip(logits, ms)]
    ls = [jnp.sum(p, axis=-1, keepdims=True) for p in ps]
    outs = [jnp.dot(p.astype(BF16), v, preferred_element_type=F32) / l for p, v, l in zip(ps, vs, ls)]
    for j, o in enumerate(outs):
        o_ref[0, :, j * LANES:(j + 1) * LANES] = jnp.where(lo_half, o[:t_len], o[t_len:]).astype(o_ref.dtype)


def _na_context(u_na, bsz, q_g, k_g):
    t_len = u_na.shape[1]
    tok = lambda seg: pl.BlockSpec((1, t_len, NA_WIDTH), lambda b: (b, 0, seg))
    out_blk = pl.BlockSpec((1, t_len, NA_WIDTH), lambda b: (b, 0, 0))
    g2 = lambda g: jnp.tile(g.reshape(1, HEAD_DIM), (1, 2))
    shp = jax.ShapeDtypeStruct((bsz, t_len, NA_WIDTH), F32)
    return pl.pallas_call(
        _na_ctx_kernel,
        grid=(bsz,),
        in_specs=[tok(0), tok(1), tok(2), _const_spec((1, LANES)), _const_spec((1, LANES))],
        out_specs=[out_blk, out_blk, out_blk],
        out_shape=[jax.ShapeDtypeStruct(shp.shape, BF16), shp, shp],
        compiler_params=_params(1),
        name="na_context",
    )(u_na, u_na, u_na, g2(q_g), g2(k_g))


NA_ROW_ILP = 8


def _na_lat_kernel(q_ref, k_ref, v_ref, kc_ref, vc_ref, qg_ref, kg_ref, tab_ref, o_ref,
                   q0_s, q1_s, kn_s, v_s, kc_s, vc_s, *, rows, kr):
    ones = _head_ones()
    lo_half = lax.broadcasted_iota(jnp.int32, (1, LANES), 1) < HEAD_DIM
    lo = jnp.where(lo_half, 1.0, 0.0)
    qn = _qk_norm(q_ref[0], qg_ref[...], ones) * QK_SCALE
    q0_s[...] = (qn * lo).astype(BF16)
    q1_s[...] = (qn * (1.0 - lo)).astype(BF16)
    kn_s[...] = _qk_norm(k_ref[0], kg_ref[...], ones).astype(BF16)
    v_s[...] = v_ref[0].astype(BF16)
    kc_s[...] = kc_ref[0].astype(BF16)
    vc_s[...] = vc_ref[0].astype(BF16)
    win = kr * GRID_W

    def body(it, carry):
        qs, k_rows, q_rows, biases = [], [], [], []
        for s in range(NA_ROW_ILP):
            i = it * NA_ROW_ILP + s
            r0 = jnp.clip(i - kr // 2, 0, rows - kr)
            d0 = r0 - i + (NA_ROWS - 1)
            qr = pl.ds(pl.multiple_of(i * GRID_W, GRID_W), GRID_W)
            q_rows.append(qr)
            k_rows.append(pl.ds(pl.multiple_of(r0 * GRID_W, GRID_W), win))
            qs.append(jnp.concatenate([q0_s[qr, :], q1_s[qr, :]], axis=0))
            biases.append(jnp.concatenate(
                [jnp.concatenate([tab_ref[h, d0 + 2 * m] for m in range(kr // 2)], axis=1) for h in range(2)],
                axis=0))
        lw = [_nt(q, kn_s[kr_, :]) + b for q, kr_, b in zip(qs, k_rows, biases)]
        lc = [_nt(q, kc_s[...]) for q in qs]
        ms = [jnp.maximum(jnp.max(a, axis=-1, keepdims=True), jnp.max(c, axis=-1, keepdims=True))
              for a, c in zip(lw, lc)]
        pw = [jnp.exp(a - m) for a, m in zip(lw, ms)]
        pc = [jnp.exp(c - m) for c, m in zip(lc, ms)]
        ls = [jnp.sum(a, axis=-1, keepdims=True) + jnp.sum(c, axis=-1, keepdims=True) for a, c in zip(pw, pc)]
        outs = [(jnp.dot(a.astype(BF16), v_s[kr_, :], preferred_element_type=F32)
                 + jnp.dot(c.astype(BF16), vc_s[...], preferred_element_type=F32)) / l
                for a, c, kr_, l in zip(pw, pc, k_rows, ls)]
        for qr, o in zip(q_rows, outs):
            o_ref[0, qr, :] = jnp.where(lo_half, o[:GRID_W], o[GRID_W:]).astype(o_ref.dtype)
        return carry

    lax.fori_loop(0, rows // NA_ROW_ILP, body, 0)


def _latent_bias_table(rpb):
    qc = np.arange(GRID_W)[:, None]
    kc = np.arange(GRID_W)[None, :]
    ws = np.clip(qc - NA_COLS // 2, 0, GRID_W - NA_COLS)
    valid = (kc >= ws) & (kc < ws + NA_COLS)
    dc = np.clip(kc - qc, -(NA_COLS - 1), NA_COLS - 1) + NA_COLS - 1
    onehot = (dc[None] == np.arange(2 * NA_COLS - 1)[:, None, None]).astype(np.float32)
    cb = jnp.einsum("hdc,cqk->hdqk", rpb, jnp.asarray(onehot), precision=lax.Precision.HIGHEST)
    cb = jnp.where(valid[None, None], cb, NEG_INF)
    return jnp.concatenate([cb[:, :-1], cb[:, 1:]], axis=-1)


def _na_latent(u_na, bsz, b_off, k_ctx, v_ctx, q_g, k_g, rpb):
    t_len = u_na.shape[1]
    rows = t_len // GRID_W
    kr = min(NA_ROWS, rows)
    assert kr % 2 == 0 and rows % NA_ROW_ILP == 0
    ctx_len = k_ctx.shape[1]
    seg = NA_WIDTH // LANES
    tok = lambda off: pl.BlockSpec((1, t_len, LANES), lambda b, j: (b_off + b, 0, off + j))
    ctx = pl.BlockSpec((1, ctx_len, LANES), lambda b, j: (b, 0, j))
    g2 = lambda g: jnp.tile(g.reshape(1, HEAD_DIM), (1, 2))
    tab = _latent_bias_table(rpb)
    tok_s = pltpu.VMEM((t_len, LANES), BF16)
    ctx_s = pltpu.VMEM((ctx_len, LANES), BF16)
    return pl.pallas_call(
        functools.partial(_na_lat_kernel, rows=rows, kr=kr),
        grid=(bsz, seg),
        in_specs=[tok(0), tok(seg), tok(2 * seg), ctx, ctx,
                  _const_spec((1, LANES)), _const_spec((1, LANES)),
                  pl.BlockSpec((2, 2 * NA_ROWS - 2, GRID_W, 2 * GRID_W), lambda b, j: (j, 0, 0, 0))],
        out_specs=pl.BlockSpec((1, t_len, LANES), lambda b, j: (b, 0, j)),
        out_shape=jax.ShapeDtypeStruct((bsz, t_len, NA_WIDTH), BF16),
        scratch_shapes=[tok_s, tok_s, tok_s, tok_s, ctx_s, ctx_s],
        compiler_params=_params(2),
        name="na_latent",
    )(u_na, u_na, u_na, k_ctx, v_ctx, g2(q_g), g2(k_g), tab)


LATE_WEIGHTS = ("w_o_rwkv", "w_o_na", "w_out", "ffn_w1", "ffn_w3", "ffn_w2")


def _out_ffn_kernel(xa_ref, xb_ref, orwa_ref, orwb_ref, onaa_ref, onab_ref, gt_ref, mod_ref, g_ref,
                    wor_ref, won_ref, wout_ref, w1_ref, w3_ref, w2_ref, ya_ref, yb_ref, *, tiles_a):
    i = pl.program_id(0)
    x = _pick_group(i, tiles_a, xa_ref, xb_ref)
    o_rw = _pick_group(i, tiles_a, orwa_ref, orwb_ref)
    o_na = _pick_group(i, tiles_a, onaa_ref, onab_ref)
    g_rw = _sigmoid(gt_ref[:, :D_MODEL].astype(F32))
    g_na = _sigmoid(gt_ref[:, D_MODEL:].astype(F32))
    merged = g_rw * _dot(o_rw, wor_ref[...]) + g_na * _dot(o_na, won_ref[...])
    x1 = x + mod_ref[0, 2:3, :] * _dot(merged, wout_ref[...])
    h2 = _rms_rows(x1) * g_ref[...]
    h2 = (h2 * (1.0 + mod_ref[0, 4:5, :]) + mod_ref[0, 3:4, :]).astype(BF16)
    acc = jnp.zeros(x1.shape, F32)
    for c in range(FF_HIDDEN // FF_CHUNK):
        cols = slice(c * FF_CHUNK, (c + 1) * FF_CHUNK)
        a = jnp.dot(h2, w1_ref[:, cols], preferred_element_type=F32)
        b = jnp.dot(h2, w3_ref[:, cols], preferred_element_type=F32)
        hh = (a * _sigmoid(a) * b).astype(BF16)
        acc = acc + jnp.dot(hh, w2_ref[cols, :], preferred_element_type=F32)
    y = x1 + mod_ref[0, 5:6, :] * acc

    @pl.when(i < tiles_a)
    def _():
        ya_ref[0] = y

    @pl.when(i >= tiles_a)
    def _():
        yb_ref[0] = y


def _out_ffn(xa, xb, orw_a, orw_b, ona_a, ona_b, gates, tiles_per_b, mod_all, norm_g, wb):
    tm = TOKEN_TILE
    tiles_a, tiles = xa.shape[0], xa.shape[0] + xb.shape[0]
    weights = [wb[n] for n in LATE_WEIGHTS]
    groups = lambda width: _group_specs(tiles_a, tm, width)
    return pl.pallas_call(
        functools.partial(_out_ffn_kernel, tiles_a=tiles_a),
        grid=(tiles,),
        in_specs=groups(D_MODEL) + groups(RW_WIDTH) + groups(NA_WIDTH)
        + [pl.BlockSpec((tm, GATE_COLS), lambda i: (i, 0)), _mod_spec(tiles_a, tiles_per_b),
           _const_spec((1, D_MODEL))] + [_const_spec(w.shape) for w in weights],
        out_specs=groups(D_MODEL),
        out_shape=[jax.ShapeDtypeStruct(xa.shape, F32), jax.ShapeDtypeStruct(xb.shape, F32)],
        compiler_params=_params(1),
        name="out_ffn",
    )(xa, xb, orw_a, orw_b, ona_a, ona_b, gates, mod_all, norm_g.reshape(1, -1), *weights)


def kernel(x_prompt, x_sample, state_rwkv, cache_na_k, cache_na_v, c, c_ctx, norm1_g, norm2_g, w_ada, b_ada,
           w_in, shift_mu, rw_w0, rw_w_up, rw_a0, rw_a_up, rw_g_up, rw_k_k, rw_k_a, rw_r_k, rw_ln_g, rw_ln_b,
           na_q_g, na_k_g, na_rpb, w_o_rwkv, w_o_na, w_out, ffn_w1, ffn_w3, ffn_w2):
    depth = w_in.shape[0]
    bsz, seq = x_prompt.shape[:2]
    dec, dec_seq = x_sample.shape[:2]
    tm = TOKEN_TILE
    n_ctx = bsz * seq
    assert n_ctx % tm == 0 and dec_seq % tm == 0 and n_ctx % dec_seq == 0
    tiles_per_b = dec_seq // tm
    tiled = lambda t: t.reshape(-1, tm, t.shape[-1])
    cvecs = jnp.concatenate([c_ctx[None, :], c], axis=0).T
    y_p, y_s = x_prompt, x_sample
    new_s, new_k, new_v = [], [], []
    for l in range(depth):
        p = dict(norm1_g=norm1_g[l], norm2_g=norm2_g[l], shift_mu=shift_mu[l], rw_w0=rw_w0[l],
                 rw_w_up=rw_w_up[l], rw_a0=rw_a0[l], rw_a_up=rw_a_up[l], rw_g_up=rw_g_up[l],
                 rw_k_k=rw_k_k[l], rw_k_a=rw_k_a[l], rw_r_k=rw_r_k[l], rw_ln_g=rw_ln_g[l],
                 rw_ln_b=rw_ln_b[l], na_q_g=na_q_g[l], na_k_g=na_k_g[l], na_rpb=na_rpb[l])
        late_f32 = [w[l] for w in (w_o_rwkv, w_o_na, w_out, ffn_w1, ffn_w3, ffn_w2)]
        mod_all, w_in_bf = _modulation(cvecs, w_ada[l], b_ada[l], w_in[l])
        mod_all = mod_all.reshape(1 + dec, 6, D_MODEL)
        u_rw, u_na, gates = _in_proj(tiled(y_p), tiled(y_s), tiles_per_b, mod_all, p["norm1_g"], w_in_bf)
        ctx_view = lambda t: t.reshape(-1, seq, t.shape[-1])
        lat_view = lambda t: t.reshape(-1, dec_seq, t.shape[-1])
        lat_off = n_ctx // dec_seq
        o_rw_p, s_l, casted = _rwkv_branch(ctx_view(u_rw), bsz, 0, None, p, RWKV_PAIRS_CTX, RWKV_UNITS_CTX, late_f32)
        late = dict(zip(LATE_WEIGHTS, casted))
        o_na_p, k_l, v_l = _na_context(ctx_view(u_na), bsz, p["na_q_g"], p["na_k_g"])
        new_s.append(s_l)
        new_k.append(k_l.reshape(bsz, seq, NA_HEADS, HEAD_DIM))
        new_v.append(v_l.reshape(bsz, seq, NA_HEADS, HEAD_DIM))
        ctx_k = cache_na_k[:, l].reshape(dec, -1, NA_WIDTH)
        ctx_v = cache_na_v[:, l].reshape(dec, -1, NA_WIDTH)
        o_rw_s, _, _ = _rwkv_branch(lat_view(u_rw), dec, lat_off, _state_to_big(state_rwkv[:, l]), p, RWKV_PAIRS_LAT,
                                   RWKV_UNITS_LAT)
        o_na_s = _na_latent(lat_view(u_na), dec, lat_off, ctx_k, ctx_v, p["na_q_g"], p["na_k_g"], p["na_rpb"])
        y_p_t, y_s_t = _out_ffn(tiled(y_p), tiled(y_s), tiled(o_rw_p), tiled(o_rw_s), tiled(o_na_p), tiled(o_na_s),
                                gates, tiles_per_b, mod_all, p["norm2_g"], late)
        y_p, y_s = y_p_t.reshape(x_prompt.shape), y_s_t.reshape(x_sample.shape)
    return (y_p, y_s, jnp.stack(new_s, axis=1), jnp.stack(new_k, axis=1), jnp.stack(new_v, axis=1))
```

```python
import functools

import numpy as np
import jax
import jax.numpy as jnp
from jax import lax
from jax.experimental import pallas as pl
from jax.experimental.pallas import tpu as pltpu

D_MODEL = 1024
GRID_W = 64
HEAD_DIM = 64
RW_HEADS = 8
RW_WIDTH = RW_HEADS * HEAD_DIM
NA_HEADS = 8
NA_WIDTH = NA_HEADS * HEAD_DIM
LORA_DECAY = 64
LORA_ICLR = 64
LORA_GATE = 128
NA_ROWS = 8
NA_COLS = 16
FF_HIDDEN = 2816
RW_COLS = 3 * RW_WIDTH + 2 * LORA_DECAY + 2 * LORA_ICLR + LORA_GATE
NA_IN_COLS = 3 * NA_WIDTH
GATE_COLS = 2 * D_MODEL
RMS_EPS = 1e-6
GN_EPS = 64e-5
L2_EPS = 1e-12
NEG_INF = -1e30
DECAY_SCALE = float(np.exp(-0.5))
QK_SCALE = HEAD_DIM ** -0.5
assert QK_SCALE == 0.125

LANES = 128
PAIRS = RW_HEADS // 2
CHUNK = 64
STACK = 2 * CHUNK
RWKV_PAIRS_CTX = 4
RWKV_PAIRS_LAT = 2
RWKV_UNITS = 16
TOKEN_TILE = 512
FF_CHUNK = 256
VMEM_LIMIT = 56 * 1024 * 1024

F32 = jnp.float32
BF16 = jnp.bfloat16


def _dot(a, b):
    return jnp.dot(a.astype(BF16), b.astype(BF16), preferred_element_type=F32)


def _split2(x):
    hi = x.astype(BF16)
    lo = (x - hi.astype(F32)).astype(BF16)
    return hi, lo


def _dot_exact_lhs(a_exact, b):
    h, l = _split2(b)
    d = lambda x: jnp.dot(a_exact, x, preferred_element_type=F32)
    return d(h) + d(l)


def _head_ones():
    r = lax.broadcasted_iota(jnp.int32, (LANES, LANES), 0) // HEAD_DIM
    c = lax.broadcasted_iota(jnp.int32, (LANES, LANES), 1) // HEAD_DIM
    return jnp.where(r == c, 1.0, 0.0).astype(BF16)


def _head_sum(x, ones):
    return jnp.dot(x.astype(BF16), ones, preferred_element_type=F32)


def _sigmoid(x):
    return 0.5 * jnp.tanh(0.5 * x) + 0.5


def _rms_rows(x):
    return x * lax.rsqrt(jnp.mean(x * x, axis=-1, keepdims=True) + RMS_EPS)


def _const_spec(shape):
    nd = len(shape)
    return pl.BlockSpec(shape, lambda *_: (0,) * nd, pipeline_mode=pl.Buffered(1))


def _params(n_axes):
    return pltpu.CompilerParams(dimension_semantics=("arbitrary",) * n_axes,
                                vmem_limit_bytes=VMEM_LIMIT)


def _mod_kernel(c_ref, w_ref, b_ref, win_ref, o_ref, winb_ref):
    s = c_ref[...]
    s = s * _sigmoid(s)
    w = w_ref[...]
    for r in range(s.shape[1]):
        o_ref[r] = jnp.sum(w * s[:, r:r + 1], axis=0, keepdims=True) + b_ref[...]
    winb_ref[...] = win_ref[...].astype(BF16)


def _modulation(cvecs, w_ada, b_ada, w_in):
    n = cvecs.shape[1]
    tn = 768
    steps = 6 * D_MODEL // tn
    rows = w_in.shape[0] // steps
    assert rows % 16 == 0
    return pl.pallas_call(
        _mod_kernel,
        grid=(steps,),
        in_specs=[pl.BlockSpec((D_MODEL, n), lambda j: (0, 0)),
                  pl.BlockSpec((D_MODEL, tn), lambda j: (0, j)),
                  pl.BlockSpec((1, tn), lambda j: (0, j)),
                  pl.BlockSpec((rows, w_in.shape[1]), lambda j: (j, 0))],
        out_specs=[pl.BlockSpec((n, 1, tn), lambda j: (0, 0, j)),
                   pl.BlockSpec((rows, w_in.shape[1]), lambda j: (j, 0))],
        out_shape=[jax.ShapeDtypeStruct((n, 1, 6 * D_MODEL), F32),
                   jax.ShapeDtypeStruct(w_in.shape, BF16)],
        compiler_params=_params(1),
        name="modulation",
    )(cvecs, w_ada, b_ada.reshape(1, -1), w_in)


def _pick_group(i, tiles_a, a_ref, b_ref):
    return jnp.where(i < tiles_a, a_ref[0], b_ref[0])


def _group_specs(tiles_a, tm, width):
    return [pl.BlockSpec((1, tm, width), lambda i: (jnp.minimum(i, tiles_a - 1), 0, 0)),
            pl.BlockSpec((1, tm, width), lambda i: (jnp.maximum(i - tiles_a, 0), 0, 0))]


def _mod_spec(tiles_a, tiles_per_b):
    return pl.BlockSpec((1, 1, 6 * D_MODEL),
                        lambda i: (jnp.where(i < tiles_a, 0, 1 + (i - tiles_a) // tiles_per_b), 0, 0))


def _mod_part(mod_ref, k):
    return mod_ref[0, :, k * D_MODEL:(k + 1) * D_MODEL]


def _inproj_kernel(xa_ref, xb_ref, mod_ref, g_ref, w_ref, urw_ref, una_ref, gt_ref, *, tiles_a):
    x = _pick_group(pl.program_id(0), tiles_a, xa_ref, xb_ref)
    h = _rms_rows(x) * g_ref[...]
    h = (h * (1.0 + _mod_part(mod_ref, 1)) + _mod_part(mod_ref, 0)).astype(BF16)
    d = lambda lo, hi: jnp.dot(h, w_ref[:, lo:hi], preferred_element_type=F32)
    urw_ref[...] = d(0, RW_COLS)
    una_ref[...] = d(RW_COLS, RW_COLS + NA_IN_COLS)
    gt_ref[...] = d(RW_COLS + NA_IN_COLS, RW_COLS + NA_IN_COLS + GATE_COLS).astype(BF16)


def _in_proj(xa, xb, tiles_per_b, mod_all, norm_g, w_in_bf):
    tm = TOKEN_TILE
    tiles_a, tiles = xa.shape[0], xa.shape[0] + xb.shape[0]
    row = lambda i: (i, 0)
    return pl.pallas_call(
        functools.partial(_inproj_kernel, tiles_a=tiles_a),
        grid=(tiles,),
        in_specs=_group_specs(tiles_a, tm, D_MODEL) + [_mod_spec(tiles_a, tiles_per_b),
                                                       _const_spec((1, D_MODEL)), _const_spec(w_in_bf.shape)],
        out_specs=[pl.BlockSpec((tm, RW_COLS), row),
                   pl.BlockSpec((tm, NA_IN_COLS), row),
                   pl.BlockSpec((tm, GATE_COLS), row)],
        out_shape=[jax.ShapeDtypeStruct((tiles * tm, RW_COLS), F32),
                   jax.ShapeDtypeStruct((tiles * tm, NA_IN_COLS), F32),
                   jax.ShapeDtypeStruct((tiles * tm, GATE_COLS), BF16)],
        compiler_params=_params(1),
        name="in_proj",
    )(xa, xb, mod_all, norm_g.reshape(1, -1), w_in_bf)


def _shift(x, mu):
    t_len = x.shape[0]
    row = lax.broadcasted_iota(jnp.int32, x.shape, 0)
    prev = jnp.where(row == 0, 0.0, pltpu.roll(x, 1, 0))
    nxt = jnp.where(row == t_len - 1, 0.0, pltpu.roll(x, t_len - 1, 0))
    return x + mu[0:1, :] * (prev - x) + mu[1:2, :] * (nxt - x)


def _stack_heads(x, lane_lo):
    return jnp.concatenate([x * lane_lo, x * (1.0 - lane_lo)], axis=0)


def _wkv_intra(units, consts):
    tri, mask_s, mask_i, eye, lane_lo, blk = consts
    stack = lambda z: _stack_heads(z, lane_lo)

    cums = [_dot_exact_lhs(tri[int(u[6])], u[1]) for u in units]

    prep = []
    for (r, lw, kd, v, kk, b, reverse), cum in zip(units, cums):
        mid_row = CHUNK // 2 if reverse else CHUNK // 2 - 1
        tot_row = 0 if reverse else CHUNK - 1
        a = -kk
        ex = cum - lw
        mid = cum[mid_row:mid_row + 1, :]
        tot = cum[tot_row:tot_row + 1, :]
        up = jnp.exp(cum - mid)
        dn = jnp.exp(mid - cum)
        tail = jnp.exp(tot - cum)
        prep.append(dict(
            at_m=stack(a * jnp.exp(ex - mid)).astype(BF16),
            rt_m=stack(r * up).astype(BF16),
            bk_m=jnp.concatenate([b * dn, kd * dn], axis=0).astype(BF16),
            a_e=stack(a * jnp.exp(ex)),
            r_e=stack(r * jnp.exp(cum)),
            bk_t=jnp.concatenate([stack(b * tail), stack(kd * tail)], axis=0).T.astype(BF16),
            vv=stack(v).astype(BF16),
            diag=jnp.where(eye, jnp.exp(tot), 0.0),
            rev=int(reverse)))

    ntd = lambda x, y: lax.dot_general(x, y, (((1,), (1,)), ((), ())), preferred_element_type=F32)
    mm = lambda x, y: jnp.dot(x, y, preferred_element_type=F32)
    diag_blk, swap_eye = blk
    roll_head = lambda z: pltpu.roll(z, HEAD_DIM, 1)
    by_block = lambda z: jnp.concatenate([z[:CHUNK], roll_head(z[CHUNK:])], axis=0)
    top = [by_block(ntd(p["at_m"], p["bk_m"])) for p in prep]
    low = [by_block(ntd(p["rt_m"], p["bk_m"])) for p in prep]
    a_ak = [(roll_head(t) * mask_s[p["rev"]]).astype(BF16) for t, p in zip(top, prep)]
    bot = [jnp.concatenate([t * mask_i[p["rev"]], roll_head(t) * mask_i[p["rev"]]], axis=1).astype(BF16)
           for t, p in zip(low, prep)]

    off_blk = 1.0 - diag_blk
    both = [t * mask_s[p["rev"]] + swap_eye for t, p in zip(top, prep)]
    steps = CHUNK.bit_length() - 1
    diag_bf = diag_blk.astype(BF16)
    for j in range(steps):
        packed = [q.astype(BF16) for q in both]
        res = [mm(qb * diag_bf, qb) for qb in packed]
        both = [r + off_blk * q for r, q in zip(res, both)]
    ts = [pltpu.roll(q, HEAD_DIM, 1).astype(BF16) for q in both]
    x0 = [jnp.concatenate([p["a_e"], mm(ak, p["vv"])], axis=1) for p, ak in zip(prep, a_ak)]
    xs = [mm(t, x.astype(BF16)) for x, t in zip(x0, ts)]

    out = []
    zeros = jnp.zeros((STACK, LANES), BF16)
    for p, x, bt in zip(prep, xs, bot):
        rhs = jnp.concatenate([x.astype(BF16), jnp.concatenate([zeros, p["vv"]], axis=1)], axis=0)
        lhs = jnp.concatenate([bt, p["bk_t"]], axis=0)
        res = mm(lhs, rhs)
        lhs2 = res[:, :LANES] + jnp.concatenate([p["r_e"], p["diag"]], axis=0)
        out.append((lhs2.astype(BF16), res[:, LANES:]))
    return out


def _wkv_constants():
    lane_head = np.arange(LANES) // HEAD_DIM
    ones = (lane_head[:, None] == lane_head[None, :]).astype(np.float32)
    t = np.arange(CHUNK)
    tri = np.stack([t[None, :] <= t[:, None], t[None, :] >= t[:, None]]).astype(np.float32)
    rs, cs = t[:, None], t[None, :]
    tri_masks = np.stack([cs < rs, cs > rs, cs <= rs, cs >= rs]).astype(np.float32)
    masks = np.kron(np.eye(2, dtype=np.float32), tri_masks)
    return jnp.asarray(ones, BF16), jnp.asarray(tri, BF16), jnp.asarray(masks, F32)


def _rwkv_kernel(*refs, t_len, has_s0, pairs, units, n_cast):
    (r_ref, k_ref, v_ref, lo_ref, mur_ref, muk_ref, muv_ref, mul_ref, w0_ref, a0_ref, wup_ref, aup_ref,
     gup_ref, kk_ref, ka_ref, rk_ref, lng_ref, lnb_ref, ones_ref, tri_ref, mask_ref) = refs[:21]
    pos = 21
    s0_ref = None
    if has_s0:
        s0_ref = refs[pos]
        pos += 1
    cast_in = refs[pos:pos + n_cast]
    pos += n_cast
    o_ref, sn_ref = refs[pos], refs[pos + 1]
    cast_out = refs[pos + 2:pos + 2 + n_cast]
    (r_s, v_s, kk_s, b0_s, b1_s, lw0_s, lw1_s, kd0_s, kd1_s, gate_s, bonus_s, yf_s, yb_s,
     lhs_s, add_s, st_s) = refs[pos + 2 + n_cast:]
    for w_in_ref, w_out_ref in zip(cast_in, cast_out):
        w_out_ref[...] = w_in_ref[...].astype(BF16)

    ones = ones_ref[...]
    lane = lax.broadcasted_iota(jnp.int32, (1, LANES), 1)
    lo_half = lane < HEAD_DIM
    lane_lo = jnp.where(lo_half, 1.0, 0.0)
    mm = lambda x, y: jnp.dot(x, y, preferred_element_type=F32)

    lo = _shift(lo_ref[0], mul_ref[...])
    wd = jnp.tanh(lo[:, 0:LANES])
    ad = lo[:, LANES:2 * LANES]
    sig_gd = _sigmoid(lo[:, 2 * LANES:3 * LANES]).astype(BF16)
    wd_split = [_split2(wd * m) for m in (lane_lo, 1.0 - lane_lo)]
    ad_bf = [(ad * m).astype(BF16) for m in (lane_lo, 1.0 - lane_lo)]
    for j in range(pairs):
        cols = slice(j * LANES, (j + 1) * LANES)
        r = _shift(r_ref[0, :, cols], mur_ref[:, cols])
        k = _shift(k_ref[0, :, cols], muk_ref[:, cols])
        v = _shift(v_ref[0, :, cols], muv_ref[:, cols])
        kk = k * kk_ref[:, cols]
        kk = kk * lax.rsqrt(_head_sum(kk * kk, ones) + L2_EPS)
        wup_h, wup_l = _split2(wup_ref[:, cols])
        aup = aup_ref[:, cols].astype(BF16)
        kdirs = []
        for e, (lw_s, kd_s, b_s) in enumerate(((lw0_s, kd0_s, b0_s), (lw1_s, kd1_s, b1_s))):
            wd_h, wd_l = wd_split[e]
            w_lin = w0_ref[e:e + 1, cols] + (mm(wd_h, wup_h) + mm(wd_l, wup_h) + mm(wd_h, wup_l))
            lw_s[j] = -DECAY_SCALE * _sigmoid(w_lin)
            iclr = _sigmoid(a0_ref[e:e + 1, cols] + mm(ad_bf[e], aup))
            kd = k * (1.0 + (iclr - 1.0) * ka_ref[:, cols])
            kd_s[j] = kd
            b_s[j] = kk * iclr
            kdirs.append(kd)
        gate_s[:, cols] = mm(sig_gd, gup_ref[:, cols].astype(BF16))
        bonus_s[:, cols] = _head_sum(r * (0.5 * (kdirs[0] + kdirs[1])) * rk_ref[:, cols], ones) * v
        r_s[j] = r
        v_s[j] = v
        kk_s[j] = kk

    n_chunks = t_len // CHUNK
    chunks_per = min(n_chunks, units // 2)
    pairs_per = min(pairs, units // (2 * chunks_per))
    groups = n_chunks // chunks_per
    rs = lax.broadcasted_iota(jnp.int32, (STACK, STACK), 0)
    cs = lax.broadcasted_iota(jnp.int32, (STACK, STACK), 1)
    as_f32 = lambda m: jnp.where(m, 1.0, 0.0)
    blk = (as_f32(rs // CHUNK == cs // CHUNK), as_f32(cs == (rs + CHUNK) % STACK))
    consts = ((tri_ref[0], tri_ref[1]), (mask_ref[0], mask_ref[1]), (mask_ref[2], mask_ref[3]),
              rs == cs, lane_lo, blk)
    dirs = ((lw0_s, kd0_s, b0_s), (lw1_s, kd1_s, b1_s))

    def intra_body(it, carry):
        pg = it // groups
        g = it % groups
        units_, ids = [], []
        for jj in range(pairs_per):
            j = pg * pairs_per + jj
            for cc in range(chunks_per):
                c = g * chunks_per + cc
                rows = pl.ds(pl.multiple_of(c * CHUNK, CHUNK), CHUNK)
                for e, (lw_s, kd_s, b_s) in enumerate(dirs):
                    units_.append((r_s[j, rows, :], lw_s[j, rows, :], kd_s[j, rows, :], v_s[j, rows, :],
                                   kk_s[j, rows, :], b_s[j, rows, :], e == 1))
                    ids.append((j * 2 + e) * n_chunks + c)
        for uid, (lhs, add) in zip(ids, _wkv_intra(units_, consts)):
            lhs_s[uid] = lhs
            add_s[uid] = add
        return carry

    lax.fori_loop(0, (pairs // pairs_per) * groups, intra_body, 0)

    for j in range(pairs):
        for e in range(2):
            if has_s0:
                st_s[2 * j + e] = s0_ref[0, e, j].T
            else:
                st_s[2 * j + e] = jnp.zeros((LANES, LANES), F32)

    def state_body(it, carry):
        chunk = (it, n_chunks - 1 - it)
        uids = [(j * 2 + e) * n_chunks + chunk[e] for j in range(pairs) for e in range(2)]
        sts = [st_s[ch].astype(BF16) for ch in range(2 * pairs)]
        res = [mm(lhs_s[uid], st) + add_s[uid] for uid, st in zip(uids, sts)]
        for ch, rr in enumerate(res):
            j, e = divmod(ch, 2)
            y_s = yb_s if e else yf_s
            y_s[j, pl.ds(pl.multiple_of(chunk[e] * CHUNK, CHUNK), CHUNK), :] = rr[:CHUNK] + rr[CHUNK:STACK]
            st_s[ch] = rr[STACK:]
        return carry

    lax.fori_loop(0, n_chunks, state_body, 0)
    for j in range(pairs):
        for e in range(2):
            st_t = st_s[2 * j + e].T
            sn_ref[0, e, 2 * j] = st_t[:HEAD_DIM, :HEAD_DIM]
            sn_ref[0, e, 2 * j + 1] = pltpu.roll(st_t, HEAD_DIM, 1)[HEAD_DIM:, :HEAD_DIM]

    inv_d = 1.0 / HEAD_DIM
    for j in range(pairs):
        cols = slice(j * LANES, (j + 1) * LANES)
        y = yf_s[j] + yb_s[j]
        mean = _head_sum(y, ones) * inv_d
        dlt = y - mean
        var = _head_sum(dlt * dlt, ones) * inv_d
        yn = dlt * lax.rsqrt(var + GN_EPS) * lng_ref[:, cols] + lnb_ref[:, cols]
        o_ref[0, :, cols] = ((yn + bonus_s[:, cols]) * gate_s[:, cols]).astype(o_ref.dtype)


def _rwkv_branch(u_rw, bsz, b_off, s0_big, p, pairs, units, cast=()):
    t_len = u_rw.shape[1]
    has_s0 = s0_big is not None
    width = pairs * LANES
    seg = RW_WIDTH // width
    tok = lambda off: pl.BlockSpec((1, t_len, width), lambda b, j: (b_off + b, 0, off + j))
    mu = lambda off: pl.BlockSpec((2, width), lambda b, j: (0, off + j))
    vec2 = pl.BlockSpec((2, width), lambda b, j: (0, j))
    vec1 = pl.BlockSpec((1, width), lambda b, j: (0, j))
    mat = pl.BlockSpec((LANES, width), lambda b, j: (0, j))
    lora_w = 3 * LANES
    lora_blk = 3 * RW_WIDTH // lora_w
    in_specs = [tok(0), tok(seg), tok(2 * seg),
                pl.BlockSpec((1, t_len, lora_w), lambda b, j: (b_off + b, 0, lora_blk)),
                mu(0), mu(seg), mu(2 * seg),
                pl.BlockSpec((2, lora_w), lambda b, j: (0, lora_blk)),
                vec2, vec2, mat, mat, mat, vec1, vec1, vec1, vec1, vec1,
                _const_spec((LANES, LANES)), _const_spec((2, CHUNK, CHUNK)), _const_spec((4, STACK, STACK))]
    args = [u_rw, u_rw, u_rw, u_rw, p["shift_mu"], p["shift_mu"], p["shift_mu"], p["shift_mu"],
            p["rw_w0"], p["rw_a0"],
            p["rw_w_up"].reshape(2 * LORA_DECAY, RW_WIDTH), p["rw_a_up"].reshape(2 * LORA_ICLR, RW_WIDTH),
            p["rw_g_up"], p["rw_k_k"].reshape(1, -1), p["rw_k_a"].reshape(1, -1),
            p["rw_r_k"].reshape(1, -1), p["rw_ln_g"].reshape(1, -1), p["rw_ln_b"].reshape(1, -1),
            *_wkv_constants()]
    st_spec = pl.BlockSpec((1, 2, pairs, LANES, LANES), lambda b, j: (b, 0, j, 0, 0))
    if has_s0:
        in_specs.append(st_spec)
        args.append(s0_big)
    out_specs = [pl.BlockSpec((1, t_len, width), lambda b, j: (b, 0, j)),
                 pl.BlockSpec((1, 2, 2 * pairs, HEAD_DIM, HEAD_DIM), lambda b, j: (b, 0, j, 0, 0))]
    out_shape = [jax.ShapeDtypeStruct((bsz, t_len, RW_WIDTH), BF16),
                 jax.ShapeDtypeStruct((bsz, 2, RW_HEADS, HEAD_DIM, HEAD_DIM), F32)]
    for w in cast:
        assert PAIRS == pairs and w.shape[0] % (16 * bsz) == 0
        blk = pl.BlockSpec((w.shape[0] // bsz, w.shape[1]), lambda b, j: (b, 0))
        in_specs.append(blk)
        args.append(w)
        out_specs.append(blk)
        out_shape.append(jax.ShapeDtypeStruct(w.shape, BF16))
    n_units = 2 * pairs * (t_len // CHUNK)
    per_pair = pltpu.VMEM((pairs, t_len, LANES), F32)
    full = pltpu.VMEM((t_len, width), F32)
    scratch = [per_pair] * 9 + [full, full, per_pair, per_pair,
                                pltpu.VMEM((n_units, 2 * STACK, LANES), BF16),
                                pltpu.VMEM((n_units, 2 * STACK, LANES), F32),
                                pltpu.VMEM((2 * pairs, LANES, LANES), F32)]
    o_rw, s_new, *casted = pl.pallas_call(
        functools.partial(_rwkv_kernel, t_len=t_len, has_s0=has_s0, pairs=pairs, units=units, n_cast=len(cast)),
        grid=(bsz, PAIRS // pairs),
        in_specs=in_specs,
        out_specs=out_specs,
        out_shape=out_shape,
        scratch_shapes=scratch,
        compiler_params=_params(2),
        name="rwkv_branch",
    )(*args)
    return o_rw, s_new, casted


def _state_to_big(s0):
    bsz = s0.shape[0]
    x = s0.reshape(bsz, 2, PAIRS, 2, HEAD_DIM, 1, HEAD_DIM)
    same_head = np.eye(2, dtype=bool).reshape(2, 1, 2, 1)
    return jnp.where(same_head, x, 0.0).reshape(bsz, 2, PAIRS, LANES, LANES)


def _qk_norm(t, g, ones):
    ms = _head_sum(t * t, ones) * (1.0 / HEAD_DIM)
    return t * lax.rsqrt(ms + RMS_EPS) * g


def _nt(x, y):
    return lax.dot_general(x, y, (((1,), (1,)), ((), ())), preferred_element_type=F32)


def _na_ctx_kernel(q_ref, k_ref, v_ref, qg_ref, kg_ref, o_ref, kn_ref, vc_ref):
    ones = _head_ones()
    lo_half = lax.broadcasted_iota(jnp.int32, (1, LANES), 1) < HEAD_DIM
    lo = jnp.where(lo_half, 1.0, 0.0)
    t_len = q_ref.shape[1]
    qs, ks, vs = [], [], []
    for j in range(NA_WIDTH // LANES):
        cols = slice(j * LANES, (j + 1) * LANES)
        qn = _qk_norm(q_ref[0, :, cols], qg_ref[...], ones)
        kn = _qk_norm(k_ref[0, :, cols], kg_ref[...], ones)
        v = v_ref[0, :, cols]
        kn_ref[0, :, cols] = kn
        vc_ref[0, :, cols] = v
        qn = qn * QK_SCALE
        qs.append(jnp.concatenate([qn * lo, qn * (1.0 - lo)], axis=0).astype(BF16))
        ks.append(kn.astype(BF16))
        vs.append(v.astype(BF16))
    logits = [_nt(q, k) for q, k in zip(qs, ks)]
    ms = [jnp.max(s, axis=-1, keepdims=True) for s in logits]
    ps = [jnp.exp(s - m) for s, m in zip(logits, ms)]
    ls = [jnp.sum(p, axis=-1, keepdims=True) for p in ps]
    outs = [jnp.dot(p.astype(BF16), v, preferred_element_type=F32) / l for p, v, l in zip(ps, vs, ls)]
    for j, o in enumerate(outs):
        o_ref[0, :, j * LANES:(j + 1) * LANES] = jnp.where(lo_half, o[:t_len], o[t_len:]).astype(o_ref.dtype)


def _na_context(u_na, bsz, q_g, k_g):
    t_len = u_na.shape[1]
    tok = lambda seg: pl.BlockSpec((1, t_len, NA_WIDTH), lambda b: (b, 0, seg))
    out_blk = pl.BlockSpec((1, t_len, NA_WIDTH), lambda b: (b, 0, 0))
    g2 = lambda g: jnp.tile(g.reshape(1, HEAD_DIM), (1, 2))
    shp = jax.ShapeDtypeStruct((bsz, t_len, NA_WIDTH), F32)
    return pl.pallas_call(
        _na_ctx_kernel,
        grid=(bsz,),
        in_specs=[tok(0), tok(1), tok(2), _const_spec((1, LANES)), _const_spec((1, LANES))],
        out_specs=[out_blk, out_blk, out_blk],
        out_shape=[jax.ShapeDtypeStruct(shp.shape, BF16), shp, shp],
        compiler_params=_params(1),
        name="na_context",
    )(u_na, u_na, u_na, g2(q_g), g2(k_g))


NA_ROW_ILP = 8


def _na_lat_kernel(q_ref, k_ref, v_ref, kc_ref, vc_ref, qg_ref, kg_ref, tab_ref, o_ref,
                   q0_s, q1_s, kn_s, v_s, kc_s, vc_s, *, rows, kr):
    ones = _head_ones()
    lo_half = lax.broadcasted_iota(jnp.int32, (1, LANES), 1) < HEAD_DIM
    lo = jnp.where(lo_half, 1.0, 0.0)
    qn = _qk_norm(q_ref[0], qg_ref[...], ones) * QK_SCALE
    q0_s[...] = (qn * lo).astype(BF16)
    q1_s[...] = (qn * (1.0 - lo)).astype(BF16)
    kn_s[...] = _qk_norm(k_ref[0], kg_ref[...], ones).astype(BF16)
    v_s[...] = v_ref[0].astype(BF16)
    kc_s[...] = kc_ref[0].astype(BF16)
    vc_s[...] = vc_ref[0].astype(BF16)
    win = kr * GRID_W

    def body(it, carry):
        qs, k_rows, q_rows, biases = [], [], [], []
        for s in range(NA_ROW_ILP):
            i = it * NA_ROW_ILP + s
            r0 = jnp.clip(i - kr // 2, 0, rows - kr)
            d0 = r0 - i + (NA_ROWS - 1)
            qr = pl.ds(pl.multiple_of(i * GRID_W, GRID_W), GRID_W)
            q_rows.append(qr)
            k_rows.append(pl.ds(pl.multiple_of(r0 * GRID_W, GRID_W), win))
            qs.append(jnp.concatenate([q0_s[qr, :], q1_s[qr, :]], axis=0))
            biases.append(jnp.concatenate(
                [jnp.concatenate([tab_ref[h, d0 + 2 * m] for m in range(kr // 2)], axis=1) for h in range(2)],
                axis=0))
        lw = [_nt(q, kn_s[kr_, :]) + b for q, kr_, b in zip(qs, k_rows, biases)]
        lc = [_nt(q, kc_s[...]) for q in qs]
        ms = [jnp.maximum(jnp.max(a, axis=-1, keepdims=True), jnp.max(c, axis=-1, keepdims=True))
              for a, c in zip(lw, lc)]
        pw = [jnp.exp(a - m) for a, m in zip(lw, ms)]
        pc = [jnp.exp(c - m) for c, m in zip(lc, ms)]
        ls = [jnp.sum(a, axis=-1, keepdims=True) + jnp.sum(c, axis=-1, keepdims=True) for a, c in zip(pw, pc)]
        outs = [(jnp.dot(a.astype(BF16), v_s[kr_, :], preferred_element_type=F32)
                 + jnp.dot(c.astype(BF16), vc_s[...], preferred_element_type=F32)) / l
                for a, c, kr_, l in zip(pw, pc, k_rows, ls)]
        for qr, o in zip(q_rows, outs):
            o_ref[0, qr, :] = jnp.where(lo_half, o[:GRID_W], o[GRID_W:]).astype(o_ref.dtype)
        return carry

    lax.fori_loop(0, rows // NA_ROW_ILP, body, 0)


def _latent_bias_table(rpb):
    qc = np.arange(GRID_W)[:, None]
    kc = np.arange(GRID_W)[None, :]
    ws = np.clip(qc - NA_COLS // 2, 0, GRID_W - NA_COLS)
    valid = (kc >= ws) & (kc < ws + NA_COLS)
    dc = np.clip(kc - qc, -(NA_COLS - 1), NA_COLS - 1) + NA_COLS - 1
    onehot = (dc[None] == np.arange(2 * NA_COLS - 1)[:, None, None]).astype(np.float32)
    cb = jnp.einsum("hdc,cqk->hdqk", rpb, jnp.asarray(onehot), precision=lax.Precision.HIGHEST)
    cb = jnp.where(valid[None, None], cb, NEG_INF)
    return jnp.concatenate([cb[:, :-1], cb[:, 1:]], axis=-1)


def _na_latent(u_na, bsz, b_off, k_ctx, v_ctx, q_g, k_g, rpb):
    t_len = u_na.shape[1]
    rows = t_len // GRID_W
    kr = min(NA_ROWS, rows)
    assert kr % 2 == 0 and rows % NA_ROW_ILP == 0
    ctx_len = k_ctx.shape[1]
    seg = NA_WIDTH // LANES
    tok = lambda off: pl.BlockSpec((1, t_len, LANES), lambda b, j: (b_off + b, 0, off + j))
    ctx = pl.BlockSpec((1, ctx_len, LANES), lambda b, j: (b, 0, j))
    g2 = lambda g: jnp.tile(g.reshape(1, HEAD_DIM), (1, 2))
    tab = _latent_bias_table(rpb)
    tok_s = pltpu.VMEM((t_len, LANES), BF16)
    ctx_s = pltpu.VMEM((ctx_len, LANES), BF16)
    return pl.pallas_call(
        functools.partial(_na_lat_kernel, rows=rows, kr=kr),
        grid=(bsz, seg),
        in_specs=[tok(0), tok(seg), tok(2 * seg), ctx, ctx,
                  _const_spec((1, LANES)), _const_spec((1, LANES)),
                  pl.BlockSpec((2, 2 * NA_ROWS - 2, GRID_W, 2 * GRID_W), lambda b, j: (j, 0, 0, 0))],
        out_specs=pl.BlockSpec((1, t_len, LANES), lambda b, j: (b, 0, j)),
        out_shape=jax.ShapeDtypeStruct((bsz, t_len, NA_WIDTH), BF16),
        scratch_shapes=[tok_s, tok_s, tok_s, tok_s, ctx_s, ctx_s],
        compiler_params=_params(2),
        name="na_latent",
    )(u_na, u_na, u_na, k_ctx, v_ctx, g2(q_g), g2(k_g), tab)


LATE_WEIGHTS = ("w_o_rwkv", "w_o_na", "w_out", "ffn_w1", "ffn_w3", "ffn_w2")


def _out_ffn_kernel(xa_ref, xb_ref, orwa_ref, orwb_ref, onaa_ref, onab_ref, gt_ref, mod_ref, g_ref,
                    wor_ref, won_ref, wout_ref, w1_ref, w3_ref, w2_ref, ya_ref, yb_ref, *, tiles_a):
    i = pl.program_id(0)
    x = _pick_group(i, tiles_a, xa_ref, xb_ref)
    o_rw = _pick_group(i, tiles_a, orwa_ref, orwb_ref)
    o_na = _pick_group(i, tiles_a, onaa_ref, onab_ref)
    g_rw = _sigmoid(gt_ref[:, :D_MODEL].astype(F32))
    g_na = _sigmoid(gt_ref[:, D_MODEL:].astype(F32))
    merged = g_rw * _dot(o_rw, wor_ref[...]) + g_na * _dot(o_na, won_ref[...])
    x1 = x + _mod_part(mod_ref, 2) * _dot(merged, wout_ref[...])
    h2 = _rms_rows(x1) * g_ref[...]
    h2 = (h2 * (1.0 + _mod_part(mod_ref, 4)) + _mod_part(mod_ref, 3)).astype(BF16)
    acc = jnp.zeros(x1.shape, F32)
    for c in range(FF_HIDDEN // FF_CHUNK):
        cols = slice(c * FF_CHUNK, (c + 1) * FF_CHUNK)
        a = jnp.dot(h2, w1_ref[:, cols], preferred_element_type=F32)
        b = jnp.dot(h2, w3_ref[:, cols], preferred_element_type=F32)
        hh = (a * _sigmoid(a) * b).astype(BF16)
        acc = acc + jnp.dot(hh, w2_ref[cols, :], preferred_element_type=F32)
    y = x1 + _mod_part(mod_ref, 5) * acc

    @pl.when(i < tiles_a)
    def _():
        ya_ref[0] = y

    @pl.when(i >= tiles_a)
    def _():
        yb_ref[0] = y


def _out_ffn(xa, xb, orw_a, orw_b, ona_a, ona_b, gates, tiles_per_b, mod_all, norm_g, wb):
    tm = TOKEN_TILE
    tiles_a, tiles = xa.shape[0], xa.shape[0] + xb.shape[0]
    weights = [wb[n] for n in LATE_WEIGHTS]
    groups = lambda width: _group_specs(tiles_a, tm, width)
    return pl.pallas_call(
        functools.partial(_out_ffn_kernel, tiles_a=tiles_a),
        grid=(tiles,),
        in_specs=groups(D_MODEL) + groups(RW_WIDTH) + groups(NA_WIDTH)
        + [pl.BlockSpec((tm, GATE_COLS), lambda i: (i, 0)), _mod_spec(tiles_a, tiles_per_b),
           _const_spec((1, D_MODEL))] + [_const_spec(w.shape) for w in weights],
        out_specs=groups(D_MODEL),
        out_shape=[jax.ShapeDtypeStruct(xa.shape, F32), jax.ShapeDtypeStruct(xb.shape, F32)],
        compiler_params=_params(1),
        name="out_ffn",
    )(xa, xb, orw_a, orw_b, ona_a, ona_b, gates, mod_all, norm_g.reshape(1, -1), *weights)


def kernel(x_prompt, x_sample, state_rwkv, cache_na_k, cache_na_v, c, c_ctx, norm1_g, norm2_g, w_ada, b_ada,
           w_in, shift_mu, rw_w0, rw_w_up, rw_a0, rw_a_up, rw_g_up, rw_k_k, rw_k_a, rw_r_k, rw_ln_g, rw_ln_b,
           na_q_g, na_k_g, na_rpb, w_o_rwkv, w_o_na, w_out, ffn_w1, ffn_w3, ffn_w2):
    depth = w_in.shape[0]
    bsz, seq = x_prompt.shape[:2]
    dec, dec_seq = x_sample.shape[:2]
    tm = TOKEN_TILE
    n_ctx = bsz * seq
    assert n_ctx % tm == 0 and dec_seq % tm == 0 and n_ctx % dec_seq == 0
    tiles_per_b = dec_seq // tm
    tiled = lambda t: t.reshape(-1, tm, t.shape[-1])
    cvecs = jnp.concatenate([c_ctx[None, :], c], axis=0).T
    y_p, y_s = x_prompt, x_sample
    new_s, new_k, new_v = [], [], []
    for l in range(depth):
        p = dict(norm1_g=norm1_g[l], norm2_g=norm2_g[l], shift_mu=shift_mu[l], rw_w0=rw_w0[l],
                 rw_w_up=rw_w_up[l], rw_a0=rw_a0[l], rw_a_up=rw_a_up[l], rw_g_up=rw_g_up[l],
                 rw_k_k=rw_k_k[l], rw_k_a=rw_k_a[l], rw_r_k=rw_r_k[l], rw_ln_g=rw_ln_g[l],
                 rw_ln_b=rw_ln_b[l], na_q_g=na_q_g[l], na_k_g=na_k_g[l], na_rpb=na_rpb[l])
        late_f32 = [w[l] for w in (w_o_rwkv, w_o_na, w_out, ffn_w1, ffn_w3, ffn_w2)]
        mod_all, w_in_bf = _modulation(cvecs, w_ada[l], b_ada[l], w_in[l])
        u_rw, u_na, gates = _in_proj(tiled(y_p), tiled(y_s), tiles_per_b, mod_all, p["norm1_g"], w_in_bf)
        ctx_view = lambda t: t.reshape(-1, seq, t.shape[-1])
        lat_view = lambda t: t.reshape(-1, dec_seq, t.shape[-1])
        lat_off = n_ctx // dec_seq
        o_rw_p, s_l, casted = _rwkv_branch(ctx_view(u_rw), bsz, 0, None, p, RWKV_PAIRS_CTX, RWKV_UNITS, late_f32)
        late = dict(zip(LATE_WEIGHTS, casted))
        o_na_p, k_l, v_l = _na_context(ctx_view(u_na), bsz, p["na_q_g"], p["na_k_g"])
        new_s.append(s_l)
        new_k.append(k_l.reshape(bsz, seq, NA_HEADS, HEAD_DIM))
        new_v.append(v_l.reshape(bsz, seq, NA_HEADS, HEAD_DIM))
        ctx_k = cache_na_k[:, l].reshape(dec, -1, NA_WIDTH)
        ctx_v = cache_na_v[:, l].reshape(dec, -1, NA_WIDTH)
        o_rw_s, _, _ = _rwkv_branch(lat_view(u_rw), dec, lat_off, _state_to_big(state_rwkv[:, l]), p, RWKV_PAIRS_LAT,
                                   RWKV_UNITS)
        o_na_s = _na_latent(lat_view(u_na), dec, lat_off, ctx_k, ctx_v, p["na_q_g"], p["na_k_g"], p["na_rpb"])
        y_p_t, y_s_t = _out_ffn(tiled(y_p), tiled(y_s), tiled(o_rw_p), tiled(o_rw_s), tiled(o_na_p), tiled(o_na_s),
                                gates, tiles_per_b, mod_all, p["norm2_g"], late)
        y_p, y_s = y_p_t.reshape(x_prompt.shape), y_s_t.reshape(x_sample.shape)
    return (y_p, y_s, jnp.stack(new_s, axis=1), jnp.stack(new_k, axis=1), jnp.stack(new_v, axis=1))
```

```python
import functools

import numpy as np
import jax
import jax.numpy as jnp
from jax import lax
from jax.experimental import pallas as pl
from jax.experimental.pallas import tpu as pltpu

D_MODEL = 1024
GRID_W = 64
HEAD_DIM = 64
RW_HEADS = 8
RW_WIDTH = RW_HEADS * HEAD_DIM
NA_HEADS = 8
NA_WIDTH = NA_HEADS * HEAD_DIM
LORA_DECAY = 64
LORA_ICLR = 64
LORA_GATE = 128
NA_ROWS = 8
NA_COLS = 16
FF_HIDDEN = 2816
RW_COLS = 3 * RW_WIDTH + 2 * LORA_DECAY + 2 * LORA_ICLR + LORA_GATE
NA_IN_COLS = 3 * NA_WIDTH
GATE_COLS = 2 * D_MODEL
RMS_EPS = 1e-6
GN_EPS = 64e-5
L2_EPS = 1e-12
NEG_INF = -1e30
DECAY_SCALE = float(np.exp(-0.5))
QK_SCALE = HEAD_DIM ** -0.5
assert QK_SCALE == 0.125

LANES = 128
PAIRS = RW_HEADS // 2
CHUNK = 64
STACK = 2 * CHUNK
RWKV_PAIRS_CTX = 4
RWKV_PAIRS_LAT = 2
RWKV_UNITS = 16
TOKEN_TILE = 512
FF_CHUNK = 256
VMEM_LIMIT = 56 * 1024 * 1024

F32 = jnp.float32
BF16 = jnp.bfloat16


def _dot(a, b):
    return jnp.dot(a.astype(BF16), b.astype(BF16), preferred_element_type=F32)


def _split2(x):
    hi = x.astype(BF16)
    lo = (x - hi.astype(F32)).astype(BF16)
    return hi, lo


def _dot_exact_lhs(a_exact, b):
    h, l = _split2(b)
    d = lambda x: jnp.dot(a_exact, x, preferred_element_type=F32)
    return d(h) + d(l)


def _head_ones():
    r = lax.broadcasted_iota(jnp.int32, (LANES, LANES), 0) // HEAD_DIM
    c = lax.broadcasted_iota(jnp.int32, (LANES, LANES), 1) // HEAD_DIM
    return jnp.where(r == c, 1.0, 0.0).astype(BF16)


def _head_sum(x, ones):
    return jnp.dot(x.astype(BF16), ones, preferred_element_type=F32)


def _sigmoid(x):
    return 0.5 * jnp.tanh(0.5 * x) + 0.5


def _rms_rows(x):
    return x * lax.rsqrt(jnp.mean(x * x, axis=-1, keepdims=True) + RMS_EPS)


def _const_spec(shape):
    nd = len(shape)
    return pl.BlockSpec(shape, lambda *_: (0,) * nd, pipeline_mode=pl.Buffered(1))


def _params(n_axes):
    return pltpu.CompilerParams(dimension_semantics=("arbitrary",) * n_axes,
                                vmem_limit_bytes=VMEM_LIMIT)


def _mod_kernel(c_ref, w_ref, b_ref, win_ref, o_ref, winb_ref):
    s = c_ref[...]
    s = s * _sigmoid(s)
    w = w_ref[...]
    for r in range(s.shape[1]):
        o_ref[r] = jnp.sum(w * s[:, r:r + 1], axis=0, keepdims=True) + b_ref[...]
    winb_ref[...] = win_ref[...].astype(BF16)


def _modulation(cvecs, w_ada, b_ada, w_in):
    n = cvecs.shape[1]
    tn = 768
    steps = 6 * D_MODEL // tn
    rows = w_in.shape[0] // steps
    assert rows % 16 == 0
    return pl.pallas_call(
        _mod_kernel,
        grid=(steps,),
        in_specs=[pl.BlockSpec((D_MODEL, n), lambda j: (0, 0)),
                  pl.BlockSpec((D_MODEL, tn), lambda j: (0, j)),
                  pl.BlockSpec((1, tn), lambda j: (0, j)),
                  pl.BlockSpec((rows, w_in.shape[1]), lambda j: (j, 0))],
        out_specs=[pl.BlockSpec((n, 1, tn), lambda j: (0, 0, j)),
                   pl.BlockSpec((rows, w_in.shape[1]), lambda j: (j, 0))],
        out_shape=[jax.ShapeDtypeStruct((n, 1, 6 * D_MODEL), F32),
                   jax.ShapeDtypeStruct(w_in.shape, BF16)],
        compiler_params=_params(1),
        name="modulation",
    )(cvecs, w_ada, b_ada.reshape(1, -1), w_in)


def _pick_group(i, tiles_a, a_ref, b_ref):
    return jnp.where(i < tiles_a, a_ref[0], b_ref[0])


def _group_specs(tiles_a, tm, width):
    return [pl.BlockSpec((1, tm, width), lambda i: (jnp.minimum(i, tiles_a - 1), 0, 0)),
            pl.BlockSpec((1, tm, width), lambda i: (jnp.maximum(i - tiles_a, 0), 0, 0))]


def _mod_spec(tiles_a, tiles_per_b):
    return pl.BlockSpec((1, 1, 6 * D_MODEL),
                        lambda i: (jnp.where(i < tiles_a, 0, 1 + (i - tiles_a) // tiles_per_b), 0, 0))


def _mod_part(mod_ref, k):
    return mod_ref[0, :, k * D_MODEL:(k + 1) * D_MODEL]


def _inproj_kernel(xa_ref, xb_ref, mod_ref, g_ref, w_ref, urw_ref, una_ref, gt_ref, *, tiles_a):
    x = _pick_group(pl.program_id(0), tiles_a, xa_ref, xb_ref)
    h = _rms_rows(x) * g_ref[...]
    h = (h * (1.0 + _mod_part(mod_ref, 1)) + _mod_part(mod_ref, 0)).astype(BF16)
    d = lambda lo, hi: jnp.dot(h, w_ref[:, lo:hi], preferred_element_type=F32)
    urw_ref[...] = d(0, RW_COLS)
    una_ref[...] = d(RW_COLS, RW_COLS + NA_IN_COLS)
    gt_ref[...] = d(RW_COLS + NA_IN_COLS, RW_COLS + NA_IN_COLS + GATE_COLS).astype(BF16)


def _in_proj(xa, xb, tiles_per_b, mod_all, norm_g, w_in_bf):
    tm = TOKEN_TILE
    tiles_a, tiles = xa.shape[0], xa.shape[0] + xb.shape[0]
    row = lambda i: (i, 0)
    return pl.pallas_call(
        functools.partial(_inproj_kernel, tiles_a=tiles_a),
        grid=(tiles,),
        in_specs=_group_specs(tiles_a, tm, D_MODEL) + [_mod_spec(tiles_a, tiles_per_b),
                                                       _const_spec((1, D_MODEL)), _const_spec(w_in_bf.shape)],
        out_specs=[pl.BlockSpec((tm, RW_COLS), row),
                   pl.BlockSpec((tm, NA_IN_COLS), row),
                   pl.BlockSpec((tm, GATE_COLS), row)],
        out_shape=[jax.ShapeDtypeStruct((tiles * tm, RW_COLS), F32),
                   jax.ShapeDtypeStruct((tiles * tm, NA_IN_COLS), F32),
                   jax.ShapeDtypeStruct((tiles * tm, GATE_COLS), BF16)],
        compiler_params=_params(1),
        name="in_proj",
    )(xa, xb, mod_all, norm_g.reshape(1, -1), w_in_bf)


def _shift(x, mu):
    t_len = x.shape[0]
    row = lax.broadcasted_iota(jnp.int32, x.shape, 0)
    prev = jnp.where(row == 0, 0.0, pltpu.roll(x, 1, 0))
    nxt = jnp.where(row == t_len - 1, 0.0, pltpu.roll(x, t_len - 1, 0))
    return x + mu[0:1, :] * (prev - x) + mu[1:2, :] * (nxt - x)


def _stack_heads(x, lane_lo):
    return jnp.concatenate([x * lane_lo, x * (1.0 - lane_lo)], axis=0)


def _wkv_intra(units, consts):
    tri, mask_s, mask_i, eye, lane_lo, blk = consts
    stack = lambda z: _stack_heads(z, lane_lo)

    cums = [_dot_exact_lhs(tri[int(u[6])], u[1]) for u in units]

    prep = []
    for (r, lw, kd, v, kk, b, reverse), cum in zip(units, cums):
        mid_row = CHUNK // 2 if reverse else CHUNK // 2 - 1
        tot_row = 0 if reverse else CHUNK - 1
        a = -kk
        ex = cum - lw
        mid = cum[mid_row:mid_row + 1, :]
        tot = cum[tot_row:tot_row + 1, :]
        up = jnp.exp(cum - mid)
        dn = jnp.exp(mid - cum)
        tail = jnp.exp(tot - cum)
        prep.append(dict(
            at_m=stack(a * jnp.exp(ex - mid)).astype(BF16),
            rt_m=stack(r * up).astype(BF16),
            bk_m=jnp.concatenate([b * dn, kd * dn], axis=0).astype(BF16),
            a_e=stack(a * jnp.exp(ex)),
            r_e=stack(r * jnp.exp(cum)),
            bk_t=jnp.concatenate([stack(b * tail), stack(kd * tail)], axis=0).T.astype(BF16),
            vv=stack(v).astype(BF16),
            diag=jnp.where(eye, jnp.exp(tot), 0.0),
            rev=int(reverse)))

    ntd = lambda x, y: lax.dot_general(x, y, (((1,), (1,)), ((), ())), preferred_element_type=F32)
    mm = lambda x, y: jnp.dot(x, y, preferred_element_type=F32)
    diag_blk, swap_eye = blk
    roll_head = lambda z: pltpu.roll(z, HEAD_DIM, 1)
    by_block = lambda z: jnp.concatenate([z[:CHUNK], roll_head(z[CHUNK:])], axis=0)
    top = [by_block(ntd(p["at_m"], p["bk_m"])) for p in prep]
    low = [by_block(ntd(p["rt_m"], p["bk_m"])) for p in prep]
    a_ak = [(roll_head(t) * mask_s[p["rev"]]).astype(BF16) for t, p in zip(top, prep)]
    bot = [jnp.concatenate([t * mask_i[p["rev"]], roll_head(t) * mask_i[p["rev"]]], axis=1).astype(BF16)
           for t, p in zip(low, prep)]

    off_blk = 1.0 - diag_blk
    both = [t * mask_s[p["rev"]] + swap_eye for t, p in zip(top, prep)]
    steps = CHUNK.bit_length() - 1
    diag_bf = diag_blk.astype(BF16)
    for j in range(steps):
        packed = [q.astype(BF16) for q in both]
        res = [mm(qb * diag_bf, qb) for qb in packed]
        both = [r + off_blk * q for r, q in zip(res, both)]
    ts = [pltpu.roll(q, HEAD_DIM, 1).astype(BF16) for q in both]
    x0 = [jnp.concatenate([p["a_e"], mm(ak, p["vv"])], axis=1) for p, ak in zip(prep, a_ak)]
    xs = [mm(t, x.astype(BF16)) for x, t in zip(x0, ts)]

    out = []
    zeros = jnp.zeros((STACK, LANES), BF16)
    for p, x, bt in zip(prep, xs, bot):
        rhs = jnp.concatenate([x.astype(BF16), jnp.concatenate([zeros, p["vv"]], axis=1)], axis=0)
        lhs = jnp.concatenate([bt, p["bk_t"]], axis=0)
        res = mm(lhs, rhs)
        lhs2 = res[:, :LANES] + jnp.concatenate([p["r_e"], p["diag"]], axis=0)
        out.append((lhs2.astype(BF16), res[:, LANES:]))
    return out


def _wkv_constants():
    lane_head = np.arange(LANES) // HEAD_DIM
    ones = (lane_head[:, None] == lane_head[None, :]).astype(np.float32)
    t = np.arange(CHUNK)
    tri = np.stack([t[None, :] <= t[:, None], t[None, :] >= t[:, None]]).astype(np.float32)
    rs, cs = t[:, None], t[None, :]
    tri_masks = np.stack([cs < rs, cs > rs, cs <= rs, cs >= rs]).astype(np.float32)
    masks = np.kron(np.eye(2, dtype=np.float32), tri_masks)
    return jnp.asarray(ones, BF16), jnp.asarray(tri, BF16), jnp.asarray(masks, F32)


def _rwkv_kernel(*refs, t_len, has_s0, pairs, units, n_cast):
    (r_ref, k_ref, v_ref, lo_ref, mur_ref, muk_ref, muv_ref, mul_ref, w0_ref, a0_ref, wup_ref, aup_ref,
     gup_ref, kk_ref, ka_ref, rk_ref, lng_ref, lnb_ref, ones_ref, tri_ref, mask_ref) = refs[:21]
    pos = 21
    s0_ref = None
    if has_s0:
        s0_ref = refs[pos]
        pos += 1
    cast_in = refs[pos:pos + n_cast]
    pos += n_cast
    o_ref, sn_ref = refs[pos], refs[pos + 1]
    cast_out = refs[pos + 2:pos + 2 + n_cast]
    (r_s, v_s, kk_s, b0_s, b1_s, lw0_s, lw1_s, kd0_s, kd1_s, gate_s, bonus_s, yf_s, yb_s,
     lhs_s, add_s, st_s) = refs[pos + 2 + n_cast:]
    for w_in_ref, w_out_ref in zip(cast_in, cast_out):
        w_out_ref[...] = w_in_ref[...].astype(BF16)

    ones = ones_ref[...]
    lane = lax.broadcasted_iota(jnp.int32, (1, LANES), 1)
    lo_half = lane < HEAD_DIM
    lane_lo = jnp.where(lo_half, 1.0, 0.0)
    mm = lambda x, y: jnp.dot(x, y, preferred_element_type=F32)

    lo = _shift(lo_ref[0], mul_ref[...])
    wd = jnp.tanh(lo[:, 0:LANES])
    ad = lo[:, LANES:2 * LANES]
    sig_gd = _sigmoid(lo[:, 2 * LANES:3 * LANES]).astype(BF16)
    wd_split = [_split2(wd * m) for m in (lane_lo, 1.0 - lane_lo)]
    ad_bf = [(ad * m).astype(BF16) for m in (lane_lo, 1.0 - lane_lo)]
    for j in range(pairs):
        cols = slice(j * LANES, (j + 1) * LANES)
        r = _shift(r_ref[0, :, cols], mur_ref[:, cols])
        k = _shift(k_ref[0, :, cols], muk_ref[:, cols])
        v = _shift(v_ref[0, :, cols], muv_ref[:, cols])
        kk = k * kk_ref[:, cols]
        kk = kk * lax.rsqrt(_head_sum(kk * kk, ones) + L2_EPS)
        wup_h, wup_l = _split2(wup_ref[:, cols])
        aup = aup_ref[:, cols].astype(BF16)
        kdirs = []
        for e, (lw_s, kd_s, b_s) in enumerate(((lw0_s, kd0_s, b0_s), (lw1_s, kd1_s, b1_s))):
            wd_h, wd_l = wd_split[e]
            w_lin = w0_ref[e:e + 1, cols] + (mm(wd_h, wup_h) + mm(wd_l, wup_h) + mm(wd_h, wup_l))
            lw_s[j] = -DECAY_SCALE * _sigmoid(w_lin)
            iclr = _sigmoid(a0_ref[e:e + 1, cols] + mm(ad_bf[e], aup))
            kd = k * (1.0 + (iclr - 1.0) * ka_ref[:, cols])
            kd_s[j] = kd
            b_s[j] = kk * iclr
            kdirs.append(kd)
        gate_s[:, cols] = mm(sig_gd, gup_ref[:, cols].astype(BF16))
        bonus_s[:, cols] = _head_sum(r * (0.5 * (kdirs[0] + kdirs[1])) * rk_ref[:, cols], ones) * v
        r_s[j] = r
        v_s[j] = v
        kk_s[j] = kk

    n_chunks = t_len // CHUNK
    chunks_per = min(n_chunks, units // 2)
    pairs_per = min(pairs, units // (2 * chunks_per))
    groups = n_chunks // chunks_per
    rs = lax.broadcasted_iota(jnp.int32, (STACK, STACK), 0)
    cs = lax.broadcasted_iota(jnp.int32, (STACK, STACK), 1)
    as_f32 = lambda m: jnp.where(m, 1.0, 0.0)
    blk = (as_f32(rs // CHUNK == cs // CHUNK), as_f32(cs == (rs + CHUNK) % STACK))
    consts = ((tri_ref[0], tri_ref[1]), (mask_ref[0], mask_ref[1]), (mask_ref[2], mask_ref[3]),
              rs == cs, lane_lo, blk)
    dirs = ((lw0_s, kd0_s, b0_s), (lw1_s, kd1_s, b1_s))

    def intra_body(it, carry):
        pg = it // groups
        g = it % groups
        units_, ids = [], []
        for jj in range(pairs_per):
            j = pg * pairs_per + jj
            for cc in range(chunks_per):
                c = g * chunks_per + cc
                rows = pl.ds(pl.multiple_of(c * CHUNK, CHUNK), CHUNK)
                for e, (lw_s, kd_s, b_s) in enumerate(dirs):
                    units_.append((r_s[j, rows, :], lw_s[j, rows, :], kd_s[j, rows, :], v_s[j, rows, :],
                                   kk_s[j, rows, :], b_s[j, rows, :], e == 1))
                    ids.append((j * 2 + e) * n_chunks + c)
        for uid, (lhs, add) in zip(ids, _wkv_intra(units_, consts)):
            lhs_s[uid] = lhs
            add_s[uid] = add
        return carry

    lax.fori_loop(0, (pairs // pairs_per) * groups, intra_body, 0, unroll=2)

    for j in range(pairs):
        for e in range(2):
            if has_s0:
                st_s[2 * j + e] = s0_ref[0, e, j].T
            else:
                st_s[2 * j + e] = jnp.zeros((LANES, LANES), F32)

    def state_body(it, carry):
        chunk = (it, n_chunks - 1 - it)
        uids = [(j * 2 + e) * n_chunks + chunk[e] for j in range(pairs) for e in range(2)]
        sts = [st_s[ch].astype(BF16) for ch in range(2 * pairs)]
        res = [mm(lhs_s[uid], st) + add_s[uid] for uid, st in zip(uids, sts)]
        for ch, rr in enumerate(res):
            j, e = divmod(ch, 2)
            y_s = yb_s if e else yf_s
            y_s[j, pl.ds(pl.multiple_of(chunk[e] * CHUNK, CHUNK), CHUNK), :] = rr[:CHUNK] + rr[CHUNK:STACK]
            st_s[ch] = rr[STACK:]
        return carry

    lax.fori_loop(0, n_chunks, state_body, 0, unroll=True)
    for j in range(pairs):
        for e in range(2):
            st_t = st_s[2 * j + e].T
            sn_ref[0, e, 2 * j] = st_t[:HEAD_DIM, :HEAD_DIM]
            sn_ref[0, e, 2 * j + 1] = pltpu.roll(st_t, HEAD_DIM, 1)[HEAD_DIM:, :HEAD_DIM]

    inv_d = 1.0 / HEAD_DIM
    for j in range(pairs):
        cols = slice(j * LANES, (j + 1) * LANES)
        y = yf_s[j] + yb_s[j]
        mean = _head_sum(y, ones) * inv_d
        dlt = y - mean
        var = _head_sum(dlt * dlt, ones) * inv_d
        yn = dlt * lax.rsqrt(var + GN_EPS) * lng_ref[:, cols] + lnb_ref[:, cols]
        o_ref[0, :, cols] = ((yn + bonus_s[:, cols]) * gate_s[:, cols]).astype(o_ref.dtype)


def _rwkv_branch(u_rw, bsz, b_off, s0_big, p, pairs, units, cast=()):
    t_len = u_rw.shape[1]
    has_s0 = s0_big is not None
    width = pairs * LANES
    seg = RW_WIDTH // width
    tok = lambda off: pl.BlockSpec((1, t_len, width), lambda b, j: (b_off + b, 0, off + j))
    mu = lambda off: pl.BlockSpec((2, width), lambda b, j: (0, off + j))
    vec2 = pl.BlockSpec((2, width), lambda b, j: (0, j))
    vec1 = pl.BlockSpec((1, width), lambda b, j: (0, j))
    mat = pl.BlockSpec((LANES, width), lambda b, j: (0, j))
    lora_w = 3 * LANES
    lora_blk = 3 * RW_WIDTH // lora_w
    in_specs = [tok(0), tok(seg), tok(2 * seg),
                pl.BlockSpec((1, t_len, lora_w), lambda b, j: (b_off + b, 0, lora_blk)),
                mu(0), mu(seg), mu(2 * seg),
                pl.BlockSpec((2, lora_w), lambda b, j: (0, lora_blk)),
                vec2, vec2, mat, mat, mat, vec1, vec1, vec1, vec1, vec1,
                _const_spec((LANES, LANES)), _const_spec((2, CHUNK, CHUNK)), _const_spec((4, STACK, STACK))]
    args = [u_rw, u_rw, u_rw, u_rw, p["shift_mu"], p["shift_mu"], p["shift_mu"], p["shift_mu"],
            p["rw_w0"], p["rw_a0"],
            p["rw_w_up"].reshape(2 * LORA_DECAY, RW_WIDTH), p["rw_a_up"].reshape(2 * LORA_ICLR, RW_WIDTH),
            p["rw_g_up"], p["rw_k_k"].reshape(1, -1), p["rw_k_a"].reshape(1, -1),
            p["rw_r_k"].reshape(1, -1), p["rw_ln_g"].reshape(1, -1), p["rw_ln_b"].reshape(1, -1),
            *_wkv_constants()]
    st_spec = pl.BlockSpec((1, 2, pairs, LANES, LANES), lambda b, j: (b, 0, j, 0, 0))
    if has_s0:
        in_specs.append(st_spec)
        args.append(s0_big)
    out_specs = [pl.BlockSpec((1, t_len, width), lambda b, j: (b, 0, j)),
                 pl.BlockSpec((1, 2, 2 * pairs, HEAD_DIM, HEAD_DIM), lambda b, j: (b, 0, j, 0, 0))]
    out_shape = [jax.ShapeDtypeStruct((bsz, t_len, RW_WIDTH), BF16),
                 jax.ShapeDtypeStruct((bsz, 2, RW_HEADS, HEAD_DIM, HEAD_DIM), F32)]
    for w in cast:
        assert PAIRS == pairs and w.shape[0] % (16 * bsz) == 0
        blk = pl.BlockSpec((w.shape[0] // bsz, w.shape[1]), lambda b, j: (b, 0))
        in_specs.append(blk)
        args.append(w)
        out_specs.append(blk)
        out_shape.append(jax.ShapeDtypeStruct(w.shape, BF16))
    n_units = 2 * pairs * (t_len // CHUNK)
    per_pair = pltpu.VMEM((pairs, t_len, LANES), F32)
    full = pltpu.VMEM((t_len, width), F32)
    scratch = [per_pair] * 9 + [full, full, per_pair, per_pair,
                                pltpu.VMEM((n_units, 2 * STACK, LANES), BF16),
                                pltpu.VMEM((n_units, 2 * STACK, LANES), F32),
                                pltpu.VMEM((2 * pairs, LANES, LANES), F32)]
    o_rw, s_new, *casted = pl.pallas_call(
        functools.partial(_rwkv_kernel, t_len=t_len, has_s0=has_s0, pairs=pairs, units=units, n_cast=len(cast)),
        grid=(bsz, PAIRS // pairs),
        in_specs=in_specs,
        out_specs=out_specs,
        out_shape=out_shape,
        scratch_shapes=scratch,
        compiler_params=_params(2),
        name="rwkv_branch",
    )(*args)
    return o_rw, s_new, casted


def _state_to_big(s0):
    bsz = s0.shape[0]
    x = s0.reshape(bsz, 2, PAIRS, 2, HEAD_DIM, 1, HEAD_DIM)
    same_head = np.eye(2, dtype=bool).reshape(2, 1, 2, 1)
    return jnp.where(same_head, x, 0.0).reshape(bsz, 2, PAIRS, LANES, LANES)


def _qk_norm(t, g, ones):
    ms = _head_sum(t * t, ones) * (1.0 / HEAD_DIM)
    return t * lax.rsqrt(ms + RMS_EPS) * g


def _nt(x, y):
    return lax.dot_general(x, y, (((1,), (1,)), ((), ())), preferred_element_type=F32)


def _na_ctx_kernel(q_ref, k_ref, v_ref, qg_ref, kg_ref, o_ref, kn_ref, vc_ref):
    ones = _head_ones()
    lo_half = lax.broadcasted_iota(jnp.int32, (1, LANES), 1) < HEAD_DIM
    lo = jnp.where(lo_half, 1.0, 0.0)
    t_len = q_ref.shape[1]
    qs, ks, vs = [], [], []
    for j in range(NA_WIDTH // LANES):
        cols = slice(j * LANES, (j + 1) * LANES)
        qn = _qk_norm(q_ref[0, :, cols], qg_ref[...], ones)
        kn = _qk_norm(k_ref[0, :, cols], kg_ref[...], ones)
        v = v_ref[0, :, cols]
        kn_ref[0, :, cols] = kn
        vc_ref[0, :, cols] = v
        qn = qn * QK_SCALE
        qs.append(jnp.concatenate([qn * lo, qn * (1.0 - lo)], axis=0).astype(BF16))
        ks.append(kn.astype(BF16))
        vs.append(v.astype(BF16))
    logits = [_nt(q, k) for q, k in zip(qs, ks)]
    ms = [jnp.max(s, axis=-1, keepdims=True) for s in logits]
    ps = [jnp.exp(s - m) for s, m in zip(logits, ms)]
    ls = [jnp.sum(p, axis=-1, keepdims=True) for p in ps]
    outs = [jnp.dot(p.astype(BF16), v, preferred_element_type=F32) / l for p, v, l in zip(ps, vs, ls)]
    for j, o in enumerate(outs):
        o_ref[0, :, j * LANES:(j + 1) * LANES] = jnp.where(lo_half, o[:t_len], o[t_len:]).astype(o_ref.dtype)


def _na_context(u_na, bsz, q_g, k_g):
    t_len = u_na.shape[1]
    tok = lambda seg: pl.BlockSpec((1, t_len, NA_WIDTH), lambda b: (b, 0, seg))
    out_blk = pl.BlockSpec((1, t_len, NA_WIDTH), lambda b: (b, 0, 0))
    g2 = lambda g: jnp.tile(g.reshape(1, HEAD_DIM), (1, 2))
    shp = jax.ShapeDtypeStruct((bsz, t_len, NA_WIDTH), F32)
    return pl.pallas_call(
        _na_ctx_kernel,
        grid=(bsz,),
        in_specs=[tok(0), tok(1), tok(2), _const_spec((1, LANES)), _const_spec((1, LANES))],
        out_specs=[out_blk, out_blk, out_blk],
        out_shape=[jax.ShapeDtypeStruct(shp.shape, BF16), shp, shp],
        compiler_params=_params(1),
        name="na_context",
    )(u_na, u_na, u_na, g2(q_g), g2(k_g))


NA_ROW_ILP = 8


def _na_lat_kernel(q_ref, k_ref, v_ref, kc_ref, vc_ref, qg_ref, kg_ref, tab_ref, o_ref,
                   q0_s, q1_s, kn_s, v_s, kc_s, vc_s, *, rows, kr):
    ones = _head_ones()
    lo_half = lax.broadcasted_iota(jnp.int32, (1, LANES), 1) < HEAD_DIM
    lo = jnp.where(lo_half, 1.0, 0.0)
    qn = _qk_norm(q_ref[0], qg_ref[...], ones) * QK_SCALE
    q0_s[...] = (qn * lo).astype(BF16)
    q1_s[...] = (qn * (1.0 - lo)).astype(BF16)
    kn_s[...] = _qk_norm(k_ref[0], kg_ref[...], ones).astype(BF16)
    v_s[...] = v_ref[0].astype(BF16)
    kc_s[...] = kc_ref[0].astype(BF16)
    vc_s[...] = vc_ref[0].astype(BF16)
    win = kr * GRID_W

    def body(it, carry):
        qs, k_rows, q_rows, biases = [], [], [], []
        for s in range(NA_ROW_ILP):
            i = it * NA_ROW_ILP + s
            r0 = jnp.clip(i - kr // 2, 0, rows - kr)
            d0 = r0 - i + (NA_ROWS - 1)
            qr = pl.ds(pl.multiple_of(i * GRID_W, GRID_W), GRID_W)
            q_rows.append(qr)
            k_rows.append(pl.ds(pl.multiple_of(r0 * GRID_W, GRID_W), win))
            qs.append(jnp.concatenate([q0_s[qr, :], q1_s[qr, :]], axis=0))
            biases.append(jnp.concatenate(
                [jnp.concatenate([tab_ref[h, d0 + 2 * m] for m in range(kr // 2)], axis=1) for h in range(2)],
                axis=0))
        lw = [_nt(q, kn_s[kr_, :]) + b for q, kr_, b in zip(qs, k_rows, biases)]
        lc = [_nt(q, kc_s[...]) for q in qs]
        ms = [jnp.maximum(jnp.max(a, axis=-1, keepdims=True), jnp.max(c, axis=-1, keepdims=True))
              for a, c in zip(lw, lc)]
        pw = [jnp.exp(a - m) for a, m in zip(lw, ms)]
        pc = [jnp.exp(c - m) for c, m in zip(lc, ms)]
        ls = [jnp.sum(a, axis=-1, keepdims=True) + jnp.sum(c, axis=-1, keepdims=True) for a, c in zip(pw, pc)]
        outs = [(jnp.dot(a.astype(BF16), v_s[kr_, :], preferred_element_type=F32)
                 + jnp.dot(c.astype(BF16), vc_s[...], preferred_element_type=F32)) / l
                for a, c, kr_, l in zip(pw, pc, k_rows, ls)]
        for qr, o in zip(q_rows, outs):
            o_ref[0, qr, :] = jnp.where(lo_half, o[:GRID_W], o[GRID_W:]).astype(o_ref.dtype)
        return carry

    lax.fori_loop(0, rows // NA_ROW_ILP, body, 0)


def _latent_bias_table(rpb):
    qc = np.arange(GRID_W)[:, None]
    kc = np.arange(GRID_W)[None, :]
    ws = np.clip(qc - NA_COLS // 2, 0, GRID_W - NA_COLS)
    valid = (kc >= ws) & (kc < ws + NA_COLS)
    dc = np.clip(kc - qc, -(NA_COLS - 1), NA_COLS - 1) + NA_COLS - 1
    onehot = (dc[None] == np.arange(2 * NA_COLS - 1)[:, None, None]).astype(np.float32)
    cb = jnp.einsum("hdc,cqk->hdqk", rpb, jnp.asarray(onehot), precision=lax.Precision.HIGHEST)
    cb = jnp.where(valid[None, None], cb, NEG_INF)
    return jnp.concatenate([cb[:, :-1], cb[:, 1:]], axis=-1)


def _na_latent(u_na, bsz, b_off, k_ctx, v_ctx, q_g, k_g, rpb):
    t_len = u_na.shape[1]
    rows = t_len // GRID_W
    kr = min(NA_ROWS, rows)
    assert kr % 2 == 0 and rows % NA_ROW_ILP == 0
    ctx_len = k_ctx.shape[1]
    seg = NA_WIDTH // LANES
    tok = lambda off: pl.BlockSpec((1, t_len, LANES), lambda b, j: (b_off + b, 0, off + j))
    ctx = pl.BlockSpec((1, ctx_len, LANES), lambda b, j: (b, 0, j))
    g2 = lambda g: jnp.tile(g.reshape(1, HEAD_DIM), (1, 2))
    tab = _latent_bias_table(rpb)
    tok_s = pltpu.VMEM((t_len, LANES), BF16)
    ctx_s = pltpu.VMEM((ctx_len, LANES), BF16)
    return pl.pallas_call(
        functools.partial(_na_lat_kernel, rows=rows, kr=kr),
        grid=(bsz, seg),
        in_specs=[tok(0), tok(seg), tok(2 * seg), ctx, ctx,
                  _const_spec((1, LANES)), _const_spec((1, LANES)),
                  pl.BlockSpec((2, 2 * NA_ROWS - 2, GRID_W, 2 * GRID_W), lambda b, j: (j, 0, 0, 0))],
        out_specs=pl.BlockSpec((1, t_len, LANES), lambda b, j: (b, 0, j)),
        out_shape=jax.ShapeDtypeStruct((bsz, t_len, NA_WIDTH), BF16),
        scratch_shapes=[tok_s, tok_s, tok_s, tok_s, ctx_s, ctx_s],
        compiler_params=_params(2),
        name="na_latent",
    )(u_na, u_na, u_na, k_ctx, v_ctx, g2(q_g), g2(k_g), tab)


LATE_WEIGHTS = ("w_o_rwkv", "w_o_na", "w_out", "ffn_w1", "ffn_w3", "ffn_w2")


def _out_ffn_kernel(xa_ref, xb_ref, orwa_ref, orwb_ref, onaa_ref, onab_ref, gt_ref, mod_ref, g_ref,
                    wor_ref, won_ref, wout_ref, w1_ref, w3_ref, w2_ref, ya_ref, yb_ref, *, tiles_a):
    i = pl.program_id(0)
    x = _pick_group(i, tiles_a, xa_ref, xb_ref)
    o_rw = _pick_group(i, tiles_a, orwa_ref, orwb_ref)
    o_na = _pick_group(i, tiles_a, onaa_ref, onab_ref)
    g_rw = _sigmoid(gt_ref[:, :D_MODEL].astype(F32))
    g_na = _sigmoid(gt_ref[:, D_MODEL:].astype(F32))
    merged = g_rw * _dot(o_rw, wor_ref[...]) + g_na * _dot(o_na, won_ref[...])
    x1 = x + _mod_part(mod_ref, 2) * _dot(merged, wout_ref[...])
    h2 = _rms_rows(x1) * g_ref[...]
    h2 = (h2 * (1.0 + _mod_part(mod_ref, 4)) + _mod_part(mod_ref, 3)).astype(BF16)
    acc = jnp.zeros(x1.shape, F32)
    for c in range(FF_HIDDEN // FF_CHUNK):
        cols = slice(c * FF_CHUNK, (c + 1) * FF_CHUNK)
        a = jnp.dot(h2, w1_ref[:, cols], preferred_element_type=F32)
        b = jnp.dot(h2, w3_ref[:, cols], preferred_element_type=F32)
        hh = (a * _sigmoid(a) * b).astype(BF16)
        acc = acc + jnp.dot(hh, w2_ref[cols, :], preferred_element_type=F32)
    y = x1 + _mod_part(mod_ref, 5) * acc

    @pl.when(i < tiles_a)
    def _():
        ya_ref[0] = y

    @pl.when(i >= tiles_a)
    def _():
        yb_ref[0] = y


def _out_ffn(xa, xb, orw_a, orw_b, ona_a, ona_b, gates, tiles_per_b, mod_all, norm_g, wb):
    tm = TOKEN_TILE
    tiles_a, tiles = xa.shape[0], xa.shape[0] + xb.shape[0]
    weights = [wb[n] for n in LATE_WEIGHTS]
    groups = lambda width: _group_specs(tiles_a, tm, width)
    return pl.pallas_call(
        functools.partial(_out_ffn_kernel, tiles_a=tiles_a),
        grid=(tiles,),
        in_specs=groups(D_MODEL) + groups(RW_WIDTH) + groups(NA_WIDTH)
        + [pl.BlockSpec((tm, GATE_COLS), lambda i: (i, 0)), _mod_spec(tiles_a, tiles_per_b),
           _const_spec((1, D_MODEL))] + [_const_spec(w.shape) for w in weights],
        out_specs=groups(D_MODEL),
        out_shape=[jax.ShapeDtypeStruct(xa.shape, F32), jax.ShapeDtypeStruct(xb.shape, F32)],
        compiler_params=_params(1),
        name="out_ffn",
    )(xa, xb, orw_a, orw_b, ona_a, ona_b, gates, mod_all, norm_g.reshape(1, -1), *weights)


def kernel(x_prompt, x_sample, state_rwkv, cache_na_k, cache_na_v, c, c_ctx, norm1_g, norm2_g, w_ada, b_ada,
           w_in, shift_mu, rw_w0, rw_w_up, rw_a0, rw_a_up, rw_g_up, rw_k_k, rw_k_a, rw_r_k, rw_ln_g, rw_ln_b,
           na_q_g, na_k_g, na_rpb, w_o_rwkv, w_o_na, w_out, ffn_w1, ffn_w3, ffn_w2):
    depth = w_in.shape[0]
    bsz, seq = x_prompt.shape[:2]
    dec, dec_seq = x_sample.shape[:2]
    tm = TOKEN_TILE
    n_ctx = bsz * seq
    assert n_ctx % tm == 0 and dec_seq % tm == 0 and n_ctx % dec_seq == 0
    tiles_per_b = dec_seq // tm
    tiled = lambda t: t.reshape(-1, tm, t.shape[-1])
    cvecs = jnp.concatenate([c_ctx[None, :], c], axis=0).T
    y_p, y_s = x_prompt, x_sample
    new_s, new_k, new_v = [], [], []
    for l in range(depth):
        p = dict(norm1_g=norm1_g[l], norm2_g=norm2_g[l], shift_mu=shift_mu[l], rw_w0=rw_w0[l],
                 rw_w_up=rw_w_up[l], rw_a0=rw_a0[l], rw_a_up=rw_a_up[l], rw_g_up=rw_g_up[l],
                 rw_k_k=rw_k_k[l], rw_k_a=rw_k_a[l], rw_r_k=rw_r_k[l], rw_ln_g=rw_ln_g[l],
                 rw_ln_b=rw_ln_b[l], na_q_g=na_q_g[l], na_k_g=na_k_g[l], na_rpb=na_rpb[l])
        late_f32 = [w[l] for w in (w_o_rwkv, w_o_na, w_out, ffn_w1, ffn_w3, ffn_w2)]
        mod_all, w_in_bf = _modulation(cvecs, w_ada[l], b_ada[l], w_in[l])
        u_rw, u_na, gates = _in_proj(tiled(y_p), tiled(y_s), tiles_per_b, mod_all, p["norm1_g"], w_in_bf)
        ctx_view = lambda t: t.reshape(-1, seq, t.shape[-1])
        lat_view = lambda t: t.reshape(-1, dec_seq, t.shape[-1])
        lat_off = n_ctx // dec_seq
        o_rw_p, s_l, casted = _rwkv_branch(ctx_view(u_rw), bsz, 0, None, p, RWKV_PAIRS_CTX, RWKV_UNITS, late_f32)
        late = dict(zip(LATE_WEIGHTS, casted))
        o_na_p, k_l, v_l = _na_context(ctx_view(u_na), bsz, p["na_q_g"], p["na_k_g"])
        new_s.append(s_l)
        new_k.append(k_l.reshape(bsz, seq, NA_HEADS, HEAD_DIM))
        new_v.append(v_l.reshape(bsz, seq, NA_HEADS, HEAD_DIM))
        ctx_k = cache_na_k[:, l].reshape(dec, -1, NA_WIDTH)
        ctx_v = cache_na_v[:, l].reshape(dec, -1, NA_WIDTH)
        o_rw_s, _, _ = _rwkv_branch(lat_view(u_rw), dec, lat_off, _state_to_big(state_rwkv[:, l]), p, RWKV_PAIRS_LAT,
                                   RWKV_UNITS)
        o_na_s = _na_latent(lat_view(u_na), dec, lat_off, ctx_k, ctx_v, p["na_q_g"], p["na_k_g"], p["na_rpb"])
        y_p_t, y_s_t = _out_ffn(tiled(y_p), tiled(y_s), tiled(o_rw_p), tiled(o_rw_s), tiled(o_na_p), tiled(o_na_s),
                                gates, tiles_per_b, mod_all, p["norm2_g"], late)
        y_p, y_s = y_p_t.reshape(x_prompt.shape), y_s_t.reshape(x_sample.shape)
    return (y_p, y_s, jnp.stack(new_s, axis=1), jnp.stack(new_k, axis=1), jnp.stack(new_v, axis=1))
```

```python
import functools

import numpy as np
import jax
import jax.numpy as jnp
from jax import lax
from jax.experimental import pallas as pl
from jax.experimental.pallas import tpu as pltpu

D_MODEL = 1024
GRID_W = 64
HEAD_DIM = 64
RW_HEADS = 8
RW_WIDTH = RW_HEADS * HEAD_DIM
NA_HEADS = 8
NA_WIDTH = NA_HEADS * HEAD_DIM
LORA_DECAY = 64
LORA_ICLR = 64
LORA_GATE = 128
NA_ROWS = 8
NA_COLS = 16
FF_HIDDEN = 2816
RW_COLS = 3 * RW_WIDTH + 2 * LORA_DECAY + 2 * LORA_ICLR + LORA_GATE
NA_IN_COLS = 3 * NA_WIDTH
GATE_COLS = 2 * D_MODEL
RMS_EPS = 1e-6
GN_EPS = 64e-5
L2_EPS = 1e-12
NEG_INF = -1e30
DECAY_SCALE = float(np.exp(-0.5))
QK_SCALE = HEAD_DIM ** -0.5
assert QK_SCALE == 0.125

LANES = 128
PAIRS = RW_HEADS // 2
CHUNK = 64
STACK = 2 * CHUNK
RWKV_PAIRS_CTX = 4
RWKV_PAIRS_LAT = 2
RWKV_UNITS = 16
TOKEN_TILE = 512
FF_CHUNK = 256
VMEM_LIMIT = 56 * 1024 * 1024

F32 = jnp.float32
BF16 = jnp.bfloat16


def _dot(a, b):
    return jnp.dot(a.astype(BF16), b.astype(BF16), preferred_element_type=F32)


def _split2(x):
    hi = x.astype(BF16)
    lo = (x - hi.astype(F32)).astype(BF16)
    return hi, lo


def _dot_exact_lhs(a_exact, b):
    h, l = _split2(b)
    d = lambda x: jnp.dot(a_exact, x, preferred_element_type=F32)
    return d(h) + d(l)


def _head_ones():
    r = lax.broadcasted_iota(jnp.int32, (LANES, LANES), 0) // HEAD_DIM
    c = lax.broadcasted_iota(jnp.int32, (LANES, LANES), 1) // HEAD_DIM
    return jnp.where(r == c, 1.0, 0.0).astype(BF16)


def _head_sum(x, ones):
    return jnp.dot(x.astype(BF16), ones, preferred_element_type=F32)


def _sigmoid(x):
    return 0.5 * jnp.tanh(0.5 * x) + 0.5


def _rms_rows(x):
    return x * lax.rsqrt(jnp.mean(x * x, axis=-1, keepdims=True) + RMS_EPS)


def _const_spec(shape):
    nd = len(shape)
    return pl.BlockSpec(shape, lambda *_: (0,) * nd, pipeline_mode=pl.Buffered(1))


def _params(n_axes):
    return pltpu.CompilerParams(dimension_semantics=("arbitrary",) * n_axes,
                                vmem_limit_bytes=VMEM_LIMIT)


def _mod_kernel(c_ref, w_ref, b_ref, win_ref, o_ref, winb_ref):
    s = c_ref[...]
    s = s * _sigmoid(s)
    w = w_ref[...]
    for r in range(s.shape[1]):
        o_ref[r] = jnp.sum(w * s[:, r:r + 1], axis=0, keepdims=True) + b_ref[...]
    winb_ref[...] = win_ref[...].astype(BF16)


def _modulation(cvecs, w_ada, b_ada, w_in):
    n = cvecs.shape[1]
    tn = 768
    steps = 6 * D_MODEL // tn
    rows = w_in.shape[0] // steps
    assert rows % 16 == 0
    return pl.pallas_call(
        _mod_kernel,
        grid=(steps,),
        in_specs=[pl.BlockSpec((D_MODEL, n), lambda j: (0, 0)),
                  pl.BlockSpec((D_MODEL, tn), lambda j: (0, j)),
                  pl.BlockSpec((1, tn), lambda j: (0, j)),
                  pl.BlockSpec((rows, w_in.shape[1]), lambda j: (j, 0))],
        out_specs=[pl.BlockSpec((n, 1, tn), lambda j: (0, 0, j)),
                   pl.BlockSpec((rows, w_in.shape[1]), lambda j: (j, 0))],
        out_shape=[jax.ShapeDtypeStruct((n, 1, 6 * D_MODEL), F32),
                   jax.ShapeDtypeStruct(w_in.shape, BF16)],
        compiler_params=_params(1),
        name="modulation",
    )(cvecs, w_ada, b_ada.reshape(1, -1), w_in)


def _pick_group(i, tiles_a, a_ref, b_ref):
    return jnp.where(i < tiles_a, a_ref[0], b_ref[0])


def _group_specs(tiles_a, tm, width):
    return [pl.BlockSpec((1, tm, width), lambda i: (jnp.minimum(i, tiles_a - 1), 0, 0)),
            pl.BlockSpec((1, tm, width), lambda i: (jnp.maximum(i - tiles_a, 0), 0, 0))]


def _mod_spec(tiles_a, tiles_per_b):
    return pl.BlockSpec((1, 1, 6 * D_MODEL),
                        lambda i: (jnp.where(i < tiles_a, 0, 1 + (i - tiles_a) // tiles_per_b), 0, 0))


def _mod_part(mod_ref, k):
    return mod_ref[0, :, k * D_MODEL:(k + 1) * D_MODEL]


def _inproj_kernel(xa_ref, xb_ref, mod_ref, g_ref, w_ref, urw_ref, una_ref, gt_ref, *, tiles_a):
    x = _pick_group(pl.program_id(0), tiles_a, xa_ref, xb_ref)
    h = _rms_rows(x) * g_ref[...]
    h = (h * (1.0 + _mod_part(mod_ref, 1)) + _mod_part(mod_ref, 0)).astype(BF16)
    d = lambda lo, hi: jnp.dot(h, w_ref[:, lo:hi], preferred_element_type=F32)
    urw_ref[...] = d(0, RW_COLS)
    una_ref[...] = d(RW_COLS, RW_COLS + NA_IN_COLS)
    gt_ref[...] = d(RW_COLS + NA_IN_COLS, RW_COLS + NA_IN_COLS + GATE_COLS).astype(BF16)


def _in_proj(xa, xb, tiles_per_b, mod_all, norm_g, w_in_bf):
    tm = TOKEN_TILE
    tiles_a, tiles = xa.shape[0], xa.shape[0] + xb.shape[0]
    row = lambda i: (i, 0)
    return pl.pallas_call(
        functools.partial(_inproj_kernel, tiles_a=tiles_a),
        grid=(tiles,),
        in_specs=_group_specs(tiles_a, tm, D_MODEL) + [_mod_spec(tiles_a, tiles_per_b),
                                                       _const_spec((1, D_MODEL)), _const_spec(w_in_bf.shape)],
        out_specs=[pl.BlockSpec((tm, RW_COLS), row),
                   pl.BlockSpec((tm, NA_IN_COLS), row),
                   pl.BlockSpec((tm, GATE_COLS), row)],
        out_shape=[jax.ShapeDtypeStruct((tiles * tm, RW_COLS), F32),
                   jax.ShapeDtypeStruct((tiles * tm, NA_IN_COLS), F32),
                   jax.ShapeDtypeStruct((tiles * tm, GATE_COLS), BF16)],
        compiler_params=_params(1),
        name="in_proj",
    )(xa, xb, mod_all, norm_g.reshape(1, -1), w_in_bf)


def _shift(x, mu):
    t_len = x.shape[0]
    row = lax.broadcasted_iota(jnp.int32, x.shape, 0)
    prev = jnp.where(row == 0, 0.0, pltpu.roll(x, 1, 0))
    nxt = jnp.where(row == t_len - 1, 0.0, pltpu.roll(x, t_len - 1, 0))
    return x + mu[0:1, :] * (prev - x) + mu[1:2, :] * (nxt - x)


def _stack_heads(x, lane_lo):
    return jnp.concatenate([x * lane_lo, x * (1.0 - lane_lo)], axis=0)


def _wkv_intra(units, consts):
    tri, mask_s, mask_i, eye, lane_lo, blk = consts
    stack = lambda z: _stack_heads(z, lane_lo)

    cums = [_dot_exact_lhs(tri[int(u[6])], u[1]) for u in units]

    prep = []
    for (r, lw, kd, v, kk, b, reverse), cum in zip(units, cums):
        mid_row = CHUNK // 2 if reverse else CHUNK // 2 - 1
        tot_row = 0 if reverse else CHUNK - 1
        a = -kk
        ex = cum - lw
        mid = cum[mid_row:mid_row + 1, :]
        tot = cum[tot_row:tot_row + 1, :]
        up = jnp.exp(cum - mid)
        dn = jnp.exp(mid - cum)
        tail = jnp.exp(tot - cum)
        prep.append(dict(
            at_m=stack(a * jnp.exp(ex - mid)).astype(BF16),
            rt_m=stack(r * up).astype(BF16),
            bk_m=jnp.concatenate([b * dn, kd * dn], axis=0).astype(BF16),
            a_e=stack(a * jnp.exp(ex)),
            r_e=stack(r * jnp.exp(cum)),
            bk_t=jnp.concatenate([stack(b * tail), stack(kd * tail)], axis=0).T.astype(BF16),
            vv=stack(v).astype(BF16),
            diag=jnp.where(eye, jnp.exp(tot), 0.0),
            rev=int(reverse)))

    ntd = lambda x, y: lax.dot_general(x, y, (((1,), (1,)), ((), ())), preferred_element_type=F32)
    mm = lambda x, y: jnp.dot(x, y, preferred_element_type=F32)
    diag_blk, swap_eye = blk
    roll_head = lambda z: pltpu.roll(z, HEAD_DIM, 1)
    by_block = lambda z: jnp.concatenate([z[:CHUNK], roll_head(z[CHUNK:])], axis=0)
    top = [by_block(ntd(p["at_m"], p["bk_m"])) for p in prep]
    low = [by_block(ntd(p["rt_m"], p["bk_m"])) for p in prep]
    a_ak = [(roll_head(t) * mask_s[p["rev"]]).astype(BF16) for t, p in zip(top, prep)]
    bot = [jnp.concatenate([t * mask_i[p["rev"]], roll_head(t) * mask_i[p["rev"]]], axis=1).astype(BF16)
           for t, p in zip(low, prep)]

    off_blk = 1.0 - diag_blk
    both = [t * mask_s[p["rev"]] + swap_eye for t, p in zip(top, prep)]
    steps = CHUNK.bit_length() - 1
    diag_bf = diag_blk.astype(BF16)
    for j in range(steps):
        packed = [q.astype(BF16) for q in both]
        res = [mm(qb * diag_bf, qb) for qb in packed]
        both = [r + off_blk * q for r, q in zip(res, both)]
    ts = [pltpu.roll(q, HEAD_DIM, 1).astype(BF16) for q in both]
    x0 = [jnp.concatenate([p["a_e"], mm(ak, p["vv"])], axis=1) for p, ak in zip(prep, a_ak)]
    xs = [mm(t, x.astype(BF16)) for x, t in zip(x0, ts)]

    out = []
    zeros = jnp.zeros((STACK, LANES), BF16)
    for p, x, bt in zip(prep, xs, bot):
        rhs = jnp.concatenate([x.astype(BF16), jnp.concatenate([zeros, p["vv"]], axis=1)], axis=0)
        lhs = jnp.concatenate([bt, p["bk_t"]], axis=0)
        res = mm(lhs, rhs)
        lhs2 = res[:, :LANES] + jnp.concatenate([p["r_e"], p["diag"]], axis=0)
        out.append((lhs2.astype(BF16), res[:, LANES:]))
    return out


def _wkv_constants():
    lane_head = np.arange(LANES) // HEAD_DIM
    ones = (lane_head[:, None] == lane_head[None, :]).astype(np.float32)
    t = np.arange(CHUNK)
    tri = np.stack([t[None, :] <= t[:, None], t[None, :] >= t[:, None]]).astype(np.float32)
    rs, cs = t[:, None], t[None, :]
    tri_masks = np.stack([cs < rs, cs > rs, cs <= rs, cs >= rs]).astype(np.float32)
    masks = np.kron(np.eye(2, dtype=np.float32), tri_masks)
    return jnp.asarray(ones, BF16), jnp.asarray(tri, BF16), jnp.asarray(masks, F32)


def _rwkv_kernel(*refs, t_len, has_s0, pairs, units, n_cast):
    (r_ref, k_ref, v_ref, lo_ref, mur_ref, muk_ref, muv_ref, mul_ref, w0_ref, a0_ref, wup_ref, aup_ref,
     gup_ref, kk_ref, ka_ref, rk_ref, lng_ref, lnb_ref, ones_ref, tri_ref, mask_ref) = refs[:21]
    pos = 21
    s0_ref = None
    if has_s0:
        s0_ref = refs[pos]
        pos += 1
    cast_in = refs[pos:pos + n_cast]
    pos += n_cast
    o_ref, sn_ref = refs[pos], refs[pos + 1]
    cast_out = refs[pos + 2:pos + 2 + n_cast]
    (r_s, v_s, kk_s, b0_s, b1_s, lw0_s, lw1_s, kd0_s, kd1_s, gate_s, bonus_s, yf_s, yb_s,
     lhs_s, add_s, st_s) = refs[pos + 2 + n_cast:]
    for w_in_ref, w_out_ref in zip(cast_in, cast_out):
        w_out_ref[...] = w_in_ref[...].astype(BF16)

    ones = ones_ref[...]
    lane = lax.broadcasted_iota(jnp.int32, (1, LANES), 1)
    lo_half = lane < HEAD_DIM
    lane_lo = jnp.where(lo_half, 1.0, 0.0)
    mm = lambda x, y: jnp.dot(x, y, preferred_element_type=F32)

    lo = _shift(lo_ref[0], mul_ref[...])
    wd = jnp.tanh(lo[:, 0:LANES])
    ad = lo[:, LANES:2 * LANES]
    sig_gd = _sigmoid(lo[:, 2 * LANES:3 * LANES]).astype(BF16)
    wd_split = [_split2(wd * m) for m in (lane_lo, 1.0 - lane_lo)]
    ad_bf = [(ad * m).astype(BF16) for m in (lane_lo, 1.0 - lane_lo)]
    for j in range(pairs):
        cols = slice(j * LANES, (j + 1) * LANES)
        r = _shift(r_ref[0, :, cols], mur_ref[:, cols])
        k = _shift(k_ref[0, :, cols], muk_ref[:, cols])
        v = _shift(v_ref[0, :, cols], muv_ref[:, cols])
        kk = k * kk_ref[:, cols]
        kk = kk * lax.rsqrt(_head_sum(kk * kk, ones) + L2_EPS)
        wup_h, wup_l = _split2(wup_ref[:, cols])
        aup = aup_ref[:, cols].astype(BF16)
        kdirs = []
        for e, (lw_s, kd_s, b_s) in enumerate(((lw0_s, kd0_s, b0_s), (lw1_s, kd1_s, b1_s))):
            wd_h, wd_l = wd_split[e]
            w_lin = w0_ref[e:e + 1, cols] + (mm(wd_h, wup_h) + mm(wd_l, wup_h) + mm(wd_h, wup_l))
            lw_s[j] = -DECAY_SCALE * _sigmoid(w_lin)
            iclr = _sigmoid(a0_ref[e:e + 1, cols] + mm(ad_bf[e], aup))
            kd = k * (1.0 + (iclr - 1.0) * ka_ref[:, cols])
            kd_s[j] = kd
            b_s[j] = kk * iclr
            kdirs.append(kd)
        gate_s[:, cols] = mm(sig_gd, gup_ref[:, cols].astype(BF16))
        bonus_s[:, cols] = _head_sum(r * (0.5 * (kdirs[0] + kdirs[1])) * rk_ref[:, cols], ones) * v
        r_s[j] = r
        v_s[j] = v
        kk_s[j] = kk

    n_chunks = t_len // CHUNK
    chunks_per = min(n_chunks, units // 2)
    pairs_per = min(pairs, units // (2 * chunks_per))
    groups = n_chunks // chunks_per
    rs = lax.broadcasted_iota(jnp.int32, (STACK, STACK), 0)
    cs = lax.broadcasted_iota(jnp.int32, (STACK, STACK), 1)
    as_f32 = lambda m: jnp.where(m, 1.0, 0.0)
    blk = (as_f32(rs // CHUNK == cs // CHUNK), as_f32(cs == (rs + CHUNK) % STACK))
    consts = ((tri_ref[0], tri_ref[1]), (mask_ref[0], mask_ref[1]), (mask_ref[2], mask_ref[3]),
              rs == cs, lane_lo, blk)
    dirs = ((lw0_s, kd0_s, b0_s), (lw1_s, kd1_s, b1_s))

    def intra_body(it, carry):
        pg = it // groups
        g = it % groups
        units_, ids = [], []
        for jj in range(pairs_per):
            j = pg * pairs_per + jj
            for cc in range(chunks_per):
                c = g * chunks_per + cc
                rows = pl.ds(pl.multiple_of(c * CHUNK, CHUNK), CHUNK)
                for e, (lw_s, kd_s, b_s) in enumerate(dirs):
                    units_.append((r_s[j, rows, :], lw_s[j, rows, :], kd_s[j, rows, :], v_s[j, rows, :],
                                   kk_s[j, rows, :], b_s[j, rows, :], e == 1))
                    ids.append((j * 2 + e) * n_chunks + c)
        for uid, (lhs, add) in zip(ids, _wkv_intra(units_, consts)):
            lhs_s[uid] = lhs
            add_s[uid] = add
        return carry

    lax.fori_loop(0, (pairs // pairs_per) * groups, intra_body, 0, unroll=2)

    for j in range(pairs):
        for e in range(2):
            if has_s0:
                st_s[2 * j + e] = s0_ref[0, e, j].T
            else:
                st_s[2 * j + e] = jnp.zeros((LANES, LANES), F32)

    def state_body(it, carry):
        chunk = (it, n_chunks - 1 - it)
        uids = [(j * 2 + e) * n_chunks + chunk[e] for j in range(pairs) for e in range(2)]
        sts = [st_s[ch].astype(BF16) for ch in range(2 * pairs)]
        res = [mm(lhs_s[uid], st) + add_s[uid] for uid, st in zip(uids, sts)]
        for ch, rr in enumerate(res):
            j, e = divmod(ch, 2)
            y_s = yb_s if e else yf_s
            y_s[j, pl.ds(pl.multiple_of(chunk[e] * CHUNK, CHUNK), CHUNK), :] = rr[:CHUNK] + rr[CHUNK:STACK]
            st_s[ch] = rr[STACK:]
        return carry

    lax.fori_loop(0, n_chunks, state_body, 0, unroll=True)
    for j in range(pairs):
        for e in range(2):
            st_t = st_s[2 * j + e].T
            sn_ref[0, e, 2 * j] = st_t[:HEAD_DIM, :HEAD_DIM]
            sn_ref[0, e, 2 * j + 1] = pltpu.roll(st_t, HEAD_DIM, 1)[HEAD_DIM:, :HEAD_DIM]

    inv_d = 1.0 / HEAD_DIM
    for j in range(pairs):
        cols = slice(j * LANES, (j + 1) * LANES)
        y = yf_s[j] + yb_s[j]
        mean = _head_sum(y, ones) * inv_d
        dlt = y - mean
        var = _head_sum(dlt * dlt, ones) * inv_d
        yn = dlt * lax.rsqrt(var + GN_EPS) * lng_ref[:, cols] + lnb_ref[:, cols]
        o_ref[0, :, cols] = ((yn + bonus_s[:, cols]) * gate_s[:, cols]).astype(o_ref.dtype)


def _rwkv_branch(u_rw, bsz, b_off, s0_big, p, pairs, units, cast=()):
    t_len = u_rw.shape[1]
    has_s0 = s0_big is not None
    width = pairs * LANES
    seg = RW_WIDTH // width
    tok = lambda off: pl.BlockSpec((1, t_len, width), lambda b, j: (b_off + b, 0, off + j))
    mu = lambda off: pl.BlockSpec((2, width), lambda b, j: (0, off + j))
    vec2 = pl.BlockSpec((2, width), lambda b, j: (0, j))
    vec1 = pl.BlockSpec((1, width), lambda b, j: (0, j))
    mat = pl.BlockSpec((LANES, width), lambda b, j: (0, j))
    lora_w = 3 * LANES
    lora_blk = 3 * RW_WIDTH // lora_w
    in_specs = [tok(0), tok(seg), tok(2 * seg),
                pl.BlockSpec((1, t_len, lora_w), lambda b, j: (b_off + b, 0, lora_blk)),
                mu(0), mu(seg), mu(2 * seg),
                pl.BlockSpec((2, lora_w), lambda b, j: (0, lora_blk)),
                vec2, vec2, mat, mat, mat, vec1, vec1, vec1, vec1, vec1,
                _const_spec((LANES, LANES)), _const_spec((2, CHUNK, CHUNK)), _const_spec((4, STACK, STACK))]
    args = [u_rw, u_rw, u_rw, u_rw, p["shift_mu"], p["shift_mu"], p["shift_mu"], p["shift_mu"],
            p["rw_w0"], p["rw_a0"],
            p["rw_w_up"].reshape(2 * LORA_DECAY, RW_WIDTH), p["rw_a_up"].reshape(2 * LORA_ICLR, RW_WIDTH),
            p["rw_g_up"], p["rw_k_k"].reshape(1, -1), p["rw_k_a"].reshape(1, -1),
            p["rw_r_k"].reshape(1, -1), p["rw_ln_g"].reshape(1, -1), p["rw_ln_b"].reshape(1, -1),
            *_wkv_constants()]
    st_spec = pl.BlockSpec((1, 2, pairs, LANES, LANES), lambda b, j: (b, 0, j, 0, 0))
    if has_s0:
        in_specs.append(st_spec)
        args.append(s0_big)
    out_specs = [pl.BlockSpec((1, t_len, width), lambda b, j: (b, 0, j)),
                 pl.BlockSpec((1, 2, 2 * pairs, HEAD_DIM, HEAD_DIM), lambda b, j: (b, 0, j, 0, 0))]
    out_shape = [jax.ShapeDtypeStruct((bsz, t_len, RW_WIDTH), BF16),
                 jax.ShapeDtypeStruct((bsz, 2, RW_HEADS, HEAD_DIM, HEAD_DIM), F32)]
    for w in cast:
        assert PAIRS == pairs and w.shape[0] % (16 * bsz) == 0
        blk = pl.BlockSpec((w.shape[0] // bsz, w.shape[1]), lambda b, j: (b, 0))
        in_specs.append(blk)
        args.append(w)
        out_specs.append(blk)
        out_shape.append(jax.ShapeDtypeStruct(w.shape, BF16))
    n_units = 2 * pairs * (t_len // CHUNK)
    per_pair = pltpu.VMEM((pairs, t_len, LANES), F32)
    full = pltpu.VMEM((t_len, width), F32)
    scratch = [per_pair] * 9 + [full, full, per_pair, per_pair,
                                pltpu.VMEM((n_units, 2 * STACK, LANES), BF16),
                                pltpu.VMEM((n_units, 2 * STACK, LANES), F32),
                                pltpu.VMEM((2 * pairs, LANES, LANES), F32)]
    o_rw, s_new, *casted = pl.pallas_call(
        functools.partial(_rwkv_kernel, t_len=t_len, has_s0=has_s0, pairs=pairs, units=units, n_cast=len(cast)),
        grid=(bsz, PAIRS // pairs),
        in_specs=in_specs,
        out_specs=out_specs,
        out_shape=out_shape,
        scratch_shapes=scratch,
        compiler_params=_params(2),
        name="rwkv_branch",
    )(*args)
    return o_rw, s_new, casted


def _state_to_big(s0):
    bsz = s0.shape[0]
    x = s0.reshape(bsz, 2, PAIRS, 2, HEAD_DIM, 1, HEAD_DIM)
    same_head = np.eye(2, dtype=bool).reshape(2, 1, 2, 1)
    return jnp.where(same_head, x, 0.0).reshape(bsz, 2, PAIRS, LANES, LANES)


def _qk_norm(t, g, ones):
    ms = _head_sum(t * t, ones) * (1.0 / HEAD_DIM)
    return t * lax.rsqrt(ms + RMS_EPS) * g


def _nt(x, y):
    return lax.dot_general(x, y, (((1,), (1,)), ((), ())), preferred_element_type=F32)


def _na_ctx_kernel(q_ref, k_ref, v_ref, qg_ref, kg_ref, o_ref, kn_ref, vc_ref):
    ones = _head_ones()
    lo_half = lax.broadcasted_iota(jnp.int32, (1, LANES), 1) < HEAD_DIM
    lo = jnp.where(lo_half, 1.0, 0.0)
    t_len = q_ref.shape[1]
    qs, ks, vs = [], [], []
    for j in range(NA_WIDTH // LANES):
        cols = slice(j * LANES, (j + 1) * LANES)
        qn = _qk_norm(q_ref[0, :, cols], qg_ref[...], ones)
        kn = _qk_norm(k_ref[0, :, cols], kg_ref[...], ones)
        v = v_ref[0, :, cols]
        kn_ref[0, :, cols] = kn
        vc_ref[0, :, cols] = v
        qn = qn * QK_SCALE
        qs.append(jnp.concatenate([qn * lo, qn * (1.0 - lo)], axis=0).astype(BF16))
        ks.append(kn.astype(BF16))
        vs.append(v.astype(BF16))
    logits = [_nt(q, k) for q, k in zip(qs, ks)]
    ms = [jnp.max(s, axis=-1, keepdims=True) for s in logits]
    ps = [jnp.exp(s - m) for s, m in zip(logits, ms)]
    ls = [jnp.sum(p, axis=-1, keepdims=True) for p in ps]
    outs = [jnp.dot(p.astype(BF16), v, preferred_element_type=F32) / l for p, v, l in zip(ps, vs, ls)]
    for j, o in enumerate(outs):
        o_ref[0, :, j * LANES:(j + 1) * LANES] = jnp.where(lo_half, o[:t_len], o[t_len:]).astype(o_ref.dtype)


def _na_context(u_na, bsz, q_g, k_g):
    t_len = u_na.shape[1]
    tok = lambda seg: pl.BlockSpec((1, t_len, NA_WIDTH), lambda b: (b, 0, seg))
    out_blk = pl.BlockSpec((1, t_len, NA_WIDTH), lambda b: (b, 0, 0))
    g2 = lambda g: jnp.tile(g.reshape(1, HEAD_DIM), (1, 2))
    shp = jax.ShapeDtypeStruct((bsz, t_len, NA_WIDTH), F32)
    return pl.pallas_call(
        _na_ctx_kernel,
        grid=(bsz,),
        in_specs=[tok(0), tok(1), tok(2), _const_spec((1, LANES)), _const_spec((1, LANES))],
        out_specs=[out_blk, out_blk, out_blk],
        out_shape=[jax.ShapeDtypeStruct(shp.shape, BF16), shp, shp],
        compiler_params=_params(1),
        name="na_context",
    )(u_na, u_na, u_na, g2(q_g), g2(k_g))


NA_ROW_ILP = 4


def _na_lat_kernel(q_ref, k_ref, v_ref, kc_ref, vc_ref, qg_ref, kg_ref, tab_ref, o_ref,
                   q0_s, q1_s, kn_s, v_s, kc_s, vc_s, *, rows, kr):
    ones = _head_ones()
    lo_half = lax.broadcasted_iota(jnp.int32, (1, LANES), 1) < HEAD_DIM
    lo = jnp.where(lo_half, 1.0, 0.0)
    qn = _qk_norm(q_ref[0], qg_ref[...], ones) * QK_SCALE
    q0_s[...] = (qn * lo).astype(BF16)
    q1_s[...] = (qn * (1.0 - lo)).astype(BF16)
    kn_s[...] = _qk_norm(k_ref[0], kg_ref[...], ones).astype(BF16)
    v_s[...] = v_ref[0].astype(BF16)
    kc_s[...] = kc_ref[0].astype(BF16)
    vc_s[...] = vc_ref[0].astype(BF16)
    win = kr * GRID_W

    def body(it, carry):
        qs, k_rows, q_rows, biases = [], [], [], []
        for s in range(NA_ROW_ILP):
            i = it * NA_ROW_ILP + s
            r0 = jnp.clip(i - kr // 2, 0, rows - kr)
            d0 = r0 - i + (NA_ROWS - 1)
            qr = pl.ds(pl.multiple_of(i * GRID_W, GRID_W), GRID_W)
            q_rows.append(qr)
            k_rows.append(pl.ds(pl.multiple_of(r0 * GRID_W, GRID_W), win))
            qs.append(jnp.concatenate([q0_s[qr, :], q1_s[qr, :]], axis=0))
            biases.append(jnp.concatenate(
                [jnp.concatenate([tab_ref[h, d0 + 2 * m] for m in range(kr // 2)], axis=1) for h in range(2)],
                axis=0))
        lw = [_nt(q, kn_s[kr_, :]) + b for q, kr_, b in zip(qs, k_rows, biases)]
        lc = [_nt(q, kc_s[...]) for q in qs]
        ms = [jnp.maximum(jnp.max(a, axis=-1, keepdims=True), jnp.max(c, axis=-1, keepdims=True))
              for a, c in zip(lw, lc)]
        pw = [jnp.exp(a - m) for a, m in zip(lw, ms)]
        pc = [jnp.exp(c - m) for c, m in zip(lc, ms)]
        ls = [jnp.sum(a, axis=-1, keepdims=True) + jnp.sum(c, axis=-1, keepdims=True) for a, c in zip(pw, pc)]
        outs = [(jnp.dot(a.astype(BF16), v_s[kr_, :], preferred_element_type=F32)
                 + jnp.dot(c.astype(BF16), vc_s[...], preferred_element_type=F32)) / l
                for a, c, kr_, l in zip(pw, pc, k_rows, ls)]
        for qr, o in zip(q_rows, outs):
            o_ref[0, qr, :] = jnp.where(lo_half, o[:GRID_W], o[GRID_W:]).astype(o_ref.dtype)
        return carry

    lax.fori_loop(0, rows // NA_ROW_ILP, body, 0, unroll=True)


def _latent_bias_table(rpb):
    qc = np.arange(GRID_W)[:, None]
    kc = np.arange(GRID_W)[None, :]
    ws = np.clip(qc - NA_COLS // 2, 0, GRID_W - NA_COLS)
    valid = (kc >= ws) & (kc < ws + NA_COLS)
    dc = np.clip(kc - qc, -(NA_COLS - 1), NA_COLS - 1) + NA_COLS - 1
    onehot = (dc[None] == np.arange(2 * NA_COLS - 1)[:, None, None]).astype(np.float32)
    cb = jnp.einsum("hdc,cqk->hdqk", rpb, jnp.asarray(onehot), precision=lax.Precision.HIGHEST)
    cb = jnp.where(valid[None, None], cb, NEG_INF)
    return jnp.concatenate([cb[:, :-1], cb[:, 1:]], axis=-1)


def _na_latent(u_na, bsz, b_off, k_ctx, v_ctx, q_g, k_g, rpb):
    t_len = u_na.shape[1]
    rows = t_len // GRID_W
    kr = min(NA_ROWS, rows)
    assert kr % 2 == 0 and rows % NA_ROW_ILP == 0
    ctx_len = k_ctx.shape[1]
    seg = NA_WIDTH // LANES
    tok = lambda off: pl.BlockSpec((1, t_len, LANES), lambda b, j: (b_off + b, 0, off + j))
    ctx = pl.BlockSpec((1, ctx_len, LANES), lambda b, j: (b, 0, j))
    g2 = lambda g: jnp.tile(g.reshape(1, HEAD_DIM), (1, 2))
    tab = _latent_bias_table(rpb)
    tok_s = pltpu.VMEM((t_len, LANES), BF16)
    ctx_s = pltpu.VMEM((ctx_len, LANES), BF16)
    return pl.pallas_call(
        functools.partial(_na_lat_kernel, rows=rows, kr=kr),
        grid=(bsz, seg),
        in_specs=[tok(0), tok(seg), tok(2 * seg), ctx, ctx,
                  _const_spec((1, LANES)), _const_spec((1, LANES)),
                  pl.BlockSpec((2, 2 * NA_ROWS - 2, GRID_W, 2 * GRID_W), lambda b, j: (j, 0, 0, 0))],
        out_specs=pl.BlockSpec((1, t_len, LANES), lambda b, j: (b, 0, j)),
        out_shape=jax.ShapeDtypeStruct((bsz, t_len, NA_WIDTH), BF16),
        scratch_shapes=[tok_s, tok_s, tok_s, tok_s, ctx_s, ctx_s],
        compiler_params=_params(2),
        name="na_latent",
    )(u_na, u_na, u_na, k_ctx, v_ctx, g2(q_g), g2(k_g), tab)


LATE_WEIGHTS = ("w_o_rwkv", "w_o_na", "w_out", "ffn_w1", "ffn_w3", "ffn_w2")


def _out_ffn_kernel(xa_ref, xb_ref, orwa_ref, orwb_ref, onaa_ref, onab_ref, gt_ref, mod_ref, g_ref,
                    wor_ref, won_ref, wout_ref, w1_ref, w3_ref, w2_ref, ya_ref, yb_ref, *, tiles_a):
    i = pl.program_id(0)
    x = _pick_group(i, tiles_a, xa_ref, xb_ref)
    o_rw = _pick_group(i, tiles_a, orwa_ref, orwb_ref)
    o_na = _pick_group(i, tiles_a, onaa_ref, onab_ref)
    g_rw = _sigmoid(gt_ref[:, :D_MODEL].astype(F32))
    g_na = _sigmoid(gt_ref[:, D_MODEL:].astype(F32))
    merged = g_rw * _dot(o_rw, wor_ref[...]) + g_na * _dot(o_na, won_ref[...])
    x1 = x + _mod_part(mod_ref, 2) * _dot(merged, wout_ref[...])
    h2 = _rms_rows(x1) * g_ref[...]
    h2 = (h2 * (1.0 + _mod_part(mod_ref, 4)) + _mod_part(mod_ref, 3)).astype(BF16)
    acc = jnp.zeros(x1.shape, F32)
    for c in range(FF_HIDDEN // FF_CHUNK):
        cols = slice(c * FF_CHUNK, (c + 1) * FF_CHUNK)
        a = jnp.dot(h2, w1_ref[:, cols], preferred_element_type=F32)
        b = jnp.dot(h2, w3_ref[:, cols], preferred_element_type=F32)
        hh = (a * _sigmoid(a) * b).astype(BF16)
        acc = acc + jnp.dot(hh, w2_ref[cols, :], preferred_element_type=F32)
    y = x1 + _mod_part(mod_ref, 5) * acc

    @pl.when(i < tiles_a)
    def _():
        ya_ref[0] = y

    @pl.when(i >= tiles_a)
    def _():
        yb_ref[0] = y


def _out_ffn(xa, xb, orw_a, orw_b, ona_a, ona_b, gates, tiles_per_b, mod_all, norm_g, wb):
    tm = TOKEN_TILE
    tiles_a, tiles = xa.shape[0], xa.shape[0] + xb.shape[0]
    weights = [wb[n] for n in LATE_WEIGHTS]
    groups = lambda width: _group_specs(tiles_a, tm, width)
    return pl.pallas_call(
        functools.partial(_out_ffn_kernel, tiles_a=tiles_a),
        grid=(tiles,),
        in_specs=groups(D_MODEL) + groups(RW_WIDTH) + groups(NA_WIDTH)
        + [pl.BlockSpec((tm, GATE_COLS), lambda i: (i, 0)), _mod_spec(tiles_a, tiles_per_b),
           _const_spec((1, D_MODEL))] + [_const_spec(w.shape) for w in weights],
        out_specs=groups(D_MODEL),
        out_shape=[jax.ShapeDtypeStruct(xa.shape, F32), jax.ShapeDtypeStruct(xb.shape, F32)],
        compiler_params=_params(1),
        name="out_ffn",
    )(xa, xb, orw_a, orw_b, ona_a, ona_b, gates, mod_all, norm_g.reshape(1, -1), *weights)


def kernel(x_prompt, x_sample, state_rwkv, cache_na_k, cache_na_v, c, c_ctx, norm1_g, norm2_g, w_ada, b_ada,
           w_in, shift_mu, rw_w0, rw_w_up, rw_a0, rw_a_up, rw_g_up, rw_k_k, rw_k_a, rw_r_k, rw_ln_g, rw_ln_b,
           na_q_g, na_k_g, na_rpb, w_o_rwkv, w_o_na, w_out, ffn_w1, ffn_w3, ffn_w2):
    depth = w_in.shape[0]
    bsz, seq = x_prompt.shape[:2]
    dec, dec_seq = x_sample.shape[:2]
    tm = TOKEN_TILE
    n_ctx = bsz * seq
    assert n_ctx % tm == 0 and dec_seq % tm == 0 and n_ctx % dec_seq == 0
    tiles_per_b = dec_seq // tm
    tiled = lambda t: t.reshape(-1, tm, t.shape[-1])
    cvecs = jnp.concatenate([c_ctx[None, :], c], axis=0).T
    y_p, y_s = x_prompt, x_sample
    new_s, new_k, new_v = [], [], []
    for l in range(depth):
        p = dict(norm1_g=norm1_g[l], norm2_g=norm2_g[l], shift_mu=shift_mu[l], rw_w0=rw_w0[l],
                 rw_w_up=rw_w_up[l], rw_a0=rw_a0[l], rw_a_up=rw_a_up[l], rw_g_up=rw_g_up[l],
                 rw_k_k=rw_k_k[l], rw_k_a=rw_k_a[l], rw_r_k=rw_r_k[l], rw_ln_g=rw_ln_g[l],
                 rw_ln_b=rw_ln_b[l], na_q_g=na_q_g[l], na_k_g=na_k_g[l], na_rpb=na_rpb[l])
        late_f32 = [w[l] for w in (w_o_rwkv, w_o_na, w_out, ffn_w1, ffn_w3, ffn_w2)]
        mod_all, w_in_bf = _modulation(cvecs, w_ada[l], b_ada[l], w_in[l])
        u_rw, u_na, gates = _in_proj(tiled(y_p), tiled(y_s), tiles_per_b, mod_all, p["norm1_g"], w_in_bf)
        ctx_view = lambda t: t.reshape(-1, seq, t.shape[-1])
        lat_view = lambda t: t.reshape(-1, dec_seq, t.shape[-1])
        lat_off = n_ctx // dec_seq
        o_rw_p, s_l, casted = _rwkv_branch(ctx_view(u_rw), bsz, 0, None, p, RWKV_PAIRS_CTX, RWKV_UNITS, late_f32)
        late = dict(zip(LATE_WEIGHTS, casted))
        o_na_p, k_l, v_l = _na_context(ctx_view(u_na), bsz, p["na_q_g"], p["na_k_g"])
        new_s.append(s_l)
        new_k.append(k_l.reshape(bsz, seq, NA_HEADS, HEAD_DIM))
        new_v.append(v_l.reshape(bsz, seq, NA_HEADS, HEAD_DIM))
        ctx_k = cache_na_k[:, l].reshape(dec, -1, NA_WIDTH)
        ctx_v = cache_na_v[:, l].reshape(dec, -1, NA_WIDTH)
        o_rw_s, _, _ = _rwkv_branch(lat_view(u_rw), dec, lat_off, _state_to_big(state_rwkv[:, l]), p, RWKV_PAIRS_LAT,
                                   RWKV_UNITS)
        o_na_s = _na_latent(lat_view(u_na), dec, lat_off, ctx_k, ctx_v, p["na_q_g"], p["na_k_g"], p["na_rpb"])
        y_p_t, y_s_t = _out_ffn(tiled(y_p), tiled(y_s), tiled(o_rw_p), tiled(o_rw_s), tiled(o_na_p), tiled(o_na_s),
                                gates, tiles_per_b, mod_all, p["norm2_g"], late)
        y_p, y_s = y_p_t.reshape(x_prompt.shape), y_s_t.reshape(x_sample.shape)
    return (y_p, y_s, jnp.stack(new_s, axis=1), jnp.stack(new_k, axis=1), jnp.stack(new_v, axis=1))
```

```python
import functools

import numpy as np
import jax
import jax.numpy as jnp
from jax import lax
from jax.experimental import pallas as pl
from jax.experimental.pallas import tpu as pltpu

D_MODEL = 1024
GRID_W = 64
HEAD_DIM = 64
RW_HEADS = 8
RW_WIDTH = RW_HEADS * HEAD_DIM
NA_HEADS = 8
NA_WIDTH = NA_HEADS * HEAD_DIM
LORA_DECAY = 64
LORA_ICLR = 64
LORA_GATE = 128
NA_ROWS = 8
NA_COLS = 16
FF_HIDDEN = 2816
RW_COLS = 3 * RW_WIDTH + 2 * LORA_DECAY + 2 * LORA_ICLR + LORA_GATE
NA_IN_COLS = 3 * NA_WIDTH
GATE_COLS = 2 * D_MODEL
RMS_EPS = 1e-6
GN_EPS = 64e-5
L2_EPS = 1e-12
NEG_INF = -1e30
DECAY_SCALE = float(np.exp(-0.5))
QK_SCALE = HEAD_DIM ** -0.5
assert QK_SCALE == 0.125

LANES = 128
PAIRS = RW_HEADS // 2
CHUNK = 64
STACK = 2 * CHUNK
RWKV_PAIRS_CTX = 4
RWKV_PAIRS_LAT = 2
RWKV_UNITS = 16
TOKEN_TILE = 512
FF_CHUNK = 256
VMEM_LIMIT = 56 * 1024 * 1024

F32 = jnp.float32
BF16 = jnp.bfloat16


def _dot(a, b):
    return jnp.dot(a.astype(BF16), b.astype(BF16), preferred_element_type=F32)


def _split2(x):
    hi = x.astype(BF16)
    lo = (x - hi.astype(F32)).astype(BF16)
    return hi, lo


def _dot_exact_lhs(a_exact, b):
    h, l = _split2(b)
    d = lambda x: jnp.dot(a_exact, x, preferred_element_type=F32)
    return d(h) + d(l)


def _head_ones():
    r = lax.broadcasted_iota(jnp.int32, (LANES, LANES), 0) // HEAD_DIM
    c = lax.broadcasted_iota(jnp.int32, (LANES, LANES), 1) // HEAD_DIM
    return jnp.where(r == c, 1.0, 0.0).astype(BF16)


def _head_sum(x, ones):
    return jnp.dot(x.astype(BF16), ones, preferred_element_type=F32)


def _sigmoid(x):
    return 0.5 * jnp.tanh(0.5 * x) + 0.5


def _rms_rows(x):
    return x * lax.rsqrt(jnp.mean(x * x, axis=-1, keepdims=True) + RMS_EPS)


def _const_spec(shape):
    nd = len(shape)
    return pl.BlockSpec(shape, lambda *_: (0,) * nd, pipeline_mode=pl.Buffered(1))


def _params(n_axes):
    return pltpu.CompilerParams(dimension_semantics=("arbitrary",) * n_axes,
                                vmem_limit_bytes=VMEM_LIMIT)


def _mod_kernel(c_ref, w_ref, b_ref, win_ref, o_ref, winb_ref):
    s = c_ref[...]
    s = s * _sigmoid(s)
    w = w_ref[...]
    for r in range(s.shape[1]):
        o_ref[r] = jnp.sum(w * s[:, r:r + 1], axis=0, keepdims=True) + b_ref[...]
    winb_ref[...] = win_ref[...].astype(BF16)


def _modulation(cvecs, w_ada, b_ada, w_in):
    n = cvecs.shape[1]
    tn = 768
    steps = 6 * D_MODEL // tn
    rows = w_in.shape[0] // steps
    assert rows % 16 == 0
    return pl.pallas_call(
        _mod_kernel,
        grid=(steps,),
        in_specs=[pl.BlockSpec((D_MODEL, n), lambda j: (0, 0)),
                  pl.BlockSpec((D_MODEL, tn), lambda j: (0, j)),
                  pl.BlockSpec((1, tn), lambda j: (0, j)),
                  pl.BlockSpec((rows, w_in.shape[1]), lambda j: (j, 0))],
        out_specs=[pl.BlockSpec((n, 1, tn), lambda j: (0, 0, j)),
                   pl.BlockSpec((rows, w_in.shape[1]), lambda j: (j, 0))],
        out_shape=[jax.ShapeDtypeStruct((n, 1, 6 * D_MODEL), F32),
                   jax.ShapeDtypeStruct(w_in.shape, BF16)],
        compiler_params=_params(1),
        name="modulation",
    )(cvecs, w_ada, b_ada.reshape(1, -1), w_in)


def _pick_group(i, tiles_a, a_ref, b_ref):
    return jnp.where(i < tiles_a, a_ref[0], b_ref[0])


def _group_specs(tiles_a, tm, width):
    return [pl.BlockSpec((1, tm, width), lambda i: (jnp.minimum(i, tiles_a - 1), 0, 0)),
            pl.BlockSpec((1, tm, width), lambda i: (jnp.maximum(i - tiles_a, 0), 0, 0))]


def _mod_spec(tiles_a, tiles_per_b):
    return pl.BlockSpec((1, 1, 6 * D_MODEL),
                        lambda i: (jnp.where(i < tiles_a, 0, 1 + (i - tiles_a) // tiles_per_b), 0, 0))


def _mod_part(mod_ref, k):
    return mod_ref[0, :, k * D_MODEL:(k + 1) * D_MODEL]


def _inproj_kernel(xa_ref, xb_ref, mod_ref, g_ref, w_ref, urw_ref, una_ref, gt_ref, *, tiles_a):
    x = _pick_group(pl.program_id(0), tiles_a, xa_ref, xb_ref)
    h = _rms_rows(x) * g_ref[...]
    h = (h * (1.0 + _mod_part(mod_ref, 1)) + _mod_part(mod_ref, 0)).astype(BF16)
    d = lambda lo, hi: jnp.dot(h, w_ref[:, lo:hi], preferred_element_type=F32)
    urw_ref[...] = d(0, RW_COLS)
    una_ref[...] = d(RW_COLS, RW_COLS + NA_IN_COLS)
    gt_ref[...] = d(RW_COLS + NA_IN_COLS, RW_COLS + NA_IN_COLS + GATE_COLS).astype(BF16)


def _in_proj(xa, xb, tiles_per_b, mod_all, norm_g, w_in_bf):
    tm = TOKEN_TILE
    tiles_a, tiles = xa.shape[0], xa.shape[0] + xb.shape[0]
    row = lambda i: (i, 0)
    return pl.pallas_call(
        functools.partial(_inproj_kernel, tiles_a=tiles_a),
        grid=(tiles,),
        in_specs=_group_specs(tiles_a, tm, D_MODEL) + [_mod_spec(tiles_a, tiles_per_b),
                                                       _const_spec((1, D_MODEL)), _const_spec(w_in_bf.shape)],
        out_specs=[pl.BlockSpec((tm, RW_COLS), row),
                   pl.BlockSpec((tm, NA_IN_COLS), row),
                   pl.BlockSpec((tm, GATE_COLS), row)],
        out_shape=[jax.ShapeDtypeStruct((tiles * tm, RW_COLS), F32),
                   jax.ShapeDtypeStruct((tiles * tm, NA_IN_COLS), F32),
                   jax.ShapeDtypeStruct((tiles * tm, GATE_COLS), BF16)],
        compiler_params=_params(1),
        name="in_proj",
    )(xa, xb, mod_all, norm_g.reshape(1, -1), w_in_bf)


def _shift(x, mu):
    t_len = x.shape[0]
    row = lax.broadcasted_iota(jnp.int32, x.shape, 0)
    prev = jnp.where(row == 0, 0.0, pltpu.roll(x, 1, 0))
    nxt = jnp.where(row == t_len - 1, 0.0, pltpu.roll(x, t_len - 1, 0))
    return x + mu[0:1, :] * (prev - x) + mu[1:2, :] * (nxt - x)


def _stack_heads(x, lane_lo):
    return jnp.concatenate([x * lane_lo, x * (1.0 - lane_lo)], axis=0)


def _wkv_intra(units, consts):
    tri, mask_s, mask_i, eye, lane_lo, blk = consts
    stack = lambda z: _stack_heads(z, lane_lo)

    cums = [_dot_exact_lhs(tri[int(u[6])], u[1]) for u in units]

    prep = []
    for (r, lw, kd, v, kk, b, reverse), cum in zip(units, cums):
        mid_row = CHUNK // 2 if reverse else CHUNK // 2 - 1
        tot_row = 0 if reverse else CHUNK - 1
        a = -kk
        ex = cum - lw
        mid = cum[mid_row:mid_row + 1, :]
        tot = cum[tot_row:tot_row + 1, :]
        up = jnp.exp(cum - mid)
        dn = jnp.exp(mid - cum)
        tail = jnp.exp(tot - cum)
        prep.append(dict(
            at_m=stack(a * jnp.exp(ex - mid)).astype(BF16),
            rt_m=stack(r * up).astype(BF16),
            bk_m=jnp.concatenate([b * dn, kd * dn], axis=0).astype(BF16),
            a_e=stack(a * jnp.exp(ex)),
            r_e=stack(r * jnp.exp(cum)),
            bk_t=jnp.concatenate([stack(b * tail), stack(kd * tail)], axis=0).T.astype(BF16),
            vv=stack(v).astype(BF16),
            diag=jnp.where(eye, jnp.exp(tot), 0.0),
            rev=int(reverse)))

    ntd = lambda x, y: lax.dot_general(x, y, (((1,), (1,)), ((), ())), preferred_element_type=F32)
    mm = lambda x, y: jnp.dot(x, y, preferred_element_type=F32)
    diag_blk, swap_eye = blk
    roll_head = lambda z: pltpu.roll(z, HEAD_DIM, 1)
    by_block = lambda z: jnp.concatenate([z[:CHUNK], roll_head(z[CHUNK:])], axis=0)
    top = [by_block(ntd(p["at_m"], p["bk_m"])) for p in prep]
    low = [by_block(ntd(p["rt_m"], p["bk_m"])) for p in prep]
    a_ak = [(roll_head(t) * mask_s[p["rev"]]).astype(BF16) for t, p in zip(top, prep)]
    bot = [jnp.concatenate([t * mask_i[p["rev"]], roll_head(t) * mask_i[p["rev"]]], axis=1).astype(BF16)
           for t, p in zip(low, prep)]

    off_blk = 1.0 - diag_blk
    both = [t * mask_s[p["rev"]] + swap_eye for t, p in zip(top, prep)]
    steps = CHUNK.bit_length() - 1
    diag_bf = diag_blk.astype(BF16)
    for j in range(steps):
        packed = [q.astype(BF16) for q in both]
        res = [mm(qb * diag_bf, qb) for qb in packed]
        both = [r + off_blk * q for r, q in zip(res, both)]
    ts = [pltpu.roll(q, HEAD_DIM, 1).astype(BF16) for q in both]
    x0 = [jnp.concatenate([p["a_e"], mm(ak, p["vv"])], axis=1) for p, ak in zip(prep, a_ak)]
    xs = [mm(t, x.astype(BF16)) for x, t in zip(x0, ts)]

    out = []
    zeros = jnp.zeros((STACK, LANES), BF16)
    for p, x, bt in zip(prep, xs, bot):
        rhs = jnp.concatenate([x.astype(BF16), jnp.concatenate([zeros, p["vv"]], axis=1)], axis=0)
        lhs = jnp.concatenate([bt, p["bk_t"]], axis=0)
        res = mm(lhs, rhs)
        lhs2 = res[:, :LANES] + jnp.concatenate([p["r_e"], p["diag"]], axis=0)
        out.append((lhs2.astype(BF16), res[:, LANES:]))
    return out


def _wkv_constants():
    lane_head = np.arange(LANES) // HEAD_DIM
    ones = (lane_head[:, None] == lane_head[None, :]).astype(np.float32)
    t = np.arange(CHUNK)
    tri = np.stack([t[None, :] <= t[:, None], t[None, :] >= t[:, None]]).astype(np.float32)
    rs, cs = t[:, None], t[None, :]
    tri_masks = np.stack([cs < rs, cs > rs, cs <= rs, cs >= rs]).astype(np.float32)
    masks = np.kron(np.eye(2, dtype=np.float32), tri_masks)
    return jnp.asarray(ones, BF16), jnp.asarray(tri, BF16), jnp.asarray(masks, F32)


def _rwkv_kernel(*refs, t_len, has_s0, pairs, units, n_cast, attend):
    (r_ref, k_ref, v_ref, lo_ref, mur_ref, muk_ref, muv_ref, mul_ref, w0_ref, a0_ref, wup_ref, aup_ref,
     gup_ref, kk_ref, ka_ref, rk_ref, lng_ref, lnb_ref, ones_ref, tri_ref, mask_ref) = refs[:21]
    pos = 21
    s0_ref = None
    if has_s0:
        s0_ref = refs[pos]
        pos += 1
    cast_in = refs[pos:pos + n_cast]
    pos += n_cast
    n_att_in, n_att_out = (5, 3) if attend else (0, 0)
    att_in = refs[pos:pos + n_att_in]
    pos += n_att_in
    o_ref, sn_ref = refs[pos], refs[pos + 1]
    cast_out = refs[pos + 2:pos + 2 + n_cast]
    pos += 2 + n_cast
    att_out = refs[pos:pos + n_att_out]
    (r_s, v_s, kk_s, b0_s, b1_s, lw0_s, lw1_s, kd0_s, kd1_s, gate_s, bonus_s, yf_s, yb_s,
     lhs_s, add_s, st_s) = refs[pos + n_att_out:]
    if attend:
        _na_ctx_kernel(*att_in, *att_out)
    for w_in_ref, w_out_ref in zip(cast_in, cast_out):
        w_out_ref[...] = w_in_ref[...].astype(BF16)

    ones = ones_ref[...]
    lane = lax.broadcasted_iota(jnp.int32, (1, LANES), 1)
    lo_half = lane < HEAD_DIM
    lane_lo = jnp.where(lo_half, 1.0, 0.0)
    mm = lambda x, y: jnp.dot(x, y, preferred_element_type=F32)

    lo = _shift(lo_ref[0], mul_ref[...])
    wd = jnp.tanh(lo[:, 0:LANES])
    ad = lo[:, LANES:2 * LANES]
    sig_gd = _sigmoid(lo[:, 2 * LANES:3 * LANES]).astype(BF16)
    wd_split = [_split2(wd * m) for m in (lane_lo, 1.0 - lane_lo)]
    ad_bf = [(ad * m).astype(BF16) for m in (lane_lo, 1.0 - lane_lo)]
    for j in range(pairs):
        cols = slice(j * LANES, (j + 1) * LANES)
        r = _shift(r_ref[0, :, cols], mur_ref[:, cols])
        k = _shift(k_ref[0, :, cols], muk_ref[:, cols])
        v = _shift(v_ref[0, :, cols], muv_ref[:, cols])
        kk = k * kk_ref[:, cols]
        kk = kk * lax.rsqrt(_head_sum(kk * kk, ones) + L2_EPS)
        wup_h, wup_l = _split2(wup_ref[:, cols])
        aup = aup_ref[:, cols].astype(BF16)
        kdirs = []
        for e, (lw_s, kd_s, b_s) in enumerate(((lw0_s, kd0_s, b0_s), (lw1_s, kd1_s, b1_s))):
            wd_h, wd_l = wd_split[e]
            w_lin = w0_ref[e:e + 1, cols] + (mm(wd_h, wup_h) + mm(wd_l, wup_h) + mm(wd_h, wup_l))
            lw_s[j] = -DECAY_SCALE * _sigmoid(w_lin)
            iclr = _sigmoid(a0_ref[e:e + 1, cols] + mm(ad_bf[e], aup))
            kd = k * (1.0 + (iclr - 1.0) * ka_ref[:, cols])
            kd_s[j] = kd
            b_s[j] = kk * iclr
            kdirs.append(kd)
        gate_s[:, cols] = mm(sig_gd, gup_ref[:, cols].astype(BF16))
        bonus_s[:, cols] = _head_sum(r * (0.5 * (kdirs[0] + kdirs[1])) * rk_ref[:, cols], ones) * v
        r_s[j] = r
        v_s[j] = v
        kk_s[j] = kk

    n_chunks = t_len // CHUNK
    chunks_per = min(n_chunks, units // 2)
    pairs_per = min(pairs, units // (2 * chunks_per))
    groups = n_chunks // chunks_per
    rs = lax.broadcasted_iota(jnp.int32, (STACK, STACK), 0)
    cs = lax.broadcasted_iota(jnp.int32, (STACK, STACK), 1)
    as_f32 = lambda m: jnp.where(m, 1.0, 0.0)
    blk = (as_f32(rs // CHUNK == cs // CHUNK), as_f32(cs == (rs + CHUNK) % STACK))
    consts = ((tri_ref[0], tri_ref[1]), (mask_ref[0], mask_ref[1]), (mask_ref[2], mask_ref[3]),
              rs == cs, lane_lo, blk)
    dirs = ((lw0_s, kd0_s, b0_s), (lw1_s, kd1_s, b1_s))

    def intra_body(it, carry):
        pg = it // groups
        g = it % groups
        units_, ids = [], []
        for jj in range(pairs_per):
            j = pg * pairs_per + jj
            for cc in range(chunks_per):
                c = g * chunks_per + cc
                rows = pl.ds(pl.multiple_of(c * CHUNK, CHUNK), CHUNK)
                for e, (lw_s, kd_s, b_s) in enumerate(dirs):
                    units_.append((r_s[j, rows, :], lw_s[j, rows, :], kd_s[j, rows, :], v_s[j, rows, :],
                                   kk_s[j, rows, :], b_s[j, rows, :], e == 1))
                    ids.append((j * 2 + e) * n_chunks + c)
        for uid, (lhs, add) in zip(ids, _wkv_intra(units_, consts)):
            lhs_s[uid] = lhs
            add_s[uid] = add
        return carry

    lax.fori_loop(0, (pairs // pairs_per) * groups, intra_body, 0, unroll=2)

    for j in range(pairs):
        for e in range(2):
            if has_s0:
                st_s[2 * j + e] = s0_ref[0, e, j].T
            else:
                st_s[2 * j + e] = jnp.zeros((LANES, LANES), F32)

    def state_body(it, carry):
        chunk = (it, n_chunks - 1 - it)
        uids = [(j * 2 + e) * n_chunks + chunk[e] for j in range(pairs) for e in range(2)]
        sts = [st_s[ch].astype(BF16) for ch in range(2 * pairs)]
        res = [mm(lhs_s[uid], st) + add_s[uid] for uid, st in zip(uids, sts)]
        for ch, rr in enumerate(res):
            j, e = divmod(ch, 2)
            y_s = yb_s if e else yf_s
            y_s[j, pl.ds(pl.multiple_of(chunk[e] * CHUNK, CHUNK), CHUNK), :] = rr[:CHUNK] + rr[CHUNK:STACK]
            st_s[ch] = rr[STACK:]
        return carry

    lax.fori_loop(0, n_chunks, state_body, 0, unroll=True)
    for j in range(pairs):
        for e in range(2):
            st_t = st_s[2 * j + e].T
            sn_ref[0, e, 2 * j] = st_t[:HEAD_DIM, :HEAD_DIM]
            sn_ref[0, e, 2 * j + 1] = pltpu.roll(st_t, HEAD_DIM, 1)[HEAD_DIM:, :HEAD_DIM]

    inv_d = 1.0 / HEAD_DIM
    for j in range(pairs):
        cols = slice(j * LANES, (j + 1) * LANES)
        y = yf_s[j] + yb_s[j]
        mean = _head_sum(y, ones) * inv_d
        dlt = y - mean
        var = _head_sum(dlt * dlt, ones) * inv_d
        yn = dlt * lax.rsqrt(var + GN_EPS) * lng_ref[:, cols] + lnb_ref[:, cols]
        o_ref[0, :, cols] = ((yn + bonus_s[:, cols]) * gate_s[:, cols]).astype(o_ref.dtype)


def _rwkv_branch(u_rw, bsz, b_off, s0_big, p, pairs, units, cast=(), attend=None):
    t_len = u_rw.shape[1]
    has_s0 = s0_big is not None
    width = pairs * LANES
    seg = RW_WIDTH // width
    tok = lambda off: pl.BlockSpec((1, t_len, width), lambda b, j: (b_off + b, 0, off + j))
    mu = lambda off: pl.BlockSpec((2, width), lambda b, j: (0, off + j))
    vec2 = pl.BlockSpec((2, width), lambda b, j: (0, j))
    vec1 = pl.BlockSpec((1, width), lambda b, j: (0, j))
    mat = pl.BlockSpec((LANES, width), lambda b, j: (0, j))
    lora_w = 3 * LANES
    lora_blk = 3 * RW_WIDTH // lora_w
    in_specs = [tok(0), tok(seg), tok(2 * seg),
                pl.BlockSpec((1, t_len, lora_w), lambda b, j: (b_off + b, 0, lora_blk)),
                mu(0), mu(seg), mu(2 * seg),
                pl.BlockSpec((2, lora_w), lambda b, j: (0, lora_blk)),
                vec2, vec2, mat, mat, mat, vec1, vec1, vec1, vec1, vec1,
                _const_spec((LANES, LANES)), _const_spec((2, CHUNK, CHUNK)), _const_spec((4, STACK, STACK))]
    args = [u_rw, u_rw, u_rw, u_rw, p["shift_mu"], p["shift_mu"], p["shift_mu"], p["shift_mu"],
            p["rw_w0"], p["rw_a0"],
            p["rw_w_up"].reshape(2 * LORA_DECAY, RW_WIDTH), p["rw_a_up"].reshape(2 * LORA_ICLR, RW_WIDTH),
            p["rw_g_up"], p["rw_k_k"].reshape(1, -1), p["rw_k_a"].reshape(1, -1),
            p["rw_r_k"].reshape(1, -1), p["rw_ln_g"].reshape(1, -1), p["rw_ln_b"].reshape(1, -1),
            *_wkv_constants()]
    st_spec = pl.BlockSpec((1, 2, pairs, LANES, LANES), lambda b, j: (b, 0, j, 0, 0))
    if has_s0:
        in_specs.append(st_spec)
        args.append(s0_big)
    out_specs = [pl.BlockSpec((1, t_len, width), lambda b, j: (b, 0, j)),
                 pl.BlockSpec((1, 2, 2 * pairs, HEAD_DIM, HEAD_DIM), lambda b, j: (b, 0, j, 0, 0))]
    out_shape = [jax.ShapeDtypeStruct((bsz, t_len, RW_WIDTH), BF16),
                 jax.ShapeDtypeStruct((bsz, 2, RW_HEADS, HEAD_DIM, HEAD_DIM), F32)]
    for w in cast:
        assert PAIRS == pairs and w.shape[0] % (16 * bsz) == 0
        blk = pl.BlockSpec((w.shape[0] // bsz, w.shape[1]), lambda b, j: (b, 0))
        in_specs.append(blk)
        args.append(w)
        out_specs.append(blk)
        out_shape.append(jax.ShapeDtypeStruct(w.shape, BF16))
    if attend is not None:
        u_na, q_g, k_g = attend
        assert PAIRS == pairs
        na_tok = lambda seg_: pl.BlockSpec((1, t_len, NA_WIDTH), lambda b, j: (b_off + b, 0, seg_))
        na_out = pl.BlockSpec((1, t_len, NA_WIDTH), lambda b, j: (b, 0, 0))
        g2 = lambda g: jnp.tile(g.reshape(1, HEAD_DIM), (1, 2))
        in_specs += [na_tok(0), na_tok(1), na_tok(2), _const_spec((1, LANES)), _const_spec((1, LANES))]
        args += [u_na, u_na, u_na, g2(q_g), g2(k_g)]
        out_specs += [na_out, na_out, na_out]
        na_shape = (bsz, t_len, NA_WIDTH)
        out_shape += [jax.ShapeDtypeStruct(na_shape, BF16), jax.ShapeDtypeStruct(na_shape, F32),
                      jax.ShapeDtypeStruct(na_shape, F32)]
    n_units = 2 * pairs * (t_len // CHUNK)
    per_pair = pltpu.VMEM((pairs, t_len, LANES), F32)
    full = pltpu.VMEM((t_len, width), F32)
    scratch = [per_pair] * 9 + [full, full, per_pair, per_pair,
                                pltpu.VMEM((n_units, 2 * STACK, LANES), BF16),
                                pltpu.VMEM((n_units, 2 * STACK, LANES), F32),
                                pltpu.VMEM((2 * pairs, LANES, LANES), F32)]
    o_rw, s_new, *rest = pl.pallas_call(
        functools.partial(_rwkv_kernel, t_len=t_len, has_s0=has_s0, pairs=pairs, units=units, n_cast=len(cast),
                          attend=attend is not None),
        grid=(bsz, PAIRS // pairs),
        in_specs=in_specs,
        out_specs=out_specs,
        out_shape=out_shape,
        scratch_shapes=scratch,
        compiler_params=_params(2),
        name="rwkv_branch",
    )(*args)
    return o_rw, s_new, rest[:len(cast)], rest[len(cast):]


def _state_to_big(s0):
    bsz = s0.shape[0]
    x = s0.reshape(bsz, 2, PAIRS, 2, HEAD_DIM, 1, HEAD_DIM)
    same_head = np.eye(2, dtype=bool).reshape(2, 1, 2, 1)
    return jnp.where(same_head, x, 0.0).reshape(bsz, 2, PAIRS, LANES, LANES)


def _qk_norm(t, g, ones):
    ms = _head_sum(t * t, ones) * (1.0 / HEAD_DIM)
    return t * lax.rsqrt(ms + RMS_EPS) * g


def _nt(x, y):
    return lax.dot_general(x, y, (((1,), (1,)), ((), ())), preferred_element_type=F32)


def _na_ctx_kernel(q_ref, k_ref, v_ref, qg_ref, kg_ref, o_ref, kn_ref, vc_ref):
    ones = _head_ones()
    lo_half = lax.broadcasted_iota(jnp.int32, (1, LANES), 1) < HEAD_DIM
    lo = jnp.where(lo_half, 1.0, 0.0)
    t_len = q_ref.shape[1]
    qs, ks, vs = [], [], []
    for j in range(NA_WIDTH // LANES):
        cols = slice(j * LANES, (j + 1) * LANES)
        qn = _qk_norm(q_ref[0, :, cols], qg_ref[...], ones)
        kn = _qk_norm(k_ref[0, :, cols], kg_ref[...], ones)
        v = v_ref[0, :, cols]
        kn_ref[0, :, cols] = kn
        vc_ref[0, :, cols] = v
        qn = qn * QK_SCALE
        qs.append(jnp.concatenate([qn * lo, qn * (1.0 - lo)], axis=0).astype(BF16))
        ks.append(kn.astype(BF16))
        vs.append(v.astype(BF16))
    logits = [_nt(q, k) for q, k in zip(qs, ks)]
    ms = [jnp.max(s, axis=-1, keepdims=True) for s in logits]
    ps = [jnp.exp(s - m) for s, m in zip(logits, ms)]
    ls = [jnp.sum(p, axis=-1, keepdims=True) for p in ps]
    outs = [jnp.dot(p.astype(BF16), v, preferred_element_type=F32) / l for p, v, l in zip(ps, vs, ls)]
    for j, o in enumerate(outs):
        o_ref[0, :, j * LANES:(j + 1) * LANES] = jnp.where(lo_half, o[:t_len], o[t_len:]).astype(o_ref.dtype)


NA_ROW_ILP = 4


def _na_lat_kernel(q_ref, k_ref, v_ref, kc_ref, vc_ref, qg_ref, kg_ref, tab_ref, o_ref,
                   q0_s, q1_s, kn_s, v_s, kc_s, vc_s, *, rows, kr):
    ones = _head_ones()
    lo_half = lax.broadcasted_iota(jnp.int32, (1, LANES), 1) < HEAD_DIM
    lo = jnp.where(lo_half, 1.0, 0.0)
    qn = _qk_norm(q_ref[0], qg_ref[...], ones) * QK_SCALE
    q0_s[...] = (qn * lo).astype(BF16)
    q1_s[...] = (qn * (1.0 - lo)).astype(BF16)
    kn_s[...] = _qk_norm(k_ref[0], kg_ref[...], ones).astype(BF16)
    v_s[...] = v_ref[0].astype(BF16)
    kc_s[...] = kc_ref[0].astype(BF16)
    vc_s[...] = vc_ref[0].astype(BF16)
    win = kr * GRID_W

    def body(it, carry):
        qs, k_rows, q_rows, biases = [], [], [], []
        for s in range(NA_ROW_ILP):
            i = it * NA_ROW_ILP + s
            r0 = jnp.clip(i - kr // 2, 0, rows - kr)
            d0 = r0 - i + (NA_ROWS - 1)
            qr = pl.ds(pl.multiple_of(i * GRID_W, GRID_W), GRID_W)
            q_rows.append(qr)
            k_rows.append(pl.ds(pl.multiple_of(r0 * GRID_W, GRID_W), win))
            qs.append(jnp.concatenate([q0_s[qr, :], q1_s[qr, :]], axis=0))
            biases.append(jnp.concatenate(
                [jnp.concatenate([tab_ref[h, d0 + 2 * m] for m in range(kr // 2)], axis=1) for h in range(2)],
                axis=0))
        lw = [_nt(q, kn_s[kr_, :]) + b for q, kr_, b in zip(qs, k_rows, biases)]
        lc = [_nt(q, kc_s[...]) for q in qs]
        ms = [jnp.maximum(jnp.max(a, axis=-1, keepdims=True), jnp.max(c, axis=-1, keepdims=True))
              for a, c in zip(lw, lc)]
        pw = [jnp.exp(a - m) for a, m in zip(lw, ms)]
        pc = [jnp.exp(c - m) for c, m in zip(lc, ms)]
        ls = [jnp.sum(a, axis=-1, keepdims=True) + jnp.sum(c, axis=-1, keepdims=True) for a, c in zip(pw, pc)]
        outs = [(jnp.dot(a.astype(BF16), v_s[kr_, :], preferred_element_type=F32)
                 + jnp.dot(c.astype(BF16), vc_s[...], preferred_element_type=F32)) / l
                for a, c, kr_, l in zip(pw, pc, k_rows, ls)]
        for qr, o in zip(q_rows, outs):
            o_ref[0, qr, :] = jnp.where(lo_half, o[:GRID_W], o[GRID_W:]).astype(o_ref.dtype)
        return carry

    lax.fori_loop(0, rows // NA_ROW_ILP, body, 0, unroll=True)


def _latent_bias_table(rpb):
    qc = np.arange(GRID_W)[:, None]
    kc = np.arange(GRID_W)[None, :]
    ws = np.clip(qc - NA_COLS // 2, 0, GRID_W - NA_COLS)
    valid = (kc >= ws) & (kc < ws + NA_COLS)
    dc = np.clip(kc - qc, -(NA_COLS - 1), NA_COLS - 1) + NA_COLS - 1
    onehot = (dc[None] == np.arange(2 * NA_COLS - 1)[:, None, None]).astype(np.float32)
    cb = jnp.einsum("hdc,cqk->hdqk", rpb, jnp.asarray(onehot), precision=lax.Precision.HIGHEST)
    cb = jnp.where(valid[None, None], cb, NEG_INF)
    return jnp.concatenate([cb[:, :-1], cb[:, 1:]], axis=-1)


def _na_latent(u_na, bsz, b_off, k_ctx, v_ctx, q_g, k_g, rpb):
    t_len = u_na.shape[1]
    rows = t_len // GRID_W
    kr = min(NA_ROWS, rows)
    assert kr % 2 == 0 and rows % NA_ROW_ILP == 0
    ctx_len = k_ctx.shape[1]
    seg = NA_WIDTH // LANES
    tok = lambda off: pl.BlockSpec((1, t_len, LANES), lambda b, j: (b_off + b, 0, off + j))
    ctx = pl.BlockSpec((1, ctx_len, LANES), lambda b, j: (b, 0, j))
    g2 = lambda g: jnp.tile(g.reshape(1, HEAD_DIM), (1, 2))
    tab = _latent_bias_table(rpb)
    tok_s = pltpu.VMEM((t_len, LANES), BF16)
    ctx_s = pltpu.VMEM((ctx_len, LANES), BF16)
    return pl.pallas_call(
        functools.partial(_na_lat_kernel, rows=rows, kr=kr),
        grid=(bsz, seg),
        in_specs=[tok(0), tok(seg), tok(2 * seg), ctx, ctx,
                  _const_spec((1, LANES)), _const_spec((1, LANES)),
                  pl.BlockSpec((2, 2 * NA_ROWS - 2, GRID_W, 2 * GRID_W), lambda b, j: (j, 0, 0, 0))],
        out_specs=pl.BlockSpec((1, t_len, LANES), lambda b, j: (b, 0, j)),
        out_shape=jax.ShapeDtypeStruct((bsz, t_len, NA_WIDTH), BF16),
        scratch_shapes=[tok_s, tok_s, tok_s, tok_s, ctx_s, ctx_s],
        compiler_params=_params(2),
        name="na_latent",
    )(u_na, u_na, u_na, k_ctx, v_ctx, g2(q_g), g2(k_g), tab)


LATE_WEIGHTS = ("w_o_rwkv", "w_o_na", "w_out", "ffn_w1", "ffn_w3", "ffn_w2")


def _out_ffn_kernel(xa_ref, xb_ref, orwa_ref, orwb_ref, onaa_ref, onab_ref, gt_ref, mod_ref, g_ref,
                    wor_ref, won_ref, wout_ref, w1_ref, w3_ref, w2_ref, ya_ref, yb_ref, *, tiles_a):
    i = pl.program_id(0)
    x = _pick_group(i, tiles_a, xa_ref, xb_ref)
    o_rw = _pick_group(i, tiles_a, orwa_ref, orwb_ref)
    o_na = _pick_group(i, tiles_a, onaa_ref, onab_ref)
    g_rw = _sigmoid(gt_ref[:, :D_MODEL].astype(F32))
    g_na = _sigmoid(gt_ref[:, D_MODEL:].astype(F32))
    merged = g_rw * _dot(o_rw, wor_ref[...]) + g_na * _dot(o_na, won_ref[...])
    x1 = x + _mod_part(mod_ref, 2) * _dot(merged, wout_ref[...])
    h2 = _rms_rows(x1) * g_ref[...]
    h2 = (h2 * (1.0 + _mod_part(mod_ref, 4)) + _mod_part(mod_ref, 3)).astype(BF16)
    acc = jnp.zeros(x1.shape, F32)
    for c in range(FF_HIDDEN // FF_CHUNK):
        cols = slice(c * FF_CHUNK, (c + 1) * FF_CHUNK)
        a = jnp.dot(h2, w1_ref[:, cols], preferred_element_type=F32)
        b = jnp.dot(h2, w3_ref[:, cols], preferred_element_type=F32)
        hh = (a * _sigmoid(a) * b).astype(BF16)
        acc = acc + jnp.dot(hh, w2_ref[cols, :], preferred_element_type=F32)
    y = x1 + _mod_part(mod_ref, 5) * acc

    @pl.when(i < tiles_a)
    def _():
        ya_ref[0] = y

    @pl.when(i >= tiles_a)
    def _():
        yb_ref[0] = y


def _out_ffn(xa, xb, orw_a, orw_b, ona_a, ona_b, gates, tiles_per_b, mod_all, norm_g, wb):
    tm = TOKEN_TILE
    tiles_a, tiles = xa.shape[0], xa.shape[0] + xb.shape[0]
    weights = [wb[n] for n in LATE_WEIGHTS]
    groups = lambda width: _group_specs(tiles_a, tm, width)
    return pl.pallas_call(
        functools.partial(_out_ffn_kernel, tiles_a=tiles_a),
        grid=(tiles,),
        in_specs=groups(D_MODEL) + groups(RW_WIDTH) + groups(NA_WIDTH)
        + [pl.BlockSpec((tm, GATE_COLS), lambda i: (i, 0)), _mod_spec(tiles_a, tiles_per_b),
           _const_spec((1, D_MODEL))] + [_const_spec(w.shape) for w in weights],
        out_specs=groups(D_MODEL),
        out_shape=[jax.ShapeDtypeStruct(xa.shape, F32), jax.ShapeDtypeStruct(xb.shape, F32)],
        compiler_params=_params(1),
        name="out_ffn",
    )(xa, xb, orw_a, orw_b, ona_a, ona_b, gates, mod_all, norm_g.reshape(1, -1), *weights)


def kernel(x_prompt, x_sample, state_rwkv, cache_na_k, cache_na_v, c, c_ctx, norm1_g, norm2_g, w_ada, b_ada,
           w_in, shift_mu, rw_w0, rw_w_up, rw_a0, rw_a_up, rw_g_up, rw_k_k, rw_k_a, rw_r_k, rw_ln_g, rw_ln_b,
           na_q_g, na_k_g, na_rpb, w_o_rwkv, w_o_na, w_out, ffn_w1, ffn_w3, ffn_w2):
    depth = w_in.shape[0]
    bsz, seq = x_prompt.shape[:2]
    dec, dec_seq = x_sample.shape[:2]
    tm = TOKEN_TILE
    n_ctx = bsz * seq
    assert n_ctx % tm == 0 and dec_seq % tm == 0 and n_ctx % dec_seq == 0
    tiles_per_b = dec_seq // tm
    tiled = lambda t: t.reshape(-1, tm, t.shape[-1])
    cvecs = jnp.concatenate([c_ctx[None, :], c], axis=0).T
    y_p, y_s = x_prompt, x_sample
    new_s, new_k, new_v = [], [], []
    for l in range(depth):
        p = dict(norm1_g=norm1_g[l], norm2_g=norm2_g[l], shift_mu=shift_mu[l], rw_w0=rw_w0[l],
                 rw_w_up=rw_w_up[l], rw_a0=rw_a0[l], rw_a_up=rw_a_up[l], rw_g_up=rw_g_up[l],
                 rw_k_k=rw_k_k[l], rw_k_a=rw_k_a[l], rw_r_k=rw_r_k[l], rw_ln_g=rw_ln_g[l],
                 rw_ln_b=rw_ln_b[l], na_q_g=na_q_g[l], na_k_g=na_k_g[l], na_rpb=na_rpb[l])
        late_f32 = [w[l] for w in (w_o_rwkv, w_o_na, w_out, ffn_w1, ffn_w3, ffn_w2)]
        mod_all, w_in_bf = _modulation(cvecs, w_ada[l], b_ada[l], w_in[l])
        u_rw, u_na, gates = _in_proj(tiled(y_p), tiled(y_s), tiles_per_b, mod_all, p["norm1_g"], w_in_bf)
        ctx_view = lambda t: t.reshape(-1, seq, t.shape[-1])
        lat_view = lambda t: t.reshape(-1, dec_seq, t.shape[-1])
        lat_off = n_ctx // dec_seq
        o_rw_p, s_l, casted, (o_na_p, k_l, v_l) = _rwkv_branch(
            ctx_view(u_rw), bsz, 0, None, p, RWKV_PAIRS_CTX, RWKV_UNITS, late_f32,
            attend=(ctx_view(u_na), p["na_q_g"], p["na_k_g"]))
        late = dict(zip(LATE_WEIGHTS, casted))
        new_s.append(s_l)
        new_k.append(k_l.reshape(bsz, seq, NA_HEADS, HEAD_DIM))
        new_v.append(v_l.reshape(bsz, seq, NA_HEADS, HEAD_DIM))
        ctx_k = cache_na_k[:, l].reshape(dec, -1, NA_WIDTH)
        ctx_v = cache_na_v[:, l].reshape(dec, -1, NA_WIDTH)
        o_rw_s = _rwkv_branch(lat_view(u_rw), dec, lat_off, _state_to_big(state_rwkv[:, l]), p, RWKV_PAIRS_LAT,
                               RWKV_UNITS)[0]
        o_na_s = _na_latent(lat_view(u_na), dec, lat_off, ctx_k, ctx_v, p["na_q_g"], p["na_k_g"], p["na_rpb"])
        y_p_t, y_s_t = _out_ffn(tiled(y_p), tiled(y_s), tiled(o_rw_p), tiled(o_rw_s), tiled(o_na_p), tiled(o_na_s),
                                gates, tiles_per_b, mod_all, p["norm2_g"], late)
        y_p, y_s = y_p_t.reshape(x_prompt.shape), y_s_t.reshape(x_sample.shape)
    return (y_p, y_s, jnp.stack(new_s, axis=1), jnp.stack(new_k, axis=1), jnp.stack(new_v, axis=1))
```

```python
import functools

import numpy as np
import jax
import jax.numpy as jnp
from jax import lax
from jax.experimental import pallas as pl
from jax.experimental.pallas import tpu as pltpu

D_MODEL = 1024
GRID_W = 64
HEAD_DIM = 64
RW_HEADS = 8
RW_WIDTH = RW_HEADS * HEAD_DIM
NA_HEADS = 8
NA_WIDTH = NA_HEADS * HEAD_DIM
LORA_DECAY = 64
LORA_ICLR = 64
LORA_GATE = 128
NA_ROWS = 8
NA_COLS = 16
FF_HIDDEN = 2816
RW_COLS = 3 * RW_WIDTH + 2 * LORA_DECAY + 2 * LORA_ICLR + LORA_GATE
NA_IN_COLS = 3 * NA_WIDTH
GATE_COLS = 2 * D_MODEL
RMS_EPS = 1e-6
GN_EPS = 64e-5
L2_EPS = 1e-12
NEG_INF = -1e30
DECAY_SCALE = float(np.exp(-0.5))
QK_SCALE = HEAD_DIM ** -0.5
assert QK_SCALE == 0.125

LANES = 128
PAIRS = RW_HEADS // 2
CHUNK = 64
STACK = 2 * CHUNK
RWKV_PAIRS_CTX = 4
RWKV_PAIRS_LAT = 2
RWKV_UNITS = 16
TOKEN_TILE = 512
FF_CHUNK = 256
VMEM_LIMIT = 60 * 1024 * 1024

F32 = jnp.float32
BF16 = jnp.bfloat16


def _dot(a, b):
    return jnp.dot(a.astype(BF16), b.astype(BF16), preferred_element_type=F32)


def _split2(x):
    hi = x.astype(BF16)
    lo = (x - hi.astype(F32)).astype(BF16)
    return hi, lo


def _dot_exact_lhs(a_exact, b):
    h, l = _split2(b)
    d = lambda x: jnp.dot(a_exact, x, preferred_element_type=F32)
    return d(h) + d(l)


def _head_ones():
    r = lax.broadcasted_iota(jnp.int32, (LANES, LANES), 0) // HEAD_DIM
    c = lax.broadcasted_iota(jnp.int32, (LANES, LANES), 1) // HEAD_DIM
    return jnp.where(r == c, 1.0, 0.0).astype(BF16)


def _head_sum(x, ones):
    return jnp.dot(x.astype(BF16), ones, preferred_element_type=F32)


def _sigmoid(x):
    return 0.5 * jnp.tanh(0.5 * x) + 0.5


def _rms_rows(x):
    return x * lax.rsqrt(jnp.mean(x * x, axis=-1, keepdims=True) + RMS_EPS)


def _const_spec(shape):
    nd = len(shape)
    return pl.BlockSpec(shape, lambda *_: (0,) * nd, pipeline_mode=pl.Buffered(1))


def _params(n_axes):
    return pltpu.CompilerParams(dimension_semantics=("arbitrary",) * n_axes,
                                vmem_limit_bytes=VMEM_LIMIT)


def _mod_kernel(c_ref, w_ref, b_ref, win_ref, o_ref, winb_ref):
    s = c_ref[...]
    s = s * _sigmoid(s)
    w = w_ref[...]
    for r in range(s.shape[1]):
        o_ref[r] = jnp.sum(w * s[:, r:r + 1], axis=0, keepdims=True) + b_ref[...]
    winb_ref[...] = win_ref[...].astype(BF16)


def _modulation(cvecs, w_ada, b_ada, w_in):
    n = cvecs.shape[1]
    tn = 768
    steps = 6 * D_MODEL // tn
    rows = w_in.shape[0] // steps
    assert rows % 16 == 0
    return pl.pallas_call(
        _mod_kernel,
        grid=(steps,),
        in_specs=[pl.BlockSpec((D_MODEL, n), lambda j: (0, 0)),
                  pl.BlockSpec((D_MODEL, tn), lambda j: (0, j)),
                  pl.BlockSpec((1, tn), lambda j: (0, j)),
                  pl.BlockSpec((rows, w_in.shape[1]), lambda j: (j, 0))],
        out_specs=[pl.BlockSpec((n, 1, tn), lambda j: (0, 0, j)),
                   pl.BlockSpec((rows, w_in.shape[1]), lambda j: (j, 0))],
        out_shape=[jax.ShapeDtypeStruct((n, 1, 6 * D_MODEL), F32),
                   jax.ShapeDtypeStruct(w_in.shape, BF16)],
        compiler_params=_params(1),
        name="modulation",
    )(cvecs, w_ada, b_ada.reshape(1, -1), w_in)


def _pick_group(i, tiles_a, a_ref, b_ref):
    return jnp.where(i < tiles_a, a_ref[0], b_ref[0])


def _group_specs(tiles_a, tm, width):
    return [pl.BlockSpec((1, tm, width), lambda i: (jnp.minimum(i, tiles_a - 1), 0, 0)),
            pl.BlockSpec((1, tm, width), lambda i: (jnp.maximum(i - tiles_a, 0), 0, 0))]


def _mod_spec(tiles_a, tiles_per_b):
    return pl.BlockSpec((1, 1, 6 * D_MODEL),
                        lambda i: (jnp.where(i < tiles_a, 0, 1 + (i - tiles_a) // tiles_per_b), 0, 0))


def _mod_part(mod_ref, k):
    return mod_ref[0, :, k * D_MODEL:(k + 1) * D_MODEL]


def _inproj_kernel(xa_ref, xb_ref, mod_ref, g_ref, w_ref, urw_ref, una_ref, gt_ref, *, tiles_a):
    x = _pick_group(pl.program_id(0), tiles_a, xa_ref, xb_ref)
    h = _rms_rows(x) * g_ref[...]
    h = (h * (1.0 + _mod_part(mod_ref, 1)) + _mod_part(mod_ref, 0)).astype(BF16)
    d = lambda lo, hi: jnp.dot(h, w_ref[:, lo:hi], preferred_element_type=F32)
    urw_ref[...] = d(0, RW_COLS)
    una_ref[...] = d(RW_COLS, RW_COLS + NA_IN_COLS)
    gt_ref[...] = d(RW_COLS + NA_IN_COLS, RW_COLS + NA_IN_COLS + GATE_COLS).astype(BF16)


def _in_proj(xa, xb, tiles_per_b, mod_all, norm_g, w_in_bf):
    tm = TOKEN_TILE
    tiles_a, tiles = xa.shape[0], xa.shape[0] + xb.shape[0]
    row = lambda i: (i, 0)
    return pl.pallas_call(
        functools.partial(_inproj_kernel, tiles_a=tiles_a),
        grid=(tiles,),
        in_specs=_group_specs(tiles_a, tm, D_MODEL) + [_mod_spec(tiles_a, tiles_per_b),
                                                       _const_spec((1, D_MODEL)), _const_spec(w_in_bf.shape)],
        out_specs=[pl.BlockSpec((tm, RW_COLS), row),
                   pl.BlockSpec((tm, NA_IN_COLS), row),
                   pl.BlockSpec((tm, GATE_COLS), row)],
        out_shape=[jax.ShapeDtypeStruct((tiles * tm, RW_COLS), F32),
                   jax.ShapeDtypeStruct((tiles * tm, NA_IN_COLS), F32),
                   jax.ShapeDtypeStruct((tiles * tm, GATE_COLS), BF16)],
        compiler_params=_params(1),
        name="in_proj",
    )(xa, xb, mod_all, norm_g.reshape(1, -1), w_in_bf)


def _shift(x, mu):
    t_len = x.shape[0]
    row = lax.broadcasted_iota(jnp.int32, x.shape, 0)
    prev = jnp.where(row == 0, 0.0, pltpu.roll(x, 1, 0))
    nxt = jnp.where(row == t_len - 1, 0.0, pltpu.roll(x, t_len - 1, 0))
    return x + mu[0:1, :] * (prev - x) + mu[1:2, :] * (nxt - x)


def _stack_heads(x, lane_lo):
    return jnp.concatenate([x * lane_lo, x * (1.0 - lane_lo)], axis=0)


def _wkv_intra(units, consts):
    tri, mask_s, mask_i, eye, lane_lo, blk = consts
    stack = lambda z: _stack_heads(z, lane_lo)

    cums = [_dot_exact_lhs(tri[int(u[6])], u[1]) for u in units]

    prep = []
    for (r, lw, kd, v, kk, b, reverse), cum in zip(units, cums):
        mid_row = CHUNK // 2 if reverse else CHUNK // 2 - 1
        tot_row = 0 if reverse else CHUNK - 1
        a = -kk
        ex = cum - lw
        mid = cum[mid_row:mid_row + 1, :]
        tot = cum[tot_row:tot_row + 1, :]
        up = jnp.exp(cum - mid)
        dn = jnp.exp(mid - cum)
        tail = jnp.exp(tot - cum)
        prep.append(dict(
            at_m=stack(a * jnp.exp(ex - mid)).astype(BF16),
            rt_m=stack(r * up).astype(BF16),
            bk_m=jnp.concatenate([b * dn, kd * dn], axis=0).astype(BF16),
            a_e=stack(a * jnp.exp(ex)),
            r_e=stack(r * jnp.exp(cum)),
            bk_t=jnp.concatenate([stack(b * tail), stack(kd * tail)], axis=0).T.astype(BF16),
            vv=stack(v).astype(BF16),
            diag=jnp.where(eye, jnp.exp(tot), 0.0),
            rev=int(reverse)))

    ntd = lambda x, y: lax.dot_general(x, y, (((1,), (1,)), ((), ())), preferred_element_type=F32)
    mm = lambda x, y: jnp.dot(x, y, preferred_element_type=F32)
    diag_blk, swap_eye = blk
    roll_head = lambda z: pltpu.roll(z, HEAD_DIM, 1)
    by_block = lambda z: jnp.concatenate([z[:CHUNK], roll_head(z[CHUNK:])], axis=0)
    top = [by_block(ntd(p["at_m"], p["bk_m"])) for p in prep]
    low = [by_block(ntd(p["rt_m"], p["bk_m"])) for p in prep]
    a_ak = [(roll_head(t) * mask_s[p["rev"]]).astype(BF16) for t, p in zip(top, prep)]
    bot = [jnp.concatenate([t * mask_i[p["rev"]], roll_head(t) * mask_i[p["rev"]]], axis=1).astype(BF16)
           for t, p in zip(low, prep)]

    off_blk = 1.0 - diag_blk
    both = [t * mask_s[p["rev"]] + swap_eye for t, p in zip(top, prep)]
    steps = CHUNK.bit_length() - 1
    diag_bf = diag_blk.astype(BF16)
    for j in range(steps):
        packed = [q.astype(BF16) for q in both]
        res = [mm(qb * diag_bf, qb) for qb in packed]
        both = [r + off_blk * q for r, q in zip(res, both)]
    ts = [pltpu.roll(q, HEAD_DIM, 1).astype(BF16) for q in both]
    x0 = [jnp.concatenate([p["a_e"], mm(ak, p["vv"])], axis=1) for p, ak in zip(prep, a_ak)]
    xs = [mm(t, x.astype(BF16)) for x, t in zip(x0, ts)]

    out = []
    zeros = jnp.zeros((STACK, LANES), BF16)
    for p, x, bt in zip(prep, xs, bot):
        rhs = jnp.concatenate([x.astype(BF16), jnp.concatenate([zeros, p["vv"]], axis=1)], axis=0)
        lhs = jnp.concatenate([bt, p["bk_t"]], axis=0)
        res = mm(lhs, rhs)
        lhs2 = res[:, :LANES] + jnp.concatenate([p["r_e"], p["diag"]], axis=0)
        out.append((lhs2.astype(BF16), res[:, LANES:]))
    return out


def _wkv_constants():
    lane_head = np.arange(LANES) // HEAD_DIM
    ones = (lane_head[:, None] == lane_head[None, :]).astype(np.float32)
    t = np.arange(CHUNK)
    tri = np.stack([t[None, :] <= t[:, None], t[None, :] >= t[:, None]]).astype(np.float32)
    rs, cs = t[:, None], t[None, :]
    tri_masks = np.stack([cs < rs, cs > rs, cs <= rs, cs >= rs]).astype(np.float32)
    masks = np.kron(np.eye(2, dtype=np.float32), tri_masks)
    return jnp.asarray(ones, BF16), jnp.asarray(tri, BF16), jnp.asarray(masks, F32)


def _rwkv_kernel(*refs, t_len, has_s0, pairs, units, n_cast, attend):
    (r_ref, k_ref, v_ref, lo_ref, mur_ref, muk_ref, muv_ref, mul_ref, w0_ref, a0_ref, wup_ref, aup_ref,
     gup_ref, kk_ref, ka_ref, rk_ref, lng_ref, lnb_ref, ones_ref, tri_ref, mask_ref) = refs[:21]
    pos = 21
    s0_ref = None
    if has_s0:
        s0_ref = refs[pos]
        pos += 1
    cast_in = refs[pos:pos + n_cast]
    pos += n_cast
    n_att_in, n_att_out = {None: (0, 0), "context": (5, 3), "latent": (8, 1)}[attend]
    att_in = refs[pos:pos + n_att_in]
    pos += n_att_in
    o_ref, sn_ref = refs[pos], refs[pos + 1]
    cast_out = refs[pos + 2:pos + 2 + n_cast]
    pos += 2 + n_cast
    att_out = refs[pos:pos + n_att_out]
    pos += n_att_out
    (r_s, v_s, kk_s, b0_s, b1_s, lw0_s, lw1_s, kd0_s, kd1_s, gate_s, bonus_s,
     lhs_s, add_s, st_s) = refs[pos:pos + 14]
    yf_s, yb_s = r_s, v_s
    if attend == "context":
        _na_ctx_kernel(*att_in, *att_out)
    elif attend == "latent":
        rows = t_len // GRID_W
        for pp in range(pairs):
            _na_lat_kernel(*att_in, *att_out, *refs[pos + 14:], rows=rows, kr=min(NA_ROWS, rows),
                           cols=slice(pp * LANES, (pp + 1) * LANES), head0=2 * pp)
    for w_in_ref, w_out_ref in zip(cast_in, cast_out):
        w_out_ref[...] = w_in_ref[...].astype(BF16)

    ones = ones_ref[...]
    lane = lax.broadcasted_iota(jnp.int32, (1, LANES), 1)
    lo_half = lane < HEAD_DIM
    lane_lo = jnp.where(lo_half, 1.0, 0.0)
    mm = lambda x, y: jnp.dot(x, y, preferred_element_type=F32)

    lo = _shift(lo_ref[0], mul_ref[...])
    wd = jnp.tanh(lo[:, 0:LANES])
    ad = lo[:, LANES:2 * LANES]
    sig_gd = _sigmoid(lo[:, 2 * LANES:3 * LANES]).astype(BF16)
    wd_split = [_split2(wd * m) for m in (lane_lo, 1.0 - lane_lo)]
    ad_bf = [(ad * m).astype(BF16) for m in (lane_lo, 1.0 - lane_lo)]
    for j in range(pairs):
        cols = slice(j * LANES, (j + 1) * LANES)
        r = _shift(r_ref[0, :, cols], mur_ref[:, cols])
        k = _shift(k_ref[0, :, cols], muk_ref[:, cols])
        v = _shift(v_ref[0, :, cols], muv_ref[:, cols])
        kk = k * kk_ref[:, cols]
        kk = kk * lax.rsqrt(_head_sum(kk * kk, ones) + L2_EPS)
        wup_h, wup_l = _split2(wup_ref[:, cols])
        aup = aup_ref[:, cols].astype(BF16)
        kdirs = []
        for e, (lw_s, kd_s, b_s) in enumerate(((lw0_s, kd0_s, b0_s), (lw1_s, kd1_s, b1_s))):
            wd_h, wd_l = wd_split[e]
            w_lin = w0_ref[e:e + 1, cols] + (mm(wd_h, wup_h) + mm(wd_l, wup_h) + mm(wd_h, wup_l))
            lw_s[j] = -DECAY_SCALE * _sigmoid(w_lin)
            iclr = _sigmoid(a0_ref[e:e + 1, cols] + mm(ad_bf[e], aup))
            kd = k * (1.0 + (iclr - 1.0) * ka_ref[:, cols])
            kd_s[j] = kd
            b_s[j] = kk * iclr
            kdirs.append(kd)
        gate_s[:, cols] = mm(sig_gd, gup_ref[:, cols].astype(BF16))
        bonus_s[:, cols] = _head_sum(r * (0.5 * (kdirs[0] + kdirs[1])) * rk_ref[:, cols], ones) * v
        r_s[j] = r
        v_s[j] = v
        kk_s[j] = kk

    n_chunks = t_len // CHUNK
    chunks_per = min(n_chunks, units // 2)
    pairs_per = min(pairs, units // (2 * chunks_per))
    groups = n_chunks // chunks_per
    rs = lax.broadcasted_iota(jnp.int32, (STACK, STACK), 0)
    cs = lax.broadcasted_iota(jnp.int32, (STACK, STACK), 1)
    as_f32 = lambda m: jnp.where(m, 1.0, 0.0)
    blk = (as_f32(rs // CHUNK == cs // CHUNK), as_f32(cs == (rs + CHUNK) % STACK))
    consts = ((tri_ref[0], tri_ref[1]), (mask_ref[0], mask_ref[1]), (mask_ref[2], mask_ref[3]),
              rs == cs, lane_lo, blk)
    dirs = ((lw0_s, kd0_s, b0_s), (lw1_s, kd1_s, b1_s))

    def intra_body(it, carry):
        pg = it // groups
        g = it % groups
        units_, ids = [], []
        for jj in range(pairs_per):
            j = pg * pairs_per + jj
            for cc in range(chunks_per):
                c = g * chunks_per + cc
                rows = pl.ds(pl.multiple_of(c * CHUNK, CHUNK), CHUNK)
                for e, (lw_s, kd_s, b_s) in enumerate(dirs):
                    units_.append((r_s[j, rows, :], lw_s[j, rows, :], kd_s[j, rows, :], v_s[j, rows, :],
                                   kk_s[j, rows, :], b_s[j, rows, :], e == 1))
                    ids.append((j * 2 + e) * n_chunks + c)
        for uid, (lhs, add) in zip(ids, _wkv_intra(units_, consts)):
            lhs_s[uid] = lhs
            add_s[uid] = add
        return carry

    lax.fori_loop(0, (pairs // pairs_per) * groups, intra_body, 0, unroll=2)

    for j in range(pairs):
        for e in range(2):
            if has_s0:
                st_s[2 * j + e] = s0_ref[0, e, j].T
            else:
                st_s[2 * j + e] = jnp.zeros((LANES, LANES), F32)

    def state_body(it, carry):
        chunk = (it, n_chunks - 1 - it)
        uids = [(j * 2 + e) * n_chunks + chunk[e] for j in range(pairs) for e in range(2)]
        sts = [st_s[ch].astype(BF16) for ch in range(2 * pairs)]
        res = [mm(lhs_s[uid], st) + add_s[uid] for uid, st in zip(uids, sts)]
        for ch, rr in enumerate(res):
            j, e = divmod(ch, 2)
            y_s = yb_s if e else yf_s
            y_s[j, pl.ds(pl.multiple_of(chunk[e] * CHUNK, CHUNK), CHUNK), :] = rr[:CHUNK] + rr[CHUNK:STACK]
            st_s[ch] = rr[STACK:]
        return carry

    lax.fori_loop(0, n_chunks, state_body, 0, unroll=True)
    for j in range(pairs):
        for e in range(2):
            st_t = st_s[2 * j + e].T
            sn_ref[0, e, 2 * j] = st_t[:HEAD_DIM, :HEAD_DIM]
            sn_ref[0, e, 2 * j + 1] = pltpu.roll(st_t, HEAD_DIM, 1)[HEAD_DIM:, :HEAD_DIM]

    inv_d = 1.0 / HEAD_DIM
    for j in range(pairs):
        cols = slice(j * LANES, (j + 1) * LANES)
        y = yf_s[j] + yb_s[j]
        mean = _head_sum(y, ones) * inv_d
        dlt = y - mean
        var = _head_sum(dlt * dlt, ones) * inv_d
        yn = dlt * lax.rsqrt(var + GN_EPS) * lng_ref[:, cols] + lnb_ref[:, cols]
        o_ref[0, :, cols] = ((yn + bonus_s[:, cols]) * gate_s[:, cols]).astype(o_ref.dtype)


def _rwkv_branch(u_rw, bsz, b_off, s0_big, p, pairs, units, cast=(), attend=None):
    t_len = u_rw.shape[1]
    has_s0 = s0_big is not None
    width = pairs * LANES
    seg = RW_WIDTH // width
    tok = lambda off: pl.BlockSpec((1, t_len, width), lambda b, j: (b_off + b, 0, off + j))
    mu = lambda off: pl.BlockSpec((2, width), lambda b, j: (0, off + j))
    vec2 = pl.BlockSpec((2, width), lambda b, j: (0, j))
    vec1 = pl.BlockSpec((1, width), lambda b, j: (0, j))
    mat = pl.BlockSpec((LANES, width), lambda b, j: (0, j))
    lora_w = 3 * LANES
    lora_blk = 3 * RW_WIDTH // lora_w
    in_specs = [tok(0), tok(seg), tok(2 * seg),
                pl.BlockSpec((1, t_len, lora_w), lambda b, j: (b_off + b, 0, lora_blk)),
                mu(0), mu(seg), mu(2 * seg),
                pl.BlockSpec((2, lora_w), lambda b, j: (0, lora_blk)),
                vec2, vec2, mat, mat, mat, vec1, vec1, vec1, vec1, vec1,
                _const_spec((LANES, LANES)), _const_spec((2, CHUNK, CHUNK)), _const_spec((4, STACK, STACK))]
    args = [u_rw, u_rw, u_rw, u_rw, p["shift_mu"], p["shift_mu"], p["shift_mu"], p["shift_mu"],
            p["rw_w0"], p["rw_a0"],
            p["rw_w_up"].reshape(2 * LORA_DECAY, RW_WIDTH), p["rw_a_up"].reshape(2 * LORA_ICLR, RW_WIDTH),
            p["rw_g_up"], p["rw_k_k"].reshape(1, -1), p["rw_k_a"].reshape(1, -1),
            p["rw_r_k"].reshape(1, -1), p["rw_ln_g"].reshape(1, -1), p["rw_ln_b"].reshape(1, -1),
            *_wkv_constants()]
    st_spec = pl.BlockSpec((1, 2, pairs, LANES, LANES), lambda b, j: (b, 0, j, 0, 0))
    if has_s0:
        in_specs.append(st_spec)
        args.append(s0_big)
    out_specs = [pl.BlockSpec((1, t_len, width), lambda b, j: (b, 0, j)),
                 pl.BlockSpec((1, 2, 2 * pairs, HEAD_DIM, HEAD_DIM), lambda b, j: (b, 0, j, 0, 0))]
    out_shape = [jax.ShapeDtypeStruct((bsz, t_len, RW_WIDTH), BF16),
                 jax.ShapeDtypeStruct((bsz, 2, RW_HEADS, HEAD_DIM, HEAD_DIM), F32)]
    for w in cast:
        assert PAIRS == pairs and w.shape[0] % (16 * bsz) == 0
        blk = pl.BlockSpec((w.shape[0] // bsz, w.shape[1]), lambda b, j: (b, 0))
        in_specs.append(blk)
        args.append(w)
        out_specs.append(blk)
        out_shape.append(jax.ShapeDtypeStruct(w.shape, BF16))
    na_scratch = []
    g2 = lambda g: jnp.tile(g.reshape(1, HEAD_DIM), (1, 2))
    na_shape = (bsz, t_len, NA_WIDTH)
    if attend is not None and attend[0] == "context":
        _, u_na, q_g, k_g = attend
        assert PAIRS == pairs
        na_tok = lambda seg_: pl.BlockSpec((1, t_len, NA_WIDTH), lambda b, j: (b_off + b, 0, seg_))
        na_out = pl.BlockSpec((1, t_len, NA_WIDTH), lambda b, j: (b, 0, 0))
        in_specs += [na_tok(0), na_tok(1), na_tok(2), _const_spec((1, LANES)), _const_spec((1, LANES))]
        args += [u_na, u_na, u_na, g2(q_g), g2(k_g)]
        out_specs += [na_out, na_out, na_out]
        out_shape += [jax.ShapeDtypeStruct(na_shape, BF16), jax.ShapeDtypeStruct(na_shape, F32),
                      jax.ShapeDtypeStruct(na_shape, F32)]
    elif attend is not None:
        _, u_na, k_ctx, v_ctx, q_g, k_g, rpb = attend
        assert RW_WIDTH == NA_WIDTH and (t_len // GRID_W) % NA_ROW_ILP == 0 and min(NA_ROWS, t_len // GRID_W) % 2 == 0
        ctx_len = k_ctx.shape[1]
        ctx = pl.BlockSpec((1, ctx_len, width), lambda b, j: (b, 0, j))
        in_specs += [tok(0), tok(seg), tok(2 * seg), ctx, ctx, _const_spec((1, LANES)), _const_spec((1, LANES)),
                     pl.BlockSpec((2 * pairs, 2 * NA_ROWS - 2, GRID_W, 2 * GRID_W), lambda b, j: (j, 0, 0, 0))]
        args += [u_na, u_na, u_na, k_ctx, v_ctx, g2(q_g), g2(k_g), _latent_bias_table(rpb)]
        out_specs.append(pl.BlockSpec((1, t_len, width), lambda b, j: (b, 0, j)))
        out_shape.append(jax.ShapeDtypeStruct(na_shape, BF16))
        na_scratch = [pltpu.VMEM((t_len, LANES), BF16)] * 4 + [pltpu.VMEM((ctx_len, LANES), BF16)] * 2
    n_units = 2 * pairs * (t_len // CHUNK)
    per_pair = pltpu.VMEM((pairs, t_len, LANES), F32)
    full = pltpu.VMEM((t_len, width), F32)
    scratch = [per_pair] * 9 + [full, full,
                                pltpu.VMEM((n_units, 2 * STACK, LANES), BF16),
                                pltpu.VMEM((n_units, 2 * STACK, LANES), F32),
                                pltpu.VMEM((2 * pairs, LANES, LANES), F32)] + na_scratch
    o_rw, s_new, *rest = pl.pallas_call(
        functools.partial(_rwkv_kernel, t_len=t_len, has_s0=has_s0, pairs=pairs, units=units, n_cast=len(cast),
                          attend=None if attend is None else attend[0]),
        grid=(bsz, PAIRS // pairs),
        in_specs=in_specs,
        out_specs=out_specs,
        out_shape=out_shape,
        scratch_shapes=scratch,
        compiler_params=_params(2),
        name="rwkv_branch",
    )(*args)
    return o_rw, s_new, rest[:len(cast)], rest[len(cast):]


def _state_to_big(s0):
    bsz = s0.shape[0]
    x = s0.reshape(bsz, 2, PAIRS, 2, HEAD_DIM, 1, HEAD_DIM)
    same_head = np.eye(2, dtype=bool).reshape(2, 1, 2, 1)
    return jnp.where(same_head, x, 0.0).reshape(bsz, 2, PAIRS, LANES, LANES)


def _qk_norm(t, g, ones):
    ms = _head_sum(t * t, ones) * (1.0 / HEAD_DIM)
    return t * lax.rsqrt(ms + RMS_EPS) * g


def _nt(x, y):
    return lax.dot_general(x, y, (((1,), (1,)), ((), ())), preferred_element_type=F32)


def _na_ctx_kernel(q_ref, k_ref, v_ref, qg_ref, kg_ref, o_ref, kn_ref, vc_ref):
    ones = _head_ones()
    lo_half = lax.broadcasted_iota(jnp.int32, (1, LANES), 1) < HEAD_DIM
    lo = jnp.where(lo_half, 1.0, 0.0)
    t_len = q_ref.shape[1]
    qs, ks, vs = [], [], []
    for j in range(NA_WIDTH // LANES):
        cols = slice(j * LANES, (j + 1) * LANES)
        qn = _qk_norm(q_ref[0, :, cols], qg_ref[...], ones)
        kn = _qk_norm(k_ref[0, :, cols], kg_ref[...], ones)
        v = v_ref[0, :, cols]
        kn_ref[0, :, cols] = kn
        vc_ref[0, :, cols] = v
        qn = qn * QK_SCALE
        qs.append(jnp.concatenate([qn * lo, qn * (1.0 - lo)], axis=0).astype(BF16))
        ks.append(kn.astype(BF16))
        vs.append(v.astype(BF16))
    logits = [_nt(q, k) for q, k in zip(qs, ks)]
    ms = [jnp.max(s, axis=-1, keepdims=True) for s in logits]
    ps = [jnp.exp(s - m) for s, m in zip(logits, ms)]
    ls = [jnp.sum(p, axis=-1, keepdims=True) for p in ps]
    outs = [jnp.dot(p.astype(BF16), v, preferred_element_type=F32) / l for p, v, l in zip(ps, vs, ls)]
    for j, o in enumerate(outs):
        o_ref[0, :, j * LANES:(j + 1) * LANES] = jnp.where(lo_half, o[:t_len], o[t_len:]).astype(o_ref.dtype)


NA_ROW_ILP = 4


def _na_lat_kernel(q_ref, k_ref, v_ref, kc_ref, vc_ref, qg_ref, kg_ref, tab_ref, o_ref,
                   q0_s, q1_s, kn_s, v_s, kc_s, vc_s, *, rows, kr, cols=slice(0, LANES), head0=0):
    ones = _head_ones()
    lo_half = lax.broadcasted_iota(jnp.int32, (1, LANES), 1) < HEAD_DIM
    lo = jnp.where(lo_half, 1.0, 0.0)
    qn = _qk_norm(q_ref[0, :, cols], qg_ref[...], ones) * QK_SCALE
    q0_s[...] = (qn * lo).astype(BF16)
    q1_s[...] = (qn * (1.0 - lo)).astype(BF16)
    kn_s[...] = _qk_norm(k_ref[0, :, cols], kg_ref[...], ones).astype(BF16)
    v_s[...] = v_ref[0, :, cols].astype(BF16)
    kc_s[...] = kc_ref[0, :, cols].astype(BF16)
    vc_s[...] = vc_ref[0, :, cols].astype(BF16)
    win = kr * GRID_W

    def body(it, carry):
        qs, k_rows, q_rows, biases = [], [], [], []
        for s in range(NA_ROW_ILP):
            i = it * NA_ROW_ILP + s
            r0 = jnp.clip(i - kr // 2, 0, rows - kr)
            d0 = r0 - i + (NA_ROWS - 1)
            qr = pl.ds(pl.multiple_of(i * GRID_W, GRID_W), GRID_W)
            q_rows.append(qr)
            k_rows.append(pl.ds(pl.multiple_of(r0 * GRID_W, GRID_W), win))
            qs.append(jnp.concatenate([q0_s[qr, :], q1_s[qr, :]], axis=0))
            biases.append(jnp.concatenate(
                [jnp.concatenate([tab_ref[head0 + h, d0 + 2 * m] for m in range(kr // 2)], axis=1) for h in range(2)],
                axis=0))
        lw = [_nt(q, kn_s[kr_, :]) + b for q, kr_, b in zip(qs, k_rows, biases)]
        lc = [_nt(q, kc_s[...]) for q in qs]
        ms = [jnp.maximum(jnp.max(a, axis=-1, keepdims=True), jnp.max(c, axis=-1, keepdims=True))
              for a, c in zip(lw, lc)]
        pw = [jnp.exp(a - m) for a, m in zip(lw, ms)]
        pc = [jnp.exp(c - m) for c, m in zip(lc, ms)]
        ls = [jnp.sum(a, axis=-1, keepdims=True) + jnp.sum(c, axis=-1, keepdims=True) for a, c in zip(pw, pc)]
        outs = [(jnp.dot(a.astype(BF16), v_s[kr_, :], preferred_element_type=F32)
                 + jnp.dot(c.astype(BF16), vc_s[...], preferred_element_type=F32)) / l
                for a, c, kr_, l in zip(pw, pc, k_rows, ls)]
        for qr, o in zip(q_rows, outs):
            o_ref[0, qr, cols] = jnp.where(lo_half, o[:GRID_W], o[GRID_W:]).astype(o_ref.dtype)
        return carry

    lax.fori_loop(0, rows // NA_ROW_ILP, body, 0)


def _latent_bias_table(rpb):
    qc = np.arange(GRID_W)[:, None]
    kc = np.arange(GRID_W)[None, :]
    ws = np.clip(qc - NA_COLS // 2, 0, GRID_W - NA_COLS)
    valid = (kc >= ws) & (kc < ws + NA_COLS)
    dc = np.clip(kc - qc, -(NA_COLS - 1), NA_COLS - 1) + NA_COLS - 1
    onehot = (dc[None] == np.arange(2 * NA_COLS - 1)[:, None, None]).astype(np.float32)
    cb = jnp.einsum("hdc,cqk->hdqk", rpb, jnp.asarray(onehot), precision=lax.Precision.HIGHEST)
    cb = jnp.where(valid[None, None], cb, NEG_INF)
    return jnp.concatenate([cb[:, :-1], cb[:, 1:]], axis=-1)


LATE_WEIGHTS = ("w_o_rwkv", "w_o_na", "w_out", "ffn_w1", "ffn_w3", "ffn_w2")


def _out_ffn_kernel(xa_ref, xb_ref, orwa_ref, orwb_ref, onaa_ref, onab_ref, gt_ref, mod_ref, g_ref,
                    wor_ref, won_ref, wout_ref, w1_ref, w3_ref, w2_ref, ya_ref, yb_ref, *, tiles_a):
    i = pl.program_id(0)
    x = _pick_group(i, tiles_a, xa_ref, xb_ref)
    o_rw = _pick_group(i, tiles_a, orwa_ref, orwb_ref)
    o_na = _pick_group(i, tiles_a, onaa_ref, onab_ref)
    g_rw = _sigmoid(gt_ref[:, :D_MODEL].astype(F32))
    g_na = _sigmoid(gt_ref[:, D_MODEL:].astype(F32))
    merged = g_rw * _dot(o_rw, wor_ref[...]) + g_na * _dot(o_na, won_ref[...])
    x1 = x + _mod_part(mod_ref, 2) * _dot(merged, wout_ref[...])
    h2 = _rms_rows(x1) * g_ref[...]
    h2 = (h2 * (1.0 + _mod_part(mod_ref, 4)) + _mod_part(mod_ref, 3)).astype(BF16)
    acc = jnp.zeros(x1.shape, F32)
    for c in range(FF_HIDDEN // FF_CHUNK):
        cols = slice(c * FF_CHUNK, (c + 1) * FF_CHUNK)
        a = jnp.dot(h2, w1_ref[:, cols], preferred_element_type=F32)
        b = jnp.dot(h2, w3_ref[:, cols], preferred_element_type=F32)
        hh = (a * _sigmoid(a) * b).astype(BF16)
        acc = acc + jnp.dot(hh, w2_ref[cols, :], preferred_element_type=F32)
    y = x1 + _mod_part(mod_ref, 5) * acc

    @pl.when(i < tiles_a)
    def _():
        ya_ref[0] = y

    @pl.when(i >= tiles_a)
    def _():
        yb_ref[0] = y


def _out_ffn(xa, xb, orw_a, orw_b, ona_a, ona_b, gates, tiles_per_b, mod_all, norm_g, wb):
    tm = TOKEN_TILE
    tiles_a, tiles = xa.shape[0], xa.shape[0] + xb.shape[0]
    weights = [wb[n] for n in LATE_WEIGHTS]
    groups = lambda width: _group_specs(tiles_a, tm, width)
    return pl.pallas_call(
        functools.partial(_out_ffn_kernel, tiles_a=tiles_a),
        grid=(tiles,),
        in_specs=groups(D_MODEL) + groups(RW_WIDTH) + groups(NA_WIDTH)
        + [pl.BlockSpec((tm, GATE_COLS), lambda i: (i, 0)), _mod_spec(tiles_a, tiles_per_b),
           _const_spec((1, D_MODEL))] + [_const_spec(w.shape) for w in weights],
        out_specs=groups(D_MODEL),
        out_shape=[jax.ShapeDtypeStruct(xa.shape, F32), jax.ShapeDtypeStruct(xb.shape, F32)],
        compiler_params=_params(1),
        name="out_ffn",
    )(xa, xb, orw_a, orw_b, ona_a, ona_b, gates, mod_all, norm_g.reshape(1, -1), *weights)


def kernel(x_prompt, x_sample, state_rwkv, cache_na_k, cache_na_v, c, c_ctx, norm1_g, norm2_g, w_ada, b_ada,
           w_in, shift_mu, rw_w0, rw_w_up, rw_a0, rw_a_up, rw_g_up, rw_k_k, rw_k_a, rw_r_k, rw_ln_g, rw_ln_b,
           na_q_g, na_k_g, na_rpb, w_o_rwkv, w_o_na, w_out, ffn_w1, ffn_w3, ffn_w2):
    depth = w_in.shape[0]
    bsz, seq = x_prompt.shape[:2]
    dec, dec_seq = x_sample.shape[:2]
    tm = TOKEN_TILE
    n_ctx = bsz * seq
    assert n_ctx % tm == 0 and dec_seq % tm == 0 and n_ctx % dec_seq == 0
    tiles_per_b = dec_seq // tm
    tiled = lambda t: t.reshape(-1, tm, t.shape[-1])
    cvecs = jnp.concatenate([c_ctx[None, :], c], axis=0).T
    y_p, y_s = x_prompt, x_sample
    new_s, new_k, new_v = [], [], []
    for l in range(depth):
        p = dict(norm1_g=norm1_g[l], norm2_g=norm2_g[l], shift_mu=shift_mu[l], rw_w0=rw_w0[l],
                 rw_w_up=rw_w_up[l], rw_a0=rw_a0[l], rw_a_up=rw_a_up[l], rw_g_up=rw_g_up[l],
                 rw_k_k=rw_k_k[l], rw_k_a=rw_k_a[l], rw_r_k=rw_r_k[l], rw_ln_g=rw_ln_g[l],
                 rw_ln_b=rw_ln_b[l], na_q_g=na_q_g[l], na_k_g=na_k_g[l], na_rpb=na_rpb[l])
        late_f32 = [w[l] for w in (w_o_rwkv, w_o_na, w_out, ffn_w1, ffn_w3, ffn_w2)]
        mod_all, w_in_bf = _modulation(cvecs, w_ada[l], b_ada[l], w_in[l])
        u_rw, u_na, gates = _in_proj(tiled(y_p), tiled(y_s), tiles_per_b, mod_all, p["norm1_g"], w_in_bf)
        ctx_view = lambda t: t.reshape(-1, seq, t.shape[-1])
        lat_view = lambda t: t.reshape(-1, dec_seq, t.shape[-1])
        lat_off = n_ctx // dec_seq
        o_rw_p, s_l, casted, (o_na_p, k_l, v_l) = _rwkv_branch(
            ctx_view(u_rw), bsz, 0, None, p, RWKV_PAIRS_CTX, RWKV_UNITS, late_f32,
            attend=("context", ctx_view(u_na), p["na_q_g"], p["na_k_g"]))
        late = dict(zip(LATE_WEIGHTS, casted))
        new_s.append(s_l)
        new_k.append(k_l.reshape(bsz, seq, NA_HEADS, HEAD_DIM))
        new_v.append(v_l.reshape(bsz, seq, NA_HEADS, HEAD_DIM))
        ctx_k = cache_na_k[:, l].reshape(dec, -1, NA_WIDTH)
        ctx_v = cache_na_v[:, l].reshape(dec, -1, NA_WIDTH)
        o_rw_s, _, _, (o_na_s,) = _rwkv_branch(
            lat_view(u_rw), dec, lat_off, _state_to_big(state_rwkv[:, l]), p, RWKV_PAIRS_LAT, RWKV_UNITS,
            attend=("latent", lat_view(u_na), ctx_k, ctx_v, p["na_q_g"], p["na_k_g"], p["na_rpb"]))
        y_p_t, y_s_t = _out_ffn(tiled(y_p), tiled(y_s), tiled(o_rw_p), tiled(o_rw_s), tiled(o_na_p), tiled(o_na_s),
                                gates, tiles_per_b, mod_all, p["norm2_g"], late)
        y_p, y_s = y_p_t.reshape(x_prompt.shape), y_s_t.reshape(x_sample.shape)
    return (y_p, y_s, jnp.stack(new_s, axis=1), jnp.stack(new_k, axis=1), jnp.stack(new_v, axis=1))
```

```python
import functools

import numpy as np
import jax
import jax.numpy as jnp
from jax import lax
from jax.experimental import pallas as pl
from jax.experimental.pallas import tpu as pltpu

D_MODEL = 1024
GRID_W = 64
HEAD_DIM = 64
RW_HEADS = 8
RW_WIDTH = RW_HEADS * HEAD_DIM
NA_HEADS = 8
NA_WIDTH = NA_HEADS * HEAD_DIM
LORA_DECAY = 64
LORA_ICLR = 64
LORA_GATE = 128
NA_ROWS = 8
NA_COLS = 16
FF_HIDDEN = 2816
RW_COLS = 3 * RW_WIDTH + 2 * LORA_DECAY + 2 * LORA_ICLR + LORA_GATE
NA_IN_COLS = 3 * NA_WIDTH
GATE_COLS = 2 * D_MODEL
RMS_EPS = 1e-6
GN_EPS = 64e-5
L2_EPS = 1e-12
NEG_INF = -1e30
DECAY_SCALE = float(np.exp(-0.5))
QK_SCALE = HEAD_DIM ** -0.5
assert QK_SCALE == 0.125

LANES = 128
PAIRS = RW_HEADS // 2
CHUNK = 64
STACK = 2 * CHUNK
RWKV_PAIRS_CTX = 4
RWKV_PAIRS_LAT = 2
RWKV_UNITS = 16
TOKEN_TILE = 512
FF_CHUNK = 256
VMEM_LIMIT = 56 * 1024 * 1024

F32 = jnp.float32
BF16 = jnp.bfloat16


def _dot(a, b):
    return jnp.dot(a.astype(BF16), b.astype(BF16), preferred_element_type=F32)


def _split2(x):
    hi = x.astype(BF16)
    lo = (x - hi.astype(F32)).astype(BF16)
    return hi, lo


def _dot_exact_lhs(a_exact, b):
    h, l = _split2(b)
    d = lambda x: jnp.dot(a_exact, x, preferred_element_type=F32)
    return d(h) + d(l)


def _head_ones():
    r = lax.broadcasted_iota(jnp.int32, (LANES, LANES), 0) // HEAD_DIM
    c = lax.broadcasted_iota(jnp.int32, (LANES, LANES), 1) // HEAD_DIM
    return jnp.where(r == c, 1.0, 0.0).astype(BF16)


def _head_sum(x, ones):
    return jnp.dot(x.astype(BF16), ones, preferred_element_type=F32)


def _sigmoid(x):
    return 0.5 * jnp.tanh(0.5 * x) + 0.5


def _rms_rows(x):
    return x * lax.rsqrt(jnp.mean(x * x, axis=-1, keepdims=True) + RMS_EPS)


def _const_spec(shape):
    nd = len(shape)
    return pl.BlockSpec(shape, lambda *_: (0,) * nd, pipeline_mode=pl.Buffered(1))


def _params(n_axes):
    return pltpu.CompilerParams(dimension_semantics=("arbitrary",) * n_axes,
                                vmem_limit_bytes=VMEM_LIMIT)


def _mod_kernel(c_ref, w_ref, b_ref, win_ref, o_ref, winb_ref):
    s = c_ref[...]
    s = s * _sigmoid(s)
    w = w_ref[...]
    for r in range(s.shape[1]):
        o_ref[r] = jnp.sum(w * s[:, r:r + 1], axis=0, keepdims=True) + b_ref[...]
    winb_ref[...] = win_ref[...].astype(BF16)


def _modulation(cvecs, w_ada, b_ada, w_in):
    n = cvecs.shape[1]
    tn = 768
    steps = 6 * D_MODEL // tn
    rows = w_in.shape[0] // steps
    assert rows % 16 == 0
    return pl.pallas_call(
        _mod_kernel,
        grid=(steps,),
        in_specs=[pl.BlockSpec((D_MODEL, n), lambda j: (0, 0)),
                  pl.BlockSpec((D_MODEL, tn), lambda j: (0, j)),
                  pl.BlockSpec((1, tn), lambda j: (0, j)),
                  pl.BlockSpec((rows, w_in.shape[1]), lambda j: (j, 0))],
        out_specs=[pl.BlockSpec((n, 1, tn), lambda j: (0, 0, j)),
                   pl.BlockSpec((rows, w_in.shape[1]), lambda j: (j, 0))],
        out_shape=[jax.ShapeDtypeStruct((n, 1, 6 * D_MODEL), F32),
                   jax.ShapeDtypeStruct(w_in.shape, BF16)],
        compiler_params=_params(1),
        name="modulation",
    )(cvecs, w_ada, b_ada.reshape(1, -1), w_in)


def _pick_group(i, tiles_a, a_ref, b_ref):
    return jnp.where(i < tiles_a, a_ref[0], b_ref[0])


def _group_specs(tiles_a, tm, width):
    return [pl.BlockSpec((1, tm, width), lambda i: (jnp.minimum(i, tiles_a - 1), 0, 0)),
            pl.BlockSpec((1, tm, width), lambda i: (jnp.maximum(i - tiles_a, 0), 0, 0))]


def _mod_spec(tiles_a, tiles_per_b):
    return pl.BlockSpec((1, 1, 6 * D_MODEL),
                        lambda i: (jnp.where(i < tiles_a, 0, 1 + (i - tiles_a) // tiles_per_b), 0, 0))


def _mod_part(mod_ref, k):
    return mod_ref[0, :, k * D_MODEL:(k + 1) * D_MODEL]


def _inproj_kernel(xa_ref, xb_ref, mod_ref, g_ref, w_ref, urw_ref, una_ref, gt_ref, *, tiles_a):
    x = _pick_group(pl.program_id(0), tiles_a, xa_ref, xb_ref)
    h = _rms_rows(x) * g_ref[...]
    h = (h * (1.0 + _mod_part(mod_ref, 1)) + _mod_part(mod_ref, 0)).astype(BF16)
    d = lambda lo, hi: jnp.dot(h, w_ref[:, lo:hi], preferred_element_type=F32)
    urw_ref[...] = d(0, RW_COLS)
    una_ref[...] = d(RW_COLS, RW_COLS + NA_IN_COLS)
    gt_ref[...] = d(RW_COLS + NA_IN_COLS, RW_COLS + NA_IN_COLS + GATE_COLS).astype(BF16)


def _in_proj(xa, xb, tiles_per_b, mod_all, norm_g, w_in_bf):
    tm = TOKEN_TILE
    tiles_a, tiles = xa.shape[0], xa.shape[0] + xb.shape[0]
    row = lambda i: (i, 0)
    return pl.pallas_call(
        functools.partial(_inproj_kernel, tiles_a=tiles_a),
        grid=(tiles,),
        in_specs=_group_specs(tiles_a, tm, D_MODEL) + [_mod_spec(tiles_a, tiles_per_b),
                                                       _const_spec((1, D_MODEL)), _const_spec(w_in_bf.shape)],
        out_specs=[pl.BlockSpec((tm, RW_COLS), row),
                   pl.BlockSpec((tm, NA_IN_COLS), row),
                   pl.BlockSpec((tm, GATE_COLS), row)],
        out_shape=[jax.ShapeDtypeStruct((tiles * tm, RW_COLS), F32),
                   jax.ShapeDtypeStruct((tiles * tm, NA_IN_COLS), F32),
                   jax.ShapeDtypeStruct((tiles * tm, GATE_COLS), BF16)],
        compiler_params=_params(1),
        name="in_proj",
    )(xa, xb, mod_all, norm_g.reshape(1, -1), w_in_bf)


def _shift(x, mu):
    t_len = x.shape[0]
    row = lax.broadcasted_iota(jnp.int32, x.shape, 0)
    prev = jnp.where(row == 0, 0.0, pltpu.roll(x, 1, 0))
    nxt = jnp.where(row == t_len - 1, 0.0, pltpu.roll(x, t_len - 1, 0))
    return x + mu[0:1, :] * (prev - x) + mu[1:2, :] * (nxt - x)


def _stack_heads(x, lane_lo):
    return jnp.concatenate([x * lane_lo, x * (1.0 - lane_lo)], axis=0)


def _wkv_intra(units, consts):
    tri, mask_s, mask_i, eye, lane_lo, blk = consts
    stack = lambda z: _stack_heads(z, lane_lo)

    cums = [_dot_exact_lhs(tri[int(u[6])], u[1]) for u in units]

    prep = []
    for (r, lw, kd, v, kk, b, reverse), cum in zip(units, cums):
        mid_row = CHUNK // 2 if reverse else CHUNK // 2 - 1
        tot_row = 0 if reverse else CHUNK - 1
        a = -kk
        ex = cum - lw
        mid = cum[mid_row:mid_row + 1, :]
        tot = cum[tot_row:tot_row + 1, :]
        up = jnp.exp(cum - mid)
        dn = jnp.exp(mid - cum)
        tail = jnp.exp(tot - cum)
        prep.append(dict(
            at_m=stack(a * jnp.exp(ex - mid)).astype(BF16),
            rt_m=stack(r * up).astype(BF16),
            bk_m=jnp.concatenate([b * dn, kd * dn], axis=0).astype(BF16),
            a_e=stack(a * jnp.exp(ex)),
            r_e=stack(r * jnp.exp(cum)),
            bk_t=jnp.concatenate([stack(b * tail), stack(kd * tail)], axis=0).T.astype(BF16),
            vv=stack(v).astype(BF16),
            diag=jnp.where(eye, jnp.exp(tot), 0.0),
            rev=int(reverse)))

    ntd = lambda x, y: lax.dot_general(x, y, (((1,), (1,)), ((), ())), preferred_element_type=F32)
    mm = lambda x, y: jnp.dot(x, y, preferred_element_type=F32)
    diag_blk, swap_eye = blk
    roll_head = lambda z: pltpu.roll(z, HEAD_DIM, 1)
    by_block = lambda z: jnp.concatenate([z[:CHUNK], roll_head(z[CHUNK:])], axis=0)
    top = [by_block(ntd(p["at_m"], p["bk_m"])) for p in prep]
    low = [by_block(ntd(p["rt_m"], p["bk_m"])) for p in prep]
    a_ak = [(roll_head(t) * mask_s[p["rev"]]).astype(BF16) for t, p in zip(top, prep)]
    bot = [jnp.concatenate([t * mask_i[p["rev"]], roll_head(t) * mask_i[p["rev"]]], axis=1).astype(BF16)
           for t, p in zip(low, prep)]

    off_blk = 1.0 - diag_blk
    both = [t * mask_s[p["rev"]] + swap_eye for t, p in zip(top, prep)]
    steps = CHUNK.bit_length() - 1
    diag_bf = diag_blk.astype(BF16)
    for j in range(steps):
        packed = [q.astype(BF16) for q in both]
        res = [mm(qb * diag_bf, qb) for qb in packed]
        both = [r + off_blk * q for r, q in zip(res, both)]
    ts = [pltpu.roll(q, HEAD_DIM, 1).astype(BF16) for q in both]
    x0 = [jnp.concatenate([p["a_e"], mm(ak, p["vv"])], axis=1) for p, ak in zip(prep, a_ak)]
    xs = [mm(t, x.astype(BF16)) for x, t in zip(x0, ts)]

    out = []
    zeros = jnp.zeros((STACK, LANES), BF16)
    for p, x, bt in zip(prep, xs, bot):
        rhs = jnp.concatenate([x.astype(BF16), jnp.concatenate([zeros, p["vv"]], axis=1)], axis=0)
        lhs = jnp.concatenate([bt, p["bk_t"]], axis=0)
        res = mm(lhs, rhs)
        lhs2 = res[:, :LANES] + jnp.concatenate([p["r_e"], p["diag"]], axis=0)
        out.append((lhs2.astype(BF16), res[:, LANES:]))
    return out


def _wkv_constants():
    lane_head = np.arange(LANES) // HEAD_DIM
    ones = (lane_head[:, None] == lane_head[None, :]).astype(np.float32)
    t = np.arange(CHUNK)
    tri = np.stack([t[None, :] <= t[:, None], t[None, :] >= t[:, None]]).astype(np.float32)
    rs, cs = t[:, None], t[None, :]
    tri_masks = np.stack([cs < rs, cs > rs, cs <= rs, cs >= rs]).astype(np.float32)
    masks = np.kron(np.eye(2, dtype=np.float32), tri_masks)
    return jnp.asarray(ones, BF16), jnp.asarray(tri, BF16), jnp.asarray(masks, F32)


def _rwkv_kernel(*refs, t_len, has_s0, pairs, units, n_cast, attend):
    (r_ref, k_ref, v_ref, lo_ref, mur_ref, muk_ref, muv_ref, mul_ref, w0_ref, a0_ref, wup_ref, aup_ref,
     gup_ref, kk_ref, ka_ref, rk_ref, lng_ref, lnb_ref, ones_ref, tri_ref, mask_ref) = refs[:21]
    pos = 21
    s0_ref = None
    if has_s0:
        s0_ref = refs[pos]
        pos += 1
    cast_in = refs[pos:pos + n_cast]
    pos += n_cast
    n_att_in, n_att_out = (5, 3) if attend else (0, 0)
    att_in = refs[pos:pos + n_att_in]
    pos += n_att_in
    o_ref, sn_ref = refs[pos], refs[pos + 1]
    cast_out = refs[pos + 2:pos + 2 + n_cast]
    pos += 2 + n_cast
    att_out = refs[pos:pos + n_att_out]
    (r_s, v_s, kk_s, b0_s, b1_s, lw0_s, lw1_s, kd0_s, kd1_s, gate_s, bonus_s, yf_s, yb_s,
     lhs_s, add_s, st_s) = refs[pos + n_att_out:]
    if attend:
        _na_ctx_kernel(*att_in, *att_out)
    for w_in_ref, w_out_ref in zip(cast_in, cast_out):
        w_out_ref[...] = w_in_ref[...].astype(BF16)

    ones = ones_ref[...]
    lane = lax.broadcasted_iota(jnp.int32, (1, LANES), 1)
    lo_half = lane < HEAD_DIM
    lane_lo = jnp.where(lo_half, 1.0, 0.0)
    mm = lambda x, y: jnp.dot(x, y, preferred_element_type=F32)

    lo = _shift(lo_ref[0], mul_ref[...])
    wd = jnp.tanh(lo[:, 0:LANES])
    ad = lo[:, LANES:2 * LANES]
    sig_gd = _sigmoid(lo[:, 2 * LANES:3 * LANES]).astype(BF16)
    wd_split = [_split2(wd * m) for m in (lane_lo, 1.0 - lane_lo)]
    ad_bf = [(ad * m).astype(BF16) for m in (lane_lo, 1.0 - lane_lo)]
    for j in range(pairs):
        cols = slice(j * LANES, (j + 1) * LANES)
        r = _shift(r_ref[0, :, cols], mur_ref[:, cols])
        k = _shift(k_ref[0, :, cols], muk_ref[:, cols])
        v = _shift(v_ref[0, :, cols], muv_ref[:, cols])
        kk = k * kk_ref[:, cols]
        kk = kk * lax.rsqrt(_head_sum(kk * kk, ones) + L2_EPS)
        wup_h, wup_l = _split2(wup_ref[:, cols])
        aup = aup_ref[:, cols].astype(BF16)
        kdirs = []
        for e, (lw_s, kd_s, b_s) in enumerate(((lw0_s, kd0_s, b0_s), (lw1_s, kd1_s, b1_s))):
            wd_h, wd_l = wd_split[e]
            w_lin = w0_ref[e:e + 1, cols] + (mm(wd_h, wup_h) + mm(wd_l, wup_h) + mm(wd_h, wup_l))
            lw_s[j] = -DECAY_SCALE * _sigmoid(w_lin)
            iclr = _sigmoid(a0_ref[e:e + 1, cols] + mm(ad_bf[e], aup))
            kd = k * (1.0 + (iclr - 1.0) * ka_ref[:, cols])
            kd_s[j] = kd
            b_s[j] = kk * iclr
            kdirs.append(kd)
        gate_s[:, cols] = mm(sig_gd, gup_ref[:, cols].astype(BF16))
        bonus_s[:, cols] = _head_sum(r * (0.5 * (kdirs[0] + kdirs[1])) * rk_ref[:, cols], ones) * v
        r_s[j] = r
        v_s[j] = v
        kk_s[j] = kk

    n_chunks = t_len // CHUNK
    chunks_per = min(n_chunks, units // 2)
    pairs_per = min(pairs, units // (2 * chunks_per))
    groups = n_chunks // chunks_per
    rs = lax.broadcasted_iota(jnp.int32, (STACK, STACK), 0)
    cs = lax.broadcasted_iota(jnp.int32, (STACK, STACK), 1)
    as_f32 = lambda m: jnp.where(m, 1.0, 0.0)
    blk = (as_f32(rs // CHUNK == cs // CHUNK), as_f32(cs == (rs + CHUNK) % STACK))
    consts = ((tri_ref[0], tri_ref[1]), (mask_ref[0], mask_ref[1]), (mask_ref[2], mask_ref[3]),
              rs == cs, lane_lo, blk)
    dirs = ((lw0_s, kd0_s, b0_s), (lw1_s, kd1_s, b1_s))

    def intra_body(it, carry):
        pg = it // groups
        g = it % groups
        units_, ids = [], []
        for jj in range(pairs_per):
            j = pg * pairs_per + jj
            for cc in range(chunks_per):
                c = g * chunks_per + cc
                rows = pl.ds(pl.multiple_of(c * CHUNK, CHUNK), CHUNK)
                for e, (lw_s, kd_s, b_s) in enumerate(dirs):
                    units_.append((r_s[j, rows, :], lw_s[j, rows, :], kd_s[j, rows, :], v_s[j, rows, :],
                                   kk_s[j, rows, :], b_s[j, rows, :], e == 1))
                    ids.append((j * 2 + e) * n_chunks + c)
        for uid, (lhs, add) in zip(ids, _wkv_intra(units_, consts)):
            lhs_s[uid] = lhs
            add_s[uid] = add
        return carry

    lax.fori_loop(0, (pairs // pairs_per) * groups, intra_body, 0, unroll=2)

    for j in range(pairs):
        for e in range(2):
            if has_s0:
                st_s[2 * j + e] = s0_ref[0, e, j].T
            else:
                st_s[2 * j + e] = jnp.zeros((LANES, LANES), F32)

    def state_body(it, carry):
        chunk = (it, n_chunks - 1 - it)
        uids = [(j * 2 + e) * n_chunks + chunk[e] for j in range(pairs) for e in range(2)]
        sts = [st_s[ch].astype(BF16) for ch in range(2 * pairs)]
        res = [mm(lhs_s[uid], st) + add_s[uid] for uid, st in zip(uids, sts)]
        for ch, rr in enumerate(res):
            j, e = divmod(ch, 2)
            y_s = yb_s if e else yf_s
            y_s[j, pl.ds(pl.multiple_of(chunk[e] * CHUNK, CHUNK), CHUNK), :] = rr[:CHUNK] + rr[CHUNK:STACK]
            st_s[ch] = rr[STACK:]
        return carry

    lax.fori_loop(0, n_chunks, state_body, 0, unroll=True)
    for j in range(pairs):
        for e in range(2):
            st_t = st_s[2 * j + e].T
            sn_ref[0, e, 2 * j] = st_t[:HEAD_DIM, :HEAD_DIM]
            sn_ref[0, e, 2 * j + 1] = pltpu.roll(st_t, HEAD_DIM, 1)[HEAD_DIM:, :HEAD_DIM]

    inv_d = 1.0 / HEAD_DIM
    for j in range(pairs):
        cols = slice(j * LANES, (j + 1) * LANES)
        y = yf_s[j] + yb_s[j]
        mean = _head_sum(y, ones) * inv_d
        dlt = y - mean
        var = _head_sum(dlt * dlt, ones) * inv_d
        yn = dlt * lax.rsqrt(var + GN_EPS) * lng_ref[:, cols] + lnb_ref[:, cols]
        o_ref[0, :, cols] = ((yn + bonus_s[:, cols]) * gate_s[:, cols]).astype(o_ref.dtype)


def _rwkv_branch(u_rw, bsz, b_off, s0_big, p, pairs, units, cast=(), attend=None):
    t_len = u_rw.shape[1]
    has_s0 = s0_big is not None
    width = pairs * LANES
    seg = RW_WIDTH // width
    tok = lambda off: pl.BlockSpec((1, t_len, width), lambda b, j: (b_off + b, 0, off + j))
    mu = lambda off: pl.BlockSpec((2, width), lambda b, j: (0, off + j))
    vec2 = pl.BlockSpec((2, width), lambda b, j: (0, j))
    vec1 = pl.BlockSpec((1, width), lambda b, j: (0, j))
    mat = pl.BlockSpec((LANES, width), lambda b, j: (0, j))
    lora_w = 3 * LANES
    lora_blk = 3 * RW_WIDTH // lora_w
    in_specs = [tok(0), tok(seg), tok(2 * seg),
                pl.BlockSpec((1, t_len, lora_w), lambda b, j: (b_off + b, 0, lora_blk)),
                mu(0), mu(seg), mu(2 * seg),
                pl.BlockSpec((2, lora_w), lambda b, j: (0, lora_blk)),
                vec2, vec2, mat, mat, mat, vec1, vec1, vec1, vec1, vec1,
                _const_spec((LANES, LANES)), _const_spec((2, CHUNK, CHUNK)), _const_spec((4, STACK, STACK))]
    args = [u_rw, u_rw, u_rw, u_rw, p["shift_mu"], p["shift_mu"], p["shift_mu"], p["shift_mu"],
            p["rw_w0"], p["rw_a0"],
            p["rw_w_up"].reshape(2 * LORA_DECAY, RW_WIDTH), p["rw_a_up"].reshape(2 * LORA_ICLR, RW_WIDTH),
            p["rw_g_up"], p["rw_k_k"].reshape(1, -1), p["rw_k_a"].reshape(1, -1),
            p["rw_r_k"].reshape(1, -1), p["rw_ln_g"].reshape(1, -1), p["rw_ln_b"].reshape(1, -1),
            *_wkv_constants()]
    st_spec = pl.BlockSpec((1, 2, pairs, LANES, LANES), lambda b, j: (b, 0, j, 0, 0))
    if has_s0:
        in_specs.append(st_spec)
        args.append(s0_big)
    out_specs = [pl.BlockSpec((1, t_len, width), lambda b, j: (b, 0, j)),
                 pl.BlockSpec((1, 2, 2 * pairs, HEAD_DIM, HEAD_DIM), lambda b, j: (b, 0, j, 0, 0))]
    out_shape = [jax.ShapeDtypeStruct((bsz, t_len, RW_WIDTH), BF16),
                 jax.ShapeDtypeStruct((bsz, 2, RW_HEADS, HEAD_DIM, HEAD_DIM), F32)]
    for w in cast:
        assert PAIRS == pairs and w.shape[0] % (16 * bsz) == 0
        blk = pl.BlockSpec((w.shape[0] // bsz, w.shape[1]), lambda b, j: (b, 0))
        in_specs.append(blk)
        args.append(w)
        out_specs.append(blk)
        out_shape.append(jax.ShapeDtypeStruct(w.shape, BF16))
    if attend is not None:
        u_na, q_g, k_g = attend
        assert PAIRS == pairs
        na_tok = lambda seg_: pl.BlockSpec((1, t_len, NA_WIDTH), lambda b, j: (b_off + b, 0, seg_))
        na_out = pl.BlockSpec((1, t_len, NA_WIDTH), lambda b, j: (b, 0, 0))
        g2 = lambda g: jnp.tile(g.reshape(1, HEAD_DIM), (1, 2))
        in_specs += [na_tok(0), na_tok(1), na_tok(2), _const_spec((1, LANES)), _const_spec((1, LANES))]
        args += [u_na, u_na, u_na, g2(q_g), g2(k_g)]
        out_specs += [na_out, na_out, na_out]
        na_shape = (bsz, t_len, NA_WIDTH)
        out_shape += [jax.ShapeDtypeStruct(na_shape, BF16), jax.ShapeDtypeStruct(na_shape, F32),
                      jax.ShapeDtypeStruct(na_shape, F32)]
    n_units = 2 * pairs * (t_len // CHUNK)
    per_pair = pltpu.VMEM((pairs, t_len, LANES), F32)
    full = pltpu.VMEM((t_len, width), F32)
    scratch = [per_pair] * 9 + [full, full, per_pair, per_pair,
                                pltpu.VMEM((n_units, 2 * STACK, LANES), BF16),
                                pltpu.VMEM((n_units, 2 * STACK, LANES), F32),
                                pltpu.VMEM((2 * pairs, LANES, LANES), F32)]
    o_rw, s_new, *rest = pl.pallas_call(
        functools.partial(_rwkv_kernel, t_len=t_len, has_s0=has_s0, pairs=pairs, units=units, n_cast=len(cast),
                          attend=attend is not None),
        grid=(bsz, PAIRS // pairs),
        in_specs=in_specs,
        out_specs=out_specs,
        out_shape=out_shape,
        scratch_shapes=scratch,
        compiler_params=_params(2),
        name="rwkv_branch",
    )(*args)
    return o_rw, s_new, rest[:len(cast)], rest[len(cast):]


def _state_to_big(s0):
    bsz = s0.shape[0]
    x = s0.reshape(bsz, 2, PAIRS, 2, HEAD_DIM, 1, HEAD_DIM)
    same_head = np.eye(2, dtype=bool).reshape(2, 1, 2, 1)
    return jnp.where(same_head, x, 0.0).reshape(bsz, 2, PAIRS, LANES, LANES)


def _qk_norm(t, g, ones):
    ms = _head_sum(t * t, ones) * (1.0 / HEAD_DIM)
    return t * lax.rsqrt(ms + RMS_EPS) * g


def _nt(x, y):
    return lax.dot_general(x, y, (((1,), (1,)), ((), ())), preferred_element_type=F32)


def _na_ctx_kernel(q_ref, k_ref, v_ref, qg_ref, kg_ref, o_ref, kn_ref, vc_ref):
    ones = _head_ones()
    lo_half = lax.broadcasted_iota(jnp.int32, (1, LANES), 1) < HEAD_DIM
    lo = jnp.where(lo_half, 1.0, 0.0)
    t_len = q_ref.shape[1]
    qs, ks, vs = [], [], []
    for j in range(NA_WIDTH // LANES):
        cols = slice(j * LANES, (j + 1) * LANES)
        qn = _qk_norm(q_ref[0, :, cols], qg_ref[...], ones)
        kn = _qk_norm(k_ref[0, :, cols], kg_ref[...], ones)
        v = v_ref[0, :, cols]
        kn_ref[0, :, cols] = kn
        vc_ref[0, :, cols] = v
        qn = qn * QK_SCALE
        qs.append(jnp.concatenate([qn * lo, qn * (1.0 - lo)], axis=0).astype(BF16))
        ks.append(kn.astype(BF16))
        vs.append(v.astype(BF16))
    logits = [_nt(q, k) for q, k in zip(qs, ks)]
    ms = [jnp.max(s, axis=-1, keepdims=True) for s in logits]
    ps = [jnp.exp(s - m) for s, m in zip(logits, ms)]
    ls = [jnp.sum(p, axis=-1, keepdims=True) for p in ps]
    outs = [jnp.dot(p.astype(BF16), v, preferred_element_type=F32) / l for p, v, l in zip(ps, vs, ls)]
    for j, o in enumerate(outs):
        o_ref[0, :, j * LANES:(j + 1) * LANES] = jnp.where(lo_half, o[:t_len], o[t_len:]).astype(o_ref.dtype)


NA_ROW_ILP = 4


def _na_lat_kernel(q_ref, k_ref, v_ref, kc_ref, vc_ref, qg_ref, kg_ref, diag_ref, o_ref,
                   q0_s, q1_s, kn_s, v_s, kc_s, vc_s, tab_ref, *, rows, kr):
    ones = _head_ones()
    lo_half = lax.broadcasted_iota(jnp.int32, (1, LANES), 1) < HEAD_DIM
    lo = jnp.where(lo_half, 1.0, 0.0)
    q_col = lax.broadcasted_iota(jnp.int32, (GRID_W, 2 * GRID_W), 0)
    k_col = lax.broadcasted_iota(jnp.int32, (GRID_W, 2 * GRID_W), 1) % GRID_W
    w_start = jnp.clip(q_col - NA_COLS // 2, 0, GRID_W - NA_COLS)
    valid = (k_col >= w_start) & (k_col < w_start + NA_COLS)
    for h in range(2):
        for d in range(2 * NA_ROWS - 2):
            diag = jnp.broadcast_to(diag_ref[h, d:d + 1, :], (GRID_W, 2 * GRID_W))
            tab_ref[h, d] = jnp.where(valid, pltpu.roll(diag, 0, 1, stride=1, stride_axis=0), NEG_INF)
    qn = _qk_norm(q_ref[0], qg_ref[...], ones) * QK_SCALE
    q0_s[...] = (qn * lo).astype(BF16)
    q1_s[...] = (qn * (1.0 - lo)).astype(BF16)
    kn_s[...] = _qk_norm(k_ref[0], kg_ref[...], ones).astype(BF16)
    v_s[...] = v_ref[0].astype(BF16)
    kc_s[...] = kc_ref[0].astype(BF16)
    vc_s[...] = vc_ref[0].astype(BF16)
    win = kr * GRID_W

    def body(it, carry):
        qs, k_rows, q_rows, biases = [], [], [], []
        for s in range(NA_ROW_ILP):
            i = it * NA_ROW_ILP + s
            r0 = jnp.clip(i - kr // 2, 0, rows - kr)
            d0 = r0 - i + (NA_ROWS - 1)
            qr = pl.ds(pl.multiple_of(i * GRID_W, GRID_W), GRID_W)
            q_rows.append(qr)
            k_rows.append(pl.ds(pl.multiple_of(r0 * GRID_W, GRID_W), win))
            qs.append(jnp.concatenate([q0_s[qr, :], q1_s[qr, :]], axis=0))
            biases.append(jnp.concatenate(
                [jnp.concatenate([tab_ref[h, d0 + 2 * m] for m in range(kr // 2)], axis=1) for h in range(2)],
                axis=0))
        lw = [_nt(q, kn_s[kr_, :]) + b for q, kr_, b in zip(qs, k_rows, biases)]
        lc = [_nt(q, kc_s[...]) for q in qs]
        ms = [jnp.maximum(jnp.max(a, axis=-1, keepdims=True), jnp.max(c, axis=-1, keepdims=True))
              for a, c in zip(lw, lc)]
        pw = [jnp.exp(a - m) for a, m in zip(lw, ms)]
        pc = [jnp.exp(c - m) for c, m in zip(lc, ms)]
        ls = [jnp.sum(a, axis=-1, keepdims=True) + jnp.sum(c, axis=-1, keepdims=True) for a, c in zip(pw, pc)]
        outs = [(jnp.dot(a.astype(BF16), v_s[kr_, :], preferred_element_type=F32)
                 + jnp.dot(c.astype(BF16), vc_s[...], preferred_element_type=F32)) / l
                for a, c, kr_, l in zip(pw, pc, k_rows, ls)]
        for qr, o in zip(q_rows, outs):
            o_ref[0, qr, :] = jnp.where(lo_half, o[:GRID_W], o[GRID_W:]).astype(o_ref.dtype)
        return carry

    lax.fori_loop(0, rows // NA_ROW_ILP, body, 0, unroll=True)


def _latent_bias_diagonals(rpb):
    c1 = NA_COLS - 1
    assert 2 * c1 + 1 <= GRID_W - c1
    a, b = rpb[:, :-1], rpb[:, 1:]
    gap = jnp.zeros(a.shape[:2] + (GRID_W - 2 * c1 - 1,), rpb.dtype)
    return jnp.concatenate([a[..., c1:], gap, b, gap, a[..., :c1]], axis=-1)


def _na_latent(u_na, bsz, b_off, k_ctx, v_ctx, q_g, k_g, rpb):
    t_len = u_na.shape[1]
    rows = t_len // GRID_W
    kr = min(NA_ROWS, rows)
    assert kr % 2 == 0 and rows % NA_ROW_ILP == 0
    ctx_len = k_ctx.shape[1]
    seg = NA_WIDTH // LANES
    tok = lambda off: pl.BlockSpec((1, t_len, LANES), lambda b, j: (b_off + b, 0, off + j))
    ctx = pl.BlockSpec((1, ctx_len, LANES), lambda b, j: (b, 0, j))
    g2 = lambda g: jnp.tile(g.reshape(1, HEAD_DIM), (1, 2))
    diags = _latent_bias_diagonals(rpb)
    tok_s = pltpu.VMEM((t_len, LANES), BF16)
    ctx_s = pltpu.VMEM((ctx_len, LANES), BF16)
    return pl.pallas_call(
        functools.partial(_na_lat_kernel, rows=rows, kr=kr),
        grid=(bsz, seg),
        in_specs=[tok(0), tok(seg), tok(2 * seg), ctx, ctx,
                  _const_spec((1, LANES)), _const_spec((1, LANES)),
                  pl.BlockSpec((2, 2 * NA_ROWS - 2, 2 * GRID_W), lambda b, j: (j, 0, 0))],
        out_specs=pl.BlockSpec((1, t_len, LANES), lambda b, j: (b, 0, j)),
        out_shape=jax.ShapeDtypeStruct((bsz, t_len, NA_WIDTH), BF16),
        scratch_shapes=[tok_s, tok_s, tok_s, tok_s, ctx_s, ctx_s,
                        pltpu.VMEM((2, 2 * NA_ROWS - 2, GRID_W, 2 * GRID_W), F32)],
        compiler_params=_params(2),
        name="na_latent",
    )(u_na, u_na, u_na, k_ctx, v_ctx, g2(q_g), g2(k_g), diags)


LATE_WEIGHTS = ("w_o_rwkv", "w_o_na", "w_out", "ffn_w1", "ffn_w3", "ffn_w2")


def _out_ffn_kernel(xa_ref, xb_ref, orwa_ref, orwb_ref, onaa_ref, onab_ref, gt_ref, mod_ref, g_ref,
                    wor_ref, won_ref, wout_ref, w1_ref, w3_ref, w2_ref, ya_ref, yb_ref, *, tiles_a):
    i = pl.program_id(0)
    x = _pick_group(i, tiles_a, xa_ref, xb_ref)
    o_rw = _pick_group(i, tiles_a, orwa_ref, orwb_ref)
    o_na = _pick_group(i, tiles_a, onaa_ref, onab_ref)
    g_rw = _sigmoid(gt_ref[:, :D_MODEL].astype(F32))
    g_na = _sigmoid(gt_ref[:, D_MODEL:].astype(F32))
    merged = g_rw * _dot(o_rw, wor_ref[...]) + g_na * _dot(o_na, won_ref[...])
    x1 = x + _mod_part(mod_ref, 2) * _dot(merged, wout_ref[...])
    h2 = _rms_rows(x1) * g_ref[...]
    h2 = (h2 * (1.0 + _mod_part(mod_ref, 4)) + _mod_part(mod_ref, 3)).astype(BF16)
    acc = jnp.zeros(x1.shape, F32)
    for c in range(FF_HIDDEN // FF_CHUNK):
        cols = slice(c * FF_CHUNK, (c + 1) * FF_CHUNK)
        a = jnp.dot(h2, w1_ref[:, cols], preferred_element_type=F32)
        b = jnp.dot(h2, w3_ref[:, cols], preferred_element_type=F32)
        hh = (a * _sigmoid(a) * b).astype(BF16)
        acc = acc + jnp.dot(hh, w2_ref[cols, :], preferred_element_type=F32)
    y = x1 + _mod_part(mod_ref, 5) * acc

    @pl.when(i < tiles_a)
    def _():
        ya_ref[0] = y

    @pl.when(i >= tiles_a)
    def _():
        yb_ref[0] = y


def _out_ffn(xa, xb, orw_a, orw_b, ona_a, ona_b, gates, tiles_per_b, mod_all, norm_g, wb):
    tm = TOKEN_TILE
    tiles_a, tiles = xa.shape[0], xa.shape[0] + xb.shape[0]
    weights = [wb[n] for n in LATE_WEIGHTS]
    groups = lambda width: _group_specs(tiles_a, tm, width)
    return pl.pallas_call(
        functools.partial(_out_ffn_kernel, tiles_a=tiles_a),
        grid=(tiles,),
        in_specs=groups(D_MODEL) + groups(RW_WIDTH) + groups(NA_WIDTH)
        + [pl.BlockSpec((tm, GATE_COLS), lambda i: (i, 0)), _mod_spec(tiles_a, tiles_per_b),
           _const_spec((1, D_MODEL))] + [_const_spec(w.shape) for w in weights],
        out_specs=groups(D_MODEL),
        out_shape=[jax.ShapeDtypeStruct(xa.shape, F32), jax.ShapeDtypeStruct(xb.shape, F32)],
        compiler_params=_params(1),
        name="out_ffn",
    )(xa, xb, orw_a, orw_b, ona_a, ona_b, gates, mod_all, norm_g.reshape(1, -1), *weights)


def kernel(x_prompt, x_sample, state_rwkv, cache_na_k, cache_na_v, c, c_ctx, norm1_g, norm2_g, w_ada, b_ada,
           w_in, shift_mu, rw_w0, rw_w_up, rw_a0, rw_a_up, rw_g_up, rw_k_k, rw_k_a, rw_r_k, rw_ln_g, rw_ln_b,
           na_q_g, na_k_g, na_rpb, w_o_rwkv, w_o_na, w_out, ffn_w1, ffn_w3, ffn_w2):
    depth = w_in.shape[0]
    bsz, seq = x_prompt.shape[:2]
    dec, dec_seq = x_sample.shape[:2]
    tm = TOKEN_TILE
    n_ctx = bsz * seq
    assert n_ctx % tm == 0 and dec_seq % tm == 0 and n_ctx % dec_seq == 0
    tiles_per_b = dec_seq // tm
    tiled = lambda t: t.reshape(-1, tm, t.shape[-1])
    cvecs = jnp.concatenate([c_ctx[None, :], c], axis=0).T
    y_p, y_s = x_prompt, x_sample
    new_s, new_k, new_v = [], [], []
    for l in range(depth):
        p = dict(norm1_g=norm1_g[l], norm2_g=norm2_g[l], shift_mu=shift_mu[l], rw_w0=rw_w0[l],
                 rw_w_up=rw_w_up[l], rw_a0=rw_a0[l], rw_a_up=rw_a_up[l], rw_g_up=rw_g_up[l],
                 rw_k_k=rw_k_k[l], rw_k_a=rw_k_a[l], rw_r_k=rw_r_k[l], rw_ln_g=rw_ln_g[l],
                 rw_ln_b=rw_ln_b[l], na_q_g=na_q_g[l], na_k_g=na_k_g[l], na_rpb=na_rpb[l])
        late_f32 = [w[l] for w in (w_o_rwkv, w_o_na, w_out, ffn_w1, ffn_w3, ffn_w2)]
        mod_all, w_in_bf = _modulation(cvecs, w_ada[l], b_ada[l], w_in[l])
        u_rw, u_na, gates = _in_proj(tiled(y_p), tiled(y_s), tiles_per_b, mod_all, p["norm1_g"], w_in_bf)
        ctx_view = lambda t: t.reshape(-1, seq, t.shape[-1])
        lat_view = lambda t: t.reshape(-1, dec_seq, t.shape[-1])
        lat_off = n_ctx // dec_seq
        o_rw_p, s_l, casted, (o_na_p, k_l, v_l) = _rwkv_branch(
            ctx_view(u_rw), bsz, 0, None, p, RWKV_PAIRS_CTX, RWKV_UNITS, late_f32,
            attend=(ctx_view(u_na), p["na_q_g"], p["na_k_g"]))
        late = dict(zip(LATE_WEIGHTS, casted))
        new_s.append(s_l)
        new_k.append(k_l.reshape(bsz, seq, NA_HEADS, HEAD_DIM))
        new_v.append(v_l.reshape(bsz, seq, NA_HEADS, HEAD_DIM))
        ctx_k = cache_na_k[:, l].reshape(dec, -1, NA_WIDTH)
        ctx_v = cache_na_v[:, l].reshape(dec, -1, NA_WIDTH)
        o_rw_s = _rwkv_branch(lat_view(u_rw), dec, lat_off, _state_to_big(state_rwkv[:, l]), p, RWKV_PAIRS_LAT,
                               RWKV_UNITS)[0]
        o_na_s = _na_latent(lat_view(u_na), dec, lat_off, ctx_k, ctx_v, p["na_q_g"], p["na_k_g"], p["na_rpb"])
        y_p_t, y_s_t = _out_ffn(tiled(y_p), tiled(y_s), tiled(o_rw_p), tiled(o_rw_s), tiled(o_na_p), tiled(o_na_s),
                                gates, tiles_per_b, mod_all, p["norm2_g"], late)
        y_p, y_s = y_p_t.reshape(x_prompt.shape), y_s_t.reshape(x_sample.shape)
    return (y_p, y_s, jnp.stack(new_s, axis=1), jnp.stack(new_k, axis=1), jnp.stack(new_v, axis=1))
```

```python
import functools

import numpy as np
import jax
import jax.numpy as jnp
from jax import lax
from jax.experimental import pallas as pl
from jax.experimental.pallas import tpu as pltpu

D_MODEL = 1024
GRID_W = 64
HEAD_DIM = 64
RW_HEADS = 8
RW_WIDTH = RW_HEADS * HEAD_DIM
NA_HEADS = 8
NA_WIDTH = NA_HEADS * HEAD_DIM
LORA_DECAY = 64
LORA_ICLR = 64
LORA_GATE = 128
NA_ROWS = 8
NA_COLS = 16
FF_HIDDEN = 2816
RW_COLS = 3 * RW_WIDTH + 2 * LORA_DECAY + 2 * LORA_ICLR + LORA_GATE
NA_IN_COLS = 3 * NA_WIDTH
GATE_COLS = 2 * D_MODEL
RMS_EPS = 1e-6
GN_EPS = 64e-5
L2_EPS = 1e-12
NEG_INF = -1e30
DECAY_SCALE = float(np.exp(-0.5))
QK_SCALE = HEAD_DIM ** -0.5
assert QK_SCALE == 0.125

LANES = 128
PAIRS = RW_HEADS // 2
CHUNK = 64
STACK = 2 * CHUNK
RWKV_PAIRS_CTX = 4
RWKV_PAIRS_LAT = 2
RWKV_UNITS = 16
TOKEN_TILE = 512
FF_CHUNK = 256
VMEM_LIMIT = 56 * 1024 * 1024

F32 = jnp.float32
BF16 = jnp.bfloat16


def _dot(a, b):
    return jnp.dot(a.astype(BF16), b.astype(BF16), preferred_element_type=F32)


def _split2(x):
    hi = x.astype(BF16)
    lo = (x - hi.astype(F32)).astype(BF16)
    return hi, lo


def _dot_exact_lhs(a_exact, b):
    h, l = _split2(b)
    d = lambda x: jnp.dot(a_exact, x, preferred_element_type=F32)
    return d(h) + d(l)


def _head_ones():
    r = lax.broadcasted_iota(jnp.int32, (LANES, LANES), 0) // HEAD_DIM
    c = lax.broadcasted_iota(jnp.int32, (LANES, LANES), 1) // HEAD_DIM
    return jnp.where(r == c, 1.0, 0.0).astype(BF16)


def _head_sum(x, ones):
    return jnp.dot(x.astype(BF16), ones, preferred_element_type=F32)


def _sigmoid(x):
    return 0.5 * jnp.tanh(0.5 * x) + 0.5


def _rms_rows(x):
    return x * lax.rsqrt(jnp.mean(x * x, axis=-1, keepdims=True) + RMS_EPS)


def _const_spec(shape):
    nd = len(shape)
    return pl.BlockSpec(shape, lambda *_: (0,) * nd, pipeline_mode=pl.Buffered(1))


def _params(n_axes):
    return pltpu.CompilerParams(dimension_semantics=("arbitrary",) * n_axes,
                                vmem_limit_bytes=VMEM_LIMIT)


def _mod_kernel(c_ref, w_ref, b_ref, win_ref, o_ref, winb_ref):
    s = c_ref[...]
    s = s * _sigmoid(s)
    w = w_ref[...]
    for r in range(s.shape[1]):
        o_ref[r] = jnp.sum(w * s[:, r:r + 1], axis=0, keepdims=True) + b_ref[...]
    winb_ref[...] = win_ref[...].astype(BF16)


def _modulation(cvecs, w_ada, b_ada, w_in):
    n = cvecs.shape[1]
    tn = 768
    steps = 6 * D_MODEL // tn
    rows = w_in.shape[0] // steps
    assert rows % 16 == 0
    return pl.pallas_call(
        _mod_kernel,
        grid=(steps,),
        in_specs=[pl.BlockSpec((D_MODEL, n), lambda j: (0, 0)),
                  pl.BlockSpec((D_MODEL, tn), lambda j: (0, j)),
                  pl.BlockSpec((1, tn), lambda j: (0, j)),
                  pl.BlockSpec((rows, w_in.shape[1]), lambda j: (j, 0))],
        out_specs=[pl.BlockSpec((n, 1, tn), lambda j: (0, 0, j)),
                   pl.BlockSpec((rows, w_in.shape[1]), lambda j: (j, 0))],
        out_shape=[jax.ShapeDtypeStruct((n, 1, 6 * D_MODEL), F32),
                   jax.ShapeDtypeStruct(w_in.shape, BF16)],
        compiler_params=_params(1),
        name="modulation",
    )(cvecs, w_ada, b_ada.reshape(1, -1), w_in)


def _pick_group(i, tiles_a, a_ref, b_ref):
    return jnp.where(i < tiles_a, a_ref[0], b_ref[0])


def _group_specs(tiles_a, tm, width):
    return [pl.BlockSpec((1, tm, width), lambda i: (jnp.minimum(i, tiles_a - 1), 0, 0)),
            pl.BlockSpec((1, tm, width), lambda i: (jnp.maximum(i - tiles_a, 0), 0, 0))]


def _mod_spec(tiles_a, tiles_per_b):
    return pl.BlockSpec((1, 1, 6 * D_MODEL),
                        lambda i: (jnp.where(i < tiles_a, 0, 1 + (i - tiles_a) // tiles_per_b), 0, 0))


def _mod_part(mod_ref, k):
    return mod_ref[0, :, k * D_MODEL:(k + 1) * D_MODEL]


def _inproj_kernel(xa_ref, xb_ref, mod_ref, g_ref, w_ref, urw_ref, una_ref, gt_ref, *, tiles_a):
    x = _pick_group(pl.program_id(0), tiles_a, xa_ref, xb_ref)
    h = _rms_rows(x) * g_ref[...]
    h = (h * (1.0 + _mod_part(mod_ref, 1)) + _mod_part(mod_ref, 0)).astype(BF16)
    d = lambda lo, hi: jnp.dot(h, w_ref[:, lo:hi], preferred_element_type=F32)
    urw_ref[...] = d(0, RW_COLS)
    una_ref[...] = d(RW_COLS, RW_COLS + NA_IN_COLS)
    gt_ref[...] = d(RW_COLS + NA_IN_COLS, RW_COLS + NA_IN_COLS + GATE_COLS).astype(BF16)


def _in_proj(xa, xb, tiles_per_b, mod_all, norm_g, w_in_bf):
    tm = TOKEN_TILE
    tiles_a, tiles = xa.shape[0], xa.shape[0] + xb.shape[0]
    row = lambda i: (i, 0)
    return pl.pallas_call(
        functools.partial(_inproj_kernel, tiles_a=tiles_a),
        grid=(tiles,),
        in_specs=_group_specs(tiles_a, tm, D_MODEL) + [_mod_spec(tiles_a, tiles_per_b),
                                                       _const_spec((1, D_MODEL)), _const_spec(w_in_bf.shape)],
        out_specs=[pl.BlockSpec((tm, RW_COLS), row),
                   pl.BlockSpec((tm, NA_IN_COLS), row),
                   pl.BlockSpec((tm, GATE_COLS), row)],
        out_shape=[jax.ShapeDtypeStruct((tiles * tm, RW_COLS), F32),
                   jax.ShapeDtypeStruct((tiles * tm, NA_IN_COLS), F32),
                   jax.ShapeDtypeStruct((tiles * tm, GATE_COLS), BF16)],
        compiler_params=_params(1),
        name="in_proj",
    )(xa, xb, mod_all, norm_g.reshape(1, -1), w_in_bf)


def _shift(x, mu):
    t_len = x.shape[0]
    row = lax.broadcasted_iota(jnp.int32, x.shape, 0)
    prev = jnp.where(row == 0, 0.0, pltpu.roll(x, 1, 0))
    nxt = jnp.where(row == t_len - 1, 0.0, pltpu.roll(x, t_len - 1, 0))
    return x + mu[0:1, :] * (prev - x) + mu[1:2, :] * (nxt - x)


def _stack_heads(x, lane_lo):
    return jnp.concatenate([x * lane_lo, x * (1.0 - lane_lo)], axis=0)


def _wkv_intra(units, consts):
    tri, mask_s, mask_i, eye, lane_lo, blk = consts
    stack = lambda z: _stack_heads(z, lane_lo)

    cums = [_dot_exact_lhs(tri[int(u[6])], u[1]) for u in units]

    prep = []
    for (r, lw, kd, v, kk, b, reverse), cum in zip(units, cums):
        mid_row = CHUNK // 2 if reverse else CHUNK // 2 - 1
        tot_row = 0 if reverse else CHUNK - 1
        a = -kk
        ex = cum - lw
        mid = cum[mid_row:mid_row + 1, :]
        tot = cum[tot_row:tot_row + 1, :]
        up = jnp.exp(cum - mid)
        dn = jnp.exp(mid - cum)
        tail = jnp.exp(tot - cum)
        prep.append(dict(
            at_m=stack(a * jnp.exp(ex - mid)).astype(BF16),
            rt_m=stack(r * up).astype(BF16),
            bk_m=jnp.concatenate([b * dn, kd * dn], axis=0).astype(BF16),
            a_e=stack(a * jnp.exp(ex)),
            r_e=stack(r * jnp.exp(cum)),
            bk_t=jnp.concatenate([stack(b * tail), stack(kd * tail)], axis=0).T.astype(BF16),
            vv=stack(v).astype(BF16),
            diag=jnp.where(eye, jnp.exp(tot), 0.0),
            rev=int(reverse)))

    ntd = lambda x, y: lax.dot_general(x, y, (((1,), (1,)), ((), ())), preferred_element_type=F32)
    mm = lambda x, y: jnp.dot(x, y, preferred_element_type=F32)
    diag_blk, swap_eye = blk
    roll_head = lambda z: pltpu.roll(z, HEAD_DIM, 1)
    by_block = lambda z: jnp.concatenate([z[:CHUNK], roll_head(z[CHUNK:])], axis=0)
    top = [by_block(ntd(p["at_m"], p["bk_m"])) for p in prep]
    low = [by_block(ntd(p["rt_m"], p["bk_m"])) for p in prep]
    a_ak = [(roll_head(t) * mask_s[p["rev"]]).astype(BF16) for t, p in zip(top, prep)]
    bot = [jnp.concatenate([t * mask_i[p["rev"]], roll_head(t) * mask_i[p["rev"]]], axis=1).astype(BF16)
           for t, p in zip(low, prep)]

    off_blk = 1.0 - diag_blk
    both = [t * mask_s[p["rev"]] + swap_eye for t, p in zip(top, prep)]
    steps = CHUNK.bit_length() - 1
    diag_bf = diag_blk.astype(BF16)
    for j in range(steps):
        packed = [q.astype(BF16) for q in both]
        res = [mm(qb * diag_bf, qb) for qb in packed]
        both = [r + off_blk * q for r, q in zip(res, both)]
    ts = [pltpu.roll(q, HEAD_DIM, 1).astype(BF16) for q in both]
    x0 = [jnp.concatenate([p["a_e"], mm(ak, p["vv"])], axis=1) for p, ak in zip(prep, a_ak)]
    xs = [mm(t, x.astype(BF16)) for x, t in zip(x0, ts)]

    out = []
    zeros = jnp.zeros((STACK, LANES), BF16)
    for p, x, bt in zip(prep, xs, bot):
        rhs = jnp.concatenate([x.astype(BF16), jnp.concatenate([zeros, p["vv"]], axis=1)], axis=0)
        lhs = jnp.concatenate([bt, p["bk_t"]], axis=0)
        res = mm(lhs, rhs)
        lhs2 = res[:, :LANES] + jnp.concatenate([p["r_e"], p["diag"]], axis=0)
        out.append((lhs2.astype(BF16), res[:, LANES:]))
    return out


def _wkv_constants():
    lane_head = np.arange(LANES) // HEAD_DIM
    ones = (lane_head[:, None] == lane_head[None, :]).astype(np.float32)
    t = np.arange(CHUNK)
    tri = np.stack([t[None, :] <= t[:, None], t[None, :] >= t[:, None]]).astype(np.float32)
    rs, cs = t[:, None], t[None, :]
    tri_masks = np.stack([cs < rs, cs > rs, cs <= rs, cs >= rs]).astype(np.float32)
    masks = np.kron(np.eye(2, dtype=np.float32), tri_masks)
    return jnp.asarray(ones, BF16), jnp.asarray(tri, BF16), jnp.asarray(masks, F32)


def _rwkv_kernel(*refs, t_len, has_s0, pairs, units, n_cast, attend):
    (r_ref, k_ref, v_ref, lo_ref, mur_ref, muk_ref, muv_ref, mul_ref, w0_ref, a0_ref, wup_ref, aup_ref,
     gup_ref, kk_ref, ka_ref, rk_ref, lng_ref, lnb_ref, ones_ref, tri_ref, mask_ref) = refs[:21]
    pos = 21
    s0_ref = None
    if has_s0:
        s0_ref = refs[pos]
        pos += 1
    cast_in = refs[pos:pos + n_cast]
    pos += n_cast
    n_att_in, n_att_out = (5, 3) if attend else (0, 0)
    att_in = refs[pos:pos + n_att_in]
    pos += n_att_in
    o_ref, sn_ref = refs[pos], refs[pos + 1]
    cast_out = refs[pos + 2:pos + 2 + n_cast]
    pos += 2 + n_cast
    att_out = refs[pos:pos + n_att_out]
    (r_s, v_s, kk_s, b0_s, b1_s, lw0_s, lw1_s, kd0_s, kd1_s, gate_s, bonus_s, yf_s, yb_s,
     lhs_s, add_s, st_s) = refs[pos + n_att_out:]
    if attend:
        _na_ctx_kernel(*att_in, *att_out)
    for w_in_ref, w_out_ref in zip(cast_in, cast_out):
        w_out_ref[...] = w_in_ref[...].astype(BF16)

    ones = ones_ref[...]
    lane = lax.broadcasted_iota(jnp.int32, (1, LANES), 1)
    lo_half = lane < HEAD_DIM
    lane_lo = jnp.where(lo_half, 1.0, 0.0)
    mm = lambda x, y: jnp.dot(x, y, preferred_element_type=F32)

    lo = _shift(lo_ref[0], mul_ref[...])
    wd = jnp.tanh(lo[:, 0:LANES])
    ad = lo[:, LANES:2 * LANES]
    sig_gd = _sigmoid(lo[:, 2 * LANES:3 * LANES]).astype(BF16)
    wd_split = [_split2(wd * m) for m in (lane_lo, 1.0 - lane_lo)]
    ad_bf = [(ad * m).astype(BF16) for m in (lane_lo, 1.0 - lane_lo)]
    for j in range(pairs):
        cols = slice(j * LANES, (j + 1) * LANES)
        r = _shift(r_ref[0, :, cols], mur_ref[:, cols])
        k = _shift(k_ref[0, :, cols], muk_ref[:, cols])
        v = _shift(v_ref[0, :, cols], muv_ref[:, cols])
        kk = k * kk_ref[:, cols]
        kk = kk * lax.rsqrt(_head_sum(kk * kk, ones) + L2_EPS)
        wup_h, wup_l = _split2(wup_ref[:, cols])
        aup = aup_ref[:, cols].astype(BF16)
        kdirs = []
        for e, (lw_s, kd_s, b_s) in enumerate(((lw0_s, kd0_s, b0_s), (lw1_s, kd1_s, b1_s))):
            wd_h, wd_l = wd_split[e]
            w_lin = w0_ref[e:e + 1, cols] + (mm(wd_h, wup_h) + mm(wd_l, wup_h) + mm(wd_h, wup_l))
            lw_s[j] = -DECAY_SCALE * _sigmoid(w_lin)
            iclr = _sigmoid(a0_ref[e:e + 1, cols] + mm(ad_bf[e], aup))
            kd = k * (1.0 + (iclr - 1.0) * ka_ref[:, cols])
            kd_s[j] = kd
            b_s[j] = kk * iclr
            kdirs.append(kd)
        gate_s[:, cols] = mm(sig_gd, gup_ref[:, cols].astype(BF16))
        h0 = 2 * (pl.program_id(1) * pairs + j)
        rk = jnp.concatenate([rk_ref[pl.ds(h0, 1), :], rk_ref[pl.ds(h0 + 1, 1), :]], axis=1)
        bonus_s[:, cols] = _head_sum(r * (0.5 * (kdirs[0] + kdirs[1])) * rk, ones) * v
        r_s[j] = r
        v_s[j] = v
        kk_s[j] = kk

    n_chunks = t_len // CHUNK
    chunks_per = min(n_chunks, units // 2)
    pairs_per = min(pairs, units // (2 * chunks_per))
    groups = n_chunks // chunks_per
    rs = lax.broadcasted_iota(jnp.int32, (STACK, STACK), 0)
    cs = lax.broadcasted_iota(jnp.int32, (STACK, STACK), 1)
    as_f32 = lambda m: jnp.where(m, 1.0, 0.0)
    blk = (as_f32(rs // CHUNK == cs // CHUNK), as_f32(cs == (rs + CHUNK) % STACK))
    consts = ((tri_ref[0], tri_ref[1]), (mask_ref[0], mask_ref[1]), (mask_ref[2], mask_ref[3]),
              rs == cs, lane_lo, blk)
    dirs = ((lw0_s, kd0_s, b0_s), (lw1_s, kd1_s, b1_s))

    def intra_body(it, carry):
        pg = it // groups
        g = it % groups
        units_, ids = [], []
        for jj in range(pairs_per):
            j = pg * pairs_per + jj
            for cc in range(chunks_per):
                c = g * chunks_per + cc
                rows = pl.ds(pl.multiple_of(c * CHUNK, CHUNK), CHUNK)
                for e, (lw_s, kd_s, b_s) in enumerate(dirs):
                    units_.append((r_s[j, rows, :], lw_s[j, rows, :], kd_s[j, rows, :], v_s[j, rows, :],
                                   kk_s[j, rows, :], b_s[j, rows, :], e == 1))
                    ids.append((j * 2 + e) * n_chunks + c)
        for uid, (lhs, add) in zip(ids, _wkv_intra(units_, consts)):
            lhs_s[uid] = lhs
            add_s[uid] = add
        return carry

    lax.fori_loop(0, (pairs // pairs_per) * groups, intra_body, 0, unroll=2)

    same_head = rs // HEAD_DIM == cs // HEAD_DIM
    for j in range(pairs):
        for e in range(2):
            if has_s0:
                both = jnp.concatenate([s0_ref[0, e, 2 * j], s0_ref[0, e, 2 * j + 1]], axis=1)
                st_s[2 * j + e] = jnp.where(same_head, jnp.concatenate([both, both], axis=0), 0.0).T
            else:
                st_s[2 * j + e] = jnp.zeros((LANES, LANES), F32)

    def state_body(it, carry):
        chunk = (it, n_chunks - 1 - it)
        uids = [(j * 2 + e) * n_chunks + chunk[e] for j in range(pairs) for e in range(2)]
        sts = [st_s[ch].astype(BF16) for ch in range(2 * pairs)]
        res = [mm(lhs_s[uid], st) + add_s[uid] for uid, st in zip(uids, sts)]
        for ch, rr in enumerate(res):
            j, e = divmod(ch, 2)
            y_s = yb_s if e else yf_s
            y_s[j, pl.ds(pl.multiple_of(chunk[e] * CHUNK, CHUNK), CHUNK), :] = rr[:CHUNK] + rr[CHUNK:STACK]
            st_s[ch] = rr[STACK:]
        return carry

    lax.fori_loop(0, n_chunks, state_body, 0, unroll=True)
    for j in range(pairs):
        for e in range(2):
            st_t = st_s[2 * j + e].T
            sn_ref[0, e, 2 * j] = st_t[:HEAD_DIM, :HEAD_DIM]
            sn_ref[0, e, 2 * j + 1] = pltpu.roll(st_t, HEAD_DIM, 1)[HEAD_DIM:, :HEAD_DIM]

    inv_d = 1.0 / HEAD_DIM
    for j in range(pairs):
        cols = slice(j * LANES, (j + 1) * LANES)
        y = yf_s[j] + yb_s[j]
        mean = _head_sum(y, ones) * inv_d
        dlt = y - mean
        var = _head_sum(dlt * dlt, ones) * inv_d
        yn = dlt * lax.rsqrt(var + GN_EPS) * lng_ref[:, cols] + lnb_ref[:, cols]
        o_ref[0, :, cols] = ((yn + bonus_s[:, cols]) * gate_s[:, cols]).astype(o_ref.dtype)


def _rwkv_branch(u_rw, bsz, b_off, s0, p, pairs, units, cast=(), attend=None):
    t_len = u_rw.shape[1]
    has_s0 = s0 is not None
    width = pairs * LANES
    seg = RW_WIDTH // width
    tok = lambda off: pl.BlockSpec((1, t_len, width), lambda b, j: (b_off + b, 0, off + j))
    mu = lambda off: pl.BlockSpec((2, width), lambda b, j: (0, off + j))
    vec2 = pl.BlockSpec((2, width), lambda b, j: (0, j))
    vec1 = pl.BlockSpec((1, width), lambda b, j: (0, j))
    mat = pl.BlockSpec((LANES, width), lambda b, j: (0, j))
    lora_w = 3 * LANES
    lora_blk = 3 * RW_WIDTH // lora_w
    in_specs = [tok(0), tok(seg), tok(2 * seg),
                pl.BlockSpec((1, t_len, lora_w), lambda b, j: (b_off + b, 0, lora_blk)),
                mu(0), mu(seg), mu(2 * seg),
                pl.BlockSpec((2, lora_w), lambda b, j: (0, lora_blk)),
                vec2, vec2, mat, mat, mat, vec1, vec1, _const_spec((RW_HEADS, HEAD_DIM)), vec1, vec1,
                _const_spec((LANES, LANES)), _const_spec((2, CHUNK, CHUNK)), _const_spec((4, STACK, STACK))]
    args = [u_rw, u_rw, u_rw, u_rw, p["shift_mu"], p["shift_mu"], p["shift_mu"], p["shift_mu"],
            p["rw_w0"], p["rw_a0"],
            p["rw_w_up"].reshape(2 * LORA_DECAY, RW_WIDTH), p["rw_a_up"].reshape(2 * LORA_ICLR, RW_WIDTH),
            p["rw_g_up"], p["rw_k_k"].reshape(1, -1), p["rw_k_a"].reshape(1, -1),
            p["rw_r_k"], p["rw_ln_g"].reshape(1, -1), p["rw_ln_b"].reshape(1, -1),
            *_wkv_constants()]
    st_spec = pl.BlockSpec((1, 2, 2 * pairs, HEAD_DIM, HEAD_DIM), lambda b, j: (b, 0, j, 0, 0))
    if has_s0:
        in_specs.append(st_spec)
        args.append(s0)
    out_specs = [pl.BlockSpec((1, t_len, width), lambda b, j: (b, 0, j)),
                 pl.BlockSpec((1, 2, 2 * pairs, HEAD_DIM, HEAD_DIM), lambda b, j: (b, 0, j, 0, 0))]
    out_shape = [jax.ShapeDtypeStruct((bsz, t_len, RW_WIDTH), BF16),
                 jax.ShapeDtypeStruct((bsz, 2, RW_HEADS, HEAD_DIM, HEAD_DIM), F32)]
    for w in cast:
        assert PAIRS == pairs and w.shape[0] % (16 * bsz) == 0
        blk = pl.BlockSpec((w.shape[0] // bsz, w.shape[1]), lambda b, j: (b, 0))
        in_specs.append(blk)
        args.append(w)
        out_specs.append(blk)
        out_shape.append(jax.ShapeDtypeStruct(w.shape, BF16))
    if attend is not None:
        u_na, q_g, k_g = attend
        assert PAIRS == pairs
        na_tok = lambda seg_: pl.BlockSpec((1, t_len, NA_WIDTH), lambda b, j: (b_off + b, 0, seg_))
        na_out = pl.BlockSpec((1, t_len, NA_WIDTH), lambda b, j: (b, 0, 0))
        g2 = lambda g: g.reshape(1, HEAD_DIM)
        in_specs += [na_tok(0), na_tok(1), na_tok(2), _const_spec((1, HEAD_DIM)), _const_spec((1, HEAD_DIM))]
        args += [u_na, u_na, u_na, g2(q_g), g2(k_g)]
        out_specs += [na_out, na_out, na_out]
        na_shape = (bsz, t_len, NA_WIDTH)
        out_shape += [jax.ShapeDtypeStruct(na_shape, BF16), jax.ShapeDtypeStruct(na_shape, F32),
                      jax.ShapeDtypeStruct(na_shape, F32)]
    n_units = 2 * pairs * (t_len // CHUNK)
    per_pair = pltpu.VMEM((pairs, t_len, LANES), F32)
    full = pltpu.VMEM((t_len, width), F32)
    scratch = [per_pair] * 9 + [full, full, per_pair, per_pair,
                                pltpu.VMEM((n_units, 2 * STACK, LANES), BF16),
                                pltpu.VMEM((n_units, 2 * STACK, LANES), F32),
                                pltpu.VMEM((2 * pairs, LANES, LANES), F32)]
    o_rw, s_new, *rest = pl.pallas_call(
        functools.partial(_rwkv_kernel, t_len=t_len, has_s0=has_s0, pairs=pairs, units=units, n_cast=len(cast),
                          attend=attend is not None),
        grid=(bsz, PAIRS // pairs),
        in_specs=in_specs,
        out_specs=out_specs,
        out_shape=out_shape,
        scratch_shapes=scratch,
        compiler_params=_params(2),
        name="rwkv_branch",
    )(*args)
    return o_rw, s_new, rest[:len(cast)], rest[len(cast):]


def _qk_norm(t, g, ones):
    ms = _head_sum(t * t, ones) * (1.0 / HEAD_DIM)
    return t * lax.rsqrt(ms + RMS_EPS) * jnp.concatenate([g, g], axis=1)


def _nt(x, y):
    return lax.dot_general(x, y, (((1,), (1,)), ((), ())), preferred_element_type=F32)


def _na_ctx_kernel(q_ref, k_ref, v_ref, qg_ref, kg_ref, o_ref, kn_ref, vc_ref):
    ones = _head_ones()
    lo_half = lax.broadcasted_iota(jnp.int32, (1, LANES), 1) < HEAD_DIM
    lo = jnp.where(lo_half, 1.0, 0.0)
    t_len = q_ref.shape[1]
    qs, ks, vs = [], [], []
    for j in range(NA_WIDTH // LANES):
        cols = slice(j * LANES, (j + 1) * LANES)
        qn = _qk_norm(q_ref[0, :, cols], qg_ref[...], ones)
        kn = _qk_norm(k_ref[0, :, cols], kg_ref[...], ones)
        v = v_ref[0, :, cols]
        kn_ref[0, :, cols] = kn
        vc_ref[0, :, cols] = v
        qn = qn * QK_SCALE
        qs.append(jnp.concatenate([qn * lo, qn * (1.0 - lo)], axis=0).astype(BF16))
        ks.append(kn.astype(BF16))
        vs.append(v.astype(BF16))
    logits = [_nt(q, k) for q, k in zip(qs, ks)]
    ms = [jnp.max(s, axis=-1, keepdims=True) for s in logits]
    ps = [jnp.exp(s - m) for s, m in zip(logits, ms)]
    ls = [jnp.sum(p, axis=-1, keepdims=True) for p in ps]
    outs = [jnp.dot(p.astype(BF16), v, preferred_element_type=F32) / l for p, v, l in zip(ps, vs, ls)]
    for j, o in enumerate(outs):
        o_ref[0, :, j * LANES:(j + 1) * LANES] = jnp.where(lo_half, o[:t_len], o[t_len:]).astype(o_ref.dtype)


NA_ROW_ILP = 4


def _na_lat_kernel(q_ref, k_ref, v_ref, kc_ref, vc_ref, qg_ref, kg_ref, diag_ref, o_ref,
                   q0_s, q1_s, kn_s, v_s, kc_s, vc_s, tab_ref, *, rows, kr):
    ones = _head_ones()
    lo_half = lax.broadcasted_iota(jnp.int32, (1, LANES), 1) < HEAD_DIM
    lo = jnp.where(lo_half, 1.0, 0.0)
    q_col = lax.broadcasted_iota(jnp.int32, (GRID_W, 2 * GRID_W), 0)
    k_col = lax.broadcasted_iota(jnp.int32, (GRID_W, 2 * GRID_W), 1) % GRID_W
    w_start = jnp.clip(q_col - NA_COLS // 2, 0, GRID_W - NA_COLS)
    valid = (k_col >= w_start) & (k_col < w_start + NA_COLS)
    for h in range(2):
        for d in range(2 * NA_ROWS - 2):
            diag = jnp.broadcast_to(diag_ref[h, d:d + 1, :], (GRID_W, 2 * GRID_W))
            tab_ref[h, d] = jnp.where(valid, pltpu.roll(diag, 0, 1, stride=1, stride_axis=0), NEG_INF)
    qn = _qk_norm(q_ref[0], qg_ref[...], ones) * QK_SCALE
    q0_s[...] = (qn * lo).astype(BF16)
    q1_s[...] = (qn * (1.0 - lo)).astype(BF16)
    kn_s[...] = _qk_norm(k_ref[0], kg_ref[...], ones).astype(BF16)
    v_s[...] = v_ref[0].astype(BF16)
    kc_s[...] = kc_ref[0].astype(BF16)
    vc_s[...] = vc_ref[0].astype(BF16)
    win = kr * GRID_W

    def body(it, carry):
        qs, k_rows, q_rows, biases = [], [], [], []
        for s in range(NA_ROW_ILP):
            i = it * NA_ROW_ILP + s
            r0 = jnp.clip(i - kr // 2, 0, rows - kr)
            d0 = r0 - i + (NA_ROWS - 1)
            qr = pl.ds(pl.multiple_of(i * GRID_W, GRID_W), GRID_W)
            q_rows.append(qr)
            k_rows.append(pl.ds(pl.multiple_of(r0 * GRID_W, GRID_W), win))
            qs.append(jnp.concatenate([q0_s[qr, :], q1_s[qr, :]], axis=0))
            biases.append(jnp.concatenate(
                [jnp.concatenate([tab_ref[h, d0 + 2 * m] for m in range(kr // 2)], axis=1) for h in range(2)],
                axis=0))
        lw = [_nt(q, kn_s[kr_, :]) + b for q, kr_, b in zip(qs, k_rows, biases)]
        lc = [_nt(q, kc_s[...]) for q in qs]
        ms = [jnp.maximum(jnp.max(a, axis=-1, keepdims=True), jnp.max(c, axis=-1, keepdims=True))
              for a, c in zip(lw, lc)]
        pw = [jnp.exp(a - m) for a, m in zip(lw, ms)]
        pc = [jnp.exp(c - m) for c, m in zip(lc, ms)]
        ls = [jnp.sum(a, axis=-1, keepdims=True) + jnp.sum(c, axis=-1, keepdims=True) for a, c in zip(pw, pc)]
        outs = [(jnp.dot(a.astype(BF16), v_s[kr_, :], preferred_element_type=F32)
                 + jnp.dot(c.astype(BF16), vc_s[...], preferred_element_type=F32)) / l
                for a, c, kr_, l in zip(pw, pc, k_rows, ls)]
        for qr, o in zip(q_rows, outs):
            o_ref[0, qr, :] = jnp.where(lo_half, o[:GRID_W], o[GRID_W:]).astype(o_ref.dtype)
        return carry

    lax.fori_loop(0, rows // NA_ROW_ILP, body, 0, unroll=True)


def _latent_bias_diagonals(rpb):
    c1 = NA_COLS - 1
    assert 2 * c1 + 1 <= GRID_W - c1
    a, b = rpb[:, :-1], rpb[:, 1:]
    gap = jnp.zeros(a.shape[:2] + (GRID_W - 2 * c1 - 1,), rpb.dtype)
    return jnp.concatenate([a[..., c1:], gap, b, gap, a[..., :c1]], axis=-1)


def _na_latent(u_na, bsz, b_off, k_ctx, v_ctx, q_g, k_g, rpb):
    t_len = u_na.shape[1]
    rows = t_len // GRID_W
    kr = min(NA_ROWS, rows)
    assert kr % 2 == 0 and rows % NA_ROW_ILP == 0
    ctx_len = k_ctx.shape[1]
    seg = NA_WIDTH // LANES
    tok = lambda off: pl.BlockSpec((1, t_len, LANES), lambda b, j: (b_off + b, 0, off + j))
    ctx = pl.BlockSpec((1, ctx_len, LANES), lambda b, j: (b, 0, j))
    g2 = lambda g: g.reshape(1, HEAD_DIM)
    diags = _latent_bias_diagonals(rpb)
    tok_s = pltpu.VMEM((t_len, LANES), BF16)
    ctx_s = pltpu.VMEM((ctx_len, LANES), BF16)
    return pl.pallas_call(
        functools.partial(_na_lat_kernel, rows=rows, kr=kr),
        grid=(bsz, seg),
        in_specs=[tok(0), tok(seg), tok(2 * seg), ctx, ctx,
                  _const_spec((1, HEAD_DIM)), _const_spec((1, HEAD_DIM)),
                  pl.BlockSpec((2, 2 * NA_ROWS - 2, 2 * GRID_W), lambda b, j: (j, 0, 0))],
        out_specs=pl.BlockSpec((1, t_len, LANES), lambda b, j: (b, 0, j)),
        out_shape=jax.ShapeDtypeStruct((bsz, t_len, NA_WIDTH), BF16),
        scratch_shapes=[tok_s, tok_s, tok_s, tok_s, ctx_s, ctx_s,
                        pltpu.VMEM((2, 2 * NA_ROWS - 2, GRID_W, 2 * GRID_W), F32)],
        compiler_params=_params(2),
        name="na_latent",
    )(u_na, u_na, u_na, k_ctx, v_ctx, g2(q_g), g2(k_g), diags)


LATE_WEIGHTS = ("w_o_rwkv", "w_o_na", "w_out", "ffn_w1", "ffn_w3", "ffn_w2")


def _out_ffn_kernel(xa_ref, xb_ref, orwa_ref, orwb_ref, onaa_ref, onab_ref, gt_ref, mod_ref, g_ref,
                    wor_ref, won_ref, wout_ref, w1_ref, w3_ref, w2_ref, ya_ref, yb_ref, *, tiles_a):
    i = pl.program_id(0)
    x = _pick_group(i, tiles_a, xa_ref, xb_ref)
    o_rw = _pick_group(i, tiles_a, orwa_ref, orwb_ref)
    o_na = _pick_group(i, tiles_a, onaa_ref, onab_ref)
    g_rw = _sigmoid(gt_ref[:, :D_MODEL].astype(F32))
    g_na = _sigmoid(gt_ref[:, D_MODEL:].astype(F32))
    merged = g_rw * _dot(o_rw, wor_ref[...]) + g_na * _dot(o_na, won_ref[...])
    x1 = x + _mod_part(mod_ref, 2) * _dot(merged, wout_ref[...])
    h2 = _rms_rows(x1) * g_ref[...]
    h2 = (h2 * (1.0 + _mod_part(mod_ref, 4)) + _mod_part(mod_ref, 3)).astype(BF16)
    acc = jnp.zeros(x1.shape, F32)
    for c in range(FF_HIDDEN // FF_CHUNK):
        cols = slice(c * FF_CHUNK, (c + 1) * FF_CHUNK)
        a = jnp.dot(h2, w1_ref[:, cols], preferred_element_type=F32)
        b = jnp.dot(h2, w3_ref[:, cols], preferred_element_type=F32)
        hh = (a * _sigmoid(a) * b).astype(BF16)
        acc = acc + jnp.dot(hh, w2_ref[cols, :], preferred_element_type=F32)
    y = x1 + _mod_part(mod_ref, 5) * acc

    @pl.when(i < tiles_a)
    def _():
        ya_ref[0] = y

    @pl.when(i >= tiles_a)
    def _():
        yb_ref[0] = y


def _out_ffn(xa, xb, orw_a, orw_b, ona_a, ona_b, gates, tiles_per_b, mod_all, norm_g, wb):
    tm = TOKEN_TILE
    tiles_a, tiles = xa.shape[0], xa.shape[0] + xb.shape[0]
    weights = [wb[n] for n in LATE_WEIGHTS]
    groups = lambda width: _group_specs(tiles_a, tm, width)
    return pl.pallas_call(
        functools.partial(_out_ffn_kernel, tiles_a=tiles_a),
        grid=(tiles,),
        in_specs=groups(D_MODEL) + groups(RW_WIDTH) + groups(NA_WIDTH)
        + [pl.BlockSpec((tm, GATE_COLS), lambda i: (i, 0)), _mod_spec(tiles_a, tiles_per_b),
           _const_spec((1, D_MODEL))] + [_const_spec(w.shape) for w in weights],
        out_specs=groups(D_MODEL),
        out_shape=[jax.ShapeDtypeStruct(xa.shape, F32), jax.ShapeDtypeStruct(xb.shape, F32)],
        compiler_params=_params(1),
        name="out_ffn",
    )(xa, xb, orw_a, orw_b, ona_a, ona_b, gates, mod_all, norm_g.reshape(1, -1), *weights)


def kernel(x_prompt, x_sample, state_rwkv, cache_na_k, cache_na_v, c, c_ctx, norm1_g, norm2_g, w_ada, b_ada,
           w_in, shift_mu, rw_w0, rw_w_up, rw_a0, rw_a_up, rw_g_up, rw_k_k, rw_k_a, rw_r_k, rw_ln_g, rw_ln_b,
           na_q_g, na_k_g, na_rpb, w_o_rwkv, w_o_na, w_out, ffn_w1, ffn_w3, ffn_w2):
    depth = w_in.shape[0]
    bsz, seq = x_prompt.shape[:2]
    dec, dec_seq = x_sample.shape[:2]
    tm = TOKEN_TILE
    n_ctx = bsz * seq
    assert n_ctx % tm == 0 and dec_seq % tm == 0 and n_ctx % dec_seq == 0
    tiles_per_b = dec_seq // tm
    tiled = lambda t: t.reshape(-1, tm, t.shape[-1])
    cvecs = jnp.concatenate([c_ctx[None, :], c], axis=0).T
    y_p, y_s = x_prompt, x_sample
    new_s, new_k, new_v = [], [], []
    for l in range(depth):
        p = dict(norm1_g=norm1_g[l], norm2_g=norm2_g[l], shift_mu=shift_mu[l], rw_w0=rw_w0[l],
                 rw_w_up=rw_w_up[l], rw_a0=rw_a0[l], rw_a_up=rw_a_up[l], rw_g_up=rw_g_up[l],
                 rw_k_k=rw_k_k[l], rw_k_a=rw_k_a[l], rw_r_k=rw_r_k[l], rw_ln_g=rw_ln_g[l],
                 rw_ln_b=rw_ln_b[l], na_q_g=na_q_g[l], na_k_g=na_k_g[l], na_rpb=na_rpb[l])
        late_f32 = [w[l] for w in (w_o_rwkv, w_o_na, w_out, ffn_w1, ffn_w3, ffn_w2)]
        mod_all, w_in_bf = _modulation(cvecs, w_ada[l], b_ada[l], w_in[l])
        u_rw, u_na, gates = _in_proj(tiled(y_p), tiled(y_s), tiles_per_b, mod_all, p["norm1_g"], w_in_bf)
        ctx_view = lambda t: t.reshape(-1, seq, t.shape[-1])
        lat_view = lambda t: t.reshape(-1, dec_seq, t.shape[-1])
        lat_off = n_ctx // dec_seq
        o_rw_p, s_l, casted, (o_na_p, k_l, v_l) = _rwkv_branch(
            ctx_view(u_rw), bsz, 0, None, p, RWKV_PAIRS_CTX, RWKV_UNITS, late_f32,
            attend=(ctx_view(u_na), p["na_q_g"], p["na_k_g"]))
        late = dict(zip(LATE_WEIGHTS, casted))
        new_s.append(s_l)
        new_k.append(k_l.reshape(bsz, seq, NA_HEADS, HEAD_DIM))
        new_v.append(v_l.reshape(bsz, seq, NA_HEADS, HEAD_DIM))
        ctx_k = cache_na_k[:, l].reshape(dec, -1, NA_WIDTH)
        ctx_v = cache_na_v[:, l].reshape(dec, -1, NA_WIDTH)
        o_rw_s = _rwkv_branch(lat_view(u_rw), dec, lat_off, state_rwkv[:, l], p, RWKV_PAIRS_LAT,
                               RWKV_UNITS)[0]
        o_na_s = _na_latent(lat_view(u_na), dec, lat_off, ctx_k, ctx_v, p["na_q_g"], p["na_k_g"], p["na_rpb"])
        y_p_t, y_s_t = _out_ffn(tiled(y_p), tiled(y_s), tiled(o_rw_p), tiled(o_rw_s), tiled(o_na_p), tiled(o_na_s),
                                gates, tiles_per_b, mod_all, p["norm2_g"], late)
        y_p, y_s = y_p_t.reshape(x_prompt.shape), y_s_t.reshape(x_sample.shape)
    return (y_p, y_s, jnp.stack(new_s, axis=1), jnp.stack(new_k, axis=1), jnp.stack(new_v, axis=1))
```

```python
import functools

import numpy as np
import jax
import jax.numpy as jnp
from jax import lax
from jax.experimental import pallas as pl
from jax.experimental.pallas import tpu as pltpu

D_MODEL = 1024
GRID_W = 64
HEAD_DIM = 64
RW_HEADS = 8
RW_WIDTH = RW_HEADS * HEAD_DIM
NA_HEADS = 8
NA_WIDTH = NA_HEADS * HEAD_DIM
LORA_DECAY = 64
LORA_ICLR = 64
LORA_GATE = 128
NA_ROWS = 8
NA_COLS = 16
FF_HIDDEN = 2816
RW_COLS = 3 * RW_WIDTH + 2 * LORA_DECAY + 2 * LORA_ICLR + LORA_GATE
NA_IN_COLS = 3 * NA_WIDTH
GATE_COLS = 2 * D_MODEL
RMS_EPS = 1e-6
GN_EPS = 64e-5
L2_EPS = 1e-12
NEG_INF = -1e30
DECAY_SCALE = float(np.exp(-0.5))
QK_SCALE = HEAD_DIM ** -0.5
assert QK_SCALE == 0.125

LANES = 128
PAIRS = RW_HEADS // 2
CHUNK = 64
STACK = 2 * CHUNK
RWKV_PAIRS_CTX = 4
RWKV_PAIRS_LAT = 2
RWKV_UNITS = 16
TOKEN_TILE = 512
FF_CHUNK = 256
VMEM_LIMIT = 56 * 1024 * 1024

F32 = jnp.float32
BF16 = jnp.bfloat16


def _dot(a, b):
    return jnp.dot(a.astype(BF16), b.astype(BF16), preferred_element_type=F32)


def _split2(x):
    hi = x.astype(BF16)
    lo = (x - hi.astype(F32)).astype(BF16)
    return hi, lo


def _dot_exact_lhs(a_exact, b):
    h, l = _split2(b)
    d = lambda x: jnp.dot(a_exact, x, preferred_element_type=F32)
    return d(h) + d(l)


def _head_ones():
    r = lax.broadcasted_iota(jnp.int32, (LANES, LANES), 0) // HEAD_DIM
    c = lax.broadcasted_iota(jnp.int32, (LANES, LANES), 1) // HEAD_DIM
    return jnp.where(r == c, 1.0, 0.0).astype(BF16)


def _head_sum(x, ones):
    return jnp.dot(x.astype(BF16), ones, preferred_element_type=F32)


def _sigmoid(x):
    return 0.5 * jnp.tanh(0.5 * x) + 0.5


def _rms_rows(x):
    return x * lax.rsqrt(jnp.mean(x * x, axis=-1, keepdims=True) + RMS_EPS)


def _const_spec(shape):
    nd = len(shape)
    return pl.BlockSpec(shape, lambda *_: (0,) * nd, pipeline_mode=pl.Buffered(1))


def _params(n_axes):
    return pltpu.CompilerParams(dimension_semantics=("arbitrary",) * n_axes,
                                vmem_limit_bytes=VMEM_LIMIT)


def _mod_kernel(c_ref, w_ref, b_ref, win_ref, o_ref, winb_ref):
    s = c_ref[...]
    s = s * _sigmoid(s)
    w = w_ref[...]
    for r in range(s.shape[1]):
        o_ref[r] = jnp.sum(w * s[:, r:r + 1], axis=0, keepdims=True) + b_ref[...]
    winb_ref[...] = win_ref[...].astype(BF16)


def _modulation(cvecs, w_ada, b_ada, w_in):
    n = cvecs.shape[1]
    tn = 768
    steps = 6 * D_MODEL // tn
    rows = w_in.shape[0] // steps
    assert rows % 16 == 0
    return pl.pallas_call(
        _mod_kernel,
        grid=(steps,),
        in_specs=[pl.BlockSpec((D_MODEL, n), lambda j: (0, 0)),
                  pl.BlockSpec((D_MODEL, tn), lambda j: (0, j)),
                  pl.BlockSpec((1, tn), lambda j: (0, j)),
                  pl.BlockSpec((rows, w_in.shape[1]), lambda j: (j, 0))],
        out_specs=[pl.BlockSpec((n, 1, tn), lambda j: (0, 0, j)),
                   pl.BlockSpec((rows, w_in.shape[1]), lambda j: (j, 0))],
        out_shape=[jax.ShapeDtypeStruct((n, 1, 6 * D_MODEL), F32),
                   jax.ShapeDtypeStruct(w_in.shape, BF16)],
        compiler_params=_params(1),
        name="modulation",
    )(cvecs, w_ada, b_ada.reshape(1, -1), w_in)


def _pick_group(i, tiles_a, a_ref, b_ref):
    return jnp.where(i < tiles_a, a_ref[0], b_ref[0])


def _group_specs(tiles_a, tm, width):
    return [pl.BlockSpec((1, tm, width), lambda i: (jnp.minimum(i, tiles_a - 1), 0, 0)),
            pl.BlockSpec((1, tm, width), lambda i: (jnp.maximum(i - tiles_a, 0), 0, 0))]


def _mod_spec(tiles_a, tiles_per_b):
    return pl.BlockSpec((1, 1, 6 * D_MODEL),
                        lambda i: (jnp.where(i < tiles_a, 0, 1 + (i - tiles_a) // tiles_per_b), 0, 0))


def _mod_part(mod_ref, k):
    return mod_ref[0, :, k * D_MODEL:(k + 1) * D_MODEL]


def _inproj_kernel(xa_ref, xb_ref, mod_ref, g_ref, w_ref, urw_ref, una_ref, gt_ref, *, tiles_a):
    x = _pick_group(pl.program_id(0), tiles_a, xa_ref, xb_ref)
    h = _rms_rows(x) * g_ref[...]
    h = (h * (1.0 + _mod_part(mod_ref, 1)) + _mod_part(mod_ref, 0)).astype(BF16)
    d = lambda lo, hi: jnp.dot(h, w_ref[:, lo:hi], preferred_element_type=F32)
    urw_ref[...] = d(0, RW_COLS)
    una_ref[...] = d(RW_COLS, RW_COLS + NA_IN_COLS)
    gt_ref[...] = d(RW_COLS + NA_IN_COLS, RW_COLS + NA_IN_COLS + GATE_COLS).astype(BF16)


def _in_proj(xa, xb, tiles_per_b, mod_all, norm_g, w_in_bf):
    tm = TOKEN_TILE
    tiles_a, tiles = xa.shape[0], xa.shape[0] + xb.shape[0]
    row = lambda i: (i, 0)
    return pl.pallas_call(
        functools.partial(_inproj_kernel, tiles_a=tiles_a),
        grid=(tiles,),
        in_specs=_group_specs(tiles_a, tm, D_MODEL) + [_mod_spec(tiles_a, tiles_per_b),
                                                       _const_spec((1, D_MODEL)), _const_spec(w_in_bf.shape)],
        out_specs=[pl.BlockSpec((tm, RW_COLS), row),
                   pl.BlockSpec((tm, NA_IN_COLS), row),
                   pl.BlockSpec((tm, GATE_COLS), row)],
        out_shape=[jax.ShapeDtypeStruct((tiles * tm, RW_COLS), F32),
                   jax.ShapeDtypeStruct((tiles * tm, NA_IN_COLS), F32),
                   jax.ShapeDtypeStruct((tiles * tm, GATE_COLS), BF16)],
        compiler_params=_params(1),
        name="in_proj",
    )(xa, xb, mod_all, norm_g.reshape(1, -1), w_in_bf)


def _shift(x, mu):
    t_len = x.shape[0]
    row = lax.broadcasted_iota(jnp.int32, x.shape, 0)
    prev = jnp.where(row == 0, 0.0, pltpu.roll(x, 1, 0))
    nxt = jnp.where(row == t_len - 1, 0.0, pltpu.roll(x, t_len - 1, 0))
    return x + mu[0:1, :] * (prev - x) + mu[1:2, :] * (nxt - x)


def _stack_heads(x, lane_lo):
    return jnp.concatenate([x * lane_lo, x * (1.0 - lane_lo)], axis=0)


def _wkv_intra(units, consts):
    tri, mask_s, mask_i, eye, lane_lo, blk = consts
    stack = lambda z: _stack_heads(z, lane_lo)

    cums = [_dot_exact_lhs(tri[int(u[6])], u[1]) for u in units]

    prep = []
    for (r, lw, kd, v, kk, b, reverse), cum in zip(units, cums):
        mid_row = CHUNK // 2 if reverse else CHUNK // 2 - 1
        tot_row = 0 if reverse else CHUNK - 1
        a = -kk
        ex = cum - lw
        mid = cum[mid_row:mid_row + 1, :]
        tot = cum[tot_row:tot_row + 1, :]
        up = jnp.exp(cum - mid)
        dn = jnp.exp(mid - cum)
        tail = jnp.exp(tot - cum)
        prep.append(dict(
            at_m=stack(a * jnp.exp(ex - mid)).astype(BF16),
            rt_m=stack(r * up).astype(BF16),
            bk_m=jnp.concatenate([b * dn, kd * dn], axis=0).astype(BF16),
            a_e=stack(a * jnp.exp(ex)),
            r_e=stack(r * jnp.exp(cum)),
            bk_t=jnp.concatenate([stack(b * tail), stack(kd * tail)], axis=0).T.astype(BF16),
            vv=stack(v).astype(BF16),
            diag=jnp.where(eye, jnp.exp(tot), 0.0),
            rev=int(reverse)))

    ntd = lambda x, y: lax.dot_general(x, y, (((1,), (1,)), ((), ())), preferred_element_type=F32)
    mm = lambda x, y: jnp.dot(x, y, preferred_element_type=F32)
    diag_blk, swap_eye = blk
    roll_head = lambda z: pltpu.roll(z, HEAD_DIM, 1)
    by_block = lambda z: jnp.concatenate([z[:CHUNK], roll_head(z[CHUNK:])], axis=0)
    top = [by_block(ntd(p["at_m"], p["bk_m"])) for p in prep]
    low = [by_block(ntd(p["rt_m"], p["bk_m"])) for p in prep]
    a_ak = [(roll_head(t) * mask_s[p["rev"]]).astype(BF16) for t, p in zip(top, prep)]
    bot = [jnp.concatenate([t * mask_i[p["rev"]], roll_head(t) * mask_i[p["rev"]]], axis=1).astype(BF16)
           for t, p in zip(low, prep)]

    off_blk = 1.0 - diag_blk
    both = [t * mask_s[p["rev"]] + swap_eye for t, p in zip(top, prep)]
    steps = CHUNK.bit_length() - 1
    diag_bf = diag_blk.astype(BF16)
    for j in range(steps):
        packed = [q.astype(BF16) for q in both]
        res = [mm(qb * diag_bf, qb) for qb in packed]
        both = [r + off_blk * q for r, q in zip(res, both)]
    ts = [pltpu.roll(q, HEAD_DIM, 1).astype(BF16) for q in both]
    x0 = [jnp.concatenate([p["a_e"], mm(ak, p["vv"])], axis=1) for p, ak in zip(prep, a_ak)]
    xs = [mm(t, x.astype(BF16)) for x, t in zip(x0, ts)]

    out = []
    zeros = jnp.zeros((STACK, LANES), BF16)
    for p, x, bt in zip(prep, xs, bot):
        rhs = jnp.concatenate([x.astype(BF16), jnp.concatenate([zeros, p["vv"]], axis=1)], axis=0)
        lhs = jnp.concatenate([bt, p["bk_t"]], axis=0)
        res = mm(lhs, rhs)
        lhs2 = res[:, :LANES] + jnp.concatenate([p["r_e"], p["diag"]], axis=0)
        out.append((lhs2.astype(BF16), res[:, LANES:]))
    return out


def _wkv_constants():
    lane_head = np.arange(LANES) // HEAD_DIM
    ones = (lane_head[:, None] == lane_head[None, :]).astype(np.float32)
    t = np.arange(CHUNK)
    tri = np.stack([t[None, :] <= t[:, None], t[None, :] >= t[:, None]]).astype(np.float32)
    rs, cs = t[:, None], t[None, :]
    tri_masks = np.stack([cs < rs, cs > rs, cs <= rs, cs >= rs]).astype(np.float32)
    masks = np.kron(np.eye(2, dtype=np.float32), tri_masks)
    return jnp.asarray(ones, BF16), jnp.asarray(tri, BF16), jnp.asarray(masks, F32)


def _rwkv_kernel(*refs, t_len, has_s0, pairs, units, n_cast, attend):
    (r_ref, k_ref, v_ref, lo_ref, mur_ref, muk_ref, muv_ref, mul_ref, w0_ref, a0_ref, wup_ref, aup_ref,
     gup_ref, kk_ref, ka_ref, rk_ref, lng_ref, lnb_ref, ones_ref, tri_ref, mask_ref) = refs[:21]
    pos = 21
    s0_ref = None
    if has_s0:
        s0_ref = refs[pos]
        pos += 1
    cast_in = refs[pos:pos + n_cast]
    pos += n_cast
    n_att_in, n_att_out = (5, 3) if attend else (0, 0)
    att_in = refs[pos:pos + n_att_in]
    pos += n_att_in
    o_ref, sn_ref = refs[pos], refs[pos + 1]
    cast_out = refs[pos + 2:pos + 2 + n_cast]
    pos += 2 + n_cast
    att_out = refs[pos:pos + n_att_out]
    (r_s, v_s, kk_s, b0_s, b1_s, lw0_s, lw1_s, kd0_s, kd1_s, gate_s, bonus_s, yf_s, yb_s,
     lhs_s, add_s, st_s) = refs[pos + n_att_out:]
    if attend:
        _na_ctx_kernel(*att_in, *att_out)
    for w_in_ref, w_out_ref in zip(cast_in, cast_out):
        w_out_ref[...] = w_in_ref[...].astype(BF16)

    ones = ones_ref[...]
    lane = lax.broadcasted_iota(jnp.int32, (1, LANES), 1)
    lo_half = lane < HEAD_DIM
    lane_lo = jnp.where(lo_half, 1.0, 0.0)
    mm = lambda x, y: jnp.dot(x, y, preferred_element_type=F32)

    lo = _shift(lo_ref[0], mul_ref[...])
    wd = jnp.tanh(lo[:, 0:LANES])
    ad = lo[:, LANES:2 * LANES]
    sig_gd = _sigmoid(lo[:, 2 * LANES:3 * LANES]).astype(BF16)
    wd_split = [_split2(wd * m) for m in (lane_lo, 1.0 - lane_lo)]
    ad_bf = [(ad * m).astype(BF16) for m in (lane_lo, 1.0 - lane_lo)]
    for j in range(pairs):
        cols = slice(j * LANES, (j + 1) * LANES)
        r = _shift(r_ref[0, :, cols], mur_ref[:, cols])
        k = _shift(k_ref[0, :, cols], muk_ref[:, cols])
        v = _shift(v_ref[0, :, cols], muv_ref[:, cols])
        kk = k * kk_ref[:, cols]
        kk = kk * lax.rsqrt(_head_sum(kk * kk, ones) + L2_EPS)
        wup_h, wup_l = _split2(wup_ref[:, cols])
        aup = aup_ref[:, cols].astype(BF16)
        kdirs = []
        for e, (lw_s, kd_s, b_s) in enumerate(((lw0_s, kd0_s, b0_s), (lw1_s, kd1_s, b1_s))):
            wd_h, wd_l = wd_split[e]
            w_lin = w0_ref[e:e + 1, cols] + (mm(wd_h, wup_h) + mm(wd_l, wup_h) + mm(wd_h, wup_l))
            lw_s[j] = -DECAY_SCALE * _sigmoid(w_lin)
            iclr = _sigmoid(a0_ref[e:e + 1, cols] + mm(ad_bf[e], aup))
            kd = k * (1.0 + (iclr - 1.0) * ka_ref[:, cols])
            kd_s[j] = kd
            b_s[j] = kk * iclr
            kdirs.append(kd)
        gate_s[:, cols] = mm(sig_gd, gup_ref[:, cols].astype(BF16))
        h0 = 2 * (pl.program_id(1) * pairs + j)
        rk = jnp.concatenate([rk_ref[pl.ds(h0, 1), :], rk_ref[pl.ds(h0 + 1, 1), :]], axis=1)
        bonus_s[:, cols] = _head_sum(r * (0.5 * (kdirs[0] + kdirs[1])) * rk, ones) * v
        r_s[j] = r
        v_s[j] = v
        kk_s[j] = kk

    n_chunks = t_len // CHUNK
    chunks_per = min(n_chunks, units // 2)
    pairs_per = min(pairs, units // (2 * chunks_per))
    groups = n_chunks // chunks_per
    rs = lax.broadcasted_iota(jnp.int32, (STACK, STACK), 0)
    cs = lax.broadcasted_iota(jnp.int32, (STACK, STACK), 1)
    as_f32 = lambda m: jnp.where(m, 1.0, 0.0)
    blk = (as_f32(rs // CHUNK == cs // CHUNK), as_f32(cs == (rs + CHUNK) % STACK))
    consts = ((tri_ref[0], tri_ref[1]), (mask_ref[0], mask_ref[1]), (mask_ref[2], mask_ref[3]),
              rs == cs, lane_lo, blk)
    dirs = ((lw0_s, kd0_s, b0_s), (lw1_s, kd1_s, b1_s))

    def intra_body(it, carry):
        pg = it // groups
        g = it % groups
        units_, ids = [], []
        for jj in range(pairs_per):
            j = pg * pairs_per + jj
            for cc in range(chunks_per):
                c = g * chunks_per + cc
                rows = pl.ds(pl.multiple_of(c * CHUNK, CHUNK), CHUNK)
                for e, (lw_s, kd_s, b_s) in enumerate(dirs):
                    units_.append((r_s[j, rows, :], lw_s[j, rows, :], kd_s[j, rows, :], v_s[j, rows, :],
                                   kk_s[j, rows, :], b_s[j, rows, :], e == 1))
                    ids.append((j * 2 + e) * n_chunks + c)
        for uid, (lhs, add) in zip(ids, _wkv_intra(units_, consts)):
            lhs_s[uid] = lhs
            add_s[uid] = add
        return carry

    lax.fori_loop(0, (pairs // pairs_per) * groups, intra_body, 0, unroll=2)

    same_head = rs // HEAD_DIM == cs // HEAD_DIM
    for j in range(pairs):
        for e in range(2):
            if has_s0:
                both = jnp.concatenate([s0_ref[0, e, 2 * j], s0_ref[0, e, 2 * j + 1]], axis=1)
                st_s[2 * j + e] = jnp.where(same_head, jnp.concatenate([both, both], axis=0), 0.0).T
            else:
                st_s[2 * j + e] = jnp.zeros((LANES, LANES), F32)

    def state_body(it, carry):
        chunk = (it, n_chunks - 1 - it)
        uids = [(j * 2 + e) * n_chunks + chunk[e] for j in range(pairs) for e in range(2)]
        sts = [st_s[ch].astype(BF16) for ch in range(2 * pairs)]
        res = [mm(lhs_s[uid], st) + add_s[uid] for uid, st in zip(uids, sts)]
        for ch, rr in enumerate(res):
            j, e = divmod(ch, 2)
            y_s = yb_s if e else yf_s
            y_s[j, pl.ds(pl.multiple_of(chunk[e] * CHUNK, CHUNK), CHUNK), :] = rr[:CHUNK] + rr[CHUNK:STACK]
            st_s[ch] = rr[STACK:]
        return carry

    lax.fori_loop(0, n_chunks, state_body, 0, unroll=True)
    for j in range(pairs):
        for e in range(2):
            st_t = st_s[2 * j + e].T
            sn_ref[0, e, 2 * j] = st_t[:HEAD_DIM, :HEAD_DIM]
            sn_ref[0, e, 2 * j + 1] = pltpu.roll(st_t, HEAD_DIM, 1)[HEAD_DIM:, :HEAD_DIM]

    inv_d = 1.0 / HEAD_DIM
    for j in range(pairs):
        cols = slice(j * LANES, (j + 1) * LANES)
        y = yf_s[j] + yb_s[j]
        mean = _head_sum(y, ones) * inv_d
        dlt = y - mean
        var = _head_sum(dlt * dlt, ones) * inv_d
        yn = dlt * lax.rsqrt(var + GN_EPS) * lng_ref[:, cols] + lnb_ref[:, cols]
        o_ref[0, :, cols] = ((yn + bonus_s[:, cols]) * gate_s[:, cols]).astype(o_ref.dtype)


def _rwkv_branch(u_rw, bsz, b_off, s0, p, pairs, units, cast=(), attend=None):
    t_len = u_rw.shape[1]
    has_s0 = s0 is not None
    width = pairs * LANES
    seg = RW_WIDTH // width
    tok = lambda off: pl.BlockSpec((1, t_len, width), lambda b, j: (b_off + b, 0, off + j))
    mu = lambda off: pl.BlockSpec((2, width), lambda b, j: (0, off + j))
    vec2 = pl.BlockSpec((2, width), lambda b, j: (0, j))
    vec1 = pl.BlockSpec((1, width), lambda b, j: (0, j))
    mat = pl.BlockSpec((LANES, width), lambda b, j: (0, j))
    lora_w = 3 * LANES
    lora_blk = 3 * RW_WIDTH // lora_w
    in_specs = [tok(0), tok(seg), tok(2 * seg),
                pl.BlockSpec((1, t_len, lora_w), lambda b, j: (b_off + b, 0, lora_blk)),
                mu(0), mu(seg), mu(2 * seg),
                pl.BlockSpec((2, lora_w), lambda b, j: (0, lora_blk)),
                vec2, vec2, mat, mat, mat, vec1, vec1, _const_spec((RW_HEADS, HEAD_DIM)), vec1, vec1,
                _const_spec((LANES, LANES)), _const_spec((2, CHUNK, CHUNK)), _const_spec((4, STACK, STACK))]
    args = [u_rw, u_rw, u_rw, u_rw, p["shift_mu"], p["shift_mu"], p["shift_mu"], p["shift_mu"],
            p["rw_w0"], p["rw_a0"],
            p["rw_w_up"].reshape(2 * LORA_DECAY, RW_WIDTH), p["rw_a_up"].reshape(2 * LORA_ICLR, RW_WIDTH),
            p["rw_g_up"], p["rw_k_k"].reshape(1, -1), p["rw_k_a"].reshape(1, -1),
            p["rw_r_k"], p["rw_ln_g"].reshape(1, -1), p["rw_ln_b"].reshape(1, -1),
            *_wkv_constants()]
    st_spec = pl.BlockSpec((1, 2, 2 * pairs, HEAD_DIM, HEAD_DIM), lambda b, j: (b, 0, j, 0, 0))
    if has_s0:
        in_specs.append(st_spec)
        args.append(s0)
    out_specs = [pl.BlockSpec((1, t_len, width), lambda b, j: (b, 0, j)),
                 pl.BlockSpec((1, 2, 2 * pairs, HEAD_DIM, HEAD_DIM), lambda b, j: (b, 0, j, 0, 0))]
    out_shape = [jax.ShapeDtypeStruct((bsz, t_len, RW_WIDTH), BF16),
                 jax.ShapeDtypeStruct((bsz, 2, RW_HEADS, HEAD_DIM, HEAD_DIM), F32)]
    for w in cast:
        assert PAIRS == pairs and w.shape[0] % (16 * bsz) == 0
        blk = pl.BlockSpec((w.shape[0] // bsz, w.shape[1]), lambda b, j: (b, 0))
        in_specs.append(blk)
        args.append(w)
        out_specs.append(blk)
        out_shape.append(jax.ShapeDtypeStruct(w.shape, BF16))
    if attend is not None:
        u_na, q_g, k_g = attend
        assert PAIRS == pairs
        na_tok = lambda seg_: pl.BlockSpec((1, t_len, NA_WIDTH), lambda b, j: (b_off + b, 0, seg_))
        na_out = pl.BlockSpec((1, t_len, NA_WIDTH), lambda b, j: (b, 0, 0))
        g2 = lambda g: g.reshape(1, HEAD_DIM)
        in_specs += [na_tok(0), na_tok(1), na_tok(2), _const_spec((1, HEAD_DIM)), _const_spec((1, HEAD_DIM))]
        args += [u_na, u_na, u_na, g2(q_g), g2(k_g)]
        out_specs += [na_out, na_out, na_out]
        na_shape = (bsz, t_len, NA_WIDTH)
        out_shape += [jax.ShapeDtypeStruct(na_shape, BF16), jax.ShapeDtypeStruct(na_shape, F32),
                      jax.ShapeDtypeStruct(na_shape, F32)]
    n_units = 2 * pairs * (t_len // CHUNK)
    per_pair = pltpu.VMEM((pairs, t_len, LANES), F32)
    full = pltpu.VMEM((t_len, width), F32)
    scratch = [per_pair] * 9 + [full, full, per_pair, per_pair,
                                pltpu.VMEM((n_units, 2 * STACK, LANES), BF16),
                                pltpu.VMEM((n_units, 2 * STACK, LANES), F32),
                                pltpu.VMEM((2 * pairs, LANES, LANES), F32)]
    o_rw, s_new, *rest = pl.pallas_call(
        functools.partial(_rwkv_kernel, t_len=t_len, has_s0=has_s0, pairs=pairs, units=units, n_cast=len(cast),
                          attend=attend is not None),
        grid=(bsz, PAIRS // pairs),
        in_specs=in_specs,
        out_specs=out_specs,
        out_shape=out_shape,
        scratch_shapes=scratch,
        compiler_params=_params(2),
        name="rwkv_branch",
    )(*args)
    return o_rw, s_new, rest[:len(cast)], rest[len(cast):]


def _qk_norm(t, g, ones):
    ms = _head_sum(t * t, ones) * (1.0 / HEAD_DIM)
    return t * lax.rsqrt(ms + RMS_EPS) * jnp.concatenate([g, g], axis=1)


def _nt(x, y):
    return lax.dot_general(x, y, (((1,), (1,)), ((), ())), preferred_element_type=F32)


def _na_ctx_kernel(q_ref, k_ref, v_ref, qg_ref, kg_ref, o_ref, kn_ref, vc_ref):
    ones = _head_ones()
    lo_half = lax.broadcasted_iota(jnp.int32, (1, LANES), 1) < HEAD_DIM
    lo = jnp.where(lo_half, 1.0, 0.0)
    t_len = q_ref.shape[1]
    qs, ks, vs = [], [], []
    for j in range(NA_WIDTH // LANES):
        cols = slice(j * LANES, (j + 1) * LANES)
        qn = _qk_norm(q_ref[0, :, cols], qg_ref[...], ones)
        kn = _qk_norm(k_ref[0, :, cols], kg_ref[...], ones)
        v = v_ref[0, :, cols]
        kn_ref[0, :, cols] = kn
        vc_ref[0, :, cols] = v
        qn = qn * QK_SCALE
        qs.append(jnp.concatenate([qn * lo, qn * (1.0 - lo)], axis=0).astype(BF16))
        ks.append(kn.astype(BF16))
        vs.append(v.astype(BF16))
    logits = [_nt(q, k) for q, k in zip(qs, ks)]
    ms = [jnp.max(s, axis=-1, keepdims=True) for s in logits]
    ps = [jnp.exp(s - m) for s, m in zip(logits, ms)]
    ls = [jnp.sum(p, axis=-1, keepdims=True) for p in ps]
    outs = [jnp.dot(p.astype(BF16), v, preferred_element_type=F32) / l for p, v, l in zip(ps, vs, ls)]
    for j, o in enumerate(outs):
        o_ref[0, :, j * LANES:(j + 1) * LANES] = jnp.where(lo_half, o[:t_len], o[t_len:]).astype(o_ref.dtype)


NA_ROW_ILP = 4


def _na_lat_kernel(q_ref, k_ref, v_ref, kc_ref, vc_ref, qg_ref, kg_ref, diag_ref, o_ref,
                   q0_s, q1_s, kn_s, v_s, kc_s, vc_s, tab_ref, *, rows, kr):
    ones = _head_ones()
    lo_half = lax.broadcasted_iota(jnp.int32, (1, LANES), 1) < HEAD_DIM
    lo = jnp.where(lo_half, 1.0, 0.0)
    q_col = lax.broadcasted_iota(jnp.int32, (GRID_W, 2 * GRID_W), 0)
    k_col = lax.broadcasted_iota(jnp.int32, (GRID_W, 2 * GRID_W), 1) % GRID_W
    w_start = jnp.clip(q_col - NA_COLS // 2, 0, GRID_W - NA_COLS)
    valid = (k_col >= w_start) & (k_col < w_start + NA_COLS)
    for h in range(2):
        for d in range(2 * NA_ROWS - 2):
            diag = jnp.broadcast_to(diag_ref[h, d:d + 1, :], (GRID_W, 2 * GRID_W))
            tab_ref[h, d] = jnp.where(valid, pltpu.roll(diag, 0, 1, stride=1, stride_axis=0), NEG_INF)
    qn = _qk_norm(q_ref[0], qg_ref[...], ones) * QK_SCALE
    q0_s[...] = (qn * lo).astype(BF16)
    q1_s[...] = (qn * (1.0 - lo)).astype(BF16)
    kn_s[...] = _qk_norm(k_ref[0], kg_ref[...], ones).astype(BF16)
    v_s[...] = v_ref[0].astype(BF16)
    kc_s[...] = kc_ref[0].astype(BF16)
    vc_s[...] = vc_ref[0].astype(BF16)
    win = kr * GRID_W

    def body(it, carry):
        qs, k_rows, q_rows, biases = [], [], [], []
        for s in range(NA_ROW_ILP):
            i = it * NA_ROW_ILP + s
            r0 = jnp.clip(i - kr // 2, 0, rows - kr)
            d0 = r0 - i + (NA_ROWS - 1)
            qr = pl.ds(pl.multiple_of(i * GRID_W, GRID_W), GRID_W)
            q_rows.append(qr)
            k_rows.append(pl.ds(pl.multiple_of(r0 * GRID_W, GRID_W), win))
            qs.append(jnp.concatenate([q0_s[qr, :], q1_s[qr, :]], axis=0))
            biases.append(jnp.concatenate(
                [jnp.concatenate([tab_ref[h, d0 + 2 * m] for m in range(kr // 2)], axis=1) for h in range(2)],
                axis=0))
        lw = [_nt(q, kn_s[kr_, :]) + b for q, kr_, b in zip(qs, k_rows, biases)]
        lc = [jnp.dot(q, kc_s[...], preferred_element_type=F32) for q in qs]
        ms = [jnp.maximum(jnp.max(a, axis=-1, keepdims=True), jnp.max(c, axis=-1, keepdims=True))
              for a, c in zip(lw, lc)]
        pw = [jnp.exp(a - m) for a, m in zip(lw, ms)]
        pc = [jnp.exp(c - m) for c, m in zip(lc, ms)]
        ls = [jnp.sum(a, axis=-1, keepdims=True) + jnp.sum(c, axis=-1, keepdims=True) for a, c in zip(pw, pc)]
        outs = [(jnp.dot(a.astype(BF16), v_s[kr_, :], preferred_element_type=F32)
                 + _nt(c.astype(BF16), vc_s[...])) / l
                for a, c, kr_, l in zip(pw, pc, k_rows, ls)]
        for qr, o in zip(q_rows, outs):
            o_ref[0, qr, :] = jnp.where(lo_half, o[:GRID_W], o[GRID_W:]).astype(o_ref.dtype)
        return carry

    lax.fori_loop(0, rows // NA_ROW_ILP, body, 0, unroll=True)


def _latent_bias_diagonals(rpb):
    c1 = NA_COLS - 1
    assert 2 * c1 + 1 <= GRID_W - c1
    a, b = rpb[:, :-1], rpb[:, 1:]
    gap = jnp.zeros(a.shape[:2] + (GRID_W - 2 * c1 - 1,), rpb.dtype)
    return jnp.concatenate([a[..., c1:], gap, b, gap, a[..., :c1]], axis=-1)


def _na_latent(u_na, bsz, b_off, k_ctx, v_ctx, q_g, k_g, rpb):
    t_len = u_na.shape[1]
    rows = t_len // GRID_W
    kr = min(NA_ROWS, rows)
    assert kr % 2 == 0 and rows % NA_ROW_ILP == 0
    ctx_len = k_ctx.shape[2]
    seg = NA_WIDTH // LANES
    tok = lambda off: pl.BlockSpec((1, t_len, LANES), lambda b, j: (b_off + b, 0, off + j))
    ctx = pl.BlockSpec((1, LANES, ctx_len), lambda b, j: (b, j, 0))
    g2 = lambda g: g.reshape(1, HEAD_DIM)
    diags = _latent_bias_diagonals(rpb)
    tok_s = pltpu.VMEM((t_len, LANES), BF16)
    ctx_s = pltpu.VMEM((LANES, ctx_len), BF16)
    return pl.pallas_call(
        functools.partial(_na_lat_kernel, rows=rows, kr=kr),
        grid=(bsz, seg),
        in_specs=[tok(0), tok(seg), tok(2 * seg), ctx, ctx,
                  _const_spec((1, HEAD_DIM)), _const_spec((1, HEAD_DIM)),
                  pl.BlockSpec((2, 2 * NA_ROWS - 2, 2 * GRID_W), lambda b, j: (j, 0, 0))],
        out_specs=pl.BlockSpec((1, t_len, LANES), lambda b, j: (b, 0, j)),
        out_shape=jax.ShapeDtypeStruct((bsz, t_len, NA_WIDTH), BF16),
        scratch_shapes=[tok_s, tok_s, tok_s, tok_s, ctx_s, ctx_s,
                        pltpu.VMEM((2, 2 * NA_ROWS - 2, GRID_W, 2 * GRID_W), F32)],
        compiler_params=_params(2),
        name="na_latent",
    )(u_na, u_na, u_na, k_ctx, v_ctx, g2(q_g), g2(k_g), diags)


LATE_WEIGHTS = ("w_o_rwkv", "w_o_na", "w_out", "ffn_w1", "ffn_w3", "ffn_w2")


def _out_ffn_kernel(xa_ref, xb_ref, orwa_ref, orwb_ref, onaa_ref, onab_ref, gt_ref, mod_ref, g_ref,
                    wor_ref, won_ref, wout_ref, w1_ref, w3_ref, w2_ref, ya_ref, yb_ref, *, tiles_a):
    i = pl.program_id(0)
    x = _pick_group(i, tiles_a, xa_ref, xb_ref)
    o_rw = _pick_group(i, tiles_a, orwa_ref, orwb_ref)
    o_na = _pick_group(i, tiles_a, onaa_ref, onab_ref)
    g_rw = _sigmoid(gt_ref[:, :D_MODEL].astype(F32))
    g_na = _sigmoid(gt_ref[:, D_MODEL:].astype(F32))
    merged = g_rw * _dot(o_rw, wor_ref[...]) + g_na * _dot(o_na, won_ref[...])
    x1 = x + _mod_part(mod_ref, 2) * _dot(merged, wout_ref[...])
    h2 = _rms_rows(x1) * g_ref[...]
    h2 = (h2 * (1.0 + _mod_part(mod_ref, 4)) + _mod_part(mod_ref, 3)).astype(BF16)
    acc = jnp.zeros(x1.shape, F32)
    for c in range(FF_HIDDEN // FF_CHUNK):
        cols = slice(c * FF_CHUNK, (c + 1) * FF_CHUNK)
        a = jnp.dot(h2, w1_ref[:, cols], preferred_element_type=F32)
        b = jnp.dot(h2, w3_ref[:, cols], preferred_element_type=F32)
        hh = (a * _sigmoid(a) * b).astype(BF16)
        acc = acc + jnp.dot(hh, w2_ref[cols, :], preferred_element_type=F32)
    y = x1 + _mod_part(mod_ref, 5) * acc

    @pl.when(i < tiles_a)
    def _():
        ya_ref[0] = y

    @pl.when(i >= tiles_a)
    def _():
        yb_ref[0] = y


def _out_ffn(xa, xb, orw_a, orw_b, ona_a, ona_b, gates, tiles_per_b, mod_all, norm_g, wb):
    tm = TOKEN_TILE
    tiles_a, tiles = xa.shape[0], xa.shape[0] + xb.shape[0]
    weights = [wb[n] for n in LATE_WEIGHTS]
    groups = lambda width: _group_specs(tiles_a, tm, width)
    return pl.pallas_call(
        functools.partial(_out_ffn_kernel, tiles_a=tiles_a),
        grid=(tiles,),
        in_specs=groups(D_MODEL) + groups(RW_WIDTH) + groups(NA_WIDTH)
        + [pl.BlockSpec((tm, GATE_COLS), lambda i: (i, 0)), _mod_spec(tiles_a, tiles_per_b),
           _const_spec((1, D_MODEL))] + [_const_spec(w.shape) for w in weights],
        out_specs=groups(D_MODEL),
        out_shape=[jax.ShapeDtypeStruct(xa.shape, F32), jax.ShapeDtypeStruct(xb.shape, F32)],
        compiler_params=_params(1),
        name="out_ffn",
    )(xa, xb, orw_a, orw_b, ona_a, ona_b, gates, mod_all, norm_g.reshape(1, -1), *weights)


def kernel(x_prompt, x_sample, state_rwkv, cache_na_k, cache_na_v, c, c_ctx, norm1_g, norm2_g, w_ada, b_ada,
           w_in, shift_mu, rw_w0, rw_w_up, rw_a0, rw_a_up, rw_g_up, rw_k_k, rw_k_a, rw_r_k, rw_ln_g, rw_ln_b,
           na_q_g, na_k_g, na_rpb, w_o_rwkv, w_o_na, w_out, ffn_w1, ffn_w3, ffn_w2):
    depth = w_in.shape[0]
    bsz, seq = x_prompt.shape[:2]
    dec, dec_seq = x_sample.shape[:2]
    tm = TOKEN_TILE
    n_ctx = bsz * seq
    assert n_ctx % tm == 0 and dec_seq % tm == 0 and n_ctx % dec_seq == 0
    tiles_per_b = dec_seq // tm
    tiled = lambda t: t.reshape(-1, tm, t.shape[-1])
    cvecs = jnp.concatenate([c_ctx[None, :], c], axis=0).T
    y_p, y_s = x_prompt, x_sample
    new_s, new_k, new_v = [], [], []
    for l in range(depth):
        p = dict(norm1_g=norm1_g[l], norm2_g=norm2_g[l], shift_mu=shift_mu[l], rw_w0=rw_w0[l],
                 rw_w_up=rw_w_up[l], rw_a0=rw_a0[l], rw_a_up=rw_a_up[l], rw_g_up=rw_g_up[l],
                 rw_k_k=rw_k_k[l], rw_k_a=rw_k_a[l], rw_r_k=rw_r_k[l], rw_ln_g=rw_ln_g[l],
                 rw_ln_b=rw_ln_b[l], na_q_g=na_q_g[l], na_k_g=na_k_g[l], na_rpb=na_rpb[l])
        late_f32 = [w[l] for w in (w_o_rwkv, w_o_na, w_out, ffn_w1, ffn_w3, ffn_w2)]
        mod_all, w_in_bf = _modulation(cvecs, w_ada[l], b_ada[l], w_in[l])
        u_rw, u_na, gates = _in_proj(tiled(y_p), tiled(y_s), tiles_per_b, mod_all, p["norm1_g"], w_in_bf)
        ctx_view = lambda t: t.reshape(-1, seq, t.shape[-1])
        lat_view = lambda t: t.reshape(-1, dec_seq, t.shape[-1])
        lat_off = n_ctx // dec_seq
        o_rw_p, s_l, casted, (o_na_p, k_l, v_l) = _rwkv_branch(
            ctx_view(u_rw), bsz, 0, None, p, RWKV_PAIRS_CTX, RWKV_UNITS, late_f32,
            attend=(ctx_view(u_na), p["na_q_g"], p["na_k_g"]))
        late = dict(zip(LATE_WEIGHTS, casted))
        new_s.append(s_l)
        new_k.append(k_l.reshape(bsz, seq, NA_HEADS, HEAD_DIM))
        new_v.append(v_l.reshape(bsz, seq, NA_HEADS, HEAD_DIM))
        ctx_k = jnp.transpose(cache_na_k[:, l], (0, 2, 3, 1)).reshape(dec, NA_WIDTH, -1)
        ctx_v = jnp.transpose(cache_na_v[:, l], (0, 2, 3, 1)).reshape(dec, NA_WIDTH, -1)
        o_rw_s = _rwkv_branch(lat_view(u_rw), dec, lat_off, state_rwkv[:, l], p, RWKV_PAIRS_LAT,
                               RWKV_UNITS)[0]
        o_na_s = _na_latent(lat_view(u_na), dec, lat_off, ctx_k, ctx_v, p["na_q_g"], p["na_k_g"], p["na_rpb"])
        y_p_t, y_s_t = _out_ffn(tiled(y_p), tiled(y_s), tiled(o_rw_p), tiled(o_rw_s), tiled(o_na_p), tiled(o_na_s),
                                gates, tiles_per_b, mod_all, p["norm2_g"], late)
        y_p, y_s = y_p_t.reshape(x_prompt.shape), y_s_t.reshape(x_sample.shape)
    return (y_p, y_s, jnp.stack(new_s, axis=1), jnp.stack(new_k, axis=1), jnp.stack(new_v, axis=1))
```

```python
import functools

import numpy as np
import jax
import jax.numpy as jnp
from jax import lax
from jax.experimental import pallas as pl
from jax.experimental.pallas import tpu as pltpu

D_MODEL = 1024
GRID_W = 64
HEAD_DIM = 64
RW_HEADS = 8
RW_WIDTH = RW_HEADS * HEAD_DIM
NA_HEADS = 8
NA_WIDTH = NA_HEADS * HEAD_DIM
LORA_DECAY = 64
LORA_ICLR = 64
LORA_GATE = 128
NA_ROWS = 8
NA_COLS = 16
FF_HIDDEN = 2816
RW_COLS = 3 * RW_WIDTH + 2 * LORA_DECAY + 2 * LORA_ICLR + LORA_GATE
NA_IN_COLS = 3 * NA_WIDTH
GATE_COLS = 2 * D_MODEL
RMS_EPS = 1e-6
GN_EPS = 64e-5
L2_EPS = 1e-12
NEG_INF = -1e30
DECAY_SCALE = float(np.exp(-0.5))
QK_SCALE = HEAD_DIM ** -0.5
assert QK_SCALE == 0.125

LANES = 128
PAIRS = RW_HEADS // 2
CHUNK = 64
STACK = 2 * CHUNK
RWKV_PAIRS_CTX = 4
RWKV_PAIRS_LAT = 2
RWKV_UNITS = 16
TOKEN_TILE = 512
FF_CHUNK = 256
VMEM_LIMIT = 56 * 1024 * 1024

F32 = jnp.float32
BF16 = jnp.bfloat16


def _dot(a, b):
    return jnp.dot(a.astype(BF16), b.astype(BF16), preferred_element_type=F32)


def _split2(x):
    hi = x.astype(BF16)
    lo = (x - hi.astype(F32)).astype(BF16)
    return hi, lo


def _dot_exact_lhs(a_exact, b):
    h, l = _split2(b)
    d = lambda x: jnp.dot(a_exact, x, preferred_element_type=F32)
    return d(h) + d(l)


def _head_ones():
    r = lax.broadcasted_iota(jnp.int32, (LANES, LANES), 0) // HEAD_DIM
    c = lax.broadcasted_iota(jnp.int32, (LANES, LANES), 1) // HEAD_DIM
    return jnp.where(r == c, 1.0, 0.0).astype(BF16)


def _head_sum(x, ones):
    return jnp.dot(x.astype(BF16), ones, preferred_element_type=F32)


def _sigmoid(x):
    return 0.5 * jnp.tanh(0.5 * x) + 0.5


def _rms_rows(x):
    return x * lax.rsqrt(jnp.mean(x * x, axis=-1, keepdims=True) + RMS_EPS)


def _const_spec(shape):
    nd = len(shape)
    return pl.BlockSpec(shape, lambda *_: (0,) * nd, pipeline_mode=pl.Buffered(1))


def _params(n_axes):
    return pltpu.CompilerParams(dimension_semantics=("arbitrary",) * n_axes,
                                vmem_limit_bytes=VMEM_LIMIT)


def _mod_kernel(c_ref, w_ref, b_ref, win_ref, o_ref, winb_ref):
    s = c_ref[...]
    s = s * _sigmoid(s)
    w = w_ref[...]
    for r in range(s.shape[1]):
        o_ref[r] = jnp.sum(w * s[:, r:r + 1], axis=0, keepdims=True) + b_ref[...]
    winb_ref[...] = win_ref[...].astype(BF16)


def _modulation(cvecs, w_ada, b_ada, w_in):
    n = cvecs.shape[1]
    tn = 768
    steps = 6 * D_MODEL // tn
    rows = w_in.shape[0] // steps
    assert rows % 16 == 0
    return pl.pallas_call(
        _mod_kernel,
        grid=(steps,),
        in_specs=[pl.BlockSpec((D_MODEL, n), lambda j: (0, 0)),
                  pl.BlockSpec((D_MODEL, tn), lambda j: (0, j)),
                  pl.BlockSpec((1, tn), lambda j: (0, j)),
                  pl.BlockSpec((rows, w_in.shape[1]), lambda j: (j, 0))],
        out_specs=[pl.BlockSpec((n, 1, tn), lambda j: (0, 0, j)),
                   pl.BlockSpec((rows, w_in.shape[1]), lambda j: (j, 0))],
        out_shape=[jax.ShapeDtypeStruct((n, 1, 6 * D_MODEL), F32),
                   jax.ShapeDtypeStruct(w_in.shape, BF16)],
        compiler_params=_params(1),
        name="modulation",
    )(cvecs, w_ada, b_ada.reshape(1, -1), w_in)


def _pick_group(i, tiles_a, a_ref, b_ref):
    return jnp.where(i < tiles_a, a_ref[0], b_ref[0])


def _group_specs(tiles_a, tm, width):
    return [pl.BlockSpec((1, tm, width), lambda i: (jnp.minimum(i, tiles_a - 1), 0, 0)),
            pl.BlockSpec((1, tm, width), lambda i: (jnp.maximum(i - tiles_a, 0), 0, 0))]


def _mod_spec(tiles_a, tiles_per_b):
    return pl.BlockSpec((1, 1, 6 * D_MODEL),
                        lambda i: (jnp.where(i < tiles_a, 0, 1 + (i - tiles_a) // tiles_per_b), 0, 0))


def _mod_part(mod_ref, k):
    return mod_ref[0, :, k * D_MODEL:(k + 1) * D_MODEL]


def _inproj_kernel(xa_ref, xb_ref, mod_ref, g_ref, w_ref, urw_ref, una_ref, gt_ref, *, tiles_a):
    x = _pick_group(pl.program_id(0), tiles_a, xa_ref, xb_ref)
    h = _rms_rows(x) * g_ref[...]
    h = (h * (1.0 + _mod_part(mod_ref, 1)) + _mod_part(mod_ref, 0)).astype(BF16)
    d = lambda lo, hi: jnp.dot(h, w_ref[:, lo:hi], preferred_element_type=F32)
    urw_ref[...] = d(0, RW_COLS)
    una_ref[...] = d(RW_COLS, RW_COLS + NA_IN_COLS)
    gt_ref[...] = d(RW_COLS + NA_IN_COLS, RW_COLS + NA_IN_COLS + GATE_COLS).astype(BF16)


def _in_proj(xa, xb, tiles_per_b, mod_all, norm_g, w_in_bf):
    tm = TOKEN_TILE
    tiles_a, tiles = xa.shape[0], xa.shape[0] + xb.shape[0]
    row = lambda i: (i, 0)
    return pl.pallas_call(
        functools.partial(_inproj_kernel, tiles_a=tiles_a),
        grid=(tiles,),
        in_specs=_group_specs(tiles_a, tm, D_MODEL) + [_mod_spec(tiles_a, tiles_per_b),
                                                       _const_spec((1, D_MODEL)), _const_spec(w_in_bf.shape)],
        out_specs=[pl.BlockSpec((tm, RW_COLS), row),
                   pl.BlockSpec((tm, NA_IN_COLS), row),
                   pl.BlockSpec((tm, GATE_COLS), row)],
        out_shape=[jax.ShapeDtypeStruct((tiles * tm, RW_COLS), F32),
                   jax.ShapeDtypeStruct((tiles * tm, NA_IN_COLS), F32),
                   jax.ShapeDtypeStruct((tiles * tm, GATE_COLS), BF16)],
        compiler_params=_params(1),
        name="in_proj",
    )(xa, xb, mod_all, norm_g.reshape(1, -1), w_in_bf)


def _shift(x, mu):
    t_len = x.shape[0]
    row = lax.broadcasted_iota(jnp.int32, x.shape, 0)
    prev = jnp.where(row == 0, 0.0, pltpu.roll(x, 1, 0))
    nxt = jnp.where(row == t_len - 1, 0.0, pltpu.roll(x, t_len - 1, 0))
    return x + mu[0:1, :] * (prev - x) + mu[1:2, :] * (nxt - x)


def _stack_heads(x, lane_lo):
    return jnp.concatenate([x * lane_lo, x * (1.0 - lane_lo)], axis=0)


def _wkv_intra(units, consts):
    tri, mask_s, mask_i, eye, lane_lo, blk = consts
    stack = lambda z: _stack_heads(z, lane_lo)

    cums = [_dot_exact_lhs(tri[int(u[6])], u[1]) for u in units]

    prep = []
    for (r, lw, kd, v, kk, b, reverse), cum in zip(units, cums):
        mid_row = CHUNK // 2 if reverse else CHUNK // 2 - 1
        tot_row = 0 if reverse else CHUNK - 1
        a = -kk
        ex = cum - lw
        mid = cum[mid_row:mid_row + 1, :]
        tot = cum[tot_row:tot_row + 1, :]
        up = jnp.exp(cum - mid)
        dn = jnp.exp(mid - cum)
        tail = jnp.exp(tot - cum)
        prep.append(dict(
            at_m=stack(a * jnp.exp(ex - mid)).astype(BF16),
            rt_m=stack(r * up).astype(BF16),
            bk_m=jnp.concatenate([b * dn, kd * dn], axis=0).astype(BF16),
            a_e=stack(a * jnp.exp(ex)),
            r_e=stack(r * jnp.exp(cum)),
            bk_t=jnp.concatenate([stack(b * tail), stack(kd * tail)], axis=0).T.astype(BF16),
            vv=stack(v).astype(BF16),
            diag=jnp.where(eye, jnp.exp(tot), 0.0),
            rev=int(reverse)))

    ntd = lambda x, y: lax.dot_general(x, y, (((1,), (1,)), ((), ())), preferred_element_type=F32)
    mm = lambda x, y: jnp.dot(x, y, preferred_element_type=F32)
    diag_blk, swap_eye = blk
    roll_head = lambda z: pltpu.roll(z, HEAD_DIM, 1)
    by_block = lambda z: jnp.concatenate([z[:CHUNK], roll_head(z[CHUNK:])], axis=0)
    top = [by_block(ntd(p["at_m"], p["bk_m"])) for p in prep]
    low = [by_block(ntd(p["rt_m"], p["bk_m"])) for p in prep]
    a_ak = [(roll_head(t) * mask_s[p["rev"]]).astype(BF16) for t, p in zip(top, prep)]
    bot = [jnp.concatenate([t * mask_i[p["rev"]], roll_head(t) * mask_i[p["rev"]]], axis=1).astype(BF16)
           for t, p in zip(low, prep)]

    off_blk = 1.0 - diag_blk
    both = [t * mask_s[p["rev"]] + swap_eye for t, p in zip(top, prep)]
    steps = CHUNK.bit_length() - 1
    diag_bf = diag_blk.astype(BF16)
    for j in range(steps):
        packed = [q.astype(BF16) for q in both]
        res = [mm(qb * diag_bf, qb) for qb in packed]
        both = [r + off_blk * q for r, q in zip(res, both)]
    ts = [pltpu.roll(q, HEAD_DIM, 1).astype(BF16) for q in both]
    x0 = [jnp.concatenate([p["a_e"], mm(ak, p["vv"])], axis=1) for p, ak in zip(prep, a_ak)]
    xs = [mm(t, x.astype(BF16)) for x, t in zip(x0, ts)]

    out = []
    zeros = jnp.zeros((STACK, LANES), BF16)
    for p, x, bt in zip(prep, xs, bot):
        rhs = jnp.concatenate([x.astype(BF16), jnp.concatenate([zeros, p["vv"]], axis=1)], axis=0)
        lhs = jnp.concatenate([bt, p["bk_t"]], axis=0)
        res = mm(lhs, rhs)
        lhs2 = res[:, :LANES] + jnp.concatenate([p["r_e"], p["diag"]], axis=0)
        out.append((lhs2.astype(BF16), res[:, LANES:]))
    return out


def _wkv_constants():
    lane_head = np.arange(LANES) // HEAD_DIM
    ones = (lane_head[:, None] == lane_head[None, :]).astype(np.float32)
    t = np.arange(CHUNK)
    tri = np.stack([t[None, :] <= t[:, None], t[None, :] >= t[:, None]]).astype(np.float32)
    rs, cs = t[:, None], t[None, :]
    tri_masks = np.stack([cs < rs, cs > rs, cs <= rs, cs >= rs]).astype(np.float32)
    masks = np.kron(np.eye(2, dtype=np.float32), tri_masks)
    return jnp.asarray(ones, BF16), jnp.asarray(tri, BF16), jnp.asarray(masks, F32)


def _rwkv_kernel(*refs, t_len, has_s0, pairs, units, n_cast, attend):
    (r_ref, k_ref, v_ref, lo_ref, mur_ref, muk_ref, muv_ref, mul_ref, w0_ref, a0_ref, wup_ref, aup_ref,
     gup_ref, kk_ref, ka_ref, rk_ref, lng_ref, lnb_ref, ones_ref, tri_ref, mask_ref) = refs[:21]
    pos = 21
    s0_ref = None
    if has_s0:
        s0_ref = refs[pos]
        pos += 1
    cast_in = refs[pos:pos + n_cast]
    pos += n_cast
    n_att_in, n_att_out = (5, 3) if attend else (0, 0)
    att_in = refs[pos:pos + n_att_in]
    pos += n_att_in
    o_ref, sn_ref = refs[pos], refs[pos + 1]
    cast_out = refs[pos + 2:pos + 2 + n_cast]
    pos += 2 + n_cast
    att_out = refs[pos:pos + n_att_out]
    (r_s, v_s, kk_s, b0_s, b1_s, lw0_s, lw1_s, kd0_s, kd1_s, gate_s, bonus_s, yf_s, yb_s,
     lhs_s, add_s, st_s) = refs[pos + n_att_out:]
    if attend:
        _na_ctx_kernel(*att_in, *att_out)
    for w_in_ref, w_out_ref in zip(cast_in, cast_out):
        w_out_ref[...] = w_in_ref[...].astype(BF16)

    ones = ones_ref[...]
    lane = lax.broadcasted_iota(jnp.int32, (1, LANES), 1)
    lo_half = lane < HEAD_DIM
    lane_lo = jnp.where(lo_half, 1.0, 0.0)
    mm = lambda x, y: jnp.dot(x, y, preferred_element_type=F32)

    lo = _shift(lo_ref[0], mul_ref[...])
    wd = jnp.tanh(lo[:, 0:LANES])
    ad = lo[:, LANES:2 * LANES]
    sig_gd = _sigmoid(lo[:, 2 * LANES:3 * LANES]).astype(BF16)
    wd_split = [_split2(wd * m) for m in (lane_lo, 1.0 - lane_lo)]
    ad_bf = [(ad * m).astype(BF16) for m in (lane_lo, 1.0 - lane_lo)]
    for j in range(pairs):
        cols = slice(j * LANES, (j + 1) * LANES)
        r = _shift(r_ref[0, :, cols], mur_ref[:, cols])
        k = _shift(k_ref[0, :, cols], muk_ref[:, cols])
        v = _shift(v_ref[0, :, cols], muv_ref[:, cols])
        kk = k * kk_ref[:, cols]
        kk = kk * lax.rsqrt(_head_sum(kk * kk, ones) + L2_EPS)
        wup_h, wup_l = _split2(wup_ref[:, cols])
        aup = aup_ref[:, cols].astype(BF16)
        kdirs = []
        for e, (lw_s, kd_s, b_s) in enumerate(((lw0_s, kd0_s, b0_s), (lw1_s, kd1_s, b1_s))):
            wd_h, wd_l = wd_split[e]
            w_lin = w0_ref[e:e + 1, cols] + (mm(wd_h, wup_h) + mm(wd_l, wup_h) + mm(wd_h, wup_l))
            lw_s[j] = -DECAY_SCALE * _sigmoid(w_lin)
            iclr = _sigmoid(a0_ref[e:e + 1, cols] + mm(ad_bf[e], aup))
            kd = k * (1.0 + (iclr - 1.0) * ka_ref[:, cols])
            kd_s[j] = kd
            b_s[j] = kk * iclr
            kdirs.append(kd)
        gate_s[:, cols] = mm(sig_gd, gup_ref[:, cols].astype(BF16))
        h0 = 2 * (pl.program_id(1) * pairs + j)
        rk = jnp.concatenate([rk_ref[pl.ds(h0, 1), :], rk_ref[pl.ds(h0 + 1, 1), :]], axis=1)
        bonus_s[:, cols] = _head_sum(r * (0.5 * (kdirs[0] + kdirs[1])) * rk, ones) * v
        r_s[j] = r
        v_s[j] = v
        kk_s[j] = kk

    n_chunks = t_len // CHUNK
    chunks_per = min(n_chunks, units // 2)
    pairs_per = min(pairs, units // (2 * chunks_per))
    groups = n_chunks // chunks_per
    rs = lax.broadcasted_iota(jnp.int32, (STACK, STACK), 0)
    cs = lax.broadcasted_iota(jnp.int32, (STACK, STACK), 1)
    as_f32 = lambda m: jnp.where(m, 1.0, 0.0)
    blk = (as_f32(rs // CHUNK == cs // CHUNK), as_f32(cs == (rs + CHUNK) % STACK))
    consts = ((tri_ref[0], tri_ref[1]), (mask_ref[0], mask_ref[1]), (mask_ref[2], mask_ref[3]),
              rs == cs, lane_lo, blk)
    dirs = ((lw0_s, kd0_s, b0_s), (lw1_s, kd1_s, b1_s))

    def intra_body(it, carry):
        pg = it // groups
        g = it % groups
        units_, ids = [], []
        for jj in range(pairs_per):
            j = pg * pairs_per + jj
            for cc in range(chunks_per):
                c = g * chunks_per + cc
                rows = pl.ds(pl.multiple_of(c * CHUNK, CHUNK), CHUNK)
                for e, (lw_s, kd_s, b_s) in enumerate(dirs):
                    units_.append((r_s[j, rows, :], lw_s[j, rows, :], kd_s[j, rows, :], v_s[j, rows, :],
                                   kk_s[j, rows, :], b_s[j, rows, :], e == 1))
                    ids.append((j * 2 + e) * n_chunks + c)
        for uid, (lhs, add) in zip(ids, _wkv_intra(units_, consts)):
            lhs_s[uid] = lhs
            add_s[uid] = add
        return carry

    lax.fori_loop(0, (pairs // pairs_per) * groups, intra_body, 0, unroll=2)

    same_head = rs // HEAD_DIM == cs // HEAD_DIM
    for j in range(pairs):
        for e in range(2):
            if has_s0:
                both = jnp.concatenate([s0_ref[0, e, 2 * j], s0_ref[0, e, 2 * j + 1]], axis=1)
                st_s[2 * j + e] = jnp.where(same_head, jnp.concatenate([both, both], axis=0), 0.0).T
            else:
                st_s[2 * j + e] = jnp.zeros((LANES, LANES), F32)

    def state_body(it, carry):
        chunk = (it, n_chunks - 1 - it)
        uids = [(j * 2 + e) * n_chunks + chunk[e] for j in range(pairs) for e in range(2)]
        sts = [st_s[ch].astype(BF16) for ch in range(2 * pairs)]
        res = [mm(lhs_s[uid], st) + add_s[uid] for uid, st in zip(uids, sts)]
        for ch, rr in enumerate(res):
            j, e = divmod(ch, 2)
            y_s = yb_s if e else yf_s
            y_s[j, pl.ds(pl.multiple_of(chunk[e] * CHUNK, CHUNK), CHUNK), :] = rr[:CHUNK] + rr[CHUNK:STACK]
            st_s[ch] = rr[STACK:]
        return carry

    lax.fori_loop(0, n_chunks, state_body, 0, unroll=True)
    for j in range(pairs):
        for e in range(2):
            st_t = st_s[2 * j + e].T
            sn_ref[0, e, 2 * j] = st_t[:HEAD_DIM, :HEAD_DIM]
            sn_ref[0, e, 2 * j + 1] = pltpu.roll(st_t, HEAD_DIM, 1)[HEAD_DIM:, :HEAD_DIM]

    inv_d = 1.0 / HEAD_DIM
    for j in range(pairs):
        cols = slice(j * LANES, (j + 1) * LANES)
        y = yf_s[j] + yb_s[j]
        mean = _head_sum(y, ones) * inv_d
        dlt = y - mean
        var = _head_sum(dlt * dlt, ones) * inv_d
        yn = dlt * lax.rsqrt(var + GN_EPS) * lng_ref[:, cols] + lnb_ref[:, cols]
        o_ref[0, :, cols] = ((yn + bonus_s[:, cols]) * gate_s[:, cols]).astype(o_ref.dtype)


def _rwkv_branch(u_rw, bsz, b_off, s0, p, pairs, units, cast=(), attend=None):
    t_len = u_rw.shape[1]
    has_s0 = s0 is not None
    width = pairs * LANES
    seg = RW_WIDTH // width
    tok = lambda off: pl.BlockSpec((1, t_len, width), lambda b, j: (b_off + b, 0, off + j))
    mu = lambda off: pl.BlockSpec((2, width), lambda b, j: (0, off + j))
    vec2 = pl.BlockSpec((2, width), lambda b, j: (0, j))
    vec1 = pl.BlockSpec((1, width), lambda b, j: (0, j))
    mat = pl.BlockSpec((LANES, width), lambda b, j: (0, j))
    lora_w = 3 * LANES
    lora_blk = 3 * RW_WIDTH // lora_w
    in_specs = [tok(0), tok(seg), tok(2 * seg),
                pl.BlockSpec((1, t_len, lora_w), lambda b, j: (b_off + b, 0, lora_blk)),
                mu(0), mu(seg), mu(2 * seg),
                pl.BlockSpec((2, lora_w), lambda b, j: (0, lora_blk)),
                vec2, vec2, mat, mat, mat, vec1, vec1, _const_spec((RW_HEADS, HEAD_DIM)), vec1, vec1,
                _const_spec((LANES, LANES)), _const_spec((2, CHUNK, CHUNK)), _const_spec((4, STACK, STACK))]
    args = [u_rw, u_rw, u_rw, u_rw, p["shift_mu"], p["shift_mu"], p["shift_mu"], p["shift_mu"],
            p["rw_w0"], p["rw_a0"],
            p["rw_w_up"].reshape(2 * LORA_DECAY, RW_WIDTH), p["rw_a_up"].reshape(2 * LORA_ICLR, RW_WIDTH),
            p["rw_g_up"], p["rw_k_k"].reshape(1, -1), p["rw_k_a"].reshape(1, -1),
            p["rw_r_k"], p["rw_ln_g"].reshape(1, -1), p["rw_ln_b"].reshape(1, -1),
            *_wkv_constants()]
    st_spec = pl.BlockSpec((1, 2, 2 * pairs, HEAD_DIM, HEAD_DIM), lambda b, j: (b, 0, j, 0, 0))
    if has_s0:
        in_specs.append(st_spec)
        args.append(s0)
    out_specs = [pl.BlockSpec((1, t_len, width), lambda b, j: (b, 0, j)),
                 pl.BlockSpec((1, 2, 2 * pairs, HEAD_DIM, HEAD_DIM), lambda b, j: (b, 0, j, 0, 0))]
    out_shape = [jax.ShapeDtypeStruct((bsz, t_len, RW_WIDTH), BF16),
                 jax.ShapeDtypeStruct((bsz, 2, RW_HEADS, HEAD_DIM, HEAD_DIM), F32)]
    for w in cast:
        assert PAIRS == pairs and w.shape[0] % (16 * bsz) == 0
        blk = pl.BlockSpec((w.shape[0] // bsz, w.shape[1]), lambda b, j: (b, 0))
        in_specs.append(blk)
        args.append(w)
        out_specs.append(blk)
        out_shape.append(jax.ShapeDtypeStruct(w.shape, BF16))
    if attend is not None:
        u_na, q_g, k_g = attend
        assert PAIRS == pairs
        na_tok = lambda seg_: pl.BlockSpec((1, t_len, NA_WIDTH), lambda b, j: (b_off + b, 0, seg_))
        na_out = pl.BlockSpec((1, t_len, NA_WIDTH), lambda b, j: (b, 0, 0))
        g2 = lambda g: g.reshape(1, HEAD_DIM)
        in_specs += [na_tok(0), na_tok(1), na_tok(2), _const_spec((1, HEAD_DIM)), _const_spec((1, HEAD_DIM))]
        args += [u_na, u_na, u_na, g2(q_g), g2(k_g)]
        na_out_t = pl.BlockSpec((1, NA_WIDTH, t_len), lambda b, j: (b, 0, 0))
        out_specs += [na_out, na_out_t, na_out_t]
        na_shape = (bsz, t_len, NA_WIDTH)
        na_shape_t = (bsz, NA_WIDTH, t_len)
        out_shape += [jax.ShapeDtypeStruct(na_shape, BF16), jax.ShapeDtypeStruct(na_shape_t, F32),
                      jax.ShapeDtypeStruct(na_shape_t, F32)]
    n_units = 2 * pairs * (t_len // CHUNK)
    per_pair = pltpu.VMEM((pairs, t_len, LANES), F32)
    full = pltpu.VMEM((t_len, width), F32)
    scratch = [per_pair] * 9 + [full, full, per_pair, per_pair,
                                pltpu.VMEM((n_units, 2 * STACK, LANES), BF16),
                                pltpu.VMEM((n_units, 2 * STACK, LANES), F32),
                                pltpu.VMEM((2 * pairs, LANES, LANES), F32)]
    o_rw, s_new, *rest = pl.pallas_call(
        functools.partial(_rwkv_kernel, t_len=t_len, has_s0=has_s0, pairs=pairs, units=units, n_cast=len(cast),
                          attend=attend is not None),
        grid=(bsz, PAIRS // pairs),
        in_specs=in_specs,
        out_specs=out_specs,
        out_shape=out_shape,
        scratch_shapes=scratch,
        compiler_params=_params(2),
        name="rwkv_branch",
    )(*args)
    return o_rw, s_new, rest[:len(cast)], rest[len(cast):]


def _qk_norm(t, g, ones):
    ms = _head_sum(t * t, ones) * (1.0 / HEAD_DIM)
    return t * lax.rsqrt(ms + RMS_EPS) * jnp.concatenate([g, g], axis=1)


def _nt(x, y):
    return lax.dot_general(x, y, (((1,), (1,)), ((), ())), preferred_element_type=F32)


def _na_ctx_kernel(q_ref, k_ref, v_ref, qg_ref, kg_ref, o_ref, kn_ref, vc_ref):
    ones = _head_ones()
    lo_half = lax.broadcasted_iota(jnp.int32, (1, LANES), 1) < HEAD_DIM
    lo = jnp.where(lo_half, 1.0, 0.0)
    t_len = q_ref.shape[1]
    qs, ks, vs = [], [], []
    for j in range(NA_WIDTH // LANES):
        cols = slice(j * LANES, (j + 1) * LANES)
        qn = _qk_norm(q_ref[0, :, cols], qg_ref[...], ones)
        kn = _qk_norm(k_ref[0, :, cols], kg_ref[...], ones)
        v = v_ref[0, :, cols]
        kn_ref[0, cols, :] = kn.T
        vc_ref[0, cols, :] = v.T
        qn = qn * QK_SCALE
        qs.append(jnp.concatenate([qn * lo, qn * (1.0 - lo)], axis=0).astype(BF16))
        ks.append(kn.astype(BF16))
        vs.append(v.astype(BF16))
    logits = [_nt(q, k) for q, k in zip(qs, ks)]
    ms = [jnp.max(s, axis=-1, keepdims=True) for s in logits]
    ps = [jnp.exp(s - m) for s, m in zip(logits, ms)]
    ls = [jnp.sum(p, axis=-1, keepdims=True) for p in ps]
    outs = [jnp.dot(p.astype(BF16), v, preferred_element_type=F32) / l for p, v, l in zip(ps, vs, ls)]
    for j, o in enumerate(outs):
        o_ref[0, :, j * LANES:(j + 1) * LANES] = jnp.where(lo_half, o[:t_len], o[t_len:]).astype(o_ref.dtype)


NA_ROW_ILP = 4


def _na_lat_kernel(q_ref, k_ref, v_ref, kc_ref, vc_ref, qg_ref, kg_ref, diag_ref, o_ref,
                   q0_s, q1_s, kn_s, v_s, kc_s, vc_s, tab_ref, *, rows, kr):
    ones = _head_ones()
    lo_half = lax.broadcasted_iota(jnp.int32, (1, LANES), 1) < HEAD_DIM
    lo = jnp.where(lo_half, 1.0, 0.0)
    q_col = lax.broadcasted_iota(jnp.int32, (GRID_W, 2 * GRID_W), 0)
    k_col = lax.broadcasted_iota(jnp.int32, (GRID_W, 2 * GRID_W), 1) % GRID_W
    w_start = jnp.clip(q_col - NA_COLS // 2, 0, GRID_W - NA_COLS)
    valid = (k_col >= w_start) & (k_col < w_start + NA_COLS)
    for h in range(2):
        for d in range(2 * NA_ROWS - 2):
            diag = jnp.broadcast_to(diag_ref[h, d:d + 1, :], (GRID_W, 2 * GRID_W))
            tab_ref[h, d] = jnp.where(valid, pltpu.roll(diag, 0, 1, stride=1, stride_axis=0), NEG_INF)
    qn = _qk_norm(q_ref[0], qg_ref[...], ones) * QK_SCALE
    q0_s[...] = (qn * lo).astype(BF16)
    q1_s[...] = (qn * (1.0 - lo)).astype(BF16)
    kn_s[...] = _qk_norm(k_ref[0], kg_ref[...], ones).astype(BF16)
    v_s[...] = v_ref[0].astype(BF16)
    kc_s[...] = kc_ref[0].astype(BF16)
    vc_s[...] = vc_ref[0].astype(BF16)
    win = kr * GRID_W

    def body(it, carry):
        qs, k_rows, q_rows, biases = [], [], [], []
        for s in range(NA_ROW_ILP):
            i = it * NA_ROW_ILP + s
            r0 = jnp.clip(i - kr // 2, 0, rows - kr)
            d0 = r0 - i + (NA_ROWS - 1)
            qr = pl.ds(pl.multiple_of(i * GRID_W, GRID_W), GRID_W)
            q_rows.append(qr)
            k_rows.append(pl.ds(pl.multiple_of(r0 * GRID_W, GRID_W), win))
            qs.append(jnp.concatenate([q0_s[qr, :], q1_s[qr, :]], axis=0))
            biases.append(jnp.concatenate(
                [jnp.concatenate([tab_ref[h, d0 + 2 * m] for m in range(kr // 2)], axis=1) for h in range(2)],
                axis=0))
        lw = [_nt(q, kn_s[kr_, :]) + b for q, kr_, b in zip(qs, k_rows, biases)]
        lc = [jnp.dot(q, kc_s[...], preferred_element_type=F32) for q in qs]
        ms = [jnp.maximum(jnp.max(a, axis=-1, keepdims=True), jnp.max(c, axis=-1, keepdims=True))
              for a, c in zip(lw, lc)]
        pw = [jnp.exp(a - m) for a, m in zip(lw, ms)]
        pc = [jnp.exp(c - m) for c, m in zip(lc, ms)]
        ls = [jnp.sum(a, axis=-1, keepdims=True) + jnp.sum(c, axis=-1, keepdims=True) for a, c in zip(pw, pc)]
        outs = [(jnp.dot(a.astype(BF16), v_s[kr_, :], preferred_element_type=F32)
                 + _nt(c.astype(BF16), vc_s[...])) / l
                for a, c, kr_, l in zip(pw, pc, k_rows, ls)]
        for qr, o in zip(q_rows, outs):
            o_ref[0, qr, :] = jnp.where(lo_half, o[:GRID_W], o[GRID_W:]).astype(o_ref.dtype)
        return carry

    lax.fori_loop(0, rows // NA_ROW_ILP, body, 0, unroll=True)


def _latent_bias_diagonals(rpb):
    c1 = NA_COLS - 1
    assert 2 * c1 + 1 <= GRID_W - c1
    a, b = rpb[:, :-1], rpb[:, 1:]
    gap = jnp.zeros(a.shape[:2] + (GRID_W - 2 * c1 - 1,), rpb.dtype)
    return jnp.concatenate([a[..., c1:], gap, b, gap, a[..., :c1]], axis=-1)


def _na_latent(u_na, bsz, b_off, k_ctx, v_ctx, q_g, k_g, rpb):
    t_len = u_na.shape[1]
    rows = t_len // GRID_W
    kr = min(NA_ROWS, rows)
    assert kr % 2 == 0 and rows % NA_ROW_ILP == 0
    ctx_len = k_ctx.shape[2]
    seg = NA_WIDTH // LANES
    tok = lambda off: pl.BlockSpec((1, t_len, LANES), lambda b, j: (b_off + b, 0, off + j))
    ctx = pl.BlockSpec((1, LANES, ctx_len), lambda b, j: (b, j, 0))
    g2 = lambda g: g.reshape(1, HEAD_DIM)
    diags = _latent_bias_diagonals(rpb)
    tok_s = pltpu.VMEM((t_len, LANES), BF16)
    ctx_s = pltpu.VMEM((LANES, ctx_len), BF16)
    return pl.pallas_call(
        functools.partial(_na_lat_kernel, rows=rows, kr=kr),
        grid=(bsz, seg),
        in_specs=[tok(0), tok(seg), tok(2 * seg), ctx, ctx,
                  _const_spec((1, HEAD_DIM)), _const_spec((1, HEAD_DIM)),
                  pl.BlockSpec((2, 2 * NA_ROWS - 2, 2 * GRID_W), lambda b, j: (j, 0, 0))],
        out_specs=pl.BlockSpec((1, t_len, LANES), lambda b, j: (b, 0, j)),
        out_shape=jax.ShapeDtypeStruct((bsz, t_len, NA_WIDTH), BF16),
        scratch_shapes=[tok_s, tok_s, tok_s, tok_s, ctx_s, ctx_s,
                        pltpu.VMEM((2, 2 * NA_ROWS - 2, GRID_W, 2 * GRID_W), F32)],
        compiler_params=_params(2),
        name="na_latent",
    )(u_na, u_na, u_na, k_ctx, v_ctx, g2(q_g), g2(k_g), diags)


LATE_WEIGHTS = ("w_o_rwkv", "w_o_na", "w_out", "ffn_w1", "ffn_w3", "ffn_w2")


def _out_ffn_kernel(xa_ref, xb_ref, orwa_ref, orwb_ref, onaa_ref, onab_ref, gt_ref, mod_ref, g_ref,
                    wor_ref, won_ref, wout_ref, w1_ref, w3_ref, w2_ref, ya_ref, yb_ref, *, tiles_a):
    i = pl.program_id(0)
    x = _pick_group(i, tiles_a, xa_ref, xb_ref)
    o_rw = _pick_group(i, tiles_a, orwa_ref, orwb_ref)
    o_na = _pick_group(i, tiles_a, onaa_ref, onab_ref)
    g_rw = _sigmoid(gt_ref[:, :D_MODEL].astype(F32))
    g_na = _sigmoid(gt_ref[:, D_MODEL:].astype(F32))
    merged = g_rw * _dot(o_rw, wor_ref[...]) + g_na * _dot(o_na, won_ref[...])
    x1 = x + _mod_part(mod_ref, 2) * _dot(merged, wout_ref[...])
    h2 = _rms_rows(x1) * g_ref[...]
    h2 = (h2 * (1.0 + _mod_part(mod_ref, 4)) + _mod_part(mod_ref, 3)).astype(BF16)
    acc = jnp.zeros(x1.shape, F32)
    for c in range(FF_HIDDEN // FF_CHUNK):
        cols = slice(c * FF_CHUNK, (c + 1) * FF_CHUNK)
        a = jnp.dot(h2, w1_ref[:, cols], preferred_element_type=F32)
        b = jnp.dot(h2, w3_ref[:, cols], preferred_element_type=F32)
        hh = (a * _sigmoid(a) * b).astype(BF16)
        acc = acc + jnp.dot(hh, w2_ref[cols, :], preferred_element_type=F32)
    y = x1 + _mod_part(mod_ref, 5) * acc

    @pl.when(i < tiles_a)
    def _():
        ya_ref[0] = y

    @pl.when(i >= tiles_a)
    def _():
        yb_ref[0] = y


def _out_ffn(xa, xb, orw_a, orw_b, ona_a, ona_b, gates, tiles_per_b, mod_all, norm_g, wb):
    tm = TOKEN_TILE
    tiles_a, tiles = xa.shape[0], xa.shape[0] + xb.shape[0]
    weights = [wb[n] for n in LATE_WEIGHTS]
    groups = lambda width: _group_specs(tiles_a, tm, width)
    return pl.pallas_call(
        functools.partial(_out_ffn_kernel, tiles_a=tiles_a),
        grid=(tiles,),
        in_specs=groups(D_MODEL) + groups(RW_WIDTH) + groups(NA_WIDTH)
        + [pl.BlockSpec((tm, GATE_COLS), lambda i: (i, 0)), _mod_spec(tiles_a, tiles_per_b),
           _const_spec((1, D_MODEL))] + [_const_spec(w.shape) for w in weights],
        out_specs=groups(D_MODEL),
        out_shape=[jax.ShapeDtypeStruct(xa.shape, F32), jax.ShapeDtypeStruct(xb.shape, F32)],
        compiler_params=_params(1),
        name="out_ffn",
    )(xa, xb, orw_a, orw_b, ona_a, ona_b, gates, mod_all, norm_g.reshape(1, -1), *weights)


def kernel(x_prompt, x_sample, state_rwkv, cache_na_k, cache_na_v, c, c_ctx, norm1_g, norm2_g, w_ada, b_ada,
           w_in, shift_mu, rw_w0, rw_w_up, rw_a0, rw_a_up, rw_g_up, rw_k_k, rw_k_a, rw_r_k, rw_ln_g, rw_ln_b,
           na_q_g, na_k_g, na_rpb, w_o_rwkv, w_o_na, w_out, ffn_w1, ffn_w3, ffn_w2):
    depth = w_in.shape[0]
    bsz, seq = x_prompt.shape[:2]
    dec, dec_seq = x_sample.shape[:2]
    tm = TOKEN_TILE
    n_ctx = bsz * seq
    assert n_ctx % tm == 0 and dec_seq % tm == 0 and n_ctx % dec_seq == 0
    tiles_per_b = dec_seq // tm
    tiled = lambda t: t.reshape(-1, tm, t.shape[-1])
    cvecs = jnp.concatenate([c_ctx[None, :], c], axis=0).T
    y_p, y_s = x_prompt, x_sample
    new_s, new_k, new_v = [], [], []
    for l in range(depth):
        p = dict(norm1_g=norm1_g[l], norm2_g=norm2_g[l], shift_mu=shift_mu[l], rw_w0=rw_w0[l],
                 rw_w_up=rw_w_up[l], rw_a0=rw_a0[l], rw_a_up=rw_a_up[l], rw_g_up=rw_g_up[l],
                 rw_k_k=rw_k_k[l], rw_k_a=rw_k_a[l], rw_r_k=rw_r_k[l], rw_ln_g=rw_ln_g[l],
                 rw_ln_b=rw_ln_b[l], na_q_g=na_q_g[l], na_k_g=na_k_g[l], na_rpb=na_rpb[l])
        late_f32 = [w[l] for w in (w_o_rwkv, w_o_na, w_out, ffn_w1, ffn_w3, ffn_w2)]
        mod_all, w_in_bf = _modulation(cvecs, w_ada[l], b_ada[l], w_in[l])
        u_rw, u_na, gates = _in_proj(tiled(y_p), tiled(y_s), tiles_per_b, mod_all, p["norm1_g"], w_in_bf)
        ctx_view = lambda t: t.reshape(-1, seq, t.shape[-1])
        lat_view = lambda t: t.reshape(-1, dec_seq, t.shape[-1])
        lat_off = n_ctx // dec_seq
        o_rw_p, s_l, casted, (o_na_p, k_l, v_l) = _rwkv_branch(
            ctx_view(u_rw), bsz, 0, None, p, RWKV_PAIRS_CTX, RWKV_UNITS, late_f32,
            attend=(ctx_view(u_na), p["na_q_g"], p["na_k_g"]))
        late = dict(zip(LATE_WEIGHTS, casted))
        new_s.append(s_l)
        heads_last = lambda t: jnp.transpose(t.reshape(bsz, NA_HEADS, HEAD_DIM, seq), (0, 3, 1, 2))
        new_k.append(heads_last(k_l))
        new_v.append(heads_last(v_l))
        ctx_k = jnp.transpose(cache_na_k[:, l], (0, 2, 3, 1)).reshape(dec, NA_WIDTH, -1)
        ctx_v = jnp.transpose(cache_na_v[:, l], (0, 2, 3, 1)).reshape(dec, NA_WIDTH, -1)
        o_rw_s = _rwkv_branch(lat_view(u_rw), dec, lat_off, state_rwkv[:, l], p, RWKV_PAIRS_LAT,
                               RWKV_UNITS)[0]
        o_na_s = _na_latent(lat_view(u_na), dec, lat_off, ctx_k, ctx_v, p["na_q_g"], p["na_k_g"], p["na_rpb"])
        y_p_t, y_s_t = _out_ffn(tiled(y_p), tiled(y_s), tiled(o_rw_p), tiled(o_rw_s), tiled(o_na_p), tiled(o_na_s),
                                gates, tiles_per_b, mod_all, p["norm2_g"], late)
        y_p, y_s = y_p_t.reshape(x_prompt.shape), y_s_t.reshape(x_sample.shape)
    return (y_p, y_s, jnp.stack(new_s, axis=1), jnp.stack(new_k, axis=1), jnp.stack(new_v, axis=1))
```

```python
import functools

import numpy as np
import jax
import jax.numpy as jnp
from jax import lax
from jax.experimental import pallas as pl
from jax.experimental.pallas import tpu as pltpu

D_MODEL = 1024
GRID_W = 64
HEAD_DIM = 64
RW_HEADS = 8
RW_WIDTH = RW_HEADS * HEAD_DIM
NA_HEADS = 8
NA_WIDTH = NA_HEADS * HEAD_DIM
LORA_DECAY = 64
LORA_ICLR = 64
LORA_GATE = 128
NA_ROWS = 8
NA_COLS = 16
FF_HIDDEN = 2816
RW_COLS = 3 * RW_WIDTH + 2 * LORA_DECAY + 2 * LORA_ICLR + LORA_GATE
NA_IN_COLS = 3 * NA_WIDTH
GATE_COLS = 2 * D_MODEL
RMS_EPS = 1e-6
GN_EPS = 64e-5
L2_EPS = 1e-12
NEG_INF = -1e30
DECAY_SCALE = float(np.exp(-0.5))
QK_SCALE = HEAD_DIM ** -0.5
assert QK_SCALE == 0.125

LANES = 128
PAIRS = RW_HEADS // 2
CHUNK = 64
STACK = 2 * CHUNK
RWKV_PAIRS_CTX = 4
RWKV_PAIRS_LAT = 2
RWKV_UNITS = 16
TOKEN_TILE = 512
FF_CHUNK = 256
VMEM_LIMIT = 56 * 1024 * 1024

F32 = jnp.float32
BF16 = jnp.bfloat16


def _dot(a, b):
    return jnp.dot(a.astype(BF16), b.astype(BF16), preferred_element_type=F32)


def _split2(x):
    hi = x.astype(BF16)
    lo = (x - hi.astype(F32)).astype(BF16)
    return hi, lo


def _dot_exact_lhs(a_exact, b):
    h, l = _split2(b)
    d = lambda x: jnp.dot(a_exact, x, preferred_element_type=F32)
    return d(h) + d(l)


def _head_ones():
    r = lax.broadcasted_iota(jnp.int32, (LANES, LANES), 0) // HEAD_DIM
    c = lax.broadcasted_iota(jnp.int32, (LANES, LANES), 1) // HEAD_DIM
    return jnp.where(r == c, 1.0, 0.0).astype(BF16)


def _head_sum(x, ones):
    return jnp.dot(x.astype(BF16), ones, preferred_element_type=F32)


def _sigmoid(x):
    return 0.5 * jnp.tanh(0.5 * x) + 0.5


def _rms_rows(x):
    return x * lax.rsqrt(jnp.mean(x * x, axis=-1, keepdims=True) + RMS_EPS)


def _const_spec(shape):
    nd = len(shape)
    return pl.BlockSpec(shape, lambda *_: (0,) * nd, pipeline_mode=pl.Buffered(1))


def _params(n_axes):
    return pltpu.CompilerParams(dimension_semantics=("arbitrary",) * n_axes,
                                vmem_limit_bytes=VMEM_LIMIT)


def _mod_kernel(cc_ref, c_ref, w_ref, b_ref, win_ref, o_ref, winb_ref, s_ref):
    n = 1 + c_ref.shape[0]

    @pl.when(pl.program_id(0) == 0)
    def _():
        c = jnp.concatenate([cc_ref[...], c_ref[...], jnp.zeros((LANES - n, D_MODEL), F32)], axis=0)
        s_ref[...] = (c * _sigmoid(c)).T

    s = s_ref[:, 0:n]
    w = w_ref[...]
    for r in range(n):
        o_ref[r] = jnp.sum(w * s[:, r:r + 1], axis=0, keepdims=True) + b_ref[...]
    winb_ref[...] = win_ref[...].astype(BF16)


def _modulation(c_ctx, c, w_ada, b_ada, w_in):
    n = 1 + c.shape[0]
    tn = 768
    steps = 6 * D_MODEL // tn
    rows = w_in.shape[0] // steps
    assert rows % 16 == 0
    return pl.pallas_call(
        _mod_kernel,
        grid=(steps,),
        in_specs=[pl.BlockSpec((1, D_MODEL), lambda j: (0, 0)),
                  pl.BlockSpec((n - 1, D_MODEL), lambda j: (0, 0)),
                  pl.BlockSpec((D_MODEL, tn), lambda j: (0, j)),
                  pl.BlockSpec((1, tn), lambda j: (0, j)),
                  pl.BlockSpec((rows, w_in.shape[1]), lambda j: (j, 0))],
        out_specs=[pl.BlockSpec((n, 1, tn), lambda j: (0, 0, j)),
                   pl.BlockSpec((rows, w_in.shape[1]), lambda j: (j, 0))],
        out_shape=[jax.ShapeDtypeStruct((n, 1, 6 * D_MODEL), F32),
                   jax.ShapeDtypeStruct(w_in.shape, BF16)],
        scratch_shapes=[pltpu.VMEM((D_MODEL, LANES), F32)],
        compiler_params=_params(1),
        name="modulation",
    )(c_ctx.reshape(1, -1), c, w_ada, b_ada.reshape(1, -1), w_in)


def _pick_group(i, tiles_a, a_ref, b_ref):
    return jnp.where(i < tiles_a, a_ref[0], b_ref[0])


def _group_specs(tiles_a, tm, width):
    return [pl.BlockSpec((1, tm, width), lambda i: (jnp.minimum(i, tiles_a - 1), 0, 0)),
            pl.BlockSpec((1, tm, width), lambda i: (jnp.maximum(i - tiles_a, 0), 0, 0))]


def _mod_spec(tiles_a, tiles_per_b):
    return pl.BlockSpec((1, 1, 6 * D_MODEL),
                        lambda i: (jnp.where(i < tiles_a, 0, 1 + (i - tiles_a) // tiles_per_b), 0, 0))


def _mod_part(mod_ref, k):
    return mod_ref[0, :, k * D_MODEL:(k + 1) * D_MODEL]


def _inproj_kernel(xa_ref, xb_ref, mod_ref, g_ref, w_ref, urw_ref, una_ref, gt_ref, *, tiles_a):
    x = _pick_group(pl.program_id(0), tiles_a, xa_ref, xb_ref)
    h = _rms_rows(x) * g_ref[...]
    h = (h * (1.0 + _mod_part(mod_ref, 1)) + _mod_part(mod_ref, 0)).astype(BF16)
    d = lambda lo, hi: jnp.dot(h, w_ref[:, lo:hi], preferred_element_type=F32)
    urw_ref[...] = d(0, RW_COLS)
    una_ref[...] = d(RW_COLS, RW_COLS + NA_IN_COLS)
    gt_ref[...] = d(RW_COLS + NA_IN_COLS, RW_COLS + NA_IN_COLS + GATE_COLS).astype(BF16)


def _in_proj(xa, xb, tiles_per_b, mod_all, norm_g, w_in_bf):
    tm = TOKEN_TILE
    tiles_a, tiles = xa.shape[0], xa.shape[0] + xb.shape[0]
    row = lambda i: (i, 0)
    return pl.pallas_call(
        functools.partial(_inproj_kernel, tiles_a=tiles_a),
        grid=(tiles,),
        in_specs=_group_specs(tiles_a, tm, D_MODEL) + [_mod_spec(tiles_a, tiles_per_b),
                                                       _const_spec((1, D_MODEL)), _const_spec(w_in_bf.shape)],
        out_specs=[pl.BlockSpec((tm, RW_COLS), row),
                   pl.BlockSpec((tm, NA_IN_COLS), row),
                   pl.BlockSpec((tm, GATE_COLS), row)],
        out_shape=[jax.ShapeDtypeStruct((tiles * tm, RW_COLS), F32),
                   jax.ShapeDtypeStruct((tiles * tm, NA_IN_COLS), F32),
                   jax.ShapeDtypeStruct((tiles * tm, GATE_COLS), BF16)],
        compiler_params=_params(1),
        name="in_proj",
    )(xa, xb, mod_all, norm_g.reshape(1, -1), w_in_bf)


def _shift(x, mu):
    t_len = x.shape[0]
    row = lax.broadcasted_iota(jnp.int32, x.shape, 0)
    prev = jnp.where(row == 0, 0.0, pltpu.roll(x, 1, 0))
    nxt = jnp.where(row == t_len - 1, 0.0, pltpu.roll(x, t_len - 1, 0))
    return x + mu[0:1, :] * (prev - x) + mu[1:2, :] * (nxt - x)


def _stack_heads(x, lane_lo):
    return jnp.concatenate([x * lane_lo, x * (1.0 - lane_lo)], axis=0)


def _wkv_intra(units, consts):
    tri, mask_s, mask_i, eye, lane_lo, blk = consts
    stack = lambda z: _stack_heads(z, lane_lo)

    cums = [_dot_exact_lhs(tri[int(u[6])], u[1]) for u in units]

    prep = []
    for (r, lw, kd, v, kk, b, reverse), cum in zip(units, cums):
        mid_row = CHUNK // 2 if reverse else CHUNK // 2 - 1
        tot_row = 0 if reverse else CHUNK - 1
        a = -kk
        ex = cum - lw
        mid = cum[mid_row:mid_row + 1, :]
        tot = cum[tot_row:tot_row + 1, :]
        up = jnp.exp(cum - mid)
        dn = jnp.exp(mid - cum)
        tail = jnp.exp(tot - cum)
        prep.append(dict(
            at_m=stack(a * jnp.exp(ex - mid)).astype(BF16),
            rt_m=stack(r * up).astype(BF16),
            bk_m=jnp.concatenate([b * dn, kd * dn], axis=0).astype(BF16),
            a_e=stack(a * jnp.exp(ex)),
            r_e=stack(r * jnp.exp(cum)),
            bk_t=jnp.concatenate([stack(b * tail), stack(kd * tail)], axis=0).T.astype(BF16),
            vv=stack(v).astype(BF16),
            diag=jnp.where(eye, jnp.exp(tot), 0.0),
            rev=int(reverse)))

    ntd = lambda x, y: lax.dot_general(x, y, (((1,), (1,)), ((), ())), preferred_element_type=F32)
    mm = lambda x, y: jnp.dot(x, y, preferred_element_type=F32)
    diag_blk, swap_eye = blk
    roll_head = lambda z: pltpu.roll(z, HEAD_DIM, 1)
    by_block = lambda z: jnp.concatenate([z[:CHUNK], roll_head(z[CHUNK:])], axis=0)
    top = [by_block(ntd(p["at_m"], p["bk_m"])) for p in prep]
    low = [by_block(ntd(p["rt_m"], p["bk_m"])) for p in prep]
    a_ak = [(roll_head(t) * mask_s[p["rev"]]).astype(BF16) for t, p in zip(top, prep)]
    bot = [jnp.concatenate([t * mask_i[p["rev"]], roll_head(t) * mask_i[p["rev"]]], axis=1).astype(BF16)
           for t, p in zip(low, prep)]

    off_blk = 1.0 - diag_blk
    both = [t * mask_s[p["rev"]] + swap_eye for t, p in zip(top, prep)]
    steps = CHUNK.bit_length() - 1
    diag_bf = diag_blk.astype(BF16)
    for j in range(steps):
        packed = [q.astype(BF16) for q in both]
        res = [mm(qb * diag_bf, qb) for qb in packed]
        both = [r + off_blk * q for r, q in zip(res, both)]
    ts = [pltpu.roll(q, HEAD_DIM, 1).astype(BF16) for q in both]
    x0 = [jnp.concatenate([p["a_e"], mm(ak, p["vv"])], axis=1) for p, ak in zip(prep, a_ak)]
    xs = [mm(t, x.astype(BF16)) for x, t in zip(x0, ts)]

    out = []
    zeros = jnp.zeros((STACK, LANES), BF16)
    for p, x, bt in zip(prep, xs, bot):
        rhs = jnp.concatenate([x.astype(BF16), jnp.concatenate([zeros, p["vv"]], axis=1)], axis=0)
        lhs = jnp.concatenate([bt, p["bk_t"]], axis=0)
        res = mm(lhs, rhs)
        lhs2 = res[:, :LANES] + jnp.concatenate([p["r_e"], p["diag"]], axis=0)
        out.append((lhs2.astype(BF16), res[:, LANES:]))
    return out


def _wkv_constants():
    lane_head = np.arange(LANES) // HEAD_DIM
    ones = (lane_head[:, None] == lane_head[None, :]).astype(np.float32)
    t = np.arange(CHUNK)
    tri = np.stack([t[None, :] <= t[:, None], t[None, :] >= t[:, None]]).astype(np.float32)
    rs, cs = t[:, None], t[None, :]
    tri_masks = np.stack([cs < rs, cs > rs, cs <= rs, cs >= rs]).astype(np.float32)
    masks = np.kron(np.eye(2, dtype=np.float32), tri_masks)
    return jnp.asarray(ones, BF16), jnp.asarray(tri, BF16), jnp.asarray(masks, F32)


def _rwkv_kernel(*refs, t_len, has_s0, pairs, units, n_cast, attend):
    (r_ref, k_ref, v_ref, lo_ref, mur_ref, muk_ref, muv_ref, mul_ref, w0_ref, a0_ref, wup_ref, aup_ref,
     gup_ref, kk_ref, ka_ref, rk_ref, lng_ref, lnb_ref, ones_ref, tri_ref, mask_ref) = refs[:21]
    pos = 21
    s0_ref = None
    if has_s0:
        s0_ref = refs[pos]
        pos += 1
    cast_in = refs[pos:pos + n_cast]
    pos += n_cast
    n_att_in, n_att_out = (5, 3) if attend else (0, 0)
    att_in = refs[pos:pos + n_att_in]
    pos += n_att_in
    o_ref, sn_ref = refs[pos], refs[pos + 1]
    cast_out = refs[pos + 2:pos + 2 + n_cast]
    pos += 2 + n_cast
    att_out = refs[pos:pos + n_att_out]
    (r_s, v_s, kk_s, b0_s, b1_s, lw0_s, lw1_s, kd0_s, kd1_s, gate_s, bonus_s, yf_s, yb_s,
     lhs_s, add_s, st_s) = refs[pos + n_att_out:]
    if attend:
        _na_ctx_kernel(*att_in, *att_out)
    for w_in_ref, w_out_ref in zip(cast_in, cast_out):
        w_out_ref[...] = w_in_ref[...].astype(BF16)

    ones = ones_ref[...]
    lane = lax.broadcasted_iota(jnp.int32, (1, LANES), 1)
    lo_half = lane < HEAD_DIM
    lane_lo = jnp.where(lo_half, 1.0, 0.0)
    mm = lambda x, y: jnp.dot(x, y, preferred_element_type=F32)

    lo = _shift(lo_ref[0], mul_ref[...])
    wd = jnp.tanh(lo[:, 0:LANES])
    ad = lo[:, LANES:2 * LANES]
    sig_gd = _sigmoid(lo[:, 2 * LANES:3 * LANES]).astype(BF16)
    wd_split = [_split2(wd * m) for m in (lane_lo, 1.0 - lane_lo)]
    ad_bf = [(ad * m).astype(BF16) for m in (lane_lo, 1.0 - lane_lo)]
    for j in range(pairs):
        cols = slice(j * LANES, (j + 1) * LANES)
        r = _shift(r_ref[0, :, cols], mur_ref[:, cols])
        k = _shift(k_ref[0, :, cols], muk_ref[:, cols])
        v = _shift(v_ref[0, :, cols], muv_ref[:, cols])
        kk = k * kk_ref[:, cols]
        kk = kk * lax.rsqrt(_head_sum(kk * kk, ones) + L2_EPS)
        wup_h, wup_l = _split2(wup_ref[:, cols])
        aup = aup_ref[:, cols].astype(BF16)
        kdirs = []
        for e, (lw_s, kd_s, b_s) in enumerate(((lw0_s, kd0_s, b0_s), (lw1_s, kd1_s, b1_s))):
            wd_h, wd_l = wd_split[e]
            w_lin = w0_ref[e:e + 1, cols] + (mm(wd_h, wup_h) + mm(wd_l, wup_h) + mm(wd_h, wup_l))
            lw_s[j] = -DECAY_SCALE * _sigmoid(w_lin)
            iclr = _sigmoid(a0_ref[e:e + 1, cols] + mm(ad_bf[e], aup))
            kd = k * (1.0 + (iclr - 1.0) * ka_ref[:, cols])
            kd_s[j] = kd
            b_s[j] = kk * iclr
            kdirs.append(kd)
        gate_s[:, cols] = mm(sig_gd, gup_ref[:, cols].astype(BF16))
        h0 = 2 * (pl.program_id(1) * pairs + j)
        rk = jnp.concatenate([rk_ref[pl.ds(h0, 1), :], rk_ref[pl.ds(h0 + 1, 1), :]], axis=1)
        bonus_s[:, cols] = _head_sum(r * (0.5 * (kdirs[0] + kdirs[1])) * rk, ones) * v
        r_s[j] = r
        v_s[j] = v
        kk_s[j] = kk

    n_chunks = t_len // CHUNK
    chunks_per = min(n_chunks, units // 2)
    pairs_per = min(pairs, units // (2 * chunks_per))
    groups = n_chunks // chunks_per
    rs = lax.broadcasted_iota(jnp.int32, (STACK, STACK), 0)
    cs = lax.broadcasted_iota(jnp.int32, (STACK, STACK), 1)
    as_f32 = lambda m: jnp.where(m, 1.0, 0.0)
    blk = (as_f32(rs // CHUNK == cs // CHUNK), as_f32(cs == (rs + CHUNK) % STACK))
    consts = ((tri_ref[0], tri_ref[1]), (mask_ref[0], mask_ref[1]), (mask_ref[2], mask_ref[3]),
              rs == cs, lane_lo, blk)
    dirs = ((lw0_s, kd0_s, b0_s), (lw1_s, kd1_s, b1_s))

    def intra_body(it, carry):
        pg = it // groups
        g = it % groups
        units_, ids = [], []
        for jj in range(pairs_per):
            j = pg * pairs_per + jj
            for cc in range(chunks_per):
                c = g * chunks_per + cc
                rows = pl.ds(pl.multiple_of(c * CHUNK, CHUNK), CHUNK)
                for e, (lw_s, kd_s, b_s) in enumerate(dirs):
                    units_.append((r_s[j, rows, :], lw_s[j, rows, :], kd_s[j, rows, :], v_s[j, rows, :],
                                   kk_s[j, rows, :], b_s[j, rows, :], e == 1))
                    ids.append((j * 2 + e) * n_chunks + c)
        for uid, (lhs, add) in zip(ids, _wkv_intra(units_, consts)):
            lhs_s[uid] = lhs
            add_s[uid] = add
        return carry

    lax.fori_loop(0, (pairs // pairs_per) * groups, intra_body, 0, unroll=2)

    same_head = rs // HEAD_DIM == cs // HEAD_DIM
    for j in range(pairs):
        for e in range(2):
            if has_s0:
                both = jnp.concatenate([s0_ref[0, e, 2 * j], s0_ref[0, e, 2 * j + 1]], axis=1)
                st_s[2 * j + e] = jnp.where(same_head, jnp.concatenate([both, both], axis=0), 0.0).T
            else:
                st_s[2 * j + e] = jnp.zeros((LANES, LANES), F32)

    def state_body(it, carry):
        chunk = (it, n_chunks - 1 - it)
        uids = [(j * 2 + e) * n_chunks + chunk[e] for j in range(pairs) for e in range(2)]
        sts = [st_s[ch].astype(BF16) for ch in range(2 * pairs)]
        res = [mm(lhs_s[uid], st) + add_s[uid] for uid, st in zip(uids, sts)]
        for ch, rr in enumerate(res):
            j, e = divmod(ch, 2)
            y_s = yb_s if e else yf_s
            y_s[j, pl.ds(pl.multiple_of(chunk[e] * CHUNK, CHUNK), CHUNK), :] = rr[:CHUNK] + rr[CHUNK:STACK]
            st_s[ch] = rr[STACK:]
        return carry

    lax.fori_loop(0, n_chunks, state_body, 0, unroll=True)
    for j in range(pairs):
        for e in range(2):
            st_t = st_s[2 * j + e].T
            sn_ref[0, e, 2 * j] = st_t[:HEAD_DIM, :HEAD_DIM]
            sn_ref[0, e, 2 * j + 1] = pltpu.roll(st_t, HEAD_DIM, 1)[HEAD_DIM:, :HEAD_DIM]

    inv_d = 1.0 / HEAD_DIM
    for j in range(pairs):
        cols = slice(j * LANES, (j + 1) * LANES)
        y = yf_s[j] + yb_s[j]
        mean = _head_sum(y, ones) * inv_d
        dlt = y - mean
        var = _head_sum(dlt * dlt, ones) * inv_d
        yn = dlt * lax.rsqrt(var + GN_EPS) * lng_ref[:, cols] + lnb_ref[:, cols]
        o_ref[0, :, cols] = ((yn + bonus_s[:, cols]) * gate_s[:, cols]).astype(o_ref.dtype)


def _rwkv_branch(u_rw, bsz, b_off, s0, p, pairs, units, cast=(), attend=None):
    t_len = u_rw.shape[1]
    has_s0 = s0 is not None
    width = pairs * LANES
    seg = RW_WIDTH // width
    tok = lambda off: pl.BlockSpec((1, t_len, width), lambda b, j: (b_off + b, 0, off + j))
    mu = lambda off: pl.BlockSpec((2, width), lambda b, j: (0, off + j))
    vec2 = pl.BlockSpec((2, width), lambda b, j: (0, j))
    vec1 = pl.BlockSpec((1, width), lambda b, j: (0, j))
    mat = pl.BlockSpec((LANES, width), lambda b, j: (0, j))
    lora_w = 3 * LANES
    lora_blk = 3 * RW_WIDTH // lora_w
    in_specs = [tok(0), tok(seg), tok(2 * seg),
                pl.BlockSpec((1, t_len, lora_w), lambda b, j: (b_off + b, 0, lora_blk)),
                mu(0), mu(seg), mu(2 * seg),
                pl.BlockSpec((2, lora_w), lambda b, j: (0, lora_blk)),
                vec2, vec2, mat, mat, mat, vec1, vec1, _const_spec((RW_HEADS, HEAD_DIM)), vec1, vec1,
                _const_spec((LANES, LANES)), _const_spec((2, CHUNK, CHUNK)), _const_spec((4, STACK, STACK))]
    args = [u_rw, u_rw, u_rw, u_rw, p["shift_mu"], p["shift_mu"], p["shift_mu"], p["shift_mu"],
            p["rw_w0"], p["rw_a0"],
            p["rw_w_up"].reshape(2 * LORA_DECAY, RW_WIDTH), p["rw_a_up"].reshape(2 * LORA_ICLR, RW_WIDTH),
            p["rw_g_up"], p["rw_k_k"].reshape(1, -1), p["rw_k_a"].reshape(1, -1),
            p["rw_r_k"], p["rw_ln_g"].reshape(1, -1), p["rw_ln_b"].reshape(1, -1),
            *_wkv_constants()]
    st_spec = pl.BlockSpec((1, 2, 2 * pairs, HEAD_DIM, HEAD_DIM), lambda b, j: (b, 0, j, 0, 0))
    if has_s0:
        in_specs.append(st_spec)
        args.append(s0)
    out_specs = [pl.BlockSpec((1, t_len, width), lambda b, j: (b, 0, j)),
                 pl.BlockSpec((1, 2, 2 * pairs, HEAD_DIM, HEAD_DIM), lambda b, j: (b, 0, j, 0, 0))]
    out_shape = [jax.ShapeDtypeStruct((bsz, t_len, RW_WIDTH), BF16),
                 jax.ShapeDtypeStruct((bsz, 2, RW_HEADS, HEAD_DIM, HEAD_DIM), F32)]
    for w in cast:
        assert PAIRS == pairs and w.shape[0] % (16 * bsz) == 0
        blk = pl.BlockSpec((w.shape[0] // bsz, w.shape[1]), lambda b, j: (b, 0))
        in_specs.append(blk)
        args.append(w)
        out_specs.append(blk)
        out_shape.append(jax.ShapeDtypeStruct(w.shape, BF16))
    if attend is not None:
        u_na, q_g, k_g = attend
        assert PAIRS == pairs
        na_tok = lambda seg_: pl.BlockSpec((1, t_len, NA_WIDTH), lambda b, j: (b_off + b, 0, seg_))
        na_out = pl.BlockSpec((1, t_len, NA_WIDTH), lambda b, j: (b, 0, 0))
        g2 = lambda g: g.reshape(1, HEAD_DIM)
        in_specs += [na_tok(0), na_tok(1), na_tok(2), _const_spec((1, HEAD_DIM)), _const_spec((1, HEAD_DIM))]
        args += [u_na, u_na, u_na, g2(q_g), g2(k_g)]
        na_out_t = pl.BlockSpec((1, NA_WIDTH, t_len), lambda b, j: (b, 0, 0))
        out_specs += [na_out, na_out_t, na_out_t]
        na_shape = (bsz, t_len, NA_WIDTH)
        na_shape_t = (bsz, NA_WIDTH, t_len)
        out_shape += [jax.ShapeDtypeStruct(na_shape, BF16), jax.ShapeDtypeStruct(na_shape_t, F32),
                      jax.ShapeDtypeStruct(na_shape_t, F32)]
    n_units = 2 * pairs * (t_len // CHUNK)
    per_pair = pltpu.VMEM((pairs, t_len, LANES), F32)
    full = pltpu.VMEM((t_len, width), F32)
    scratch = [per_pair] * 9 + [full, full, per_pair, per_pair,
                                pltpu.VMEM((n_units, 2 * STACK, LANES), BF16),
                                pltpu.VMEM((n_units, 2 * STACK, LANES), F32),
                                pltpu.VMEM((2 * pairs, LANES, LANES), F32)]
    o_rw, s_new, *rest = pl.pallas_call(
        functools.partial(_rwkv_kernel, t_len=t_len, has_s0=has_s0, pairs=pairs, units=units, n_cast=len(cast),
                          attend=attend is not None),
        grid=(bsz, PAIRS // pairs),
        in_specs=in_specs,
        out_specs=out_specs,
        out_shape=out_shape,
        scratch_shapes=scratch,
        compiler_params=_params(2),
        name="rwkv_branch",
    )(*args)
    return o_rw, s_new, rest[:len(cast)], rest[len(cast):]


def _qk_norm(t, g, ones):
    ms = _head_sum(t * t, ones) * (1.0 / HEAD_DIM)
    return t * lax.rsqrt(ms + RMS_EPS) * jnp.concatenate([g, g], axis=1)


def _nt(x, y):
    return lax.dot_general(x, y, (((1,), (1,)), ((), ())), preferred_element_type=F32)


def _na_ctx_kernel(q_ref, k_ref, v_ref, qg_ref, kg_ref, o_ref, kn_ref, vc_ref):
    ones = _head_ones()
    lo_half = lax.broadcasted_iota(jnp.int32, (1, LANES), 1) < HEAD_DIM
    lo = jnp.where(lo_half, 1.0, 0.0)
    t_len = q_ref.shape[1]
    qs, ks, vs = [], [], []
    for j in range(NA_WIDTH // LANES):
        cols = slice(j * LANES, (j + 1) * LANES)
        qn = _qk_norm(q_ref[0, :, cols], qg_ref[...], ones)
        kn = _qk_norm(k_ref[0, :, cols], kg_ref[...], ones)
        v = v_ref[0, :, cols]
        kn_ref[0, cols, :] = kn.T
        vc_ref[0, cols, :] = v.T
        qn = qn * QK_SCALE
        qs.append(jnp.concatenate([qn * lo, qn * (1.0 - lo)], axis=0).astype(BF16))
        ks.append(kn.astype(BF16))
        vs.append(v.astype(BF16))
    logits = [_nt(q, k) for q, k in zip(qs, ks)]
    ms = [jnp.max(s, axis=-1, keepdims=True) for s in logits]
    ps = [jnp.exp(s - m) for s, m in zip(logits, ms)]
    ls = [jnp.sum(p, axis=-1, keepdims=True) for p in ps]
    outs = [jnp.dot(p.astype(BF16), v, preferred_element_type=F32) / l for p, v, l in zip(ps, vs, ls)]
    for j, o in enumerate(outs):
        o_ref[0, :, j * LANES:(j + 1) * LANES] = jnp.where(lo_half, o[:t_len], o[t_len:]).astype(o_ref.dtype)


NA_ROW_ILP = 4


def _na_lat_kernel(q_ref, k_ref, v_ref, kc_ref, vc_ref, qg_ref, kg_ref, diag_ref, o_ref,
                   q0_s, q1_s, kn_s, v_s, kc_s, vc_s, tab_ref, *, rows, kr):
    ones = _head_ones()
    lo_half = lax.broadcasted_iota(jnp.int32, (1, LANES), 1) < HEAD_DIM
    lo = jnp.where(lo_half, 1.0, 0.0)
    q_col = lax.broadcasted_iota(jnp.int32, (GRID_W, 2 * GRID_W), 0)
    k_col = lax.broadcasted_iota(jnp.int32, (GRID_W, 2 * GRID_W), 1) % GRID_W
    w_start = jnp.clip(q_col - NA_COLS // 2, 0, GRID_W - NA_COLS)
    valid = (k_col >= w_start) & (k_col < w_start + NA_COLS)
    for h in range(2):
        for d in range(2 * NA_ROWS - 2):
            diag = jnp.broadcast_to(diag_ref[h, d:d + 1, :], (GRID_W, 2 * GRID_W))
            tab_ref[h, d] = jnp.where(valid, pltpu.roll(diag, 0, 1, stride=1, stride_axis=0), NEG_INF)
    qn = _qk_norm(q_ref[0], qg_ref[...], ones) * QK_SCALE
    q0_s[...] = (qn * lo).astype(BF16)
    q1_s[...] = (qn * (1.0 - lo)).astype(BF16)
    kn_s[...] = _qk_norm(k_ref[0], kg_ref[...], ones).astype(BF16)
    v_s[...] = v_ref[0].astype(BF16)
    kc_s[...] = kc_ref[0].astype(BF16)
    vc_s[...] = vc_ref[0].astype(BF16)
    win = kr * GRID_W

    def body(it, carry):
        qs, k_rows, q_rows, biases = [], [], [], []
        for s in range(NA_ROW_ILP):
            i = it * NA_ROW_ILP + s
            r0 = jnp.clip(i - kr // 2, 0, rows - kr)
            d0 = r0 - i + (NA_ROWS - 1)
            qr = pl.ds(pl.multiple_of(i * GRID_W, GRID_W), GRID_W)
            q_rows.append(qr)
            k_rows.append(pl.ds(pl.multiple_of(r0 * GRID_W, GRID_W), win))
            qs.append(jnp.concatenate([q0_s[qr, :], q1_s[qr, :]], axis=0))
            biases.append(jnp.concatenate(
                [jnp.concatenate([tab_ref[h, d0 + 2 * m] for m in range(kr // 2)], axis=1) for h in range(2)],
                axis=0))
        lw = [_nt(q, kn_s[kr_, :]) + b for q, kr_, b in zip(qs, k_rows, biases)]
        lc = [jnp.dot(q, kc_s[...], preferred_element_type=F32) for q in qs]
        ms = [jnp.maximum(jnp.max(a, axis=-1, keepdims=True), jnp.max(c, axis=-1, keepdims=True))
              for a, c in zip(lw, lc)]
        pw = [jnp.exp(a - m) for a, m in zip(lw, ms)]
        pc = [jnp.exp(c - m) for c, m in zip(lc, ms)]
        ls = [jnp.sum(a, axis=-1, keepdims=True) + jnp.sum(c, axis=-1, keepdims=True) for a, c in zip(pw, pc)]
        outs = [(jnp.dot(a.astype(BF16), v_s[kr_, :], preferred_element_type=F32)
                 + _nt(c.astype(BF16), vc_s[...])) / l
                for a, c, kr_, l in zip(pw, pc, k_rows, ls)]
        for qr, o in zip(q_rows, outs):
            o_ref[0, qr, :] = jnp.where(lo_half, o[:GRID_W], o[GRID_W:]).astype(o_ref.dtype)
        return carry

    lax.fori_loop(0, rows // NA_ROW_ILP, body, 0, unroll=True)


def _latent_bias_diagonals(rpb):
    c1 = NA_COLS - 1
    assert 2 * c1 + 1 <= GRID_W - c1
    a, b = rpb[:, :-1], rpb[:, 1:]
    gap = jnp.zeros(a.shape[:2] + (GRID_W - 2 * c1 - 1,), rpb.dtype)
    return jnp.concatenate([a[..., c1:], gap, b, gap, a[..., :c1]], axis=-1)


def _na_latent(u_na, bsz, b_off, k_ctx, v_ctx, q_g, k_g, rpb):
    t_len = u_na.shape[1]
    rows = t_len // GRID_W
    kr = min(NA_ROWS, rows)
    assert kr % 2 == 0 and rows % NA_ROW_ILP == 0
    ctx_len = k_ctx.shape[2]
    seg = NA_WIDTH // LANES
    tok = lambda off: pl.BlockSpec((1, t_len, LANES), lambda b, j: (b_off + b, 0, off + j))
    ctx = pl.BlockSpec((1, LANES, ctx_len), lambda b, j: (b, j, 0))
    g2 = lambda g: g.reshape(1, HEAD_DIM)
    diags = _latent_bias_diagonals(rpb)
    tok_s = pltpu.VMEM((t_len, LANES), BF16)
    ctx_s = pltpu.VMEM((LANES, ctx_len), BF16)
    return pl.pallas_call(
        functools.partial(_na_lat_kernel, rows=rows, kr=kr),
        grid=(bsz, seg),
        in_specs=[tok(0), tok(seg), tok(2 * seg), ctx, ctx,
                  _const_spec((1, HEAD_DIM)), _const_spec((1, HEAD_DIM)),
                  pl.BlockSpec((2, 2 * NA_ROWS - 2, 2 * GRID_W), lambda b, j: (j, 0, 0))],
        out_specs=pl.BlockSpec((1, t_len, LANES), lambda b, j: (b, 0, j)),
        out_shape=jax.ShapeDtypeStruct((bsz, t_len, NA_WIDTH), BF16),
        scratch_shapes=[tok_s, tok_s, tok_s, tok_s, ctx_s, ctx_s,
                        pltpu.VMEM((2, 2 * NA_ROWS - 2, GRID_W, 2 * GRID_W), F32)],
        compiler_params=_params(2),
        name="na_latent",
    )(u_na, u_na, u_na, k_ctx, v_ctx, g2(q_g), g2(k_g), diags)


LATE_WEIGHTS = ("w_o_rwkv", "w_o_na", "w_out", "ffn_w1", "ffn_w3", "ffn_w2")


def _out_ffn_kernel(xa_ref, xb_ref, orwa_ref, orwb_ref, onaa_ref, onab_ref, gt_ref, mod_ref, g_ref,
                    wor_ref, won_ref, wout_ref, w1_ref, w3_ref, w2_ref, ya_ref, yb_ref, *, tiles_a):
    i = pl.program_id(0)
    x = _pick_group(i, tiles_a, xa_ref, xb_ref)
    o_rw = _pick_group(i, tiles_a, orwa_ref, orwb_ref)
    o_na = _pick_group(i, tiles_a, onaa_ref, onab_ref)
    g_rw = _sigmoid(gt_ref[:, :D_MODEL].astype(F32))
    g_na = _sigmoid(gt_ref[:, D_MODEL:].astype(F32))
    merged = g_rw * _dot(o_rw, wor_ref[...]) + g_na * _dot(o_na, won_ref[...])
    x1 = x + _mod_part(mod_ref, 2) * _dot(merged, wout_ref[...])
    h2 = _rms_rows(x1) * g_ref[...]
    h2 = (h2 * (1.0 + _mod_part(mod_ref, 4)) + _mod_part(mod_ref, 3)).astype(BF16)
    acc = jnp.zeros(x1.shape, F32)
    for c in range(FF_HIDDEN // FF_CHUNK):
        cols = slice(c * FF_CHUNK, (c + 1) * FF_CHUNK)
        a = jnp.dot(h2, w1_ref[:, cols], preferred_element_type=F32)
        b = jnp.dot(h2, w3_ref[:, cols], preferred_element_type=F32)
        hh = (a * _sigmoid(a) * b).astype(BF16)
        acc = acc + jnp.dot(hh, w2_ref[cols, :], preferred_element_type=F32)
    y = x1 + _mod_part(mod_ref, 5) * acc

    @pl.when(i < tiles_a)
    def _():
        ya_ref[0] = y

    @pl.when(i >= tiles_a)
    def _():
        yb_ref[0] = y


def _out_ffn(xa, xb, orw_a, orw_b, ona_a, ona_b, gates, tiles_per_b, mod_all, norm_g, wb):
    tm = TOKEN_TILE
    tiles_a, tiles = xa.shape[0], xa.shape[0] + xb.shape[0]
    weights = [wb[n] for n in LATE_WEIGHTS]
    groups = lambda width: _group_specs(tiles_a, tm, width)
    return pl.pallas_call(
        functools.partial(_out_ffn_kernel, tiles_a=tiles_a),
        grid=(tiles,),
        in_specs=groups(D_MODEL) + groups(RW_WIDTH) + groups(NA_WIDTH)
        + [pl.BlockSpec((tm, GATE_COLS), lambda i: (i, 0)), _mod_spec(tiles_a, tiles_per_b),
           _const_spec((1, D_MODEL))] + [_const_spec(w.shape) for w in weights],
        out_specs=groups(D_MODEL),
        out_shape=[jax.ShapeDtypeStruct(xa.shape, F32), jax.ShapeDtypeStruct(xb.shape, F32)],
        compiler_params=_params(1),
        name="out_ffn",
    )(xa, xb, orw_a, orw_b, ona_a, ona_b, gates, mod_all, norm_g.reshape(1, -1), *weights)


def kernel(x_prompt, x_sample, state_rwkv, cache_na_k, cache_na_v, c, c_ctx, norm1_g, norm2_g, w_ada, b_ada,
           w_in, shift_mu, rw_w0, rw_w_up, rw_a0, rw_a_up, rw_g_up, rw_k_k, rw_k_a, rw_r_k, rw_ln_g, rw_ln_b,
           na_q_g, na_k_g, na_rpb, w_o_rwkv, w_o_na, w_out, ffn_w1, ffn_w3, ffn_w2):
    depth = w_in.shape[0]
    bsz, seq = x_prompt.shape[:2]
    dec, dec_seq = x_sample.shape[:2]
    tm = TOKEN_TILE
    n_ctx = bsz * seq
    assert n_ctx % tm == 0 and dec_seq % tm == 0 and n_ctx % dec_seq == 0
    tiles_per_b = dec_seq // tm
    tiled = lambda t: t.reshape(-1, tm, t.shape[-1])
    y_p, y_s = x_prompt, x_sample
    new_s, new_k, new_v = [], [], []
    for l in range(depth):
        p = dict(norm1_g=norm1_g[l], norm2_g=norm2_g[l], shift_mu=shift_mu[l], rw_w0=rw_w0[l],
                 rw_w_up=rw_w_up[l], rw_a0=rw_a0[l], rw_a_up=rw_a_up[l], rw_g_up=rw_g_up[l],
                 rw_k_k=rw_k_k[l], rw_k_a=rw_k_a[l], rw_r_k=rw_r_k[l], rw_ln_g=rw_ln_g[l],
                 rw_ln_b=rw_ln_b[l], na_q_g=na_q_g[l], na_k_g=na_k_g[l], na_rpb=na_rpb[l])
        late_f32 = [w[l] for w in (w_o_rwkv, w_o_na, w_out, ffn_w1, ffn_w3, ffn_w2)]
        mod_all, w_in_bf = _modulation(c_ctx, c, w_ada[l], b_ada[l], w_in[l])
        u_rw, u_na, gates = _in_proj(tiled(y_p), tiled(y_s), tiles_per_b, mod_all, p["norm1_g"], w_in_bf)
        ctx_view = lambda t: t.reshape(-1, seq, t.shape[-1])
        lat_view = lambda t: t.reshape(-1, dec_seq, t.shape[-1])
        lat_off = n_ctx // dec_seq
        o_rw_p, s_l, casted, (o_na_p, k_l, v_l) = _rwkv_branch(
            ctx_view(u_rw), bsz, 0, None, p, RWKV_PAIRS_CTX, RWKV_UNITS, late_f32,
            attend=(ctx_view(u_na), p["na_q_g"], p["na_k_g"]))
        late = dict(zip(LATE_WEIGHTS, casted))
        new_s.append(s_l)
        heads_last = lambda t: jnp.transpose(t.reshape(bsz, NA_HEADS, HEAD_DIM, seq), (0, 3, 1, 2))
        new_k.append(heads_last(k_l))
        new_v.append(heads_last(v_l))
        ctx_k = jnp.transpose(cache_na_k[:, l], (0, 2, 3, 1)).reshape(dec, NA_WIDTH, -1)
        ctx_v = jnp.transpose(cache_na_v[:, l], (0, 2, 3, 1)).reshape(dec, NA_WIDTH, -1)
        o_rw_s = _rwkv_branch(lat_view(u_rw), dec, lat_off, state_rwkv[:, l], p, RWKV_PAIRS_LAT,
                               RWKV_UNITS)[0]
        o_na_s = _na_latent(lat_view(u_na), dec, lat_off, ctx_k, ctx_v, p["na_q_g"], p["na_k_g"], p["na_rpb"])
        y_p_t, y_s_t = _out_ffn(tiled(y_p), tiled(y_s), tiled(o_rw_p), tiled(o_rw_s), tiled(o_na_p), tiled(o_na_s),
                                gates, tiles_per_b, mod_all, p["norm2_g"], late)
        y_p, y_s = y_p_t.reshape(x_prompt.shape), y_s_t.reshape(x_sample.shape)
    return (y_p, y_s, jnp.stack(new_s, axis=1), jnp.stack(new_k, axis=1), jnp.stack(new_v, axis=1))
```

```python
import functools

import numpy as np
import jax
import jax.numpy as jnp
from jax import lax
from jax.experimental import pallas as pl
from jax.experimental.pallas import tpu as pltpu

D_MODEL = 1024
GRID_W = 64
HEAD_DIM = 64
RW_HEADS = 8
RW_WIDTH = RW_HEADS * HEAD_DIM
NA_HEADS = 8
NA_WIDTH = NA_HEADS * HEAD_DIM
LORA_DECAY = 64
LORA_ICLR = 64
LORA_GATE = 128
NA_ROWS = 8
NA_COLS = 16
FF_HIDDEN = 2816
RW_COLS = 3 * RW_WIDTH + 2 * LORA_DECAY + 2 * LORA_ICLR + LORA_GATE
NA_IN_COLS = 3 * NA_WIDTH
GATE_COLS = 2 * D_MODEL
RMS_EPS = 1e-6
GN_EPS = 64e-5
L2_EPS = 1e-12
NEG_INF = -1e30
DECAY_SCALE = float(np.exp(-0.5))
QK_SCALE = HEAD_DIM ** -0.5
assert QK_SCALE == 0.125

LANES = 128
PAIRS = RW_HEADS // 2
CHUNK = 64
STACK = 2 * CHUNK
RWKV_PAIRS_CTX = 4
RWKV_PAIRS_LAT = 2
RWKV_UNITS = 16
TOKEN_TILE = 512
FF_CHUNK = 256
VMEM_LIMIT = 60 * 1024 * 1024

F32 = jnp.float32
BF16 = jnp.bfloat16


def _dot(a, b):
    return jnp.dot(a.astype(BF16), b.astype(BF16), preferred_element_type=F32)


def _split2(x):
    hi = x.astype(BF16)
    lo = (x - hi.astype(F32)).astype(BF16)
    return hi, lo


def _dot_exact_lhs(a_exact, b):
    h, l = _split2(b)
    d = lambda x: jnp.dot(a_exact, x, preferred_element_type=F32)
    return d(h) + d(l)


def _head_ones():
    r = lax.broadcasted_iota(jnp.int32, (LANES, LANES), 0) // HEAD_DIM
    c = lax.broadcasted_iota(jnp.int32, (LANES, LANES), 1) // HEAD_DIM
    return jnp.where(r == c, 1.0, 0.0).astype(BF16)


def _head_sum(x, ones):
    return jnp.dot(x.astype(BF16), ones, preferred_element_type=F32)


def _sigmoid(x):
    return 0.5 * jnp.tanh(0.5 * x) + 0.5


def _rms_rows(x):
    return x * lax.rsqrt(jnp.mean(x * x, axis=-1, keepdims=True) + RMS_EPS)


def _const_spec(shape):
    nd = len(shape)
    return pl.BlockSpec(shape, lambda *_: (0,) * nd, pipeline_mode=pl.Buffered(1))


def _params(n_axes):
    return pltpu.CompilerParams(dimension_semantics=("arbitrary",) * n_axes,
                                vmem_limit_bytes=VMEM_LIMIT)


def _mod_kernel(cc_ref, c_ref, w_ref, b_ref, win_ref, o_ref, winb_ref, s_ref):
    n = 1 + c_ref.shape[0]

    @pl.when(pl.program_id(0) == 0)
    def _():
        c = jnp.concatenate([cc_ref[...], c_ref[...], jnp.zeros((LANES - n, D_MODEL), F32)], axis=0)
        s_ref[...] = (c * _sigmoid(c)).T

    s = s_ref[:, 0:n]
    w = w_ref[...]
    for r in range(n):
        o_ref[r] = jnp.sum(w * s[:, r:r + 1], axis=0, keepdims=True) + b_ref[...]
    winb_ref[...] = win_ref[...].astype(BF16)


def _modulation(c_ctx, c, w_ada, b_ada, w_in):
    n = 1 + c.shape[0]
    tn = 768
    steps = 6 * D_MODEL // tn
    rows = w_in.shape[0] // steps
    assert rows % 16 == 0
    return pl.pallas_call(
        _mod_kernel,
        grid=(steps,),
        in_specs=[pl.BlockSpec((1, D_MODEL), lambda j: (0, 0)),
                  pl.BlockSpec((n - 1, D_MODEL), lambda j: (0, 0)),
                  pl.BlockSpec((D_MODEL, tn), lambda j: (0, j)),
                  pl.BlockSpec((1, tn), lambda j: (0, j)),
                  pl.BlockSpec((rows, w_in.shape[1]), lambda j: (j, 0))],
        out_specs=[pl.BlockSpec((n, 1, tn), lambda j: (0, 0, j)),
                   pl.BlockSpec((rows, w_in.shape[1]), lambda j: (j, 0))],
        out_shape=[jax.ShapeDtypeStruct((n, 1, 6 * D_MODEL), F32),
                   jax.ShapeDtypeStruct(w_in.shape, BF16)],
        scratch_shapes=[pltpu.VMEM((D_MODEL, LANES), F32)],
        compiler_params=_params(1),
        name="modulation",
    )(c_ctx.reshape(1, -1), c, w_ada, b_ada.reshape(1, -1), w_in)


def _pick_group(i, tiles_a, a_ref, b_ref):
    return jnp.where(i < tiles_a, a_ref[0], b_ref[0])


def _group_specs(tiles_a, tm, width):
    return [pl.BlockSpec((1, tm, width), lambda i: (jnp.minimum(i, tiles_a - 1), 0, 0)),
            pl.BlockSpec((1, tm, width), lambda i: (jnp.maximum(i - tiles_a, 0), 0, 0))]


def _mod_spec(tiles_a, tiles_per_b):
    return pl.BlockSpec((1, 1, 6 * D_MODEL),
                        lambda i: (jnp.where(i < tiles_a, 0, 1 + (i - tiles_a) // tiles_per_b), 0, 0))


def _mod_part(mod_ref, k):
    return mod_ref[0, :, k * D_MODEL:(k + 1) * D_MODEL]


def _inproj_kernel(xa_ref, xb_ref, mod_ref, g_ref, w_ref, urw_ref, una_ref, gt_ref, *, tiles_a):
    x = _pick_group(pl.program_id(0), tiles_a, xa_ref, xb_ref)
    h = _rms_rows(x) * g_ref[...]
    h = (h * (1.0 + _mod_part(mod_ref, 1)) + _mod_part(mod_ref, 0)).astype(BF16)
    d = lambda lo, hi: jnp.dot(h, w_ref[:, lo:hi], preferred_element_type=F32)
    urw_ref[...] = d(0, RW_COLS)
    una_ref[...] = d(RW_COLS, RW_COLS + NA_IN_COLS)
    gt_ref[...] = d(RW_COLS + NA_IN_COLS, RW_COLS + NA_IN_COLS + GATE_COLS).astype(BF16)


def _in_proj(xa, xb, tiles_per_b, mod_all, norm_g, w_in_bf):
    tm = TOKEN_TILE
    tiles_a, tiles = xa.shape[0], xa.shape[0] + xb.shape[0]
    row = lambda i: (i, 0)
    return pl.pallas_call(
        functools.partial(_inproj_kernel, tiles_a=tiles_a),
        grid=(tiles,),
        in_specs=_group_specs(tiles_a, tm, D_MODEL) + [_mod_spec(tiles_a, tiles_per_b),
                                                       _const_spec((1, D_MODEL)), _const_spec(w_in_bf.shape)],
        out_specs=[pl.BlockSpec((tm, RW_COLS), row),
                   pl.BlockSpec((tm, NA_IN_COLS), row),
                   pl.BlockSpec((tm, GATE_COLS), row)],
        out_shape=[jax.ShapeDtypeStruct((tiles * tm, RW_COLS), F32),
                   jax.ShapeDtypeStruct((tiles * tm, NA_IN_COLS), F32),
                   jax.ShapeDtypeStruct((tiles * tm, GATE_COLS), BF16)],
        compiler_params=_params(1),
        name="in_proj",
    )(xa, xb, mod_all, norm_g.reshape(1, -1), w_in_bf)


def _shift(x, mu):
    t_len = x.shape[0]
    row = lax.broadcasted_iota(jnp.int32, x.shape, 0)
    prev = jnp.where(row == 0, 0.0, pltpu.roll(x, 1, 0))
    nxt = jnp.where(row == t_len - 1, 0.0, pltpu.roll(x, t_len - 1, 0))
    return x + mu[0:1, :] * (prev - x) + mu[1:2, :] * (nxt - x)


def _stack_heads(x, lane_lo):
    return jnp.concatenate([x * lane_lo, x * (1.0 - lane_lo)], axis=0)


def _wkv_intra(units, consts):
    tri, mask_s, mask_i, eye, lane_lo, blk = consts
    stack = lambda z: _stack_heads(z, lane_lo)

    cums = [_dot_exact_lhs(tri[int(u[6])], u[1]) for u in units]

    prep = []
    for (r, lw, kd, v, kk, b, reverse), cum in zip(units, cums):
        mid_row = CHUNK // 2 if reverse else CHUNK // 2 - 1
        tot_row = 0 if reverse else CHUNK - 1
        a = -kk
        ex = cum - lw
        mid = cum[mid_row:mid_row + 1, :]
        tot = cum[tot_row:tot_row + 1, :]
        up = jnp.exp(cum - mid)
        dn = jnp.exp(mid - cum)
        tail = jnp.exp(tot - cum)
        prep.append(dict(
            at_m=stack(a * jnp.exp(ex - mid)).astype(BF16),
            rt_m=stack(r * up).astype(BF16),
            bk_m=jnp.concatenate([b * dn, kd * dn], axis=0).astype(BF16),
            a_e=stack(a * jnp.exp(ex)),
            r_e=stack(r * jnp.exp(cum)),
            bk_t=jnp.concatenate([stack(b * tail), stack(kd * tail)], axis=0).T.astype(BF16),
            vv=stack(v).astype(BF16),
            diag=jnp.where(eye, jnp.exp(tot), 0.0),
            rev=int(reverse)))

    ntd = lambda x, y: lax.dot_general(x, y, (((1,), (1,)), ((), ())), preferred_element_type=F32)
    mm = lambda x, y: jnp.dot(x, y, preferred_element_type=F32)
    diag_blk, swap_eye = blk
    roll_head = lambda z: pltpu.roll(z, HEAD_DIM, 1)
    by_block = lambda z: jnp.concatenate([z[:CHUNK], roll_head(z[CHUNK:])], axis=0)
    top = [by_block(ntd(p["at_m"], p["bk_m"])) for p in prep]
    low = [by_block(ntd(p["rt_m"], p["bk_m"])) for p in prep]
    a_ak = [(roll_head(t) * mask_s[p["rev"]]).astype(BF16) for t, p in zip(top, prep)]
    bot = [jnp.concatenate([t * mask_i[p["rev"]], roll_head(t) * mask_i[p["rev"]]], axis=1).astype(BF16)
           for t, p in zip(low, prep)]

    off_blk = 1.0 - diag_blk
    both = [t * mask_s[p["rev"]] + swap_eye for t, p in zip(top, prep)]
    steps = CHUNK.bit_length() - 1
    diag_bf = diag_blk.astype(BF16)
    for j in range(steps):
        packed = [q.astype(BF16) for q in both]
        res = [mm(qb * diag_bf, qb) for qb in packed]
        both = [r + off_blk * q for r, q in zip(res, both)]
    ts = [pltpu.roll(q, HEAD_DIM, 1).astype(BF16) for q in both]
    x0 = [jnp.concatenate([p["a_e"], mm(ak, p["vv"])], axis=1) for p, ak in zip(prep, a_ak)]
    xs = [mm(t, x.astype(BF16)) for x, t in zip(x0, ts)]

    out = []
    zeros = jnp.zeros((STACK, LANES), BF16)
    for p, x, bt in zip(prep, xs, bot):
        rhs = jnp.concatenate([x.astype(BF16), jnp.concatenate([zeros, p["vv"]], axis=1)], axis=0)
        lhs = jnp.concatenate([bt, p["bk_t"]], axis=0)
        res = mm(lhs, rhs)
        lhs2 = res[:, :LANES] + jnp.concatenate([p["r_e"], p["diag"]], axis=0)
        out.append((lhs2.astype(BF16), res[:, LANES:]))
    return out


def _wkv_constants():
    lane_head = np.arange(LANES) // HEAD_DIM
    ones = (lane_head[:, None] == lane_head[None, :]).astype(np.float32)
    t = np.arange(CHUNK)
    tri = np.stack([t[None, :] <= t[:, None], t[None, :] >= t[:, None]]).astype(np.float32)
    rs, cs = t[:, None], t[None, :]
    tri_masks = np.stack([cs < rs, cs > rs, cs <= rs, cs >= rs]).astype(np.float32)
    masks = np.kron(np.eye(2, dtype=np.float32), tri_masks)
    return jnp.asarray(ones, BF16), jnp.asarray(tri, BF16), jnp.asarray(masks, F32)


def _rwkv_kernel(*refs, t_len, has_s0, pairs, units, n_cast, attend):
    (r_ref, k_ref, v_ref, lo_ref, mur_ref, muk_ref, muv_ref, mul_ref, w0_ref, a0_ref, wup_ref, aup_ref,
     gup_ref, kk_ref, ka_ref, rk_ref, lng_ref, lnb_ref, ones_ref, tri_ref, mask_ref) = refs[:21]
    pos = 21
    s0_ref = None
    if has_s0:
        s0_ref = refs[pos]
        pos += 1
    cast_in = refs[pos:pos + n_cast]
    pos += n_cast
    n_att_in, n_att_out = (5, 3) if attend else (0, 0)
    att_in = refs[pos:pos + n_att_in]
    pos += n_att_in
    o_ref, sn_ref = refs[pos], refs[pos + 1]
    cast_out = refs[pos + 2:pos + 2 + n_cast]
    pos += 2 + n_cast
    att_out = refs[pos:pos + n_att_out]
    (r_s, v_s, kk_s, b0_s, b1_s, lw0_s, lw1_s, kd0_s, kd1_s, gate_s, bonus_s, yf_s, yb_s,
     lhs_s, add_s, st_s) = refs[pos + n_att_out:]
    if attend:
        _na_ctx_kernel(*att_in, *att_out)
    for w_in_ref, w_out_ref in zip(cast_in, cast_out):
        w_out_ref[...] = w_in_ref[...].astype(BF16)

    ones = ones_ref[...]
    lane = lax.broadcasted_iota(jnp.int32, (1, LANES), 1)
    lo_half = lane < HEAD_DIM
    lane_lo = jnp.where(lo_half, 1.0, 0.0)
    mm = lambda x, y: jnp.dot(x, y, preferred_element_type=F32)

    lo = _shift(lo_ref[0], mul_ref[...])
    wd = jnp.tanh(lo[:, 0:LANES])
    ad = lo[:, LANES:2 * LANES]
    sig_gd = _sigmoid(lo[:, 2 * LANES:3 * LANES]).astype(BF16)
    wd_split = [_split2(wd * m) for m in (lane_lo, 1.0 - lane_lo)]
    ad_bf = [(ad * m).astype(BF16) for m in (lane_lo, 1.0 - lane_lo)]
    for j in range(pairs):
        cols = slice(j * LANES, (j + 1) * LANES)
        r = _shift(r_ref[0, :, cols], mur_ref[:, cols])
        k = _shift(k_ref[0, :, cols], muk_ref[:, cols])
        v = _shift(v_ref[0, :, cols], muv_ref[:, cols])
        kk = k * kk_ref[:, cols]
        kk = kk * lax.rsqrt(_head_sum(kk * kk, ones) + L2_EPS)
        wup_h, wup_l = _split2(wup_ref[:, cols])
        aup = aup_ref[:, cols].astype(BF16)
        kdirs = []
        for e, (lw_s, kd_s, b_s) in enumerate(((lw0_s, kd0_s, b0_s), (lw1_s, kd1_s, b1_s))):
            wd_h, wd_l = wd_split[e]
            w_lin = w0_ref[e:e + 1, cols] + (mm(wd_h, wup_h) + mm(wd_l, wup_h) + mm(wd_h, wup_l))
            lw_s[j] = -DECAY_SCALE * _sigmoid(w_lin)
            iclr = _sigmoid(a0_ref[e:e + 1, cols] + mm(ad_bf[e], aup))
            kd = k * (1.0 + (iclr - 1.0) * ka_ref[:, cols])
            kd_s[j] = kd
            b_s[j] = kk * iclr
            kdirs.append(kd)
        gate_s[:, cols] = mm(sig_gd, gup_ref[:, cols].astype(BF16))
        h0 = 2 * (pl.program_id(1) * pairs + j)
        rk = jnp.concatenate([rk_ref[pl.ds(h0, 1), :], rk_ref[pl.ds(h0 + 1, 1), :]], axis=1)
        bonus_s[:, cols] = _head_sum(r * (0.5 * (kdirs[0] + kdirs[1])) * rk, ones) * v
        r_s[j] = r
        v_s[j] = v
        kk_s[j] = kk

    n_chunks = t_len // CHUNK
    chunks_per = min(n_chunks, units // 2)
    pairs_per = min(pairs, units // (2 * chunks_per))
    groups = n_chunks // chunks_per
    rs = lax.broadcasted_iota(jnp.int32, (STACK, STACK), 0)
    cs = lax.broadcasted_iota(jnp.int32, (STACK, STACK), 1)
    as_f32 = lambda m: jnp.where(m, 1.0, 0.0)
    blk = (as_f32(rs // CHUNK == cs // CHUNK), as_f32(cs == (rs + CHUNK) % STACK))
    consts = ((tri_ref[0], tri_ref[1]), (mask_ref[0], mask_ref[1]), (mask_ref[2], mask_ref[3]),
              rs == cs, lane_lo, blk)
    dirs = ((lw0_s, kd0_s, b0_s), (lw1_s, kd1_s, b1_s))

    def intra_body(it, carry):
        pg = it // groups
        g = it % groups
        units_, ids = [], []
        for jj in range(pairs_per):
            j = pg * pairs_per + jj
            for cc in range(chunks_per):
                c = g * chunks_per + cc
                rows = pl.ds(pl.multiple_of(c * CHUNK, CHUNK), CHUNK)
                for e, (lw_s, kd_s, b_s) in enumerate(dirs):
                    units_.append((r_s[j, rows, :], lw_s[j, rows, :], kd_s[j, rows, :], v_s[j, rows, :],
                                   kk_s[j, rows, :], b_s[j, rows, :], e == 1))
                    ids.append((j * 2 + e) * n_chunks + c)
        for uid, (lhs, add) in zip(ids, _wkv_intra(units_, consts)):
            lhs_s[uid] = lhs
            add_s[uid] = add
        return carry

    lax.fori_loop(0, (pairs // pairs_per) * groups, intra_body, 0, unroll=2)

    same_head = rs // HEAD_DIM == cs // HEAD_DIM
    for j in range(pairs):
        for e in range(2):
            if has_s0:
                both = jnp.concatenate([s0_ref[0, e, 2 * j], s0_ref[0, e, 2 * j + 1]], axis=1)
                st_s[2 * j + e] = jnp.where(same_head, jnp.concatenate([both, both], axis=0), 0.0).T
            else:
                st_s[2 * j + e] = jnp.zeros((LANES, LANES), F32)

    def state_body(it, carry):
        chunk = (it, n_chunks - 1 - it)
        uids = [(j * 2 + e) * n_chunks + chunk[e] for j in range(pairs) for e in range(2)]
        sts = [st_s[ch].astype(BF16) for ch in range(2 * pairs)]
        res = [mm(lhs_s[uid], st) + add_s[uid] for uid, st in zip(uids, sts)]
        for ch, rr in enumerate(res):
            j, e = divmod(ch, 2)
            y_s = yb_s if e else yf_s
            y_s[j, pl.ds(pl.multiple_of(chunk[e] * CHUNK, CHUNK), CHUNK), :] = rr[:CHUNK] + rr[CHUNK:STACK]
            st_s[ch] = rr[STACK:]
        return carry

    lax.fori_loop(0, n_chunks, state_body, 0, unroll=True)
    for j in range(pairs):
        for e in range(2):
            st_t = st_s[2 * j + e].T
            sn_ref[0, e, 2 * j] = st_t[:HEAD_DIM, :HEAD_DIM]
            sn_ref[0, e, 2 * j + 1] = pltpu.roll(st_t, HEAD_DIM, 1)[HEAD_DIM:, :HEAD_DIM]

    inv_d = 1.0 / HEAD_DIM
    for j in range(pairs):
        cols = slice(j * LANES, (j + 1) * LANES)
        y = yf_s[j] + yb_s[j]
        mean = _head_sum(y, ones) * inv_d
        dlt = y - mean
        var = _head_sum(dlt * dlt, ones) * inv_d
        yn = dlt * lax.rsqrt(var + GN_EPS) * lng_ref[:, cols] + lnb_ref[:, cols]
        o_ref[0, :, cols] = ((yn + bonus_s[:, cols]) * gate_s[:, cols]).astype(o_ref.dtype)


def _rwkv_branch(u_rw, bsz, b_off, s0, p, pairs, units, cast=(), attend=None):
    t_len = u_rw.shape[1]
    has_s0 = s0 is not None
    width = pairs * LANES
    seg = RW_WIDTH // width
    tok = lambda off: pl.BlockSpec((1, t_len, width), lambda b, j: (b_off + b, 0, off + j))
    mu = lambda off: pl.BlockSpec((2, width), lambda b, j: (0, off + j))
    vec2 = pl.BlockSpec((2, width), lambda b, j: (0, j))
    vec1 = pl.BlockSpec((1, width), lambda b, j: (0, j))
    mat = pl.BlockSpec((LANES, width), lambda b, j: (0, j))
    lora_w = 3 * LANES
    lora_blk = 3 * RW_WIDTH // lora_w
    in_specs = [tok(0), tok(seg), tok(2 * seg),
                pl.BlockSpec((1, t_len, lora_w), lambda b, j: (b_off + b, 0, lora_blk)),
                mu(0), mu(seg), mu(2 * seg),
                pl.BlockSpec((2, lora_w), lambda b, j: (0, lora_blk)),
                vec2, vec2, mat, mat, mat, vec1, vec1, _const_spec((RW_HEADS, HEAD_DIM)), vec1, vec1,
                _const_spec((LANES, LANES)), _const_spec((2, CHUNK, CHUNK)), _const_spec((4, STACK, STACK))]
    args = [u_rw, u_rw, u_rw, u_rw, p["shift_mu"], p["shift_mu"], p["shift_mu"], p["shift_mu"],
            p["rw_w0"], p["rw_a0"],
            p["rw_w_up"].reshape(2 * LORA_DECAY, RW_WIDTH), p["rw_a_up"].reshape(2 * LORA_ICLR, RW_WIDTH),
            p["rw_g_up"], p["rw_k_k"].reshape(1, -1), p["rw_k_a"].reshape(1, -1),
            p["rw_r_k"], p["rw_ln_g"].reshape(1, -1), p["rw_ln_b"].reshape(1, -1),
            *_wkv_constants()]
    st_spec = pl.BlockSpec((1, 2, 2 * pairs, HEAD_DIM, HEAD_DIM), lambda b, j: (b, 0, j, 0, 0))
    if has_s0:
        in_specs.append(st_spec)
        args.append(s0)
    out_specs = [pl.BlockSpec((1, t_len, width), lambda b, j: (b, 0, j)),
                 pl.BlockSpec((1, 2, 2 * pairs, HEAD_DIM, HEAD_DIM), lambda b, j: (b, 0, j, 0, 0))]
    out_shape = [jax.ShapeDtypeStruct((bsz, t_len, RW_WIDTH), BF16),
                 jax.ShapeDtypeStruct((bsz, 2, RW_HEADS, HEAD_DIM, HEAD_DIM), F32)]
    for w in cast:
        assert PAIRS == pairs and w.shape[0] % (16 * bsz) == 0
        blk = pl.BlockSpec((w.shape[0] // bsz, w.shape[1]), lambda b, j: (b, 0))
        in_specs.append(blk)
        args.append(w)
        out_specs.append(blk)
        out_shape.append(jax.ShapeDtypeStruct(w.shape, BF16))
    if attend is not None:
        u_na, q_g, k_g = attend
        assert PAIRS == pairs
        na_tok = lambda seg_: pl.BlockSpec((1, t_len, NA_WIDTH), lambda b, j: (b_off + b, 0, seg_))
        na_out = pl.BlockSpec((1, t_len, NA_WIDTH), lambda b, j: (b, 0, 0))
        g2 = lambda g: g.reshape(1, HEAD_DIM)
        in_specs += [na_tok(0), na_tok(1), na_tok(2), _const_spec((1, HEAD_DIM)), _const_spec((1, HEAD_DIM))]
        args += [u_na, u_na, u_na, g2(q_g), g2(k_g)]
        na_out_t = pl.BlockSpec((1, NA_WIDTH, t_len), lambda b, j: (b, 0, 0))
        out_specs += [na_out, na_out_t, na_out_t]
        na_shape = (bsz, t_len, NA_WIDTH)
        na_shape_t = (bsz, NA_WIDTH, t_len)
        out_shape += [jax.ShapeDtypeStruct(na_shape, BF16), jax.ShapeDtypeStruct(na_shape_t, F32),
                      jax.ShapeDtypeStruct(na_shape_t, F32)]
    n_units = 2 * pairs * (t_len // CHUNK)
    per_pair = pltpu.VMEM((pairs, t_len, LANES), F32)
    full = pltpu.VMEM((t_len, width), F32)
    scratch = [per_pair] * 9 + [full, full, per_pair, per_pair,
                                pltpu.VMEM((n_units, 2 * STACK, LANES), BF16),
                                pltpu.VMEM((n_units, 2 * STACK, LANES), F32),
                                pltpu.VMEM((2 * pairs, LANES, LANES), F32)]
    o_rw, s_new, *rest = pl.pallas_call(
        functools.partial(_rwkv_kernel, t_len=t_len, has_s0=has_s0, pairs=pairs, units=units, n_cast=len(cast),
                          attend=attend is not None),
        grid=(bsz, PAIRS // pairs),
        in_specs=in_specs,
        out_specs=out_specs,
        out_shape=out_shape,
        scratch_shapes=scratch,
        compiler_params=_params(2),
        name="rwkv_branch",
    )(*args)
    return o_rw, s_new, rest[:len(cast)], rest[len(cast):]


def _qk_norm(t, g, ones):
    ms = _head_sum(t * t, ones) * (1.0 / HEAD_DIM)
    return t * lax.rsqrt(ms + RMS_EPS) * jnp.concatenate([g, g], axis=1)


def _nt(x, y):
    return lax.dot_general(x, y, (((1,), (1,)), ((), ())), preferred_element_type=F32)


def _na_ctx_kernel(q_ref, k_ref, v_ref, qg_ref, kg_ref, o_ref, kn_ref, vc_ref):
    ones = _head_ones()
    lo_half = lax.broadcasted_iota(jnp.int32, (1, LANES), 1) < HEAD_DIM
    lo = jnp.where(lo_half, 1.0, 0.0)
    t_len = q_ref.shape[1]
    qs, ks, vs = [], [], []
    for j in range(NA_WIDTH // LANES):
        cols = slice(j * LANES, (j + 1) * LANES)
        qn = _qk_norm(q_ref[0, :, cols], qg_ref[...], ones)
        kn = _qk_norm(k_ref[0, :, cols], kg_ref[...], ones)
        v = v_ref[0, :, cols]
        kn_ref[0, cols, :] = kn.T
        vc_ref[0, cols, :] = v.T
        qn = qn * QK_SCALE
        qs.append(jnp.concatenate([qn * lo, qn * (1.0 - lo)], axis=0).astype(BF16))
        ks.append(kn.astype(BF16))
        vs.append(v.astype(BF16))
    logits = [_nt(q, k) for q, k in zip(qs, ks)]
    ms = [jnp.max(s, axis=-1, keepdims=True) for s in logits]
    ps = [jnp.exp(s - m) for s, m in zip(logits, ms)]
    ls = [jnp.sum(p, axis=-1, keepdims=True) for p in ps]
    outs = [jnp.dot(p.astype(BF16), v, preferred_element_type=F32) / l for p, v, l in zip(ps, vs, ls)]
    for j, o in enumerate(outs):
        o_ref[0, :, j * LANES:(j + 1) * LANES] = jnp.where(lo_half, o[:t_len], o[t_len:]).astype(o_ref.dtype)


NA_ROW_ILP = 4


def _na_lat_kernel(q_ref, k_ref, v_ref, kc_ref, vc_ref, qg_ref, kg_ref, diag_ref, o_ref,
                   q0_s, q1_s, kn_s, v_s, kc_s, vc_s, tab_ref, *, rows, kr):
    ones = _head_ones()
    lo_half = lax.broadcasted_iota(jnp.int32, (1, LANES), 1) < HEAD_DIM
    lo = jnp.where(lo_half, 1.0, 0.0)
    q_col = lax.broadcasted_iota(jnp.int32, (GRID_W, 2 * GRID_W), 0)
    k_col = lax.broadcasted_iota(jnp.int32, (GRID_W, 2 * GRID_W), 1) % GRID_W
    w_start = jnp.clip(q_col - NA_COLS // 2, 0, GRID_W - NA_COLS)
    valid = (k_col >= w_start) & (k_col < w_start + NA_COLS)
    for h in range(2):
        for d in range(2 * NA_ROWS - 2):
            diag = jnp.broadcast_to(diag_ref[h, d:d + 1, :], (GRID_W, 2 * GRID_W))
            tab_ref[h, d] = jnp.where(valid, pltpu.roll(diag, 0, 1, stride=1, stride_axis=0), NEG_INF)
    qn = _qk_norm(q_ref[0], qg_ref[...], ones) * QK_SCALE
    q0_s[...] = (qn * lo).astype(BF16)
    q1_s[...] = (qn * (1.0 - lo)).astype(BF16)
    kn_s[...] = _qk_norm(k_ref[0], kg_ref[...], ones).astype(BF16)
    v_s[...] = v_ref[0].astype(BF16)
    kc_s[...] = kc_ref[0].astype(BF16)
    vc_s[...] = vc_ref[0].astype(BF16)
    win = kr * GRID_W

    def body(it, carry):
        qs, k_rows, q_rows, biases = [], [], [], []
        for s in range(NA_ROW_ILP):
            i = it * NA_ROW_ILP + s
            r0 = jnp.clip(i - kr // 2, 0, rows - kr)
            d0 = r0 - i + (NA_ROWS - 1)
            qr = pl.ds(pl.multiple_of(i * GRID_W, GRID_W), GRID_W)
            q_rows.append(qr)
            k_rows.append(pl.ds(pl.multiple_of(r0 * GRID_W, GRID_W), win))
            qs.append(jnp.concatenate([q0_s[qr, :], q1_s[qr, :]], axis=0))
            biases.append(jnp.concatenate(
                [jnp.concatenate([tab_ref[h, d0 + 2 * m] for m in range(kr // 2)], axis=1) for h in range(2)],
                axis=0))
        lw = [_nt(q, kn_s[kr_, :]) + b for q, kr_, b in zip(qs, k_rows, biases)]
        lc = [jnp.dot(q, kc_s[...], preferred_element_type=F32) for q in qs]
        ms = [jnp.maximum(jnp.max(a, axis=-1, keepdims=True), jnp.max(c, axis=-1, keepdims=True))
              for a, c in zip(lw, lc)]
        pw = [jnp.exp(a - m) for a, m in zip(lw, ms)]
        pc = [jnp.exp(c - m) for c, m in zip(lc, ms)]
        ls = [jnp.sum(a, axis=-1, keepdims=True) + jnp.sum(c, axis=-1, keepdims=True) for a, c in zip(pw, pc)]
        outs = [(jnp.dot(a.astype(BF16), v_s[kr_, :], preferred_element_type=F32)
                 + _nt(c.astype(BF16), vc_s[...])) / l
                for a, c, kr_, l in zip(pw, pc, k_rows, ls)]
        for qr, o in zip(q_rows, outs):
            o_ref[0, qr, :] = jnp.where(lo_half, o[:GRID_W], o[GRID_W:]).astype(o_ref.dtype)
        return carry

    lax.fori_loop(0, rows // NA_ROW_ILP, body, 0, unroll=True)


def _latent_bias_diagonals(rpb):
    c1 = NA_COLS - 1
    assert 2 * c1 + 1 <= GRID_W - c1
    a, b = rpb[:, :-1], rpb[:, 1:]
    gap = jnp.zeros(a.shape[:2] + (GRID_W - 2 * c1 - 1,), rpb.dtype)
    return jnp.concatenate([a[..., c1:], gap, b, gap, a[..., :c1]], axis=-1)


def _na_latent(u_na, bsz, b_off, k_ctx, v_ctx, q_g, k_g, rpb):
    t_len = u_na.shape[1]
    rows = t_len // GRID_W
    kr = min(NA_ROWS, rows)
    assert kr % 2 == 0 and rows % NA_ROW_ILP == 0
    ctx_len = k_ctx.shape[2]
    seg = NA_WIDTH // LANES
    tok = lambda off: pl.BlockSpec((1, t_len, LANES), lambda b, j: (b_off + b, 0, off + j))
    ctx = pl.BlockSpec((1, LANES, ctx_len), lambda b, j: (b, j, 0))
    g2 = lambda g: g.reshape(1, HEAD_DIM)
    diags = _latent_bias_diagonals(rpb)
    tok_s = pltpu.VMEM((t_len, LANES), BF16)
    ctx_s = pltpu.VMEM((LANES, ctx_len), BF16)
    return pl.pallas_call(
        functools.partial(_na_lat_kernel, rows=rows, kr=kr),
        grid=(bsz, seg),
        in_specs=[tok(0), tok(seg), tok(2 * seg), ctx, ctx,
                  _const_spec((1, HEAD_DIM)), _const_spec((1, HEAD_DIM)),
                  pl.BlockSpec((2, 2 * NA_ROWS - 2, 2 * GRID_W), lambda b, j: (j, 0, 0))],
        out_specs=pl.BlockSpec((1, t_len, LANES), lambda b, j: (b, 0, j)),
        out_shape=jax.ShapeDtypeStruct((bsz, t_len, NA_WIDTH), BF16),
        scratch_shapes=[tok_s, tok_s, tok_s, tok_s, ctx_s, ctx_s,
                        pltpu.VMEM((2, 2 * NA_ROWS - 2, GRID_W, 2 * GRID_W), F32)],
        compiler_params=_params(2),
        name="na_latent",
    )(u_na, u_na, u_na, k_ctx, v_ctx, g2(q_g), g2(k_g), diags)


LATE_WEIGHTS = ("w_o_rwkv", "w_o_na", "w_out", "ffn_w1", "ffn_w3", "ffn_w2")


def _out_ffn_kernel(xa_ref, xb_ref, orwa_ref, orwb_ref, onaa_ref, onab_ref, gt_ref, mod_ref, g_ref,
                    wor_ref, won_ref, wout_ref, w1_hbm, w3_hbm, w2_hbm, ya_ref, yb_ref,
                    w1_ref, w3_ref, w2_ref, sem, *, tiles_a):
    i = pl.program_id(0)
    n_chunks = FF_HIDDEN // FF_CHUNK

    def copies(c):
        cols = slice(c * FF_CHUNK, (c + 1) * FF_CHUNK)
        return (pltpu.make_async_copy(w1_hbm.at[:, cols], w1_ref.at[:, cols], sem.at[0, c]),
                pltpu.make_async_copy(w3_hbm.at[:, cols], w3_ref.at[:, cols], sem.at[1, c]),
                pltpu.make_async_copy(w2_hbm.at[cols, :], w2_ref.at[cols, :], sem.at[2, c]))

    @pl.when(i == 0)
    def _():
        for c in range(n_chunks):
            for cp in copies(c):
                cp.start()

    def step(first_step):
        x = _pick_group(i, tiles_a, xa_ref, xb_ref)
        o_rw = _pick_group(i, tiles_a, orwa_ref, orwb_ref)
        o_na = _pick_group(i, tiles_a, onaa_ref, onab_ref)
        g_rw = _sigmoid(gt_ref[:, :D_MODEL].astype(F32))
        g_na = _sigmoid(gt_ref[:, D_MODEL:].astype(F32))
        merged = g_rw * _dot(o_rw, wor_ref[...]) + g_na * _dot(o_na, won_ref[...])
        x1 = x + _mod_part(mod_ref, 2) * _dot(merged, wout_ref[...])
        h2 = _rms_rows(x1) * g_ref[...]
        h2 = (h2 * (1.0 + _mod_part(mod_ref, 4)) + _mod_part(mod_ref, 3)).astype(BF16)

        acc = jnp.zeros(x1.shape, F32)
        for c in range(n_chunks):
            cols = slice(c * FF_CHUNK, (c + 1) * FF_CHUNK)
            if first_step:
                for cp in copies(c):
                    cp.wait()
            a = jnp.dot(h2, w1_ref[:, cols], preferred_element_type=F32)
            b = jnp.dot(h2, w3_ref[:, cols], preferred_element_type=F32)
            hh = (a * _sigmoid(a) * b).astype(BF16)
            acc = acc + jnp.dot(hh, w2_ref[cols, :], preferred_element_type=F32)
        return x1 + _mod_part(mod_ref, 5) * acc

    @pl.when(i == 0)
    def _():
        ya_ref[0] = step(True)

    @pl.when(i > 0)
    def _():
        y = step(False)

        @pl.when(i < tiles_a)
        def _():
            ya_ref[0] = y

        @pl.when(i >= tiles_a)
        def _():
            yb_ref[0] = y


def _out_ffn(xa, xb, orw_a, orw_b, ona_a, ona_b, gates, tiles_per_b, mod_all, norm_g, wb):
    tm = TOKEN_TILE
    tiles_a, tiles = xa.shape[0], xa.shape[0] + xb.shape[0]
    weights = [wb[n] for n in LATE_WEIGHTS]
    groups = lambda width: _group_specs(tiles_a, tm, width)
    return pl.pallas_call(
        functools.partial(_out_ffn_kernel, tiles_a=tiles_a),
        grid=(tiles,),
        in_specs=groups(D_MODEL) + groups(RW_WIDTH) + groups(NA_WIDTH)
        + [pl.BlockSpec((tm, GATE_COLS), lambda i: (i, 0)), _mod_spec(tiles_a, tiles_per_b),
           _const_spec((1, D_MODEL))] + [_const_spec(w.shape) for w in weights[:3]]
        + [pl.BlockSpec(memory_space=pl.ANY)] * 3,
        out_specs=groups(D_MODEL),
        out_shape=[jax.ShapeDtypeStruct(xa.shape, F32), jax.ShapeDtypeStruct(xb.shape, F32)],
        scratch_shapes=[pltpu.VMEM(w.shape, BF16) for w in weights[3:]]
        + [pltpu.SemaphoreType.DMA((3, FF_HIDDEN // FF_CHUNK))],
        compiler_params=_params(1),
        name="out_ffn",
    )(xa, xb, orw_a, orw_b, ona_a, ona_b, gates, mod_all, norm_g.reshape(1, -1), *weights)


def kernel(x_prompt, x_sample, state_rwkv, cache_na_k, cache_na_v, c, c_ctx, norm1_g, norm2_g, w_ada, b_ada,
           w_in, shift_mu, rw_w0, rw_w_up, rw_a0, rw_a_up, rw_g_up, rw_k_k, rw_k_a, rw_r_k, rw_ln_g, rw_ln_b,
           na_q_g, na_k_g, na_rpb, w_o_rwkv, w_o_na, w_out, ffn_w1, ffn_w3, ffn_w2):
    depth = w_in.shape[0]
    bsz, seq = x_prompt.shape[:2]
    dec, dec_seq = x_sample.shape[:2]
    tm = TOKEN_TILE
    n_ctx = bsz * seq
    assert n_ctx % tm == 0 and dec_seq % tm == 0 and n_ctx % dec_seq == 0
    tiles_per_b = dec_seq // tm
    tiled = lambda t: t.reshape(-1, tm, t.shape[-1])
    y_p, y_s = x_prompt, x_sample
    new_s, new_k, new_v = [], [], []
    for l in range(depth):
        p = dict(norm1_g=norm1_g[l], norm2_g=norm2_g[l], shift_mu=shift_mu[l], rw_w0=rw_w0[l],
                 rw_w_up=rw_w_up[l], rw_a0=rw_a0[l], rw_a_up=rw_a_up[l], rw_g_up=rw_g_up[l],
                 rw_k_k=rw_k_k[l], rw_k_a=rw_k_a[l], rw_r_k=rw_r_k[l], rw_ln_g=rw_ln_g[l],
                 rw_ln_b=rw_ln_b[l], na_q_g=na_q_g[l], na_k_g=na_k_g[l], na_rpb=na_rpb[l])
        late_f32 = [w[l] for w in (w_o_rwkv, w_o_na, w_out, ffn_w1, ffn_w3, ffn_w2)]
        mod_all, w_in_bf = _modulation(c_ctx, c, w_ada[l], b_ada[l], w_in[l])
        u_rw, u_na, gates = _in_proj(tiled(y_p), tiled(y_s), tiles_per_b, mod_all, p["norm1_g"], w_in_bf)
        ctx_view = lambda t: t.reshape(-1, seq, t.shape[-1])
        lat_view = lambda t: t.reshape(-1, dec_seq, t.shape[-1])
        lat_off = n_ctx // dec_seq
        o_rw_p, s_l, casted, (o_na_p, k_l, v_l) = _rwkv_branch(
            ctx_view(u_rw), bsz, 0, None, p, RWKV_PAIRS_CTX, RWKV_UNITS, late_f32,
            attend=(ctx_view(u_na), p["na_q_g"], p["na_k_g"]))
        late = dict(zip(LATE_WEIGHTS, casted))
        new_s.append(s_l)
        heads_last = lambda t: jnp.transpose(t.reshape(bsz, NA_HEADS, HEAD_DIM, seq), (0, 3, 1, 2))
        new_k.append(heads_last(k_l))
        new_v.append(heads_last(v_l))
        ctx_k = jnp.transpose(cache_na_k[:, l], (0, 2, 3, 1)).reshape(dec, NA_WIDTH, -1)
        ctx_v = jnp.transpose(cache_na_v[:, l], (0, 2, 3, 1)).reshape(dec, NA_WIDTH, -1)
        o_rw_s = _rwkv_branch(lat_view(u_rw), dec, lat_off, state_rwkv[:, l], p, RWKV_PAIRS_LAT,
                               RWKV_UNITS)[0]
        o_na_s = _na_latent(lat_view(u_na), dec, lat_off, ctx_k, ctx_v, p["na_q_g"], p["na_k_g"], p["na_rpb"])
        y_p_t, y_s_t = _out_ffn(tiled(y_p), tiled(y_s), tiled(o_rw_p), tiled(o_rw_s), tiled(o_na_p), tiled(o_na_s),
                                gates, tiles_per_b, mod_all, p["norm2_g"], late)
        y_p, y_s = y_p_t.reshape(x_prompt.shape), y_s_t.reshape(x_sample.shape)
    return (y_p, y_s, jnp.stack(new_s, axis=1), jnp.stack(new_k, axis=1), jnp.stack(new_v, axis=1))
```

```python
import functools

import numpy as np
import jax
import jax.numpy as jnp
from jax import lax
from jax.experimental import pallas as pl
from jax.experimental.pallas import tpu as pltpu

D_MODEL = 1024
GRID_W = 64
HEAD_DIM = 64
RW_HEADS = 8
RW_WIDTH = RW_HEADS * HEAD_DIM
NA_HEADS = 8
NA_WIDTH = NA_HEADS * HEAD_DIM
LORA_DECAY = 64
LORA_ICLR = 64
LORA_GATE = 128
NA_ROWS = 8
NA_COLS = 16
FF_HIDDEN = 2816
RW_COLS = 3 * RW_WIDTH + 2 * LORA_DECAY + 2 * LORA_ICLR + LORA_GATE
NA_IN_COLS = 3 * NA_WIDTH
GATE_COLS = 2 * D_MODEL
RMS_EPS = 1e-6
GN_EPS = 64e-5
L2_EPS = 1e-12
NEG_INF = -1e30
DECAY_SCALE = float(np.exp(-0.5))
QK_SCALE = HEAD_DIM ** -0.5
assert QK_SCALE == 0.125

LANES = 128
PAIRS = RW_HEADS // 2
CHUNK = 64
STACK = 2 * CHUNK
RWKV_PAIRS_CTX = 4
RWKV_PAIRS_LAT = 2
RWKV_UNITS = 16
TOKEN_TILE = 512
FF_CHUNK = 256
VMEM_LIMIT = 60 * 1024 * 1024

F32 = jnp.float32
BF16 = jnp.bfloat16


def _dot(a, b):
    return jnp.dot(a.astype(BF16), b.astype(BF16), preferred_element_type=F32)


def _split2(x):
    hi = x.astype(BF16)
    lo = (x - hi.astype(F32)).astype(BF16)
    return hi, lo


def _dot_exact_lhs(a_exact, b):
    h, l = _split2(b)
    d = lambda x: jnp.dot(a_exact, x, preferred_element_type=F32)
    return d(h) + d(l)


def _head_ones():
    r = lax.broadcasted_iota(jnp.int32, (LANES, LANES), 0) // HEAD_DIM
    c = lax.broadcasted_iota(jnp.int32, (LANES, LANES), 1) // HEAD_DIM
    return jnp.where(r == c, 1.0, 0.0).astype(BF16)


def _head_sum(x, ones):
    return jnp.dot(x.astype(BF16), ones, preferred_element_type=F32)


def _sigmoid(x):
    return 0.5 * jnp.tanh(0.5 * x) + 0.5


def _rms_rows(x):
    return x * lax.rsqrt(jnp.mean(x * x, axis=-1, keepdims=True) + RMS_EPS)


def _const_spec(shape):
    nd = len(shape)
    return pl.BlockSpec(shape, lambda *_: (0,) * nd, pipeline_mode=pl.Buffered(1))


def _params(n_axes):
    return pltpu.CompilerParams(dimension_semantics=("arbitrary",) * n_axes,
                                vmem_limit_bytes=VMEM_LIMIT)


def _mod_kernel(cc_ref, c_ref, w_ref, b_ref, win_ref, o_ref, winb_ref, s_ref):
    n = 1 + c_ref.shape[0]

    @pl.when(pl.program_id(0) == 0)
    def _():
        c = jnp.concatenate([cc_ref[...], c_ref[...], jnp.zeros((LANES - n, D_MODEL), F32)], axis=0)
        s_ref[...] = (c * _sigmoid(c)).T

    s = s_ref[:, 0:n]
    w = w_ref[...]
    for r in range(n):
        o_ref[r] = jnp.sum(w * s[:, r:r + 1], axis=0, keepdims=True) + b_ref[...]
    winb_ref[...] = win_ref[...].astype(BF16)


def _modulation(c_ctx, c, w_ada, b_ada, w_in):
    n = 1 + c.shape[0]
    tn = 768
    steps = 6 * D_MODEL // tn
    rows = w_in.shape[0] // steps
    assert rows % 16 == 0
    return pl.pallas_call(
        _mod_kernel,
        grid=(steps,),
        in_specs=[pl.BlockSpec((1, D_MODEL), lambda j: (0, 0)),
                  pl.BlockSpec((n - 1, D_MODEL), lambda j: (0, 0)),
                  pl.BlockSpec((D_MODEL, tn), lambda j: (0, j)),
                  pl.BlockSpec((1, tn), lambda j: (0, j)),
                  pl.BlockSpec((rows, w_in.shape[1]), lambda j: (j, 0))],
        out_specs=[pl.BlockSpec((n, 1, tn), lambda j: (0, 0, j)),
                   pl.BlockSpec((rows, w_in.shape[1]), lambda j: (j, 0))],
        out_shape=[jax.ShapeDtypeStruct((n, 1, 6 * D_MODEL), F32),
                   jax.ShapeDtypeStruct(w_in.shape, BF16)],
        scratch_shapes=[pltpu.VMEM((D_MODEL, LANES), F32)],
        compiler_params=_params(1),
        name="modulation",
    )(c_ctx.reshape(1, -1), c, w_ada, b_ada.reshape(1, -1), w_in)


def _pick_group(i, tiles_a, a_ref, b_ref):
    return jnp.where(i < tiles_a, a_ref[0], b_ref[0])


def _group_specs(tiles_a, tm, width):
    return [pl.BlockSpec((1, tm, width), lambda i: (jnp.minimum(i, tiles_a - 1), 0, 0)),
            pl.BlockSpec((1, tm, width), lambda i: (jnp.maximum(i - tiles_a, 0), 0, 0))]


def _mod_spec(tiles_a, tiles_per_b):
    return pl.BlockSpec((1, 1, 6 * D_MODEL),
                        lambda i: (jnp.where(i < tiles_a, 0, 1 + (i - tiles_a) // tiles_per_b), 0, 0))


def _mod_part(mod_ref, k):
    return mod_ref[0, :, k * D_MODEL:(k + 1) * D_MODEL]


INPROJ_CHUNK = 512


def _inproj_kernel(xa_ref, xb_ref, mod_ref, g_ref, w_hbm, urw_ref, una_ref, gt_ref, w_ref, sem, *, tiles_a):
    i = pl.program_id(0)
    outs = ((urw_ref, 0, RW_COLS), (una_ref, RW_COLS, NA_IN_COLS), (gt_ref, RW_COLS + NA_IN_COLS, GATE_COLS))
    chunks = [(o_ref, base, off, min(INPROJ_CHUNK, width - off))
              for o_ref, base, width in outs for off in range(0, width, INPROJ_CHUNK)]

    def copy(n):
        _, base, off, size = chunks[n]
        cols = slice(base + off, base + off + size)
        return pltpu.make_async_copy(w_hbm.at[:, cols], w_ref.at[:, cols], sem.at[n])

    @pl.when(i == 0)
    def _():
        for n in range(len(chunks)):
            copy(n).start()

    def step(first_step):
        x = _pick_group(i, tiles_a, xa_ref, xb_ref)
        h = _rms_rows(x) * g_ref[...]
        h = (h * (1.0 + _mod_part(mod_ref, 1)) + _mod_part(mod_ref, 0)).astype(BF16)
        d = lambda lo, hi: jnp.dot(h, w_ref[:, lo:hi], preferred_element_type=F32)
        if first_step:
            for n, (o_ref, base, off, size) in enumerate(chunks):
                copy(n).wait()
                o_ref[:, off:off + size] = d(base + off, base + off + size).astype(o_ref.dtype)
        else:
            for o_ref, base, width in outs:
                o_ref[...] = d(base, base + width).astype(o_ref.dtype)

    @pl.when(i == 0)
    def _():
        step(True)

    @pl.when(i > 0)
    def _():
        step(False)


def _in_proj(xa, xb, tiles_per_b, mod_all, norm_g, w_in_bf):
    tm = TOKEN_TILE
    tiles_a, tiles = xa.shape[0], xa.shape[0] + xb.shape[0]
    row = lambda i: (i, 0)
    return pl.pallas_call(
        functools.partial(_inproj_kernel, tiles_a=tiles_a),
        grid=(tiles,),
        in_specs=_group_specs(tiles_a, tm, D_MODEL) + [_mod_spec(tiles_a, tiles_per_b),
                                                       _const_spec((1, D_MODEL)),
                                                       pl.BlockSpec(memory_space=pl.ANY)],
        out_specs=[pl.BlockSpec((tm, RW_COLS), row),
                   pl.BlockSpec((tm, NA_IN_COLS), row),
                   pl.BlockSpec((tm, GATE_COLS), row)],
        out_shape=[jax.ShapeDtypeStruct((tiles * tm, RW_COLS), F32),
                   jax.ShapeDtypeStruct((tiles * tm, NA_IN_COLS), F32),
                   jax.ShapeDtypeStruct((tiles * tm, GATE_COLS), BF16)],
        scratch_shapes=[pltpu.VMEM(w_in_bf.shape, BF16), pltpu.SemaphoreType.DMA((sum(-(-w // INPROJ_CHUNK) for w in (RW_COLS, NA_IN_COLS, GATE_COLS)),))],
        compiler_params=_params(1),
        name="in_proj",
    )(xa, xb, mod_all, norm_g.reshape(1, -1), w_in_bf)


def _shift(x, mu):
    t_len = x.shape[0]
    row = lax.broadcasted_iota(jnp.int32, x.shape, 0)
    prev = jnp.where(row == 0, 0.0, pltpu.roll(x, 1, 0))
    nxt = jnp.where(row == t_len - 1, 0.0, pltpu.roll(x, t_len - 1, 0))
    return x + mu[0:1, :] * (prev - x) + mu[1:2, :] * (nxt - x)


def _stack_heads(x, lane_lo):
    return jnp.concatenate([x * lane_lo, x * (1.0 - lane_lo)], axis=0)


def _wkv_intra(units, consts):
    tri, mask_s, mask_i, eye, lane_lo, blk = consts
    stack = lambda z: _stack_heads(z, lane_lo)

    cums = [_dot_exact_lhs(tri[int(u[6])], u[1]) for u in units]

    prep = []
    for (r, lw, kd, v, kk, b, reverse), cum in zip(units, cums):
        mid_row = CHUNK // 2 if reverse else CHUNK // 2 - 1
        tot_row = 0 if reverse else CHUNK - 1
        a = -kk
        ex = cum - lw
        mid = cum[mid_row:mid_row + 1, :]
        tot = cum[tot_row:tot_row + 1, :]
        up = jnp.exp(cum - mid)
        dn = jnp.exp(mid - cum)
        tail = jnp.exp(tot - cum)
        prep.append(dict(
            at_m=stack(a * jnp.exp(ex - mid)).astype(BF16),
            rt_m=stack(r * up).astype(BF16),
            bk_m=jnp.concatenate([b * dn, kd * dn], axis=0).astype(BF16),
            a_e=stack(a * jnp.exp(ex)),
            r_e=stack(r * jnp.exp(cum)),
            bk_t=jnp.concatenate([stack(b * tail), stack(kd * tail)], axis=0).T.astype(BF16),
            vv=stack(v).astype(BF16),
            diag=jnp.where(eye, jnp.exp(tot), 0.0),
            rev=int(reverse)))

    ntd = lambda x, y: lax.dot_general(x, y, (((1,), (1,)), ((), ())), preferred_element_type=F32)
    mm = lambda x, y: jnp.dot(x, y, preferred_element_type=F32)
    diag_blk, swap_eye = blk
    roll_head = lambda z: pltpu.roll(z, HEAD_DIM, 1)
    by_block = lambda z: jnp.concatenate([z[:CHUNK], roll_head(z[CHUNK:])], axis=0)
    top = [by_block(ntd(p["at_m"], p["bk_m"])) for p in prep]
    low = [by_block(ntd(p["rt_m"], p["bk_m"])) for p in prep]
    a_ak = [(roll_head(t) * mask_s[p["rev"]]).astype(BF16) for t, p in zip(top, prep)]
    bot = [jnp.concatenate([t * mask_i[p["rev"]], roll_head(t) * mask_i[p["rev"]]], axis=1).astype(BF16)
           for t, p in zip(low, prep)]

    off_blk = 1.0 - diag_blk
    both = [t * mask_s[p["rev"]] + swap_eye for t, p in zip(top, prep)]
    steps = CHUNK.bit_length() - 1
    diag_bf = diag_blk.astype(BF16)
    for j in range(steps):
        packed = [q.astype(BF16) for q in both]
        res = [mm(qb * diag_bf, qb) for qb in packed]
        both = [r + off_blk * q for r, q in zip(res, both)]
    ts = [pltpu.roll(q, HEAD_DIM, 1).astype(BF16) for q in both]
    x0 = [jnp.concatenate([p["a_e"], mm(ak, p["vv"])], axis=1) for p, ak in zip(prep, a_ak)]
    xs = [mm(t, x.astype(BF16)) for x, t in zip(x0, ts)]

    out = []
    zeros = jnp.zeros((STACK, LANES), BF16)
    for p, x, bt in zip(prep, xs, bot):
        rhs = jnp.concatenate([x.astype(BF16), jnp.concatenate([zeros, p["vv"]], axis=1)], axis=0)
        lhs = jnp.concatenate([bt, p["bk_t"]], axis=0)
        res = mm(lhs, rhs)
        lhs2 = res[:, :LANES] + jnp.concatenate([p["r_e"], p["diag"]], axis=0)
        out.append((lhs2.astype(BF16), res[:, LANES:]))
    return out


def _wkv_constants():
    lane_head = np.arange(LANES) // HEAD_DIM
    ones = (lane_head[:, None] == lane_head[None, :]).astype(np.float32)
    t = np.arange(CHUNK)
    tri = np.stack([t[None, :] <= t[:, None], t[None, :] >= t[:, None]]).astype(np.float32)
    rs, cs = t[:, None], t[None, :]
    tri_masks = np.stack([cs < rs, cs > rs, cs <= rs, cs >= rs]).astype(np.float32)
    masks = np.kron(np.eye(2, dtype=np.float32), tri_masks)
    return jnp.asarray(ones, BF16), jnp.asarray(tri, BF16), jnp.asarray(masks, F32)


def _rwkv_kernel(*refs, t_len, has_s0, pairs, units, n_cast, attend):
    (r_ref, k_ref, v_ref, lo_ref, mur_ref, muk_ref, muv_ref, mul_ref, w0_ref, a0_ref, wup_ref, aup_ref,
     gup_ref, kk_ref, ka_ref, rk_ref, lng_ref, lnb_ref, ones_ref, tri_ref, mask_ref) = refs[:21]
    pos = 21
    s0_ref = None
    if has_s0:
        s0_ref = refs[pos]
        pos += 1
    cast_in = refs[pos:pos + n_cast]
    pos += n_cast
    n_att_in, n_att_out = (5, 3) if attend else (0, 0)
    att_in = refs[pos:pos + n_att_in]
    pos += n_att_in
    o_ref, sn_ref = refs[pos], refs[pos + 1]
    cast_out = refs[pos + 2:pos + 2 + n_cast]
    pos += 2 + n_cast
    att_out = refs[pos:pos + n_att_out]
    (r_s, v_s, kk_s, b0_s, b1_s, lw0_s, lw1_s, kd0_s, kd1_s, gate_s, bonus_s, yf_s, yb_s,
     lhs_s, add_s, st_s) = refs[pos + n_att_out:]
    if attend:
        _na_ctx_kernel(*att_in, *att_out)
    for w_in_ref, w_out_ref in zip(cast_in, cast_out):
        w_out_ref[...] = w_in_ref[...].astype(BF16)

    ones = ones_ref[...]
    lane = lax.broadcasted_iota(jnp.int32, (1, LANES), 1)
    lo_half = lane < HEAD_DIM
    lane_lo = jnp.where(lo_half, 1.0, 0.0)
    mm = lambda x, y: jnp.dot(x, y, preferred_element_type=F32)

    lo = _shift(lo_ref[0], mul_ref[...])
    wd = jnp.tanh(lo[:, 0:LANES])
    ad = lo[:, LANES:2 * LANES]
    sig_gd = _sigmoid(lo[:, 2 * LANES:3 * LANES]).astype(BF16)
    wd_split = [_split2(wd * m) for m in (lane_lo, 1.0 - lane_lo)]
    ad_bf = [(ad * m).astype(BF16) for m in (lane_lo, 1.0 - lane_lo)]
    for j in range(pairs):
        cols = slice(j * LANES, (j + 1) * LANES)
        r = _shift(r_ref[0, :, cols], mur_ref[:, cols])
        k = _shift(k_ref[0, :, cols], muk_ref[:, cols])
        v = _shift(v_ref[0, :, cols], muv_ref[:, cols])
        kk = k * kk_ref[:, cols]
        kk = kk * lax.rsqrt(_head_sum(kk * kk, ones) + L2_EPS)
        wup_h, wup_l = _split2(wup_ref[:, cols])
        aup = aup_ref[:, cols].astype(BF16)
        kdirs = []
        for e, (lw_s, kd_s, b_s) in enumerate(((lw0_s, kd0_s, b0_s), (lw1_s, kd1_s, b1_s))):
            wd_h, wd_l = wd_split[e]
            w_lin = w0_ref[e:e + 1, cols] + (mm(wd_h, wup_h) + mm(wd_l, wup_h) + mm(wd_h, wup_l))
            lw_s[j] = -DECAY_SCALE * _sigmoid(w_lin)
            iclr = _sigmoid(a0_ref[e:e + 1, cols] + mm(ad_bf[e], aup))
            kd = k * (1.0 + (iclr - 1.0) * ka_ref[:, cols])
            kd_s[j] = kd
            b_s[j] = kk * iclr
            kdirs.append(kd)
        gate_s[:, cols] = mm(sig_gd, gup_ref[:, cols].astype(BF16))
        h0 = 2 * (pl.program_id(1) * pairs + j)
        rk = jnp.concatenate([rk_ref[pl.ds(h0, 1), :], rk_ref[pl.ds(h0 + 1, 1), :]], axis=1)
        bonus_s[:, cols] = _head_sum(r * (0.5 * (kdirs[0] + kdirs[1])) * rk, ones) * v
        r_s[j] = r
        v_s[j] = v
        kk_s[j] = kk

    n_chunks = t_len // CHUNK
    chunks_per = min(n_chunks, units // 2)
    pairs_per = min(pairs, units // (2 * chunks_per))
    groups = n_chunks // chunks_per
    rs = lax.broadcasted_iota(jnp.int32, (STACK, STACK), 0)
    cs = lax.broadcasted_iota(jnp.int32, (STACK, STACK), 1)
    as_f32 = lambda m: jnp.where(m, 1.0, 0.0)
    blk = (as_f32(rs // CHUNK == cs // CHUNK), as_f32(cs == (rs + CHUNK) % STACK))
    consts = ((tri_ref[0], tri_ref[1]), (mask_ref[0], mask_ref[1]), (mask_ref[2], mask_ref[3]),
              rs == cs, lane_lo, blk)
    dirs = ((lw0_s, kd0_s, b0_s), (lw1_s, kd1_s, b1_s))

    def intra_body(it, carry):
        pg = it // groups
        g = it % groups
        units_, ids = [], []
        for jj in range(pairs_per):
            j = pg * pairs_per + jj
            for cc in range(chunks_per):
                c = g * chunks_per + cc
                rows = pl.ds(pl.multiple_of(c * CHUNK, CHUNK), CHUNK)
                for e, (lw_s, kd_s, b_s) in enumerate(dirs):
                    units_.append((r_s[j, rows, :], lw_s[j, rows, :], kd_s[j, rows, :], v_s[j, rows, :],
                                   kk_s[j, rows, :], b_s[j, rows, :], e == 1))
                    ids.append((j * 2 + e) * n_chunks + c)
        for uid, (lhs, add) in zip(ids, _wkv_intra(units_, consts)):
            lhs_s[uid] = lhs
            add_s[uid] = add
        return carry

    lax.fori_loop(0, (pairs // pairs_per) * groups, intra_body, 0, unroll=2)

    same_head = rs // HEAD_DIM == cs // HEAD_DIM
    for j in range(pairs):
        for e in range(2):
            if has_s0:
                both = jnp.concatenate([s0_ref[0, e, 2 * j], s0_ref[0, e, 2 * j + 1]], axis=1)
                st_s[2 * j + e] = jnp.where(same_head, jnp.concatenate([both, both], axis=0), 0.0).T
            else:
                st_s[2 * j + e] = jnp.zeros((LANES, LANES), F32)

    def state_body(it, carry):
        chunk = (it, n_chunks - 1 - it)
        uids = [(j * 2 + e) * n_chunks + chunk[e] for j in range(pairs) for e in range(2)]
        sts = [st_s[ch].astype(BF16) for ch in range(2 * pairs)]
        res = [mm(lhs_s[uid], st) + add_s[uid] for uid, st in zip(uids, sts)]
        for ch, rr in enumerate(res):
            j, e = divmod(ch, 2)
            y_s = yb_s if e else yf_s
            y_s[j, pl.ds(pl.multiple_of(chunk[e] * CHUNK, CHUNK), CHUNK), :] = rr[:CHUNK] + rr[CHUNK:STACK]
            st_s[ch] = rr[STACK:]
        return carry

    lax.fori_loop(0, n_chunks, state_body, 0, unroll=True)
    for j in range(pairs):
        for e in range(2):
            st_t = st_s[2 * j + e].T
            sn_ref[0, e, 2 * j] = st_t[:HEAD_DIM, :HEAD_DIM]
            sn_ref[0, e, 2 * j + 1] = pltpu.roll(st_t, HEAD_DIM, 1)[HEAD_DIM:, :HEAD_DIM]

    inv_d = 1.0 / HEAD_DIM
    for j in range(pairs):
        cols = slice(j * LANES, (j + 1) * LANES)
        y = yf_s[j] + yb_s[j]
        mean = _head_sum(y, ones) * inv_d
        dlt = y - mean
        var = _head_sum(dlt * dlt, ones) * inv_d
        yn = dlt * lax.rsqrt(var + GN_EPS) * lng_ref[:, cols] + lnb_ref[:, cols]
        o_ref[0, :, cols] = ((yn + bonus_s[:, cols]) * gate_s[:, cols]).astype(o_ref.dtype)


def _rwkv_branch(u_rw, bsz, b_off, s0, p, pairs, units, cast=(), attend=None):
    t_len = u_rw.shape[1]
    has_s0 = s0 is not None
    width = pairs * LANES
    seg = RW_WIDTH // width
    tok = lambda off: pl.BlockSpec((1, t_len, width), lambda b, j: (b_off + b, 0, off + j))
    mu = lambda off: pl.BlockSpec((2, width), lambda b, j: (0, off + j))
    vec2 = pl.BlockSpec((2, width), lambda b, j: (0, j))
    vec1 = pl.BlockSpec((1, width), lambda b, j: (0, j))
    mat = pl.BlockSpec((LANES, width), lambda b, j: (0, j))
    lora_w = 3 * LANES
    lora_blk = 3 * RW_WIDTH // lora_w
    in_specs = [tok(0), tok(seg), tok(2 * seg),
                pl.BlockSpec((1, t_len, lora_w), lambda b, j: (b_off + b, 0, lora_blk)),
                mu(0), mu(seg), mu(2 * seg),
                pl.BlockSpec((2, lora_w), lambda b, j: (0, lora_blk)),
                vec2, vec2, mat, mat, mat, vec1, vec1, _const_spec((RW_HEADS, HEAD_DIM)), vec1, vec1,
                _const_spec((LANES, LANES)), _const_spec((2, CHUNK, CHUNK)), _const_spec((4, STACK, STACK))]
    args = [u_rw, u_rw, u_rw, u_rw, p["shift_mu"], p["shift_mu"], p["shift_mu"], p["shift_mu"],
            p["rw_w0"], p["rw_a0"],
            p["rw_w_up"].reshape(2 * LORA_DECAY, RW_WIDTH), p["rw_a_up"].reshape(2 * LORA_ICLR, RW_WIDTH),
            p["rw_g_up"], p["rw_k_k"].reshape(1, -1), p["rw_k_a"].reshape(1, -1),
            p["rw_r_k"], p["rw_ln_g"].reshape(1, -1), p["rw_ln_b"].reshape(1, -1),
            *_wkv_constants()]
    st_spec = pl.BlockSpec((1, 2, 2 * pairs, HEAD_DIM, HEAD_DIM), lambda b, j: (b, 0, j, 0, 0))
    if has_s0:
        in_specs.append(st_spec)
        args.append(s0)
    out_specs = [pl.BlockSpec((1, t_len, width), lambda b, j: (b, 0, j)),
                 pl.BlockSpec((1, 2, 2 * pairs, HEAD_DIM, HEAD_DIM), lambda b, j: (b, 0, j, 0, 0))]
    out_shape = [jax.ShapeDtypeStruct((bsz, t_len, RW_WIDTH), BF16),
                 jax.ShapeDtypeStruct((bsz, 2, RW_HEADS, HEAD_DIM, HEAD_DIM), F32)]
    for w in cast:
        assert PAIRS == pairs and w.shape[0] % (16 * bsz) == 0
        blk = pl.BlockSpec((w.shape[0] // bsz, w.shape[1]), lambda b, j: (b, 0))
        in_specs.append(blk)
        args.append(w)
        out_specs.append(blk)
        out_shape.append(jax.ShapeDtypeStruct(w.shape, BF16))
    if attend is not None:
        u_na, q_g, k_g = attend
        assert PAIRS == pairs
        na_tok = lambda seg_: pl.BlockSpec((1, t_len, NA_WIDTH), lambda b, j: (b_off + b, 0, seg_))
        na_out = pl.BlockSpec((1, t_len, NA_WIDTH), lambda b, j: (b, 0, 0))
        g2 = lambda g: g.reshape(1, HEAD_DIM)
        in_specs += [na_tok(0), na_tok(1), na_tok(2), _const_spec((1, HEAD_DIM)), _const_spec((1, HEAD_DIM))]
        args += [u_na, u_na, u_na, g2(q_g), g2(k_g)]
        na_out_t = pl.BlockSpec((1, NA_WIDTH, t_len), lambda b, j: (b, 0, 0))
        out_specs += [na_out, na_out_t, na_out_t]
        na_shape = (bsz, t_len, NA_WIDTH)
        na_shape_t = (bsz, NA_WIDTH, t_len)
        out_shape += [jax.ShapeDtypeStruct(na_shape, BF16), jax.ShapeDtypeStruct(na_shape_t, F32),
                      jax.ShapeDtypeStruct(na_shape_t, F32)]
    n_units = 2 * pairs * (t_len // CHUNK)
    per_pair = pltpu.VMEM((pairs, t_len, LANES), F32)
    full = pltpu.VMEM((t_len, width), F32)
    scratch = [per_pair] * 9 + [full, full, per_pair, per_pair,
                                pltpu.VMEM((n_units, 2 * STACK, LANES), BF16),
                                pltpu.VMEM((n_units, 2 * STACK, LANES), F32),
                                pltpu.VMEM((2 * pairs, LANES, LANES), F32)]
    o_rw, s_new, *rest = pl.pallas_call(
        functools.partial(_rwkv_kernel, t_len=t_len, has_s0=has_s0, pairs=pairs, units=units, n_cast=len(cast),
                          attend=attend is not None),
        grid=(bsz, PAIRS // pairs),
        in_specs=in_specs,
        out_specs=out_specs,
        out_shape=out_shape,
        scratch_shapes=scratch,
        compiler_params=_params(2),
        name="rwkv_branch",
    )(*args)
    return o_rw, s_new, rest[:len(cast)], rest[len(cast):]


def _qk_norm(t, g, ones):
    ms = _head_sum(t * t, ones) * (1.0 / HEAD_DIM)
    return t * lax.rsqrt(ms + RMS_EPS) * jnp.concatenate([g, g], axis=1)


def _nt(x, y):
    return lax.dot_general(x, y, (((1,), (1,)), ((), ())), preferred_element_type=F32)


def _na_ctx_kernel(q_ref, k_ref, v_ref, qg_ref, kg_ref, o_ref, kn_ref, vc_ref):
    ones = _head_ones()
    lo_half = lax.broadcasted_iota(jnp.int32, (1, LANES), 1) < HEAD_DIM
    lo = jnp.where(lo_half, 1.0, 0.0)
    t_len = q_ref.shape[1]
    qs, ks, vs = [], [], []
    for j in range(NA_WIDTH // LANES):
        cols = slice(j * LANES, (j + 1) * LANES)
        qn = _qk_norm(q_ref[0, :, cols], qg_ref[...], ones)
        kn = _qk_norm(k_ref[0, :, cols], kg_ref[...], ones)
        v = v_ref[0, :, cols]
        kn_ref[0, cols, :] = kn.T
        vc_ref[0, cols, :] = v.T
        qn = qn * QK_SCALE
        qs.append(jnp.concatenate([qn * lo, qn * (1.0 - lo)], axis=0).astype(BF16))
        ks.append(kn.astype(BF16))
        vs.append(v.astype(BF16))
    logits = [_nt(q, k) for q, k in zip(qs, ks)]
    ms = [jnp.max(s, axis=-1, keepdims=True) for s in logits]
    ps = [jnp.exp(s - m) for s, m in zip(logits, ms)]
    ls = [jnp.sum(p, axis=-1, keepdims=True) for p in ps]
    outs = [jnp.dot(p.astype(BF16), v, preferred_element_type=F32) / l for p, v, l in zip(ps, vs, ls)]
    for j, o in enumerate(outs):
        o_ref[0, :, j * LANES:(j + 1) * LANES] = jnp.where(lo_half, o[:t_len], o[t_len:]).astype(o_ref.dtype)


NA_ROW_ILP = 4


def _na_lat_kernel(q_ref, k_ref, v_ref, kc_ref, vc_ref, qg_ref, kg_ref, diag_ref, o_ref,
                   q0_s, q1_s, kn_s, v_s, kc_s, vc_s, tab_ref, *, rows, kr):
    ones = _head_ones()
    lo_half = lax.broadcasted_iota(jnp.int32, (1, LANES), 1) < HEAD_DIM
    lo = jnp.where(lo_half, 1.0, 0.0)
    q_col = lax.broadcasted_iota(jnp.int32, (GRID_W, 2 * GRID_W), 0)
    k_col = lax.broadcasted_iota(jnp.int32, (GRID_W, 2 * GRID_W), 1) % GRID_W
    w_start = jnp.clip(q_col - NA_COLS // 2, 0, GRID_W - NA_COLS)
    valid = (k_col >= w_start) & (k_col < w_start + NA_COLS)
    for h in range(2):
        for d in range(2 * NA_ROWS - 2):
            diag = jnp.broadcast_to(diag_ref[h, d:d + 1, :], (GRID_W, 2 * GRID_W))
            tab_ref[h, d] = jnp.where(valid, pltpu.roll(diag, 0, 1, stride=1, stride_axis=0), NEG_INF)
    qn = _qk_norm(q_ref[0], qg_ref[...], ones) * QK_SCALE
    q0_s[...] = (qn * lo).astype(BF16)
    q1_s[...] = (qn * (1.0 - lo)).astype(BF16)
    kn_s[...] = _qk_norm(k_ref[0], kg_ref[...], ones).astype(BF16)
    v_s[...] = v_ref[0].astype(BF16)
    kc_s[...] = kc_ref[0].astype(BF16)
    vc_s[...] = vc_ref[0].astype(BF16)
    win = kr * GRID_W

    def body(it, carry):
        qs, k_rows, q_rows, biases = [], [], [], []
        for s in range(NA_ROW_ILP):
            i = it * NA_ROW_ILP + s
            r0 = jnp.clip(i - kr // 2, 0, rows - kr)
            d0 = r0 - i + (NA_ROWS - 1)
            qr = pl.ds(pl.multiple_of(i * GRID_W, GRID_W), GRID_W)
            q_rows.append(qr)
            k_rows.append(pl.ds(pl.multiple_of(r0 * GRID_W, GRID_W), win))
            qs.append(jnp.concatenate([q0_s[qr, :], q1_s[qr, :]], axis=0))
            biases.append(jnp.concatenate(
                [jnp.concatenate([tab_ref[h, d0 + 2 * m] for m in range(kr // 2)], axis=1) for h in range(2)],
                axis=0))
        lw = [_nt(q, kn_s[kr_, :]) + b for q, kr_, b in zip(qs, k_rows, biases)]
        lc = [jnp.dot(q, kc_s[...], preferred_element_type=F32) for q in qs]
        ms = [jnp.maximum(jnp.max(a, axis=-1, keepdims=True), jnp.max(c, axis=-1, keepdims=True))
              for a, c in zip(lw, lc)]
        pw = [jnp.exp(a - m) for a, m in zip(lw, ms)]
        pc = [jnp.exp(c - m) for c, m in zip(lc, ms)]
        ls = [jnp.sum(a, axis=-1, keepdims=True) + jnp.sum(c, axis=-1, keepdims=True) for a, c in zip(pw, pc)]
        outs = [(jnp.dot(a.astype(BF16), v_s[kr_, :], preferred_element_type=F32)
                 + _nt(c.astype(BF16), vc_s[...])) / l
                for a, c, kr_, l in zip(pw, pc, k_rows, ls)]
        for qr, o in zip(q_rows, outs):
            o_ref[0, qr, :] = jnp.where(lo_half, o[:GRID_W], o[GRID_W:]).astype(o_ref.dtype)
        return carry

    lax.fori_loop(0, rows // NA_ROW_ILP, body, 0, unroll=True)


def _latent_bias_diagonals(rpb):
    c1 = NA_COLS - 1
    assert 2 * c1 + 1 <= GRID_W - c1
    a, b = rpb[:, :-1], rpb[:, 1:]
    gap = jnp.zeros(a.shape[:2] + (GRID_W - 2 * c1 - 1,), rpb.dtype)
    return jnp.concatenate([a[..., c1:], gap, b, gap, a[..., :c1]], axis=-1)


def _na_latent(u_na, bsz, b_off, k_ctx, v_ctx, q_g, k_g, rpb):
    t_len = u_na.shape[1]
    rows = t_len // GRID_W
    kr = min(NA_ROWS, rows)
    assert kr % 2 == 0 and rows % NA_ROW_ILP == 0
    ctx_len = k_ctx.shape[2]
    seg = NA_WIDTH // LANES
    tok = lambda off: pl.BlockSpec((1, t_len, LANES), lambda b, j: (b_off + b, 0, off + j))
    ctx = pl.BlockSpec((1, LANES, ctx_len), lambda b, j: (b, j, 0))
    g2 = lambda g: g.reshape(1, HEAD_DIM)
    diags = _latent_bias_diagonals(rpb)
    tok_s = pltpu.VMEM((t_len, LANES), BF16)
    ctx_s = pltpu.VMEM((LANES, ctx_len), BF16)
    return pl.pallas_call(
        functools.partial(_na_lat_kernel, rows=rows, kr=kr),
        grid=(bsz, seg),
        in_specs=[tok(0), tok(seg), tok(2 * seg), ctx, ctx,
                  _const_spec((1, HEAD_DIM)), _const_spec((1, HEAD_DIM)),
                  pl.BlockSpec((2, 2 * NA_ROWS - 2, 2 * GRID_W), lambda b, j: (j, 0, 0))],
        out_specs=pl.BlockSpec((1, t_len, LANES), lambda b, j: (b, 0, j)),
        out_shape=jax.ShapeDtypeStruct((bsz, t_len, NA_WIDTH), BF16),
        scratch_shapes=[tok_s, tok_s, tok_s, tok_s, ctx_s, ctx_s,
                        pltpu.VMEM((2, 2 * NA_ROWS - 2, GRID_W, 2 * GRID_W), F32)],
        compiler_params=_params(2),
        name="na_latent",
    )(u_na, u_na, u_na, k_ctx, v_ctx, g2(q_g), g2(k_g), diags)


LATE_WEIGHTS = ("w_o_rwkv", "w_o_na", "w_out", "ffn_w1", "ffn_w3", "ffn_w2")


def _out_ffn_kernel(xa_ref, xb_ref, orwa_ref, orwb_ref, onaa_ref, onab_ref, gt_ref, mod_ref, g_ref,
                    wor_ref, won_ref, wout_ref, w1_hbm, w3_hbm, w2_hbm, ya_ref, yb_ref,
                    w1_ref, w3_ref, w2_ref, sem, *, tiles_a):
    i = pl.program_id(0)
    n_chunks = FF_HIDDEN // FF_CHUNK

    def copies(c):
        cols = slice(c * FF_CHUNK, (c + 1) * FF_CHUNK)
        return (pltpu.make_async_copy(w1_hbm.at[:, cols], w1_ref.at[:, cols], sem.at[0, c]),
                pltpu.make_async_copy(w3_hbm.at[:, cols], w3_ref.at[:, cols], sem.at[1, c]),
                pltpu.make_async_copy(w2_hbm.at[cols, :], w2_ref.at[cols, :], sem.at[2, c]))

    @pl.when(i == 0)
    def _():
        for c in range(n_chunks):
            for cp in copies(c):
                cp.start()

    def step(first_step):
        x = _pick_group(i, tiles_a, xa_ref, xb_ref)
        o_rw = _pick_group(i, tiles_a, orwa_ref, orwb_ref)
        o_na = _pick_group(i, tiles_a, onaa_ref, onab_ref)
        g_rw = _sigmoid(gt_ref[:, :D_MODEL].astype(F32))
        g_na = _sigmoid(gt_ref[:, D_MODEL:].astype(F32))
        merged = g_rw * _dot(o_rw, wor_ref[...]) + g_na * _dot(o_na, won_ref[...])
        x1 = x + _mod_part(mod_ref, 2) * _dot(merged, wout_ref[...])
        h2 = _rms_rows(x1) * g_ref[...]
        h2 = (h2 * (1.0 + _mod_part(mod_ref, 4)) + _mod_part(mod_ref, 3)).astype(BF16)

        acc = jnp.zeros(x1.shape, F32)
        for c in range(n_chunks):
            cols = slice(c * FF_CHUNK, (c + 1) * FF_CHUNK)
            if first_step:
                for cp in copies(c):
                    cp.wait()
            a = jnp.dot(h2, w1_ref[:, cols], preferred_element_type=F32)
            b = jnp.dot(h2, w3_ref[:, cols], preferred_element_type=F32)
            hh = (a * _sigmoid(a) * b).astype(BF16)
            acc = acc + jnp.dot(hh, w2_ref[cols, :], preferred_element_type=F32)
        return x1 + _mod_part(mod_ref, 5) * acc

    @pl.when(i == 0)
    def _():
        ya_ref[0] = step(True)

    @pl.when(i > 0)
    def _():
        y = step(False)

        @pl.when(i < tiles_a)
        def _():
            ya_ref[0] = y

        @pl.when(i >= tiles_a)
        def _():
            yb_ref[0] = y


def _out_ffn(xa, xb, orw_a, orw_b, ona_a, ona_b, gates, tiles_per_b, mod_all, norm_g, wb):
    tm = TOKEN_TILE
    tiles_a, tiles = xa.shape[0], xa.shape[0] + xb.shape[0]
    weights = [wb[n] for n in LATE_WEIGHTS]
    groups = lambda width: _group_specs(tiles_a, tm, width)
    return pl.pallas_call(
        functools.partial(_out_ffn_kernel, tiles_a=tiles_a),
        grid=(tiles,),
        in_specs=groups(D_MODEL) + groups(RW_WIDTH) + groups(NA_WIDTH)
        + [pl.BlockSpec((tm, GATE_COLS), lambda i: (i, 0)), _mod_spec(tiles_a, tiles_per_b),
           _const_spec((1, D_MODEL))] + [_const_spec(w.shape) for w in weights[:3]]
        + [pl.BlockSpec(memory_space=pl.ANY)] * 3,
        out_specs=groups(D_MODEL),
        out_shape=[jax.ShapeDtypeStruct(xa.shape, F32), jax.ShapeDtypeStruct(xb.shape, F32)],
        scratch_shapes=[pltpu.VMEM(w.shape, BF16) for w in weights[3:]]
        + [pltpu.SemaphoreType.DMA((3, FF_HIDDEN // FF_CHUNK))],
        compiler_params=_params(1),
        name="out_ffn",
    )(xa, xb, orw_a, orw_b, ona_a, ona_b, gates, mod_all, norm_g.reshape(1, -1), *weights)


def kernel(x_prompt, x_sample, state_rwkv, cache_na_k, cache_na_v, c, c_ctx, norm1_g, norm2_g, w_ada, b_ada,
           w_in, shift_mu, rw_w0, rw_w_up, rw_a0, rw_a_up, rw_g_up, rw_k_k, rw_k_a, rw_r_k, rw_ln_g, rw_ln_b,
           na_q_g, na_k_g, na_rpb, w_o_rwkv, w_o_na, w_out, ffn_w1, ffn_w3, ffn_w2):
    depth = w_in.shape[0]
    bsz, seq = x_prompt.shape[:2]
    dec, dec_seq = x_sample.shape[:2]
    tm = TOKEN_TILE
    n_ctx = bsz * seq
    assert n_ctx % tm == 0 and dec_seq % tm == 0 and n_ctx % dec_seq == 0
    tiles_per_b = dec_seq // tm
    tiled = lambda t: t.reshape(-1, tm, t.shape[-1])
    y_p, y_s = x_prompt, x_sample
    new_s, new_k, new_v = [], [], []
    for l in range(depth):
        p = dict(norm1_g=norm1_g[l], norm2_g=norm2_g[l], shift_mu=shift_mu[l], rw_w0=rw_w0[l],
                 rw_w_up=rw_w_up[l], rw_a0=rw_a0[l], rw_a_up=rw_a_up[l], rw_g_up=rw_g_up[l],
                 rw_k_k=rw_k_k[l], rw_k_a=rw_k_a[l], rw_r_k=rw_r_k[l], rw_ln_g=rw_ln_g[l],
                 rw_ln_b=rw_ln_b[l], na_q_g=na_q_g[l], na_k_g=na_k_g[l], na_rpb=na_rpb[l])
        late_f32 = [w[l] for w in (w_o_rwkv, w_o_na, w_out, ffn_w1, ffn_w3, ffn_w2)]
        mod_all, w_in_bf = _modulation(c_ctx, c, w_ada[l], b_ada[l], w_in[l])
        u_rw, u_na, gates = _in_proj(tiled(y_p), tiled(y_s), tiles_per_b, mod_all, p["norm1_g"], w_in_bf)
        ctx_view = lambda t: t.reshape(-1, seq, t.shape[-1])
        lat_view = lambda t: t.reshape(-1, dec_seq, t.shape[-1])
        lat_off = n_ctx // dec_seq
        o_rw_p, s_l, casted, (o_na_p, k_l, v_l) = _rwkv_branch(
            ctx_view(u_rw), bsz, 0, None, p, RWKV_PAIRS_CTX, RWKV_UNITS, late_f32,
            attend=(ctx_view(u_na), p["na_q_g"], p["na_k_g"]))
        late = dict(zip(LATE_WEIGHTS, casted))
        new_s.append(s_l)
        heads_last = lambda t: jnp.transpose(t.reshape(bsz, NA_HEADS, HEAD_DIM, seq), (0, 3, 1, 2))
        new_k.append(heads_last(k_l))
        new_v.append(heads_last(v_l))
        ctx_k = jnp.transpose(cache_na_k[:, l], (0, 2, 3, 1)).reshape(dec, NA_WIDTH, -1)
        ctx_v = jnp.transpose(cache_na_v[:, l], (0, 2, 3, 1)).reshape(dec, NA_WIDTH, -1)
        o_rw_s = _rwkv_branch(lat_view(u_rw), dec, lat_off, state_rwkv[:, l], p, RWKV_PAIRS_LAT,
                               RWKV_UNITS)[0]
        o_na_s = _na_latent(lat_view(u_na), dec, lat_off, ctx_k, ctx_v, p["na_q_g"], p["na_k_g"], p["na_rpb"])
        y_p_t, y_s_t = _out_ffn(tiled(y_p), tiled(y_s), tiled(o_rw_p), tiled(o_rw_s), tiled(o_na_p), tiled(o_na_s),
                                gates, tiles_per_b, mod_all, p["norm2_g"], late)
        y_p, y_s = y_p_t.reshape(x_prompt.shape), y_s_t.reshape(x_sample.shape)
    return (y_p, y_s, jnp.stack(new_s, axis=1), jnp.stack(new_k, axis=1), jnp.stack(new_v, axis=1))
```
